```python
import math
import jax
import jax.numpy as jnp
from jax import lax
import numpy as np

D_MODEL = 1024
BATCH = 8
SEQ = 8192
DEPTH = 4

N_EVEN = (DEPTH + 1) // 2
N_ODD = DEPTH // 2
RMS_EPS = 1e-6

GDN_HEADS = 4
GDN_DK = 128
GDN_DV = 128
GDN_QK_W = GDN_HEADS * GDN_DK
GDN_V_W = GDN_HEADS * GDN_DV
GDN_CONV = 4
GDN_CHUNK = 64

POOL_WINDOWS = (2, 4, 8, 16)
POOL_GROUPS = len(POOL_WINDOWS)
POOL_GROUP_W = 128
POOL_W = POOL_GROUPS * POOL_GROUP_W

EVEN_QKV = 2 * GDN_QK_W + GDN_V_W
EVEN_IN = EVEN_QKV + GDN_V_W + 2 * GDN_HEADS + POOL_W
EVEN_OUT = GDN_V_W + POOL_W

DIL_PATTERNS = ((128, 1), (512, 4), (2048, 16))
ATT_GROUPS = len(DIL_PATTERNS)
ATT_HEADS = 8
ATT_DH = 128
ATT_W = ATT_HEADS * ATT_DH
ODD_IN = ATT_GROUPS * 3 * ATT_W
ATT_BLOCK = 128

D_FF = 2816
FFN_CONV = 3

kernel_name = "hybrid_gdn_pool_dilated_alibi_convffn"


def rms_norm(x, gain):
    xf = x.astype(jnp.float32)
    y = xf * lax.rsqrt(jnp.mean(xf * xf, axis=-1, keepdims=True) + RMS_EPS)
    return (y * gain.astype(jnp.float32)).astype(x.dtype)


def l2_norm(x):
    xf = x.astype(jnp.float32)
    return xf * lax.rsqrt(jnp.sum(xf * xf, axis=-1, keepdims=True) + RMS_EPS)


def causal_dwconv(x, w):
    K, C = w.shape
    return lax.conv_general_dilated(
        x, w[:, None, :].astype(x.dtype), window_strides=(1,), padding=[(K - 1, 0)],
        dimension_numbers=("NWC", "WIO", "NWC"), feature_group_count=C)


def gated_delta_rule(q, k, v, g, beta):
    B, T, H, dk = q.shape
    dv = v.shape[-1]
    C = GDN_CHUNK
    N = T // C
    f32 = jnp.float32

    def chunks(t):
        return t.astype(f32).reshape(B, N, C, H, -1).transpose(0, 3, 1, 2, 4)

    q = chunks(q) * (dk ** -0.5)
    k = chunks(k)
    v = chunks(v)
    g = g.astype(f32).reshape(B, N, C, H).transpose(0, 3, 1, 2)
    beta = beta.astype(f32).reshape(B, N, C, H).transpose(0, 3, 1, 2)
    gc = jnp.cumsum(g, axis=-1)

    idx = jnp.arange(C)
    causal = idx[:, None] >= idx[None, :]
    strict = idx[:, None] > idx[None, :]
    diff = gc[..., :, None] - gc[..., None, :]
    decay = jnp.where(causal, jnp.exp(jnp.where(causal, diff, 0.0)), 0.0)

    kb = k * beta[..., None]
    L = jnp.where(strict, jnp.einsum("bhncd,bhnsd->bhncs", kb, k) * decay, 0.0)
    rhs = jnp.concatenate([v * beta[..., None], kb * jnp.exp(gc)[..., None]], axis=-1)
    uw = lax.linalg.triangular_solve(L, rhs, left_side=True, lower=True,
                                     unit_diagonal=True)
    u, w = uw[..., :dv], uw[..., dv:]

    intra = jnp.where(causal, jnp.einsum("bhncd,bhnsd->bhncs", q, k) * decay, 0.0)
    qg = q * jnp.exp(gc)[..., None]
    kdec = k * jnp.exp(gc[..., -1:] - gc)[..., None]
    glast = jnp.exp(gc[..., -1])

    def step(S, xs):
        qg_i, kdec_i, u_i, w_i, intra_i, gl_i = xs
        v_new = u_i - jnp.einsum("bhck,bhkv->bhcv", w_i, S)
        o_i = (jnp.einsum("bhck,bhkv->bhcv", qg_i, S)
               + jnp.einsum("bhcs,bhsv->bhcv", intra_i, v_new))
        S = S * gl_i[..., None, None] + jnp.einsum("bhck,bhcv->bhkv", kdec_i, v_new)
        return S, o_i

    xs = tuple(jnp.moveaxis(t, 2, 0) for t in (qg, kdec, u, w, intra, glast))
    S0 = jnp.zeros((B, H, dk, dv), f32)
    _, o = lax.scan(step, S0, xs)
    return o.transpose(1, 0, 3, 2, 4).reshape(B, T, H, dv)


def multiscale_pool(p, pool_w, pool_scale):
    B, T, _ = p.shape
    pf = p.astype(jnp.float32).reshape(B, T, POOL_GROUPS, POOL_GROUP_W)
    csum = jnp.cumsum(pf, axis=1)
    t1 = jnp.arange(1, T + 1, dtype=jnp.float32)
    pooled = []
    for gi, win in enumerate(POOL_WINDOWS):
        cg = csum[:, :, gi]
        lag = jnp.pad(cg, ((0, 0), (win, 0), (0, 0)))[:, :T]
        cnt = jnp.minimum(t1, float(win))[None, :, None]
        pooled.append((cg - lag) / cnt)
    pooled = jnp.stack(pooled, axis=2) - pf
    y = jnp.einsum("btgc,gcd->btgd", pooled, pool_w.astype(jnp.float32))
    return (y.reshape(B, T, POOL_W) * pool_scale.astype(jnp.float32)).astype(p.dtype)


def even_mixer(h, w_in, w_out, conv_w, a_log, dt_bias, gdn_norm, pool_w, pool_scale):
    B, T, _ = h.shape
    proj = h @ w_in
    i1 = EVEN_QKV
    i2 = i1 + GDN_V_W
    i3 = i2 + GDN_HEADS
    i4 = i3 + GDN_HEADS
    qkv, z, b_raw, a_raw, pool_in = jnp.split(proj, [i1, i2, i3, i4], axis=-1)

    qkv = jax.nn.silu(causal_dwconv(qkv, conv_w))
    q, k, v = jnp.split(qkv, [GDN_QK_W, 2 * GDN_QK_W], axis=-1)
    q = l2_norm(q.reshape(B, T, GDN_HEADS, GDN_DK))
    k = l2_norm(k.reshape(B, T, GDN_HEADS, GDN_DK))
    v = v.reshape(B, T, GDN_HEADS, GDN_DV)
    beta = jax.nn.sigmoid(b_raw.astype(jnp.float32))
    g = -jnp.exp(a_log.astype(jnp.float32)) * jax.nn.softplus(
        a_raw.astype(jnp.float32) + dt_bias.astype(jnp.float32))
    o = gated_delta_rule(q, k, v, g, beta)
    o = rms_norm(o, gdn_norm) * jax.nn.silu(
        z.astype(jnp.float32).reshape(B, T, GDN_HEADS, GDN_DV))
    o_a = o.reshape(B, T, GDN_V_W).astype(h.dtype)

    o_b = multiscale_pool(pool_in, pool_w, pool_scale)
    return jnp.concatenate([o_a, o_b], axis=-1) @ w_out


def alibi_slopes(n_heads):
    return jnp.exp2(-8.0 * jnp.arange(1, n_heads + 1, dtype=jnp.float32) / n_heads)


def dilated_band_attention(q, k, v, dil, n_back, slopes):
    B, T, H, dh = q.shape
    L = T // dil
    nb = -(-L // ATT_BLOCK)
    Lp = nb * ATT_BLOCK

    def to_sub(t):
        t = t.reshape(B, L, dil, H, dh).transpose(0, 2, 3, 1, 4)
        t = jnp.pad(t, ((0, 0), (0, 0), (0, 0), (0, Lp - L), (0, 0)))
        return t.reshape(B, dil, H, nb, ATT_BLOCK, dh)

    def band(t):
        prev = jnp.pad(t, ((0, 0), (0, 0), (0, 0), (1, 0), (0, 0), (0, 0)))[:, :, :, :nb]
        return jnp.concatenate([prev, t], axis=4)

    qb = to_sub(q)
    kb = band(to_sub(k))
    vb = band(to_sub(v))

    a = jnp.arange(ATT_BLOCK)[:, None]
    j = jnp.arange(2 * ATT_BLOCK)[None, :]
    rel = ATT_BLOCK + a - j
    blk = jnp.arange(nb)[:, None, None]
    mask = (rel >= 0) & (rel <= n_back) & ((j >= ATT_BLOCK) | (blk > 0))
    bias = -(slopes * dil)[:, None, None, None] * rel.astype(jnp.float32)

    s = jnp.einsum("brhnqd,brhnkd->brhnqk", qb, kb) + bias
    s = jnp.where(mask, s, -jnp.inf)
    m = jnp.max(s, axis=-1, keepdims=True)
    p = jnp.exp(s - m)
    l = jnp.sum(p, axis=-1, keepdims=True)
    o = jnp.einsum("brhnqk,brhnkd->brhnqd", p, vb) / l
    lse = m + jnp.log(l)

    def from_sub(t):
        t = t.reshape(B, dil, H, Lp, -1)[:, :, :, :L]
        return t.transpose(0, 3, 1, 2, 4).reshape(B, T, H, -1)

    return from_sub(o), from_sub(lse)[..., 0]


def odd_mixer(h, w_in, w_out, q_norm, k_norm):
    B, T, _ = h.shape
    slopes = alibi_slopes(ATT_HEADS)
    outs, lses = [], []
    for gi, (window, dil) in enumerate(DIL_PATTERNS):
        cols = w_in[:, gi * 3 * ATT_W:(gi + 1) * 3 * ATT_W]
        proj = (h @ cols).astype(jnp.float32).reshape(B, T, 3, ATT_HEADS, ATT_DH)
        q = rms_norm(proj[:, :, 0], q_norm) * (ATT_DH ** -0.5)
        k = rms_norm(proj[:, :, 1], k_norm)
        v = proj[:, :, 2]
        o, lse = dilated_band_attention(q, k, v, dil, window // dil, slopes)
        outs.append(o)
        lses.append(lse)
    wts = jax.nn.softmax(jnp.stack(lses), axis=0)
    o = jnp.sum(wts[..., None] * jnp.stack(outs), axis=0)
    return o.reshape(B, T, ATT_W).astype(h.dtype) @ w_out


def conv_ffn(h, w_up, conv_w, conv_b, w_down):
    up = h @ w_up
    gate, val = jnp.split(up, 2, axis=-1)
    gate = causal_dwconv(gate, conv_w) + conv_b
    return (jax.nn.silu(gate) * val) @ w_down


def _fwd_setup_inputs(seed: int = 0) -> dict:
    key = jax.random.key(seed)
    ks = jax.random.split(key, 24)
    f32 = jnp.float32
    D = D_MODEL

    def nrm(k, shape, s):
        return jax.random.normal(k, shape, f32) * s

    dt = jnp.exp(jax.random.uniform(ks[10], (N_EVEN, GDN_HEADS), f32,
                                    math.log(1e-3), math.log(1e-1)))
    return {
        "x": nrm(ks[0], (BATCH, SEQ, D), 1.0),
        "c": nrm(ks[1], (BATCH, D), 1.0),
        "ada_w": nrm(ks[2], (DEPTH, D, 6 * D), 0.5 * D ** -0.5),
        "ada_b": nrm(ks[3], (DEPTH, 6 * D), 0.01),
        "norm_mix": 1.0 + nrm(ks[4], (DEPTH, D), 0.02),
        "norm_ffn": 1.0 + nrm(ks[5], (DEPTH, D), 0.02),
        "ev_w_in": nrm(ks[6], (N_EVEN, D, EVEN_IN), D ** -0.5),
        "ev_w_out": nrm(ks[7], (N_EVEN, EVEN_OUT, D), EVEN_OUT ** -0.5),
        "gdn_conv_w": nrm(ks[8], (N_EVEN, GDN_CONV, EVEN_QKV), GDN_CONV ** -0.5),
        "gdn_a_log": jnp.log(jax.random.uniform(ks[9], (N_EVEN, GDN_HEADS), f32, 1.0, 16.0)),
        "gdn_dt_bias": dt + jnp.log(-jnp.expm1(-dt)),
        "gdn_norm": 1.0 + nrm(ks[11], (N_EVEN, GDN_DV), 0.02),
        "pool_w": nrm(ks[12], (N_EVEN, POOL_GROUPS, POOL_GROUP_W, POOL_GROUP_W), POOL_GROUP_W ** -0.5),
        "pool_scale": 1.0 + nrm(ks[13], (N_EVEN, POOL_W), 0.1),
        "od_w_in": nrm(ks[14], (N_ODD, D, ODD_IN), D ** -0.5),
        "od_w_out": nrm(ks[15], (N_ODD, ATT_W, D), ATT_W ** -0.5),
        "att_q_norm": 1.0 + nrm(ks[16], (N_ODD, ATT_DH), 0.02),
        "att_k_norm": 1.0 + nrm(ks[17], (N_ODD, ATT_DH), 0.02),
        "ffn_w_up": nrm(ks[18], (DEPTH, D, 2 * D_FF), D ** -0.5),
        "ffn_conv_w": nrm(ks[19], (DEPTH, FFN_CONV, D_FF), FFN_CONV ** -0.5),
        "ffn_conv_b": nrm(ks[20], (DEPTH, D_FF), 0.01),
        "ffn_w_down": nrm(ks[21], (DEPTH, D_FF, D), D_FF ** -0.5),
    }


def _fwd_reference(x, c, ada_w, ada_b, norm_mix, norm_ffn, ev_w_in, ev_w_out, gdn_conv_w,
              gdn_a_log, gdn_dt_bias, gdn_norm, pool_w, pool_scale, od_w_in, od_w_out,
              att_q_norm, att_k_norm, ffn_w_up, ffn_conv_w, ffn_conv_b, ffn_w_down):
    cs = jax.nn.silu(c)
    for i in range(DEPTH):
        mod = (cs @ ada_w[i] + ada_b[i])[:, None, :]
        sh_m, sc_m, g_m, sh_f, sc_f, g_f = jnp.split(mod, 6, axis=-1)

        hm = rms_norm(x, norm_mix[i]) * (1.0 + sc_m) + sh_m
        if i % 2 == 0:
            e = i // 2
            y = even_mixer(hm, ev_w_in[e], ev_w_out[e], gdn_conv_w[e], gdn_a_log[e],
                           gdn_dt_bias[e], gdn_norm[e], pool_w[e], pool_scale[e])
        else:
            o = i // 2
            y = odd_mixer(hm, od_w_in[o], od_w_out[o], att_q_norm[o], att_k_norm[o])
        x = x + g_m * y

        hf = rms_norm(x, norm_ffn[i]) * (1.0 + sc_f) + sh_f
        x = x + g_f * conv_ffn(hf, ffn_w_up[i], ffn_conv_w[i], ffn_conv_b[i], ffn_w_down[i])
    return x


import jax as _jax
import jax.numpy as _jnp

TWIN_FORMAT = 'train_step'
FWD_PARAMS = ['x', 'c', 'ada_w', 'ada_b', 'norm_mix', 'norm_ffn', 'ev_w_in', 'ev_w_out', 'gdn_conv_w', 'gdn_a_log', 'gdn_dt_bias', 'gdn_norm', 'pool_w', 'pool_scale', 'od_w_in', 'od_w_out', 'att_q_norm', 'att_k_norm', 'ffn_w_up', 'ffn_conv_w', 'ffn_conv_b', 'ffn_w_down']
TWIN_WEIGHTS = ['ada_w', 'ada_b', 'norm_mix', 'norm_ffn', 'ev_w_in', 'ev_w_out', 'gdn_conv_w', 'gdn_a_log', 'gdn_dt_bias', 'gdn_norm', 'pool_w', 'pool_scale', 'od_w_in', 'od_w_out', 'att_q_norm', 'att_k_norm', 'ffn_w_up', 'ffn_conv_w', 'ffn_conv_b', 'ffn_w_down']
TWIN_DIFF_INPUT = 'x'
TWIN_INPUTS = ['x', 'c', 'ada_w', 'ada_b', 'norm_mix', 'norm_ffn', 'ev_w_in', 'ev_w_out', 'gdn_conv_w', 'gdn_a_log', 'gdn_dt_bias', 'gdn_norm', 'pool_w', 'pool_scale', 'od_w_in', 'od_w_out', 'att_q_norm', 'att_k_norm', 'ffn_w_up', 'ffn_conv_w', 'ffn_conv_b', 'ffn_w_down', 'loss_target', 'm_ada_w', 'm_ada_b', 'm_norm_mix', 'm_norm_ffn', 'm_ev_w_in', 'm_ev_w_out', 'm_gdn_conv_w', 'm_gdn_a_log', 'm_gdn_dt_bias', 'm_gdn_norm', 'm_pool_w', 'm_pool_scale', 'm_od_w_in', 'm_od_w_out', 'm_att_q_norm', 'm_att_k_norm', 'm_ffn_w_up', 'm_ffn_conv_w', 'm_ffn_conv_b', 'm_ffn_w_down', 'v_ada_w', 'v_ada_b', 'v_norm_mix', 'v_norm_ffn', 'v_ev_w_in', 'v_ev_w_out', 'v_gdn_conv_w', 'v_gdn_a_log', 'v_gdn_dt_bias', 'v_gdn_norm', 'v_pool_w', 'v_pool_scale', 'v_od_w_in', 'v_od_w_out', 'v_att_q_norm', 'v_att_k_norm', 'v_ffn_w_up', 'v_ffn_conv_w', 'v_ffn_conv_b', 'v_ffn_w_down']
TWIN_OUTPUTS = ['loss', 'grad_x', 'grad_ada_w', 'grad_ada_b', 'grad_norm_mix', 'grad_norm_ffn', 'grad_ev_w_in', 'grad_ev_w_out', 'grad_gdn_conv_w', 'grad_gdn_a_log', 'grad_gdn_dt_bias', 'grad_gdn_norm', 'grad_pool_w', 'grad_pool_scale', 'grad_od_w_in', 'grad_od_w_out', 'grad_att_q_norm', 'grad_att_k_norm', 'grad_ffn_w_up', 'grad_ffn_conv_w', 'grad_ffn_conv_b', 'grad_ffn_w_down', 'delta_ada_w', 'delta_ada_b', 'delta_norm_mix', 'delta_norm_ffn', 'delta_ev_w_in', 'delta_ev_w_out', 'delta_gdn_conv_w', 'delta_gdn_a_log', 'delta_gdn_dt_bias', 'delta_gdn_norm', 'delta_pool_w', 'delta_pool_scale', 'delta_od_w_in', 'delta_od_w_out', 'delta_att_q_norm', 'delta_att_k_norm', 'delta_ffn_w_up', 'delta_ffn_conv_w', 'delta_ffn_conv_b', 'delta_ffn_w_down', 'new_m_ada_w', 'new_m_ada_b', 'new_m_norm_mix', 'new_m_norm_ffn', 'new_m_ev_w_in', 'new_m_ev_w_out', 'new_m_gdn_conv_w', 'new_m_gdn_a_log', 'new_m_gdn_dt_bias', 'new_m_gdn_norm', 'new_m_pool_w', 'new_m_pool_scale', 'new_m_od_w_in', 'new_m_od_w_out', 'new_m_att_q_norm', 'new_m_att_k_norm', 'new_m_ffn_w_up', 'new_m_ffn_conv_w', 'new_m_ffn_conv_b', 'new_m_ffn_w_down', 'new_v_ada_w', 'new_v_ada_b', 'new_v_norm_mix', 'new_v_norm_ffn', 'new_v_ev_w_in', 'new_v_ev_w_out', 'new_v_gdn_conv_w', 'new_v_gdn_a_log', 'new_v_gdn_dt_bias', 'new_v_gdn_norm', 'new_v_pool_w', 'new_v_pool_scale', 'new_v_od_w_in', 'new_v_od_w_out', 'new_v_att_q_norm', 'new_v_att_k_norm', 'new_v_ffn_w_up', 'new_v_ffn_conv_w', 'new_v_ffn_conv_b', 'new_v_ffn_w_down']
TWIN_LEAF_KINDS = {'loss': 'loss', 'grad_x': 'grad_x', 'grad_ada_w': 'grad_w', 'grad_ada_b': 'grad_w', 'grad_norm_mix': 'grad_w', 'grad_norm_ffn': 'grad_w', 'grad_ev_w_in': 'grad_w', 'grad_ev_w_out': 'grad_w', 'grad_gdn_conv_w': 'grad_w', 'grad_gdn_a_log': 'grad_w', 'grad_gdn_dt_bias': 'grad_w', 'grad_gdn_norm': 'grad_w', 'grad_pool_w': 'grad_w', 'grad_pool_scale': 'grad_w', 'grad_od_w_in': 'grad_w', 'grad_od_w_out': 'grad_w', 'grad_att_q_norm': 'grad_w', 'grad_att_k_norm': 'grad_w', 'grad_ffn_w_up': 'grad_w', 'grad_ffn_conv_w': 'grad_w', 'grad_ffn_conv_b': 'grad_w', 'grad_ffn_w_down': 'grad_w', 'delta_ada_w': 'delta_w', 'delta_ada_b': 'delta_w', 'delta_norm_mix': 'delta_w', 'delta_norm_ffn': 'delta_w', 'delta_ev_w_in': 'delta_w', 'delta_ev_w_out': 'delta_w', 'delta_gdn_conv_w': 'delta_w', 'delta_gdn_a_log': 'delta_w', 'delta_gdn_dt_bias': 'delta_w', 'delta_gdn_norm': 'delta_w', 'delta_pool_w': 'delta_w', 'delta_pool_scale': 'delta_w', 'delta_od_w_in': 'delta_w', 'delta_od_w_out': 'delta_w', 'delta_att_q_norm': 'delta_w', 'delta_att_k_norm': 'delta_w', 'delta_ffn_w_up': 'delta_w', 'delta_ffn_conv_w': 'delta_w', 'delta_ffn_conv_b': 'delta_w', 'delta_ffn_w_down': 'delta_w', 'new_m_ada_w': 'new_m', 'new_m_ada_b': 'new_m', 'new_m_norm_mix': 'new_m', 'new_m_norm_ffn': 'new_m', 'new_m_ev_w_in': 'new_m', 'new_m_ev_w_out': 'new_m', 'new_m_gdn_conv_w': 'new_m', 'new_m_gdn_a_log': 'new_m', 'new_m_gdn_dt_bias': 'new_m', 'new_m_gdn_norm': 'new_m', 'new_m_pool_w': 'new_m', 'new_m_pool_scale': 'new_m', 'new_m_od_w_in': 'new_m', 'new_m_od_w_out': 'new_m', 'new_m_att_q_norm': 'new_m', 'new_m_att_k_norm': 'new_m', 'new_m_ffn_w_up': 'new_m', 'new_m_ffn_conv_w': 'new_m', 'new_m_ffn_conv_b': 'new_m', 'new_m_ffn_w_down': 'new_m', 'new_v_ada_w': 'new_v', 'new_v_ada_b': 'new_v', 'new_v_norm_mix': 'new_v', 'new_v_norm_ffn': 'new_v', 'new_v_ev_w_in': 'new_v', 'new_v_ev_w_out': 'new_v', 'new_v_gdn_conv_w': 'new_v', 'new_v_gdn_a_log': 'new_v', 'new_v_gdn_dt_bias': 'new_v', 'new_v_gdn_norm': 'new_v', 'new_v_pool_w': 'new_v', 'new_v_pool_scale': 'new_v', 'new_v_od_w_in': 'new_v', 'new_v_od_w_out': 'new_v', 'new_v_att_q_norm': 'new_v', 'new_v_att_k_norm': 'new_v', 'new_v_ffn_w_up': 'new_v', 'new_v_ffn_conv_w': 'new_v', 'new_v_ffn_conv_b': 'new_v', 'new_v_ffn_w_down': 'new_v'}


def _forward(args):
    return _fwd_reference(*[args[k] for k in FWD_PARAMS])


def _output_shape():
    def fwd():
        inp = _fwd_setup_inputs(0)
        return _fwd_reference(*[inp[k] for k in FWD_PARAMS])
    out = _jax.eval_shape(fwd)
    return out.shape, out.dtype

N_MICROBATCH = 1
ADAM_LR = 0.001
ADAM_B1 = 0.9
ADAM_B2 = 0.999
ADAM_EPS = 1e-08
ADAM_WD = 0.01
ADAM_STEP = 10
PER_EXAMPLE_BATCH_AXIS = {'x': 0, 'c': 0, 'loss_target': 0}
SHARED_INPUTS = []
_WEIGHT_DTYPES = {'ada_w': _jnp.float32, 'ada_b': _jnp.float32, 'norm_mix': _jnp.float32, 'norm_ffn': _jnp.float32, 'ev_w_in': _jnp.float32, 'ev_w_out': _jnp.float32, 'gdn_conv_w': _jnp.float32, 'gdn_a_log': _jnp.float32, 'gdn_dt_bias': _jnp.float32, 'gdn_norm': _jnp.float32, 'pool_w': _jnp.float32, 'pool_scale': _jnp.float32, 'od_w_in': _jnp.float32, 'od_w_out': _jnp.float32, 'att_q_norm': _jnp.float32, 'att_k_norm': _jnp.float32, 'ffn_w_up': _jnp.float32, 'ffn_conv_w': _jnp.float32, 'ffn_conv_b': _jnp.float32, 'ffn_w_down': _jnp.float32}
MOMENT_SCALE = {'ada_w': 1.452039e+00, 'ada_b': 3.831554e+00, 'norm_mix': 2.795918e+00, 'norm_ffn': 6.455718e+00, 'ev_w_in': 2.097647e-01, 'ev_w_out': 2.424394e-01, 'gdn_conv_w': 1.916056e-01, 'gdn_a_log': 6.159199e+00, 'gdn_dt_bias': 5.805972e+00, 'gdn_norm': 1.050471e+01, 'pool_w': 4.409845e-01, 'pool_scale': 5.058022e+00, 'od_w_in': 7.034152e-02, 'od_w_out': 2.090237e-01, 'att_q_norm': 1.578193e+00, 'att_k_norm': 1.575932e+00, 'ffn_w_up': 1.290063e-01, 'ffn_conv_w': 7.664351e-01, 'ffn_conv_b': 8.533708e-01, 'ffn_w_down': 1.413596e-01}


def _to_microbatches(a, axis):
    t = _jnp.moveaxis(a, axis, 0)
    t = t.reshape((N_MICROBATCH, t.shape[0] // N_MICROBATCH) + t.shape[1:])
    return _jnp.moveaxis(t, 1, axis + 1)


def setup_inputs(seed: int = 0) -> dict:
    inp = _fwd_setup_inputs(seed)
    key = _jax.random.fold_in(_jax.random.key(seed), 7919)
    shape, _ = _output_shape()
    out = dict(inp)
    out["loss_target"] = _jax.random.normal(_jax.random.fold_in(key, 0), shape, _jnp.float32)
    for i, name in enumerate(TWIN_WEIGHTS):
        w = inp[name].astype(_jnp.float32)
        if MOMENT_SCALE is None:
            s = _jnp.sqrt(_jnp.mean(_jnp.square(w)) + 1e-30)
        else:
            s = MOMENT_SCALE[name]
        km, kv = _jax.random.split(_jax.random.fold_in(key, i + 1))
        out[name] = w
        out["m_" + name] = s * _jax.random.normal(km, w.shape, _jnp.float32)
        out["v_" + name] = (s * s) * _jax.random.uniform(kv, w.shape, _jnp.float32, 0.5, 1.5)
    if N_MICROBATCH > 1:
        for name, axis in PER_EXAMPLE_BATCH_AXIS.items():
            out[name] = _to_microbatches(out[name], axis)
    return {'x': out['x'], 'c': out['c'], 'ada_w': out['ada_w'], 'ada_b': out['ada_b'], 'norm_mix': out['norm_mix'], 'norm_ffn': out['norm_ffn'], 'ev_w_in': out['ev_w_in'], 'ev_w_out': out['ev_w_out'], 'gdn_conv_w': out['gdn_conv_w'], 'gdn_a_log': out['gdn_a_log'], 'gdn_dt_bias': out['gdn_dt_bias'], 'gdn_norm': out['gdn_norm'], 'pool_w': out['pool_w'], 'pool_scale': out['pool_scale'], 'od_w_in': out['od_w_in'], 'od_w_out': out['od_w_out'], 'att_q_norm': out['att_q_norm'], 'att_k_norm': out['att_k_norm'], 'ffn_w_up': out['ffn_w_up'], 'ffn_conv_w': out['ffn_conv_w'], 'ffn_conv_b': out['ffn_conv_b'], 'ffn_w_down': out['ffn_w_down'], 'loss_target': out['loss_target'], 'm_ada_w': out['m_ada_w'], 'm_ada_b': out['m_ada_b'], 'm_norm_mix': out['m_norm_mix'], 'm_norm_ffn': out['m_norm_ffn'], 'm_ev_w_in': out['m_ev_w_in'], 'm_ev_w_out': out['m_ev_w_out'], 'm_gdn_conv_w': out['m_gdn_conv_w'], 'm_gdn_a_log': out['m_gdn_a_log'], 'm_gdn_dt_bias': out['m_gdn_dt_bias'], 'm_gdn_norm': out['m_gdn_norm'], 'm_pool_w': out['m_pool_w'], 'm_pool_scale': out['m_pool_scale'], 'm_od_w_in': out['m_od_w_in'], 'm_od_w_out': out['m_od_w_out'], 'm_att_q_norm': out['m_att_q_norm'], 'm_att_k_norm': out['m_att_k_norm'], 'm_ffn_w_up': out['m_ffn_w_up'], 'm_ffn_conv_w': out['m_ffn_conv_w'], 'm_ffn_conv_b': out['m_ffn_conv_b'], 'm_ffn_w_down': out['m_ffn_w_down'], 'v_ada_w': out['v_ada_w'], 'v_ada_b': out['v_ada_b'], 'v_norm_mix': out['v_norm_mix'], 'v_norm_ffn': out['v_norm_ffn'], 'v_ev_w_in': out['v_ev_w_in'], 'v_ev_w_out': out['v_ev_w_out'], 'v_gdn_conv_w': out['v_gdn_conv_w'], 'v_gdn_a_log': out['v_gdn_a_log'], 'v_gdn_dt_bias': out['v_gdn_dt_bias'], 'v_gdn_norm': out['v_gdn_norm'], 'v_pool_w': out['v_pool_w'], 'v_pool_scale': out['v_pool_scale'], 'v_od_w_in': out['v_od_w_in'], 'v_od_w_out': out['v_od_w_out'], 'v_att_q_norm': out['v_att_q_norm'], 'v_att_k_norm': out['v_att_k_norm'], 'v_ffn_w_up': out['v_ffn_w_up'], 'v_ffn_conv_w': out['v_ffn_conv_w'], 'v_ffn_conv_b': out['v_ffn_conv_b'], 'v_ffn_w_down': out['v_ffn_w_down']}


def _loss(weights, diff, rest, loss_target):
    with _jax.named_scope("forward"):
        args = {**rest, TWIN_DIFF_INPUT: diff, **{k: w.astype(_WEIGHT_DTYPES[k]) for k, w in weights.items()}}
        y = _forward(args)
    with _jax.named_scope("loss_head"):
        err = _jnp.square(y.astype(_jnp.float32) - loss_target)
        return 0.5 * _jnp.sum(_jnp.mean(err, axis=-1)) if err.ndim else 0.5 * err


def _adamw(w, g, m, v):
    m = ADAM_B1 * m + (1.0 - ADAM_B1) * g
    v = ADAM_B2 * v + (1.0 - ADAM_B2) * _jnp.square(g)
    m_hat = m / (1.0 - ADAM_B1 ** ADAM_STEP)
    v_hat = v / (1.0 - ADAM_B2 ** ADAM_STEP)
    delta = -ADAM_LR * (m_hat / (_jnp.sqrt(v_hat) + ADAM_EPS) + ADAM_WD * w)
    return delta, m, v


def reference(x, c, ada_w, ada_b, norm_mix, norm_ffn, ev_w_in, ev_w_out, gdn_conv_w, gdn_a_log, gdn_dt_bias, gdn_norm, pool_w, pool_scale, od_w_in, od_w_out, att_q_norm, att_k_norm, ffn_w_up, ffn_conv_w, ffn_conv_b, ffn_w_down, loss_target, m_ada_w, m_ada_b, m_norm_mix, m_norm_ffn, m_ev_w_in, m_ev_w_out, m_gdn_conv_w, m_gdn_a_log, m_gdn_dt_bias, m_gdn_norm, m_pool_w, m_pool_scale, m_od_w_in, m_od_w_out, m_att_q_norm, m_att_k_norm, m_ffn_w_up, m_ffn_conv_w, m_ffn_conv_b, m_ffn_w_down, v_ada_w, v_ada_b, v_norm_mix, v_norm_ffn, v_ev_w_in, v_ev_w_out, v_gdn_conv_w, v_gdn_a_log, v_gdn_dt_bias, v_gdn_norm, v_pool_w, v_pool_scale, v_od_w_in, v_od_w_out, v_att_q_norm, v_att_k_norm, v_ffn_w_up, v_ffn_conv_w, v_ffn_conv_b, v_ffn_w_down):
    given = dict(x=x, c=c, ada_w=ada_w, ada_b=ada_b, norm_mix=norm_mix, norm_ffn=norm_ffn, ev_w_in=ev_w_in, ev_w_out=ev_w_out, gdn_conv_w=gdn_conv_w, gdn_a_log=gdn_a_log, gdn_dt_bias=gdn_dt_bias, gdn_norm=gdn_norm, pool_w=pool_w, pool_scale=pool_scale, od_w_in=od_w_in, od_w_out=od_w_out, att_q_norm=att_q_norm, att_k_norm=att_k_norm, ffn_w_up=ffn_w_up, ffn_conv_w=ffn_conv_w, ffn_conv_b=ffn_conv_b, ffn_w_down=ffn_w_down, loss_target=loss_target, m_ada_w=m_ada_w, m_ada_b=m_ada_b, m_norm_mix=m_norm_mix, m_norm_ffn=m_norm_ffn, m_ev_w_in=m_ev_w_in, m_ev_w_out=m_ev_w_out, m_gdn_conv_w=m_gdn_conv_w, m_gdn_a_log=m_gdn_a_log, m_gdn_dt_bias=m_gdn_dt_bias, m_gdn_norm=m_gdn_norm, m_pool_w=m_pool_w, m_pool_scale=m_pool_scale, m_od_w_in=m_od_w_in, m_od_w_out=m_od_w_out, m_att_q_norm=m_att_q_norm, m_att_k_norm=m_att_k_norm, m_ffn_w_up=m_ffn_w_up, m_ffn_conv_w=m_ffn_conv_w, m_ffn_conv_b=m_ffn_conv_b, m_ffn_w_down=m_ffn_w_down, v_ada_w=v_ada_w, v_ada_b=v_ada_b, v_norm_mix=v_norm_mix, v_norm_ffn=v_norm_ffn, v_ev_w_in=v_ev_w_in, v_ev_w_out=v_ev_w_out, v_gdn_conv_w=v_gdn_conv_w, v_gdn_a_log=v_gdn_a_log, v_gdn_dt_bias=v_gdn_dt_bias, v_gdn_norm=v_gdn_norm, v_pool_w=v_pool_w, v_pool_scale=v_pool_scale, v_od_w_in=v_od_w_in, v_od_w_out=v_od_w_out, v_att_q_norm=v_att_q_norm, v_att_k_norm=v_att_k_norm, v_ffn_w_up=v_ffn_w_up, v_ffn_conv_w=v_ffn_conv_w, v_ffn_conv_b=v_ffn_conv_b, v_ffn_w_down=v_ffn_w_down)
    weights = {n: given[n] for n in TWIN_WEIGHTS}
    shared = {n: given[n] for n in SHARED_INPUTS}
    per_example = {n: given[n] for n in ['x', 'c']}
    grad_fn = _jax.value_and_grad(_loss, argnums=(0, 1))

    def one_microbatch(ex, loss_target):
        ex = dict(ex)
        diff = ex.pop(TWIN_DIFF_INPUT)
        return grad_fn(weights, diff, {**shared, **ex}, loss_target)

    if N_MICROBATCH == 1:
        loss, (grad_w, grad_x) = one_microbatch(per_example, given["loss_target"])
    else:
        def body(carry, xs):
            loss_sum, grad_sum = carry
            l_k, (gw_k, gx_k) = one_microbatch(xs[0], xs[1])
            with _jax.named_scope("update"):
                return (loss_sum + l_k, _jax.tree.map(_jnp.add, grad_sum, gw_k)), gx_k

        init = (_jnp.zeros((), _jnp.float32), _jax.tree.map(_jnp.zeros_like, weights))
        (loss, grad_w), grad_x = _jax.lax.scan(body, init, (per_example, given["loss_target"]))
    with _jax.named_scope("update"):
        delta_w, new_m, new_v = {}, {}, {}
        for n in TWIN_WEIGHTS:
            delta_w[n], new_m[n], new_v[n] = _adamw(weights[n], grad_w[n], given["m_" + n], given["v_" + n])
    return (loss, grad_x, *[grad_w[n] for n in TWIN_WEIGHTS], *[delta_w[n] for n in TWIN_WEIGHTS],
            *[new_m[n] for n in TWIN_WEIGHTS], *[new_v[n] for n in TWIN_WEIGHTS])
```

```python
import functools
import math

import jax
import jax.numpy as jnp
from jax import lax
from jax.experimental import pallas as pl
from jax.experimental.pallas import tpu as pltpu

F32 = jnp.float32
BF16 = jnp.bfloat16
LANE = 128
SUBLANE = 8
VMEM_LIMIT = 56 * 1024 * 1024
MESH = pl.DeviceIdType.MESH

RMS_EPS = 1e-6
GDN_H = 4
HD = 128
GDN_CHUNK = 64
GDN_CONV = 4
FFN_CONV = 3
POOL_G = 4
ATT_H = 8
ATT_BLK = 128
DIL = (1, 4, 16)
EVEN_COLS = 2568
EVEN_PAD = 2688
ADAM_LR, ADAM_B1, ADAM_B2, ADAM_EPS, ADAM_WD, ADAM_STEP = 0.001, 0.9, 0.999, 1e-08, 0.01, 10
NEG = -1e30

NN = (((1,), (0,)), ((), ()))
NT = (((1,), (1,)), ((), ()))
TN = (((0,), (0,)), ((), ()))


def _params(n_grid):
    return pltpu.CompilerParams(dimension_semantics=("arbitrary",) * n_grid,
                                vmem_limit_bytes=VMEM_LIMIT)


def _tile(n, target):
    if n <= target:
        return n
    best = None
    for t in range(LANE, target + 1, LANE):
        if n % t == 0:
            best = t
    assert best is not None, (n, target)
    return best


def _rows(n, target):
    if n <= target:
        return n
    best = None
    for t in range(16, target + 1, 16):
        if n % t == 0:
            best = t
    assert best is not None, (n, target)
    return best


def _bdot(a, b, dims):
    return lax.dot_general(a.astype(BF16), b.astype(BF16), dims, preferred_element_type=F32)


def _hdot(a, b, dims):
    return lax.dot_general(a, b, dims, precision=lax.Precision.HIGHEST, preferred_element_type=F32)


def _sigmoid(x):
    return 1.0 / (1.0 + jnp.exp(-x))


def _silu(x):
    return x * _sigmoid(x)


def _softplus(x):
    return jnp.maximum(x, 0.0) + jnp.log(1.0 + jnp.exp(-jnp.abs(x)))


def matmul(a, b, mode, out_dtype, name, tm=512, tn=1536, tk=1536):
    if mode == "nn":
        (M, K), (K2, N) = a.shape, b.shape
    elif mode == "nt":
        (M, K), (N, K2) = a.shape, b.shape
    else:
        (K, M), (K2, N) = a.shape, b.shape
    assert K == K2, (a.shape, b.shape, mode)
    tm, tn, tk = _tile(M, tm), _tile(N, tn), _tile(K, tk)
    nk = K // tk
    dims = {"nn": NN, "nt": NT, "tn": TN}[mode]
    if mode == "tn":
        a_spec = pl.BlockSpec((tk, tm), lambda i, j, k: (k, i))
    else:
        a_spec = pl.BlockSpec((tm, tk), lambda i, j, k: (i, k))
    if mode == "nt":
        b_spec = pl.BlockSpec((tn, tk), lambda i, j, k: (j, k))
    else:
        b_spec = pl.BlockSpec((tk, tn), lambda i, j, k: (k, j))

    def body(a_ref, b_ref, o_ref, acc_ref):
        k = pl.program_id(2)
        p = _bdot(a_ref[...], b_ref[...], dims)

        @pl.when(k == 0)
        def _():
            acc_ref[...] = p

        @pl.when(k > 0)
        def _():
            acc_ref[...] += p

        @pl.when(k == nk - 1)
        def _():
            o_ref[...] = acc_ref[...].astype(out_dtype)

    return pl.pallas_call(
        body, name=name, grid=(M // tm, N // tn, nk),
        in_specs=[a_spec, b_spec],
        out_specs=pl.BlockSpec((tm, tn), lambda i, j, k: (i, j)),
        out_shape=jax.ShapeDtypeStruct((M, N), out_dtype),
        scratch_shapes=[pltpu.VMEM((tm, tn), F32)],
        compiler_params=_params(3))(a, b)


def _row_spec(d):
    return pl.BlockSpec((1, d), lambda i: (0, 0))


def modnorm_fwd(x, gain, sc, sh, name):
    T, D = x.shape
    tb = _rows(T, 512)

    def body(x_ref, g_ref, sc_ref, sh_ref, o_ref):
        xv = x_ref[...]
        r = lax.rsqrt(jnp.mean(xv * xv, axis=-1, keepdims=True) + RMS_EPS)
        o_ref[...] = ((xv * r) * g_ref[...] * (1.0 + sc_ref[...]) + sh_ref[...]).astype(BF16)

    blk = pl.BlockSpec((tb, D), lambda i: (i, 0))
    return pl.pallas_call(
        body, name=name, grid=(T // tb,),
        in_specs=[blk, _row_spec(D), _row_spec(D), _row_spec(D)],
        out_specs=blk, out_shape=jax.ShapeDtypeStruct((T, D), BF16),
        compiler_params=_params(1))(x, gain, sc, sh)


def modnorm_bwd(x, gain, sc, dh, dres, name):
    T, D = x.shape
    tb = _rows(T, 512)

    def body(x_ref, g_ref, sc_ref, dh_ref, dres_ref, dx_ref, dg_ref, dsc_ref, dsh_ref):
        i = pl.program_id(0)
        xv = x_ref[...]
        r = lax.rsqrt(jnp.mean(xv * xv, axis=-1, keepdims=True) + RMS_EPS)
        n = xv * r
        dhv = dh_ref[...].astype(F32)
        gain_v, sc1 = g_ref[...], 1.0 + sc_ref[...]
        dn = dhv * (gain_v * sc1)
        dx_ref[...] = r * (dn - n * jnp.mean(dn * n, axis=-1, keepdims=True)) + dres_ref[...]
        dhn = dhv * n

        @pl.when(i == 0)
        def _():
            dg_ref[...] = jnp.zeros_like(dg_ref)
            dsc_ref[...] = jnp.zeros_like(dsc_ref)
            dsh_ref[...] = jnp.zeros_like(dsh_ref)

        dg_ref[...] += jnp.sum(dhn * sc1, axis=0, keepdims=True)
        dsc_ref[...] += jnp.sum(dhn * gain_v, axis=0, keepdims=True)
        dsh_ref[...] += jnp.sum(dhv, axis=0, keepdims=True)

    blk = pl.BlockSpec((tb, D), lambda i: (i, 0))
    row = jax.ShapeDtypeStruct((1, D), F32)
    return pl.pallas_call(
        body, name=name, grid=(T // tb,),
        in_specs=[blk, _row_spec(D), _row_spec(D), blk, blk],
        out_specs=[blk, _row_spec(D), _row_spec(D), _row_spec(D)],
        out_shape=[jax.ShapeDtypeStruct((T, D), F32), row, row, row],
        compiler_params=_params(1))(x, gain, sc, dh, dres)


def gres_fwd(x, g, y, name):
    T, D = x.shape
    tb = _rows(T, 512)

    def body(x_ref, g_ref, y_ref, o_ref):
        o_ref[...] = x_ref[...] + g_ref[...] * y_ref[...]

    blk = pl.BlockSpec((tb, D), lambda i: (i, 0))
    return pl.pallas_call(
        body, name=name, grid=(T // tb,), in_specs=[blk, _row_spec(D), blk], out_specs=blk,
        out_shape=jax.ShapeDtypeStruct((T, D), F32), compiler_params=_params(1))(x, g, y)


def gres_bwd(dx, g, y, name):
    T, D = dx.shape
    tb = _rows(T, 512)

    def body(dx_ref, g_ref, y_ref, dy_ref, dg_ref):
        i = pl.program_id(0)
        dxv = dx_ref[...]
        dy_ref[...] = (dxv * g_ref[...]).astype(BF16)

        @pl.when(i == 0)
        def _():
            dg_ref[...] = jnp.zeros_like(dg_ref)

        dg_ref[...] += jnp.sum(dxv * y_ref[...], axis=0, keepdims=True)

    blk = pl.BlockSpec((tb, D), lambda i: (i, 0))
    return pl.pallas_call(
        body, name=name, grid=(T // tb,), in_specs=[blk, _row_spec(D), blk],
        out_specs=[blk, _row_spec(D)],
        out_shape=[jax.ShapeDtypeStruct((T, D), BF16), jax.ShapeDtypeStruct((1, D), F32)],
        compiler_params=_params(1))(dx, g, y)


def loss_head(y, target, name):
    T, D = y.shape
    tb = _rows(T, 512)

    def body(y_ref, t_ref, l_ref, dy_ref):
        i = pl.program_id(0)
        err = y_ref[...] - t_ref[...]
        dy_ref[...] = err * (1.0 / D)

        @pl.when(i == 0)
        def _():
            l_ref[...] = jnp.zeros_like(l_ref)

        sq = jnp.sum(err * err, axis=0, keepdims=True)
        tot = sq[:, 0:LANE]
        for k in range(1, D // LANE):
            tot = tot + sq[:, k * LANE:(k + 1) * LANE]
        l_ref[...] += tot

    blk = pl.BlockSpec((tb, D), lambda i: (i, 0))
    return pl.pallas_call(
        body, name=name, grid=(T // tb,), in_specs=[blk, blk],
        out_specs=[_row_spec(LANE), blk],
        out_shape=[jax.ShapeDtypeStruct((1, LANE), F32), jax.ShapeDtypeStruct((T, D), F32)],
        compiler_params=_params(1))(y, target)


def _back(ext, s):
    return ext if s == 0 else pltpu.roll(ext, s, 0)


def _ahead(ext, s):
    return ext if s == 0 else pltpu.roll(ext, ext.shape[0] - s, 0)


def _halo_prev(tb, h):
    return lambda i, j: (jnp.maximum(i * (tb // h) - 1, 0), j)


def _halo_next(tb, h, nrb):
    return lambda i, j: (jnp.minimum(i + 1, nrb - 1) * (tb // h), j)


def ffn_mid_fwd(up, conv_w8, conv_b, name):
    T, F2 = up.shape
    Fd = F2 // 2
    tb, cb = _rows(T, 512), _tile(Fd, 256)
    ncb = Fd // cb

    def body(g_ref, gp_ref, v_ref, w_ref, b_ref, o_ref):
        i = pl.program_id(0)
        g = g_ref[...]
        prev = jnp.where(i > 0, gp_ref[...], 0.0)
        ext = jnp.concatenate([prev, g], axis=0)
        w = w_ref[...]
        gc = (w[2:3] * g + w[1:2] * _back(ext, 1)[SUBLANE:] + w[0:1] * _back(ext, 2)[SUBLANE:]
              + b_ref[...])
        o_ref[...] = (_silu(gc) * v_ref[...]).astype(BF16)

    return pl.pallas_call(
        body, name=name, grid=(T // tb, ncb),
        in_specs=[pl.BlockSpec((tb, cb), lambda i, j: (i, j)),
                  pl.BlockSpec((SUBLANE, cb), _halo_prev(tb, SUBLANE)),
                  pl.BlockSpec((tb, cb), lambda i, j: (i, j + ncb)),
                  pl.BlockSpec((SUBLANE, cb), lambda i, j: (0, j)),
                  pl.BlockSpec((1, cb), lambda i, j: (0, j))],
        out_specs=pl.BlockSpec((tb, cb), lambda i, j: (i, j)),
        out_shape=jax.ShapeDtypeStruct((T, Fd), BF16),
        compiler_params=_params(2))(up, up, up, conv_w8, conv_b)


def ffn_mid_bwd(up, conv_w8, conv_b, dact, name):
    T, F2 = up.shape
    Fd = F2 // 2
    tb, cb = _rows(T, 512), _tile(Fd, 256)
    ncb, nrb = Fd // cb, T // tb
    H = SUBLANE

    def body(g_ref, gp_ref, gn_ref, v_ref, vn_ref, d_ref, dn_ref, w_ref, b_ref,
             dg_ref, dv_ref, dw_ref, db_ref):
        i = pl.program_id(1)
        g = g_ref[...]
        prev = jnp.where(i > 0, gp_ref[...], 0.0)
        ext = jnp.concatenate([prev, g, gn_ref[...]], axis=0)
        w = w_ref[...]
        e1, e2 = _back(ext, 1), _back(ext, 2)
        gc = (w[2:3] * ext + w[1:2] * e1 + w[0:1] * e2 + b_ref[...])[H:]
        val = jnp.concatenate([v_ref[...], vn_ref[...]], axis=0)
        dnext = jnp.where(i < nrb - 1, dn_ref[...].astype(F32), 0.0)
        da = jnp.concatenate([d_ref[...].astype(F32), dnext], axis=0)
        sg = _sigmoid(gc)
        dv_ref[...] = (da * gc * sg)[:tb].astype(BF16)
        dgc = da * val * (sg * (1.0 + gc * (1.0 - sg)))
        dg_ref[...] = (w[2:3] * dgc + w[1:2] * _ahead(dgc, 1) + w[0:1] * _ahead(dgc, 2))[:tb].astype(BF16)
        dc = dgc[:tb]

        @pl.when(i == 0)
        def _():
            dw_ref[...] = jnp.zeros_like(dw_ref)
            db_ref[...] = jnp.zeros_like(db_ref)

        dw_ref[2:3, :] += jnp.sum(dc * g, axis=0, keepdims=True)
        dw_ref[1:2, :] += jnp.sum(dc * e1[H:H + tb], axis=0, keepdims=True)
        dw_ref[0:1, :] += jnp.sum(dc * e2[H:H + tb], axis=0, keepdims=True)
        db_ref[...] += jnp.sum(dc, axis=0, keepdims=True)

    cur = lambda j, i: (i, j)
    prv = lambda j, i: _halo_prev(tb, H)(i, j)
    nxt = lambda j, i: _halo_next(tb, H, nrb)(i, j)
    return pl.pallas_call(
        body, name=name, grid=(ncb, nrb),
        in_specs=[pl.BlockSpec((tb, cb), cur), pl.BlockSpec((H, cb), prv), pl.BlockSpec((H, cb), nxt),
                  pl.BlockSpec((tb, cb), lambda j, i: (i, j + ncb)),
                  pl.BlockSpec((H, cb), lambda j, i: (jnp.minimum(i + 1, nrb - 1) * (tb // H), j + ncb)),
                  pl.BlockSpec((tb, cb), cur), pl.BlockSpec((H, cb), nxt),
                  pl.BlockSpec((SUBLANE, cb), lambda j, i: (0, j)),
                  pl.BlockSpec((1, cb), lambda j, i: (0, j))],
        out_specs=[pl.BlockSpec((tb, cb), cur), pl.BlockSpec((tb, cb), cur),
                   pl.BlockSpec((SUBLANE, cb), lambda j, i: (0, j)),
                   pl.BlockSpec((1, cb), lambda j, i: (0, j))],
        out_shape=[jax.ShapeDtypeStruct((T, Fd), BF16), jax.ShapeDtypeStruct((T, Fd), BF16),
                   jax.ShapeDtypeStruct((SUBLANE, Fd), F32), jax.ShapeDtypeStruct((1, Fd), F32)],
        compiler_params=_params(2))(up, up, up, up, up, dact, dact, conv_w8, conv_b)


def gdn_conv_fwd(proj, w8, name):
    T = proj.shape[0]
    tb, ncb = _rows(T, 512), 3 * GDN_H
    H = SUBLANE

    def body(x_ref, xp_ref, w_ref, o_ref):
        i, j = pl.program_id(0), pl.program_id(1)
        x = x_ref[...]
        prev = jnp.where(i > 0, xp_ref[...], 0.0)
        ext = jnp.concatenate([prev, x], axis=0)
        w = w_ref[...]
        c = (w[3:4] * x + w[2:3] * _back(ext, 1)[H:] + w[1:2] * _back(ext, 2)[H:]
             + w[0:1] * _back(ext, 3)[H:])
        a = _silu(c)
        nrm = a * lax.rsqrt(jnp.sum(a * a, axis=-1, keepdims=True) + RMS_EPS)
        o_ref[...] = jnp.where(j < 2 * GDN_H, nrm, a)

    return pl.pallas_call(
        body, name=name, grid=(T // tb, ncb),
        in_specs=[pl.BlockSpec((tb, HD), lambda i, j: (i, j)),
                  pl.BlockSpec((H, HD), _halo_prev(tb, H)),
                  pl.BlockSpec((SUBLANE, HD), lambda i, j: (0, j))],
        out_specs=pl.BlockSpec((tb, HD), lambda i, j: (i, j)),
        out_shape=jax.ShapeDtypeStruct((T, ncb * HD), F32),
        compiler_params=_params(2))(proj, proj, w8)


def gdn_conv_bwd(proj, w8, dout, name):
    T = proj.shape[0]
    tb, ncb = _rows(T, 512), 3 * GDN_H
    nrb = T // tb
    H = SUBLANE

    def body(x_ref, xp_ref, xn_ref, d_ref, dn_ref, w_ref, dx_ref, dw_ref):
        j, i = pl.program_id(0), pl.program_id(1)
        x = x_ref[...]
        prev = jnp.where(i > 0, xp_ref[...], 0.0)
        ext = jnp.concatenate([prev, x, xn_ref[...]], axis=0)
        w = w_ref[...]
        e1, e2, e3 = _back(ext, 1), _back(ext, 2), _back(ext, 3)
        c = (w[3:4] * ext + w[2:3] * e1 + w[1:2] * e2 + w[0:1] * e3)[H:]
        sg = _sigmoid(c)
        a = c * sg
        dnext = jnp.where(i < nrb - 1, dn_ref[...], 0.0)
        do = jnp.concatenate([d_ref[...], dnext], axis=0)
        r = lax.rsqrt(jnp.sum(a * a, axis=-1, keepdims=True) + RMS_EPS)
        y = a * r
        da_n = r * (do - y * jnp.sum(do * y, axis=-1, keepdims=True))
        da = jnp.where(j < 2 * GDN_H, da_n, do)
        dc = da * (sg * (1.0 + c * (1.0 - sg)))
        dx_ref[...] = (w[3:4] * dc + w[2:3] * _ahead(dc, 1) + w[1:2] * _ahead(dc, 2)
                       + w[0:1] * _ahead(dc, 3))[:tb].astype(BF16)
        dcc = dc[:tb]

        @pl.when(i == 0)
        def _():
            dw_ref[...] = jnp.zeros_like(dw_ref)

        dw_ref[3:4, :] += jnp.sum(dcc * x, axis=0, keepdims=True)
        dw_ref[2:3, :] += jnp.sum(dcc * e1[H:H + tb], axis=0, keepdims=True)
        dw_ref[1:2, :] += jnp.sum(dcc * e2[H:H + tb], axis=0, keepdims=True)
        dw_ref[0:1, :] += jnp.sum(dcc * e3[H:H + tb], axis=0, keepdims=True)

    cur = lambda j, i: (i, j)
    prv = lambda j, i: _halo_prev(tb, H)(i, j)
    nxt = lambda j, i: _halo_next(tb, H, nrb)(i, j)
    return pl.pallas_call(
        body, name=name, grid=(ncb, nrb),
        in_specs=[pl.BlockSpec((tb, HD), cur), pl.BlockSpec((H, HD), prv), pl.BlockSpec((H, HD), nxt),
                  pl.BlockSpec((tb, HD), cur), pl.BlockSpec((H, HD), nxt),
                  pl.BlockSpec((SUBLANE, HD), lambda j, i: (0, j))],
        out_specs=[pl.BlockSpec((tb, HD), cur), pl.BlockSpec((SUBLANE, HD), lambda j, i: (0, j))],
        out_shape=[jax.ShapeDtypeStruct((T, ncb * HD), BF16),
                   jax.ShapeDtypeStruct((SUBLANE, ncb * HD), F32)],
        compiler_params=_params(2))(proj, proj, proj, dout, dout, w8)


def _plain_dots(dot):
    return (lambda a, b: dot(a, b, NN)), (lambda a, b: dot(a, b, NT)), (lambda a, b: dot(a, b, TN))


def _vjp_dots(dot):
    @jax.custom_vjp
    def nn(a, b):
        return dot(a, b, NN)
    nn.defvjp(lambda a, b: (dot(a, b, NN), (a, b)),
              lambda res, g: (dot(g, res[1], NT), dot(res[0], g, TN)))

    @jax.custom_vjp
    def nt(a, b):
        return dot(a, b, NT)
    nt.defvjp(lambda a, b: (dot(a, b, NT), (a, b)),
              lambda res, g: (dot(g, res[1], NN), dot(g, res[0], TN)))

    @jax.custom_vjp
    def tn(a, b):
        return dot(a, b, TN)
    tn.defvjp(lambda a, b: (dot(a, b, TN), (a, b)),
              lambda res, g: (dot(res[1], g, NT), dot(res[0], g, NN)))
    return nn, nt, tn


def _gdn_head(dots, hdots, S, q, k, v, z, b_raw, a_raw, alog, dtb, gnorm):
    nn, nt, tn = dots
    hnn = hdots[0]
    C = GDN_CHUNK
    ii = lax.broadcasted_iota(jnp.int32, (C, C), 0)
    jj = lax.broadcasted_iota(jnp.int32, (C, C), 1)
    causal, strict = ii >= jj, ii > jj
    tri, tri_t = causal.astype(F32), (ii <= jj).astype(F32)
    eye, ones = (ii == jj).astype(F32), jnp.ones((C, C), F32)

    beta = _sigmoid(b_raw)
    xs = a_raw + dtb
    pos = xs > 0.0
    softplus = jnp.where(pos, xs, 0.0) + jnp.log(1.0 + jnp.exp(jnp.where(pos, -xs, xs)))
    g = -jnp.exp(alog) * softplus
    gb = jnp.broadcast_to(g, (C, C))
    gc_c = hnn(tri, gb)
    gc_r = hnn(hnn(ones, eye * gb), tri_t)
    gc = hnn(tri, jnp.broadcast_to(g, (C, HD)))
    gl = jnp.sum(g, axis=0, keepdims=True)
    decay = jnp.where(causal, jnp.exp(jnp.where(causal, gc_c - gc_r, 0.0)), 0.0)
    q = q * (HD ** -0.5)
    kb = k * beta
    L = jnp.where(strict, nt(kb, k) * decay, 0.0)
    egc = jnp.exp(gc)
    P = eye - L
    M = hnn(L, L)
    for step in range(5):
        P = P + hnn(P, M)
        if step < 4:
            M = hnn(M, M)
    u = hnn(P, v * beta)
    w = hnn(P, kb * egc)
    intra = jnp.where(causal, nt(q, k) * decay, 0.0)
    v_new = u - nn(w, S)
    o = nn(q * egc, S) + nn(intra, v_new)
    S_new = S * jnp.exp(gl) + tn(k * jnp.exp(gl - gc), v_new)
    r = lax.rsqrt(jnp.mean(o * o, axis=-1, keepdims=True) + RMS_EPS)
    return o * r * gnorm * _silu(z), S_new


def _gdn_split(qkv, ba, z, alog_row, dt_row):
    heads = []
    for h in range(GDN_H):
        heads.append((qkv[:, h * HD:(h + 1) * HD], qkv[:, (GDN_H + h) * HD:(GDN_H + h + 1) * HD],
                      qkv[:, (2 * GDN_H + h) * HD:(2 * GDN_H + h + 1) * HD], z[:, h * HD:(h + 1) * HD],
                      ba[:, h:h + 1], ba[:, GDN_H + h:GDN_H + h + 1],
                      alog_row[:, h:h + 1], dt_row[:, h:h + 1]))
    return heads


def gdn_chunk_fwd(qkv, proj, alog_row, dt_row, gnorm, name):
    T = qkv.shape[0]
    C = GDN_CHUNK
    N = T // C
    dots, hdots = _plain_dots(_bdot), _plain_dots(_hdot)

    def body(qkv_ref, ba_ref, z_ref, al_ref, dt_ref, gn_ref, o_ref, save_ref, S_ref):
        n = pl.program_id(0)

        @pl.when(n == 0)
        def _():
            S_ref[...] = jnp.zeros_like(S_ref)

        heads = _gdn_split(qkv_ref[...], ba_ref[...], z_ref[...], al_ref[...], dt_ref[...])
        gn = gn_ref[...]
        for h, (q, k, v, z, b_raw, a_raw, alog, dtb) in enumerate(heads):
            S = S_ref[h]
            save_ref[0, h] = S
            o, S_new = _gdn_head(dots, hdots, S, q, k, v, z, b_raw, a_raw, alog, dtb, gn)
            o_ref[:, h * HD:(h + 1) * HD] = o.astype(BF16)
            S_ref[h] = S_new

    W = GDN_H * HD
    return pl.pallas_call(
        body, name=name, grid=(N,),
        in_specs=[pl.BlockSpec((C, 3 * W), lambda n: (n, 0)),
                  pl.BlockSpec((C, LANE), lambda n: (n, (4 * W + POOL_G * HD) // LANE)),
                  pl.BlockSpec((C, W), lambda n: (n, 3)),
                  _row_spec(LANE), _row_spec(LANE), _row_spec(HD)],
        out_specs=[pl.BlockSpec((C, W), lambda n: (n, 0)),
                   pl.BlockSpec((1, GDN_H, HD, HD), lambda n: (n, 0, 0, 0))],
        out_shape=[jax.ShapeDtypeStruct((T, W), BF16), jax.ShapeDtypeStruct((N, GDN_H, HD, HD), F32)],
        scratch_shapes=[pltpu.VMEM((GDN_H, HD, HD), F32)],
        compiler_params=_params(1))(qkv, proj, proj, alog_row, dt_row, gnorm)


def gdn_chunk_bwd(qkv, proj, alog_row, dt_row, gnorm, saved, docat, name):
    T = qkv.shape[0]
    C = GDN_CHUNK
    N = T // C
    dots, hdots = _vjp_dots(_bdot), _vjp_dots(_hdot)
    lane = lambda: lax.broadcasted_iota(jnp.int32, (1, LANE), 1)

    def body(qkv_ref, ba_ref, z_ref, al_ref, dt_ref, gn_ref, save_ref, do_ref,
             dqkv_ref, dz_ref, dba_ref, dal_ref, ddt_ref, dgn_ref, dS_ref):
        n = pl.program_id(0)

        @pl.when(n == 0)
        def _():
            dS_ref[...] = jnp.zeros_like(dS_ref)
            dal_ref[...] = jnp.zeros_like(dal_ref)
            ddt_ref[...] = jnp.zeros_like(ddt_ref)
            dgn_ref[...] = jnp.zeros_like(dgn_ref)

        heads = _gdn_split(qkv_ref[...], ba_ref[...], z_ref[...], al_ref[...], dt_ref[...])
        gn = gn_ref[...]
        dba = jnp.zeros((C, LANE), F32)
        dal = jnp.zeros((1, LANE), F32)
        ddt = jnp.zeros((1, LANE), F32)
        dgn = jnp.zeros((1, HD), F32)
        for h, (q, k, v, z, b_raw, a_raw, alog, dtb) in enumerate(heads):
            fn = functools.partial(_gdn_head, dots, hdots)
            _, vjp = jax.vjp(fn, save_ref[0, h], q, k, v, z, b_raw, a_raw, alog, dtb, gn)
            dS, dq, dk, dv, dzh, db_raw, da_raw, dalog, ddtb, dgn_h = vjp(
                (do_ref[:, h * HD:(h + 1) * HD], dS_ref[h]))
            dS_ref[h] = dS
            dqkv_ref[:, h * HD:(h + 1) * HD] = dq
            dqkv_ref[:, (GDN_H + h) * HD:(GDN_H + h + 1) * HD] = dk
            dqkv_ref[:, (2 * GDN_H + h) * HD:(2 * GDN_H + h + 1) * HD] = dv
            dz_ref[:, h * HD:(h + 1) * HD] = dzh.astype(BF16)
            hot_b = (lane() == h).astype(F32)
            hot_a = (lane() == GDN_H + h).astype(F32)
            dba = dba + db_raw * hot_b + da_raw * hot_a
            dal = dal + dalog * hot_b
            ddt = ddt + ddtb * hot_b
            dgn = dgn + dgn_h
        dba_ref[...] = dba.astype(BF16)
        dal_ref[...] += dal
        ddt_ref[...] += ddt
        dgn_ref[...] += dgn

    W = GDN_H * HD
    rev = lambda n: N - 1 - n
    row = jax.ShapeDtypeStruct((1, LANE), F32)
    return pl.pallas_call(
        body, name=name, grid=(N,),
        in_specs=[pl.BlockSpec((C, 3 * W), lambda n: (rev(n), 0)),
                  pl.BlockSpec((C, LANE), lambda n: (rev(n), (4 * W + POOL_G * HD) // LANE)),
                  pl.BlockSpec((C, W), lambda n: (rev(n), 3)),
                  _row_spec(LANE), _row_spec(LANE), _row_spec(HD),
                  pl.BlockSpec((1, GDN_H, HD, HD), lambda n: (rev(n), 0, 0, 0)),
                  pl.BlockSpec((C, W), lambda n: (rev(n), 0))],
        out_specs=[pl.BlockSpec((C, 3 * W), lambda n: (rev(n), 0)),
                   pl.BlockSpec((C, W), lambda n: (rev(n), 0)),
                   pl.BlockSpec((C, LANE), lambda n: (rev(n), 0)),
                   _row_spec(LANE), _row_spec(LANE), _row_spec(HD)],
        out_shape=[jax.ShapeDtypeStruct((T, 3 * W), F32), jax.ShapeDtypeStruct((T, W), BF16),
                   jax.ShapeDtypeStruct((T, LANE), BF16), row, row, jax.ShapeDtypeStruct((1, HD), F32)],
        scratch_shapes=[pltpu.VMEM((GDN_H, HD, HD), F32)],
        compiler_params=_params(1))(qkv, proj, proj, alog_row, dt_row, gnorm, saved, docat)


POOL_HALO = 16


def _pool_pick(j, s2, s4, s8, s16):
    return jnp.where(j == 0, s2, jnp.where(j == 1, s4, jnp.where(j == 2, s8, s16)))


def _pool_count(j, t0, rows):
    t1 = (t0 + 1 + lax.broadcasted_iota(jnp.int32, (rows, 1), 0)).astype(F32)
    win = jnp.where(j == 0, 2.0, jnp.where(j == 1, 4.0, jnp.where(j == 2, 8.0, 16.0)))
    return jnp.minimum(t1, win)


def _pooled(p, prev, i, j, tb):
    ext = jnp.concatenate([prev, p], axis=0)
    s2 = ext + _back(ext, 1)
    s4 = s2 + _back(s2, 2)
    s8 = s4 + _back(s4, 4)
    s16 = s8 + _back(s8, 8)
    s = _pool_pick(j, s2, s4, s8, s16)[POOL_HALO:]
    return s / _pool_count(j, i * tb, tb) - p


def pool_fwd(proj, pool_w, pool_scale, name):
    T = proj.shape[0]
    tb = _rows(T, 512)
    c0 = 4 * GDN_H

    def body(p_ref, pp_ref, w_ref, s_ref, o_ref):
        i, j = pl.program_id(0), pl.program_id(1)
        p = p_ref[...]
        prev = jnp.where(i > 0, pp_ref[...], 0.0)
        pooled = _pooled(p, prev, i, j, tb)
        o_ref[...] = (_bdot(pooled, w_ref[0], NN) * s_ref[...]).astype(BF16)

    return pl.pallas_call(
        body, name=name, grid=(T // tb, POOL_G),
        in_specs=[pl.BlockSpec((tb, HD), lambda i, j: (i, c0 + j)),
                  pl.BlockSpec((POOL_HALO, HD), lambda i, j: (jnp.maximum(i * (tb // POOL_HALO) - 1, 0), c0 + j)),
                  pl.BlockSpec((1, HD, HD), lambda i, j: (j, 0, 0)),
                  pl.BlockSpec((1, HD), lambda i, j: (0, j))],
        out_specs=pl.BlockSpec((tb, HD), lambda i, j: (i, j)),
        out_shape=jax.ShapeDtypeStruct((T, POOL_G * HD), BF16),
        compiler_params=_params(2))(proj, proj, pool_w, pool_scale)


def pool_bwd(proj, pool_w, pool_scale, docat, name):
    T = proj.shape[0]
    tb = _rows(T, 512)
    nrb = T // tb
    c0 = 4 * GDN_H
    HB = POOL_HALO

    def body(p_ref, pp_ref, w_ref, s_ref, d_ref, dn_ref, dp_ref, dw_ref, ds_ref):
        j, i = pl.program_id(0), pl.program_id(1)
        p = p_ref[...]
        prev = jnp.where(i > 0, pp_ref[...], 0.0)
        pooled = _pooled(p, prev, i, j, tb)
        w, scale = w_ref[0], s_ref[...]
        dy = d_ref[...]
        dnext = jnp.where(i < nrb - 1, dn_ref[...], 0.0)
        dyp = jnp.concatenate([dy, dnext], axis=0) * scale
        dpooled = _bdot(dyp, w, NT)
        qn = dpooled / _pool_count(j, i * tb, tb + HB)
        a2 = qn + _ahead(qn, 1)
        a4 = a2 + _ahead(a2, 2)
        a8 = a4 + _ahead(a4, 4)
        a16 = a8 + _ahead(a8, 8)
        dp_ref[...] = (_pool_pick(j, a2, a4, a8, a16) - dpooled)[:tb].astype(BF16)

        @pl.when(i == 0)
        def _():
            dw_ref[...] = jnp.zeros_like(dw_ref)
            ds_ref[...] = jnp.zeros_like(ds_ref)

        dw_ref[0] += _bdot(pooled, dyp[:tb], TN)
        ds_ref[...] += jnp.sum(dy * _bdot(pooled, w, NN), axis=0, keepdims=True)

    return pl.pallas_call(
        body, name=name, grid=(POOL_G, nrb),
        in_specs=[pl.BlockSpec((tb, HD), lambda j, i: (i, c0 + j)),
                  pl.BlockSpec((HB, HD), lambda j, i: (jnp.maximum(i * (tb // HB) - 1, 0), c0 + j)),
                  pl.BlockSpec((1, HD, HD), lambda j, i: (j, 0, 0)),
                  pl.BlockSpec((1, HD), lambda j, i: (0, j)),
                  pl.BlockSpec((tb, HD), lambda j, i: (i, POOL_G + j)),
                  pl.BlockSpec((HB, HD), lambda j, i: (jnp.minimum(i + 1, nrb - 1) * (tb // HB), POOL_G + j))],
        out_specs=[pl.BlockSpec((tb, HD), lambda j, i: (i, j)),
                   pl.BlockSpec((1, HD, HD), lambda j, i: (j, 0, 0)),
                   pl.BlockSpec((1, HD), lambda j, i: (0, j))],
        out_shape=[jax.ShapeDtypeStruct((T, POOL_G * HD), BF16),
                   jax.ShapeDtypeStruct((POOL_G, HD, HD), F32),
                   jax.ShapeDtypeStruct((1, POOL_G * HD), F32)],
        compiler_params=_params(2))(proj, proj, pool_w, pool_scale, docat, docat)


ATT_W = ATT_H * HD
ODD_COLS = 3 * 3 * ATT_W


def headnorm_fwd(proj, qk_gain, name):
    T = proj.shape[0]
    tb = _rows(T, 512)
    ncb = ODD_COLS // HD

    def body(x_ref, g_ref, o_ref):
        j = pl.program_id(1)
        part = (j % (3 * ATT_H)) // ATT_H
        x = x_ref[...]
        n = x * lax.rsqrt(jnp.mean(x * x, axis=-1, keepdims=True) + RMS_EPS)
        g = g_ref[...]
        q = n * g[0:1] * (HD ** -0.5)
        k = n * g[1:2]
        o_ref[...] = jnp.where(part == 0, q, jnp.where(part == 1, k, x)).astype(BF16)

    return pl.pallas_call(
        body, name=name, grid=(T // tb, ncb),
        in_specs=[pl.BlockSpec((tb, HD), lambda i, j: (i, j)),
                  pl.BlockSpec((SUBLANE, HD), lambda i, j: (0, 0))],
        out_specs=pl.BlockSpec((tb, HD), lambda i, j: (i, j)),
        out_shape=jax.ShapeDtypeStruct((T, ODD_COLS), BF16),
        compiler_params=_params(2))(proj, qk_gain)


def headnorm_bwd(proj, qk_gain, dqkv, name):
    T = proj.shape[0]
    tb = _rows(T, 512)
    ncb = ODD_COLS // HD

    def body(x_ref, g_ref, d_ref, dx_ref, dg_ref):
        i, j = pl.program_id(0), pl.program_id(1)
        part = (j % (3 * ATT_H)) // ATT_H
        x = x_ref[...]
        r = lax.rsqrt(jnp.mean(x * x, axis=-1, keepdims=True) + RMS_EPS)
        n = x * r
        g = g_ref[...]
        d = d_ref[...]
        gain = jnp.where(part == 0, g[0:1] * (HD ** -0.5), g[1:2])
        dn = d * gain
        dx = r * (dn - n * jnp.mean(dn * n, axis=-1, keepdims=True))
        dx_ref[...] = jnp.where(part == 2, d, dx).astype(BF16)

        @pl.when((i == 0) & (j == 0))
        def _():
            dg_ref[...] = jnp.zeros_like(dg_ref)

        s = jnp.sum(d * n, axis=0, keepdims=True)

        @pl.when(part == 0)
        def _():
            dg_ref[0:1, :] += s * (HD ** -0.5)

        @pl.when(part == 1)
        def _():
            dg_ref[1:2, :] += s

    return pl.pallas_call(
        body, name=name, grid=(T // tb, ncb),
        in_specs=[pl.BlockSpec((tb, HD), lambda i, j: (i, j)),
                  pl.BlockSpec((SUBLANE, HD), lambda i, j: (0, 0)),
                  pl.BlockSpec((tb, HD), lambda i, j: (i, j))],
        out_specs=[pl.BlockSpec((tb, HD), lambda i, j: (i, j)),
                   pl.BlockSpec((SUBLANE, HD), lambda i, j: (0, 0))],
        out_shape=[jax.ShapeDtypeStruct((T, ODD_COLS), BF16), jax.ShapeDtypeStruct((SUBLANE, HD), F32)],
        compiler_params=_params(2))(proj, qk_gain, dqkv)


def _att_scores(q, k, slope, n_ok, prev):
    a = lax.broadcasted_iota(jnp.int32, (ATT_BLK, ATT_BLK), 0)
    j = lax.broadcasted_iota(jnp.int32, (ATT_BLK, ATT_BLK), 1)
    rel = (ATT_BLK + a - j) if prev else (a - j)
    mask = ((j >= a) & n_ok) if prev else (j <= a)
    s = _bdot(q, k, NT) - slope * rel.astype(F32)
    return jnp.where(mask, s, NEG), mask


def _att_scores_t(k, q, slope, n_ok, nxt):
    j = lax.broadcasted_iota(jnp.int32, (ATT_BLK, ATT_BLK), 0)
    a = lax.broadcasted_iota(jnp.int32, (ATT_BLK, ATT_BLK), 1)
    rel = (ATT_BLK + a - j) if nxt else (a - j)
    mask = ((j >= a) & n_ok) if nxt else (j <= a)
    s = _bdot(k, q, NT) - slope * rel.astype(F32)
    return jnp.where(mask, s, NEG), mask


def _att_specs(gi, dil, nb):
    col = lambda r, part: r * 9 + gi * 3 + part
    q = pl.BlockSpec((ATT_BLK, ATT_W), lambda r, n: (n, col(r, 0)))
    kp = pl.BlockSpec((ATT_BLK, ATT_W), lambda r, n: (jnp.maximum(n - 1, 0), col(r, 1)))
    kc = pl.BlockSpec((ATT_BLK, ATT_W), lambda r, n: (n, col(r, 1)))
    vp = pl.BlockSpec((ATT_BLK, ATT_W), lambda r, n: (jnp.maximum(n - 1, 0), col(r, 2)))
    vc = pl.BlockSpec((ATT_BLK, ATT_W), lambda r, n: (n, col(r, 2)))
    return q, kp, kc, vp, vc


def att_fwd(qkvn, gi, name):
    T = qkvn.shape[0]
    dil = DIL[gi]
    L = T // dil
    nb = L // ATT_BLK
    view = qkvn.reshape(L, dil * ODD_COLS)

    def body(q_ref, kp_ref, kc_ref, vp_ref, vc_ref, o_ref, l_ref):
        n_ok = pl.program_id(1) > 0
        for h in range(ATT_H):
            sl = slice(h * HD, (h + 1) * HD)
            slope = (2.0 ** -(h + 1)) * dil
            q = q_ref[:, sl]
            s_c, _ = _att_scores(q, kc_ref[:, sl], slope, n_ok, False)
            s_p, _ = _att_scores(q, kp_ref[:, sl], slope, n_ok, True)
            m = jnp.maximum(jnp.max(s_c, axis=-1, keepdims=True), jnp.max(s_p, axis=-1, keepdims=True))
            p_c, p_p = jnp.exp(s_c - m), jnp.exp(s_p - m)
            l = jnp.sum(p_c, axis=-1, keepdims=True) + jnp.sum(p_p, axis=-1, keepdims=True)
            o = _bdot(p_c, vc_ref[:, sl], NN) + _bdot(p_p, vp_ref[:, sl], NN)
            o_ref[:, sl] = o / l
            l_ref[:, sl] = jnp.broadcast_to(m + jnp.log(l), (ATT_BLK, HD))

    out = pl.BlockSpec((ATT_BLK, ATT_W), lambda r, n: (n, r))
    o, lse = pl.pallas_call(
        body, name=name, grid=(dil, nb), in_specs=list(_att_specs(gi, dil, nb)),
        out_specs=[out, out],
        out_shape=[jax.ShapeDtypeStruct((L, dil * ATT_W), F32)] * 2,
        compiler_params=_params(2))(view, view, view, view, view)
    return o.reshape(T, ATT_W), lse.reshape(T, ATT_W)


def att_merge(os, lses, name):
    T = os[0].shape[0]
    tb = _rows(T, 512)

    def body(o0, o1, o2, l0, l1, l2, o_ref, l_ref):
        a, b, c = l0[...], l1[...], l2[...]
        m = jnp.maximum(a, jnp.maximum(b, c))
        wa, wb, wc = jnp.exp(a - m), jnp.exp(b - m), jnp.exp(c - m)
        den = wa + wb + wc
        o_ref[...] = (wa * o0[...] + wb * o1[...] + wc * o2[...]) / den
        l_ref[...] = m + jnp.log(den)

    blk = pl.BlockSpec((tb, ATT_W), lambda i: (i, 0))
    return pl.pallas_call(
        body, name=name, grid=(T // tb,), in_specs=[blk] * 6, out_specs=[blk, blk],
        out_shape=[jax.ShapeDtypeStruct((T, ATT_W), F32)] * 2,
        compiler_params=_params(1))(*os, *lses)


def att_delta(do, o, name):
    T = do.shape[0]
    tb = _rows(T, 512)

    def body(d_ref, o_ref, out_ref):
        s = jnp.sum(d_ref[...] * o_ref[...], axis=-1, keepdims=True)
        out_ref[...] = jnp.broadcast_to(s, (tb, HD))

    blk = pl.BlockSpec((tb, HD), lambda i, j: (i, j))
    return pl.pallas_call(
        body, name=name, grid=(T // tb, ATT_H), in_specs=[blk, blk], out_specs=blk,
        out_shape=jax.ShapeDtypeStruct((T, ATT_W), F32), compiler_params=_params(2))(do, o)


def att_bwd_q(qkvn, do, lse, delta, gi, name):
    T = qkvn.shape[0]
    dil = DIL[gi]
    L = T // dil
    nb = L // ATT_BLK
    view = qkvn.reshape(L, dil * ODD_COLS)
    rs = lambda t: t.reshape(L, dil * ATT_W)

    def body(q_ref, kp_ref, kc_ref, vp_ref, vc_ref, do_ref, l_ref, d_ref, dq_ref):
        n_ok = pl.program_id(1) > 0
        for h in range(ATT_H):
            sl = slice(h * HD, (h + 1) * HD)
            slope = (2.0 ** -(h + 1)) * dil
            q, do_h, lse_h, dl_h = q_ref[:, sl], do_ref[:, sl], l_ref[:, sl], d_ref[:, sl]
            dq = jnp.zeros((ATT_BLK, HD), F32)
            for k_ref, v_ref, prev in ((kc_ref, vc_ref, False), (kp_ref, vp_ref, True)):
                s, mask = _att_scores(q, k_ref[:, sl], slope, n_ok, prev)
                p = jnp.where(mask, jnp.exp(s - lse_h), 0.0)
                ds = p * (_bdot(do_h, v_ref[:, sl], NT) - dl_h)
                dq = dq + _bdot(ds, k_ref[:, sl], NN)
            dq_ref[:, sl] = dq

    blk = pl.BlockSpec((ATT_BLK, ATT_W), lambda r, n: (n, r))
    dq = pl.pallas_call(
        body, name=name, grid=(dil, nb), in_specs=list(_att_specs(gi, dil, nb)) + [blk, blk, blk],
        out_specs=blk, out_shape=jax.ShapeDtypeStruct((L, dil * ATT_W), F32),
        compiler_params=_params(2))(view, view, view, view, view, rs(do), rs(lse), rs(delta))
    return dq.reshape(T, ATT_W)


def att_bwd_kv(qkvn, do, lse, delta, gi, name):
    T = qkvn.shape[0]
    dil = DIL[gi]
    L = T // dil
    nb = L // ATT_BLK
    view = qkvn.reshape(L, dil * ODD_COLS)
    rs = lambda t: t.reshape(L, dil * ATT_W)
    col = lambda r, part: r * 9 + gi * 3 + part
    nx = lambda n: jnp.minimum(n + 1, nb - 1)

    def body(k_ref, v_ref, q0_ref, q1_ref, do0_ref, do1_ref, l0_ref, l1_ref, d0_ref, d1_ref,
             dk_ref, dv_ref):
        n_ok = pl.program_id(1) < nb - 1
        for h in range(ATT_H):
            sl = slice(h * HD, (h + 1) * HD)
            slope = (2.0 ** -(h + 1)) * dil
            k, v = k_ref[:, sl], v_ref[:, sl]
            dk = jnp.zeros((ATT_BLK, HD), F32)
            dv = jnp.zeros((ATT_BLK, HD), F32)
            for q_ref, do_ref, l_ref, d_ref, nxt in ((q0_ref, do0_ref, l0_ref, d0_ref, False),
                                                     (q1_ref, do1_ref, l1_ref, d1_ref, True)):
                q, do_h = q_ref[:, sl], do_ref[:, sl]
                s, mask = _att_scores_t(k, q, slope, n_ok, nxt)
                p = jnp.where(mask, jnp.exp(s - l_ref[:, sl].T), 0.0)
                dv = dv + _bdot(p, do_h, NN)
                ds = p * (_bdot(v, do_h, NT) - d_ref[:, sl].T)
                dk = dk + _bdot(ds, q, NN)
            dk_ref[:, sl] = dk
            dv_ref[:, sl] = dv

    kv = lambda part: pl.BlockSpec((ATT_BLK, ATT_W), lambda r, n: (n, col(r, part)))
    q0 = pl.BlockSpec((ATT_BLK, ATT_W), lambda r, n: (n, col(r, 0)))
    q1 = pl.BlockSpec((ATT_BLK, ATT_W), lambda r, n: (nx(n), col(r, 0)))
    b0 = pl.BlockSpec((ATT_BLK, ATT_W), lambda r, n: (n, r))
    b1 = pl.BlockSpec((ATT_BLK, ATT_W), lambda r, n: (nx(n), r))
    dk, dv = pl.pallas_call(
        body, name=name, grid=(dil, nb),
        in_specs=[kv(1), kv(2), q0, q1, b0, b1, b0, b1, b0, b1],
        out_specs=[b0, b0], out_shape=[jax.ShapeDtypeStruct((L, dil * ATT_W), F32)] * 2,
        compiler_params=_params(2))(view, view, view, view, rs(do), rs(do), rs(lse), rs(lse),
                                    rs(delta), rs(delta))
    return dk.reshape(T, ATT_W), dv.reshape(T, ATT_W)


def adamw(w, g, m, v, name):
    shape = w.shape
    C = shape[-1]
    R = math.prod(shape[:-1])
    to2d = lambda t: t.reshape(R, C)
    tb = _rows(R, max(16, (256 * 1536 // C) // 16 * 16))
    c1 = 1.0 - ADAM_B1 ** ADAM_STEP
    c2 = 1.0 - ADAM_B2 ** ADAM_STEP

    def body(w_ref, g_ref, m_ref, v_ref, d_ref, nm_ref, nv_ref):
        gv = g_ref[...]
        nm = ADAM_B1 * m_ref[...] + (1.0 - ADAM_B1) * gv
        nv = ADAM_B2 * v_ref[...] + (1.0 - ADAM_B2) * (gv * gv)
        d_ref[...] = -ADAM_LR * ((nm / c1) / (jnp.sqrt(nv / c2) + ADAM_EPS) + ADAM_WD * w_ref[...])
        nm_ref[...] = nm
        nv_ref[...] = nv

    blk = pl.BlockSpec((tb, C), lambda i: (i, 0))
    out = jax.ShapeDtypeStruct((R, C), F32)
    d, nm, nv = pl.pallas_call(
        body, name=name, grid=(R // tb,), in_specs=[blk] * 4, out_specs=[blk] * 3,
        out_shape=[out, out, out], compiler_params=_params(1))(to2d(w), to2d(g), to2d(m), to2d(v))
    return d.reshape(shape), nm.reshape(shape), nv.reshape(shape)


def _pad_rows8(w):
    return jnp.pad(w, ((0, SUBLANE - w.shape[0]), (0, 0)))


def _lane_row(v):
    return jnp.pad(v, (0, LANE - v.shape[0]))[None, :]


def _even_reorder(w_in):
    z4 = 4 * GDN_H * HD
    pad = jnp.zeros((w_in.shape[0], EVEN_PAD - EVEN_COLS), w_in.dtype)
    return jnp.concatenate([w_in[:, :z4], w_in[:, z4 + 2 * GDN_H:], w_in[:, z4:z4 + 2 * GDN_H], pad], axis=1)


def _even_restore(dw):
    z4 = 4 * GDN_H * HD
    p4 = POOL_G * HD
    return jnp.concatenate([dw[:, :z4], dw[:, z4 + p4:z4 + p4 + 2 * GDN_H], dw[:, z4:z4 + p4]], axis=1)


def _ffn_fwd(tag, x1, mod_f, wl):
    sh, sc, g = mod_f
    hf = modnorm_fwd(x1, wl["norm_ffn"], sc, sh, f"{tag}_ffn_norm")
    up = matmul(hf, wl["ffn_w_up"], "nn", F32, f"{tag}_ffn_up")
    act = ffn_mid_fwd(up, wl["ffn_conv_w8"], wl["ffn_conv_b"], f"{tag}_ffn_mid")
    f = matmul(act, wl["ffn_w_down"], "nn", F32, f"{tag}_ffn_down")
    x2 = gres_fwd(x1, g, f, f"{tag}_ffn_res")
    return x2, (x1, hf, up, act, f)


def _ffn_bwd(tag, dx2, saved, mod_f, wl):
    x1, hf, up, act, f = saved
    sh, sc, g = mod_f
    df, dg = gres_bwd(dx2, g, f, f"{tag}_ffn_res_bwd")
    dact = matmul(df, wl["ffn_w_down"], "nt", F32, f"{tag}_ffn_down_da")
    dw_down = matmul(act, df, "tn", F32, f"{tag}_ffn_down_dw")
    dgate, dval, dcw, dcb = ffn_mid_bwd(up, wl["ffn_conv_w8"], wl["ffn_conv_b"], dact, f"{tag}_ffn_mid_bwd")
    dup = jnp.concatenate([dgate, dval], axis=1)
    dhf = matmul(dup, wl["ffn_w_up"], "nt", F32, f"{tag}_ffn_up_da")
    dw_up = matmul(hf, dup, "tn", F32, f"{tag}_ffn_up_dw")
    dx1, dgain, dsc, dsh = modnorm_bwd(x1, wl["norm_ffn"], sc, dhf, dx2, f"{tag}_ffn_norm_bwd")
    grads = {"norm_ffn": dgain[0], "ffn_w_up": dw_up, "ffn_w_down": dw_down,
             "ffn_conv_w": dcw[:FFN_CONV], "ffn_conv_b": dcb[0]}
    return dx1, (dsh, dsc, dg), grads


def _even_fwd(tag, x, mod_m, wl):
    sh, sc, g = mod_m
    hm = modnorm_fwd(x, wl["norm_mix"], sc, sh, f"{tag}_mix_norm")
    proj = matmul(hm, wl["w_in"], "nn", F32, f"{tag}_ev_in")
    qkv = gdn_conv_fwd(proj, wl["gdn_conv_w8"], f"{tag}_gdn_conv")
    o_a, states = gdn_chunk_fwd(qkv, proj, wl["alog_row"], wl["dt_row"], wl["gdn_norm"], f"{tag}_gdn_chunk")
    o_b = pool_fwd(proj, wl["pool_w"], wl["pool_scale"], f"{tag}_pool")
    ocat = jnp.concatenate([o_a, o_b], axis=1)
    y = matmul(ocat, wl["w_out"], "nn", F32, f"{tag}_ev_out")
    x1 = gres_fwd(x, g, y, f"{tag}_mix_res")
    return x1, (x, hm, proj, qkv, states, ocat, y)


def _even_bwd(tag, dx1, saved, mod_m, wl):
    x, hm, proj, qkv, states, ocat, y = saved
    sh, sc, g = mod_m
    dy, dg = gres_bwd(dx1, g, y, f"{tag}_mix_res_bwd")
    docat = matmul(dy, wl["w_out"], "nt", F32, f"{tag}_ev_out_da")
    dw_out = matmul(ocat, dy, "tn", F32, f"{tag}_ev_out_dw")
    dqkv, dz, dba, dalog, ddt, dgn = gdn_chunk_bwd(
        qkv, proj, wl["alog_row"], wl["dt_row"], wl["gdn_norm"], states, docat, f"{tag}_gdn_chunk_bwd")
    dxc, dconv = gdn_conv_bwd(proj, wl["gdn_conv_w8"], dqkv, f"{tag}_gdn_conv_bwd")
    dp, dpw, dps = pool_bwd(proj, wl["pool_w"], wl["pool_scale"], docat, f"{tag}_pool_bwd")
    dproj = jnp.concatenate([dxc, dz, dp, dba], axis=1)
    dhm = matmul(dproj, wl["w_in"], "nt", F32, f"{tag}_ev_in_da")
    dw_in = matmul(hm, dproj, "tn", F32, f"{tag}_ev_in_dw")
    dx, dgain, dsc, dsh = modnorm_bwd(x, wl["norm_mix"], sc, dhm, dx1, f"{tag}_mix_norm_bwd")
    grads = {"norm_mix": dgain[0], "ev_w_in": _even_restore(dw_in), "ev_w_out": dw_out,
             "gdn_conv_w": dconv[:GDN_CONV], "gdn_a_log": dalog[0, :GDN_H], "gdn_dt_bias": ddt[0, :GDN_H],
             "gdn_norm": dgn[0], "pool_w": dpw, "pool_scale": dps[0]}
    return dx, (dsh, dsc, dg), grads


def _odd_fwd(tag, x, mod_m, wl):
    sh, sc, g = mod_m
    hm = modnorm_fwd(x, wl["norm_mix"], sc, sh, f"{tag}_mix_norm")
    proj = matmul(hm, wl["w_in"], "nn", F32, f"{tag}_od_in")
    qkvn = headnorm_fwd(proj, wl["qk_gain8"], f"{tag}_headnorm")
    outs = [att_fwd(qkvn, gi, f"{tag}_att{gi}") for gi in range(len(DIL))]
    o, lse = att_merge([t[0] for t in outs], [t[1] for t in outs], f"{tag}_att_merge")
    y = matmul(o, wl["w_out"], "nn", F32, f"{tag}_od_out")
    x1 = gres_fwd(x, g, y, f"{tag}_mix_res")
    return x1, (x, hm, proj, qkvn, o, lse, y)


def _odd_bwd(tag, dx1, saved, mod_m, wl):
    x, hm, proj, qkvn, o, lse, y = saved
    sh, sc, g = mod_m
    dy, dg = gres_bwd(dx1, g, y, f"{tag}_mix_res_bwd")
    do = matmul(dy, wl["w_out"], "nt", F32, f"{tag}_od_out_da")
    dw_out = matmul(o, dy, "tn", F32, f"{tag}_od_out_dw")
    delta = att_delta(do, o, f"{tag}_att_delta")
    parts = []
    for gi in range(len(DIL)):
        dq = att_bwd_q(qkvn, do, lse, delta, gi, f"{tag}_att{gi}_dq")
        dk, dv = att_bwd_kv(qkvn, do, lse, delta, gi, f"{tag}_att{gi}_dkv")
        parts += [dq, dk, dv]
    dqkv = jnp.concatenate(parts, axis=1)
    dproj, dgain_qk = headnorm_bwd(proj, wl["qk_gain8"], dqkv, f"{tag}_headnorm_bwd")
    dhm = matmul(dproj, wl["w_in"], "nt", F32, f"{tag}_od_in_da")
    dw_in = matmul(hm, dproj, "tn", F32, f"{tag}_od_in_dw")
    dx, dgain, dsc, dsh = modnorm_bwd(x, wl["norm_mix"], sc, dhm, dx1, f"{tag}_mix_norm_bwd")
    grads = {"norm_mix": dgain[0], "od_w_in": dw_in, "od_w_out": dw_out,
             "att_q_norm": dgain_qk[0], "att_k_norm": dgain_qk[1]}
    return dx, (dsh, dsc, dg), grads


def _layer_weights(i, W):
    wl = {"norm_mix": W["norm_mix"][i][None, :], "norm_ffn": W["norm_ffn"][i][None, :],
          "ffn_w_up": W["ffn_w_up"][i], "ffn_w_down": W["ffn_w_down"][i],
          "ffn_conv_w8": _pad_rows8(W["ffn_conv_w"][i]), "ffn_conv_b": W["ffn_conv_b"][i][None, :]}
    e = i // 2
    if i % 2 == 0:
        wl.update({"w_in": _even_reorder(W["ev_w_in"][e]), "w_out": W["ev_w_out"][e],
                   "gdn_conv_w8": _pad_rows8(W["gdn_conv_w"][e]),
                   "alog_row": _lane_row(W["gdn_a_log"][e]), "dt_row": _lane_row(W["gdn_dt_bias"][e]),
                   "gdn_norm": W["gdn_norm"][e][None, :], "pool_w": W["pool_w"][e],
                   "pool_scale": W["pool_scale"][e][None, :]})
    else:
        wl.update({"w_in": W["od_w_in"][e], "w_out": W["od_w_out"][e],
                   "qk_gain8": _pad_rows8(jnp.stack([W["att_q_norm"][e], W["att_k_norm"][e]]))})
    return wl


def local_step(x, target, mod, W):
    depth = mod.shape[0]
    row = lambda i, k: mod[i, k][None, :]
    saved, wls = [], []
    for i in range(depth):
        wl = _layer_weights(i, W)
        mod_m = (row(i, 0), row(i, 1), row(i, 2))
        mod_f = (row(i, 3), row(i, 4), row(i, 5))
        fwd = _even_fwd if i % 2 == 0 else _odd_fwd
        x1, s_mix = fwd(f"l{i}", x, mod_m, wl)
        x, s_ffn = _ffn_fwd(f"l{i}", x1, mod_f, wl)
        saved.append((s_mix, s_ffn, mod_m, mod_f))
        wls.append(wl)
    sq, dx = loss_head(x, target, "loss_head")
    dmod, grads = [None] * depth, [None] * depth
    for i in reversed(range(depth)):
        s_mix, s_ffn, mod_m, mod_f = saved[i]
        dx, dmf, g_ffn = _ffn_bwd(f"l{i}", dx, s_ffn, mod_f, wls[i])
        bwd = _even_bwd if i % 2 == 0 else _odd_bwd
        dx, dmm, g_mix = bwd(f"l{i}", dx, s_mix, mod_m, wls[i])
        dmod[i] = jnp.concatenate([t for t in dmm + dmf], axis=0)
        grads[i] = {**g_mix, **g_ffn}
    return sq, dx, jnp.stack(dmod), grads


HBM = pl.BlockSpec(memory_space=pltpu.HBM)
DMA_SEM = pltpu.SemaphoreType.DMA


def _place():
    return lax.axis_index("x"), lax.axis_index("y"), lax.axis_index("c")


def _other_chips(mx, my):
    return [(1 - mx, my), (mx, 1 - my), (1 - mx, 1 - my)]


def all_gather8(x, name):
    def body(x_ref, out_ref, send_sems, recv_sems, local_sem):
        mx, my, mc = _place()
        me, sibling = (mx, my, mc), (mx, my, 1 - mc)
        chips = _other_chips(mx, my)

        def slot(px, py, pc):
            return out_ref.at[4 * px + 2 * py + pc]

        def copy(k, block, to, src=None):
            return pltpu.make_async_remote_copy(
                src_ref=slot(*block) if src is None else src, dst_ref=slot(*block),
                send_sem=send_sems.at[k], recv_sem=recv_sems.at[k], device_id=to, device_id_type=MESH)

        mine = pltpu.make_async_copy(x_ref, slot(*me), local_sem)
        mine.start()
        first = [copy(0, me, sibling, src=x_ref)]
        first += [copy(1 + j, me, (*chip, mc), src=x_ref) for j, chip in enumerate(chips)]
        for cp in first:
            cp.start()
        passed = [copy(4 + j, (*chip, mc), sibling) for j, chip in enumerate(chips)]
        for j, chip in enumerate(chips):
            copy(1 + j, (*chip, mc), me).wait_recv()
            passed[j].start()
        copy(0, sibling, me).wait_recv()
        for j, chip in enumerate(chips):
            copy(4 + j, (*chip, 1 - mc), me).wait_recv()
        for cp in first + passed:
            cp.wait_send()
        mine.wait()

    return pl.pallas_call(
        body, name=name, out_shape=jax.ShapeDtypeStruct((8,) + x.shape, x.dtype),
        in_specs=[HBM], out_specs=HBM,
        scratch_shapes=[DMA_SEM((7,)), DMA_SEM((7,)), DMA_SEM(())])(x)


def sibling_swap(x, name):
    def body(x_ref, out_ref, send_sem, recv_sem):
        mx, my, mc = _place()
        cp = pltpu.make_async_remote_copy(src_ref=x_ref, dst_ref=out_ref, send_sem=send_sem, recv_sem=recv_sem,
                                          device_id=(mx, my, 1 - mc), device_id_type=MESH)
        cp.start()
        cp.wait_send()
        cp.wait_recv()

    return pl.pallas_call(
        body, name=name, out_shape=jax.ShapeDtypeStruct(x.shape, x.dtype), in_specs=[HBM], out_specs=HBM,
        scratch_shapes=[DMA_SEM(()), DMA_SEM(())])(x)


def sibling_pair(r, name):
    def body(r_ref, out_ref, send_sem, recv_sem, local_sem):
        mx, my, mc = _place()
        local = pltpu.make_async_copy(r_ref, out_ref.at[mc], local_sem)
        local.start()
        cp = pltpu.make_async_remote_copy(src_ref=r_ref, dst_ref=out_ref.at[mc], send_sem=send_sem,
                                          recv_sem=recv_sem, device_id=(mx, my, 1 - mc), device_id_type=MESH)
        cp.start()
        cp.wait_send()
        pltpu.make_async_remote_copy(src_ref=r_ref, dst_ref=out_ref.at[1 - mc], send_sem=send_sem,
                                     recv_sem=recv_sem, device_id=(mx, my, 1 - mc),
                                     device_id_type=MESH).wait_recv()
        local.wait()

    return pl.pallas_call(
        body, name=name, out_shape=jax.ShapeDtypeStruct((2,) + r.shape, r.dtype), in_specs=[HBM],
        out_specs=HBM, scratch_shapes=[DMA_SEM(()), DMA_SEM(()), DMA_SEM(())])(r)


def chip_all_to_all(p, name):
    def body(p_ref, out_ref, send_sems, recv_sems, local_sem):
        mx, my, mc = _place()
        mine = 2 * mx + my
        chips = _other_chips(mx, my)
        local = pltpu.make_async_copy(p_ref.at[mine], out_ref.at[mine], local_sem)
        local.start()

        def copy(k, chip):
            return pltpu.make_async_remote_copy(
                src_ref=p_ref.at[2 * chip[0] + chip[1]], dst_ref=out_ref.at[mine],
                send_sem=send_sems.at[k], recv_sem=recv_sems.at[k], device_id=(*chip, mc), device_id_type=MESH)

        sends = [copy(k, chip) for k, chip in enumerate(chips)]
        for cp in sends:
            cp.start()
        for k, chip in enumerate(chips):
            pltpu.make_async_remote_copy(
                src_ref=p_ref.at[mine], dst_ref=out_ref.at[2 * chip[0] + chip[1]],
                send_sem=send_sems.at[k], recv_sem=recv_sems.at[k], device_id=(*chip, mc),
                device_id_type=MESH).wait_recv()
        for cp in sends:
            cp.wait_send()
        local.wait()

    return pl.pallas_call(
        body, name=name, out_shape=jax.ShapeDtypeStruct(p.shape, p.dtype), in_specs=[HBM], out_specs=HBM,
        scratch_shapes=[DMA_SEM((3,)), DMA_SEM((3,)), DMA_SEM(())])(p)


def cast_bf16(w, name):
    R, C = w.shape
    tb = _rows(R, max(16, (512 * 1536 // C) // 16 * 16))

    def body(w_ref, o_ref):
        o_ref[...] = w_ref[...].astype(BF16)

    blk = pl.BlockSpec((tb, C), lambda i: (i, 0))
    return pl.pallas_call(body, name=name, grid=(R // tb,), in_specs=[blk], out_specs=blk,
                          out_shape=jax.ShapeDtypeStruct((R, C), BF16), compiler_params=_params(1))(w)


def sum_slots(g, name):
    n, R, C = g.shape
    tb = _rows(R, max(16, (256 * 1536 // C) // 16 * 16))

    def body(*refs):
        acc = refs[0][...]
        for r in refs[1:n]:
            acc = acc + r[...]
        refs[n][...] = acc

    specs = [pl.BlockSpec((None, tb, C), functools.partial(lambda k, i: (k, i, 0), k)) for k in range(n)]
    return pl.pallas_call(body, name=name, grid=(R // tb,), in_specs=specs,
                          out_specs=pl.BlockSpec((tb, C), lambda i: (i, 0)),
                          out_shape=jax.ShapeDtypeStruct((R, C), F32), compiler_params=_params(1))(*([g] * n))


def _add2(a, b, name):
    R, C = a.shape
    tb = _rows(R, max(16, (256 * 1536 // C) // 16 * 16))

    def body(a_ref, b_ref, o_ref):
        o_ref[...] = a_ref[...] + b_ref[...]

    blk = pl.BlockSpec((tb, C), lambda i: (i, 0))
    return pl.pallas_call(body, name=name, grid=(R // tb,), in_specs=[blk, blk], out_specs=blk,
                          out_shape=jax.ShapeDtypeStruct((R, C), F32), compiler_params=_params(1))(a, b)


def ada_fwd(c_all, ada_w, bias, name):
    n, D, Cs = ada_w.shape
    tn = _tile(Cs, 512)

    def body(c_ref, w_ref, b_ref, o_ref):
        o_ref[...] = _bdot(_silu(c_ref[...]), w_ref[...], NN) + b_ref[...]

    return pl.pallas_call(
        body, name=name, grid=(n, Cs // tn),
        in_specs=[pl.BlockSpec((8, D), lambda l, j: (0, 0)),
                  pl.BlockSpec((None, D, tn), lambda l, j: (l, 0, j)),
                  pl.BlockSpec((None, 1, tn), lambda l, j: (l, 0, j))],
        out_specs=pl.BlockSpec((None, 8, tn), lambda l, j: (l, 0, j)),
        out_shape=jax.ShapeDtypeStruct((n, 8, Cs), F32), compiler_params=_params(2))(c_all, ada_w, bias)


def ada_bwd(c16, dmod16, name):
    n, _, Cs = dmod16.shape
    D = c16.shape[1]
    tn = _tile(Cs, 512)

    def body(c_ref, d_ref, o_ref):
        o_ref[...] = _bdot(_silu(c_ref[...]), d_ref[...], TN)

    return pl.pallas_call(
        body, name=name, grid=(n, Cs // tn),
        in_specs=[pl.BlockSpec((16, D), lambda l, j: (0, 0)),
                  pl.BlockSpec((None, 16, tn), lambda l, j: (l, 0, j))],
        out_specs=pl.BlockSpec((None, D, tn), lambda l, j: (l, 0, j)),
        out_shape=jax.ShapeDtypeStruct((n, D, Cs), F32), compiler_params=_params(2))(c16, dmod16)


WEIGHTS = ["ada_w", "ada_b", "norm_mix", "norm_ffn", "ev_w_in", "ev_w_out", "gdn_conv_w", "gdn_a_log",
           "gdn_dt_bias", "gdn_norm", "pool_w", "pool_scale", "od_w_in", "od_w_out", "att_q_norm",
           "att_k_norm", "ffn_w_up", "ffn_conv_w", "ffn_conv_b", "ffn_w_down"]
COL_SHARDED = ("ev_w_in", "od_w_in", "ffn_w_up")
ROW_SHARDED = ("ev_w_out", "od_w_out", "ffn_w_down")
N_CHIPS = 4


def _pack(parts):
    rows, offs = [], []
    at = 0
    for p in parts:
        flat = p.reshape(-1).astype(F32)
        n = -(-flat.shape[0] // LANE)
        rows.append(jnp.pad(flat, (0, n * LANE - flat.shape[0])).reshape(n, LANE))
        offs.append((at, n))
        at += n
    pad = -at % 16
    if pad:
        rows.append(jnp.zeros((pad, LANE), F32))
    return jnp.concatenate(rows, axis=0), offs


def _unpack(buf, off, shape):
    at, n = off
    lead = buf.shape[:-2]
    flat = buf[..., at:at + n, :].reshape(lead + (n * LANE,))
    return flat[..., :math.prod(shape)].reshape(lead + tuple(shape))


def _gather_weight(name, w, mc):
    n = w.shape[0]
    Cs = w.shape[-1]
    Rs = math.prod(w.shape[:-1])
    half = lax.dynamic_index_in_dim(cast_bf16(w.reshape(Rs, Cs), f"cast_{name}").reshape(2, Rs // 2, Cs),
                                    mc, 0, keepdims=False)
    g = all_gather8(half, f"gather_{name}").reshape((N_CHIPS,) + w.shape)
    if name in COL_SHARDED:
        return g.transpose(1, 2, 0, 3).reshape(n, w.shape[1], N_CHIPS * Cs)
    return g.transpose(1, 0, 2, 3).reshape(n, N_CHIPS * w.shape[1], Cs)


def _reduce_weight_grad(name, dw, shard_shape, mc):
    n = dw.shape[0]
    Cs = shard_shape[-1]
    Rs = math.prod(shard_shape[:-1])
    if name in COL_SHARDED:
        by_chip = dw.reshape(n, dw.shape[1], N_CHIPS, Cs).transpose(2, 0, 1, 3)
    else:
        by_chip = dw.reshape(n, N_CHIPS, shard_shape[1], Cs).transpose(1, 0, 2, 3)
    halves = by_chip.reshape(N_CHIPS, 2, Rs // 2, Cs).transpose(1, 0, 2, 3).reshape(2, N_CHIPS * (Rs // 2), Cs)
    keep = lax.dynamic_index_in_dim(halves, mc, 0, keepdims=False)
    send = lax.dynamic_index_in_dim(halves, 1 - mc, 0, keepdims=False)
    chip_sum = _add2(keep, sibling_swap(send, f"gswap_{name}"), f"gadd_{name}")
    got = chip_all_to_all(chip_sum.reshape(N_CHIPS, Rs // 2, Cs), f"gscatter_{name}")
    half = sum_slots(got, f"gsum_{name}")
    return sibling_pair(half, f"gpair_{name}").reshape(shard_shape)


def kernel(x, c, ada_w, ada_b, norm_mix, norm_ffn, ev_w_in, ev_w_out, gdn_conv_w, gdn_a_log, gdn_dt_bias, gdn_norm, pool_w, pool_scale, od_w_in, od_w_out, att_q_norm, att_k_norm, ffn_w_up, ffn_conv_w, ffn_conv_b, ffn_w_down, loss_target, m_ada_w, m_ada_b, m_norm_mix, m_norm_ffn, m_ev_w_in, m_ev_w_out, m_gdn_conv_w, m_gdn_a_log, m_gdn_dt_bias, m_gdn_norm, m_pool_w, m_pool_scale, m_od_w_in, m_od_w_out, m_att_q_norm, m_att_k_norm, m_ffn_w_up, m_ffn_conv_w, m_ffn_conv_b, m_ffn_w_down, v_ada_w, v_ada_b, v_norm_mix, v_norm_ffn, v_ev_w_in, v_ev_w_out, v_gdn_conv_w, v_gdn_a_log, v_gdn_dt_bias, v_gdn_norm, v_pool_w, v_pool_scale, v_od_w_in, v_od_w_out, v_att_q_norm, v_att_k_norm, v_ffn_w_up, v_ffn_conv_w, v_ffn_conv_b, v_ffn_w_down):
    local = dict(ada_w=ada_w, ada_b=ada_b, norm_mix=norm_mix, norm_ffn=norm_ffn, ev_w_in=ev_w_in,
                 ev_w_out=ev_w_out, gdn_conv_w=gdn_conv_w, gdn_a_log=gdn_a_log, gdn_dt_bias=gdn_dt_bias,
                 gdn_norm=gdn_norm, pool_w=pool_w, pool_scale=pool_scale, od_w_in=od_w_in, od_w_out=od_w_out,
                 att_q_norm=att_q_norm, att_k_norm=att_k_norm, ffn_w_up=ffn_w_up, ffn_conv_w=ffn_conv_w,
                 ffn_conv_b=ffn_conv_b, ffn_w_down=ffn_w_down)
    moments_m = dict(zip(WEIGHTS, (m_ada_w, m_ada_b, m_norm_mix, m_norm_ffn, m_ev_w_in, m_ev_w_out,
                                   m_gdn_conv_w, m_gdn_a_log, m_gdn_dt_bias, m_gdn_norm, m_pool_w, m_pool_scale,
                                   m_od_w_in, m_od_w_out, m_att_q_norm, m_att_k_norm, m_ffn_w_up, m_ffn_conv_w,
                                   m_ffn_conv_b, m_ffn_w_down)))
    moments_v = dict(zip(WEIGHTS, (v_ada_w, v_ada_b, v_norm_mix, v_norm_ffn, v_ev_w_in, v_ev_w_out,
                                   v_gdn_conv_w, v_gdn_a_log, v_gdn_dt_bias, v_gdn_norm, v_pool_w, v_pool_scale,
                                   v_od_w_in, v_od_w_out, v_att_q_norm, v_att_k_norm, v_ffn_w_up, v_ffn_conv_w,
                                   v_ffn_conv_b, v_ffn_w_down)))
    mx, my, mc = _place()
    chip = 2 * mx + my
    T, D = x.shape[1], x.shape[2]
    depth = ada_w.shape[0]
    ada_cols = ada_w.shape[2]

    buf, offs = _pack([c, gdn_conv_w, ffn_conv_w])
    gathered = all_gather8(buf, "gather_small")
    c_all = _unpack(gathered, offs[0], (D,))
    by_chip = gathered[0::2]
    gdn_conv_full = jnp.concatenate(list(_unpack(by_chip, offs[1], gdn_conv_w.shape)), axis=-1)
    ffn_conv_full = jnp.concatenate(list(_unpack(by_chip, offs[2], ffn_conv_w.shape)), axis=-1)

    bias = lax.dynamic_slice_in_dim(ada_b, chip * ada_cols, ada_cols, axis=1)[:, None, :]
    mod_part = ada_fwd(c_all, ada_w, bias, "ada_fwd")
    mod_all = all_gather8(mod_part, "gather_mod")[0::2]
    mod_all = mod_all.transpose(1, 2, 0, 3).reshape(depth, 8, N_CHIPS * ada_cols)
    mod = lax.dynamic_index_in_dim(mod_all, 4 * mx + 2 * my + mc, 1, keepdims=False).reshape(depth, 6, D)

    W = dict(local)
    for name in COL_SHARDED + ROW_SHARDED:
        W[name] = _gather_weight(name, local[name], mc)
    W["gdn_conv_w"], W["ffn_conv_w"] = gdn_conv_full, ffn_conv_full

    sq, dx, dmod, grads = local_step(x[0], loss_target[0], mod, W)
    loss = lax.psum(0.5 * jnp.sum(sq) / D, ("x", "y", "c"))

    def stacked(name):
        return jnp.stack([g[name] for g in grads if name in g])

    full = {name: stacked(name) for name in WEIGHTS if name not in ("ada_w", "ada_b")}
    grad = {}
    for name in COL_SHARDED + ROW_SHARDED:
        grad[name] = _reduce_weight_grad(name, full[name], local[name].shape, mc)

    small = ["norm_mix", "norm_ffn", "gdn_conv_w", "gdn_a_log", "gdn_dt_bias", "gdn_norm", "pool_w",
             "pool_scale", "att_q_norm", "att_k_norm", "ffn_conv_w", "ffn_conv_b"]
    buf, offs = _pack([dmod] + [full[name] for name in small])
    gathered = all_gather8(buf, "gather_small_grads")
    summed = sum_slots(gathered, "sum_small_grads")
    grad["ada_b"] = _unpack(summed, offs[0], ada_b.shape)
    for k, name in enumerate(small):
        grad[name] = _unpack(summed, offs[1 + k], full[name].shape)
    for name, cols in (("gdn_conv_w", gdn_conv_w.shape[-1]), ("ffn_conv_w", ffn_conv_w.shape[-1])):
        grad[name] = lax.dynamic_slice_in_dim(grad[name], chip * cols, cols, axis=2)

    dmod_all = _unpack(gathered, offs[0], (depth, N_CHIPS * ada_cols))
    dmod_mine = lax.dynamic_slice_in_dim(dmod_all, chip * ada_cols, ada_cols, axis=2).transpose(1, 0, 2)
    grad["ada_w"] = ada_bwd(jnp.pad(c_all, ((0, 8), (0, 0))), jnp.pad(dmod_mine, ((0, 0), (0, 8), (0, 0))),
                            "ada_bwd")

    deltas, new_m, new_v = {}, {}, {}
    for name in WEIGHTS:
        deltas[name], new_m[name], new_v[name] = adamw(local[name], grad[name], moments_m[name],
                                                       moments_v[name], f"adamw_{name}")
    return (loss, dx[None], *[grad[n] for n in WEIGHTS], *[deltas[n] for n in WEIGHTS],
            *[new_m[n] for n in WEIGHTS], *[new_v[n] for n in WEIGHTS])
```

```python
import functools
import math

import jax
import jax.numpy as jnp
from jax import lax
from jax.experimental import pallas as pl
from jax.experimental.pallas import tpu as pltpu

F32 = jnp.float32
BF16 = jnp.bfloat16
LANE = 128
SUBLANE = 8
VMEM_LIMIT = 56 * 1024 * 1024
MESH = pl.DeviceIdType.MESH

RMS_EPS = 1e-6
GDN_H = 4
HD = 128
GDN_CHUNK = 64
GDN_STEP = 2
GDN_CONV = 4
FFN_CONV = 3
POOL_G = 4
ATT_H = 8
ATT_BLK = 128
DIL = (1, 4, 16)
EVEN_COLS = 2568
EVEN_PAD = 2688
ADAM_LR, ADAM_B1, ADAM_B2, ADAM_EPS, ADAM_WD, ADAM_STEP = 0.001, 0.9, 0.999, 1e-08, 0.01, 10
NEG = -1e30

NN = (((1,), (0,)), ((), ()))
NT = (((1,), (1,)), ((), ()))
TN = (((0,), (0,)), ((), ()))
BNN = (((2,), (1,)), ((0,), (0,)))
BNT = (((2,), (2,)), ((0,), (0,)))
BTN = (((1,), (1,)), ((0,), (0,)))


def _params(n_grid):
    return pltpu.CompilerParams(dimension_semantics=("arbitrary",) * n_grid,
                                vmem_limit_bytes=VMEM_LIMIT)


def _tile(n, target):
    if n <= target:
        return n
    best = None
    for t in range(LANE, target + 1, LANE):
        if n % t == 0:
            best = t
    assert best is not None, (n, target)
    return best


def _rows(n, target):
    if n <= target:
        return n
    best = None
    for t in range(16, target + 1, 16):
        if n % t == 0:
            best = t
    assert best is not None, (n, target)
    return best


def _bdot(a, b, dims):
    return lax.dot_general(a.astype(BF16), b.astype(BF16), dims, preferred_element_type=F32)


def _split(a):
    hi = a.astype(BF16)
    return hi, (a - hi.astype(F32)).astype(BF16)


def _dot3(a, b, dims):
    ah, al = _split(a)
    bh, bl = _split(b)
    d = lambda p, q: lax.dot_general(p, q, dims, preferred_element_type=F32)
    return d(ah, bh) + d(ah, bl) + d(al, bh)


def _sigmoid(x):
    return 1.0 / (1.0 + jnp.exp(-x))


def _silu(x):
    return x * _sigmoid(x)


def matmul(a, b, mode, out_dtype, name, tm=512, tn=1536, tk=1536):
    if mode == "nn":
        (M, K), (K2, N) = a.shape, b.shape
    elif mode == "nt":
        (M, K), (N, K2) = a.shape, b.shape
    else:
        (K, M), (K2, N) = a.shape, b.shape
    assert K == K2, (a.shape, b.shape, mode)
    tm, tn, tk = _tile(M, tm), _tile(N, tn), _tile(K, tk)
    nk = K // tk
    dims = {"nn": NN, "nt": NT, "tn": TN}[mode]
    if mode == "tn":
        a_spec = pl.BlockSpec((tk, tm), lambda i, j, k: (k, i))
    else:
        a_spec = pl.BlockSpec((tm, tk), lambda i, j, k: (i, k))
    if mode == "nt":
        b_spec = pl.BlockSpec((tn, tk), lambda i, j, k: (j, k))
    else:
        b_spec = pl.BlockSpec((tk, tn), lambda i, j, k: (k, j))

    def body(a_ref, b_ref, o_ref, acc_ref):
        k = pl.program_id(2)
        p = _bdot(a_ref[...], b_ref[...], dims)

        @pl.when(k == 0)
        def _():
            acc_ref[...] = p

        @pl.when(k > 0)
        def _():
            acc_ref[...] += p

        @pl.when(k == nk - 1)
        def _():
            o_ref[...] = acc_ref[...].astype(out_dtype)

    return pl.pallas_call(
        body, name=name, grid=(M // tm, N // tn, nk),
        in_specs=[a_spec, b_spec],
        out_specs=pl.BlockSpec((tm, tn), lambda i, j, k: (i, j)),
        out_shape=jax.ShapeDtypeStruct((M, N), out_dtype),
        scratch_shapes=[pltpu.VMEM((tm, tn), F32)],
        compiler_params=_params(3))(a, b)


def _row_spec(d):
    return pl.BlockSpec((1, d), lambda i: (0, 0))


def modnorm_fwd(x, gain, sc, sh, name):
    T, D = x.shape
    tb = _rows(T, 512)

    def body(x_ref, g_ref, sc_ref, sh_ref, o_ref):
        xv = x_ref[...]
        r = lax.rsqrt(jnp.mean(xv * xv, axis=-1, keepdims=True) + RMS_EPS)
        o_ref[...] = ((xv * r) * g_ref[...] * (1.0 + sc_ref[...]) + sh_ref[...]).astype(BF16)

    blk = pl.BlockSpec((tb, D), lambda i: (i, 0))
    return pl.pallas_call(
        body, name=name, grid=(T // tb,),
        in_specs=[blk, _row_spec(D), _row_spec(D), _row_spec(D)],
        out_specs=blk, out_shape=jax.ShapeDtypeStruct((T, D), BF16),
        compiler_params=_params(1))(x, gain, sc, sh)


def modnorm_bwd(x, gain, sc, dh, dres, name):
    T, D = x.shape
    tb = _rows(T, 512)

    def body(x_ref, g_ref, sc_ref, dh_ref, dres_ref, dx_ref, dg_ref, dsc_ref, dsh_ref):
        i = pl.program_id(0)
        xv = x_ref[...]
        r = lax.rsqrt(jnp.mean(xv * xv, axis=-1, keepdims=True) + RMS_EPS)
        n = xv * r
        dhv = dh_ref[...].astype(F32)
        gain_v, sc1 = g_ref[...], 1.0 + sc_ref[...]
        dn = dhv * (gain_v * sc1)
        dx_ref[...] = r * (dn - n * jnp.mean(dn * n, axis=-1, keepdims=True)) + dres_ref[...]
        dhn = dhv * n

        @pl.when(i == 0)
        def _():
            dg_ref[...] = jnp.zeros_like(dg_ref)
            dsc_ref[...] = jnp.zeros_like(dsc_ref)
            dsh_ref[...] = jnp.zeros_like(dsh_ref)

        dg_ref[...] += jnp.sum(dhn * sc1, axis=0, keepdims=True)
        dsc_ref[...] += jnp.sum(dhn * gain_v, axis=0, keepdims=True)
        dsh_ref[...] += jnp.sum(dhv, axis=0, keepdims=True)

    blk = pl.BlockSpec((tb, D), lambda i: (i, 0))
    row = jax.ShapeDtypeStruct((1, D), F32)
    return pl.pallas_call(
        body, name=name, grid=(T // tb,),
        in_specs=[blk, _row_spec(D), _row_spec(D), blk, blk],
        out_specs=[blk, _row_spec(D), _row_spec(D), _row_spec(D)],
        out_shape=[jax.ShapeDtypeStruct((T, D), F32), row, row, row],
        compiler_params=_params(1))(x, gain, sc, dh, dres)


def gres_fwd(x, g, y, name):
    T, D = x.shape
    tb = _rows(T, 512)

    def body(x_ref, g_ref, y_ref, o_ref):
        o_ref[...] = x_ref[...] + g_ref[...] * y_ref[...]

    blk = pl.BlockSpec((tb, D), lambda i: (i, 0))
    return pl.pallas_call(
        body, name=name, grid=(T // tb,), in_specs=[blk, _row_spec(D), blk], out_specs=blk,
        out_shape=jax.ShapeDtypeStruct((T, D), F32), compiler_params=_params(1))(x, g, y)


def gres_bwd(dx, g, y, name):
    T, D = dx.shape
    tb = _rows(T, 512)

    def body(dx_ref, g_ref, y_ref, dy_ref, dg_ref):
        i = pl.program_id(0)
        dxv = dx_ref[...]
        dy_ref[...] = (dxv * g_ref[...]).astype(BF16)

        @pl.when(i == 0)
        def _():
            dg_ref[...] = jnp.zeros_like(dg_ref)

        dg_ref[...] += jnp.sum(dxv * y_ref[...], axis=0, keepdims=True)

    blk = pl.BlockSpec((tb, D), lambda i: (i, 0))
    return pl.pallas_call(
        body, name=name, grid=(T // tb,), in_specs=[blk, _row_spec(D), blk],
        out_specs=[blk, _row_spec(D)],
        out_shape=[jax.ShapeDtypeStruct((T, D), BF16), jax.ShapeDtypeStruct((1, D), F32)],
        compiler_params=_params(1))(dx, g, y)


def loss_head(y, target, name):
    T, D = y.shape
    tb = _rows(T, 512)

    def body(y_ref, t_ref, l_ref, dy_ref):
        i = pl.program_id(0)
        err = y_ref[...] - t_ref[...]
        dy_ref[...] = err * (1.0 / D)

        @pl.when(i == 0)
        def _():
            l_ref[...] = jnp.zeros_like(l_ref)

        sq = jnp.sum(err * err, axis=0, keepdims=True)
        tot = sq[:, 0:LANE]
        for k in range(1, D // LANE):
            tot = tot + sq[:, k * LANE:(k + 1) * LANE]
        l_ref[...] += tot

    blk = pl.BlockSpec((tb, D), lambda i: (i, 0))
    return pl.pallas_call(
        body, name=name, grid=(T // tb,), in_specs=[blk, blk],
        out_specs=[_row_spec(LANE), blk],
        out_shape=[jax.ShapeDtypeStruct((1, LANE), F32), jax.ShapeDtypeStruct((T, D), F32)],
        compiler_params=_params(1))(y, target)


def _back(ext, s):
    return ext if s == 0 else pltpu.roll(ext, s, 0)


def _ahead(ext, s):
    return ext if s == 0 else pltpu.roll(ext, ext.shape[0] - s, 0)


def _halo_prev(tb, h):
    return lambda i, j: (jnp.maximum(i * (tb // h) - 1, 0), j)


def _halo_next(tb, h, nrb):
    return lambda i, j: (jnp.minimum(i + 1, nrb - 1) * (tb // h), j)


FFN_TB, FFN_CB = 256, 1408


def ffn_mid_fwd(up, conv_w8, conv_b, name):
    T, F2 = up.shape
    Fd = F2 // 2
    tb, cb = _rows(T, FFN_TB), _tile(Fd, FFN_CB)
    ncb = Fd // cb

    def body(g_ref, gp_ref, v_ref, w_ref, b_ref, o_ref):
        i = pl.program_id(0)
        g = g_ref[...]
        prev = jnp.where(i > 0, gp_ref[...], 0.0)
        ext = jnp.concatenate([prev, g], axis=0)
        w = w_ref[...]
        gc = (w[2:3] * g + w[1:2] * _back(ext, 1)[SUBLANE:] + w[0:1] * _back(ext, 2)[SUBLANE:]
              + b_ref[...])
        o_ref[...] = (_silu(gc) * v_ref[...]).astype(BF16)

    return pl.pallas_call(
        body, name=name, grid=(T // tb, ncb),
        in_specs=[pl.BlockSpec((tb, cb), lambda i, j: (i, j)),
                  pl.BlockSpec((SUBLANE, cb), _halo_prev(tb, SUBLANE)),
                  pl.BlockSpec((tb, cb), lambda i, j: (i, j + ncb)),
                  pl.BlockSpec((SUBLANE, cb), lambda i, j: (0, j)),
                  pl.BlockSpec((1, cb), lambda i, j: (0, j))],
        out_specs=pl.BlockSpec((tb, cb), lambda i, j: (i, j)),
        out_shape=jax.ShapeDtypeStruct((T, Fd), BF16),
        compiler_params=_params(2))(up, up, up, conv_w8, conv_b)


def ffn_mid_bwd(up, conv_w8, conv_b, dact, name):
    T, F2 = up.shape
    Fd = F2 // 2
    tb, cb = _rows(T, FFN_TB), _tile(Fd, FFN_CB)
    ncb, nrb = Fd // cb, T // tb
    H = SUBLANE

    def body(g_ref, gp_ref, gn_ref, v_ref, vn_ref, d_ref, dn_ref, w_ref, b_ref,
             dg_ref, dv_ref, dw_ref, db_ref):
        i = pl.program_id(1)
        g = g_ref[...]
        prev = jnp.where(i > 0, gp_ref[...], 0.0)
        ext = jnp.concatenate([prev, g, gn_ref[...]], axis=0)
        w = w_ref[...]
        e1, e2 = _back(ext, 1), _back(ext, 2)
        gc = (w[2:3] * ext + w[1:2] * e1 + w[0:1] * e2 + b_ref[...])[H:]
        val = jnp.concatenate([v_ref[...], vn_ref[...]], axis=0)
        dnext = jnp.where(i < nrb - 1, dn_ref[...], 0.0)
        da = jnp.concatenate([d_ref[...], dnext], axis=0)
        sg = _sigmoid(gc)
        dv_ref[...] = (da * gc * sg)[:tb].astype(BF16)
        dgc = da * val * (sg * (1.0 + gc * (1.0 - sg)))
        dg_ref[...] = (w[2:3] * dgc + w[1:2] * _ahead(dgc, 1) + w[0:1] * _ahead(dgc, 2))[:tb].astype(BF16)
        dc = dgc[:tb]

        @pl.when(i == 0)
        def _():
            dw_ref[...] = jnp.zeros_like(dw_ref)
            db_ref[...] = jnp.zeros_like(db_ref)

        dw_ref[2:3, :] += jnp.sum(dc * g, axis=0, keepdims=True)
        dw_ref[1:2, :] += jnp.sum(dc * e1[H:H + tb], axis=0, keepdims=True)
        dw_ref[0:1, :] += jnp.sum(dc * e2[H:H + tb], axis=0, keepdims=True)
        db_ref[...] += jnp.sum(dc, axis=0, keepdims=True)

    cur = lambda j, i: (i, j)
    prv = lambda j, i: _halo_prev(tb, H)(i, j)
    nxt = lambda j, i: _halo_next(tb, H, nrb)(i, j)
    return pl.pallas_call(
        body, name=name, grid=(ncb, nrb),
        in_specs=[pl.BlockSpec((tb, cb), cur), pl.BlockSpec((H, cb), prv), pl.BlockSpec((H, cb), nxt),
                  pl.BlockSpec((tb, cb), lambda j, i: (i, j + ncb)),
                  pl.BlockSpec((H, cb), lambda j, i: (jnp.minimum(i + 1, nrb - 1) * (tb // H), j + ncb)),
                  pl.BlockSpec((tb, cb), cur), pl.BlockSpec((H, cb), nxt),
                  pl.BlockSpec((SUBLANE, cb), lambda j, i: (0, j)),
                  pl.BlockSpec((1, cb), lambda j, i: (0, j))],
        out_specs=[pl.BlockSpec((tb, cb), cur), pl.BlockSpec((tb, cb), cur),
                   pl.BlockSpec((SUBLANE, cb), lambda j, i: (0, j)),
                   pl.BlockSpec((1, cb), lambda j, i: (0, j))],
        out_shape=[jax.ShapeDtypeStruct((T, Fd), BF16), jax.ShapeDtypeStruct((T, Fd), BF16),
                   jax.ShapeDtypeStruct((SUBLANE, Fd), F32), jax.ShapeDtypeStruct((1, Fd), F32)],
        compiler_params=_params(2))(up, up, up, up, up, dact, dact, conv_w8, conv_b)


GDN_W = GDN_H * HD


def _head_l2norm(a, apply):
    parts = []
    for h in range(GDN_H):
        ah = a[:, h * HD:(h + 1) * HD]
        parts.append(ah * lax.rsqrt(jnp.sum(ah * ah, axis=-1, keepdims=True) + RMS_EPS))
    return jnp.where(apply, jnp.concatenate(parts, axis=1), a)


def _head_l2norm_bwd(a, dy, apply):
    parts = []
    for h in range(GDN_H):
        sl = slice(h * HD, (h + 1) * HD)
        ah, dh = a[:, sl], dy[:, sl]
        r = lax.rsqrt(jnp.sum(ah * ah, axis=-1, keepdims=True) + RMS_EPS)
        y = ah * r
        parts.append(r * (dh - y * jnp.sum(dh * y, axis=-1, keepdims=True)))
    return jnp.where(apply, jnp.concatenate(parts, axis=1), dy)


def gdn_conv_fwd(proj, w8, name):
    T = proj.shape[0]
    tb = _rows(T, 512)
    H = SUBLANE

    def body(x_ref, xp_ref, w_ref, o_ref):
        i, j = pl.program_id(0), pl.program_id(1)
        x = x_ref[...]
        prev = jnp.where(i > 0, xp_ref[...], 0.0)
        ext = jnp.concatenate([prev, x], axis=0)
        w = w_ref[...]
        c = (w[3:4] * x + w[2:3] * _back(ext, 1)[H:] + w[1:2] * _back(ext, 2)[H:]
             + w[0:1] * _back(ext, 3)[H:])
        o_ref[...] = _head_l2norm(_silu(c), j < 2)

    return pl.pallas_call(
        body, name=name, grid=(T // tb, 3),
        in_specs=[pl.BlockSpec((tb, GDN_W), lambda i, j: (i, j)),
                  pl.BlockSpec((H, GDN_W), _halo_prev(tb, H)),
                  pl.BlockSpec((SUBLANE, GDN_W), lambda i, j: (0, j))],
        out_specs=pl.BlockSpec((tb, GDN_W), lambda i, j: (i, j)),
        out_shape=jax.ShapeDtypeStruct((T, 3 * GDN_W), F32),
        compiler_params=_params(2))(proj, proj, w8)


def gdn_conv_bwd(proj, w8, dout, name):
    T = proj.shape[0]
    tb = _rows(T, 512)
    nrb = T // tb
    H = SUBLANE

    def body(x_ref, xp_ref, xn_ref, d_ref, dn_ref, w_ref, dx_ref, dw_ref):
        j, i = pl.program_id(0), pl.program_id(1)
        x = x_ref[...]
        prev = jnp.where(i > 0, xp_ref[...], 0.0)
        ext = jnp.concatenate([prev, x, xn_ref[...]], axis=0)
        w = w_ref[...]
        e1, e2, e3 = _back(ext, 1), _back(ext, 2), _back(ext, 3)
        c = (w[3:4] * ext + w[2:3] * e1 + w[1:2] * e2 + w[0:1] * e3)[H:]
        sg = _sigmoid(c)
        dnext = jnp.where(i < nrb - 1, dn_ref[...], 0.0)
        do = jnp.concatenate([d_ref[...], dnext], axis=0)
        da = _head_l2norm_bwd(c * sg, do, j < 2)
        dc = da * (sg * (1.0 + c * (1.0 - sg)))
        dx_ref[...] = (w[3:4] * dc + w[2:3] * _ahead(dc, 1) + w[1:2] * _ahead(dc, 2)
                       + w[0:1] * _ahead(dc, 3))[:tb].astype(BF16)
        dcc = dc[:tb]

        @pl.when(i == 0)
        def _():
            dw_ref[...] = jnp.zeros_like(dw_ref)

        dw_ref[3:4, :] += jnp.sum(dcc * x, axis=0, keepdims=True)
        dw_ref[2:3, :] += jnp.sum(dcc * e1[H:H + tb], axis=0, keepdims=True)
        dw_ref[1:2, :] += jnp.sum(dcc * e2[H:H + tb], axis=0, keepdims=True)
        dw_ref[0:1, :] += jnp.sum(dcc * e3[H:H + tb], axis=0, keepdims=True)

    cur = lambda j, i: (i, j)
    prv = lambda j, i: _halo_prev(tb, H)(i, j)
    nxt = lambda j, i: _halo_next(tb, H, nrb)(i, j)
    return pl.pallas_call(
        body, name=name, grid=(3, nrb),
        in_specs=[pl.BlockSpec((tb, GDN_W), cur), pl.BlockSpec((H, GDN_W), prv), pl.BlockSpec((H, GDN_W), nxt),
                  pl.BlockSpec((tb, GDN_W), cur), pl.BlockSpec((H, GDN_W), nxt),
                  pl.BlockSpec((SUBLANE, GDN_W), lambda j, i: (0, j))],
        out_specs=[pl.BlockSpec((tb, GDN_W), cur), pl.BlockSpec((SUBLANE, GDN_W), lambda j, i: (0, j))],
        out_shape=[jax.ShapeDtypeStruct((T, 3 * GDN_W), BF16),
                   jax.ShapeDtypeStruct((SUBLANE, 3 * GDN_W), F32)],
        compiler_params=_params(2))(proj, proj, proj, dout, dout, w8)


def _dot_family(dot, diff):
    if not diff:
        return tuple(functools.partial(lambda d, a, b: dot(a, b, d), d) for d in (BNN, BNT, BTN))

    @jax.custom_vjp
    def nn(a, b):
        return dot(a, b, BNN)
    nn.defvjp(lambda a, b: (dot(a, b, BNN), (a, b)),
              lambda res, g: (dot(g, res[1], BNT), dot(res[0], g, BTN)))

    @jax.custom_vjp
    def nt(a, b):
        return dot(a, b, BNT)
    nt.defvjp(lambda a, b: (dot(a, b, BNT), (a, b)),
              lambda res, g: (dot(g, res[1], BNN), dot(g, res[0], BTN)))

    @jax.custom_vjp
    def tn(a, b):
        return dot(a, b, BTN)
    tn.defvjp(lambda a, b: (dot(a, b, BTN), (a, b)),
              lambda res, g: (dot(res[1], g, BNT), dot(res[0], g, BNN)))
    return nn, nt, tn


def _gdn_step(dots, hdots, S, q, k, v, z, b_raw, a_raw, alog, dtb, gnorm):
    nn, nt, tn = dots
    hnn = hdots[0]
    B, C = q.shape[0], GDN_CHUNK
    ii = lax.broadcasted_iota(jnp.int32, (B, C, C), 1)
    jj = lax.broadcasted_iota(jnp.int32, (B, C, C), 2)
    causal, strict = ii >= jj, ii > jj
    tri, tri_t = causal.astype(F32), (ii <= jj).astype(F32)
    eye, ones = (ii == jj).astype(F32), jnp.ones((B, C, C), F32)

    beta = _sigmoid(b_raw)
    xs = a_raw + dtb
    pos = xs > 0.0
    softplus = jnp.where(pos, xs, 0.0) + jnp.log(1.0 + jnp.exp(jnp.where(pos, -xs, xs)))
    g = -jnp.exp(alog) * softplus
    gb = jnp.broadcast_to(g, (B, C, C))
    gc_c = hnn(tri, gb)
    gc_r = hnn(hnn(ones, eye * gb), tri_t)
    gc = hnn(tri, jnp.broadcast_to(g, (B, C, HD)))
    gl = jnp.sum(g, axis=1, keepdims=True)
    decay = jnp.where(causal, jnp.exp(jnp.where(causal, gc_c - gc_r, 0.0)), 0.0)
    q = q * (HD ** -0.5)
    kb = k * beta
    L = jnp.where(strict, nt(kb, k) * decay, 0.0)
    egc = jnp.exp(gc)
    P = eye - L
    M = hnn(L, L)
    for step in range(5):
        P = P + hnn(P, M)
        if step < 4:
            M = hnn(M, M)
    u = hnn(P, v * beta)
    w = hnn(P, kb * egc)
    intra = jnp.where(causal, nt(q, k) * decay, 0.0)
    qg = q * egc
    kdec = k * jnp.exp(gl - gc)
    egl = jnp.exp(gl)
    outs = []
    for ci in range(B // GDN_H):
        sl = slice(ci * GDN_H, (ci + 1) * GDN_H)
        v_new = u[sl] - nn(w[sl], S)
        outs.append(nn(qg[sl], S) + nn(intra[sl], v_new))
        S = S * egl[sl] + tn(kdec[sl], v_new)
    o = jnp.concatenate(outs, axis=0)
    r = lax.rsqrt(jnp.mean(o * o, axis=-1, keepdims=True) + RMS_EPS)
    return o * r * gnorm * _silu(z), S


def _gdn_batches(qkv, ba, z, alog_row, dt_row):
    C = GDN_CHUNK
    q, k, v, zz, b_raw, a_raw, alog, dtb = ([] for _ in range(8))
    for ci in range(GDN_STEP):
        rows = slice(ci * C, (ci + 1) * C)
        for h in range(GDN_H):
            q.append(qkv[rows, h * HD:(h + 1) * HD])
            k.append(qkv[rows, GDN_W + h * HD:GDN_W + (h + 1) * HD])
            v.append(qkv[rows, 2 * GDN_W + h * HD:2 * GDN_W + (h + 1) * HD])
            zz.append(z[rows, h * HD:(h + 1) * HD])
            b_raw.append(ba[rows, h:h + 1])
            a_raw.append(ba[rows, GDN_H + h:GDN_H + h + 1])
            alog.append(alog_row[:, h:h + 1])
            dtb.append(dt_row[:, h:h + 1])
    return tuple(jnp.stack(t) for t in (q, k, v, zz, b_raw, a_raw, alog, dtb))


def gdn_chunk_fwd(qkv, proj, alog_row, dt_row, gnorm, name):
    T = qkv.shape[0]
    R = GDN_CHUNK * GDN_STEP
    N = T // R
    dots, hdots = _dot_family(_bdot, False), _dot_family(_dot3, False)

    def body(qkv_ref, ba_ref, z_ref, al_ref, dt_ref, gn_ref, o_ref, save_ref, S_ref):
        n = pl.program_id(0)

        @pl.when(n == 0)
        def _():
            S_ref[...] = jnp.zeros_like(S_ref)

        S = S_ref[...]
        save_ref[0] = S
        batches = _gdn_batches(qkv_ref[...], ba_ref[...], z_ref[...], al_ref[...], dt_ref[...])
        o, S_new = _gdn_step(dots, hdots, S, *batches, gn_ref[...])
        S_ref[...] = S_new
        for ci in range(GDN_STEP):
            for h in range(GDN_H):
                o_ref[ci * GDN_CHUNK:(ci + 1) * GDN_CHUNK, h * HD:(h + 1) * HD] = o[ci * GDN_H + h].astype(BF16)

    return pl.pallas_call(
        body, name=name, grid=(N,),
        in_specs=[pl.BlockSpec((R, 3 * GDN_W), lambda n: (n, 0)),
                  pl.BlockSpec((R, LANE), lambda n: (n, (4 * GDN_W + POOL_G * HD) // LANE)),
                  pl.BlockSpec((R, GDN_W), lambda n: (n, 3)),
                  _row_spec(LANE), _row_spec(LANE), _row_spec(HD)],
        out_specs=[pl.BlockSpec((R, GDN_W), lambda n: (n, 0)),
                   pl.BlockSpec((1, GDN_H, HD, HD), lambda n: (n, 0, 0, 0))],
        out_shape=[jax.ShapeDtypeStruct((T, GDN_W), BF16), jax.ShapeDtypeStruct((N, GDN_H, HD, HD), F32)],
        scratch_shapes=[pltpu.VMEM((GDN_H, HD, HD), F32)],
        compiler_params=_params(1))(qkv, proj, proj, alog_row, dt_row, gnorm)


def gdn_chunk_bwd(qkv, proj, alog_row, dt_row, gnorm, saved, docat, name):
    T = qkv.shape[0]
    C = GDN_CHUNK
    R = C * GDN_STEP
    N = T // R
    dots, hdots = _dot_family(_bdot, True), _dot_family(_dot3, True)

    def body(qkv_ref, ba_ref, z_ref, al_ref, dt_ref, gn_ref, save_ref, do_ref,
             dqkv_ref, dz_ref, dba_ref, dal_ref, ddt_ref, dgn_ref, dS_ref):
        n = pl.program_id(0)

        @pl.when(n == 0)
        def _():
            dS_ref[...] = jnp.zeros_like(dS_ref)
            dal_ref[...] = jnp.zeros_like(dal_ref)
            ddt_ref[...] = jnp.zeros_like(ddt_ref)
            dgn_ref[...] = jnp.zeros_like(dgn_ref)

        batches = _gdn_batches(qkv_ref[...], ba_ref[...], z_ref[...], al_ref[...], dt_ref[...])
        do = do_ref[...]
        do_b = jnp.stack([do[ci * C:(ci + 1) * C, h * HD:(h + 1) * HD]
                          for ci in range(GDN_STEP) for h in range(GDN_H)])
        fn = functools.partial(_gdn_step, dots, hdots)
        _, vjp = jax.vjp(fn, save_ref[0], *batches, gn_ref[...])
        dS, dq, dk, dv, dz, db_raw, da_raw, dalog, ddtb, dgn = vjp((do_b, dS_ref[...]))
        dS_ref[...] = dS
        lane = lax.broadcasted_iota(jnp.int32, (1, LANE), 1)
        dal = jnp.zeros((1, LANE), F32)
        ddt = jnp.zeros((1, LANE), F32)
        for ci in range(GDN_STEP):
            rows = slice(ci * C, (ci + 1) * C)
            dba = jnp.zeros((C, LANE), F32)
            for h in range(GDN_H):
                b = ci * GDN_H + h
                dqkv_ref[rows, h * HD:(h + 1) * HD] = dq[b]
                dqkv_ref[rows, GDN_W + h * HD:GDN_W + (h + 1) * HD] = dk[b]
                dqkv_ref[rows, 2 * GDN_W + h * HD:2 * GDN_W + (h + 1) * HD] = dv[b]
                dz_ref[rows, h * HD:(h + 1) * HD] = dz[b].astype(BF16)
                hot_b = (lane == h).astype(F32)
                dba = dba + db_raw[b] * hot_b + da_raw[b] * (lane == GDN_H + h).astype(F32)
                dal = dal + dalog[b] * hot_b
                ddt = ddt + ddtb[b] * hot_b
            dba_ref[rows, :] = dba.astype(BF16)
        dal_ref[...] += dal
        ddt_ref[...] += ddt
        dgn_ref[...] += dgn

    rev = lambda n: N - 1 - n
    row = jax.ShapeDtypeStruct((1, LANE), F32)
    return pl.pallas_call(
        body, name=name, grid=(N,),
        in_specs=[pl.BlockSpec((R, 3 * GDN_W), lambda n: (rev(n), 0)),
                  pl.BlockSpec((R, LANE), lambda n: (rev(n), (4 * GDN_W + POOL_G * HD) // LANE)),
                  pl.BlockSpec((R, GDN_W), lambda n: (rev(n), 3)),
                  _row_spec(LANE), _row_spec(LANE), _row_spec(HD),
                  pl.BlockSpec((1, GDN_H, HD, HD), lambda n: (rev(n), 0, 0, 0)),
                  pl.BlockSpec((R, GDN_W), lambda n: (rev(n), 0))],
        out_specs=[pl.BlockSpec((R, 3 * GDN_W), lambda n: (rev(n), 0)),
                   pl.BlockSpec((R, GDN_W), lambda n: (rev(n), 0)),
                   pl.BlockSpec((R, LANE), lambda n: (rev(n), 0)),
                   _row_spec(LANE), _row_spec(LANE), _row_spec(HD)],
        out_shape=[jax.ShapeDtypeStruct((T, 3 * GDN_W), F32), jax.ShapeDtypeStruct((T, GDN_W), BF16),
                   jax.ShapeDtypeStruct((T, LANE), BF16), row, row, jax.ShapeDtypeStruct((1, HD), F32)],
        scratch_shapes=[pltpu.VMEM((GDN_H, HD, HD), F32)],
        compiler_params=_params(1))(qkv, proj, proj, alog_row, dt_row, gnorm, saved, docat)


POOL_HALO = 16


def _pool_pick(j, s2, s4, s8, s16):
    return jnp.where(j == 0, s2, jnp.where(j == 1, s4, jnp.where(j == 2, s8, s16)))


def _pool_count(j, t0, rows):
    t1 = (t0 + 1 + lax.broadcasted_iota(jnp.int32, (rows, 1), 0)).astype(F32)
    win = jnp.where(j == 0, 2.0, jnp.where(j == 1, 4.0, jnp.where(j == 2, 8.0, 16.0)))
    return jnp.minimum(t1, win)


def _pooled(p, prev, i, j, tb):
    ext = jnp.concatenate([prev, p], axis=0)
    s2 = ext + _back(ext, 1)
    s4 = s2 + _back(s2, 2)
    s8 = s4 + _back(s4, 4)
    s16 = s8 + _back(s8, 8)
    s = _pool_pick(j, s2, s4, s8, s16)[POOL_HALO:]
    return s / _pool_count(j, i * tb, tb) - p


def pool_fwd(proj, pool_w, pool_scale, name):
    T = proj.shape[0]
    tb = _rows(T, 512)
    c0 = 4 * GDN_H

    def body(p_ref, pp_ref, w_ref, s_ref, o_ref):
        i, j = pl.program_id(0), pl.program_id(1)
        p = p_ref[...]
        prev = jnp.where(i > 0, pp_ref[...], 0.0)
        pooled = _pooled(p, prev, i, j, tb)
        o_ref[...] = (_bdot(pooled, w_ref[0], NN) * s_ref[...]).astype(BF16)

    return pl.pallas_call(
        body, name=name, grid=(T // tb, POOL_G),
        in_specs=[pl.BlockSpec((tb, HD), lambda i, j: (i, c0 + j)),
                  pl.BlockSpec((POOL_HALO, HD), lambda i, j: (jnp.maximum(i * (tb // POOL_HALO) - 1, 0), c0 + j)),
                  pl.BlockSpec((1, HD, HD), lambda i, j: (j, 0, 0)),
                  pl.BlockSpec((1, HD), lambda i, j: (0, j))],
        out_specs=pl.BlockSpec((tb, HD), lambda i, j: (i, j)),
        out_shape=jax.ShapeDtypeStruct((T, POOL_G * HD), BF16),
        compiler_params=_params(2))(proj, proj, pool_w, pool_scale)


def pool_bwd(proj, pool_w, pool_scale, docat, name):
    T = proj.shape[0]
    tb = _rows(T, 512)
    nrb = T // tb
    c0 = 4 * GDN_H
    HB = POOL_HALO

    def body(p_ref, pp_ref, w_ref, s_ref, d_ref, dn_ref, dp_ref, dw_ref, ds_ref):
        j, i = pl.program_id(0), pl.program_id(1)
        p = p_ref[...]
        prev = jnp.where(i > 0, pp_ref[...], 0.0)
        pooled = _pooled(p, prev, i, j, tb)
        w, scale = w_ref[0], s_ref[...]
        dy = d_ref[...]
        dnext = jnp.where(i < nrb - 1, dn_ref[...], 0.0)
        dyp = jnp.concatenate([dy, dnext], axis=0) * scale
        dpooled = _bdot(dyp, w, NT)
        qn = dpooled / _pool_count(j, i * tb, tb + HB)
        a2 = qn + _ahead(qn, 1)
        a4 = a2 + _ahead(a2, 2)
        a8 = a4 + _ahead(a4, 4)
        a16 = a8 + _ahead(a8, 8)
        dp_ref[...] = (_pool_pick(j, a2, a4, a8, a16) - dpooled)[:tb].astype(BF16)

        @pl.when(i == 0)
        def _():
            dw_ref[...] = jnp.zeros_like(dw_ref)
            ds_ref[...] = jnp.zeros_like(ds_ref)

        dw_ref[0] += _bdot(pooled, dyp[:tb], TN)
        ds_ref[...] += jnp.sum(dy * _bdot(pooled, w, NN), axis=0, keepdims=True)

    return pl.pallas_call(
        body, name=name, grid=(POOL_G, nrb),
        in_specs=[pl.BlockSpec((tb, HD), lambda j, i: (i, c0 + j)),
                  pl.BlockSpec((HB, HD), lambda j, i: (jnp.maximum(i * (tb // HB) - 1, 0), c0 + j)),
                  pl.BlockSpec((1, HD, HD), lambda j, i: (j, 0, 0)),
                  pl.BlockSpec((1, HD), lambda j, i: (0, j)),
                  pl.BlockSpec((tb, HD), lambda j, i: (i, POOL_G + j)),
                  pl.BlockSpec((HB, HD), lambda j, i: (jnp.minimum(i + 1, nrb - 1) * (tb // HB), POOL_G + j))],
        out_specs=[pl.BlockSpec((tb, HD), lambda j, i: (i, j)),
                   pl.BlockSpec((1, HD, HD), lambda j, i: (j, 0, 0)),
                   pl.BlockSpec((1, HD), lambda j, i: (0, j))],
        out_shape=[jax.ShapeDtypeStruct((T, POOL_G * HD), BF16),
                   jax.ShapeDtypeStruct((POOL_G, HD, HD), F32),
                   jax.ShapeDtypeStruct((1, POOL_G * HD), F32)],
        compiler_params=_params(2))(proj, proj, pool_w, pool_scale, docat, docat)


ATT_W = ATT_H * HD
GROUP_COLS = 3 * ATT_W


def to_residue_major(t, d):
    if d == 1:
        return t
    T, C = t.shape
    return t.reshape(T // d, d, C).transpose(1, 0, 2).reshape(T, C)


def to_token_order(t, d):
    if d == 1:
        return t
    T, C = t.shape
    return t.reshape(d, T // d, C).transpose(1, 0, 2).reshape(T, C)


def headnorm_fwd(proj, qk_gain, name):
    T = proj.shape[0]
    tb = _rows(T, 256)

    def body(x_ref, g_ref, o_ref):
        g = g_ref[...]
        for h in range(2 * ATT_H):
            sl = slice(h * HD, (h + 1) * HD)
            x = x_ref[:, sl]
            n = x * lax.rsqrt(jnp.mean(x * x, axis=-1, keepdims=True) + RMS_EPS)
            gain = g[0:1] * (HD ** -0.5) if h < ATT_H else g[1:2]
            o_ref[:, sl] = (n * gain).astype(BF16)
        o_ref[:, 2 * ATT_W:] = x_ref[:, 2 * ATT_W:].astype(BF16)

    blk = pl.BlockSpec((tb, GROUP_COLS), lambda i: (i, 0))
    return pl.pallas_call(
        body, name=name, grid=(T // tb,),
        in_specs=[blk, pl.BlockSpec((SUBLANE, HD), lambda i: (0, 0))],
        out_specs=blk, out_shape=jax.ShapeDtypeStruct((T, GROUP_COLS), BF16),
        compiler_params=_params(1))(proj, qk_gain)


def headnorm_bwd(proj, qk_gain, dq, dk, dv, name):
    T = proj.shape[0]
    tb = _rows(T, 256)

    def body(x_ref, g_ref, dq_ref, dk_ref, dv_ref, dx_ref, dg_ref):
        i = pl.program_id(0)
        g = g_ref[...]

        @pl.when(i == 0)
        def _():
            dg_ref[...] = jnp.zeros_like(dg_ref)

        for part, d_ref in enumerate((dq_ref, dk_ref)):
            gain = g[0:1] * (HD ** -0.5) if part == 0 else g[1:2]
            scale = (HD ** -0.5) if part == 0 else 1.0
            acc = jnp.zeros((1, HD), F32)
            for h in range(ATT_H):
                x = x_ref[:, part * ATT_W + h * HD:part * ATT_W + (h + 1) * HD]
                d = d_ref[:, h * HD:(h + 1) * HD]
                r = lax.rsqrt(jnp.mean(x * x, axis=-1, keepdims=True) + RMS_EPS)
                n = x * r
                dn = d * gain
                dx = r * (dn - n * jnp.mean(dn * n, axis=-1, keepdims=True))
                dx_ref[:, part * ATT_W + h * HD:part * ATT_W + (h + 1) * HD] = dx.astype(BF16)
                acc = acc + jnp.sum(d * n, axis=0, keepdims=True)
            dg_ref[part:part + 1, :] += acc * scale
        dx_ref[:, 2 * ATT_W:] = dv_ref[...].astype(BF16)

    blk = pl.BlockSpec((tb, GROUP_COLS), lambda i: (i, 0))
    dblk = pl.BlockSpec((tb, ATT_W), lambda i: (i, 0))
    gspec = pl.BlockSpec((SUBLANE, HD), lambda i: (0, 0))
    return pl.pallas_call(
        body, name=name, grid=(T // tb,),
        in_specs=[blk, gspec, dblk, dblk, dblk],
        out_specs=[blk, gspec],
        out_shape=[jax.ShapeDtypeStruct((T, GROUP_COLS), BF16), jax.ShapeDtypeStruct((SUBLANE, HD), F32)],
        compiler_params=_params(1))(proj, qk_gain, dq, dk, dv)


def _att_scores(q, k, slope, n_ok, prev):
    a = lax.broadcasted_iota(jnp.int32, (ATT_BLK, ATT_BLK), 0)
    j = lax.broadcasted_iota(jnp.int32, (ATT_BLK, ATT_BLK), 1)
    rel = (ATT_BLK + a - j) if prev else (a - j)
    mask = ((j >= a) & n_ok) if prev else (j <= a)
    s = _bdot(q, k, NT) - slope * rel.astype(F32)
    return jnp.where(mask, s, NEG), mask


def _att_scores_t(k, q, slope, n_ok, nxt):
    j = lax.broadcasted_iota(jnp.int32, (ATT_BLK, ATT_BLK), 0)
    a = lax.broadcasted_iota(jnp.int32, (ATT_BLK, ATT_BLK), 1)
    rel = (ATT_BLK + a - j) if nxt else (a - j)
    mask = ((j >= a) & n_ok) if nxt else (j <= a)
    s = _bdot(k, q, NT) - slope * rel.astype(F32)
    return jnp.where(mask, s, NEG), mask


def _att_blocks(nb, width, shift):
    def make(col):
        return pl.BlockSpec((ATT_BLK, width),
                            lambda r, n: (r * nb + jnp.clip(n + shift, 0, nb - 1), col))
    return make


def _lane_col(cols):
    lane = lax.broadcasted_iota(jnp.int32, (1, LANE), 1)
    out = jnp.zeros((ATT_BLK, LANE), F32)
    for h, c in enumerate(cols):
        out = out + c * (lane == h).astype(F32)
    return out


def att_fwd(qkvn, gi, name):
    T = qkvn.shape[0]
    dil = DIL[gi]
    nb = T // dil // ATT_BLK

    def body(q_ref, kp_ref, kc_ref, vp_ref, vc_ref, o_ref, l_ref):
        n_ok = pl.program_id(1) > 0
        lses = []
        for h in range(ATT_H):
            sl = slice(h * HD, (h + 1) * HD)
            slope = (2.0 ** -(h + 1)) * dil
            q = q_ref[:, sl]
            s_c, _ = _att_scores(q, kc_ref[:, sl], slope, n_ok, False)
            s_p, _ = _att_scores(q, kp_ref[:, sl], slope, n_ok, True)
            m = jnp.maximum(jnp.max(s_c, axis=-1, keepdims=True), jnp.max(s_p, axis=-1, keepdims=True))
            p_c, p_p = jnp.exp(s_c - m), jnp.exp(s_p - m)
            l = jnp.sum(p_c, axis=-1, keepdims=True) + jnp.sum(p_p, axis=-1, keepdims=True)
            o = _bdot(p_c, vc_ref[:, sl], NN) + _bdot(p_p, vp_ref[:, sl], NN)
            o_ref[:, sl] = o / l
            lses.append(m + jnp.log(l))
        l_ref[...] = _lane_col(lses)

    cur, prv = _att_blocks(nb, ATT_W, 0), _att_blocks(nb, ATT_W, -1)
    return pl.pallas_call(
        body, name=name, grid=(dil, nb), in_specs=[cur(0), prv(1), cur(1), prv(2), cur(2)],
        out_specs=[cur(0), _att_blocks(nb, LANE, 0)(0)],
        out_shape=[jax.ShapeDtypeStruct((T, ATT_W), F32), jax.ShapeDtypeStruct((T, LANE), F32)],
        compiler_params=_params(2))(qkvn, qkvn, qkvn, qkvn, qkvn)


def att_merge(os, lses, name):
    T = os[0].shape[0]
    tb = _rows(T, 512)

    def body(o0, o1, o2, l0, l1, l2, o_ref, l_ref):
        a, b, c = l0[...], l1[...], l2[...]
        m = jnp.maximum(a, jnp.maximum(b, c))
        wa, wb, wc = jnp.exp(a - m), jnp.exp(b - m), jnp.exp(c - m)
        den = wa + wb + wc
        l_ref[...] = m + jnp.log(den)
        wa, wb, wc = wa / den, wb / den, wc / den
        for h in range(ATT_H):
            sl = slice(h * HD, (h + 1) * HD)
            o_ref[:, sl] = (wa[:, h:h + 1] * o0[:, sl] + wb[:, h:h + 1] * o1[:, sl]
                            + wc[:, h:h + 1] * o2[:, sl])

    blk = pl.BlockSpec((tb, ATT_W), lambda i: (i, 0))
    lblk = pl.BlockSpec((tb, LANE), lambda i: (i, 0))
    return pl.pallas_call(
        body, name=name, grid=(T // tb,), in_specs=[blk] * 3 + [lblk] * 3, out_specs=[blk, lblk],
        out_shape=[jax.ShapeDtypeStruct((T, ATT_W), F32), jax.ShapeDtypeStruct((T, LANE), F32)],
        compiler_params=_params(1))(*os, *lses)


def att_delta(do, o, name):
    T = do.shape[0]
    tb = _rows(T, 512)

    def body(d_ref, o_ref, out_ref):
        lane = lax.broadcasted_iota(jnp.int32, (1, LANE), 1)
        out = jnp.zeros((tb, LANE), F32)
        for h in range(ATT_H):
            sl = slice(h * HD, (h + 1) * HD)
            s = jnp.sum(d_ref[:, sl] * o_ref[:, sl], axis=-1, keepdims=True)
            out = out + s * (lane == h).astype(F32)
        out_ref[...] = out

    blk = pl.BlockSpec((tb, ATT_W), lambda i: (i, 0))
    return pl.pallas_call(
        body, name=name, grid=(T // tb,), in_specs=[blk, blk],
        out_specs=pl.BlockSpec((tb, LANE), lambda i: (i, 0)),
        out_shape=jax.ShapeDtypeStruct((T, LANE), F32), compiler_params=_params(1))(do, o)


def att_bwd_q(qkvn, do, lse, delta, gi, name):
    T = qkvn.shape[0]
    dil = DIL[gi]
    nb = T // dil // ATT_BLK

    def body(q_ref, kp_ref, kc_ref, vp_ref, vc_ref, do_ref, l_ref, d_ref, dq_ref):
        n_ok = pl.program_id(1) > 0
        lse, dl = l_ref[...], d_ref[...]
        for h in range(ATT_H):
            sl = slice(h * HD, (h + 1) * HD)
            slope = (2.0 ** -(h + 1)) * dil
            q, do_h = q_ref[:, sl], do_ref[:, sl]
            dq = jnp.zeros((ATT_BLK, HD), F32)
            for k_ref, v_ref, prev in ((kc_ref, vc_ref, False), (kp_ref, vp_ref, True)):
                s, mask = _att_scores(q, k_ref[:, sl], slope, n_ok, prev)
                p = jnp.where(mask, jnp.exp(s - lse[:, h:h + 1]), 0.0)
                ds = p * (_bdot(do_h, v_ref[:, sl], NT) - dl[:, h:h + 1])
                dq = dq + _bdot(ds, k_ref[:, sl], NN)
            dq_ref[:, sl] = dq

    cur, prv = _att_blocks(nb, ATT_W, 0), _att_blocks(nb, ATT_W, -1)
    small = _att_blocks(nb, LANE, 0)(0)
    return pl.pallas_call(
        body, name=name, grid=(dil, nb),
        in_specs=[cur(0), prv(1), cur(1), prv(2), cur(2), cur(0), small, small],
        out_specs=cur(0), out_shape=jax.ShapeDtypeStruct((T, ATT_W), F32),
        compiler_params=_params(2))(qkvn, qkvn, qkvn, qkvn, qkvn, do, lse, delta)


def att_bwd_kv(qkvn, do, lse, delta, gi, name):
    T = qkvn.shape[0]
    dil = DIL[gi]
    nb = T // dil // ATT_BLK

    def body(k_ref, v_ref, q0_ref, q1_ref, do0_ref, do1_ref, l0_ref, l1_ref, d0_ref, d1_ref,
             dk_ref, dv_ref):
        n_ok = pl.program_id(1) < nb - 1
        stats = ((l0_ref[...].T, d0_ref[...].T), (l1_ref[...].T, d1_ref[...].T))
        for h in range(ATT_H):
            sl = slice(h * HD, (h + 1) * HD)
            slope = (2.0 ** -(h + 1)) * dil
            k, v = k_ref[:, sl], v_ref[:, sl]
            dk = jnp.zeros((ATT_BLK, HD), F32)
            dv = jnp.zeros((ATT_BLK, HD), F32)
            for q_ref, do_ref, (lse_t, dl_t), nxt in ((q0_ref, do0_ref, stats[0], False),
                                                      (q1_ref, do1_ref, stats[1], True)):
                q, do_h = q_ref[:, sl], do_ref[:, sl]
                s, mask = _att_scores_t(k, q, slope, n_ok, nxt)
                p = jnp.where(mask, jnp.exp(s - lse_t[h:h + 1, :]), 0.0)
                dv = dv + _bdot(p, do_h, NN)
                ds = p * (_bdot(v, do_h, NT) - dl_t[h:h + 1, :])
                dk = dk + _bdot(ds, q, NN)
            dk_ref[:, sl] = dk
            dv_ref[:, sl] = dv

    cur, nxt = _att_blocks(nb, ATT_W, 0), _att_blocks(nb, ATT_W, 1)
    s0, s1 = _att_blocks(nb, LANE, 0)(0), _att_blocks(nb, LANE, 1)(0)
    return pl.pallas_call(
        body, name=name, grid=(dil, nb),
        in_specs=[cur(1), cur(2), cur(0), nxt(0), cur(0), nxt(0), s0, s1, s0, s1],
        out_specs=[cur(0), cur(0)], out_shape=[jax.ShapeDtypeStruct((T, ATT_W), F32)] * 2,
        compiler_params=_params(2))(qkvn, qkvn, qkvn, qkvn, do, do, lse, lse, delta, delta)


def adamw(w, g, m, v, name):
    shape = w.shape
    C = shape[-1]
    R = math.prod(shape[:-1])
    to2d = lambda t: t.reshape(R, C)
    tb = _rows(R, max(16, (256 * 1536 // C) // 16 * 16))
    c1 = 1.0 - ADAM_B1 ** ADAM_STEP
    c2 = 1.0 - ADAM_B2 ** ADAM_STEP

    def body(w_ref, g_ref, m_ref, v_ref, d_ref, nm_ref, nv_ref):
        gv = g_ref[...]
        nm = ADAM_B1 * m_ref[...] + (1.0 - ADAM_B1) * gv
        nv = ADAM_B2 * v_ref[...] + (1.0 - ADAM_B2) * (gv * gv)
        d_ref[...] = -ADAM_LR * ((nm / c1) / (jnp.sqrt(nv / c2) + ADAM_EPS) + ADAM_WD * w_ref[...])
        nm_ref[...] = nm
        nv_ref[...] = nv

    blk = pl.BlockSpec((tb, C), lambda i: (i, 0))
    out = jax.ShapeDtypeStruct((R, C), F32)
    d, nm, nv = pl.pallas_call(
        body, name=name, grid=(R // tb,), in_specs=[blk] * 4, out_specs=[blk] * 3,
        out_shape=[out, out, out], compiler_params=_params(1))(to2d(w), to2d(g), to2d(m), to2d(v))
    return d.reshape(shape), nm.reshape(shape), nv.reshape(shape)


def _pad_rows8(w):
    return jnp.pad(w, ((0, SUBLANE - w.shape[0]), (0, 0)))


def _lane_row(v):
    return jnp.pad(v, (0, LANE - v.shape[0]))[None, :]


def _even_reorder(w_in):
    z4 = 4 * GDN_W
    pad = jnp.zeros((w_in.shape[0], EVEN_PAD - EVEN_COLS), w_in.dtype)
    return jnp.concatenate([w_in[:, :z4], w_in[:, z4 + 2 * GDN_H:], w_in[:, z4:z4 + 2 * GDN_H], pad], axis=1)


def _even_restore(dw):
    z4 = 4 * GDN_W
    p4 = POOL_G * HD
    return jnp.concatenate([dw[:, :z4], dw[:, z4 + p4:z4 + p4 + 2 * GDN_H], dw[:, z4:z4 + p4]], axis=1)


def _ffn_fwd(tag, x1, mod_f, wl):
    sh, sc, g = mod_f
    hf = modnorm_fwd(x1, wl["norm_ffn"], sc, sh, f"{tag}_ffn_norm")
    up = matmul(hf, wl["ffn_w_up"], "nn", F32, f"{tag}_ffn_up")
    act = ffn_mid_fwd(up, wl["ffn_conv_w8"], wl["ffn_conv_b"], f"{tag}_ffn_mid")
    f = matmul(act, wl["ffn_w_down"], "nn", F32, f"{tag}_ffn_down")
    x2 = gres_fwd(x1, g, f, f"{tag}_ffn_res")
    return x2, (x1, hf, up, act, f)


def _ffn_bwd(tag, dx2, saved, mod_f, wl):
    x1, hf, up, act, f = saved
    sh, sc, g = mod_f
    df, dg = gres_bwd(dx2, g, f, f"{tag}_ffn_res_bwd")
    dact = matmul(df, wl["ffn_w_down"], "nt", F32, f"{tag}_ffn_down_da")
    dw_down = matmul(act, df, "tn", F32, f"{tag}_ffn_down_dw")
    dgate, dval, dcw, dcb = ffn_mid_bwd(up, wl["ffn_conv_w8"], wl["ffn_conv_b"], dact, f"{tag}_ffn_mid_bwd")
    dup = jnp.concatenate([dgate, dval], axis=1)
    dhf = matmul(dup, wl["ffn_w_up"], "nt", F32, f"{tag}_ffn_up_da")
    dw_up = matmul(hf, dup, "tn", F32, f"{tag}_ffn_up_dw")
    dx1, dgain, dsc, dsh = modnorm_bwd(x1, wl["norm_ffn"], sc, dhf, dx2, f"{tag}_ffn_norm_bwd")
    grads = {"norm_ffn": dgain[0], "ffn_w_up": dw_up, "ffn_w_down": dw_down,
             "ffn_conv_w": dcw[:FFN_CONV], "ffn_conv_b": dcb[0]}
    return dx1, (dsh, dsc, dg), grads


def _even_fwd(tag, x, mod_m, wl):
    sh, sc, g = mod_m
    hm = modnorm_fwd(x, wl["norm_mix"], sc, sh, f"{tag}_mix_norm")
    proj = matmul(hm, wl["w_in"], "nn", F32, f"{tag}_ev_in")
    qkv = gdn_conv_fwd(proj, wl["gdn_conv_w8"], f"{tag}_gdn_conv")
    o_a, states = gdn_chunk_fwd(qkv, proj, wl["alog_row"], wl["dt_row"], wl["gdn_norm"], f"{tag}_gdn_chunk")
    o_b = pool_fwd(proj, wl["pool_w"], wl["pool_scale"], f"{tag}_pool")
    ocat = jnp.concatenate([o_a, o_b], axis=1)
    y = matmul(ocat, wl["w_out"], "nn", F32, f"{tag}_ev_out")
    x1 = gres_fwd(x, g, y, f"{tag}_mix_res")
    return x1, (x, hm, proj, qkv, states, ocat, y)


def _even_bwd(tag, dx1, saved, mod_m, wl):
    x, hm, proj, qkv, states, ocat, y = saved
    sh, sc, g = mod_m
    dy, dg = gres_bwd(dx1, g, y, f"{tag}_mix_res_bwd")
    docat = matmul(dy, wl["w_out"], "nt", F32, f"{tag}_ev_out_da")
    dw_out = matmul(ocat, dy, "tn", F32, f"{tag}_ev_out_dw")
    dqkv, dz, dba, dalog, ddt, dgn = gdn_chunk_bwd(
        qkv, proj, wl["alog_row"], wl["dt_row"], wl["gdn_norm"], states, docat, f"{tag}_gdn_chunk_bwd")
    dxc, dconv = gdn_conv_bwd(proj, wl["gdn_conv_w8"], dqkv, f"{tag}_gdn_conv_bwd")
    dp, dpw, dps = pool_bwd(proj, wl["pool_w"], wl["pool_scale"], docat, f"{tag}_pool_bwd")
    dproj = jnp.concatenate([dxc, dz, dp, dba], axis=1)
    dhm = matmul(dproj, wl["w_in"], "nt", F32, f"{tag}_ev_in_da")
    dw_in = matmul(hm, dproj, "tn", F32, f"{tag}_ev_in_dw")
    dx, dgain, dsc, dsh = modnorm_bwd(x, wl["norm_mix"], sc, dhm, dx1, f"{tag}_mix_norm_bwd")
    grads = {"norm_mix": dgain[0], "ev_w_in": _even_restore(dw_in), "ev_w_out": dw_out,
             "gdn_conv_w": dconv[:GDN_CONV], "gdn_a_log": dalog[0, :GDN_H], "gdn_dt_bias": ddt[0, :GDN_H],
             "gdn_norm": dgn[0], "pool_w": dpw, "pool_scale": dps[0]}
    return dx, (dsh, dsc, dg), grads


def _odd_fwd(tag, x, mod_m, wl):
    sh, sc, g = mod_m
    hm = modnorm_fwd(x, wl["norm_mix"], sc, sh, f"{tag}_mix_norm")
    projs, qkvns, outs, lses = [], [], [], []
    for gi, d in enumerate(DIL):
        w_g = wl["w_in"][:, gi * GROUP_COLS:(gi + 1) * GROUP_COLS]
        proj = matmul(to_residue_major(hm, d), w_g, "nn", F32, f"{tag}_od_in{gi}")
        qkvn = headnorm_fwd(proj, wl["qk_gain8"], f"{tag}_headnorm{gi}")
        o_g, l_g = att_fwd(qkvn, gi, f"{tag}_att{gi}")
        projs.append(proj)
        qkvns.append(qkvn)
        outs.append(to_token_order(o_g, d))
        lses.append(to_token_order(l_g, d))
    o, lse = att_merge(outs, lses, f"{tag}_att_merge")
    y = matmul(o, wl["w_out"], "nn", F32, f"{tag}_od_out")
    x1 = gres_fwd(x, g, y, f"{tag}_mix_res")
    return x1, (x, hm, projs, qkvns, o, lse, y)


def _odd_bwd(tag, dx1, saved, mod_m, wl):
    x, hm, projs, qkvns, o, lse, y = saved
    sh, sc, g = mod_m
    dy, dg = gres_bwd(dx1, g, y, f"{tag}_mix_res_bwd")
    do = matmul(dy, wl["w_out"], "nt", F32, f"{tag}_od_out_da")
    dw_out = matmul(o, dy, "tn", F32, f"{tag}_od_out_dw")
    delta = att_delta(do, o, f"{tag}_att_delta")
    dhm, dw_in, dgain_qk = None, [], None
    for gi, d in enumerate(DIL):
        w_g = wl["w_in"][:, gi * GROUP_COLS:(gi + 1) * GROUP_COLS]
        do_g, lse_g, dl_g = (to_residue_major(t, d) for t in (do, lse, delta))
        dq = att_bwd_q(qkvns[gi], do_g, lse_g, dl_g, gi, f"{tag}_att{gi}_dq")
        dk, dv = att_bwd_kv(qkvns[gi], do_g, lse_g, dl_g, gi, f"{tag}_att{gi}_dkv")
        dproj, dgain = headnorm_bwd(projs[gi], wl["qk_gain8"], dq, dk, dv, f"{tag}_headnorm{gi}_bwd")
        dhm_g = to_token_order(matmul(dproj, w_g, "nt", F32, f"{tag}_od_in{gi}_da"), d)
        dw_in.append(matmul(to_residue_major(hm, d), dproj, "tn", F32, f"{tag}_od_in{gi}_dw"))
        dhm = dhm_g if dhm is None else dhm + dhm_g
        dgain_qk = dgain if dgain_qk is None else dgain_qk + dgain
    dx, dgain, dsc, dsh = modnorm_bwd(x, wl["norm_mix"], sc, dhm, dx1, f"{tag}_mix_norm_bwd")
    grads = {"norm_mix": dgain[0], "od_w_in": jnp.concatenate(dw_in, axis=1), "od_w_out": dw_out,
             "att_q_norm": dgain_qk[0], "att_k_norm": dgain_qk[1]}
    return dx, (dsh, dsc, dg), grads


def _layer_weights(i, W):
    wl = {"norm_mix": W["norm_mix"][i][None, :], "norm_ffn": W["norm_ffn"][i][None, :],
          "ffn_w_up": W["ffn_w_up"][i], "ffn_w_down": W["ffn_w_down"][i],
          "ffn_conv_w8": _pad_rows8(W["ffn_conv_w"][i]), "ffn_conv_b": W["ffn_conv_b"][i][None, :]}
    e = i // 2
    if i % 2 == 0:
        wl.update({"w_in": _even_reorder(W["ev_w_in"][e]), "w_out": W["ev_w_out"][e],
                   "gdn_conv_w8": _pad_rows8(W["gdn_conv_w"][e]),
                   "alog_row": _lane_row(W["gdn_a_log"][e]), "dt_row": _lane_row(W["gdn_dt_bias"][e]),
                   "gdn_norm": W["gdn_norm"][e][None, :], "pool_w": W["pool_w"][e],
                   "pool_scale": W["pool_scale"][e][None, :]})
    else:
        wl.update({"w_in": W["od_w_in"][e], "w_out": W["od_w_out"][e],
                   "qk_gain8": _pad_rows8(jnp.stack([W["att_q_norm"][e], W["att_k_norm"][e]]))})
    return wl


def local_step(x, target, mod, W):
    depth = mod.shape[0]
    row = lambda i, k: mod[i, k][None, :]
    saved, wls = [], []
    for i in range(depth):
        wl = _layer_weights(i, W)
        mod_m = (row(i, 0), row(i, 1), row(i, 2))
        mod_f = (row(i, 3), row(i, 4), row(i, 5))
        fwd = _even_fwd if i % 2 == 0 else _odd_fwd
        x1, s_mix = fwd(f"l{i}", x, mod_m, wl)
        x, s_ffn = _ffn_fwd(f"l{i}", x1, mod_f, wl)
        saved.append((s_mix, s_ffn, mod_m, mod_f))
        wls.append(wl)
    sq, dx = loss_head(x, target, "loss_head")
    dmod, grads = [None] * depth, [None] * depth
    for i in reversed(range(depth)):
        s_mix, s_ffn, mod_m, mod_f = saved[i]
        dx, dmf, g_ffn = _ffn_bwd(f"l{i}", dx, s_ffn, mod_f, wls[i])
        bwd = _even_bwd if i % 2 == 0 else _odd_bwd
        dx, dmm, g_mix = bwd(f"l{i}", dx, s_mix, mod_m, wls[i])
        dmod[i] = jnp.concatenate([t for t in dmm + dmf], axis=0)
        grads[i] = {**g_mix, **g_ffn}
    return sq, dx, jnp.stack(dmod), grads


HBM = pl.BlockSpec(memory_space=pltpu.HBM)
DMA_SEM = pltpu.SemaphoreType.DMA


def _place():
    return lax.axis_index("x"), lax.axis_index("y"), lax.axis_index("c")


def _other_chips(mx, my):
    return [(1 - mx, my), (mx, 1 - my), (1 - mx, 1 - my)]


def all_gather8(x, name):
    def body(x_ref, out_ref, send_sems, recv_sems, local_sem):
        mx, my, mc = _place()
        me, sibling = (mx, my, mc), (mx, my, 1 - mc)
        chips = _other_chips(mx, my)

        def slot(px, py, pc):
            return out_ref.at[4 * px + 2 * py + pc]

        def copy(k, block, to, src=None):
            return pltpu.make_async_remote_copy(
                src_ref=slot(*block) if src is None else src, dst_ref=slot(*block),
                send_sem=send_sems.at[k], recv_sem=recv_sems.at[k], device_id=to, device_id_type=MESH)

        mine = pltpu.make_async_copy(x_ref, slot(*me), local_sem)
        mine.start()
        first = [copy(0, me, sibling, src=x_ref)]
        first += [copy(1 + j, me, (*chip, mc), src=x_ref) for j, chip in enumerate(chips)]
        for cp in first:
            cp.start()
        passed = [copy(4 + j, (*chip, mc), sibling) for j, chip in enumerate(chips)]
        for j, chip in enumerate(chips):
            copy(1 + j, (*chip, mc), me).wait_recv()
            passed[j].start()
        copy(0, sibling, me).wait_recv()
        for j, chip in enumerate(chips):
            copy(4 + j, (*chip, 1 - mc), me).wait_recv()
        for cp in first + passed:
            cp.wait_send()
        mine.wait()

    return pl.pallas_call(
        body, name=name, out_shape=jax.ShapeDtypeStruct((8,) + x.shape, x.dtype),
        in_specs=[HBM], out_specs=HBM,
        scratch_shapes=[DMA_SEM((7,)), DMA_SEM((7,)), DMA_SEM(())])(x)


def sibling_swap(x, name):
    def body(x_ref, out_ref, send_sem, recv_sem):
        mx, my, mc = _place()
        cp = pltpu.make_async_remote_copy(src_ref=x_ref, dst_ref=out_ref, send_sem=send_sem, recv_sem=recv_sem,
                                          device_id=(mx, my, 1 - mc), device_id_type=MESH)
        cp.start()
        cp.wait_send()
        cp.wait_recv()

    return pl.pallas_call(
        body, name=name, out_shape=jax.ShapeDtypeStruct(x.shape, x.dtype), in_specs=[HBM], out_specs=HBM,
        scratch_shapes=[DMA_SEM(()), DMA_SEM(())])(x)


def sibling_pair(r, name):
    def body(r_ref, out_ref, send_sem, recv_sem, local_sem):
        mx, my, mc = _place()
        local = pltpu.make_async_copy(r_ref, out_ref.at[mc], local_sem)
        local.start()
        cp = pltpu.make_async_remote_copy(src_ref=r_ref, dst_ref=out_ref.at[mc], send_sem=send_sem,
                                          recv_sem=recv_sem, device_id=(mx, my, 1 - mc), device_id_type=MESH)
        cp.start()
        cp.wait_send()
        pltpu.make_async_remote_copy(src_ref=r_ref, dst_ref=out_ref.at[1 - mc], send_sem=send_sem,
                                     recv_sem=recv_sem, device_id=(mx, my, 1 - mc),
                                     device_id_type=MESH).wait_recv()
        local.wait()

    return pl.pallas_call(
        body, name=name, out_shape=jax.ShapeDtypeStruct((2,) + r.shape, r.dtype), in_specs=[HBM],
        out_specs=HBM, scratch_shapes=[DMA_SEM(()), DMA_SEM(()), DMA_SEM(())])(r)


def chip_all_to_all(p, name):
    def body(p_ref, out_ref, send_sems, recv_sems, local_sem):
        mx, my, mc = _place()
        mine = 2 * mx + my
        chips = _other_chips(mx, my)
        local = pltpu.make_async_copy(p_ref.at[mine], out_ref.at[mine], local_sem)
        local.start()

        def copy(k, chip):
            return pltpu.make_async_remote_copy(
                src_ref=p_ref.at[2 * chip[0] + chip[1]], dst_ref=out_ref.at[mine],
                send_sem=send_sems.at[k], recv_sem=recv_sems.at[k], device_id=(*chip, mc), device_id_type=MESH)

        sends = [copy(k, chip) for k, chip in enumerate(chips)]
        for cp in sends:
            cp.start()
        for k, chip in enumerate(chips):
            pltpu.make_async_remote_copy(
                src_ref=p_ref.at[mine], dst_ref=out_ref.at[2 * chip[0] + chip[1]],
                send_sem=send_sems.at[k], recv_sem=recv_sems.at[k], device_id=(*chip, mc),
                device_id_type=MESH).wait_recv()
        for cp in sends:
            cp.wait_send()
        local.wait()

    return pl.pallas_call(
        body, name=name, out_shape=jax.ShapeDtypeStruct(p.shape, p.dtype), in_specs=[HBM], out_specs=HBM,
        scratch_shapes=[DMA_SEM((3,)), DMA_SEM((3,)), DMA_SEM(())])(p)


def _stream_rows(R, C):
    return _rows(R, max(16, (256 * 1536 // C) // 16 * 16))


def cast_bf16(w, name):
    R, C = w.shape
    tb = _stream_rows(R, C)

    def body(w_ref, o_ref):
        o_ref[...] = w_ref[...].astype(BF16)

    blk = pl.BlockSpec((tb, C), lambda i: (i, 0))
    return pl.pallas_call(body, name=name, grid=(R // tb,), in_specs=[blk], out_specs=blk,
                          out_shape=jax.ShapeDtypeStruct((R, C), BF16), compiler_params=_params(1))(w)


def sum_slots(g, name):
    n, R, C = g.shape
    tb = _stream_rows(R, C)

    def body(*refs):
        acc = refs[0][...].astype(F32)
        for r in refs[1:n]:
            acc = acc + r[...].astype(F32)
        refs[n][...] = acc

    specs = [pl.BlockSpec((None, tb, C), functools.partial(lambda k, i: (k, i, 0), k)) for k in range(n)]
    return pl.pallas_call(body, name=name, grid=(R // tb,), in_specs=specs,
                          out_specs=pl.BlockSpec((tb, C), lambda i: (i, 0)),
                          out_shape=jax.ShapeDtypeStruct((R, C), F32), compiler_params=_params(1))(*([g] * n))


def add_to_bf16(a, b, name):
    R, C = a.shape
    tb = _stream_rows(R, C)

    def body(a_ref, b_ref, o_ref):
        o_ref[...] = (a_ref[...] + b_ref[...]).astype(BF16)

    blk = pl.BlockSpec((tb, C), lambda i: (i, 0))
    return pl.pallas_call(body, name=name, grid=(R // tb,), in_specs=[blk, blk], out_specs=blk,
                          out_shape=jax.ShapeDtypeStruct((R, C), BF16), compiler_params=_params(1))(a, b)


def ada_fwd(c_all, ada_w, bias, name):
    n, D, Cs = ada_w.shape
    tn = _tile(Cs, 512)

    def body(c_ref, w_ref, b_ref, o_ref):
        o_ref[...] = _bdot(_silu(c_ref[...]), w_ref[...], NN) + b_ref[...]

    return pl.pallas_call(
        body, name=name, grid=(n, Cs // tn),
        in_specs=[pl.BlockSpec((8, D), lambda l, j: (0, 0)),
                  pl.BlockSpec((None, D, tn), lambda l, j: (l, 0, j)),
                  pl.BlockSpec((None, 1, tn), lambda l, j: (l, 0, j))],
        out_specs=pl.BlockSpec((None, 8, tn), lambda l, j: (l, 0, j)),
        out_shape=jax.ShapeDtypeStruct((n, 8, Cs), F32), compiler_params=_params(2))(c_all, ada_w, bias)


def ada_bwd(c16, dmod16, name):
    n, _, Cs = dmod16.shape
    D = c16.shape[1]
    tn = _tile(Cs, 512)

    def body(c_ref, d_ref, o_ref):
        o_ref[...] = _bdot(_silu(c_ref[...]), d_ref[...], TN)

    return pl.pallas_call(
        body, name=name, grid=(n, Cs // tn),
        in_specs=[pl.BlockSpec((16, D), lambda l, j: (0, 0)),
                  pl.BlockSpec((None, 16, tn), lambda l, j: (l, 0, j))],
        out_specs=pl.BlockSpec((None, D, tn), lambda l, j: (l, 0, j)),
        out_shape=jax.ShapeDtypeStruct((n, D, Cs), F32), compiler_params=_params(2))(c16, dmod16)


WEIGHTS = ["ada_w", "ada_b", "norm_mix", "norm_ffn", "ev_w_in", "ev_w_out", "gdn_conv_w", "gdn_a_log",
           "gdn_dt_bias", "gdn_norm", "pool_w", "pool_scale", "od_w_in", "od_w_out", "att_q_norm",
           "att_k_norm", "ffn_w_up", "ffn_conv_w", "ffn_conv_b", "ffn_w_down"]
COL_SHARDED = ("ev_w_in", "od_w_in", "ffn_w_up")
ROW_SHARDED = ("ev_w_out", "od_w_out", "ffn_w_down")
N_CHIPS = 4


def _pack(parts):
    rows, offs = [], []
    at = 0
    for p in parts:
        flat = p.reshape(-1).astype(F32)
        n = -(-flat.shape[0] // LANE)
        rows.append(jnp.pad(flat, (0, n * LANE - flat.shape[0])).reshape(n, LANE))
        offs.append((at, n))
        at += n
    pad = -at % 16
    if pad:
        rows.append(jnp.zeros((pad, LANE), F32))
    return jnp.concatenate(rows, axis=0), offs


def _unpack(buf, off, shape):
    at, n = off
    lead = buf.shape[:-2]
    flat = buf[..., at:at + n, :].reshape(lead + (n * LANE,))
    return flat[..., :math.prod(shape)].reshape(lead + tuple(shape))


def _gather_weight(name, w, mc):
    n = w.shape[0]
    Cs = w.shape[-1]
    Rs = math.prod(w.shape[:-1])
    half = lax.dynamic_index_in_dim(cast_bf16(w.reshape(Rs, Cs), f"cast_{name}").reshape(2, Rs // 2, Cs),
                                    mc, 0, keepdims=False)
    g = all_gather8(half, f"gather_{name}").reshape((N_CHIPS,) + w.shape)
    if name in COL_SHARDED:
        return g.transpose(1, 2, 0, 3).reshape(n, w.shape[1], N_CHIPS * Cs)
    return g.transpose(1, 0, 2, 3).reshape(n, N_CHIPS * w.shape[1], Cs)


def _reduce_weight_grad(name, dw, shard_shape, mc):
    n = dw.shape[0]
    Cs = shard_shape[-1]
    Rs = math.prod(shard_shape[:-1])
    if name in COL_SHARDED:
        by_chip = dw.reshape(n, dw.shape[1], N_CHIPS, Cs).transpose(2, 0, 1, 3)
    else:
        by_chip = dw.reshape(n, N_CHIPS, shard_shape[1], Cs).transpose(1, 0, 2, 3)
    halves = by_chip.reshape(N_CHIPS, 2, Rs // 2, Cs).transpose(1, 0, 2, 3).reshape(2, N_CHIPS * (Rs // 2), Cs)
    keep = lax.dynamic_index_in_dim(halves, mc, 0, keepdims=False)
    send = lax.dynamic_index_in_dim(halves, 1 - mc, 0, keepdims=False)
    chip_sum = add_to_bf16(keep, sibling_swap(send, f"gswap_{name}"), f"gadd_{name}")
    got = chip_all_to_all(chip_sum.reshape(N_CHIPS, Rs // 2, Cs), f"gscatter_{name}")
    half = sum_slots(got, f"gsum_{name}")
    return sibling_pair(half, f"gpair_{name}").reshape(shard_shape)


def kernel(x, c, ada_w, ada_b, norm_mix, norm_ffn, ev_w_in, ev_w_out, gdn_conv_w, gdn_a_log, gdn_dt_bias, gdn_norm, pool_w, pool_scale, od_w_in, od_w_out, att_q_norm, att_k_norm, ffn_w_up, ffn_conv_w, ffn_conv_b, ffn_w_down, loss_target, m_ada_w, m_ada_b, m_norm_mix, m_norm_ffn, m_ev_w_in, m_ev_w_out, m_gdn_conv_w, m_gdn_a_log, m_gdn_dt_bias, m_gdn_norm, m_pool_w, m_pool_scale, m_od_w_in, m_od_w_out, m_att_q_norm, m_att_k_norm, m_ffn_w_up, m_ffn_conv_w, m_ffn_conv_b, m_ffn_w_down, v_ada_w, v_ada_b, v_norm_mix, v_norm_ffn, v_ev_w_in, v_ev_w_out, v_gdn_conv_w, v_gdn_a_log, v_gdn_dt_bias, v_gdn_norm, v_pool_w, v_pool_scale, v_od_w_in, v_od_w_out, v_att_q_norm, v_att_k_norm, v_ffn_w_up, v_ffn_conv_w, v_ffn_conv_b, v_ffn_w_down):
    local = dict(ada_w=ada_w, ada_b=ada_b, norm_mix=norm_mix, norm_ffn=norm_ffn, ev_w_in=ev_w_in,
                 ev_w_out=ev_w_out, gdn_conv_w=gdn_conv_w, gdn_a_log=gdn_a_log, gdn_dt_bias=gdn_dt_bias,
                 gdn_norm=gdn_norm, pool_w=pool_w, pool_scale=pool_scale, od_w_in=od_w_in, od_w_out=od_w_out,
                 att_q_norm=att_q_norm, att_k_norm=att_k_norm, ffn_w_up=ffn_w_up, ffn_conv_w=ffn_conv_w,
                 ffn_conv_b=ffn_conv_b, ffn_w_down=ffn_w_down)
    moments_m = dict(zip(WEIGHTS, (m_ada_w, m_ada_b, m_norm_mix, m_norm_ffn, m_ev_w_in, m_ev_w_out,
                                   m_gdn_conv_w, m_gdn_a_log, m_gdn_dt_bias, m_gdn_norm, m_pool_w, m_pool_scale,
                                   m_od_w_in, m_od_w_out, m_att_q_norm, m_att_k_norm, m_ffn_w_up, m_ffn_conv_w,
                                   m_ffn_conv_b, m_ffn_w_down)))
    moments_v = dict(zip(WEIGHTS, (v_ada_w, v_ada_b, v_norm_mix, v_norm_ffn, v_ev_w_in, v_ev_w_out,
                                   v_gdn_conv_w, v_gdn_a_log, v_gdn_dt_bias, v_gdn_norm, v_pool_w, v_pool_scale,
                                   v_od_w_in, v_od_w_out, v_att_q_norm, v_att_k_norm, v_ffn_w_up, v_ffn_conv_w,
                                   v_ffn_conv_b, v_ffn_w_down)))
    mx, my, mc = _place()
    chip = 2 * mx + my
    T, D = x.shape[1], x.shape[2]
    depth = ada_w.shape[0]
    ada_cols = ada_w.shape[2]

    buf, offs = _pack([c, gdn_conv_w, ffn_conv_w])
    gathered = all_gather8(buf, "gather_small")
    c_all = _unpack(gathered, offs[0], (D,))
    by_chip = gathered[0::2]
    gdn_conv_full = jnp.concatenate(list(_unpack(by_chip, offs[1], gdn_conv_w.shape)), axis=-1)
    ffn_conv_full = jnp.concatenate(list(_unpack(by_chip, offs[2], ffn_conv_w.shape)), axis=-1)

    bias = lax.dynamic_slice_in_dim(ada_b, chip * ada_cols, ada_cols, axis=1)[:, None, :]
    mod_part = ada_fwd(c_all, ada_w, bias, "ada_fwd")
    mod_all = all_gather8(mod_part, "gather_mod")[0::2]
    mod_all = mod_all.transpose(1, 2, 0, 3).reshape(depth, 8, N_CHIPS * ada_cols)
    mod = lax.dynamic_index_in_dim(mod_all, 4 * mx + 2 * my + mc, 1, keepdims=False).reshape(depth, 6, D)

    W = dict(local)
    for name in COL_SHARDED + ROW_SHARDED:
        W[name] = _gather_weight(name, local[name], mc)
    W["gdn_conv_w"], W["ffn_conv_w"] = gdn_conv_full, ffn_conv_full

    sq, dx, dmod, grads = local_step(x[0], loss_target[0], mod, W)
    loss = lax.psum(0.5 * jnp.sum(sq) / D, ("x", "y", "c"))

    def stacked(name):
        return jnp.stack([g[name] for g in grads if name in g])

    full = {name: stacked(name) for name in WEIGHTS if name not in ("ada_w", "ada_b")}
    grad = {}
    for name in COL_SHARDED + ROW_SHARDED:
        grad[name] = _reduce_weight_grad(name, full[name], local[name].shape, mc)

    small = ["norm_mix", "norm_ffn", "gdn_conv_w", "gdn_a_log", "gdn_dt_bias", "gdn_norm", "pool_w",
             "pool_scale", "att_q_norm", "att_k_norm", "ffn_conv_w", "ffn_conv_b"]
    buf, offs = _pack([dmod] + [full[name] for name in small])
    gathered = all_gather8(buf, "gather_small_grads")
    summed = sum_slots(gathered, "sum_small_grads")
    grad["ada_b"] = _unpack(summed, offs[0], ada_b.shape)
    for k, name in enumerate(small):
        grad[name] = _unpack(summed, offs[1 + k], full[name].shape)
    for name, cols in (("gdn_conv_w", gdn_conv_w.shape[-1]), ("ffn_conv_w", ffn_conv_w.shape[-1])):
        grad[name] = lax.dynamic_slice_in_dim(grad[name], chip * cols, cols, axis=2)

    dmod_all = _unpack(gathered, offs[0], (depth, N_CHIPS * ada_cols))
    dmod_mine = lax.dynamic_slice_in_dim(dmod_all, chip * ada_cols, ada_cols, axis=2).transpose(1, 0, 2)
    grad["ada_w"] = ada_bwd(jnp.pad(c_all, ((0, 8), (0, 0))), jnp.pad(dmod_mine, ((0, 0), (0, 8), (0, 0))),
                            "ada_bwd")

    deltas, new_m, new_v = {}, {}, {}
    for name in WEIGHTS:
        deltas[name], new_m[name], new_v[name] = adamw(local[name], grad[name], moments_m[name],
                                                       moments_v[name], f"adamw_{name}")
    return (loss, dx[None], *[grad[n] for n in WEIGHTS], *[deltas[n] for n in WEIGHTS],
            *[new_m[n] for n in WEIGHTS], *[new_v[n] for n in WEIGHTS])
```

```python
import functools
import math

import jax
import jax.numpy as jnp
from jax import lax
from jax.experimental import pallas as pl
from jax.experimental.pallas import tpu as pltpu

F32 = jnp.float32
BF16 = jnp.bfloat16
LANE = 128
SUBLANE = 8
VMEM_LIMIT = 56 * 1024 * 1024
MESH = pl.DeviceIdType.MESH
N_CHIPS = 4

RMS_EPS = 1e-6
GDN_H = 4
HD = 128
GDN_CHUNK = 64
GDN_STEP = 2
GDN_CONV = 4
FFN_CONV = 3
POOL_G = 4
ATT_H = 8
ATT_BLK = 128
DIL = (1, 4, 16)
EVEN_COLS = 2568
EVEN_PAD = 2688
ADAM_LR, ADAM_B1, ADAM_B2, ADAM_EPS, ADAM_WD, ADAM_STEP = 0.001, 0.9, 0.999, 1e-08, 0.01, 10
NEG = -1e30

NN = (((1,), (0,)), ((), ()))
NT = (((1,), (1,)), ((), ()))
TN = (((0,), (0,)), ((), ()))
BNN = (((2,), (1,)), ((0,), (0,)))
BNT = (((2,), (2,)), ((0,), (0,)))
BTN = (((1,), (1,)), ((0,), (0,)))


def _params(n_grid):
    return pltpu.CompilerParams(dimension_semantics=("arbitrary",) * n_grid,
                                vmem_limit_bytes=VMEM_LIMIT)


HBM = pl.BlockSpec(memory_space=pltpu.HBM)
DMA_SEM = pltpu.SemaphoreType.DMA


class Ride:
    def __init__(self, inputs, out_shapes, sems, start, finish, then, mid=None):
        self.inputs, self.out_shapes, self.sems = list(inputs), list(out_shapes), list(sems)
        self.start, self.mid, self.finish, self.then = start, mid, finish, then


_RIDES = []


def submit_ride(ride):
    _RIDES.append(ride)


def flush_rides(until=None):
    while _RIDES and not (until is not None and until()):
        ride = _RIDES.pop(0)

        def body(*refs, ride=ride):
            a, b = len(ride.inputs), len(ride.inputs) + len(ride.out_shapes)
            ride.start(refs[:a], refs[a:b], refs[b:])
            if ride.mid is not None:
                ride.mid(refs[:a], refs[a:b], refs[b:])
            ride.finish(refs[:a], refs[a:b], refs[b:])

        outs = pl.pallas_call(body, name=f"exchange{_next_id()}", in_specs=[HBM] * len(ride.inputs),
                              out_specs=[HBM] * len(ride.out_shapes), out_shape=ride.out_shapes,
                              scratch_shapes=ride.sems)(*ride.inputs)
        ride.then(list(outs))


_IDS = [0]


def _next_id():
    _IDS[0] += 1
    return _IDS[0]


def _pcall(body, *, name, grid, in_specs, out_specs, out_shape, scratch_shapes=(), compiler_params=None):
    del compiler_params
    single = not isinstance(out_shape, (list, tuple))
    outs = [out_shape] if single else list(out_shape)
    ospecs = [out_specs] if single else list(out_specs)
    total = math.prod(grid)
    ride = _RIDES.pop(0) if (_RIDES and total > 1) else None
    if ride is None:
        call = pl.pallas_call(body, name=name, grid=grid, in_specs=list(in_specs), out_specs=ospecs,
                              out_shape=outs, scratch_shapes=list(scratch_shapes),
                              compiler_params=_params(len(grid)))

        def run_plain(*args):
            res = call(*args)
            return res[0] if single else res
        return run_plain

    n_in, n_out, n_scr = len(in_specs), len(outs), len(scratch_shapes)
    r_in, r_out = len(ride.inputs), len(ride.out_shapes)

    def carrying_body(*refs):
        at = 0
        ins = refs[at:at + n_in]; at += n_in
        r_ins = refs[at:at + r_in]; at += r_in
        os_ = refs[at:at + n_out]; at += n_out
        r_outs = refs[at:at + r_out]; at += r_out
        scr = refs[at:at + n_scr]; at += n_scr
        r_sems = refs[at:]
        step = pl.program_id(0)
        for ax in range(1, len(grid)):
            step = step * grid[ax] + pl.program_id(ax)

        @pl.when(step == 0)
        def _():
            ride.start(r_ins, r_outs, r_sems)

        body(*ins, *os_, *scr)

        if ride.mid is not None:
            @pl.when(step == total // 2)
            def _():
                ride.mid(r_ins, r_outs, r_sems)

        @pl.when(step == total - 1)
        def _():
            ride.finish(r_ins, r_outs, r_sems)

    call = pl.pallas_call(
        carrying_body, name=name, grid=grid, in_specs=list(in_specs) + [HBM] * r_in,
        out_specs=ospecs + [HBM] * r_out, out_shape=outs + ride.out_shapes,
        scratch_shapes=list(scratch_shapes) + ride.sems, compiler_params=_params(len(grid)))

    def run_carrying(*args):
        res = call(*args, *ride.inputs)
        ride.then(list(res[n_out:]))
        return res[0] if single else list(res[:n_out])
    return run_carrying


def _tile(n, target):
    if n <= target:
        return n
    best = None
    for t in range(LANE, target + 1, LANE):
        if n % t == 0:
            best = t
    assert best is not None, (n, target)
    return best


def _rows(n, target):
    if n <= target:
        return n
    best = None
    for t in range(16, target + 1, 16):
        if n % t == 0:
            best = t
    assert best is not None, (n, target)
    return best


def _bdot(a, b, dims):
    return lax.dot_general(a.astype(BF16), b.astype(BF16), dims, preferred_element_type=F32)


def _split(a):
    hi = a.astype(BF16)
    return hi, (a - hi.astype(F32)).astype(BF16)


def _dot3(a, b, dims):
    ah, al = _split(a)
    bh, bl = _split(b)
    d = lambda p, q: lax.dot_general(p, q, dims, preferred_element_type=F32)
    return d(ah, bh) + d(ah, bl) + d(al, bh)


def _sigmoid(x):
    return 1.0 / (1.0 + jnp.exp(-x))


def _silu(x):
    return x * _sigmoid(x)


def matmul(a, b, mode, out_dtype, name, tm=512, tn=1536, tk=1536, pieces=None):
    if mode == "nn":
        (M, K), (K2, N) = a.shape, b.shape
    elif mode == "nt":
        (M, K), (N, K2) = a.shape, b.shape
    else:
        (K, M), (K2, N) = a.shape, b.shape
    assert K == K2, (a.shape, b.shape, mode)
    if pieces == "col":
        tm, tn = _tile(M // 2, tm), _tile(N // N_CHIPS, tn)
    elif pieces == "row":
        quarter = M // N_CHIPS
        tm = 2 * quarter if (2 * quarter) % LANE == 0 else M
        tn = _tile(N // 2, tn)
    else:
        tm, tn = _tile(M, tm), _tile(N, tn)
    tk = _tile(K, tk)
    nk = K // tk
    dims = {"nn": NN, "nt": NT, "tn": TN}[mode]
    if mode == "tn":
        a_spec = pl.BlockSpec((tk, tm), lambda i, j, k: (k, i))
    else:
        a_spec = pl.BlockSpec((tm, tk), lambda i, j, k: (i, k))
    if mode == "nt":
        b_spec = pl.BlockSpec((tn, tk), lambda i, j, k: (j, k))
    else:
        b_spec = pl.BlockSpec((tk, tn), lambda i, j, k: (k, j))

    out_spec = pl.BlockSpec((tm, tn), lambda i, j, k: (i, j))
    out_shape = jax.ShapeDtypeStruct((M, N), out_dtype)
    rows_per_slot = tm
    if pieces == "col":
        nih, njc = (M // 2) // tm, (N // N_CHIPS) // tn
        out_spec = pl.BlockSpec((None, None, tm, tn), lambda i, j, k: (i // nih, j // njc, i % nih, j % njc))
        out_shape = jax.ShapeDtypeStruct((2, N_CHIPS, M // 2, N // N_CHIPS), out_dtype)
    elif pieces == "row":
        rows_per_slot = M // N_CHIPS
        njh = (N // 2) // tn
        out_spec = pl.BlockSpec((None, tm // rows_per_slot, rows_per_slot, tn),
                                lambda i, j, k: (j // njh, i, 0, j % njh))
        out_shape = jax.ShapeDtypeStruct((2, N_CHIPS, rows_per_slot, N // 2), out_dtype)

    def body(a_ref, b_ref, o_ref, acc_ref):
        k = pl.program_id(2)
        p = _bdot(a_ref[...], b_ref[...], dims)

        @pl.when(k == 0)
        def _():
            acc_ref[...] = p

        @pl.when(k > 0)
        def _():
            acc_ref[...] += p

        @pl.when(k == nk - 1)
        def _():
            if pieces == "row":
                for s in range(tm // rows_per_slot):
                    o_ref[s] = acc_ref[s * rows_per_slot:(s + 1) * rows_per_slot, :].astype(out_dtype)
            else:
                o_ref[...] = acc_ref[...].astype(out_dtype)

    return _pcall(
        body, name=name, grid=(M // tm, N // tn, nk),
        in_specs=[a_spec, b_spec], out_specs=out_spec, out_shape=out_shape,
        scratch_shapes=[pltpu.VMEM((tm, tn), F32)],
        compiler_params=_params(3))(a, b)


def _row_spec(d):
    return pl.BlockSpec((1, d), lambda i: (0, 0))


def modnorm_fwd(x, gain, sc, sh, name):
    T, D = x.shape
    tb = _rows(T, 512)

    def body(x_ref, g_ref, sc_ref, sh_ref, o_ref):
        xv = x_ref[...]
        r = lax.rsqrt(jnp.mean(xv * xv, axis=-1, keepdims=True) + RMS_EPS)
        o_ref[...] = ((xv * r) * g_ref[...] * (1.0 + sc_ref[...]) + sh_ref[...]).astype(BF16)

    blk = pl.BlockSpec((tb, D), lambda i: (i, 0))
    return _pcall(
        body, name=name, grid=(T // tb,),
        in_specs=[blk, _row_spec(D), _row_spec(D), _row_spec(D)],
        out_specs=blk, out_shape=jax.ShapeDtypeStruct((T, D), BF16),
        compiler_params=_params(1))(x, gain, sc, sh)


def modnorm_bwd(x, gain, sc, dh, dres, name):
    T, D = x.shape
    tb = _rows(T, 512)

    def body(x_ref, g_ref, sc_ref, dh_ref, dres_ref, dx_ref, dg_ref, dsc_ref, dsh_ref):
        i = pl.program_id(0)
        xv = x_ref[...]
        r = lax.rsqrt(jnp.mean(xv * xv, axis=-1, keepdims=True) + RMS_EPS)
        n = xv * r
        dhv = dh_ref[...].astype(F32)
        gain_v, sc1 = g_ref[...], 1.0 + sc_ref[...]
        dn = dhv * (gain_v * sc1)
        dx_ref[...] = r * (dn - n * jnp.mean(dn * n, axis=-1, keepdims=True)) + dres_ref[...]
        dhn = dhv * n

        @pl.when(i == 0)
        def _():
            dg_ref[...] = jnp.zeros_like(dg_ref)
            dsc_ref[...] = jnp.zeros_like(dsc_ref)
            dsh_ref[...] = jnp.zeros_like(dsh_ref)

        dg_ref[...] += jnp.sum(dhn * sc1, axis=0, keepdims=True)
        dsc_ref[...] += jnp.sum(dhn * gain_v, axis=0, keepdims=True)
        dsh_ref[...] += jnp.sum(dhv, axis=0, keepdims=True)

    blk = pl.BlockSpec((tb, D), lambda i: (i, 0))
    row = jax.ShapeDtypeStruct((1, D), F32)
    return _pcall(
        body, name=name, grid=(T // tb,),
        in_specs=[blk, _row_spec(D), _row_spec(D), blk, blk],
        out_specs=[blk, _row_spec(D), _row_spec(D), _row_spec(D)],
        out_shape=[jax.ShapeDtypeStruct((T, D), F32), row, row, row],
        compiler_params=_params(1))(x, gain, sc, dh, dres)


def gres_fwd(x, g, y, name):
    T, D = x.shape
    tb = _rows(T, 512)

    def body(x_ref, g_ref, y_ref, o_ref):
        o_ref[...] = x_ref[...] + g_ref[...] * y_ref[...]

    blk = pl.BlockSpec((tb, D), lambda i: (i, 0))
    return _pcall(
        body, name=name, grid=(T // tb,), in_specs=[blk, _row_spec(D), blk], out_specs=blk,
        out_shape=jax.ShapeDtypeStruct((T, D), F32), compiler_params=_params(1))(x, g, y)


def gres_bwd(dx, g, y, name):
    T, D = dx.shape
    tb = _rows(T, 512)

    def body(dx_ref, g_ref, y_ref, dy_ref, dg_ref):
        i = pl.program_id(0)
        dxv = dx_ref[...]
        dy_ref[...] = (dxv * g_ref[...]).astype(BF16)

        @pl.when(i == 0)
        def _():
            dg_ref[...] = jnp.zeros_like(dg_ref)

        dg_ref[...] += jnp.sum(dxv * y_ref[...], axis=0, keepdims=True)

    blk = pl.BlockSpec((tb, D), lambda i: (i, 0))
    return _pcall(
        body, name=name, grid=(T // tb,), in_specs=[blk, _row_spec(D), blk],
        out_specs=[blk, _row_spec(D)],
        out_shape=[jax.ShapeDtypeStruct((T, D), BF16), jax.ShapeDtypeStruct((1, D), F32)],
        compiler_params=_params(1))(dx, g, y)


def loss_head(y, target, name):
    T, D = y.shape
    tb = _rows(T, 512)

    def body(y_ref, t_ref, l_ref, dy_ref):
        i = pl.program_id(0)
        err = y_ref[...] - t_ref[...]
        dy_ref[...] = err * (1.0 / D)

        @pl.when(i == 0)
        def _():
            l_ref[...] = jnp.zeros_like(l_ref)

        sq = jnp.sum(err * err, axis=0, keepdims=True)
        tot = sq[:, 0:LANE]
        for k in range(1, D // LANE):
            tot = tot + sq[:, k * LANE:(k + 1) * LANE]
        l_ref[...] += tot

    blk = pl.BlockSpec((tb, D), lambda i: (i, 0))
    return _pcall(
        body, name=name, grid=(T // tb,), in_specs=[blk, blk],
        out_specs=[_row_spec(LANE), blk],
        out_shape=[jax.ShapeDtypeStruct((1, LANE), F32), jax.ShapeDtypeStruct((T, D), F32)],
        compiler_params=_params(1))(y, target)


def _back(ext, s):
    return ext if s == 0 else pltpu.roll(ext, s, 0)


def _ahead(ext, s):
    return ext if s == 0 else pltpu.roll(ext, ext.shape[0] - s, 0)


def _halo_prev(tb, h):
    return lambda i, j: (jnp.maximum(i * (tb // h) - 1, 0), j)


def _halo_next(tb, h, nrb):
    return lambda i, j: (jnp.minimum(i + 1, nrb - 1) * (tb // h), j)


FFN_TB, FFN_CB = 256, 1408


def ffn_mid_fwd(up, conv_w8, conv_b, name):
    T, F2 = up.shape
    Fd = F2 // 2
    tb, cb = _rows(T, FFN_TB), _tile(Fd, FFN_CB)
    ncb = Fd // cb

    def body(g_ref, gp_ref, v_ref, w_ref, b_ref, o_ref):
        i = pl.program_id(0)
        g = g_ref[...]
        prev = jnp.where(i > 0, gp_ref[...], 0.0)
        ext = jnp.concatenate([prev, g], axis=0)
        w = w_ref[...]
        gc = (w[2:3] * g + w[1:2] * _back(ext, 1)[SUBLANE:] + w[0:1] * _back(ext, 2)[SUBLANE:]
              + b_ref[...])
        o_ref[...] = (_silu(gc) * v_ref[...]).astype(BF16)

    return _pcall(
        body, name=name, grid=(T // tb, ncb),
        in_specs=[pl.BlockSpec((tb, cb), lambda i, j: (i, j)),
                  pl.BlockSpec((SUBLANE, cb), _halo_prev(tb, SUBLANE)),
                  pl.BlockSpec((tb, cb), lambda i, j: (i, j + ncb)),
                  pl.BlockSpec((SUBLANE, cb), lambda i, j: (0, j)),
                  pl.BlockSpec((1, cb), lambda i, j: (0, j))],
        out_specs=pl.BlockSpec((tb, cb), lambda i, j: (i, j)),
        out_shape=jax.ShapeDtypeStruct((T, Fd), BF16),
        compiler_params=_params(2))(up, up, up, conv_w8, conv_b)


def ffn_mid_bwd(up, conv_w8, conv_b, dact, name):
    T, F2 = up.shape
    Fd = F2 // 2
    tb, cb = _rows(T, FFN_TB), _tile(Fd, FFN_CB)
    ncb, nrb = Fd // cb, T // tb
    H = SUBLANE

    def body(g_ref, gp_ref, gn_ref, v_ref, vn_ref, d_ref, dn_ref, w_ref, b_ref,
             dg_ref, dv_ref, dw_ref, db_ref):
        i = pl.program_id(1)
        g = g_ref[...]
        prev = jnp.where(i > 0, gp_ref[...], 0.0)
        ext = jnp.concatenate([prev, g, gn_ref[...]], axis=0)
        w = w_ref[...]
        e1, e2 = _back(ext, 1), _back(ext, 2)
        gc = (w[2:3] * ext + w[1:2] * e1 + w[0:1] * e2 + b_ref[...])[H:]
        val = jnp.concatenate([v_ref[...], vn_ref[...]], axis=0)
        dnext = jnp.where(i < nrb - 1, dn_ref[...], 0.0)
        da = jnp.concatenate([d_ref[...], dnext], axis=0)
        sg = _sigmoid(gc)
        dv_ref[...] = (da * gc * sg)[:tb].astype(BF16)
        dgc = da * val * (sg * (1.0 + gc * (1.0 - sg)))
        dg_ref[...] = (w[2:3] * dgc + w[1:2] * _ahead(dgc, 1) + w[0:1] * _ahead(dgc, 2))[:tb].astype(BF16)
        dc = dgc[:tb]

        @pl.when(i == 0)
        def _():
            dw_ref[...] = jnp.zeros_like(dw_ref)
            db_ref[...] = jnp.zeros_like(db_ref)

        dw_ref[2:3, :] += jnp.sum(dc * g, axis=0, keepdims=True)
        dw_ref[1:2, :] += jnp.sum(dc * e1[H:H + tb], axis=0, keepdims=True)
        dw_ref[0:1, :] += jnp.sum(dc * e2[H:H + tb], axis=0, keepdims=True)
        db_ref[...] += jnp.sum(dc, axis=0, keepdims=True)

    cur = lambda j, i: (i, j)
    prv = lambda j, i: _halo_prev(tb, H)(i, j)
    nxt = lambda j, i: _halo_next(tb, H, nrb)(i, j)
    return _pcall(
        body, name=name, grid=(ncb, nrb),
        in_specs=[pl.BlockSpec((tb, cb), cur), pl.BlockSpec((H, cb), prv), pl.BlockSpec((H, cb), nxt),
                  pl.BlockSpec((tb, cb), lambda j, i: (i, j + ncb)),
                  pl.BlockSpec((H, cb), lambda j, i: (jnp.minimum(i + 1, nrb - 1) * (tb // H), j + ncb)),
                  pl.BlockSpec((tb, cb), cur), pl.BlockSpec((H, cb), nxt),
                  pl.BlockSpec((SUBLANE, cb), lambda j, i: (0, j)),
                  pl.BlockSpec((1, cb), lambda j, i: (0, j))],
        out_specs=[pl.BlockSpec((tb, cb), cur), pl.BlockSpec((tb, cb), cur),
                   pl.BlockSpec((SUBLANE, cb), lambda j, i: (0, j)),
                   pl.BlockSpec((1, cb), lambda j, i: (0, j))],
        out_shape=[jax.ShapeDtypeStruct((T, Fd), BF16), jax.ShapeDtypeStruct((T, Fd), BF16),
                   jax.ShapeDtypeStruct((SUBLANE, Fd), F32), jax.ShapeDtypeStruct((1, Fd), F32)],
        compiler_params=_params(2))(up, up, up, up, up, dact, dact, conv_w8, conv_b)


GDN_W = GDN_H * HD


def _head_l2norm(a, apply):
    parts = []
    for h in range(GDN_H):
        ah = a[:, h * HD:(h + 1) * HD]
        parts.append(ah * lax.rsqrt(jnp.sum(ah * ah, axis=-1, keepdims=True) + RMS_EPS))
    return jnp.where(apply, jnp.concatenate(parts, axis=1), a)


def _head_l2norm_bwd(a, dy, apply):
    parts = []
    for h in range(GDN_H):
        sl = slice(h * HD, (h + 1) * HD)
        ah, dh = a[:, sl], dy[:, sl]
        r = lax.rsqrt(jnp.sum(ah * ah, axis=-1, keepdims=True) + RMS_EPS)
        y = ah * r
        parts.append(r * (dh - y * jnp.sum(dh * y, axis=-1, keepdims=True)))
    return jnp.where(apply, jnp.concatenate(parts, axis=1), dy)


def gdn_conv_fwd(proj, w8, name):
    T = proj.shape[0]
    tb = _rows(T, 512)
    H = SUBLANE

    def body(x_ref, xp_ref, w_ref, o_ref):
        i, j = pl.program_id(0), pl.program_id(1)
        x = x_ref[...]
        prev = jnp.where(i > 0, xp_ref[...], 0.0)
        ext = jnp.concatenate([prev, x], axis=0)
        w = w_ref[...]
        c = (w[3:4] * x + w[2:3] * _back(ext, 1)[H:] + w[1:2] * _back(ext, 2)[H:]
             + w[0:1] * _back(ext, 3)[H:])
        o_ref[...] = _head_l2norm(_silu(c), j < 2)

    return _pcall(
        body, name=name, grid=(T // tb, 3),
        in_specs=[pl.BlockSpec((tb, GDN_W), lambda i, j: (i, j)),
                  pl.BlockSpec((H, GDN_W), _halo_prev(tb, H)),
                  pl.BlockSpec((SUBLANE, GDN_W), lambda i, j: (0, j))],
        out_specs=pl.BlockSpec((tb, GDN_W), lambda i, j: (i, j)),
        out_shape=jax.ShapeDtypeStruct((T, 3 * GDN_W), F32),
        compiler_params=_params(2))(proj, proj, w8)


def gdn_conv_bwd(proj, w8, dout, name):
    T = proj.shape[0]
    tb = _rows(T, 512)
    nrb = T // tb
    H = SUBLANE

    def body(x_ref, xp_ref, xn_ref, d_ref, dn_ref, w_ref, dx_ref, dw_ref):
        j, i = pl.program_id(0), pl.program_id(1)
        x = x_ref[...]
        prev = jnp.where(i > 0, xp_ref[...], 0.0)
        ext = jnp.concatenate([prev, x, xn_ref[...]], axis=0)
        w = w_ref[...]
        e1, e2, e3 = _back(ext, 1), _back(ext, 2), _back(ext, 3)
        c = (w[3:4] * ext + w[2:3] * e1 + w[1:2] * e2 + w[0:1] * e3)[H:]
        sg = _sigmoid(c)
        dnext = jnp.where(i < nrb - 1, dn_ref[...], 0.0)
        do = jnp.concatenate([d_ref[...], dnext], axis=0)
        da = _head_l2norm_bwd(c * sg, do, j < 2)
        dc = da * (sg * (1.0 + c * (1.0 - sg)))
        dx_ref[...] = (w[3:4] * dc + w[2:3] * _ahead(dc, 1) + w[1:2] * _ahead(dc, 2)
                       + w[0:1] * _ahead(dc, 3))[:tb].astype(BF16)
        dcc = dc[:tb]

        @pl.when(i == 0)
        def _():
            dw_ref[...] = jnp.zeros_like(dw_ref)

        dw_ref[3:4, :] += jnp.sum(dcc * x, axis=0, keepdims=True)
        dw_ref[2:3, :] += jnp.sum(dcc * e1[H:H + tb], axis=0, keepdims=True)
        dw_ref[1:2, :] += jnp.sum(dcc * e2[H:H + tb], axis=0, keepdims=True)
        dw_ref[0:1, :] += jnp.sum(dcc * e3[H:H + tb], axis=0, keepdims=True)

    cur = lambda j, i: (i, j)
    prv = lambda j, i: _halo_prev(tb, H)(i, j)
    nxt = lambda j, i: _halo_next(tb, H, nrb)(i, j)
    return _pcall(
        body, name=name, grid=(3, nrb),
        in_specs=[pl.BlockSpec((tb, GDN_W), cur), pl.BlockSpec((H, GDN_W), prv), pl.BlockSpec((H, GDN_W), nxt),
                  pl.BlockSpec((tb, GDN_W), cur), pl.BlockSpec((H, GDN_W), nxt),
                  pl.BlockSpec((SUBLANE, GDN_W), lambda j, i: (0, j))],
        out_specs=[pl.BlockSpec((tb, GDN_W), cur), pl.BlockSpec((SUBLANE, GDN_W), lambda j, i: (0, j))],
        out_shape=[jax.ShapeDtypeStruct((T, 3 * GDN_W), BF16),
                   jax.ShapeDtypeStruct((SUBLANE, 3 * GDN_W), F32)],
        compiler_params=_params(2))(proj, proj, proj, dout, dout, w8)


def _dot_family(dot, diff):
    if not diff:
        return tuple(functools.partial(lambda d, a, b: dot(a, b, d), d) for d in (BNN, BNT, BTN))

    @jax.custom_vjp
    def nn(a, b):
        return dot(a, b, BNN)
    nn.defvjp(lambda a, b: (dot(a, b, BNN), (a, b)),
              lambda res, g: (dot(g, res[1], BNT), dot(res[0], g, BTN)))

    @jax.custom_vjp
    def nt(a, b):
        return dot(a, b, BNT)
    nt.defvjp(lambda a, b: (dot(a, b, BNT), (a, b)),
              lambda res, g: (dot(g, res[1], BNN), dot(g, res[0], BTN)))

    @jax.custom_vjp
    def tn(a, b):
        return dot(a, b, BTN)
    tn.defvjp(lambda a, b: (dot(a, b, BTN), (a, b)),
              lambda res, g: (dot(res[1], g, BNT), dot(res[0], g, BNN)))
    return nn, nt, tn


def _gdn_step(dots, hdots, S, q, k, v, z, b_raw, a_raw, alog, dtb, gnorm):
    nn, nt, tn = dots
    hnn = hdots[0]
    B, C = q.shape[0], GDN_CHUNK
    ii = lax.broadcasted_iota(jnp.int32, (B, C, C), 1)
    jj = lax.broadcasted_iota(jnp.int32, (B, C, C), 2)
    causal, strict = ii >= jj, ii > jj
    tri, tri_t = causal.astype(F32), (ii <= jj).astype(F32)
    eye, ones = (ii == jj).astype(F32), jnp.ones((B, C, C), F32)

    beta = _sigmoid(b_raw)
    xs = a_raw + dtb
    pos = xs > 0.0
    softplus = jnp.where(pos, xs, 0.0) + jnp.log(1.0 + jnp.exp(jnp.where(pos, -xs, xs)))
    g = -jnp.exp(alog) * softplus
    gb = jnp.broadcast_to(g, (B, C, C))
    gc_c = hnn(tri, gb)
    gc_r = hnn(hnn(ones, eye * gb), tri_t)
    gc = hnn(tri, jnp.broadcast_to(g, (B, C, HD)))
    gl = jnp.sum(g, axis=1, keepdims=True)
    decay = jnp.where(causal, jnp.exp(jnp.where(causal, gc_c - gc_r, 0.0)), 0.0)
    q = q * (HD ** -0.5)
    kb = k * beta
    L = jnp.where(strict, nt(kb, k) * decay, 0.0)
    egc = jnp.exp(gc)
    P = eye - L
    M = hnn(L, L)
    for step in range(5):
        P = P + hnn(P, M)
        if step < 4:
            M = hnn(M, M)
    u = hnn(P, v * beta)
    w = hnn(P, kb * egc)
    intra = jnp.where(causal, nt(q, k) * decay, 0.0)
    qg = q * egc
    kdec = k * jnp.exp(gl - gc)
    egl = jnp.exp(gl)
    outs = []
    for ci in range(B // GDN_H):
        sl = slice(ci * GDN_H, (ci + 1) * GDN_H)
        v_new = u[sl] - nn(w[sl], S)
        outs.append(nn(qg[sl], S) + nn(intra[sl], v_new))
        S = S * egl[sl] + tn(kdec[sl], v_new)
    o = jnp.concatenate(outs, axis=0)
    r = lax.rsqrt(jnp.mean(o * o, axis=-1, keepdims=True) + RMS_EPS)
    return o * r * gnorm * _silu(z), S


def _gdn_batches(qkv, ba, z, alog_row, dt_row):
    C = GDN_CHUNK
    q, k, v, zz, b_raw, a_raw, alog, dtb = ([] for _ in range(8))
    for ci in range(GDN_STEP):
        rows = slice(ci * C, (ci + 1) * C)
        for h in range(GDN_H):
            q.append(qkv[rows, h * HD:(h + 1) * HD])
            k.append(qkv[rows, GDN_W + h * HD:GDN_W + (h + 1) * HD])
            v.append(qkv[rows, 2 * GDN_W + h * HD:2 * GDN_W + (h + 1) * HD])
            zz.append(z[rows, h * HD:(h + 1) * HD])
            b_raw.append(ba[rows, h:h + 1])
            a_raw.append(ba[rows, GDN_H + h:GDN_H + h + 1])
            alog.append(alog_row[:, h:h + 1])
            dtb.append(dt_row[:, h:h + 1])
    return tuple(jnp.stack(t) for t in (q, k, v, zz, b_raw, a_raw, alog, dtb))


def gdn_chunk_fwd(qkv, proj, alog_row, dt_row, gnorm, name):
    T = qkv.shape[0]
    R = GDN_CHUNK * GDN_STEP
    N = T // R
    dots, hdots = _dot_family(_bdot, False), _dot_family(_dot3, False)

    def body(qkv_ref, ba_ref, z_ref, al_ref, dt_ref, gn_ref, o_ref, save_ref, S_ref):
        n = pl.program_id(0)

        @pl.when(n == 0)
        def _():
            S_ref[...] = jnp.zeros_like(S_ref)

        S = S_ref[...]
        save_ref[0] = S
        batches = _gdn_batches(qkv_ref[...], ba_ref[...], z_ref[...], al_ref[...], dt_ref[...])
        o, S_new = _gdn_step(dots, hdots, S, *batches, gn_ref[...])
        S_ref[...] = S_new
        for ci in range(GDN_STEP):
            for h in range(GDN_H):
                o_ref[ci * GDN_CHUNK:(ci + 1) * GDN_CHUNK, h * HD:(h + 1) * HD] = o[ci * GDN_H + h].astype(BF16)

    return _pcall(
        body, name=name, grid=(N,),
        in_specs=[pl.BlockSpec((R, 3 * GDN_W), lambda n: (n, 0)),
                  pl.BlockSpec((R, LANE), lambda n: (n, (4 * GDN_W + POOL_G * HD) // LANE)),
                  pl.BlockSpec((R, GDN_W), lambda n: (n, 3)),
                  _row_spec(LANE), _row_spec(LANE), _row_spec(HD)],
        out_specs=[pl.BlockSpec((R, GDN_W), lambda n: (n, 0)),
                   pl.BlockSpec((1, GDN_H, HD, HD), lambda n: (n, 0, 0, 0))],
        out_shape=[jax.ShapeDtypeStruct((T, GDN_W), BF16), jax.ShapeDtypeStruct((N, GDN_H, HD, HD), F32)],
        scratch_shapes=[pltpu.VMEM((GDN_H, HD, HD), F32)],
        compiler_params=_params(1))(qkv, proj, proj, alog_row, dt_row, gnorm)


def gdn_chunk_bwd(qkv, proj, alog_row, dt_row, gnorm, saved, docat, name):
    T = qkv.shape[0]
    C = GDN_CHUNK
    R = C * GDN_STEP
    N = T // R
    dots, hdots = _dot_family(_bdot, True), _dot_family(_dot3, True)

    def body(qkv_ref, ba_ref, z_ref, al_ref, dt_ref, gn_ref, save_ref, do_ref,
             dqkv_ref, dz_ref, dba_ref, dal_ref, ddt_ref, dgn_ref, dS_ref):
        n = pl.program_id(0)

        @pl.when(n == 0)
        def _():
            dS_ref[...] = jnp.zeros_like(dS_ref)
            dal_ref[...] = jnp.zeros_like(dal_ref)
            ddt_ref[...] = jnp.zeros_like(ddt_ref)
            dgn_ref[...] = jnp.zeros_like(dgn_ref)

        batches = _gdn_batches(qkv_ref[...], ba_ref[...], z_ref[...], al_ref[...], dt_ref[...])
        do = do_ref[...]
        do_b = jnp.stack([do[ci * C:(ci + 1) * C, h * HD:(h + 1) * HD]
                          for ci in range(GDN_STEP) for h in range(GDN_H)])
        fn = functools.partial(_gdn_step, dots, hdots)
        _, vjp = jax.vjp(fn, save_ref[0], *batches, gn_ref[...])
        dS, dq, dk, dv, dz, db_raw, da_raw, dalog, ddtb, dgn = vjp((do_b, dS_ref[...]))
        dS_ref[...] = dS
        lane = lax.broadcasted_iota(jnp.int32, (1, LANE), 1)
        dal = jnp.zeros((1, LANE), F32)
        ddt = jnp.zeros((1, LANE), F32)
        for ci in range(GDN_STEP):
            rows = slice(ci * C, (ci + 1) * C)
            dba = jnp.zeros((C, LANE), F32)
            for h in range(GDN_H):
                b = ci * GDN_H + h
                dqkv_ref[rows, h * HD:(h + 1) * HD] = dq[b]
                dqkv_ref[rows, GDN_W + h * HD:GDN_W + (h + 1) * HD] = dk[b]
                dqkv_ref[rows, 2 * GDN_W + h * HD:2 * GDN_W + (h + 1) * HD] = dv[b]
                dz_ref[rows, h * HD:(h + 1) * HD] = dz[b].astype(BF16)
                hot_b = (lane == h).astype(F32)
                dba = dba + db_raw[b] * hot_b + da_raw[b] * (lane == GDN_H + h).astype(F32)
                dal = dal + dalog[b] * hot_b
                ddt = ddt + ddtb[b] * hot_b
            dba_ref[rows, :] = dba.astype(BF16)
        dal_ref[...] += dal
        ddt_ref[...] += ddt
        dgn_ref[...] += dgn

    rev = lambda n: N - 1 - n
    row = jax.ShapeDtypeStruct((1, LANE), F32)
    return _pcall(
        body, name=name, grid=(N,),
        in_specs=[pl.BlockSpec((R, 3 * GDN_W), lambda n: (rev(n), 0)),
                  pl.BlockSpec((R, LANE), lambda n: (rev(n), (4 * GDN_W + POOL_G * HD) // LANE)),
                  pl.BlockSpec((R, GDN_W), lambda n: (rev(n), 3)),
                  _row_spec(LANE), _row_spec(LANE), _row_spec(HD),
                  pl.BlockSpec((1, GDN_H, HD, HD), lambda n: (rev(n), 0, 0, 0)),
                  pl.BlockSpec((R, GDN_W), lambda n: (rev(n), 0))],
        out_specs=[pl.BlockSpec((R, 3 * GDN_W), lambda n: (rev(n), 0)),
                   pl.BlockSpec((R, GDN_W), lambda n: (rev(n), 0)),
                   pl.BlockSpec((R, LANE), lambda n: (rev(n), 0)),
                   _row_spec(LANE), _row_spec(LANE), _row_spec(HD)],
        out_shape=[jax.ShapeDtypeStruct((T, 3 * GDN_W), F32), jax.ShapeDtypeStruct((T, GDN_W), BF16),
                   jax.ShapeDtypeStruct((T, LANE), BF16), row, row, jax.ShapeDtypeStruct((1, HD), F32)],
        scratch_shapes=[pltpu.VMEM((GDN_H, HD, HD), F32)],
        compiler_params=_params(1))(qkv, proj, proj, alog_row, dt_row, gnorm, saved, docat)


POOL_HALO = 16


def _pool_pick(j, s2, s4, s8, s16):
    return jnp.where(j == 0, s2, jnp.where(j == 1, s4, jnp.where(j == 2, s8, s16)))


def _pool_count(j, t0, rows):
    t1 = (t0 + 1 + lax.broadcasted_iota(jnp.int32, (rows, 1), 0)).astype(F32)
    win = jnp.where(j == 0, 2.0, jnp.where(j == 1, 4.0, jnp.where(j == 2, 8.0, 16.0)))
    return jnp.minimum(t1, win)


def _pooled(p, prev, i, j, tb):
    ext = jnp.concatenate([prev, p], axis=0)
    s2 = ext + _back(ext, 1)
    s4 = s2 + _back(s2, 2)
    s8 = s4 + _back(s4, 4)
    s16 = s8 + _back(s8, 8)
    s = _pool_pick(j, s2, s4, s8, s16)[POOL_HALO:]
    return s / _pool_count(j, i * tb, tb) - p


def pool_fwd(proj, pool_w, pool_scale, name):
    T = proj.shape[0]
    tb = _rows(T, 512)
    c0 = 4 * GDN_H

    def body(p_ref, pp_ref, w_ref, s_ref, o_ref):
        i, j = pl.program_id(0), pl.program_id(1)
        p = p_ref[...]
        prev = jnp.where(i > 0, pp_ref[...], 0.0)
        pooled = _pooled(p, prev, i, j, tb)
        o_ref[...] = (_bdot(pooled, w_ref[0], NN) * s_ref[...]).astype(BF16)

    return _pcall(
        body, name=name, grid=(T // tb, POOL_G),
        in_specs=[pl.BlockSpec((tb, HD), lambda i, j: (i, c0 + j)),
                  pl.BlockSpec((POOL_HALO, HD), lambda i, j: (jnp.maximum(i * (tb // POOL_HALO) - 1, 0), c0 + j)),
                  pl.BlockSpec((1, HD, HD), lambda i, j: (j, 0, 0)),
                  pl.BlockSpec((1, HD), lambda i, j: (0, j))],
        out_specs=pl.BlockSpec((tb, HD), lambda i, j: (i, j)),
        out_shape=jax.ShapeDtypeStruct((T, POOL_G * HD), BF16),
        compiler_params=_params(2))(proj, proj, pool_w, pool_scale)


def pool_bwd(proj, pool_w, pool_scale, docat, name):
    T = proj.shape[0]
    tb = _rows(T, 512)
    nrb = T // tb
    c0 = 4 * GDN_H
    HB = POOL_HALO

    def body(p_ref, pp_ref, w_ref, s_ref, d_ref, dn_ref, dp_ref, dw_ref, ds_ref):
        j, i = pl.program_id(0), pl.program_id(1)
        p = p_ref[...]
        prev = jnp.where(i > 0, pp_ref[...], 0.0)
        pooled = _pooled(p, prev, i, j, tb)
        w, scale = w_ref[0], s_ref[...]
        dy = d_ref[...]
        dnext = jnp.where(i < nrb - 1, dn_ref[...], 0.0)
        dyp = jnp.concatenate([dy, dnext], axis=0) * scale
        dpooled = _bdot(dyp, w, NT)
        qn = dpooled / _pool_count(j, i * tb, tb + HB)
        a2 = qn + _ahead(qn, 1)
        a4 = a2 + _ahead(a2, 2)
        a8 = a4 + _ahead(a4, 4)
        a16 = a8 + _ahead(a8, 8)
        dp_ref[...] = (_pool_pick(j, a2, a4, a8, a16) - dpooled)[:tb].astype(BF16)

        @pl.when(i == 0)
        def _():
            dw_ref[...] = jnp.zeros_like(dw_ref)
            ds_ref[...] = jnp.zeros_like(ds_ref)

        dw_ref[0] += _bdot(pooled, dyp[:tb], TN)
        ds_ref[...] += jnp.sum(dy * _bdot(pooled, w, NN), axis=0, keepdims=True)

    return _pcall(
        body, name=name, grid=(POOL_G, nrb),
        in_specs=[pl.BlockSpec((tb, HD), lambda j, i: (i, c0 + j)),
                  pl.BlockSpec((HB, HD), lambda j, i: (jnp.maximum(i * (tb // HB) - 1, 0), c0 + j)),
                  pl.BlockSpec((1, HD, HD), lambda j, i: (j, 0, 0)),
                  pl.BlockSpec((1, HD), lambda j, i: (0, j)),
                  pl.BlockSpec((tb, HD), lambda j, i: (i, POOL_G + j)),
                  pl.BlockSpec((HB, HD), lambda j, i: (jnp.minimum(i + 1, nrb - 1) * (tb // HB), POOL_G + j))],
        out_specs=[pl.BlockSpec((tb, HD), lambda j, i: (i, j)),
                   pl.BlockSpec((1, HD, HD), lambda j, i: (j, 0, 0)),
                   pl.BlockSpec((1, HD), lambda j, i: (0, j))],
        out_shape=[jax.ShapeDtypeStruct((T, POOL_G * HD), BF16),
                   jax.ShapeDtypeStruct((POOL_G, HD, HD), F32),
                   jax.ShapeDtypeStruct((1, POOL_G * HD), F32)],
        compiler_params=_params(2))(proj, proj, pool_w, pool_scale, docat, docat)


ATT_W = ATT_H * HD
GROUP_COLS = 3 * ATT_W


def to_residue_major(t, d):
    if d == 1:
        return t
    T, C = t.shape
    return t.reshape(T // d, d, C).transpose(1, 0, 2).reshape(T, C)


def to_token_order(t, d):
    if d == 1:
        return t
    T, C = t.shape
    return t.reshape(d, T // d, C).transpose(1, 0, 2).reshape(T, C)


def headnorm_fwd(proj, qk_gain, name):
    T = proj.shape[0]
    tb = _rows(T, 256)

    def body(x_ref, g_ref, o_ref):
        g = g_ref[...]
        for h in range(2 * ATT_H):
            sl = slice(h * HD, (h + 1) * HD)
            x = x_ref[:, sl]
            n = x * lax.rsqrt(jnp.mean(x * x, axis=-1, keepdims=True) + RMS_EPS)
            gain = g[0:1] * (HD ** -0.5) if h < ATT_H else g[1:2]
            o_ref[:, sl] = (n * gain).astype(BF16)
        o_ref[:, 2 * ATT_W:] = x_ref[:, 2 * ATT_W:].astype(BF16)

    blk = pl.BlockSpec((tb, GROUP_COLS), lambda i: (i, 0))
    return _pcall(
        body, name=name, grid=(T // tb,),
        in_specs=[blk, pl.BlockSpec((SUBLANE, HD), lambda i: (0, 0))],
        out_specs=blk, out_shape=jax.ShapeDtypeStruct((T, GROUP_COLS), BF16),
        compiler_params=_params(1))(proj, qk_gain)


def headnorm_bwd(proj, qk_gain, dq, dk, dv, name):
    T = proj.shape[0]
    tb = _rows(T, 256)

    def body(x_ref, g_ref, dq_ref, dk_ref, dv_ref, dx_ref, dg_ref):
        i = pl.program_id(0)
        g = g_ref[...]

        @pl.when(i == 0)
        def _():
            dg_ref[...] = jnp.zeros_like(dg_ref)

        for part, d_ref in enumerate((dq_ref, dk_ref)):
            gain = g[0:1] * (HD ** -0.5) if part == 0 else g[1:2]
            scale = (HD ** -0.5) if part == 0 else 1.0
            acc = jnp.zeros((1, HD), F32)
            for h in range(ATT_H):
                x = x_ref[:, part * ATT_W + h * HD:part * ATT_W + (h + 1) * HD]
                d = d_ref[:, h * HD:(h + 1) * HD]
                r = lax.rsqrt(jnp.mean(x * x, axis=-1, keepdims=True) + RMS_EPS)
                n = x * r
                dn = d * gain
                dx = r * (dn - n * jnp.mean(dn * n, axis=-1, keepdims=True))
                dx_ref[:, part * ATT_W + h * HD:part * ATT_W + (h + 1) * HD] = dx.astype(BF16)
                acc = acc + jnp.sum(d * n, axis=0, keepdims=True)
            dg_ref[part:part + 1, :] += acc * scale
        dx_ref[:, 2 * ATT_W:] = dv_ref[...].astype(BF16)

    blk = pl.BlockSpec((tb, GROUP_COLS), lambda i: (i, 0))
    dblk = pl.BlockSpec((tb, ATT_W), lambda i: (i, 0))
    gspec = pl.BlockSpec((SUBLANE, HD), lambda i: (0, 0))
    return _pcall(
        body, name=name, grid=(T // tb,),
        in_specs=[blk, gspec, dblk, dblk, dblk],
        out_specs=[blk, gspec],
        out_shape=[jax.ShapeDtypeStruct((T, GROUP_COLS), BF16), jax.ShapeDtypeStruct((SUBLANE, HD), F32)],
        compiler_params=_params(1))(proj, qk_gain, dq, dk, dv)


def _att_scores(q, k, slope, n_ok, prev):
    a = lax.broadcasted_iota(jnp.int32, (ATT_BLK, ATT_BLK), 0)
    j = lax.broadcasted_iota(jnp.int32, (ATT_BLK, ATT_BLK), 1)
    rel = (ATT_BLK + a - j) if prev else (a - j)
    mask = ((j >= a) & n_ok) if prev else (j <= a)
    s = _bdot(q, k, NT) - slope * rel.astype(F32)
    return jnp.where(mask, s, NEG), mask


def _att_scores_t(k, q, slope, n_ok, nxt):
    j = lax.broadcasted_iota(jnp.int32, (ATT_BLK, ATT_BLK), 0)
    a = lax.broadcasted_iota(jnp.int32, (ATT_BLK, ATT_BLK), 1)
    rel = (ATT_BLK + a - j) if nxt else (a - j)
    mask = ((j >= a) & n_ok) if nxt else (j <= a)
    s = _bdot(k, q, NT) - slope * rel.astype(F32)
    return jnp.where(mask, s, NEG), mask


def _att_blocks(nb, width, shift):
    def make(col):
        return pl.BlockSpec((ATT_BLK, width),
                            lambda r, n: (r * nb + jnp.clip(n + shift, 0, nb - 1), col))
    return make


def _lane_col(cols):
    lane = lax.broadcasted_iota(jnp.int32, (1, LANE), 1)
    out = jnp.zeros((ATT_BLK, LANE), F32)
    for h, c in enumerate(cols):
        out = out + c * (lane == h).astype(F32)
    return out


def att_fwd(qkvn, gi, name):
    T = qkvn.shape[0]
    dil = DIL[gi]
    nb = T // dil // ATT_BLK

    def body(q_ref, kp_ref, kc_ref, vp_ref, vc_ref, o_ref, l_ref):
        n_ok = pl.program_id(1) > 0
        lses = []
        for h in range(ATT_H):
            sl = slice(h * HD, (h + 1) * HD)
            slope = (2.0 ** -(h + 1)) * dil
            q = q_ref[:, sl]
            s_c, _ = _att_scores(q, kc_ref[:, sl], slope, n_ok, False)
            s_p, _ = _att_scores(q, kp_ref[:, sl], slope, n_ok, True)
            m = jnp.maximum(jnp.max(s_c, axis=-1, keepdims=True), jnp.max(s_p, axis=-1, keepdims=True))
            p_c, p_p = jnp.exp(s_c - m), jnp.exp(s_p - m)
            l = jnp.sum(p_c, axis=-1, keepdims=True) + jnp.sum(p_p, axis=-1, keepdims=True)
            o = _bdot(p_c, vc_ref[:, sl], NN) + _bdot(p_p, vp_ref[:, sl], NN)
            o_ref[:, sl] = o / l
            lses.append(m + jnp.log(l))
        l_ref[...] = _lane_col(lses)

    cur, prv = _att_blocks(nb, ATT_W, 0), _att_blocks(nb, ATT_W, -1)
    return _pcall(
        body, name=name, grid=(dil, nb), in_specs=[cur(0), prv(1), cur(1), prv(2), cur(2)],
        out_specs=[cur(0), _att_blocks(nb, LANE, 0)(0)],
        out_shape=[jax.ShapeDtypeStruct((T, ATT_W), F32), jax.ShapeDtypeStruct((T, LANE), F32)],
        compiler_params=_params(2))(qkvn, qkvn, qkvn, qkvn, qkvn)


def att_merge(os, lses, name):
    T = os[0].shape[0]
    tb = _rows(T, 512)

    def body(o0, o1, o2, l0, l1, l2, o_ref, l_ref):
        a, b, c = l0[...], l1[...], l2[...]
        m = jnp.maximum(a, jnp.maximum(b, c))
        wa, wb, wc = jnp.exp(a - m), jnp.exp(b - m), jnp.exp(c - m)
        den = wa + wb + wc
        l_ref[...] = m + jnp.log(den)
        wa, wb, wc = wa / den, wb / den, wc / den
        for h in range(ATT_H):
            sl = slice(h * HD, (h + 1) * HD)
            o_ref[:, sl] = (wa[:, h:h + 1] * o0[:, sl] + wb[:, h:h + 1] * o1[:, sl]
                            + wc[:, h:h + 1] * o2[:, sl])

    blk = pl.BlockSpec((tb, ATT_W), lambda i: (i, 0))
    lblk = pl.BlockSpec((tb, LANE), lambda i: (i, 0))
    return _pcall(
        body, name=name, grid=(T // tb,), in_specs=[blk] * 3 + [lblk] * 3, out_specs=[blk, lblk],
        out_shape=[jax.ShapeDtypeStruct((T, ATT_W), F32), jax.ShapeDtypeStruct((T, LANE), F32)],
        compiler_params=_params(1))(*os, *lses)


def att_delta(do, o, name):
    T = do.shape[0]
    tb = _rows(T, 512)

    def body(d_ref, o_ref, out_ref):
        lane = lax.broadcasted_iota(jnp.int32, (1, LANE), 1)
        out = jnp.zeros((tb, LANE), F32)
        for h in range(ATT_H):
            sl = slice(h * HD, (h + 1) * HD)
            s = jnp.sum(d_ref[:, sl] * o_ref[:, sl], axis=-1, keepdims=True)
            out = out + s * (lane == h).astype(F32)
        out_ref[...] = out

    blk = pl.BlockSpec((tb, ATT_W), lambda i: (i, 0))
    return _pcall(
        body, name=name, grid=(T // tb,), in_specs=[blk, blk],
        out_specs=pl.BlockSpec((tb, LANE), lambda i: (i, 0)),
        out_shape=jax.ShapeDtypeStruct((T, LANE), F32), compiler_params=_params(1))(do, o)


def att_bwd_q(qkvn, do, lse, delta, gi, name):
    T = qkvn.shape[0]
    dil = DIL[gi]
    nb = T // dil // ATT_BLK

    def body(q_ref, kp_ref, kc_ref, vp_ref, vc_ref, do_ref, l_ref, d_ref, dq_ref):
        n_ok = pl.program_id(1) > 0
        lse, dl = l_ref[...], d_ref[...]
        for h in range(ATT_H):
            sl = slice(h * HD, (h + 1) * HD)
            slope = (2.0 ** -(h + 1)) * dil
            q, do_h = q_ref[:, sl], do_ref[:, sl]
            dq = jnp.zeros((ATT_BLK, HD), F32)
            for k_ref, v_ref, prev in ((kc_ref, vc_ref, False), (kp_ref, vp_ref, True)):
                s, mask = _att_scores(q, k_ref[:, sl], slope, n_ok, prev)
                p = jnp.where(mask, jnp.exp(s - lse[:, h:h + 1]), 0.0)
                ds = p * (_bdot(do_h, v_ref[:, sl], NT) - dl[:, h:h + 1])
                dq = dq + _bdot(ds, k_ref[:, sl], NN)
            dq_ref[:, sl] = dq

    cur, prv = _att_blocks(nb, ATT_W, 0), _att_blocks(nb, ATT_W, -1)
    small = _att_blocks(nb, LANE, 0)(0)
    return _pcall(
        body, name=name, grid=(dil, nb),
        in_specs=[cur(0), prv(1), cur(1), prv(2), cur(2), cur(0), small, small],
        out_specs=cur(0), out_shape=jax.ShapeDtypeStruct((T, ATT_W), F32),
        compiler_params=_params(2))(qkvn, qkvn, qkvn, qkvn, qkvn, do, lse, delta)


def att_bwd_kv(qkvn, do, lse, delta, gi, name):
    T = qkvn.shape[0]
    dil = DIL[gi]
    nb = T // dil // ATT_BLK

    def body(k_ref, v_ref, q0_ref, q1_ref, do0_ref, do1_ref, l0_ref, l1_ref, d0_ref, d1_ref,
             dk_ref, dv_ref):
        n_ok = pl.program_id(1) < nb - 1
        stats = ((l0_ref[...].T, d0_ref[...].T), (l1_ref[...].T, d1_ref[...].T))
        for h in range(ATT_H):
            sl = slice(h * HD, (h + 1) * HD)
            slope = (2.0 ** -(h + 1)) * dil
            k, v = k_ref[:, sl], v_ref[:, sl]
            dk = jnp.zeros((ATT_BLK, HD), F32)
            dv = jnp.zeros((ATT_BLK, HD), F32)
            for q_ref, do_ref, (lse_t, dl_t), nxt in ((q0_ref, do0_ref, stats[0], False),
                                                      (q1_ref, do1_ref, stats[1], True)):
                q, do_h = q_ref[:, sl], do_ref[:, sl]
                s, mask = _att_scores_t(k, q, slope, n_ok, nxt)
                p = jnp.where(mask, jnp.exp(s - lse_t[h:h + 1, :]), 0.0)
                dv = dv + _bdot(p, do_h, NN)
                ds = p * (_bdot(v, do_h, NT) - dl_t[h:h + 1, :])
                dk = dk + _bdot(ds, q, NN)
            dk_ref[:, sl] = dk
            dv_ref[:, sl] = dv

    cur, nxt = _att_blocks(nb, ATT_W, 0), _att_blocks(nb, ATT_W, 1)
    s0, s1 = _att_blocks(nb, LANE, 0)(0), _att_blocks(nb, LANE, 1)(0)
    return _pcall(
        body, name=name, grid=(dil, nb),
        in_specs=[cur(1), cur(2), cur(0), nxt(0), cur(0), nxt(0), s0, s1, s0, s1],
        out_specs=[cur(0), cur(0)], out_shape=[jax.ShapeDtypeStruct((T, ATT_W), F32)] * 2,
        compiler_params=_params(2))(qkvn, qkvn, qkvn, qkvn, do, do, lse, lse, delta, delta)


def adamw(w, g, m, v, name):
    shape = w.shape
    C = shape[-1]
    R = math.prod(shape[:-1])
    to2d = lambda t: t.reshape(R, C)
    tb = _rows(R, max(16, (256 * 1536 // C) // 16 * 16))
    c1 = 1.0 - ADAM_B1 ** ADAM_STEP
    c2 = 1.0 - ADAM_B2 ** ADAM_STEP

    def body(w_ref, g_ref, m_ref, v_ref, d_ref, nm_ref, nv_ref):
        gv = g_ref[...]
        nm = ADAM_B1 * m_ref[...] + (1.0 - ADAM_B1) * gv
        nv = ADAM_B2 * v_ref[...] + (1.0 - ADAM_B2) * (gv * gv)
        d_ref[...] = -ADAM_LR * ((nm / c1) / (jnp.sqrt(nv / c2) + ADAM_EPS) + ADAM_WD * w_ref[...])
        nm_ref[...] = nm
        nv_ref[...] = nv

    blk = pl.BlockSpec((tb, C), lambda i: (i, 0))
    out = jax.ShapeDtypeStruct((R, C), F32)
    d, nm, nv = _pcall(
        body, name=name, grid=(R // tb,), in_specs=[blk] * 4, out_specs=[blk] * 3,
        out_shape=[out, out, out], compiler_params=_params(1))(to2d(w), to2d(g), to2d(m), to2d(v))
    return d.reshape(shape), nm.reshape(shape), nv.reshape(shape)


def _pad_rows8(w):
    return jnp.pad(w, ((0, SUBLANE - w.shape[0]), (0, 0)))


def _lane_row(v):
    return jnp.pad(v, (0, LANE - v.shape[0]))[None, :]


def _even_reorder(w_in):
    z4 = 4 * GDN_W
    pad = jnp.zeros((w_in.shape[0], EVEN_PAD - EVEN_COLS), w_in.dtype)
    return jnp.concatenate([w_in[:, :z4], w_in[:, z4 + 2 * GDN_H:], w_in[:, z4:z4 + 2 * GDN_H], pad], axis=1)


def _even_restore(dw):
    z4 = 4 * GDN_W
    p4 = POOL_G * HD
    return jnp.concatenate([dw[:, :z4], dw[:, z4 + p4:z4 + p4 + 2 * GDN_H], dw[:, z4:z4 + p4]], axis=1)


def _ffn_fwd(tag, x1, mod_f, wl):
    sh, sc, g = mod_f
    hf = modnorm_fwd(x1, wl["norm_ffn"], sc, sh, f"{tag}_ffn_norm")
    up = matmul(hf, wl["ffn_w_up"], "nn", F32, f"{tag}_ffn_up")
    act = ffn_mid_fwd(up, wl["ffn_conv_w8"], wl["ffn_conv_b"], f"{tag}_ffn_mid")
    f = matmul(act, wl["ffn_w_down"], "nn", F32, f"{tag}_ffn_down")
    x2 = gres_fwd(x1, g, f, f"{tag}_ffn_res")
    return x2, (x1, hf, up, act, f)


def _ffn_bwd(tag, dx2, saved, mod_f, wl):
    x1, hf, up, act, f = saved
    sh, sc, g = mod_f
    df, dg = gres_bwd(dx2, g, f, f"{tag}_ffn_res_bwd")
    dact = matmul(df, wl["ffn_w_down"], "nt", F32, f"{tag}_ffn_down_da")
    wl["on_grad"]("ffn_w_down", matmul(act, df, "tn", F32, f"{tag}_ffn_down_dw", pieces="row"))
    dgate, dval, dcw, dcb = ffn_mid_bwd(up, wl["ffn_conv_w8"], wl["ffn_conv_b"], dact, f"{tag}_ffn_mid_bwd")
    dup = jnp.concatenate([dgate, dval], axis=1)
    dhf = matmul(dup, wl["ffn_w_up"], "nt", F32, f"{tag}_ffn_up_da")
    wl["on_grad"]("ffn_w_up", matmul(hf, dup, "tn", F32, f"{tag}_ffn_up_dw", pieces="col"))
    dx1, dgain, dsc, dsh = modnorm_bwd(x1, wl["norm_ffn"], sc, dhf, dx2, f"{tag}_ffn_norm_bwd")
    grads = {"norm_ffn": dgain[0], "ffn_conv_w": dcw[:FFN_CONV], "ffn_conv_b": dcb[0]}
    return dx1, (dsh, dsc, dg), grads


def _even_fwd(tag, x, mod_m, wl):
    sh, sc, g = mod_m
    hm = modnorm_fwd(x, wl["norm_mix"], sc, sh, f"{tag}_mix_norm")
    proj = matmul(hm, wl["w_in"], "nn", F32, f"{tag}_ev_in")
    qkv = gdn_conv_fwd(proj, wl["gdn_conv_w8"], f"{tag}_gdn_conv")
    o_a, states = gdn_chunk_fwd(qkv, proj, wl["alog_row"], wl["dt_row"], wl["gdn_norm"], f"{tag}_gdn_chunk")
    o_b = pool_fwd(proj, wl["pool_w"], wl["pool_scale"], f"{tag}_pool")
    ocat = jnp.concatenate([o_a, o_b], axis=1)
    y = matmul(ocat, wl["w_out"], "nn", F32, f"{tag}_ev_out")
    x1 = gres_fwd(x, g, y, f"{tag}_mix_res")
    return x1, (x, hm, proj, qkv, states, ocat, y)


def _even_bwd(tag, dx1, saved, mod_m, wl):
    x, hm, proj, qkv, states, ocat, y = saved
    sh, sc, g = mod_m
    dy, dg = gres_bwd(dx1, g, y, f"{tag}_mix_res_bwd")
    docat = matmul(dy, wl["w_out"], "nt", F32, f"{tag}_ev_out_da")
    wl["on_grad"]("ev_w_out", matmul(ocat, dy, "tn", F32, f"{tag}_ev_out_dw", pieces="row"))
    dqkv, dz, dba, dalog, ddt, dgn = gdn_chunk_bwd(
        qkv, proj, wl["alog_row"], wl["dt_row"], wl["gdn_norm"], states, docat, f"{tag}_gdn_chunk_bwd")
    dxc, dconv = gdn_conv_bwd(proj, wl["gdn_conv_w8"], dqkv, f"{tag}_gdn_conv_bwd")
    dp, dpw, dps = pool_bwd(proj, wl["pool_w"], wl["pool_scale"], docat, f"{tag}_pool_bwd")
    dproj = jnp.concatenate([dxc, dz, dp, dba], axis=1)
    dhm = matmul(dproj, wl["w_in"], "nt", F32, f"{tag}_ev_in_da")
    wl["on_grad"]("ev_w_in", col_pieces(_even_restore(matmul(hm, dproj, "tn", F32, f"{tag}_ev_in_dw"))))
    dx, dgain, dsc, dsh = modnorm_bwd(x, wl["norm_mix"], sc, dhm, dx1, f"{tag}_mix_norm_bwd")
    grads = {"norm_mix": dgain[0], "gdn_conv_w": dconv[:GDN_CONV], "gdn_a_log": dalog[0, :GDN_H], "gdn_dt_bias": ddt[0, :GDN_H],
             "gdn_norm": dgn[0], "pool_w": dpw, "pool_scale": dps[0]}
    return dx, (dsh, dsc, dg), grads


def _odd_fwd(tag, x, mod_m, wl):
    sh, sc, g = mod_m
    hm = modnorm_fwd(x, wl["norm_mix"], sc, sh, f"{tag}_mix_norm")
    projs, qkvns, outs, lses = [], [], [], []
    for gi, d in enumerate(DIL):
        w_g = wl["w_in"][:, gi * GROUP_COLS:(gi + 1) * GROUP_COLS]
        proj = matmul(to_residue_major(hm, d), w_g, "nn", F32, f"{tag}_od_in{gi}")
        qkvn = headnorm_fwd(proj, wl["qk_gain8"], f"{tag}_headnorm{gi}")
        o_g, l_g = att_fwd(qkvn, gi, f"{tag}_att{gi}")
        projs.append(proj)
        qkvns.append(qkvn)
        outs.append(to_token_order(o_g, d))
        lses.append(to_token_order(l_g, d))
    o, lse = att_merge(outs, lses, f"{tag}_att_merge")
    y = matmul(o, wl["w_out"], "nn", F32, f"{tag}_od_out")
    x1 = gres_fwd(x, g, y, f"{tag}_mix_res")
    return x1, (x, hm, projs, qkvns, o, lse, y)


def _odd_bwd(tag, dx1, saved, mod_m, wl):
    x, hm, projs, qkvns, o, lse, y = saved
    sh, sc, g = mod_m
    dy, dg = gres_bwd(dx1, g, y, f"{tag}_mix_res_bwd")
    do = matmul(dy, wl["w_out"], "nt", F32, f"{tag}_od_out_da")
    wl["on_grad"]("od_w_out", matmul(o, dy, "tn", F32, f"{tag}_od_out_dw", pieces="row"))
    delta = att_delta(do, o, f"{tag}_att_delta")
    dhm, dw_in, dgain_qk = None, [], None
    for gi, d in enumerate(DIL):
        w_g = wl["w_in"][:, gi * GROUP_COLS:(gi + 1) * GROUP_COLS]
        do_g, lse_g, dl_g = (to_residue_major(t, d) for t in (do, lse, delta))
        dq = att_bwd_q(qkvns[gi], do_g, lse_g, dl_g, gi, f"{tag}_att{gi}_dq")
        dk, dv = att_bwd_kv(qkvns[gi], do_g, lse_g, dl_g, gi, f"{tag}_att{gi}_dkv")
        dproj, dgain = headnorm_bwd(projs[gi], wl["qk_gain8"], dq, dk, dv, f"{tag}_headnorm{gi}_bwd")
        dhm_g = to_token_order(matmul(dproj, w_g, "nt", F32, f"{tag}_od_in{gi}_da"), d)
        dw_in.append(matmul(to_residue_major(hm, d), dproj, "tn", F32, f"{tag}_od_in{gi}_dw"))
        dhm = dhm_g if dhm is None else dhm + dhm_g
        dgain_qk = dgain if dgain_qk is None else dgain_qk + dgain
    wl["on_grad"]("od_w_in", col_pieces(jnp.concatenate(dw_in, axis=1)))
    dx, dgain, dsc, dsh = modnorm_bwd(x, wl["norm_mix"], sc, dhm, dx1, f"{tag}_mix_norm_bwd")
    grads = {"norm_mix": dgain[0], "att_q_norm": dgain_qk[0], "att_k_norm": dgain_qk[1]}
    return dx, (dsh, dsc, dg), grads


def col_pieces(dw):
    M, N = dw.shape
    return dw.reshape(2, M // 2, N_CHIPS, N // N_CHIPS).transpose(0, 2, 1, 3)


def _layer_weights(i, W, big, on_grad):
    e = i // 2
    wl = {"norm_mix": W["norm_mix"][i][None, :], "norm_ffn": W["norm_ffn"][i][None, :],
          "ffn_w_up": big("ffn_w_up", i), "ffn_w_down": big("ffn_w_down", i),
          "ffn_conv_w8": _pad_rows8(W["ffn_conv_w"][i]), "ffn_conv_b": W["ffn_conv_b"][i][None, :],
          "on_grad": lambda name, pieces: on_grad(name, i if name.startswith("ffn") else e, pieces)}
    if i % 2 == 0:
        wl.update({"w_in": _even_reorder(big("ev_w_in", e)), "w_out": big("ev_w_out", e),
                   "gdn_conv_w8": _pad_rows8(W["gdn_conv_w"][e]),
                   "alog_row": _lane_row(W["gdn_a_log"][e]), "dt_row": _lane_row(W["gdn_dt_bias"][e]),
                   "gdn_norm": W["gdn_norm"][e][None, :], "pool_w": W["pool_w"][e],
                   "pool_scale": W["pool_scale"][e][None, :]})
    else:
        wl.update({"w_in": big("od_w_in", e), "w_out": big("od_w_out", e),
                   "qk_gain8": _pad_rows8(jnp.stack([W["att_q_norm"][e], W["att_k_norm"][e]]))})
    return wl


def local_step(x, target, mod, W, big, on_grad):
    depth = mod.shape[0]
    row = lambda i, k: mod[i, k][None, :]
    saved, wls = [], []
    for i in range(depth):
        wl = _layer_weights(i, W, big, on_grad)
        mod_m = (row(i, 0), row(i, 1), row(i, 2))
        mod_f = (row(i, 3), row(i, 4), row(i, 5))
        fwd = _even_fwd if i % 2 == 0 else _odd_fwd
        x1, s_mix = fwd(f"l{i}", x, mod_m, wl)
        x, s_ffn = _ffn_fwd(f"l{i}", x1, mod_f, wl)
        saved.append((s_mix, s_ffn, mod_m, mod_f))
        wls.append(wl)
    sq, dx = loss_head(x, target, "loss_head")
    dmod, grads = [None] * depth, [None] * depth
    for i in reversed(range(depth)):
        s_mix, s_ffn, mod_m, mod_f = saved[i]
        dx, dmf, g_ffn = _ffn_bwd(f"l{i}", dx, s_ffn, mod_f, wls[i])
        bwd = _even_bwd if i % 2 == 0 else _odd_bwd
        dx, dmm, g_mix = bwd(f"l{i}", dx, s_mix, mod_m, wls[i])
        dmod[i] = jnp.concatenate([t for t in dmm + dmf], axis=0)
        grads[i] = {**g_mix, **g_ffn}
    return sq, dx, jnp.stack(dmod), grads


def _place():
    return lax.axis_index("x"), lax.axis_index("y"), lax.axis_index("c")


def _other_chips(mx, my):
    return [(1 - mx, my), (mx, 1 - my), (1 - mx, 1 - my)]


def _sems(n):
    return [DMA_SEM((n,)), DMA_SEM((n,)), DMA_SEM(())]


def gather8_ride(x, then):
    def parts(ins, outs, sems):
        (x_ref,), (out_ref,), (send_sems, recv_sems, local_sem) = ins, outs, sems
        mx, my, mc = _place()
        me, sibling = (mx, my, mc), (mx, my, 1 - mc)
        chips = _other_chips(mx, my)

        def slot(px, py, pc):
            return out_ref.at[4 * px + 2 * py + pc]

        def copy(k, block, to, src=None):
            return pltpu.make_async_remote_copy(
                src_ref=slot(*block) if src is None else src, dst_ref=slot(*block),
                send_sem=send_sems.at[k], recv_sem=recv_sems.at[k], device_id=to, device_id_type=MESH)

        mine = lambda: pltpu.make_async_copy(x_ref, slot(*me), local_sem)
        first = lambda: ([copy(0, me, sibling, src=x_ref)]
                         + [copy(1 + j, me, (*chip, mc), src=x_ref) for j, chip in enumerate(chips)])
        passed = lambda: [copy(4 + j, (*chip, mc), sibling) for j, chip in enumerate(chips)]
        landed = lambda: [copy(1 + j, (*chip, mc), me) for j, chip in enumerate(chips)]
        from_sibling = lambda: ([copy(0, sibling, me)]
                                + [copy(4 + j, (*chip, 1 - mc), me) for j, chip in enumerate(chips)])
        return mine, first, passed, landed, from_sibling

    def start(ins, outs, sems):
        mine, first, _, _, _ = parts(ins, outs, sems)
        mine().start()
        for cp in first():
            cp.start()

    def mid(ins, outs, sems):
        _, _, passed, landed, _ = parts(ins, outs, sems)
        for cp, fwd in zip(landed(), passed()):
            cp.wait_recv()
            fwd.start()

    def finish(ins, outs, sems):
        mine, first, passed, _, from_sibling = parts(ins, outs, sems)
        for cp in from_sibling():
            cp.wait_recv()
        for cp in first() + passed():
            cp.wait_send()
        mine().wait()

    return Ride([x], [jax.ShapeDtypeStruct((8,) + x.shape, x.dtype)], _sems(7), start, finish,
                lambda outs: then(outs[0]), mid=mid)


def all_gather8(x):
    box = []
    waiting = list(_RIDES)
    _RIDES[:] = [gather8_ride(x, box.append)]
    flush_rides()
    _RIDES[:] = waiting + _RIDES
    return box[0]


def sibling_halves_ride(p, then):
    def parts(ins, outs, sems):
        (p_ref,), (keep_ref, got_ref), (send_sems, recv_sems, local_sem) = ins, outs, sems
        mx, my, mc = _place()
        local = pltpu.make_async_copy(p_ref.at[mc], keep_ref, local_sem)
        cp = pltpu.make_async_remote_copy(src_ref=p_ref.at[1 - mc], dst_ref=got_ref, send_sem=send_sems.at[0],
                                          recv_sem=recv_sems.at[0], device_id=(mx, my, 1 - mc), device_id_type=MESH)
        return local, cp

    def start(ins, outs, sems):
        local, cp = parts(ins, outs, sems)
        local.start()
        cp.start()

    def finish(ins, outs, sems):
        local, cp = parts(ins, outs, sems)
        cp.wait_send()
        cp.wait_recv()
        local.wait()

    piece = jax.ShapeDtypeStruct(p.shape[1:], p.dtype)
    return Ride([p], [piece, piece], _sems(1), start, finish, lambda outs: then(outs[0], outs[1]))


def sibling_pair_ride(r, then):
    def parts(ins, outs, sems):
        (r_ref,), (out_ref,), (send_sems, recv_sems, local_sem) = ins, outs, sems
        mx, my, mc = _place()
        local = pltpu.make_async_copy(r_ref, out_ref.at[mc], local_sem)
        send = pltpu.make_async_remote_copy(src_ref=r_ref, dst_ref=out_ref.at[mc], send_sem=send_sems.at[0],
                                            recv_sem=recv_sems.at[0], device_id=(mx, my, 1 - mc), device_id_type=MESH)
        recv = lambda: pltpu.make_async_remote_copy(
            src_ref=r_ref, dst_ref=out_ref.at[1 - mc], send_sem=send_sems.at[0], recv_sem=recv_sems.at[0],
            device_id=(mx, my, 1 - mc), device_id_type=MESH)
        return local, send, recv

    def start(ins, outs, sems):
        local, send, _ = parts(ins, outs, sems)
        local.start()
        send.start()

    def finish(ins, outs, sems):
        local, send, recv = parts(ins, outs, sems)
        send.wait_send()
        recv().wait_recv()
        local.wait()

    return Ride([r], [jax.ShapeDtypeStruct((2,) + r.shape, r.dtype)], _sems(1), start, finish,
                lambda outs: then(outs[0]))


def chip_scatter_ride(p, then):
    def parts(ins, outs, sems):
        (p_ref,), (out_ref,), (send_sems, recv_sems, local_sem) = ins, outs, sems
        mx, my, mc = _place()
        mine = 2 * mx + my
        chips = _other_chips(mx, my)
        local = pltpu.make_async_copy(p_ref.at[mine], out_ref.at[mine], local_sem)
        sends = [pltpu.make_async_remote_copy(
            src_ref=p_ref.at[2 * chip[0] + chip[1]], dst_ref=out_ref.at[mine], send_sem=send_sems.at[k],
            recv_sem=recv_sems.at[k], device_id=(*chip, mc), device_id_type=MESH) for k, chip in enumerate(chips)]
        recvs = lambda: [pltpu.make_async_remote_copy(
            src_ref=p_ref.at[mine], dst_ref=out_ref.at[2 * chip[0] + chip[1]], send_sem=send_sems.at[k],
            recv_sem=recv_sems.at[k], device_id=(*chip, mc), device_id_type=MESH) for k, chip in enumerate(chips)]
        return local, sends, recvs

    def start(ins, outs, sems):
        local, sends, _ = parts(ins, outs, sems)
        local.start()
        for cp in sends:
            cp.start()

    def finish(ins, outs, sems):
        local, sends, recvs = parts(ins, outs, sems)
        for cp in recvs():
            cp.wait_recv()
        for cp in sends:
            cp.wait_send()
        local.wait()

    return Ride([p], [jax.ShapeDtypeStruct(p.shape, p.dtype)], _sems(3), start, finish,
                lambda outs: then(outs[0]))


def _stream_rows(R, C):
    return _rows(R, max(16, (256 * 1536 // C) // 16 * 16))


def cast_bf16(w, name):
    R, C = w.shape
    tb = _stream_rows(R, C)

    def body(w_ref, o_ref):
        o_ref[...] = w_ref[...].astype(BF16)

    blk = pl.BlockSpec((tb, C), lambda i: (i, 0))
    return _pcall(body, name=name, grid=(R // tb,), in_specs=[blk], out_specs=blk,
                          out_shape=jax.ShapeDtypeStruct((R, C), BF16), compiler_params=_params(1))(w)


def sum_slots(g, name):
    n, R, C = g.shape
    tb = _stream_rows(R, C)

    def body(*refs):
        acc = refs[0][...].astype(F32)
        for r in refs[1:n]:
            acc = acc + r[...].astype(F32)
        refs[n][...] = acc

    specs = [pl.BlockSpec((None, tb, C), functools.partial(lambda k, i: (k, i, 0), k)) for k in range(n)]
    return _pcall(body, name=name, grid=(R // tb,), in_specs=specs,
                          out_specs=pl.BlockSpec((tb, C), lambda i: (i, 0)),
                          out_shape=jax.ShapeDtypeStruct((R, C), F32), compiler_params=_params(1))(*([g] * n))


def add_to_bf16(a, b, name):
    R, C = a.shape
    tb = _stream_rows(R, C)

    def body(a_ref, b_ref, o_ref):
        o_ref[...] = (a_ref[...] + b_ref[...]).astype(BF16)

    blk = pl.BlockSpec((tb, C), lambda i: (i, 0))
    return _pcall(body, name=name, grid=(R // tb,), in_specs=[blk, blk], out_specs=blk,
                          out_shape=jax.ShapeDtypeStruct((R, C), BF16), compiler_params=_params(1))(a, b)


def ada_fwd(c_all, ada_w, bias, name):
    n, D, Cs = ada_w.shape
    tn = _tile(Cs, 512)

    def body(c_ref, w_ref, b_ref, o_ref):
        o_ref[...] = _bdot(_silu(c_ref[...]), w_ref[...], NN) + b_ref[...]

    return _pcall(
        body, name=name, grid=(n, Cs // tn),
        in_specs=[pl.BlockSpec((8, D), lambda l, j: (0, 0)),
                  pl.BlockSpec((None, D, tn), lambda l, j: (l, 0, j)),
                  pl.BlockSpec((None, 1, tn), lambda l, j: (l, 0, j))],
        out_specs=pl.BlockSpec((None, 8, tn), lambda l, j: (l, 0, j)),
        out_shape=jax.ShapeDtypeStruct((n, 8, Cs), F32), compiler_params=_params(2))(c_all, ada_w, bias)


def ada_bwd(c16, dmod16, name):
    n, _, Cs = dmod16.shape
    D = c16.shape[1]
    tn = _tile(Cs, 512)

    def body(c_ref, d_ref, o_ref):
        o_ref[...] = _bdot(_silu(c_ref[...]), d_ref[...], TN)

    return _pcall(
        body, name=name, grid=(n, Cs // tn),
        in_specs=[pl.BlockSpec((16, D), lambda l, j: (0, 0)),
                  pl.BlockSpec((None, 16, tn), lambda l, j: (l, 0, j))],
        out_specs=pl.BlockSpec((None, D, tn), lambda l, j: (l, 0, j)),
        out_shape=jax.ShapeDtypeStruct((n, D, Cs), F32), compiler_params=_params(2))(c16, dmod16)


WEIGHTS = ["ada_w", "ada_b", "norm_mix", "norm_ffn", "ev_w_in", "ev_w_out", "gdn_conv_w", "gdn_a_log",
           "gdn_dt_bias", "gdn_norm", "pool_w", "pool_scale", "od_w_in", "od_w_out", "att_q_norm",
           "att_k_norm", "ffn_w_up", "ffn_conv_w", "ffn_conv_b", "ffn_w_down"]
COL_SHARDED = ("ev_w_in", "od_w_in", "ffn_w_up")
ROW_SHARDED = ("ev_w_out", "od_w_out", "ffn_w_down")


def _pack(parts):
    rows, offs = [], []
    at = 0
    for p in parts:
        flat = p.reshape(-1).astype(F32)
        n = -(-flat.shape[0] // LANE)
        rows.append(jnp.pad(flat, (0, n * LANE - flat.shape[0])).reshape(n, LANE))
        offs.append((at, n))
        at += n
    pad = -at % 16
    if pad:
        rows.append(jnp.zeros((pad, LANE), F32))
    return jnp.concatenate(rows, axis=0), offs


def _unpack(buf, off, shape):
    at, n = off
    lead = buf.shape[:-2]
    flat = buf[..., at:at + n, :].reshape(lead + (n * LANE,))
    return flat[..., :math.prod(shape)].reshape(lead + tuple(shape))


def submit_weight_gather(store, key, shard, col_sharded, mc):
    R, C = shard.shape
    half = lax.dynamic_index_in_dim(shard.reshape(2, R // 2, C), mc, 0, keepdims=False)

    def landed(g):
        g = g.reshape(N_CHIPS, R, C)
        store[key] = g.transpose(1, 0, 2).reshape(R, N_CHIPS * C) if col_sharded else g.reshape(N_CHIPS * R, C)

    submit_ride(gather8_ride(half, landed))


def submit_grad_reduce(store, key, pieces, col_sharded):
    _, _, R, C = pieces.shape
    tag = f"{key[0]}{key[1]}"

    def paired(out):
        store[key] = out.reshape(2 * R, C) if col_sharded else out.transpose(1, 0, 2).reshape(R, 2 * C)

    def scattered(got):
        submit_ride(sibling_pair_ride(sum_slots(got, f"gsum_{tag}"), paired))

    def swapped(keep, got):
        chip_sum = add_to_bf16(keep.reshape(N_CHIPS * R, C), got.reshape(N_CHIPS * R, C), f"gadd_{tag}")
        submit_ride(chip_scatter_ride(chip_sum.reshape(N_CHIPS, R, C), scattered))

    submit_ride(sibling_halves_ride(pieces, swapped))


def kernel(x, c, ada_w, ada_b, norm_mix, norm_ffn, ev_w_in, ev_w_out, gdn_conv_w, gdn_a_log, gdn_dt_bias, gdn_norm, pool_w, pool_scale, od_w_in, od_w_out, att_q_norm, att_k_norm, ffn_w_up, ffn_conv_w, ffn_conv_b, ffn_w_down, loss_target, m_ada_w, m_ada_b, m_norm_mix, m_norm_ffn, m_ev_w_in, m_ev_w_out, m_gdn_conv_w, m_gdn_a_log, m_gdn_dt_bias, m_gdn_norm, m_pool_w, m_pool_scale, m_od_w_in, m_od_w_out, m_att_q_norm, m_att_k_norm, m_ffn_w_up, m_ffn_conv_w, m_ffn_conv_b, m_ffn_w_down, v_ada_w, v_ada_b, v_norm_mix, v_norm_ffn, v_ev_w_in, v_ev_w_out, v_gdn_conv_w, v_gdn_a_log, v_gdn_dt_bias, v_gdn_norm, v_pool_w, v_pool_scale, v_od_w_in, v_od_w_out, v_att_q_norm, v_att_k_norm, v_ffn_w_up, v_ffn_conv_w, v_ffn_conv_b, v_ffn_w_down):
    local = dict(ada_w=ada_w, ada_b=ada_b, norm_mix=norm_mix, norm_ffn=norm_ffn, ev_w_in=ev_w_in,
                 ev_w_out=ev_w_out, gdn_conv_w=gdn_conv_w, gdn_a_log=gdn_a_log, gdn_dt_bias=gdn_dt_bias,
                 gdn_norm=gdn_norm, pool_w=pool_w, pool_scale=pool_scale, od_w_in=od_w_in, od_w_out=od_w_out,
                 att_q_norm=att_q_norm, att_k_norm=att_k_norm, ffn_w_up=ffn_w_up, ffn_conv_w=ffn_conv_w,
                 ffn_conv_b=ffn_conv_b, ffn_w_down=ffn_w_down)
    moments_m = dict(zip(WEIGHTS, (m_ada_w, m_ada_b, m_norm_mix, m_norm_ffn, m_ev_w_in, m_ev_w_out,
                                   m_gdn_conv_w, m_gdn_a_log, m_gdn_dt_bias, m_gdn_norm, m_pool_w, m_pool_scale,
                                   m_od_w_in, m_od_w_out, m_att_q_norm, m_att_k_norm, m_ffn_w_up, m_ffn_conv_w,
                                   m_ffn_conv_b, m_ffn_w_down)))
    moments_v = dict(zip(WEIGHTS, (v_ada_w, v_ada_b, v_norm_mix, v_norm_ffn, v_ev_w_in, v_ev_w_out,
                                   v_gdn_conv_w, v_gdn_a_log, v_gdn_dt_bias, v_gdn_norm, v_pool_w, v_pool_scale,
                                   v_od_w_in, v_od_w_out, v_att_q_norm, v_att_k_norm, v_ffn_w_up, v_ffn_conv_w,
                                   v_ffn_conv_b, v_ffn_w_down)))
    _RIDES.clear()
    _IDS[0] = 0
    mx, my, mc = _place()
    chip = 2 * mx + my
    T, D = x.shape[1], x.shape[2]
    depth = ada_w.shape[0]
    ada_cols = ada_w.shape[2]

    buf, offs = _pack([c, gdn_conv_w, ffn_conv_w])
    gathered = all_gather8(buf)
    c_all = _unpack(gathered, offs[0], (D,))
    by_chip = gathered[0::2]
    gdn_conv_full = jnp.concatenate(list(_unpack(by_chip, offs[1], gdn_conv_w.shape)), axis=-1)
    ffn_conv_full = jnp.concatenate(list(_unpack(by_chip, offs[2], ffn_conv_w.shape)), axis=-1)

    bias = lax.dynamic_slice_in_dim(ada_b, chip * ada_cols, ada_cols, axis=1)[:, None, :]
    mod_part = ada_fwd(c_all, ada_w, bias, "ada_fwd")
    mod_all = all_gather8(mod_part)[0::2]
    mod_all = mod_all.transpose(1, 2, 0, 3).reshape(depth, 8, N_CHIPS * ada_cols)
    mod = lax.dynamic_index_in_dim(mod_all, 4 * mx + 2 * my + mc, 1, keepdims=False).reshape(depth, 6, D)

    full_w, big_grad = {}, {}
    order = []
    for i in range(depth):
        mixer = ("ev_w_in", "ev_w_out") if i % 2 == 0 else ("od_w_in", "od_w_out")
        order += [(name, i // 2) for name in mixer] + [("ffn_w_up", i), ("ffn_w_down", i)]
    shards = {name: cast_bf16(local[name].reshape(-1, local[name].shape[-1]), f"cast_{name}")
              .reshape(local[name].shape) for name in COL_SHARDED + ROW_SHARDED}
    for name, e in order:
        submit_weight_gather(full_w, (name, e), shards[name][e], name in COL_SHARDED, mc)

    def big(name, e):
        flush_rides(until=lambda: (name, e) in full_w)
        return full_w[(name, e)]

    def on_grad(name, e, pieces):
        submit_grad_reduce(big_grad, (name, e), pieces, name in COL_SHARDED)

    W = dict(local)
    W["gdn_conv_w"], W["ffn_conv_w"] = gdn_conv_full, ffn_conv_full
    sq, dx, dmod, grads = local_step(x[0], loss_target[0], mod, W, big, on_grad)
    loss = lax.psum(0.5 * jnp.sum(sq) / D, ("x", "y", "c"))

    small = ["norm_mix", "norm_ffn", "gdn_conv_w", "gdn_a_log", "gdn_dt_bias", "gdn_norm", "pool_w",
             "pool_scale", "att_q_norm", "att_k_norm", "ffn_conv_w", "ffn_conv_b"]
    full = {name: jnp.stack([g[name] for g in grads if name in g]) for name in small}
    grad = {}
    buf, offs = _pack([dmod] + [full[name] for name in small])
    gathered = all_gather8(buf)
    summed = sum_slots(gathered, "sum_small_grads")
    grad["ada_b"] = _unpack(summed, offs[0], ada_b.shape)
    for k, name in enumerate(small):
        grad[name] = _unpack(summed, offs[1 + k], full[name].shape)
    for name, cols in (("gdn_conv_w", gdn_conv_w.shape[-1]), ("ffn_conv_w", ffn_conv_w.shape[-1])):
        grad[name] = lax.dynamic_slice_in_dim(grad[name], chip * cols, cols, axis=2)

    dmod_all = _unpack(gathered, offs[0], (depth, N_CHIPS * ada_cols))
    dmod_mine = lax.dynamic_slice_in_dim(dmod_all, chip * ada_cols, ada_cols, axis=2).transpose(1, 0, 2)
    grad["ada_w"] = ada_bwd(jnp.pad(c_all, ((0, 8), (0, 0))), jnp.pad(dmod_mine, ((0, 0), (0, 8), (0, 0))),
                            "ada_bwd")

    deltas, new_m, new_v = {}, {}, {}
    large = COL_SHARDED + ROW_SHARDED
    for name in [n for n in WEIGHTS if n not in large] + list(large):
        if name in large:
            keys = [k for k in order if k[0] == name]
            flush_rides(until=lambda: all(k in big_grad for k in keys))
            grad[name] = jnp.stack([big_grad[k] for k in keys])
        deltas[name], new_m[name], new_v[name] = adamw(local[name], grad[name], moments_m[name],
                                                       moments_v[name], f"adamw_{name}")
    flush_rides()
    return (loss, dx[None], *[grad[n] for n in WEIGHTS], *[deltas[n] for n in WEIGHTS],
            *[new_m[n] for n in WEIGHTS], *[new_v[n] for n in WEIGHTS])
```

```python
import functools
import math

import jax
import jax.numpy as jnp
from jax import lax
from jax.experimental import pallas as pl
from jax.experimental.pallas import tpu as pltpu

F32 = jnp.float32
BF16 = jnp.bfloat16
LANE = 128
SUBLANE = 8
VMEM_LIMIT = 56 * 1024 * 1024
MESH = pl.DeviceIdType.MESH
N_CHIPS = 4

RMS_EPS = 1e-6
GDN_H = 4
HD = 128
GDN_CHUNK = 64
GDN_STEP = 2
GDN_CONV = 4
FFN_CONV = 3
POOL_G = 4
ATT_H = 8
ATT_BLK = 128
DIL = (1, 4, 16)
EVEN_COLS = 2568
EVEN_PAD = 2688
ADAM_LR, ADAM_B1, ADAM_B2, ADAM_EPS, ADAM_WD, ADAM_STEP = 0.001, 0.9, 0.999, 1e-08, 0.01, 10
NEG = -1e30

NN = (((1,), (0,)), ((), ()))
NT = (((1,), (1,)), ((), ()))
TN = (((0,), (0,)), ((), ()))
BNN = (((2,), (1,)), ((0,), (0,)))
BNT = (((2,), (2,)), ((0,), (0,)))
BTN = (((1,), (1,)), ((0,), (0,)))


def _params(n_grid):
    return pltpu.CompilerParams(dimension_semantics=("arbitrary",) * n_grid,
                                vmem_limit_bytes=VMEM_LIMIT)


HBM = pl.BlockSpec(memory_space=pltpu.HBM)
DMA_SEM = pltpu.SemaphoreType.DMA


class Ride:
    def __init__(self, inputs, out_shapes, sems, start, finish, then, mid=None, heavy=False):
        self.inputs, self.out_shapes, self.sems = list(inputs), list(out_shapes), list(sems)
        self.start, self.mid, self.finish, self.then, self.heavy = start, mid, finish, then, heavy


_RIDES = []


def submit_ride(ride):
    _RIDES.append(ride)


def flush_rides(until=None):
    while _RIDES and not (until is not None and until()):
        ride = _RIDES.pop(0)

        def body(*refs, ride=ride):
            a, b = len(ride.inputs), len(ride.inputs) + len(ride.out_shapes)
            ride.start(refs[:a], refs[a:b], refs[b:])
            if ride.mid is not None:
                ride.mid(refs[:a], refs[a:b], refs[b:])
            ride.finish(refs[:a], refs[a:b], refs[b:])

        outs = pl.pallas_call(body, name=f"exchange{_next_id()}", in_specs=[HBM] * len(ride.inputs),
                              out_specs=[HBM] * len(ride.out_shapes), out_shape=ride.out_shapes,
                              scratch_shapes=ride.sems)(*ride.inputs)
        ride.then(list(outs))


_IDS = [0]


def _next_id():
    _IDS[0] += 1
    return _IDS[0]


def _pcall(body, *, name, grid, in_specs, out_specs, out_shape, scratch_shapes=(), compiler_params=None,
           long_call=False):
    del compiler_params
    single = not isinstance(out_shape, (list, tuple))
    outs = [out_shape] if single else list(out_shape)
    ospecs = [out_specs] if single else list(out_specs)
    total = math.prod(grid)
    fits = [k for k, r in enumerate(_RIDES) if long_call or not r.heavy] if total > 1 else []
    ride = _RIDES.pop(fits[0]) if fits else None
    if ride is None:
        call = pl.pallas_call(body, name=name, grid=grid, in_specs=list(in_specs), out_specs=ospecs,
                              out_shape=outs, scratch_shapes=list(scratch_shapes),
                              compiler_params=_params(len(grid)))

        def run_plain(*args):
            res = call(*args)
            return res[0] if single else res
        return run_plain

    n_in, n_out, n_scr = len(in_specs), len(outs), len(scratch_shapes)
    r_in, r_out = len(ride.inputs), len(ride.out_shapes)

    def carrying_body(*refs):
        at = 0
        ins = refs[at:at + n_in]; at += n_in
        r_ins = refs[at:at + r_in]; at += r_in
        os_ = refs[at:at + n_out]; at += n_out
        r_outs = refs[at:at + r_out]; at += r_out
        scr = refs[at:at + n_scr]; at += n_scr
        r_sems = refs[at:]
        step = pl.program_id(0)
        for ax in range(1, len(grid)):
            step = step * grid[ax] + pl.program_id(ax)

        @pl.when(step == 0)
        def _():
            ride.start(r_ins, r_outs, r_sems)

        body(*ins, *os_, *scr)

        if ride.mid is not None:
            @pl.when(step == total // 2)
            def _():
                ride.mid(r_ins, r_outs, r_sems)

        @pl.when(step == total - 1)
        def _():
            ride.finish(r_ins, r_outs, r_sems)

    call = pl.pallas_call(
        carrying_body, name=name, grid=grid, in_specs=list(in_specs) + [HBM] * r_in,
        out_specs=ospecs + [HBM] * r_out, out_shape=outs + ride.out_shapes,
        scratch_shapes=list(scratch_shapes) + ride.sems, compiler_params=_params(len(grid)))

    def run_carrying(*args):
        res = call(*args, *ride.inputs)
        ride.then(list(res[n_out:]))
        return res[0] if single else list(res[:n_out])
    return run_carrying


def _tile(n, target):
    if n <= target:
        return n
    best = None
    for t in range(LANE, target + 1, LANE):
        if n % t == 0:
            best = t
    assert best is not None, (n, target)
    return best


def _rows(n, target):
    if n <= target:
        return n
    best = None
    for t in range(16, target + 1, 16):
        if n % t == 0:
            best = t
    assert best is not None, (n, target)
    return best


def _bdot(a, b, dims):
    return lax.dot_general(a.astype(BF16), b.astype(BF16), dims, preferred_element_type=F32)


def _split(a):
    hi = a.astype(BF16)
    return hi, (a - hi.astype(F32)).astype(BF16)


def _dot3(a, b, dims):
    ah, al = _split(a)
    bh, bl = _split(b)
    d = lambda p, q: lax.dot_general(p, q, dims, preferred_element_type=F32)
    return d(ah, bh) + d(ah, bl) + d(al, bh)


def _sigmoid(x):
    return 1.0 / (1.0 + jnp.exp(-x))


def _silu(x):
    return x * _sigmoid(x)


def matmul(a, b, mode, out_dtype, name, tm=512, tn=1536, tk=1536, pieces=None):
    if mode == "nn":
        (M, K), (K2, N) = a.shape, b.shape
    elif mode == "nt":
        (M, K), (N, K2) = a.shape, b.shape
    else:
        (K, M), (K2, N) = a.shape, b.shape
    assert K == K2, (a.shape, b.shape, mode)
    if pieces == "col":
        tm, tn = _tile(M // 2, tm), _tile(N // N_CHIPS, tn)
    elif pieces == "row":
        quarter = M // N_CHIPS
        tm = 2 * quarter if (2 * quarter) % LANE == 0 else M
        tn = _tile(N // 2, tn)
    else:
        tm, tn = _tile(M, tm), _tile(N, tn)
    tk = _tile(K, tk)
    nk = K // tk
    dims = {"nn": NN, "nt": NT, "tn": TN}[mode]
    if mode == "tn":
        a_spec = pl.BlockSpec((tk, tm), lambda i, j, k: (k, i))
    else:
        a_spec = pl.BlockSpec((tm, tk), lambda i, j, k: (i, k))
    if mode == "nt":
        b_spec = pl.BlockSpec((tn, tk), lambda i, j, k: (j, k))
    else:
        b_spec = pl.BlockSpec((tk, tn), lambda i, j, k: (k, j))

    out_spec = pl.BlockSpec((tm, tn), lambda i, j, k: (i, j))
    out_shape = jax.ShapeDtypeStruct((M, N), out_dtype)
    rows_per_slot = tm
    if pieces == "col":
        nih, njc = (M // 2) // tm, (N // N_CHIPS) // tn
        out_spec = pl.BlockSpec((None, None, tm, tn), lambda i, j, k: (i // nih, j // njc, i % nih, j % njc))
        out_shape = jax.ShapeDtypeStruct((2, N_CHIPS, M // 2, N // N_CHIPS), out_dtype)
    elif pieces == "row":
        rows_per_slot = M // N_CHIPS
        njh = (N // 2) // tn
        out_spec = pl.BlockSpec((None, tm // rows_per_slot, rows_per_slot, tn),
                                lambda i, j, k: (j // njh, i, 0, j % njh))
        out_shape = jax.ShapeDtypeStruct((2, N_CHIPS, rows_per_slot, N // 2), out_dtype)

    def body(a_ref, b_ref, o_ref, acc_ref):
        k = pl.program_id(2)
        p = _bdot(a_ref[...], b_ref[...], dims)

        @pl.when(k == 0)
        def _():
            acc_ref[...] = p

        @pl.when(k > 0)
        def _():
            acc_ref[...] += p

        @pl.when(k == nk - 1)
        def _():
            if pieces == "row":
                for s in range(tm // rows_per_slot):
                    o_ref[s] = acc_ref[s * rows_per_slot:(s + 1) * rows_per_slot, :].astype(out_dtype)
            else:
                o_ref[...] = acc_ref[...].astype(out_dtype)

    return _pcall(
        body, name=name, grid=(M // tm, N // tn, nk),
        in_specs=[a_spec, b_spec], out_specs=out_spec, out_shape=out_shape,
        scratch_shapes=[pltpu.VMEM((tm, tn), F32)],
        long_call=True)(a, b)


def _row_spec(d):
    return pl.BlockSpec((1, d), lambda i: (0, 0))


def modnorm_fwd(x, gain, sc, sh, name):
    T, D = x.shape
    tb = _rows(T, 512)

    def body(x_ref, g_ref, sc_ref, sh_ref, o_ref):
        xv = x_ref[...]
        r = lax.rsqrt(jnp.mean(xv * xv, axis=-1, keepdims=True) + RMS_EPS)
        o_ref[...] = ((xv * r) * g_ref[...] * (1.0 + sc_ref[...]) + sh_ref[...]).astype(BF16)

    blk = pl.BlockSpec((tb, D), lambda i: (i, 0))
    return _pcall(
        body, name=name, grid=(T // tb,),
        in_specs=[blk, _row_spec(D), _row_spec(D), _row_spec(D)],
        out_specs=blk, out_shape=jax.ShapeDtypeStruct((T, D), BF16),
        compiler_params=_params(1))(x, gain, sc, sh)


def modnorm_bwd(x, gain, sc, dh, dres, name):
    T, D = x.shape
    tb = _rows(T, 512)

    def body(x_ref, g_ref, sc_ref, dh_ref, dres_ref, dx_ref, dg_ref, dsc_ref, dsh_ref):
        i = pl.program_id(0)
        xv = x_ref[...]
        r = lax.rsqrt(jnp.mean(xv * xv, axis=-1, keepdims=True) + RMS_EPS)
        n = xv * r
        dhv = dh_ref[...].astype(F32)
        gain_v, sc1 = g_ref[...], 1.0 + sc_ref[...]
        dn = dhv * (gain_v * sc1)
        dx_ref[...] = r * (dn - n * jnp.mean(dn * n, axis=-1, keepdims=True)) + dres_ref[...]
        dhn = dhv * n

        @pl.when(i == 0)
        def _():
            dg_ref[...] = jnp.zeros_like(dg_ref)
            dsc_ref[...] = jnp.zeros_like(dsc_ref)
            dsh_ref[...] = jnp.zeros_like(dsh_ref)

        dg_ref[...] += jnp.sum(dhn * sc1, axis=0, keepdims=True)
        dsc_ref[...] += jnp.sum(dhn * gain_v, axis=0, keepdims=True)
        dsh_ref[...] += jnp.sum(dhv, axis=0, keepdims=True)

    blk = pl.BlockSpec((tb, D), lambda i: (i, 0))
    row = jax.ShapeDtypeStruct((1, D), F32)
    return _pcall(
        body, name=name, grid=(T // tb,),
        in_specs=[blk, _row_spec(D), _row_spec(D), blk, blk],
        out_specs=[blk, _row_spec(D), _row_spec(D), _row_spec(D)],
        out_shape=[jax.ShapeDtypeStruct((T, D), F32), row, row, row],
        compiler_params=_params(1))(x, gain, sc, dh, dres)


def gres_fwd(x, g, y, name):
    T, D = x.shape
    tb = _rows(T, 512)

    def body(x_ref, g_ref, y_ref, o_ref):
        o_ref[...] = x_ref[...] + g_ref[...] * y_ref[...]

    blk = pl.BlockSpec((tb, D), lambda i: (i, 0))
    return _pcall(
        body, name=name, grid=(T // tb,), in_specs=[blk, _row_spec(D), blk], out_specs=blk,
        out_shape=jax.ShapeDtypeStruct((T, D), F32), compiler_params=_params(1))(x, g, y)


def gres_bwd(dx, g, y, name):
    T, D = dx.shape
    tb = _rows(T, 512)

    def body(dx_ref, g_ref, y_ref, dy_ref, dg_ref):
        i = pl.program_id(0)
        dxv = dx_ref[...]
        dy_ref[...] = (dxv * g_ref[...]).astype(BF16)

        @pl.when(i == 0)
        def _():
            dg_ref[...] = jnp.zeros_like(dg_ref)

        dg_ref[...] += jnp.sum(dxv * y_ref[...], axis=0, keepdims=True)

    blk = pl.BlockSpec((tb, D), lambda i: (i, 0))
    return _pcall(
        body, name=name, grid=(T // tb,), in_specs=[blk, _row_spec(D), blk],
        out_specs=[blk, _row_spec(D)],
        out_shape=[jax.ShapeDtypeStruct((T, D), BF16), jax.ShapeDtypeStruct((1, D), F32)],
        compiler_params=_params(1))(dx, g, y)


def loss_head(y, target, name):
    T, D = y.shape
    tb = _rows(T, 512)

    def body(y_ref, t_ref, l_ref, dy_ref):
        i = pl.program_id(0)
        err = y_ref[...] - t_ref[...]
        dy_ref[...] = err * (1.0 / D)

        @pl.when(i == 0)
        def _():
            l_ref[...] = jnp.zeros_like(l_ref)

        sq = jnp.sum(err * err, axis=0, keepdims=True)
        tot = sq[:, 0:LANE]
        for k in range(1, D // LANE):
            tot = tot + sq[:, k * LANE:(k + 1) * LANE]
        l_ref[...] += tot

    blk = pl.BlockSpec((tb, D), lambda i: (i, 0))
    return _pcall(
        body, name=name, grid=(T // tb,), in_specs=[blk, blk],
        out_specs=[_row_spec(LANE), blk],
        out_shape=[jax.ShapeDtypeStruct((1, LANE), F32), jax.ShapeDtypeStruct((T, D), F32)],
        compiler_params=_params(1))(y, target)


def _back(ext, s):
    return ext if s == 0 else pltpu.roll(ext, s, 0)


def _ahead(ext, s):
    return ext if s == 0 else pltpu.roll(ext, ext.shape[0] - s, 0)


def _halo_prev(tb, h):
    return lambda i, j: (jnp.maximum(i * (tb // h) - 1, 0), j)


def _halo_next(tb, h, nrb):
    return lambda i, j: (jnp.minimum(i + 1, nrb - 1) * (tb // h), j)


FFN_TB, FFN_CB = 256, 1408


def ffn_mid_fwd(up, conv_w8, conv_b, name):
    T, F2 = up.shape
    Fd = F2 // 2
    tb, cb = _rows(T, FFN_TB), _tile(Fd, FFN_CB)
    ncb = Fd // cb

    def body(g_ref, gp_ref, v_ref, w_ref, b_ref, o_ref):
        i = pl.program_id(0)
        g = g_ref[...]
        prev = jnp.where(i > 0, gp_ref[...], 0.0)
        ext = jnp.concatenate([prev, g], axis=0)
        w = w_ref[...]
        gc = (w[2:3] * g + w[1:2] * _back(ext, 1)[SUBLANE:] + w[0:1] * _back(ext, 2)[SUBLANE:]
              + b_ref[...])
        o_ref[...] = (_silu(gc) * v_ref[...]).astype(BF16)

    return _pcall(
        body, name=name, grid=(T // tb, ncb),
        in_specs=[pl.BlockSpec((tb, cb), lambda i, j: (i, j)),
                  pl.BlockSpec((SUBLANE, cb), _halo_prev(tb, SUBLANE)),
                  pl.BlockSpec((tb, cb), lambda i, j: (i, j + ncb)),
                  pl.BlockSpec((SUBLANE, cb), lambda i, j: (0, j)),
                  pl.BlockSpec((1, cb), lambda i, j: (0, j))],
        out_specs=pl.BlockSpec((tb, cb), lambda i, j: (i, j)),
        out_shape=jax.ShapeDtypeStruct((T, Fd), BF16),
        long_call=True)(up, up, up, conv_w8, conv_b)


def ffn_mid_bwd(up, conv_w8, conv_b, dact, name):
    T, F2 = up.shape
    Fd = F2 // 2
    tb, cb = _rows(T, FFN_TB), _tile(Fd, FFN_CB)
    ncb, nrb = Fd // cb, T // tb
    H = SUBLANE

    def body(g_ref, gp_ref, gn_ref, v_ref, vn_ref, d_ref, dn_ref, w_ref, b_ref,
             dg_ref, dv_ref, dw_ref, db_ref):
        i = pl.program_id(1)
        g = g_ref[...]
        prev = jnp.where(i > 0, gp_ref[...], 0.0)
        ext = jnp.concatenate([prev, g, gn_ref[...]], axis=0)
        w = w_ref[...]
        e1, e2 = _back(ext, 1), _back(ext, 2)
        gc = (w[2:3] * ext + w[1:2] * e1 + w[0:1] * e2 + b_ref[...])[H:]
        val = jnp.concatenate([v_ref[...], vn_ref[...]], axis=0)
        dnext = jnp.where(i < nrb - 1, dn_ref[...], 0.0)
        da = jnp.concatenate([d_ref[...], dnext], axis=0)
        sg = _sigmoid(gc)
        dv_ref[...] = (da * gc * sg)[:tb].astype(BF16)
        dgc = da * val * (sg * (1.0 + gc * (1.0 - sg)))
        dg_ref[...] = (w[2:3] * dgc + w[1:2] * _ahead(dgc, 1) + w[0:1] * _ahead(dgc, 2))[:tb].astype(BF16)
        dc = dgc[:tb]

        @pl.when(i == 0)
        def _():
            dw_ref[...] = jnp.zeros_like(dw_ref)
            db_ref[...] = jnp.zeros_like(db_ref)

        dw_ref[2:3, :] += jnp.sum(dc * g, axis=0, keepdims=True)
        dw_ref[1:2, :] += jnp.sum(dc * e1[H:H + tb], axis=0, keepdims=True)
        dw_ref[0:1, :] += jnp.sum(dc * e2[H:H + tb], axis=0, keepdims=True)
        db_ref[...] += jnp.sum(dc, axis=0, keepdims=True)

    cur = lambda j, i: (i, j)
    prv = lambda j, i: _halo_prev(tb, H)(i, j)
    nxt = lambda j, i: _halo_next(tb, H, nrb)(i, j)
    return _pcall(
        body, name=name, grid=(ncb, nrb),
        in_specs=[pl.BlockSpec((tb, cb), cur), pl.BlockSpec((H, cb), prv), pl.BlockSpec((H, cb), nxt),
                  pl.BlockSpec((tb, cb), lambda j, i: (i, j + ncb)),
                  pl.BlockSpec((H, cb), lambda j, i: (jnp.minimum(i + 1, nrb - 1) * (tb // H), j + ncb)),
                  pl.BlockSpec((tb, cb), cur), pl.BlockSpec((H, cb), nxt),
                  pl.BlockSpec((SUBLANE, cb), lambda j, i: (0, j)),
                  pl.BlockSpec((1, cb), lambda j, i: (0, j))],
        out_specs=[pl.BlockSpec((tb, cb), cur), pl.BlockSpec((tb, cb), cur),
                   pl.BlockSpec((SUBLANE, cb), lambda j, i: (0, j)),
                   pl.BlockSpec((1, cb), lambda j, i: (0, j))],
        out_shape=[jax.ShapeDtypeStruct((T, Fd), BF16), jax.ShapeDtypeStruct((T, Fd), BF16),
                   jax.ShapeDtypeStruct((SUBLANE, Fd), F32), jax.ShapeDtypeStruct((1, Fd), F32)],
        long_call=True)(up, up, up, up, up, dact, dact, conv_w8, conv_b)


GDN_W = GDN_H * HD


def _head_l2norm(a, apply):
    parts = []
    for h in range(GDN_H):
        ah = a[:, h * HD:(h + 1) * HD]
        parts.append(ah * lax.rsqrt(jnp.sum(ah * ah, axis=-1, keepdims=True) + RMS_EPS))
    return jnp.where(apply, jnp.concatenate(parts, axis=1), a)


def _head_l2norm_bwd(a, dy, apply):
    parts = []
    for h in range(GDN_H):
        sl = slice(h * HD, (h + 1) * HD)
        ah, dh = a[:, sl], dy[:, sl]
        r = lax.rsqrt(jnp.sum(ah * ah, axis=-1, keepdims=True) + RMS_EPS)
        y = ah * r
        parts.append(r * (dh - y * jnp.sum(dh * y, axis=-1, keepdims=True)))
    return jnp.where(apply, jnp.concatenate(parts, axis=1), dy)


def gdn_conv_fwd(proj, w8, name):
    T = proj.shape[0]
    tb = _rows(T, 512)
    H = SUBLANE

    def body(x_ref, xp_ref, w_ref, o_ref):
        i, j = pl.program_id(0), pl.program_id(1)
        x = x_ref[...]
        prev = jnp.where(i > 0, xp_ref[...], 0.0)
        ext = jnp.concatenate([prev, x], axis=0)
        w = w_ref[...]
        c = (w[3:4] * x + w[2:3] * _back(ext, 1)[H:] + w[1:2] * _back(ext, 2)[H:]
             + w[0:1] * _back(ext, 3)[H:])
        o_ref[...] = _head_l2norm(_silu(c), j < 2)

    return _pcall(
        body, name=name, grid=(T // tb, 3),
        in_specs=[pl.BlockSpec((tb, GDN_W), lambda i, j: (i, j)),
                  pl.BlockSpec((H, GDN_W), _halo_prev(tb, H)),
                  pl.BlockSpec((SUBLANE, GDN_W), lambda i, j: (0, j))],
        out_specs=pl.BlockSpec((tb, GDN_W), lambda i, j: (i, j)),
        out_shape=jax.ShapeDtypeStruct((T, 3 * GDN_W), F32),
        compiler_params=_params(2))(proj, proj, w8)


def gdn_conv_bwd(proj, w8, dout, name):
    T = proj.shape[0]
    tb = _rows(T, 512)
    nrb = T // tb
    H = SUBLANE

    def body(x_ref, xp_ref, xn_ref, d_ref, dn_ref, w_ref, dx_ref, dw_ref):
        j, i = pl.program_id(0), pl.program_id(1)
        x = x_ref[...]
        prev = jnp.where(i > 0, xp_ref[...], 0.0)
        ext = jnp.concatenate([prev, x, xn_ref[...]], axis=0)
        w = w_ref[...]
        e1, e2, e3 = _back(ext, 1), _back(ext, 2), _back(ext, 3)
        c = (w[3:4] * ext + w[2:3] * e1 + w[1:2] * e2 + w[0:1] * e3)[H:]
        sg = _sigmoid(c)
        dnext = jnp.where(i < nrb - 1, dn_ref[...], 0.0)
        do = jnp.concatenate([d_ref[...], dnext], axis=0)
        da = _head_l2norm_bwd(c * sg, do, j < 2)
        dc = da * (sg * (1.0 + c * (1.0 - sg)))
        dx_ref[...] = (w[3:4] * dc + w[2:3] * _ahead(dc, 1) + w[1:2] * _ahead(dc, 2)
                       + w[0:1] * _ahead(dc, 3))[:tb].astype(BF16)
        dcc = dc[:tb]

        @pl.when(i == 0)
        def _():
            dw_ref[...] = jnp.zeros_like(dw_ref)

        dw_ref[3:4, :] += jnp.sum(dcc * x, axis=0, keepdims=True)
        dw_ref[2:3, :] += jnp.sum(dcc * e1[H:H + tb], axis=0, keepdims=True)
        dw_ref[1:2, :] += jnp.sum(dcc * e2[H:H + tb], axis=0, keepdims=True)
        dw_ref[0:1, :] += jnp.sum(dcc * e3[H:H + tb], axis=0, keepdims=True)

    cur = lambda j, i: (i, j)
    prv = lambda j, i: _halo_prev(tb, H)(i, j)
    nxt = lambda j, i: _halo_next(tb, H, nrb)(i, j)
    return _pcall(
        body, name=name, grid=(3, nrb),
        in_specs=[pl.BlockSpec((tb, GDN_W), cur), pl.BlockSpec((H, GDN_W), prv), pl.BlockSpec((H, GDN_W), nxt),
                  pl.BlockSpec((tb, GDN_W), cur), pl.BlockSpec((H, GDN_W), nxt),
                  pl.BlockSpec((SUBLANE, GDN_W), lambda j, i: (0, j))],
        out_specs=[pl.BlockSpec((tb, GDN_W), cur), pl.BlockSpec((SUBLANE, GDN_W), lambda j, i: (0, j))],
        out_shape=[jax.ShapeDtypeStruct((T, 3 * GDN_W), BF16),
                   jax.ShapeDtypeStruct((SUBLANE, 3 * GDN_W), F32)],
        compiler_params=_params(2))(proj, proj, proj, dout, dout, w8)


def _dot_family(dot, diff):
    if not diff:
        return tuple(functools.partial(lambda d, a, b: dot(a, b, d), d) for d in (BNN, BNT, BTN))

    @jax.custom_vjp
    def nn(a, b):
        return dot(a, b, BNN)
    nn.defvjp(lambda a, b: (dot(a, b, BNN), (a, b)),
              lambda res, g: (dot(g, res[1], BNT), dot(res[0], g, BTN)))

    @jax.custom_vjp
    def nt(a, b):
        return dot(a, b, BNT)
    nt.defvjp(lambda a, b: (dot(a, b, BNT), (a, b)),
              lambda res, g: (dot(g, res[1], BNN), dot(g, res[0], BTN)))

    @jax.custom_vjp
    def tn(a, b):
        return dot(a, b, BTN)
    tn.defvjp(lambda a, b: (dot(a, b, BTN), (a, b)),
              lambda res, g: (dot(res[1], g, BNT), dot(res[0], g, BNN)))
    return nn, nt, tn


def _gdn_step(dots, hdots, S, q, k, v, z, b_raw, a_raw, alog, dtb, gnorm):
    nn, nt, tn = dots
    hnn = hdots[0]
    B, C = q.shape[0], GDN_CHUNK
    ii = lax.broadcasted_iota(jnp.int32, (B, C, C), 1)
    jj = lax.broadcasted_iota(jnp.int32, (B, C, C), 2)
    causal, strict = ii >= jj, ii > jj
    tri, tri_t = causal.astype(F32), (ii <= jj).astype(F32)
    eye, ones = (ii == jj).astype(F32), jnp.ones((B, C, C), F32)

    beta = _sigmoid(b_raw)
    xs = a_raw + dtb
    pos = xs > 0.0
    softplus = jnp.where(pos, xs, 0.0) + jnp.log(1.0 + jnp.exp(jnp.where(pos, -xs, xs)))
    g = -jnp.exp(alog) * softplus
    gb = jnp.broadcast_to(g, (B, C, C))
    gc_c = hnn(tri, gb)
    gc_r = hnn(hnn(ones, eye * gb), tri_t)
    gc = hnn(tri, jnp.broadcast_to(g, (B, C, HD)))
    gl = jnp.sum(g, axis=1, keepdims=True)
    decay = jnp.where(causal, jnp.exp(jnp.where(causal, gc_c - gc_r, 0.0)), 0.0)
    q = q * (HD ** -0.5)
    kb = k * beta
    L = jnp.where(strict, nt(kb, k) * decay, 0.0)
    egc = jnp.exp(gc)
    P = eye - L
    M = hnn(L, L)
    for step in range(5):
        P = P + hnn(P, M)
        if step < 4:
            M = hnn(M, M)
    u = hnn(P, v * beta)
    w = hnn(P, kb * egc)
    intra = jnp.where(causal, nt(q, k) * decay, 0.0)
    qg = q * egc
    kdec = k * jnp.exp(gl - gc)
    egl = jnp.exp(gl)
    outs = []
    for ci in range(B // GDN_H):
        sl = slice(ci * GDN_H, (ci + 1) * GDN_H)
        v_new = u[sl] - nn(w[sl], S)
        outs.append(nn(qg[sl], S) + nn(intra[sl], v_new))
        S = S * egl[sl] + tn(kdec[sl], v_new)
    o = jnp.concatenate(outs, axis=0)
    r = lax.rsqrt(jnp.mean(o * o, axis=-1, keepdims=True) + RMS_EPS)
    return o * r * gnorm * _silu(z), S


def _gdn_batches(qkv, ba, z, alog_row, dt_row):
    C = GDN_CHUNK
    q, k, v, zz, b_raw, a_raw, alog, dtb = ([] for _ in range(8))
    for ci in range(GDN_STEP):
        rows = slice(ci * C, (ci + 1) * C)
        for h in range(GDN_H):
            q.append(qkv[rows, h * HD:(h + 1) * HD])
            k.append(qkv[rows, GDN_W + h * HD:GDN_W + (h + 1) * HD])
            v.append(qkv[rows, 2 * GDN_W + h * HD:2 * GDN_W + (h + 1) * HD])
            zz.append(z[rows, h * HD:(h + 1) * HD])
            b_raw.append(ba[rows, h:h + 1])
            a_raw.append(ba[rows, GDN_H + h:GDN_H + h + 1])
            alog.append(alog_row[:, h:h + 1])
            dtb.append(dt_row[:, h:h + 1])
    return tuple(jnp.stack(t) for t in (q, k, v, zz, b_raw, a_raw, alog, dtb))


def gdn_chunk_fwd(qkv, proj, alog_row, dt_row, gnorm, name):
    T = qkv.shape[0]
    R = GDN_CHUNK * GDN_STEP
    N = T // R
    dots, hdots = _dot_family(_bdot, False), _dot_family(_dot3, False)

    def body(qkv_ref, ba_ref, z_ref, al_ref, dt_ref, gn_ref, o_ref, save_ref, S_ref):
        n = pl.program_id(0)

        @pl.when(n == 0)
        def _():
            S_ref[...] = jnp.zeros_like(S_ref)

        S = S_ref[...]
        save_ref[0] = S
        batches = _gdn_batches(qkv_ref[...], ba_ref[...], z_ref[...], al_ref[...], dt_ref[...])
        o, S_new = _gdn_step(dots, hdots, S, *batches, gn_ref[...])
        S_ref[...] = S_new
        for ci in range(GDN_STEP):
            for h in range(GDN_H):
                o_ref[ci * GDN_CHUNK:(ci + 1) * GDN_CHUNK, h * HD:(h + 1) * HD] = o[ci * GDN_H + h].astype(BF16)

    return _pcall(
        body, name=name, grid=(N,),
        in_specs=[pl.BlockSpec((R, 3 * GDN_W), lambda n: (n, 0)),
                  pl.BlockSpec((R, LANE), lambda n: (n, (4 * GDN_W + POOL_G * HD) // LANE)),
                  pl.BlockSpec((R, GDN_W), lambda n: (n, 3)),
                  _row_spec(LANE), _row_spec(LANE), _row_spec(HD)],
        out_specs=[pl.BlockSpec((R, GDN_W), lambda n: (n, 0)),
                   pl.BlockSpec((1, GDN_H, HD, HD), lambda n: (n, 0, 0, 0))],
        out_shape=[jax.ShapeDtypeStruct((T, GDN_W), BF16), jax.ShapeDtypeStruct((N, GDN_H, HD, HD), F32)],
        scratch_shapes=[pltpu.VMEM((GDN_H, HD, HD), F32)],
        long_call=True)(qkv, proj, proj, alog_row, dt_row, gnorm)


def gdn_chunk_bwd(qkv, proj, alog_row, dt_row, gnorm, saved, docat, name):
    T = qkv.shape[0]
    C = GDN_CHUNK
    R = C * GDN_STEP
    N = T // R
    dots, hdots = _dot_family(_bdot, True), _dot_family(_dot3, True)

    def body(qkv_ref, ba_ref, z_ref, al_ref, dt_ref, gn_ref, save_ref, do_ref,
             dqkv_ref, dz_ref, dba_ref, dal_ref, ddt_ref, dgn_ref, dS_ref):
        n = pl.program_id(0)

        @pl.when(n == 0)
        def _():
            dS_ref[...] = jnp.zeros_like(dS_ref)
            dal_ref[...] = jnp.zeros_like(dal_ref)
            ddt_ref[...] = jnp.zeros_like(ddt_ref)
            dgn_ref[...] = jnp.zeros_like(dgn_ref)

        batches = _gdn_batches(qkv_ref[...], ba_ref[...], z_ref[...], al_ref[...], dt_ref[...])
        do = do_ref[...]
        do_b = jnp.stack([do[ci * C:(ci + 1) * C, h * HD:(h + 1) * HD]
                          for ci in range(GDN_STEP) for h in range(GDN_H)])
        fn = functools.partial(_gdn_step, dots, hdots)
        _, vjp = jax.vjp(fn, save_ref[0], *batches, gn_ref[...])
        dS, dq, dk, dv, dz, db_raw, da_raw, dalog, ddtb, dgn = vjp((do_b, dS_ref[...]))
        dS_ref[...] = dS
        lane = lax.broadcasted_iota(jnp.int32, (1, LANE), 1)
        dal = jnp.zeros((1, LANE), F32)
        ddt = jnp.zeros((1, LANE), F32)
        for ci in range(GDN_STEP):
            rows = slice(ci * C, (ci + 1) * C)
            dba = jnp.zeros((C, LANE), F32)
            for h in range(GDN_H):
                b = ci * GDN_H + h
                dqkv_ref[rows, h * HD:(h + 1) * HD] = dq[b]
                dqkv_ref[rows, GDN_W + h * HD:GDN_W + (h + 1) * HD] = dk[b]
                dqkv_ref[rows, 2 * GDN_W + h * HD:2 * GDN_W + (h + 1) * HD] = dv[b]
                dz_ref[rows, h * HD:(h + 1) * HD] = dz[b].astype(BF16)
                hot_b = (lane == h).astype(F32)
                dba = dba + db_raw[b] * hot_b + da_raw[b] * (lane == GDN_H + h).astype(F32)
                dal = dal + dalog[b] * hot_b
                ddt = ddt + ddtb[b] * hot_b
            dba_ref[rows, :] = dba.astype(BF16)
        dal_ref[...] += dal
        ddt_ref[...] += ddt
        dgn_ref[...] += dgn

    rev = lambda n: N - 1 - n
    row = jax.ShapeDtypeStruct((1, LANE), F32)
    return _pcall(
        body, name=name, grid=(N,),
        in_specs=[pl.BlockSpec((R, 3 * GDN_W), lambda n: (rev(n), 0)),
                  pl.BlockSpec((R, LANE), lambda n: (rev(n), (4 * GDN_W + POOL_G * HD) // LANE)),
                  pl.BlockSpec((R, GDN_W), lambda n: (rev(n), 3)),
                  _row_spec(LANE), _row_spec(LANE), _row_spec(HD),
                  pl.BlockSpec((1, GDN_H, HD, HD), lambda n: (rev(n), 0, 0, 0)),
                  pl.BlockSpec((R, GDN_W), lambda n: (rev(n), 0))],
        out_specs=[pl.BlockSpec((R, 3 * GDN_W), lambda n: (rev(n), 0)),
                   pl.BlockSpec((R, GDN_W), lambda n: (rev(n), 0)),
                   pl.BlockSpec((R, LANE), lambda n: (rev(n), 0)),
                   _row_spec(LANE), _row_spec(LANE), _row_spec(HD)],
        out_shape=[jax.ShapeDtypeStruct((T, 3 * GDN_W), F32), jax.ShapeDtypeStruct((T, GDN_W), BF16),
                   jax.ShapeDtypeStruct((T, LANE), BF16), row, row, jax.ShapeDtypeStruct((1, HD), F32)],
        scratch_shapes=[pltpu.VMEM((GDN_H, HD, HD), F32)],
        long_call=True)(qkv, proj, proj, alog_row, dt_row, gnorm, saved, docat)


POOL_HALO = 16


def _pool_pick(j, s2, s4, s8, s16):
    return jnp.where(j == 0, s2, jnp.where(j == 1, s4, jnp.where(j == 2, s8, s16)))


def _pool_count(j, t0, rows):
    t1 = (t0 + 1 + lax.broadcasted_iota(jnp.int32, (rows, 1), 0)).astype(F32)
    win = jnp.where(j == 0, 2.0, jnp.where(j == 1, 4.0, jnp.where(j == 2, 8.0, 16.0)))
    return jnp.minimum(t1, win)


def _pooled(p, prev, i, j, tb):
    ext = jnp.concatenate([prev, p], axis=0)
    s2 = ext + _back(ext, 1)
    s4 = s2 + _back(s2, 2)
    s8 = s4 + _back(s4, 4)
    s16 = s8 + _back(s8, 8)
    s = _pool_pick(j, s2, s4, s8, s16)[POOL_HALO:]
    return s / _pool_count(j, i * tb, tb) - p


def pool_fwd(proj, pool_w, pool_scale, name):
    T = proj.shape[0]
    tb = _rows(T, 512)
    c0 = 4 * GDN_H

    def body(p_ref, pp_ref, w_ref, s_ref, o_ref):
        i, j = pl.program_id(0), pl.program_id(1)
        p = p_ref[...]
        prev = jnp.where(i > 0, pp_ref[...], 0.0)
        pooled = _pooled(p, prev, i, j, tb)
        o_ref[...] = (_bdot(pooled, w_ref[0], NN) * s_ref[...]).astype(BF16)

    return _pcall(
        body, name=name, grid=(T // tb, POOL_G),
        in_specs=[pl.BlockSpec((tb, HD), lambda i, j: (i, c0 + j)),
                  pl.BlockSpec((POOL_HALO, HD), lambda i, j: (jnp.maximum(i * (tb // POOL_HALO) - 1, 0), c0 + j)),
                  pl.BlockSpec((1, HD, HD), lambda i, j: (j, 0, 0)),
                  pl.BlockSpec((1, HD), lambda i, j: (0, j))],
        out_specs=pl.BlockSpec((tb, HD), lambda i, j: (i, j)),
        out_shape=jax.ShapeDtypeStruct((T, POOL_G * HD), BF16),
        compiler_params=_params(2))(proj, proj, pool_w, pool_scale)


def pool_bwd(proj, pool_w, pool_scale, docat, name):
    T = proj.shape[0]
    tb = _rows(T, 512)
    nrb = T // tb
    c0 = 4 * GDN_H
    HB = POOL_HALO

    def body(p_ref, pp_ref, w_ref, s_ref, d_ref, dn_ref, dp_ref, dw_ref, ds_ref):
        j, i = pl.program_id(0), pl.program_id(1)
        p = p_ref[...]
        prev = jnp.where(i > 0, pp_ref[...], 0.0)
        pooled = _pooled(p, prev, i, j, tb)
        w, scale = w_ref[0], s_ref[...]
        dy = d_ref[...]
        dnext = jnp.where(i < nrb - 1, dn_ref[...], 0.0)
        dyp = jnp.concatenate([dy, dnext], axis=0) * scale
        dpooled = _bdot(dyp, w, NT)
        qn = dpooled / _pool_count(j, i * tb, tb + HB)
        a2 = qn + _ahead(qn, 1)
        a4 = a2 + _ahead(a2, 2)
        a8 = a4 + _ahead(a4, 4)
        a16 = a8 + _ahead(a8, 8)
        dp_ref[...] = (_pool_pick(j, a2, a4, a8, a16) - dpooled)[:tb].astype(BF16)

        @pl.when(i == 0)
        def _():
            dw_ref[...] = jnp.zeros_like(dw_ref)
            ds_ref[...] = jnp.zeros_like(ds_ref)

        dw_ref[0] += _bdot(pooled, dyp[:tb], TN)
        ds_ref[...] += jnp.sum(dy * _bdot(pooled, w, NN), axis=0, keepdims=True)

    return _pcall(
        body, name=name, grid=(POOL_G, nrb),
        in_specs=[pl.BlockSpec((tb, HD), lambda j, i: (i, c0 + j)),
                  pl.BlockSpec((HB, HD), lambda j, i: (jnp.maximum(i * (tb // HB) - 1, 0), c0 + j)),
                  pl.BlockSpec((1, HD, HD), lambda j, i: (j, 0, 0)),
                  pl.BlockSpec((1, HD), lambda j, i: (0, j)),
                  pl.BlockSpec((tb, HD), lambda j, i: (i, POOL_G + j)),
                  pl.BlockSpec((HB, HD), lambda j, i: (jnp.minimum(i + 1, nrb - 1) * (tb // HB), POOL_G + j))],
        out_specs=[pl.BlockSpec((tb, HD), lambda j, i: (i, j)),
                   pl.BlockSpec((1, HD, HD), lambda j, i: (j, 0, 0)),
                   pl.BlockSpec((1, HD), lambda j, i: (0, j))],
        out_shape=[jax.ShapeDtypeStruct((T, POOL_G * HD), BF16),
                   jax.ShapeDtypeStruct((POOL_G, HD, HD), F32),
                   jax.ShapeDtypeStruct((1, POOL_G * HD), F32)],
        compiler_params=_params(2))(proj, proj, pool_w, pool_scale, docat, docat)


ATT_W = ATT_H * HD
GROUP_COLS = 3 * ATT_W


def to_residue_major(t, d):
    if d == 1:
        return t
    T, C = t.shape
    return t.reshape(T // d, d, C).transpose(1, 0, 2).reshape(T, C)


def to_token_order(t, d):
    if d == 1:
        return t
    T, C = t.shape
    return t.reshape(d, T // d, C).transpose(1, 0, 2).reshape(T, C)


def headnorm_fwd(proj, qk_gain, name):
    T = proj.shape[0]
    tb = _rows(T, 256)

    def body(x_ref, g_ref, o_ref):
        g = g_ref[...]
        for h in range(2 * ATT_H):
            sl = slice(h * HD, (h + 1) * HD)
            x = x_ref[:, sl]
            n = x * lax.rsqrt(jnp.mean(x * x, axis=-1, keepdims=True) + RMS_EPS)
            gain = g[0:1] * (HD ** -0.5) if h < ATT_H else g[1:2]
            o_ref[:, sl] = (n * gain).astype(BF16)
        o_ref[:, 2 * ATT_W:] = x_ref[:, 2 * ATT_W:].astype(BF16)

    blk = pl.BlockSpec((tb, GROUP_COLS), lambda i: (i, 0))
    return _pcall(
        body, name=name, grid=(T // tb,),
        in_specs=[blk, pl.BlockSpec((SUBLANE, HD), lambda i: (0, 0))],
        out_specs=blk, out_shape=jax.ShapeDtypeStruct((T, GROUP_COLS), BF16),
        long_call=True)(proj, qk_gain)


def headnorm_bwd(proj, qk_gain, dq, dk, dv, name):
    T = proj.shape[0]
    tb = _rows(T, 256)

    def body(x_ref, g_ref, dq_ref, dk_ref, dv_ref, dx_ref, dg_ref):
        i = pl.program_id(0)
        g = g_ref[...]

        @pl.when(i == 0)
        def _():
            dg_ref[...] = jnp.zeros_like(dg_ref)

        for part, d_ref in enumerate((dq_ref, dk_ref)):
            gain = g[0:1] * (HD ** -0.5) if part == 0 else g[1:2]
            scale = (HD ** -0.5) if part == 0 else 1.0
            acc = jnp.zeros((1, HD), F32)
            for h in range(ATT_H):
                x = x_ref[:, part * ATT_W + h * HD:part * ATT_W + (h + 1) * HD]
                d = d_ref[:, h * HD:(h + 1) * HD]
                r = lax.rsqrt(jnp.mean(x * x, axis=-1, keepdims=True) + RMS_EPS)
                n = x * r
                dn = d * gain
                dx = r * (dn - n * jnp.mean(dn * n, axis=-1, keepdims=True))
                dx_ref[:, part * ATT_W + h * HD:part * ATT_W + (h + 1) * HD] = dx.astype(BF16)
                acc = acc + jnp.sum(d * n, axis=0, keepdims=True)
            dg_ref[part:part + 1, :] += acc * scale
        dx_ref[:, 2 * ATT_W:] = dv_ref[...].astype(BF16)

    blk = pl.BlockSpec((tb, GROUP_COLS), lambda i: (i, 0))
    dblk = pl.BlockSpec((tb, ATT_W), lambda i: (i, 0))
    gspec = pl.BlockSpec((SUBLANE, HD), lambda i: (0, 0))
    return _pcall(
        body, name=name, grid=(T // tb,),
        in_specs=[blk, gspec, dblk, dblk, dblk],
        out_specs=[blk, gspec],
        out_shape=[jax.ShapeDtypeStruct((T, GROUP_COLS), BF16), jax.ShapeDtypeStruct((SUBLANE, HD), F32)],
        long_call=True)(proj, qk_gain, dq, dk, dv)


def _att_scores(q, k, slope, n_ok, prev):
    a = lax.broadcasted_iota(jnp.int32, (ATT_BLK, ATT_BLK), 0)
    j = lax.broadcasted_iota(jnp.int32, (ATT_BLK, ATT_BLK), 1)
    rel = (ATT_BLK + a - j) if prev else (a - j)
    mask = ((j >= a) & n_ok) if prev else (j <= a)
    s = _bdot(q, k, NT) - slope * rel.astype(F32)
    return jnp.where(mask, s, NEG), mask


def _att_scores_t(k, q, slope, n_ok, nxt):
    j = lax.broadcasted_iota(jnp.int32, (ATT_BLK, ATT_BLK), 0)
    a = lax.broadcasted_iota(jnp.int32, (ATT_BLK, ATT_BLK), 1)
    rel = (ATT_BLK + a - j) if nxt else (a - j)
    mask = ((j >= a) & n_ok) if nxt else (j <= a)
    s = _bdot(k, q, NT) - slope * rel.astype(F32)
    return jnp.where(mask, s, NEG), mask


def _att_blocks(nb, width, shift):
    def make(col):
        return pl.BlockSpec((ATT_BLK, width),
                            lambda r, n: (r * nb + jnp.clip(n + shift, 0, nb - 1), col))
    return make


def _lane_col(cols):
    lane = lax.broadcasted_iota(jnp.int32, (1, LANE), 1)
    out = jnp.zeros((ATT_BLK, LANE), F32)
    for h, c in enumerate(cols):
        out = out + c * (lane == h).astype(F32)
    return out


def att_fwd(qkvn, gi, name):
    T = qkvn.shape[0]
    dil = DIL[gi]
    nb = T // dil // ATT_BLK

    def body(q_ref, kp_ref, kc_ref, vp_ref, vc_ref, o_ref, l_ref):
        n_ok = pl.program_id(1) > 0
        lses = []
        for h in range(ATT_H):
            sl = slice(h * HD, (h + 1) * HD)
            slope = (2.0 ** -(h + 1)) * dil
            q = q_ref[:, sl]
            s_c, _ = _att_scores(q, kc_ref[:, sl], slope, n_ok, False)
            s_p, _ = _att_scores(q, kp_ref[:, sl], slope, n_ok, True)
            m = jnp.maximum(jnp.max(s_c, axis=-1, keepdims=True), jnp.max(s_p, axis=-1, keepdims=True))
            p_c, p_p = jnp.exp(s_c - m), jnp.exp(s_p - m)
            l = jnp.sum(p_c, axis=-1, keepdims=True) + jnp.sum(p_p, axis=-1, keepdims=True)
            o = _bdot(p_c, vc_ref[:, sl], NN) + _bdot(p_p, vp_ref[:, sl], NN)
            o_ref[:, sl] = o / l
            lses.append(m + jnp.log(l))
        l_ref[...] = _lane_col(lses)

    cur, prv = _att_blocks(nb, ATT_W, 0), _att_blocks(nb, ATT_W, -1)
    return _pcall(
        body, name=name, grid=(dil, nb), in_specs=[cur(0), prv(1), cur(1), prv(2), cur(2)],
        out_specs=[cur(0), _att_blocks(nb, LANE, 0)(0)],
        out_shape=[jax.ShapeDtypeStruct((T, ATT_W), F32), jax.ShapeDtypeStruct((T, LANE), F32)],
        long_call=True)(qkvn, qkvn, qkvn, qkvn, qkvn)


def att_merge(os, lses, name):
    T = os[0].shape[0]
    tb = _rows(T, 512)

    def body(o0, o1, o2, l0, l1, l2, o_ref, l_ref):
        a, b, c = l0[...], l1[...], l2[...]
        m = jnp.maximum(a, jnp.maximum(b, c))
        wa, wb, wc = jnp.exp(a - m), jnp.exp(b - m), jnp.exp(c - m)
        den = wa + wb + wc
        l_ref[...] = m + jnp.log(den)
        wa, wb, wc = wa / den, wb / den, wc / den
        for h in range(ATT_H):
            sl = slice(h * HD, (h + 1) * HD)
            o_ref[:, sl] = (wa[:, h:h + 1] * o0[:, sl] + wb[:, h:h + 1] * o1[:, sl]
                            + wc[:, h:h + 1] * o2[:, sl])

    blk = pl.BlockSpec((tb, ATT_W), lambda i: (i, 0))
    lblk = pl.BlockSpec((tb, LANE), lambda i: (i, 0))
    return _pcall(
        body, name=name, grid=(T // tb,), in_specs=[blk] * 3 + [lblk] * 3, out_specs=[blk, lblk],
        out_shape=[jax.ShapeDtypeStruct((T, ATT_W), F32), jax.ShapeDtypeStruct((T, LANE), F32)],
        compiler_params=_params(1))(*os, *lses)


def att_delta(do, o, name):
    T = do.shape[0]
    tb = _rows(T, 512)

    def body(d_ref, o_ref, out_ref):
        lane = lax.broadcasted_iota(jnp.int32, (1, LANE), 1)
        out = jnp.zeros((tb, LANE), F32)
        for h in range(ATT_H):
            sl = slice(h * HD, (h + 1) * HD)
            s = jnp.sum(d_ref[:, sl] * o_ref[:, sl], axis=-1, keepdims=True)
            out = out + s * (lane == h).astype(F32)
        out_ref[...] = out

    blk = pl.BlockSpec((tb, ATT_W), lambda i: (i, 0))
    return _pcall(
        body, name=name, grid=(T // tb,), in_specs=[blk, blk],
        out_specs=pl.BlockSpec((tb, LANE), lambda i: (i, 0)),
        out_shape=jax.ShapeDtypeStruct((T, LANE), F32), compiler_params=_params(1))(do, o)


def att_bwd_q(qkvn, do, lse, delta, gi, name):
    T = qkvn.shape[0]
    dil = DIL[gi]
    nb = T // dil // ATT_BLK

    def body(q_ref, kp_ref, kc_ref, vp_ref, vc_ref, do_ref, l_ref, d_ref, dq_ref):
        n_ok = pl.program_id(1) > 0
        lse, dl = l_ref[...], d_ref[...]
        for h in range(ATT_H):
            sl = slice(h * HD, (h + 1) * HD)
            slope = (2.0 ** -(h + 1)) * dil
            q, do_h = q_ref[:, sl], do_ref[:, sl]
            dq = jnp.zeros((ATT_BLK, HD), F32)
            for k_ref, v_ref, prev in ((kc_ref, vc_ref, False), (kp_ref, vp_ref, True)):
                s, mask = _att_scores(q, k_ref[:, sl], slope, n_ok, prev)
                p = jnp.where(mask, jnp.exp(s - lse[:, h:h + 1]), 0.0)
                ds = p * (_bdot(do_h, v_ref[:, sl], NT) - dl[:, h:h + 1])
                dq = dq + _bdot(ds, k_ref[:, sl], NN)
            dq_ref[:, sl] = dq

    cur, prv = _att_blocks(nb, ATT_W, 0), _att_blocks(nb, ATT_W, -1)
    small = _att_blocks(nb, LANE, 0)(0)
    return _pcall(
        body, name=name, grid=(dil, nb),
        in_specs=[cur(0), prv(1), cur(1), prv(2), cur(2), cur(0), small, small],
        out_specs=cur(0), out_shape=jax.ShapeDtypeStruct((T, ATT_W), F32),
        long_call=True)(qkvn, qkvn, qkvn, qkvn, qkvn, do, lse, delta)


def att_bwd_kv(qkvn, do, lse, delta, gi, name):
    T = qkvn.shape[0]
    dil = DIL[gi]
    nb = T // dil // ATT_BLK

    def body(k_ref, v_ref, q0_ref, q1_ref, do0_ref, do1_ref, l0_ref, l1_ref, d0_ref, d1_ref,
             dk_ref, dv_ref):
        n_ok = pl.program_id(1) < nb - 1
        stats = ((l0_ref[...].T, d0_ref[...].T), (l1_ref[...].T, d1_ref[...].T))
        for h in range(ATT_H):
            sl = slice(h * HD, (h + 1) * HD)
            slope = (2.0 ** -(h + 1)) * dil
            k, v = k_ref[:, sl], v_ref[:, sl]
            dk = jnp.zeros((ATT_BLK, HD), F32)
            dv = jnp.zeros((ATT_BLK, HD), F32)
            for q_ref, do_ref, (lse_t, dl_t), nxt in ((q0_ref, do0_ref, stats[0], False),
                                                      (q1_ref, do1_ref, stats[1], True)):
                q, do_h = q_ref[:, sl], do_ref[:, sl]
                s, mask = _att_scores_t(k, q, slope, n_ok, nxt)
                p = jnp.where(mask, jnp.exp(s - lse_t[h:h + 1, :]), 0.0)
                dv = dv + _bdot(p, do_h, NN)
                ds = p * (_bdot(v, do_h, NT) - dl_t[h:h + 1, :])
                dk = dk + _bdot(ds, q, NN)
            dk_ref[:, sl] = dk
            dv_ref[:, sl] = dv

    cur, nxt = _att_blocks(nb, ATT_W, 0), _att_blocks(nb, ATT_W, 1)
    s0, s1 = _att_blocks(nb, LANE, 0)(0), _att_blocks(nb, LANE, 1)(0)
    return _pcall(
        body, name=name, grid=(dil, nb),
        in_specs=[cur(1), cur(2), cur(0), nxt(0), cur(0), nxt(0), s0, s1, s0, s1],
        out_specs=[cur(0), cur(0)], out_shape=[jax.ShapeDtypeStruct((T, ATT_W), F32)] * 2,
        long_call=True)(qkvn, qkvn, qkvn, qkvn, do, do, lse, lse, delta, delta)


def adamw(w, g, m, v, name):
    shape = w.shape
    C = shape[-1]
    R = math.prod(shape[:-1])
    to2d = lambda t: t.reshape(R, C)
    tb = _rows(R, max(16, (256 * 1536 // C) // 16 * 16))
    c1 = 1.0 - ADAM_B1 ** ADAM_STEP
    c2 = 1.0 - ADAM_B2 ** ADAM_STEP

    def body(w_ref, g_ref, m_ref, v_ref, d_ref, nm_ref, nv_ref):
        gv = g_ref[...]
        nm = ADAM_B1 * m_ref[...] + (1.0 - ADAM_B1) * gv
        nv = ADAM_B2 * v_ref[...] + (1.0 - ADAM_B2) * (gv * gv)
        d_ref[...] = -ADAM_LR * ((nm / c1) / (jnp.sqrt(nv / c2) + ADAM_EPS) + ADAM_WD * w_ref[...])
        nm_ref[...] = nm
        nv_ref[...] = nv

    blk = pl.BlockSpec((tb, C), lambda i: (i, 0))
    out = jax.ShapeDtypeStruct((R, C), F32)
    d, nm, nv = _pcall(
        body, name=name, grid=(R // tb,), in_specs=[blk] * 4, out_specs=[blk] * 3,
        out_shape=[out, out, out], compiler_params=_params(1))(to2d(w), to2d(g), to2d(m), to2d(v))
    return d.reshape(shape), nm.reshape(shape), nv.reshape(shape)


def _pad_rows8(w):
    return jnp.pad(w, ((0, SUBLANE - w.shape[0]), (0, 0)))


def _lane_row(v):
    return jnp.pad(v, (0, LANE - v.shape[0]))[None, :]


def _even_reorder(w_in):
    z4 = 4 * GDN_W
    pad = jnp.zeros((w_in.shape[0], EVEN_PAD - EVEN_COLS), w_in.dtype)
    return jnp.concatenate([w_in[:, :z4], w_in[:, z4 + 2 * GDN_H:], w_in[:, z4:z4 + 2 * GDN_H], pad], axis=1)


def _even_restore(dw):
    z4 = 4 * GDN_W
    p4 = POOL_G * HD
    return jnp.concatenate([dw[:, :z4], dw[:, z4 + p4:z4 + p4 + 2 * GDN_H], dw[:, z4:z4 + p4]], axis=1)


def _ffn_fwd(tag, x1, mod_f, wl):
    sh, sc, g = mod_f
    hf = modnorm_fwd(x1, wl["norm_ffn"], sc, sh, f"{tag}_ffn_norm")
    up = matmul(hf, wl["ffn_w_up"], "nn", F32, f"{tag}_ffn_up")
    act = ffn_mid_fwd(up, wl["ffn_conv_w8"], wl["ffn_conv_b"], f"{tag}_ffn_mid")
    f = matmul(act, wl["ffn_w_down"], "nn", F32, f"{tag}_ffn_down")
    x2 = gres_fwd(x1, g, f, f"{tag}_ffn_res")
    return x2, (x1, hf, up, act, f)


def _ffn_bwd(tag, dx2, saved, mod_f, wl):
    x1, hf, up, act, f = saved
    sh, sc, g = mod_f
    df, dg = gres_bwd(dx2, g, f, f"{tag}_ffn_res_bwd")
    dact = matmul(df, wl["ffn_w_down"], "nt", F32, f"{tag}_ffn_down_da")
    wl["on_grad"]("ffn_w_down", matmul(act, df, "tn", F32, f"{tag}_ffn_down_dw", pieces="row"))
    dgate, dval, dcw, dcb = ffn_mid_bwd(up, wl["ffn_conv_w8"], wl["ffn_conv_b"], dact, f"{tag}_ffn_mid_bwd")
    dup = jnp.concatenate([dgate, dval], axis=1)
    dhf = matmul(dup, wl["ffn_w_up"], "nt", F32, f"{tag}_ffn_up_da")
    wl["on_grad"]("ffn_w_up", matmul(hf, dup, "tn", F32, f"{tag}_ffn_up_dw", pieces="col"))
    dx1, dgain, dsc, dsh = modnorm_bwd(x1, wl["norm_ffn"], sc, dhf, dx2, f"{tag}_ffn_norm_bwd")
    grads = {"norm_ffn": dgain[0], "ffn_conv_w": dcw[:FFN_CONV], "ffn_conv_b": dcb[0]}
    return dx1, (dsh, dsc, dg), grads


def _even_fwd(tag, x, mod_m, wl):
    sh, sc, g = mod_m
    hm = modnorm_fwd(x, wl["norm_mix"], sc, sh, f"{tag}_mix_norm")
    proj = matmul(hm, wl["w_in"], "nn", F32, f"{tag}_ev_in")
    qkv = gdn_conv_fwd(proj, wl["gdn_conv_w8"], f"{tag}_gdn_conv")
    o_a, states = gdn_chunk_fwd(qkv, proj, wl["alog_row"], wl["dt_row"], wl["gdn_norm"], f"{tag}_gdn_chunk")
    o_b = pool_fwd(proj, wl["pool_w"], wl["pool_scale"], f"{tag}_pool")
    ocat = jnp.concatenate([o_a, o_b], axis=1)
    y = matmul(ocat, wl["w_out"], "nn", F32, f"{tag}_ev_out")
    x1 = gres_fwd(x, g, y, f"{tag}_mix_res")
    return x1, (x, hm, proj, qkv, states, ocat, y)


def _even_bwd(tag, dx1, saved, mod_m, wl):
    x, hm, proj, qkv, states, ocat, y = saved
    sh, sc, g = mod_m
    dy, dg = gres_bwd(dx1, g, y, f"{tag}_mix_res_bwd")
    docat = matmul(dy, wl["w_out"], "nt", F32, f"{tag}_ev_out_da")
    wl["on_grad"]("ev_w_out", matmul(ocat, dy, "tn", F32, f"{tag}_ev_out_dw", pieces="row"))
    dqkv, dz, dba, dalog, ddt, dgn = gdn_chunk_bwd(
        qkv, proj, wl["alog_row"], wl["dt_row"], wl["gdn_norm"], states, docat, f"{tag}_gdn_chunk_bwd")
    dxc, dconv = gdn_conv_bwd(proj, wl["gdn_conv_w8"], dqkv, f"{tag}_gdn_conv_bwd")
    dp, dpw, dps = pool_bwd(proj, wl["pool_w"], wl["pool_scale"], docat, f"{tag}_pool_bwd")
    dproj = jnp.concatenate([dxc, dz, dp, dba], axis=1)
    dhm = matmul(dproj, wl["w_in"], "nt", F32, f"{tag}_ev_in_da")
    wl["on_grad"]("ev_w_in", col_pieces(_even_restore(matmul(hm, dproj, "tn", F32, f"{tag}_ev_in_dw"))))
    dx, dgain, dsc, dsh = modnorm_bwd(x, wl["norm_mix"], sc, dhm, dx1, f"{tag}_mix_norm_bwd")
    grads = {"norm_mix": dgain[0], "gdn_conv_w": dconv[:GDN_CONV], "gdn_a_log": dalog[0, :GDN_H], "gdn_dt_bias": ddt[0, :GDN_H],
             "gdn_norm": dgn[0], "pool_w": dpw, "pool_scale": dps[0]}
    return dx, (dsh, dsc, dg), grads


def _odd_fwd(tag, x, mod_m, wl):
    sh, sc, g = mod_m
    hm = modnorm_fwd(x, wl["norm_mix"], sc, sh, f"{tag}_mix_norm")
    projs, qkvns, outs, lses = [], [], [], []
    for gi, d in enumerate(DIL):
        w_g = wl["w_in"][:, gi * GROUP_COLS:(gi + 1) * GROUP_COLS]
        proj = matmul(to_residue_major(hm, d), w_g, "nn", F32, f"{tag}_od_in{gi}")
        qkvn = headnorm_fwd(proj, wl["qk_gain8"], f"{tag}_headnorm{gi}")
        o_g, l_g = att_fwd(qkvn, gi, f"{tag}_att{gi}")
        projs.append(proj)
        qkvns.append(qkvn)
        outs.append(to_token_order(o_g, d))
        lses.append(to_token_order(l_g, d))
    o, lse = att_merge(outs, lses, f"{tag}_att_merge")
    y = matmul(o, wl["w_out"], "nn", F32, f"{tag}_od_out")
    x1 = gres_fwd(x, g, y, f"{tag}_mix_res")
    return x1, (x, hm, projs, qkvns, o, lse, y)


def _odd_bwd(tag, dx1, saved, mod_m, wl):
    x, hm, projs, qkvns, o, lse, y = saved
    sh, sc, g = mod_m
    dy, dg = gres_bwd(dx1, g, y, f"{tag}_mix_res_bwd")
    do = matmul(dy, wl["w_out"], "nt", F32, f"{tag}_od_out_da")
    wl["on_grad"]("od_w_out", matmul(o, dy, "tn", F32, f"{tag}_od_out_dw", pieces="row"))
    delta = att_delta(do, o, f"{tag}_att_delta")
    dhm, dw_in, dgain_qk = None, [], None
    for gi, d in enumerate(DIL):
        w_g = wl["w_in"][:, gi * GROUP_COLS:(gi + 1) * GROUP_COLS]
        do_g, lse_g, dl_g = (to_residue_major(t, d) for t in (do, lse, delta))
        dq = att_bwd_q(qkvns[gi], do_g, lse_g, dl_g, gi, f"{tag}_att{gi}_dq")
        dk, dv = att_bwd_kv(qkvns[gi], do_g, lse_g, dl_g, gi, f"{tag}_att{gi}_dkv")
        dproj, dgain = headnorm_bwd(projs[gi], wl["qk_gain8"], dq, dk, dv, f"{tag}_headnorm{gi}_bwd")
        dhm_g = to_token_order(matmul(dproj, w_g, "nt", F32, f"{tag}_od_in{gi}_da"), d)
        dw_in.append(matmul(to_residue_major(hm, d), dproj, "tn", F32, f"{tag}_od_in{gi}_dw"))
        dhm = dhm_g if dhm is None else dhm + dhm_g
        dgain_qk = dgain if dgain_qk is None else dgain_qk + dgain
    wl["on_grad"]("od_w_in", col_pieces(jnp.concatenate(dw_in, axis=1)))
    dx, dgain, dsc, dsh = modnorm_bwd(x, wl["norm_mix"], sc, dhm, dx1, f"{tag}_mix_norm_bwd")
    grads = {"norm_mix": dgain[0], "att_q_norm": dgain_qk[0], "att_k_norm": dgain_qk[1]}
    return dx, (dsh, dsc, dg), grads


def col_pieces(dw):
    M, N = dw.shape
    return dw.reshape(2, M // 2, N_CHIPS, N // N_CHIPS).transpose(0, 2, 1, 3)


def _layer_weights(i, W, big, on_grad):
    e = i // 2
    wl = {"norm_mix": W["norm_mix"][i][None, :], "norm_ffn": W["norm_ffn"][i][None, :],
          "ffn_w_up": big("ffn_w_up", i), "ffn_w_down": big("ffn_w_down", i),
          "ffn_conv_w8": _pad_rows8(W["ffn_conv_w"][i]), "ffn_conv_b": W["ffn_conv_b"][i][None, :],
          "on_grad": lambda name, pieces: on_grad(name, i if name.startswith("ffn") else e, pieces)}
    if i % 2 == 0:
        wl.update({"w_in": _even_reorder(big("ev_w_in", e)), "w_out": big("ev_w_out", e),
                   "gdn_conv_w8": _pad_rows8(W["gdn_conv_w"][e]),
                   "alog_row": _lane_row(W["gdn_a_log"][e]), "dt_row": _lane_row(W["gdn_dt_bias"][e]),
                   "gdn_norm": W["gdn_norm"][e][None, :], "pool_w": W["pool_w"][e],
                   "pool_scale": W["pool_scale"][e][None, :]})
    else:
        wl.update({"w_in": big("od_w_in", e), "w_out": big("od_w_out", e),
                   "qk_gain8": _pad_rows8(jnp.stack([W["att_q_norm"][e], W["att_k_norm"][e]]))})
    return wl


def local_step(x, target, mod, W, big, on_grad):
    depth = mod.shape[0]
    row = lambda i, k: mod[i, k][None, :]
    saved, wls = [], []
    for i in range(depth):
        wl = _layer_weights(i, W, big, on_grad)
        mod_m = (row(i, 0), row(i, 1), row(i, 2))
        mod_f = (row(i, 3), row(i, 4), row(i, 5))
        fwd = _even_fwd if i % 2 == 0 else _odd_fwd
        x1, s_mix = fwd(f"l{i}", x, mod_m, wl)
        x, s_ffn = _ffn_fwd(f"l{i}", x1, mod_f, wl)
        saved.append((s_mix, s_ffn, mod_m, mod_f))
        wls.append(wl)
    sq, dx = loss_head(x, target, "loss_head")
    dmod, grads = [None] * depth, [None] * depth
    for i in reversed(range(depth)):
        s_mix, s_ffn, mod_m, mod_f = saved[i]
        dx, dmf, g_ffn = _ffn_bwd(f"l{i}", dx, s_ffn, mod_f, wls[i])
        bwd = _even_bwd if i % 2 == 0 else _odd_bwd
        dx, dmm, g_mix = bwd(f"l{i}", dx, s_mix, mod_m, wls[i])
        dmod[i] = jnp.concatenate([t for t in dmm + dmf], axis=0)
        grads[i] = {**g_mix, **g_ffn}
    return sq, dx, jnp.stack(dmod), grads


def _place():
    return lax.axis_index("x"), lax.axis_index("y"), lax.axis_index("c")


def _other_chips(mx, my):
    return [(1 - mx, my), (mx, 1 - my), (1 - mx, 1 - my)]


def _sems(n):
    return [DMA_SEM((n,)), DMA_SEM((n,))]


def _put(buf, block, index):
    return lax.dynamic_update_index_in_dim(buf, block, index, 0)


def gather8_ride(x, then):
    def parts(ins, outs, sems):
        (x_ref,), (out_ref,), (send_sems, recv_sems) = ins, outs, sems
        mx, my, mc = _place()
        me, sibling = (mx, my, mc), (mx, my, 1 - mc)
        chips = _other_chips(mx, my)

        def slot(px, py, pc):
            return out_ref.at[4 * px + 2 * py + pc]

        def copy(k, block, to, src=None):
            return pltpu.make_async_remote_copy(
                src_ref=slot(*block) if src is None else src, dst_ref=slot(*block),
                send_sem=send_sems.at[k], recv_sem=recv_sems.at[k], device_id=to, device_id_type=MESH)

        first = lambda: ([copy(0, me, sibling, src=x_ref)]
                         + [copy(1 + j, me, (*chip, mc), src=x_ref) for j, chip in enumerate(chips)])
        passed = lambda: [copy(4 + j, (*chip, mc), sibling) for j, chip in enumerate(chips)]
        landed = lambda: [copy(1 + j, (*chip, mc), me) for j, chip in enumerate(chips)]
        from_sibling = lambda: ([copy(0, sibling, me)]
                                + [copy(4 + j, (*chip, 1 - mc), me) for j, chip in enumerate(chips)])
        return first, passed, landed, from_sibling

    def start(ins, outs, sems):
        first, _, _, _ = parts(ins, outs, sems)
        for cp in first():
            cp.start()

    def mid(ins, outs, sems):
        _, passed, landed, _ = parts(ins, outs, sems)
        for cp, fwd in zip(landed(), passed()):
            cp.wait_recv()
            fwd.start()

    def finish(ins, outs, sems):
        first, passed, _, from_sibling = parts(ins, outs, sems)
        for cp in from_sibling():
            cp.wait_recv()
        for cp in first() + passed():
            cp.wait_send()

    def landed_all(outs):
        mx, my, mc = _place()
        then(_put(outs[0], x, 4 * mx + 2 * my + mc))

    return Ride([x], [jax.ShapeDtypeStruct((8,) + x.shape, x.dtype)], _sems(7), start, finish,
                landed_all, mid=mid, heavy=True)


def all_gather8(x):
    box = []
    waiting = list(_RIDES)
    _RIDES[:] = [gather8_ride(x, box.append)]
    flush_rides()
    _RIDES[:] = waiting + _RIDES
    return box[0]


def sibling_halves_ride(p, then):
    def copy(ins, outs, sems):
        (p_ref,), (got_ref,), (send_sems, recv_sems) = ins, outs, sems
        mx, my, mc = _place()
        return pltpu.make_async_remote_copy(src_ref=p_ref.at[1 - mc], dst_ref=got_ref, send_sem=send_sems.at[0],
                                            recv_sem=recv_sems.at[0], device_id=(mx, my, 1 - mc), device_id_type=MESH)

    def start(ins, outs, sems):
        copy(ins, outs, sems).start()

    def finish(ins, outs, sems):
        cp = copy(ins, outs, sems)
        cp.wait_send()
        cp.wait_recv()

    def landed(outs):
        then(lax.dynamic_index_in_dim(p, _place()[2], 0, keepdims=False), outs[0])

    return Ride([p], [jax.ShapeDtypeStruct(p.shape[1:], p.dtype)], _sems(1), start, finish, landed)


def sibling_pair_ride(r, then):
    def copy(ins, outs, sems):
        (r_ref,), (got_ref,), (send_sems, recv_sems) = ins, outs, sems
        mx, my, mc = _place()
        return pltpu.make_async_remote_copy(src_ref=r_ref, dst_ref=got_ref, send_sem=send_sems.at[0],
                                            recv_sem=recv_sems.at[0], device_id=(mx, my, 1 - mc), device_id_type=MESH)

    def start(ins, outs, sems):
        copy(ins, outs, sems).start()

    def finish(ins, outs, sems):
        cp = copy(ins, outs, sems)
        cp.wait_send()
        cp.wait_recv()

    def landed(outs):
        then(jnp.where(_place()[2] == 0, jnp.stack([r, outs[0]]), jnp.stack([outs[0], r])))

    return Ride([r], [jax.ShapeDtypeStruct(r.shape, r.dtype)], _sems(1), start, finish, landed)


def chip_scatter_ride(p, then):
    def parts(ins, outs, sems):
        (p_ref,), (out_ref,), (send_sems, recv_sems) = ins, outs, sems
        mx, my, mc = _place()
        mine = 2 * mx + my
        chips = _other_chips(mx, my)
        sends = [pltpu.make_async_remote_copy(
            src_ref=p_ref.at[2 * chip[0] + chip[1]], dst_ref=out_ref.at[mine], send_sem=send_sems.at[k],
            recv_sem=recv_sems.at[k], device_id=(*chip, mc), device_id_type=MESH) for k, chip in enumerate(chips)]
        recvs = lambda: [pltpu.make_async_remote_copy(
            src_ref=p_ref.at[mine], dst_ref=out_ref.at[2 * chip[0] + chip[1]], send_sem=send_sems.at[k],
            recv_sem=recv_sems.at[k], device_id=(*chip, mc), device_id_type=MESH) for k, chip in enumerate(chips)]
        return sends, recvs

    def start(ins, outs, sems):
        sends, _ = parts(ins, outs, sems)
        for cp in sends:
            cp.start()

    def finish(ins, outs, sems):
        sends, recvs = parts(ins, outs, sems)
        for cp in recvs():
            cp.wait_recv()
        for cp in sends:
            cp.wait_send()

    def landed(outs):
        mx, my, _ = _place()
        mine = 2 * mx + my
        then(_put(outs[0], lax.dynamic_index_in_dim(p, mine, 0, keepdims=False), mine))

    return Ride([p], [jax.ShapeDtypeStruct(p.shape, p.dtype)], _sems(3), start, finish, landed, heavy=True)


def _stream_rows(R, C):
    return _rows(R, max(16, (256 * 1536 // C) // 16 * 16))


def cast_bf16(w, name):
    R, C = w.shape
    tb = _stream_rows(R, C)

    def body(w_ref, o_ref):
        o_ref[...] = w_ref[...].astype(BF16)

    blk = pl.BlockSpec((tb, C), lambda i: (i, 0))
    return _pcall(body, name=name, grid=(R // tb,), in_specs=[blk], out_specs=blk,
                          out_shape=jax.ShapeDtypeStruct((R, C), BF16), compiler_params=_params(1))(w)


def sum_slots(g, name):
    n, R, C = g.shape
    tb = _stream_rows(R, C)

    def body(*refs):
        acc = refs[0][...].astype(F32)
        for r in refs[1:n]:
            acc = acc + r[...].astype(F32)
        refs[n][...] = acc

    specs = [pl.BlockSpec((None, tb, C), functools.partial(lambda k, i: (k, i, 0), k)) for k in range(n)]
    return _pcall(body, name=name, grid=(R // tb,), in_specs=specs,
                          out_specs=pl.BlockSpec((tb, C), lambda i: (i, 0)),
                          out_shape=jax.ShapeDtypeStruct((R, C), F32), compiler_params=_params(1))(*([g] * n))


def add_to_bf16(a, b, name):
    R, C = a.shape
    tb = _stream_rows(R, C)

    def body(a_ref, b_ref, o_ref):
        o_ref[...] = (a_ref[...] + b_ref[...]).astype(BF16)

    blk = pl.BlockSpec((tb, C), lambda i: (i, 0))
    return _pcall(body, name=name, grid=(R // tb,), in_specs=[blk, blk], out_specs=blk,
                          out_shape=jax.ShapeDtypeStruct((R, C), BF16), compiler_params=_params(1))(a, b)


def ada_fwd(c_all, ada_w, bias, name):
    n, D, Cs = ada_w.shape
    tn = _tile(Cs, 512)

    def body(c_ref, w_ref, b_ref, o_ref):
        o_ref[...] = _bdot(_silu(c_ref[...]), w_ref[...], NN) + b_ref[...]

    return _pcall(
        body, name=name, grid=(n, Cs // tn),
        in_specs=[pl.BlockSpec((8, D), lambda l, j: (0, 0)),
                  pl.BlockSpec((None, D, tn), lambda l, j: (l, 0, j)),
                  pl.BlockSpec((None, 1, tn), lambda l, j: (l, 0, j))],
        out_specs=pl.BlockSpec((None, 8, tn), lambda l, j: (l, 0, j)),
        out_shape=jax.ShapeDtypeStruct((n, 8, Cs), F32), compiler_params=_params(2))(c_all, ada_w, bias)


def ada_bwd(c16, dmod16, name):
    n, _, Cs = dmod16.shape
    D = c16.shape[1]
    tn = _tile(Cs, 512)

    def body(c_ref, d_ref, o_ref):
        o_ref[...] = _bdot(_silu(c_ref[...]), d_ref[...], TN)

    return _pcall(
        body, name=name, grid=(n, Cs // tn),
        in_specs=[pl.BlockSpec((16, D), lambda l, j: (0, 0)),
                  pl.BlockSpec((None, 16, tn), lambda l, j: (l, 0, j))],
        out_specs=pl.BlockSpec((None, D, tn), lambda l, j: (l, 0, j)),
        out_shape=jax.ShapeDtypeStruct((n, D, Cs), F32), compiler_params=_params(2))(c16, dmod16)


WEIGHTS = ["ada_w", "ada_b", "norm_mix", "norm_ffn", "ev_w_in", "ev_w_out", "gdn_conv_w", "gdn_a_log",
           "gdn_dt_bias", "gdn_norm", "pool_w", "pool_scale", "od_w_in", "od_w_out", "att_q_norm",
           "att_k_norm", "ffn_w_up", "ffn_conv_w", "ffn_conv_b", "ffn_w_down"]
COL_SHARDED = ("ev_w_in", "od_w_in", "ffn_w_up")
ROW_SHARDED = ("ev_w_out", "od_w_out", "ffn_w_down")


def _pack(parts):
    rows, offs = [], []
    at = 0
    for p in parts:
        flat = p.reshape(-1).astype(F32)
        n = -(-flat.shape[0] // LANE)
        rows.append(jnp.pad(flat, (0, n * LANE - flat.shape[0])).reshape(n, LANE))
        offs.append((at, n))
        at += n
    pad = -at % 16
    if pad:
        rows.append(jnp.zeros((pad, LANE), F32))
    return jnp.concatenate(rows, axis=0), offs


def _unpack(buf, off, shape):
    at, n = off
    lead = buf.shape[:-2]
    flat = buf[..., at:at + n, :].reshape(lead + (n * LANE,))
    return flat[..., :math.prod(shape)].reshape(lead + tuple(shape))


def submit_weight_gather(store, key, shard, col_sharded, mc):
    R, C = shard.shape
    half = lax.dynamic_index_in_dim(shard.reshape(2, R // 2, C), mc, 0, keepdims=False)

    def landed(g):
        g = g.reshape(N_CHIPS, R, C)
        store[key] = g.transpose(1, 0, 2).reshape(R, N_CHIPS * C) if col_sharded else g.reshape(N_CHIPS * R, C)

    submit_ride(gather8_ride(half, landed))


def submit_grad_reduce(store, key, pieces, col_sharded):
    _, _, R, C = pieces.shape
    tag = f"{key[0]}{key[1]}"

    def paired(out):
        store[key] = out.reshape(2 * R, C) if col_sharded else out.transpose(1, 0, 2).reshape(R, 2 * C)

    def scattered(got):
        submit_ride(sibling_pair_ride(sum_slots(got, f"gsum_{tag}"), paired))

    def swapped(keep, got):
        chip_sum = add_to_bf16(keep.reshape(N_CHIPS * R, C), got.reshape(N_CHIPS * R, C), f"gadd_{tag}")
        submit_ride(chip_scatter_ride(chip_sum.reshape(N_CHIPS, R, C), scattered))

    submit_ride(sibling_halves_ride(pieces, swapped))


def kernel(x, c, ada_w, ada_b, norm_mix, norm_ffn, ev_w_in, ev_w_out, gdn_conv_w, gdn_a_log, gdn_dt_bias, gdn_norm, pool_w, pool_scale, od_w_in, od_w_out, att_q_norm, att_k_norm, ffn_w_up, ffn_conv_w, ffn_conv_b, ffn_w_down, loss_target, m_ada_w, m_ada_b, m_norm_mix, m_norm_ffn, m_ev_w_in, m_ev_w_out, m_gdn_conv_w, m_gdn_a_log, m_gdn_dt_bias, m_gdn_norm, m_pool_w, m_pool_scale, m_od_w_in, m_od_w_out, m_att_q_norm, m_att_k_norm, m_ffn_w_up, m_ffn_conv_w, m_ffn_conv_b, m_ffn_w_down, v_ada_w, v_ada_b, v_norm_mix, v_norm_ffn, v_ev_w_in, v_ev_w_out, v_gdn_conv_w, v_gdn_a_log, v_gdn_dt_bias, v_gdn_norm, v_pool_w, v_pool_scale, v_od_w_in, v_od_w_out, v_att_q_norm, v_att_k_norm, v_ffn_w_up, v_ffn_conv_w, v_ffn_conv_b, v_ffn_w_down):
    local = dict(ada_w=ada_w, ada_b=ada_b, norm_mix=norm_mix, norm_ffn=norm_ffn, ev_w_in=ev_w_in,
                 ev_w_out=ev_w_out, gdn_conv_w=gdn_conv_w, gdn_a_log=gdn_a_log, gdn_dt_bias=gdn_dt_bias,
                 gdn_norm=gdn_norm, pool_w=pool_w, pool_scale=pool_scale, od_w_in=od_w_in, od_w_out=od_w_out,
                 att_q_norm=att_q_norm, att_k_norm=att_k_norm, ffn_w_up=ffn_w_up, ffn_conv_w=ffn_conv_w,
                 ffn_conv_b=ffn_conv_b, ffn_w_down=ffn_w_down)
    moments_m = dict(zip(WEIGHTS, (m_ada_w, m_ada_b, m_norm_mix, m_norm_ffn, m_ev_w_in, m_ev_w_out,
                                   m_gdn_conv_w, m_gdn_a_log, m_gdn_dt_bias, m_gdn_norm, m_pool_w, m_pool_scale,
                                   m_od_w_in, m_od_w_out, m_att_q_norm, m_att_k_norm, m_ffn_w_up, m_ffn_conv_w,
                                   m_ffn_conv_b, m_ffn_w_down)))
    moments_v = dict(zip(WEIGHTS, (v_ada_w, v_ada_b, v_norm_mix, v_norm_ffn, v_ev_w_in, v_ev_w_out,
                                   v_gdn_conv_w, v_gdn_a_log, v_gdn_dt_bias, v_gdn_norm, v_pool_w, v_pool_scale,
                                   v_od_w_in, v_od_w_out, v_att_q_norm, v_att_k_norm, v_ffn_w_up, v_ffn_conv_w,
                                   v_ffn_conv_b, v_ffn_w_down)))
    _RIDES.clear()
    _IDS[0] = 0
    mx, my, mc = _place()
    chip = 2 * mx + my
    T, D = x.shape[1], x.shape[2]
    depth = ada_w.shape[0]
    ada_cols = ada_w.shape[2]

    buf, offs = _pack([c, gdn_conv_w, ffn_conv_w])
    gathered = all_gather8(buf)
    c_all = _unpack(gathered, offs[0], (D,))
    by_chip = gathered[0::2]
    gdn_conv_full = jnp.concatenate(list(_unpack(by_chip, offs[1], gdn_conv_w.shape)), axis=-1)
    ffn_conv_full = jnp.concatenate(list(_unpack(by_chip, offs[2], ffn_conv_w.shape)), axis=-1)

    bias = lax.dynamic_slice_in_dim(ada_b, chip * ada_cols, ada_cols, axis=1)[:, None, :]
    mod_part = ada_fwd(c_all, ada_w, bias, "ada_fwd")
    mod_all = all_gather8(mod_part)[0::2]
    mod_all = mod_all.transpose(1, 2, 0, 3).reshape(depth, 8, N_CHIPS * ada_cols)
    mod = lax.dynamic_index_in_dim(mod_all, 4 * mx + 2 * my + mc, 1, keepdims=False).reshape(depth, 6, D)

    full_w, big_grad = {}, {}
    order = []
    for i in range(depth):
        mixer = ("ev_w_in", "ev_w_out") if i % 2 == 0 else ("od_w_in", "od_w_out")
        order += [(name, i // 2) for name in mixer] + [("ffn_w_up", i), ("ffn_w_down", i)]
    shards = {name: cast_bf16(local[name].reshape(-1, local[name].shape[-1]), f"cast_{name}")
              .reshape(local[name].shape) for name in COL_SHARDED + ROW_SHARDED}
    for name, e in order:
        submit_weight_gather(full_w, (name, e), shards[name][e], name in COL_SHARDED, mc)

    def big(name, e):
        flush_rides(until=lambda: (name, e) in full_w)
        return full_w[(name, e)]

    def on_grad(name, e, pieces):
        submit_grad_reduce(big_grad, (name, e), pieces, name in COL_SHARDED)

    W = dict(local)
    W["gdn_conv_w"], W["ffn_conv_w"] = gdn_conv_full, ffn_conv_full
    sq, dx, dmod, grads = local_step(x[0], loss_target[0], mod, W, big, on_grad)
    loss = lax.psum(0.5 * jnp.sum(sq) / D, ("x", "y", "c"))

    small = ["norm_mix", "norm_ffn", "gdn_conv_w", "gdn_a_log", "gdn_dt_bias", "gdn_norm", "pool_w",
             "pool_scale", "att_q_norm", "att_k_norm", "ffn_conv_w", "ffn_conv_b"]
    full = {name: jnp.stack([g[name] for g in grads if name in g]) for name in small}
    grad = {}
    buf, offs = _pack([dmod] + [full[name] for name in small])
    gathered = all_gather8(buf)
    summed = sum_slots(gathered, "sum_small_grads")
    grad["ada_b"] = _unpack(summed, offs[0], ada_b.shape)
    for k, name in enumerate(small):
        grad[name] = _unpack(summed, offs[1 + k], full[name].shape)
    for name, cols in (("gdn_conv_w", gdn_conv_w.shape[-1]), ("ffn_conv_w", ffn_conv_w.shape[-1])):
        grad[name] = lax.dynamic_slice_in_dim(grad[name], chip * cols, cols, axis=2)

    dmod_all = _unpack(gathered, offs[0], (depth, N_CHIPS * ada_cols))
    dmod_mine = lax.dynamic_slice_in_dim(dmod_all, chip * ada_cols, ada_cols, axis=2).transpose(1, 0, 2)
    grad["ada_w"] = ada_bwd(jnp.pad(c_all, ((0, 8), (0, 0))), jnp.pad(dmod_mine, ((0, 0), (0, 8), (0, 0))),
                            "ada_bwd")

    deltas, new_m, new_v = {}, {}, {}
    large = COL_SHARDED + ROW_SHARDED
    for name in [n for n in WEIGHTS if n not in large] + list(large):
        if name in large:
            keys = [k for k in order if k[0] == name]
            flush_rides(until=lambda: all(k in big_grad for k in keys))
            grad[name] = jnp.stack([big_grad[k] for k in keys])
        deltas[name], new_m[name], new_v[name] = adamw(local[name], grad[name], moments_m[name],
                                                       moments_v[name], f"adamw_{name}")
    flush_rides()
    return (loss, dx[None], *[grad[n] for n in WEIGHTS], *[deltas[n] for n in WEIGHTS],
            *[new_m[n] for n in WEIGHTS], *[new_v[n] for n in WEIGHTS])
```

```python
import functools
import math

import jax
import jax.numpy as jnp
from jax import lax
from jax.experimental import pallas as pl
from jax.experimental.pallas import tpu as pltpu

F32 = jnp.float32
BF16 = jnp.bfloat16
LANE = 128
SUBLANE = 8
VMEM_LIMIT = 56 * 1024 * 1024
MESH = pl.DeviceIdType.MESH
N_CHIPS = 4

RMS_EPS = 1e-6
GDN_H = 4
HD = 128
GDN_CHUNK = 64
GDN_STEP = 2
GDN_CONV = 4
FFN_CONV = 3
POOL_G = 4
ATT_H = 8
ATT_BLK = 128
DIL = (1, 4, 16)
EVEN_COLS = 2568
EVEN_PAD = 2688
ADAM_LR, ADAM_B1, ADAM_B2, ADAM_EPS, ADAM_WD, ADAM_STEP = 0.001, 0.9, 0.999, 1e-08, 0.01, 10
NEG = -1e30

NN = (((1,), (0,)), ((), ()))
NT = (((1,), (1,)), ((), ()))
TN = (((0,), (0,)), ((), ()))
BNN = (((2,), (1,)), ((0,), (0,)))
BNT = (((2,), (2,)), ((0,), (0,)))
BTN = (((1,), (1,)), ((0,), (0,)))


def _params(n_grid):
    return pltpu.CompilerParams(dimension_semantics=("arbitrary",) * n_grid,
                                vmem_limit_bytes=VMEM_LIMIT)


HBM = pl.BlockSpec(memory_space=pltpu.HBM)
DMA_SEM = pltpu.SemaphoreType.DMA


class Ride:
    def __init__(self, inputs, out_shapes, sems, start, finish, then, mid=None, heavy=False):
        self.inputs, self.out_shapes, self.sems = list(inputs), list(out_shapes), list(sems)
        self.start, self.mid, self.finish, self.then, self.heavy = start, mid, finish, then, heavy


_RIDES = []


def submit_ride(ride):
    _RIDES.append(ride)


def flush_rides(until=None):
    while _RIDES and not (until is not None and until()):
        ride = _RIDES.pop(0)

        def body(*refs, ride=ride):
            a, b = len(ride.inputs), len(ride.inputs) + len(ride.out_shapes)
            ride.start(refs[:a], refs[a:b], refs[b:])
            if ride.mid is not None:
                ride.mid(refs[:a], refs[a:b], refs[b:])
            ride.finish(refs[:a], refs[a:b], refs[b:])

        outs = pl.pallas_call(body, name=f"exchange{_next_id()}", in_specs=[HBM] * len(ride.inputs),
                              out_specs=[HBM] * len(ride.out_shapes), out_shape=ride.out_shapes,
                              scratch_shapes=ride.sems)(*ride.inputs)
        ride.then(list(outs))


_IDS = [0]


def _next_id():
    _IDS[0] += 1
    return _IDS[0]


def _pcall(body, *, name, grid, in_specs, out_specs, out_shape, scratch_shapes=(), compiler_params=None,
           long_call=False):
    del compiler_params
    single = not isinstance(out_shape, (list, tuple))
    outs = [out_shape] if single else list(out_shape)
    ospecs = [out_specs] if single else list(out_specs)
    total = math.prod(grid)
    fits = [k for k, r in enumerate(_RIDES) if long_call or not r.heavy] if total > 1 else []
    ride = _RIDES.pop(fits[0]) if fits else None
    if ride is None:
        call = pl.pallas_call(body, name=name, grid=grid, in_specs=list(in_specs), out_specs=ospecs,
                              out_shape=outs, scratch_shapes=list(scratch_shapes),
                              compiler_params=_params(len(grid)))

        def run_plain(*args):
            res = call(*args)
            return res[0] if single else res
        return run_plain

    n_in, n_out, n_scr = len(in_specs), len(outs), len(scratch_shapes)
    r_in, r_out = len(ride.inputs), len(ride.out_shapes)

    def carrying_body(*refs):
        at = 0
        ins = refs[at:at + n_in]; at += n_in
        r_ins = refs[at:at + r_in]; at += r_in
        os_ = refs[at:at + n_out]; at += n_out
        r_outs = refs[at:at + r_out]; at += r_out
        scr = refs[at:at + n_scr]; at += n_scr
        r_sems = refs[at:]
        step = pl.program_id(0)
        for ax in range(1, len(grid)):
            step = step * grid[ax] + pl.program_id(ax)

        @pl.when(step == 0)
        def _():
            ride.start(r_ins, r_outs, r_sems)

        body(*ins, *os_, *scr)

        if ride.mid is not None:
            @pl.when(step == total // 2)
            def _():
                ride.mid(r_ins, r_outs, r_sems)

        @pl.when(step == total - 1)
        def _():
            ride.finish(r_ins, r_outs, r_sems)

    call = pl.pallas_call(
        carrying_body, name=name, grid=grid, in_specs=list(in_specs) + [HBM] * r_in,
        out_specs=ospecs + [HBM] * r_out, out_shape=outs + ride.out_shapes,
        scratch_shapes=list(scratch_shapes) + ride.sems, compiler_params=_params(len(grid)))

    def run_carrying(*args):
        res = call(*args, *ride.inputs)
        ride.then(list(res[n_out:]))
        return res[0] if single else list(res[:n_out])
    return run_carrying


def _tile(n, target):
    if n <= target:
        return n
    best = None
    for t in range(LANE, target + 1, LANE):
        if n % t == 0:
            best = t
    assert best is not None, (n, target)
    return best


def _rows(n, target):
    if n <= target:
        return n
    best = None
    for t in range(16, target + 1, 16):
        if n % t == 0:
            best = t
    assert best is not None, (n, target)
    return best


def _bdot(a, b, dims):
    return lax.dot_general(a.astype(BF16), b.astype(BF16), dims, preferred_element_type=F32)


def _split(a):
    hi = a.astype(BF16)
    return hi, (a - hi.astype(F32)).astype(BF16)


def _dot3(a, b, dims):
    ah, al = _split(a)
    bh, bl = _split(b)
    d = lambda p, q: lax.dot_general(p, q, dims, preferred_element_type=F32)
    return d(ah, bh) + d(ah, bl) + d(al, bh)


def _sigmoid(x):
    return 1.0 / (1.0 + jnp.exp(-x))


def _silu(x):
    return x * _sigmoid(x)


def matmul(a, b, mode, out_dtype, name, tm=1024, tn=1536, tk=1536, pieces=None):
    if mode == "nn":
        (M, K), (K2, N) = a.shape, b.shape
    elif mode == "nt":
        (M, K), (N, K2) = a.shape, b.shape
    else:
        (K, M), (K2, N) = a.shape, b.shape
    assert K == K2, (a.shape, b.shape, mode)
    if pieces == "col":
        tm, tn = _tile(M // 2, tm), _tile(N // N_CHIPS, tn)
    elif pieces == "row":
        quarter = M // N_CHIPS
        tm = 2 * quarter if (2 * quarter) % LANE == 0 else M
        tn = _tile(N // 2, tn)
    else:
        tm, tn = _tile(M, tm), _tile(N, tn)
    tk = _tile(K, tk)
    nk = K // tk
    dims = {"nn": NN, "nt": NT, "tn": TN}[mode]
    if mode == "tn":
        a_spec = pl.BlockSpec((tk, tm), lambda i, j, k: (k, i))
    else:
        a_spec = pl.BlockSpec((tm, tk), lambda i, j, k: (i, k))
    if mode == "nt":
        b_spec = pl.BlockSpec((tn, tk), lambda i, j, k: (j, k))
    else:
        b_spec = pl.BlockSpec((tk, tn), lambda i, j, k: (k, j))

    out_spec = pl.BlockSpec((tm, tn), lambda i, j, k: (i, j))
    out_shape = jax.ShapeDtypeStruct((M, N), out_dtype)
    rows_per_slot = tm
    if pieces == "col":
        nih, njc = (M // 2) // tm, (N // N_CHIPS) // tn
        out_spec = pl.BlockSpec((None, None, tm, tn), lambda i, j, k: (i // nih, j // njc, i % nih, j % njc))
        out_shape = jax.ShapeDtypeStruct((2, N_CHIPS, M // 2, N // N_CHIPS), out_dtype)
    elif pieces == "row":
        rows_per_slot = M // N_CHIPS
        njh = (N // 2) // tn
        out_spec = pl.BlockSpec((None, tm // rows_per_slot, rows_per_slot, tn),
                                lambda i, j, k: (j // njh, i, 0, j % njh))
        out_shape = jax.ShapeDtypeStruct((2, N_CHIPS, rows_per_slot, N // 2), out_dtype)

    def store(o_ref, acc):
        if pieces == "row":
            for s in range(tm // rows_per_slot):
                o_ref[s] = acc[s * rows_per_slot:(s + 1) * rows_per_slot, :].astype(out_dtype)
        else:
            o_ref[...] = acc.astype(out_dtype)

    def body(a_ref, b_ref, o_ref, acc_ref):
        k = pl.program_id(2)
        p = _bdot(a_ref[...], b_ref[...], dims)
        if nk == 1:
            store(o_ref, p)
            return

        @pl.when(k == 0)
        def _():
            acc_ref[...] = p

        @pl.when(k > 0)
        def _():
            acc_ref[...] += p

        @pl.when(k == nk - 1)
        def _():
            store(o_ref, acc_ref[...])

    return _pcall(
        body, name=name, grid=(M // tm, N // tn, nk),
        in_specs=[a_spec, b_spec], out_specs=out_spec, out_shape=out_shape,
        scratch_shapes=[pltpu.VMEM((tm, tn), F32)],
        long_call=True)(a, b)


def _row_spec(d):
    return pl.BlockSpec((1, d), lambda i: (0, 0))


def modnorm_fwd(x, gain, sc, sh, name):
    T, D = x.shape
    tb = _rows(T, 512)

    def body(x_ref, g_ref, sc_ref, sh_ref, o_ref):
        xv = x_ref[...]
        r = lax.rsqrt(jnp.mean(xv * xv, axis=-1, keepdims=True) + RMS_EPS)
        o_ref[...] = ((xv * r) * g_ref[...] * (1.0 + sc_ref[...]) + sh_ref[...]).astype(BF16)

    blk = pl.BlockSpec((tb, D), lambda i: (i, 0))
    return _pcall(
        body, name=name, grid=(T // tb,),
        in_specs=[blk, _row_spec(D), _row_spec(D), _row_spec(D)],
        out_specs=blk, out_shape=jax.ShapeDtypeStruct((T, D), BF16),
        compiler_params=_params(1))(x, gain, sc, sh)


def modnorm_bwd(x, gain, sc, dh, dres, name):
    T, D = x.shape
    tb = _rows(T, 512)

    def body(x_ref, g_ref, sc_ref, dh_ref, dres_ref, dx_ref, dg_ref, dsc_ref, dsh_ref):
        i = pl.program_id(0)
        xv = x_ref[...]
        r = lax.rsqrt(jnp.mean(xv * xv, axis=-1, keepdims=True) + RMS_EPS)
        n = xv * r
        dhv = dh_ref[...].astype(F32)
        gain_v, sc1 = g_ref[...], 1.0 + sc_ref[...]
        dn = dhv * (gain_v * sc1)
        dx_ref[...] = r * (dn - n * jnp.mean(dn * n, axis=-1, keepdims=True)) + dres_ref[...]
        dhn = dhv * n

        @pl.when(i == 0)
        def _():
            dg_ref[...] = jnp.zeros_like(dg_ref)
            dsc_ref[...] = jnp.zeros_like(dsc_ref)
            dsh_ref[...] = jnp.zeros_like(dsh_ref)

        dg_ref[...] += jnp.sum(dhn * sc1, axis=0, keepdims=True)
        dsc_ref[...] += jnp.sum(dhn * gain_v, axis=0, keepdims=True)
        dsh_ref[...] += jnp.sum(dhv, axis=0, keepdims=True)

    blk = pl.BlockSpec((tb, D), lambda i: (i, 0))
    row = jax.ShapeDtypeStruct((1, D), F32)
    return _pcall(
        body, name=name, grid=(T // tb,),
        in_specs=[blk, _row_spec(D), _row_spec(D), blk, blk],
        out_specs=[blk, _row_spec(D), _row_spec(D), _row_spec(D)],
        out_shape=[jax.ShapeDtypeStruct((T, D), F32), row, row, row],
        compiler_params=_params(1))(x, gain, sc, dh, dres)


def gres_fwd(x, g, y, name):
    T, D = x.shape
    tb = _rows(T, 512)

    def body(x_ref, g_ref, y_ref, o_ref):
        o_ref[...] = x_ref[...] + g_ref[...] * y_ref[...]

    blk = pl.BlockSpec((tb, D), lambda i: (i, 0))
    return _pcall(
        body, name=name, grid=(T // tb,), in_specs=[blk, _row_spec(D), blk], out_specs=blk,
        out_shape=jax.ShapeDtypeStruct((T, D), F32), compiler_params=_params(1))(x, g, y)


def gres_bwd(dx, g, y, name):
    T, D = dx.shape
    tb = _rows(T, 512)

    def body(dx_ref, g_ref, y_ref, dy_ref, dg_ref):
        i = pl.program_id(0)
        dxv = dx_ref[...]
        dy_ref[...] = (dxv * g_ref[...]).astype(BF16)

        @pl.when(i == 0)
        def _():
            dg_ref[...] = jnp.zeros_like(dg_ref)

        dg_ref[...] += jnp.sum(dxv * y_ref[...], axis=0, keepdims=True)

    blk = pl.BlockSpec((tb, D), lambda i: (i, 0))
    return _pcall(
        body, name=name, grid=(T // tb,), in_specs=[blk, _row_spec(D), blk],
        out_specs=[blk, _row_spec(D)],
        out_shape=[jax.ShapeDtypeStruct((T, D), BF16), jax.ShapeDtypeStruct((1, D), F32)],
        compiler_params=_params(1))(dx, g, y)


def loss_head(y, target, name):
    T, D = y.shape
    tb = _rows(T, 512)

    def body(y_ref, t_ref, l_ref, dy_ref):
        i = pl.program_id(0)
        err = y_ref[...] - t_ref[...]
        dy_ref[...] = err * (1.0 / D)

        @pl.when(i == 0)
        def _():
            l_ref[...] = jnp.zeros_like(l_ref)

        sq = jnp.sum(err * err, axis=0, keepdims=True)
        tot = sq[:, 0:LANE]
        for k in range(1, D // LANE):
            tot = tot + sq[:, k * LANE:(k + 1) * LANE]
        l_ref[...] += tot

    blk = pl.BlockSpec((tb, D), lambda i: (i, 0))
    return _pcall(
        body, name=name, grid=(T // tb,), in_specs=[blk, blk],
        out_specs=[_row_spec(LANE), blk],
        out_shape=[jax.ShapeDtypeStruct((1, LANE), F32), jax.ShapeDtypeStruct((T, D), F32)],
        compiler_params=_params(1))(y, target)


def _back(ext, s):
    return ext if s == 0 else pltpu.roll(ext, s, 0)


def _ahead(ext, s):
    return ext if s == 0 else pltpu.roll(ext, ext.shape[0] - s, 0)


def _halo_prev(tb, h):
    return lambda i, j: (jnp.maximum(i * (tb // h) - 1, 0), j)


def _halo_next(tb, h, nrb):
    return lambda i, j: (jnp.minimum(i + 1, nrb - 1) * (tb // h), j)


FFN_TB, FFN_CB = 256, 1408
HALO16 = 16


def ffn_mid_fwd(up, conv_w8, conv_b, name):
    T, F2 = up.shape
    Fd = F2 // 2
    tb, cb = _rows(T, FFN_TB), _tile(Fd, FFN_CB)
    ncb = Fd // cb
    H = HALO16

    def body(g_ref, gp_ref, v_ref, w_ref, b_ref, o_ref):
        i = pl.program_id(0)
        g = g_ref[...].astype(F32)
        prev = jnp.where(i > 0, gp_ref[...].astype(F32), 0.0)
        ext = jnp.concatenate([prev, g], axis=0)
        w = w_ref[...]
        gc = w[2:3] * g + w[1:2] * _back(ext, 1)[H:] + w[0:1] * _back(ext, 2)[H:] + b_ref[...]
        o_ref[...] = (_silu(gc) * v_ref[...].astype(F32)).astype(BF16)

    return _pcall(
        body, name=name, grid=(T // tb, ncb),
        in_specs=[pl.BlockSpec((tb, cb), lambda i, j: (i, j)),
                  pl.BlockSpec((H, cb), _halo_prev(tb, H)),
                  pl.BlockSpec((tb, cb), lambda i, j: (i, j + ncb)),
                  pl.BlockSpec((SUBLANE, cb), lambda i, j: (0, j)),
                  pl.BlockSpec((1, cb), lambda i, j: (0, j))],
        out_specs=pl.BlockSpec((tb, cb), lambda i, j: (i, j)),
        out_shape=jax.ShapeDtypeStruct((T, Fd), BF16),
        long_call=True)(up, up, up, conv_w8, conv_b)


def ffn_mid_bwd(up, conv_w8, conv_b, dact, name):
    T, F2 = up.shape
    Fd = F2 // 2
    tb, cb = _rows(T, FFN_TB), _tile(Fd, FFN_CB)
    ncb, nrb = Fd // cb, T // tb
    H = HALO16

    def body(g_ref, gp_ref, gn_ref, v_ref, vn_ref, d_ref, dn_ref, w_ref, b_ref,
             dg_ref, dv_ref, dw_ref, db_ref):
        i = pl.program_id(1)
        g = g_ref[...].astype(F32)
        prev = jnp.where(i > 0, gp_ref[...].astype(F32), 0.0)
        ext = jnp.concatenate([prev, g, gn_ref[...].astype(F32)], axis=0)
        w = w_ref[...]
        e1, e2 = _back(ext, 1), _back(ext, 2)
        gc = (w[2:3] * ext + w[1:2] * e1 + w[0:1] * e2 + b_ref[...])[H:]
        val = jnp.concatenate([v_ref[...], vn_ref[...]], axis=0).astype(F32)
        dnext = jnp.where(i < nrb - 1, dn_ref[...].astype(F32), 0.0)
        da = jnp.concatenate([d_ref[...].astype(F32), dnext], axis=0)
        sg = _sigmoid(gc)
        dv_ref[...] = (da * gc * sg)[:tb].astype(BF16)
        dgc = da * val * (sg * (1.0 + gc * (1.0 - sg)))
        dg_ref[...] = (w[2:3] * dgc + w[1:2] * _ahead(dgc, 1) + w[0:1] * _ahead(dgc, 2))[:tb].astype(BF16)
        dc = dgc[:tb]

        @pl.when(i == 0)
        def _():
            dw_ref[...] = jnp.zeros_like(dw_ref)
            db_ref[...] = jnp.zeros_like(db_ref)

        dw_ref[2:3, :] += jnp.sum(dc * g, axis=0, keepdims=True)
        dw_ref[1:2, :] += jnp.sum(dc * e1[H:H + tb], axis=0, keepdims=True)
        dw_ref[0:1, :] += jnp.sum(dc * e2[H:H + tb], axis=0, keepdims=True)
        db_ref[...] += jnp.sum(dc, axis=0, keepdims=True)

    cur = lambda j, i: (i, j)
    prv = lambda j, i: _halo_prev(tb, H)(i, j)
    nxt = lambda j, i: _halo_next(tb, H, nrb)(i, j)
    return _pcall(
        body, name=name, grid=(ncb, nrb),
        in_specs=[pl.BlockSpec((tb, cb), cur), pl.BlockSpec((H, cb), prv), pl.BlockSpec((H, cb), nxt),
                  pl.BlockSpec((tb, cb), lambda j, i: (i, j + ncb)),
                  pl.BlockSpec((H, cb), lambda j, i: (jnp.minimum(i + 1, nrb - 1) * (tb // H), j + ncb)),
                  pl.BlockSpec((tb, cb), cur), pl.BlockSpec((H, cb), nxt),
                  pl.BlockSpec((SUBLANE, cb), lambda j, i: (0, j)),
                  pl.BlockSpec((1, cb), lambda j, i: (0, j))],
        out_specs=[pl.BlockSpec((tb, cb), cur), pl.BlockSpec((tb, cb), cur),
                   pl.BlockSpec((SUBLANE, cb), lambda j, i: (0, j)),
                   pl.BlockSpec((1, cb), lambda j, i: (0, j))],
        out_shape=[jax.ShapeDtypeStruct((T, Fd), BF16), jax.ShapeDtypeStruct((T, Fd), BF16),
                   jax.ShapeDtypeStruct((SUBLANE, Fd), F32), jax.ShapeDtypeStruct((1, Fd), F32)],
        long_call=True)(up, up, up, up, up, dact, dact, conv_w8, conv_b)


GDN_W = GDN_H * HD


def _head_l2norm(a, apply):
    parts = []
    for h in range(GDN_H):
        ah = a[:, h * HD:(h + 1) * HD]
        parts.append(ah * lax.rsqrt(jnp.sum(ah * ah, axis=-1, keepdims=True) + RMS_EPS))
    return jnp.where(apply, jnp.concatenate(parts, axis=1), a)


def _head_l2norm_bwd(a, dy, apply):
    parts = []
    for h in range(GDN_H):
        sl = slice(h * HD, (h + 1) * HD)
        ah, dh = a[:, sl], dy[:, sl]
        r = lax.rsqrt(jnp.sum(ah * ah, axis=-1, keepdims=True) + RMS_EPS)
        y = ah * r
        parts.append(r * (dh - y * jnp.sum(dh * y, axis=-1, keepdims=True)))
    return jnp.where(apply, jnp.concatenate(parts, axis=1), dy)


def gdn_conv_fwd(proj, w8, name):
    T = proj.shape[0]
    tb = _rows(T, 512)
    H = SUBLANE

    def body(x_ref, xp_ref, w_ref, o_ref):
        i, j = pl.program_id(0), pl.program_id(1)
        x = x_ref[...]
        prev = jnp.where(i > 0, xp_ref[...], 0.0)
        ext = jnp.concatenate([prev, x], axis=0)
        w = w_ref[...]
        c = (w[3:4] * x + w[2:3] * _back(ext, 1)[H:] + w[1:2] * _back(ext, 2)[H:]
             + w[0:1] * _back(ext, 3)[H:])
        o_ref[...] = _head_l2norm(_silu(c), j < 2)

    return _pcall(
        body, name=name, grid=(T // tb, 3),
        in_specs=[pl.BlockSpec((tb, GDN_W), lambda i, j: (i, j)),
                  pl.BlockSpec((H, GDN_W), _halo_prev(tb, H)),
                  pl.BlockSpec((SUBLANE, GDN_W), lambda i, j: (0, j))],
        out_specs=pl.BlockSpec((tb, GDN_W), lambda i, j: (i, j)),
        out_shape=jax.ShapeDtypeStruct((T, 3 * GDN_W), F32),
        compiler_params=_params(2))(proj, proj, w8)


def gdn_conv_bwd(proj, w8, dout, name):
    T = proj.shape[0]
    tb = _rows(T, 512)
    nrb = T // tb
    H = SUBLANE

    def body(x_ref, xp_ref, xn_ref, d_ref, dn_ref, w_ref, dx_ref, dw_ref):
        j, i = pl.program_id(0), pl.program_id(1)
        x = x_ref[...]
        prev = jnp.where(i > 0, xp_ref[...], 0.0)
        ext = jnp.concatenate([prev, x, xn_ref[...]], axis=0)
        w = w_ref[...]
        e1, e2, e3 = _back(ext, 1), _back(ext, 2), _back(ext, 3)
        c = (w[3:4] * ext + w[2:3] * e1 + w[1:2] * e2 + w[0:1] * e3)[H:]
        sg = _sigmoid(c)
        dnext = jnp.where(i < nrb - 1, dn_ref[...], 0.0)
        do = jnp.concatenate([d_ref[...], dnext], axis=0)
        da = _head_l2norm_bwd(c * sg, do, j < 2)
        dc = da * (sg * (1.0 + c * (1.0 - sg)))
        dx_ref[...] = (w[3:4] * dc + w[2:3] * _ahead(dc, 1) + w[1:2] * _ahead(dc, 2)
                       + w[0:1] * _ahead(dc, 3))[:tb].astype(BF16)
        dcc = dc[:tb]

        @pl.when(i == 0)
        def _():
            dw_ref[...] = jnp.zeros_like(dw_ref)

        dw_ref[3:4, :] += jnp.sum(dcc * x, axis=0, keepdims=True)
        dw_ref[2:3, :] += jnp.sum(dcc * e1[H:H + tb], axis=0, keepdims=True)
        dw_ref[1:2, :] += jnp.sum(dcc * e2[H:H + tb], axis=0, keepdims=True)
        dw_ref[0:1, :] += jnp.sum(dcc * e3[H:H + tb], axis=0, keepdims=True)

    cur = lambda j, i: (i, j)
    prv = lambda j, i: _halo_prev(tb, H)(i, j)
    nxt = lambda j, i: _halo_next(tb, H, nrb)(i, j)
    return _pcall(
        body, name=name, grid=(3, nrb),
        in_specs=[pl.BlockSpec((tb, GDN_W), cur), pl.BlockSpec((H, GDN_W), prv), pl.BlockSpec((H, GDN_W), nxt),
                  pl.BlockSpec((tb, GDN_W), cur), pl.BlockSpec((H, GDN_W), nxt),
                  pl.BlockSpec((SUBLANE, GDN_W), lambda j, i: (0, j))],
        out_specs=[pl.BlockSpec((tb, GDN_W), cur), pl.BlockSpec((SUBLANE, GDN_W), lambda j, i: (0, j))],
        out_shape=[jax.ShapeDtypeStruct((T, 3 * GDN_W), BF16),
                   jax.ShapeDtypeStruct((SUBLANE, 3 * GDN_W), F32)],
        compiler_params=_params(2))(proj, proj, proj, dout, dout, w8)


def _dot_family(dot, diff):
    if not diff:
        return tuple(functools.partial(lambda d, a, b: dot(a, b, d), d) for d in (BNN, BNT, BTN))

    @jax.custom_vjp
    def nn(a, b):
        return dot(a, b, BNN)
    nn.defvjp(lambda a, b: (dot(a, b, BNN), (a, b)),
              lambda res, g: (dot(g, res[1], BNT), dot(res[0], g, BTN)))

    @jax.custom_vjp
    def nt(a, b):
        return dot(a, b, BNT)
    nt.defvjp(lambda a, b: (dot(a, b, BNT), (a, b)),
              lambda res, g: (dot(g, res[1], BNN), dot(g, res[0], BTN)))

    @jax.custom_vjp
    def tn(a, b):
        return dot(a, b, BTN)
    tn.defvjp(lambda a, b: (dot(a, b, BTN), (a, b)),
              lambda res, g: (dot(res[1], g, BNT), dot(res[0], g, BNN)))
    return nn, nt, tn


def _gdn_step(dots, hdots, S, q, k, v, z, b_raw, a_raw, alog, dtb, gnorm):
    nn, nt, tn = dots
    hnn = hdots[0]
    B, C = q.shape[0], GDN_CHUNK
    ii = lax.broadcasted_iota(jnp.int32, (B, C, C), 1)
    jj = lax.broadcasted_iota(jnp.int32, (B, C, C), 2)
    causal, strict = ii >= jj, ii > jj
    tri, tri_t = causal.astype(F32), (ii <= jj).astype(F32)
    eye, ones = (ii == jj).astype(F32), jnp.ones((B, C, C), F32)

    beta = _sigmoid(b_raw)
    xs = a_raw + dtb
    pos = xs > 0.0
    softplus = jnp.where(pos, xs, 0.0) + jnp.log(1.0 + jnp.exp(jnp.where(pos, -xs, xs)))
    g = -jnp.exp(alog) * softplus
    gb = jnp.broadcast_to(g, (B, C, C))
    gc_c = hnn(tri, gb)
    gc_r = hnn(hnn(ones, eye * gb), tri_t)
    gc = hnn(tri, jnp.broadcast_to(g, (B, C, HD)))
    gl = jnp.sum(g, axis=1, keepdims=True)
    decay = jnp.where(causal, jnp.exp(jnp.where(causal, gc_c - gc_r, 0.0)), 0.0)
    q = q * (HD ** -0.5)
    kb = k * beta
    L = jnp.where(strict, nt(kb, k) * decay, 0.0)
    egc = jnp.exp(gc)
    P = eye - L
    M = hnn(L, L)
    for step in range(5):
        P = P + hnn(P, M)
        if step < 4:
            M = hnn(M, M)
    u = hnn(P, v * beta)
    w = hnn(P, kb * egc)
    intra = jnp.where(causal, nt(q, k) * decay, 0.0)
    qg = q * egc
    kdec = k * jnp.exp(gl - gc)
    egl = jnp.exp(gl)
    outs = []
    for ci in range(B // GDN_H):
        sl = slice(ci * GDN_H, (ci + 1) * GDN_H)
        v_new = u[sl] - nn(w[sl], S)
        outs.append(nn(qg[sl], S) + nn(intra[sl], v_new))
        S = S * egl[sl] + tn(kdec[sl], v_new)
    o = jnp.concatenate(outs, axis=0)
    r = lax.rsqrt(jnp.mean(o * o, axis=-1, keepdims=True) + RMS_EPS)
    return o * r * gnorm * _silu(z), S


def _gdn_batches(qkv, ba, z, alog_row, dt_row):
    C = GDN_CHUNK
    q, k, v, zz, b_raw, a_raw, alog, dtb = ([] for _ in range(8))
    for ci in range(GDN_STEP):
        rows = slice(ci * C, (ci + 1) * C)
        for h in range(GDN_H):
            q.append(qkv[rows, h * HD:(h + 1) * HD])
            k.append(qkv[rows, GDN_W + h * HD:GDN_W + (h + 1) * HD])
            v.append(qkv[rows, 2 * GDN_W + h * HD:2 * GDN_W + (h + 1) * HD])
            zz.append(z[rows, h * HD:(h + 1) * HD])
            b_raw.append(ba[rows, h:h + 1])
            a_raw.append(ba[rows, GDN_H + h:GDN_H + h + 1])
            alog.append(alog_row[:, h:h + 1])
            dtb.append(dt_row[:, h:h + 1])
    return tuple(jnp.stack(t) for t in (q, k, v, zz, b_raw, a_raw, alog, dtb))


def gdn_chunk_fwd(qkv, proj, alog_row, dt_row, gnorm, name):
    T = qkv.shape[0]
    R = GDN_CHUNK * GDN_STEP
    N = T // R
    dots, hdots = _dot_family(_bdot, False), _dot_family(_dot3, False)

    def body(qkv_ref, ba_ref, z_ref, al_ref, dt_ref, gn_ref, o_ref, save_ref, S_ref):
        n = pl.program_id(0)

        @pl.when(n == 0)
        def _():
            S_ref[...] = jnp.zeros_like(S_ref)

        S = S_ref[...]
        save_ref[0] = S
        batches = _gdn_batches(qkv_ref[...], ba_ref[...], z_ref[...], al_ref[...], dt_ref[...])
        o, S_new = _gdn_step(dots, hdots, S, *batches, gn_ref[...])
        S_ref[...] = S_new
        for ci in range(GDN_STEP):
            for h in range(GDN_H):
                o_ref[ci * GDN_CHUNK:(ci + 1) * GDN_CHUNK, h * HD:(h + 1) * HD] = o[ci * GDN_H + h].astype(BF16)

    return _pcall(
        body, name=name, grid=(N,),
        in_specs=[pl.BlockSpec((R, 3 * GDN_W), lambda n: (n, 0)),
                  pl.BlockSpec((R, LANE), lambda n: (n, (4 * GDN_W + POOL_G * HD) // LANE)),
                  pl.BlockSpec((R, GDN_W), lambda n: (n, 3)),
                  _row_spec(LANE), _row_spec(LANE), _row_spec(HD)],
        out_specs=[pl.BlockSpec((R, GDN_W), lambda n: (n, 0)),
                   pl.BlockSpec((1, GDN_H, HD, HD), lambda n: (n, 0, 0, 0))],
        out_shape=[jax.ShapeDtypeStruct((T, GDN_W), BF16), jax.ShapeDtypeStruct((N, GDN_H, HD, HD), F32)],
        scratch_shapes=[pltpu.VMEM((GDN_H, HD, HD), F32)],
        long_call=True)(qkv, proj, proj, alog_row, dt_row, gnorm)


def gdn_chunk_bwd(qkv, proj, alog_row, dt_row, gnorm, saved, docat, name):
    T = qkv.shape[0]
    C = GDN_CHUNK
    R = C * GDN_STEP
    N = T // R
    dots, hdots = _dot_family(_bdot, True), _dot_family(_dot3, True)

    def body(qkv_ref, ba_ref, z_ref, al_ref, dt_ref, gn_ref, save_ref, do_ref,
             dqkv_ref, dz_ref, dba_ref, dal_ref, ddt_ref, dgn_ref, dS_ref):
        n = pl.program_id(0)

        @pl.when(n == 0)
        def _():
            dS_ref[...] = jnp.zeros_like(dS_ref)
            dal_ref[...] = jnp.zeros_like(dal_ref)
            ddt_ref[...] = jnp.zeros_like(ddt_ref)
            dgn_ref[...] = jnp.zeros_like(dgn_ref)

        batches = _gdn_batches(qkv_ref[...], ba_ref[...], z_ref[...], al_ref[...], dt_ref[...])
        do = do_ref[...]
        do_b = jnp.stack([do[ci * C:(ci + 1) * C, h * HD:(h + 1) * HD]
                          for ci in range(GDN_STEP) for h in range(GDN_H)])
        fn = functools.partial(_gdn_step, dots, hdots)
        _, vjp = jax.vjp(fn, save_ref[0], *batches, gn_ref[...])
        dS, dq, dk, dv, dz, db_raw, da_raw, dalog, ddtb, dgn = vjp((do_b, dS_ref[...]))
        dS_ref[...] = dS
        lane = lax.broadcasted_iota(jnp.int32, (1, LANE), 1)
        dal = jnp.zeros((1, LANE), F32)
        ddt = jnp.zeros((1, LANE), F32)
        for ci in range(GDN_STEP):
            rows = slice(ci * C, (ci + 1) * C)
            dba = jnp.zeros((C, LANE), F32)
            for h in range(GDN_H):
                b = ci * GDN_H + h
                dqkv_ref[rows, h * HD:(h + 1) * HD] = dq[b]
                dqkv_ref[rows, GDN_W + h * HD:GDN_W + (h + 1) * HD] = dk[b]
                dqkv_ref[rows, 2 * GDN_W + h * HD:2 * GDN_W + (h + 1) * HD] = dv[b]
                dz_ref[rows, h * HD:(h + 1) * HD] = dz[b].astype(BF16)
                hot_b = (lane == h).astype(F32)
                dba = dba + db_raw[b] * hot_b + da_raw[b] * (lane == GDN_H + h).astype(F32)
                dal = dal + dalog[b] * hot_b
                ddt = ddt + ddtb[b] * hot_b
            dba_ref[rows, :] = dba.astype(BF16)
        dal_ref[...] += dal
        ddt_ref[...] += ddt
        dgn_ref[...] += dgn

    rev = lambda n: N - 1 - n
    row = jax.ShapeDtypeStruct((1, LANE), F32)
    return _pcall(
        body, name=name, grid=(N,),
        in_specs=[pl.BlockSpec((R, 3 * GDN_W), lambda n: (rev(n), 0)),
                  pl.BlockSpec((R, LANE), lambda n: (rev(n), (4 * GDN_W + POOL_G * HD) // LANE)),
                  pl.BlockSpec((R, GDN_W), lambda n: (rev(n), 3)),
                  _row_spec(LANE), _row_spec(LANE), _row_spec(HD),
                  pl.BlockSpec((1, GDN_H, HD, HD), lambda n: (rev(n), 0, 0, 0)),
                  pl.BlockSpec((R, GDN_W), lambda n: (rev(n), 0))],
        out_specs=[pl.BlockSpec((R, 3 * GDN_W), lambda n: (rev(n), 0)),
                   pl.BlockSpec((R, GDN_W), lambda n: (rev(n), 0)),
                   pl.BlockSpec((R, LANE), lambda n: (rev(n), 0)),
                   _row_spec(LANE), _row_spec(LANE), _row_spec(HD)],
        out_shape=[jax.ShapeDtypeStruct((T, 3 * GDN_W), F32), jax.ShapeDtypeStruct((T, GDN_W), BF16),
                   jax.ShapeDtypeStruct((T, LANE), BF16), row, row, jax.ShapeDtypeStruct((1, HD), F32)],
        scratch_shapes=[pltpu.VMEM((GDN_H, HD, HD), F32)],
        long_call=True)(qkv, proj, proj, alog_row, dt_row, gnorm, saved, docat)


POOL_HALO = 16


def _pool_pick(j, s2, s4, s8, s16):
    return jnp.where(j == 0, s2, jnp.where(j == 1, s4, jnp.where(j == 2, s8, s16)))


def _pool_count(j, t0, rows):
    t1 = (t0 + 1 + lax.broadcasted_iota(jnp.int32, (rows, 1), 0)).astype(F32)
    win = jnp.where(j == 0, 2.0, jnp.where(j == 1, 4.0, jnp.where(j == 2, 8.0, 16.0)))
    return jnp.minimum(t1, win)


def _pooled(p, prev, i, j, tb):
    ext = jnp.concatenate([prev, p], axis=0)
    s2 = ext + _back(ext, 1)
    s4 = s2 + _back(s2, 2)
    s8 = s4 + _back(s4, 4)
    s16 = s8 + _back(s8, 8)
    s = _pool_pick(j, s2, s4, s8, s16)[POOL_HALO:]
    return s / _pool_count(j, i * tb, tb) - p


def pool_fwd(proj, pool_w, pool_scale, name):
    T = proj.shape[0]
    tb = _rows(T, 512)
    c0 = 4 * GDN_H

    def body(p_ref, pp_ref, w_ref, s_ref, o_ref):
        i, j = pl.program_id(0), pl.program_id(1)
        p = p_ref[...]
        prev = jnp.where(i > 0, pp_ref[...], 0.0)
        pooled = _pooled(p, prev, i, j, tb)
        o_ref[...] = (_bdot(pooled, w_ref[0], NN) * s_ref[...]).astype(BF16)

    return _pcall(
        body, name=name, grid=(T // tb, POOL_G),
        in_specs=[pl.BlockSpec((tb, HD), lambda i, j: (i, c0 + j)),
                  pl.BlockSpec((POOL_HALO, HD), lambda i, j: (jnp.maximum(i * (tb // POOL_HALO) - 1, 0), c0 + j)),
                  pl.BlockSpec((1, HD, HD), lambda i, j: (j, 0, 0)),
                  pl.BlockSpec((1, HD), lambda i, j: (0, j))],
        out_specs=pl.BlockSpec((tb, HD), lambda i, j: (i, j)),
        out_shape=jax.ShapeDtypeStruct((T, POOL_G * HD), BF16),
        compiler_params=_params(2))(proj, proj, pool_w, pool_scale)


def pool_bwd(proj, pool_w, pool_scale, docat, name):
    T = proj.shape[0]
    tb = _rows(T, 512)
    nrb = T // tb
    c0 = 4 * GDN_H
    HB = POOL_HALO

    def body(p_ref, pp_ref, w_ref, s_ref, d_ref, dn_ref, dp_ref, dw_ref, ds_ref):
        j, i = pl.program_id(0), pl.program_id(1)
        p = p_ref[...]
        prev = jnp.where(i > 0, pp_ref[...], 0.0)
        pooled = _pooled(p, prev, i, j, tb)
        w, scale = w_ref[0], s_ref[...]
        dy = d_ref[...]
        dnext = jnp.where(i < nrb - 1, dn_ref[...], 0.0)
        dyp = jnp.concatenate([dy, dnext], axis=0) * scale
        dpooled = _bdot(dyp, w, NT)
        qn = dpooled / _pool_count(j, i * tb, tb + HB)
        a2 = qn + _ahead(qn, 1)
        a4 = a2 + _ahead(a2, 2)
        a8 = a4 + _ahead(a4, 4)
        a16 = a8 + _ahead(a8, 8)
        dp_ref[...] = (_pool_pick(j, a2, a4, a8, a16) - dpooled)[:tb].astype(BF16)

        @pl.when(i == 0)
        def _():
            dw_ref[...] = jnp.zeros_like(dw_ref)
            ds_ref[...] = jnp.zeros_like(ds_ref)

        dw_ref[0] += _bdot(pooled, dyp[:tb], TN)
        ds_ref[...] += jnp.sum(dy * _bdot(pooled, w, NN), axis=0, keepdims=True)

    return _pcall(
        body, name=name, grid=(POOL_G, nrb),
        in_specs=[pl.BlockSpec((tb, HD), lambda j, i: (i, c0 + j)),
                  pl.BlockSpec((HB, HD), lambda j, i: (jnp.maximum(i * (tb // HB) - 1, 0), c0 + j)),
                  pl.BlockSpec((1, HD, HD), lambda j, i: (j, 0, 0)),
                  pl.BlockSpec((1, HD), lambda j, i: (0, j)),
                  pl.BlockSpec((tb, HD), lambda j, i: (i, POOL_G + j)),
                  pl.BlockSpec((HB, HD), lambda j, i: (jnp.minimum(i + 1, nrb - 1) * (tb // HB), POOL_G + j))],
        out_specs=[pl.BlockSpec((tb, HD), lambda j, i: (i, j)),
                   pl.BlockSpec((1, HD, HD), lambda j, i: (j, 0, 0)),
                   pl.BlockSpec((1, HD), lambda j, i: (0, j))],
        out_shape=[jax.ShapeDtypeStruct((T, POOL_G * HD), BF16),
                   jax.ShapeDtypeStruct((POOL_G, HD, HD), F32),
                   jax.ShapeDtypeStruct((1, POOL_G * HD), F32)],
        compiler_params=_params(2))(proj, proj, pool_w, pool_scale, docat, docat)


ATT_W = ATT_H * HD
GROUP_COLS = 3 * ATT_W


def to_residue_major(t, d):
    if d == 1:
        return t
    T, C = t.shape
    return t.reshape(T // d, d, C).transpose(1, 0, 2).reshape(T, C)


def to_token_order(t, d):
    if d == 1:
        return t
    T, C = t.shape
    return t.reshape(d, T // d, C).transpose(1, 0, 2).reshape(T, C)


def headnorm_fwd(proj, qk_gain, name):
    T = proj.shape[0]
    tb = _rows(T, 256)

    def body(x_ref, g_ref, o_ref):
        g = g_ref[...]
        for h in range(2 * ATT_H):
            sl = slice(h * HD, (h + 1) * HD)
            x = x_ref[:, sl].astype(F32)
            n = x * lax.rsqrt(jnp.mean(x * x, axis=-1, keepdims=True) + RMS_EPS)
            gain = g[0:1] * (HD ** -0.5) if h < ATT_H else g[1:2]
            o_ref[:, sl] = (n * gain).astype(BF16)
        o_ref[:, 2 * ATT_W:] = x_ref[:, 2 * ATT_W:]

    blk = pl.BlockSpec((tb, GROUP_COLS), lambda i: (i, 0))
    return _pcall(
        body, name=name, grid=(T // tb,),
        in_specs=[blk, pl.BlockSpec((SUBLANE, HD), lambda i: (0, 0))],
        out_specs=blk, out_shape=jax.ShapeDtypeStruct((T, GROUP_COLS), BF16),
        long_call=True)(proj, qk_gain)


def headnorm_bwd(proj, qk_gain, dq, dk, dv, name):
    T = proj.shape[0]
    tb = _rows(T, 256)

    def body(x_ref, g_ref, dq_ref, dk_ref, dv_ref, dx_ref, dg_ref):
        i = pl.program_id(0)
        g = g_ref[...]

        @pl.when(i == 0)
        def _():
            dg_ref[...] = jnp.zeros_like(dg_ref)

        for part, d_ref in enumerate((dq_ref, dk_ref)):
            gain = g[0:1] * (HD ** -0.5) if part == 0 else g[1:2]
            scale = (HD ** -0.5) if part == 0 else 1.0
            acc = jnp.zeros((1, HD), F32)
            for h in range(ATT_H):
                x = x_ref[:, part * ATT_W + h * HD:part * ATT_W + (h + 1) * HD].astype(F32)
                d = d_ref[:, h * HD:(h + 1) * HD].astype(F32)
                r = lax.rsqrt(jnp.mean(x * x, axis=-1, keepdims=True) + RMS_EPS)
                n = x * r
                dn = d * gain
                dx = r * (dn - n * jnp.mean(dn * n, axis=-1, keepdims=True))
                dx_ref[:, part * ATT_W + h * HD:part * ATT_W + (h + 1) * HD] = dx.astype(BF16)
                acc = acc + jnp.sum(d * n, axis=0, keepdims=True)
            dg_ref[part:part + 1, :] += acc * scale
        dx_ref[:, 2 * ATT_W:] = dv_ref[...]

    blk = pl.BlockSpec((tb, GROUP_COLS), lambda i: (i, 0))
    dblk = pl.BlockSpec((tb, ATT_W), lambda i: (i, 0))
    gspec = pl.BlockSpec((SUBLANE, HD), lambda i: (0, 0))
    return _pcall(
        body, name=name, grid=(T // tb,),
        in_specs=[blk, gspec, dblk, dblk, dblk],
        out_specs=[blk, gspec],
        out_shape=[jax.ShapeDtypeStruct((T, GROUP_COLS), BF16), jax.ShapeDtypeStruct((SUBLANE, HD), F32)],
        long_call=True)(proj, qk_gain, dq, dk, dv)


def _heads(ref):
    return jnp.stack([ref[:, h * HD:(h + 1) * HD] for h in range(ATT_H)])


def _slopes(dil):
    h = lax.broadcasted_iota(jnp.int32, (ATT_H, 1, 1), 0)
    return lax.bitcast_convert_type((126 - h) << 23, F32) * float(dil)


def _att_scores_b(q, k, slope, n_ok, far, keys_first=False):
    r = lax.broadcasted_iota(jnp.int32, (1, ATT_BLK, ATT_BLK), 1)
    c = lax.broadcasted_iota(jnp.int32, (1, ATT_BLK, ATT_BLK), 2)
    a, j = (c, r) if keys_first else (r, c)
    rel = (ATT_BLK + a - j) if far else (a - j)
    mask = ((j >= a) & n_ok) if far else (j <= a)
    s = (_bdot(k, q, BNT) if keys_first else _bdot(q, k, BNT)) - slope * rel.astype(F32)
    return jnp.where(mask, s, NEG), mask


def _att_blocks(nb, width, shift):
    def make(col):
        return pl.BlockSpec((ATT_BLK, width),
                            lambda r, n: (r * nb + jnp.clip(n + shift, 0, nb - 1), col))
    return make


def _lane_col(cols):
    lane = lax.broadcasted_iota(jnp.int32, (1, LANE), 1)
    out = jnp.zeros((ATT_BLK, LANE), F32)
    for h, c in enumerate(cols):
        out = out + c * (lane == h).astype(F32)
    return out


def att_fwd(qkvn, gi, name):
    T = qkvn.shape[0]
    dil = DIL[gi]
    nb = T // dil // ATT_BLK

    def body(q_ref, kp_ref, kc_ref, vp_ref, vc_ref, o_ref, l_ref):
        n_ok = pl.program_id(1) > 0
        slope = _slopes(dil)
        q = _heads(q_ref)
        s_c, _ = _att_scores_b(q, _heads(kc_ref), slope, n_ok, False)
        s_p, _ = _att_scores_b(q, _heads(kp_ref), slope, n_ok, True)
        m = jnp.maximum(jnp.max(s_c, axis=-1, keepdims=True), jnp.max(s_p, axis=-1, keepdims=True))
        p_c, p_p = jnp.exp(s_c - m), jnp.exp(s_p - m)
        l = jnp.sum(p_c, axis=-1, keepdims=True) + jnp.sum(p_p, axis=-1, keepdims=True)
        o = (_bdot(p_c, _heads(vc_ref), BNN) + _bdot(p_p, _heads(vp_ref), BNN)) / l
        lse = m + jnp.log(l)
        for h in range(ATT_H):
            o_ref[:, h * HD:(h + 1) * HD] = o[h]
        l_ref[...] = _lane_col([lse[h] for h in range(ATT_H)])

    cur, prv = _att_blocks(nb, ATT_W, 0), _att_blocks(nb, ATT_W, -1)
    return _pcall(
        body, name=name, grid=(dil, nb), in_specs=[cur(0), prv(1), cur(1), prv(2), cur(2)],
        out_specs=[cur(0), _att_blocks(nb, LANE, 0)(0)],
        out_shape=[jax.ShapeDtypeStruct((T, ATT_W), F32), jax.ShapeDtypeStruct((T, LANE), F32)],
        long_call=True)(qkvn, qkvn, qkvn, qkvn, qkvn)


def att_merge(os, lses, name):
    T = os[0].shape[0]
    tb = _rows(T, 512)

    def body(o0, o1, o2, l0, l1, l2, o_ref, l_ref):
        a, b, c = l0[...], l1[...], l2[...]
        m = jnp.maximum(a, jnp.maximum(b, c))
        wa, wb, wc = jnp.exp(a - m), jnp.exp(b - m), jnp.exp(c - m)
        den = wa + wb + wc
        l_ref[...] = m + jnp.log(den)
        wa, wb, wc = wa / den, wb / den, wc / den
        for h in range(ATT_H):
            sl = slice(h * HD, (h + 1) * HD)
            o_ref[:, sl] = (wa[:, h:h + 1] * o0[:, sl] + wb[:, h:h + 1] * o1[:, sl]
                            + wc[:, h:h + 1] * o2[:, sl])

    blk = pl.BlockSpec((tb, ATT_W), lambda i: (i, 0))
    lblk = pl.BlockSpec((tb, LANE), lambda i: (i, 0))
    return _pcall(
        body, name=name, grid=(T // tb,), in_specs=[blk] * 3 + [lblk] * 3, out_specs=[blk, lblk],
        out_shape=[jax.ShapeDtypeStruct((T, ATT_W), F32), jax.ShapeDtypeStruct((T, LANE), F32)],
        compiler_params=_params(1))(*os, *lses)


def att_delta(do, o, name):
    T = do.shape[0]
    tb = _rows(T, 512)

    def body(d_ref, o_ref, out_ref):
        lane = lax.broadcasted_iota(jnp.int32, (1, LANE), 1)
        out = jnp.zeros((tb, LANE), F32)
        for h in range(ATT_H):
            sl = slice(h * HD, (h + 1) * HD)
            s = jnp.sum(d_ref[:, sl] * o_ref[:, sl], axis=-1, keepdims=True)
            out = out + s * (lane == h).astype(F32)
        out_ref[...] = out

    blk = pl.BlockSpec((tb, ATT_W), lambda i: (i, 0))
    return _pcall(
        body, name=name, grid=(T // tb,), in_specs=[blk, blk],
        out_specs=pl.BlockSpec((tb, LANE), lambda i: (i, 0)),
        out_shape=jax.ShapeDtypeStruct((T, LANE), F32), compiler_params=_params(1))(do, o)


def att_bwd_q(qkvn, do, lse, delta, gi, name):
    T = qkvn.shape[0]
    dil = DIL[gi]
    nb = T // dil // ATT_BLK

    def body(q_ref, kp_ref, kc_ref, vp_ref, vc_ref, do_ref, l_ref, d_ref, dq_ref):
        n_ok = pl.program_id(1) > 0
        slope = _slopes(dil)
        lse, dl = l_ref[...], d_ref[...]
        lse = jnp.stack([lse[:, h:h + 1] for h in range(ATT_H)])
        dl = jnp.stack([dl[:, h:h + 1] for h in range(ATT_H)])
        q, do = _heads(q_ref), _heads(do_ref)
        dq = jnp.zeros((ATT_H, ATT_BLK, HD), F32)
        for k_ref, v_ref, far in ((kc_ref, vc_ref, False), (kp_ref, vp_ref, True)):
            k = _heads(k_ref)
            s, mask = _att_scores_b(q, k, slope, n_ok, far)
            p = jnp.where(mask, jnp.exp(s - lse), 0.0)
            ds = p * (_bdot(do, _heads(v_ref), BNT) - dl)
            dq = dq + _bdot(ds, k, BNN)
        for h in range(ATT_H):
            dq_ref[:, h * HD:(h + 1) * HD] = dq[h].astype(BF16)

    cur, prv = _att_blocks(nb, ATT_W, 0), _att_blocks(nb, ATT_W, -1)
    small = _att_blocks(nb, LANE, 0)(0)
    return _pcall(
        body, name=name, grid=(dil, nb),
        in_specs=[cur(0), prv(1), cur(1), prv(2), cur(2), cur(0), small, small],
        out_specs=cur(0), out_shape=jax.ShapeDtypeStruct((T, ATT_W), BF16),
        long_call=True)(qkvn, qkvn, qkvn, qkvn, qkvn, do, lse, delta)


def att_bwd_kv(qkvn, do, lse, delta, gi, name):
    T = qkvn.shape[0]
    dil = DIL[gi]
    nb = T // dil // ATT_BLK

    def body(k_ref, v_ref, q0_ref, q1_ref, do0_ref, do1_ref, l0_ref, l1_ref, d0_ref, d1_ref,
             dk_ref, dv_ref):
        n_ok = pl.program_id(1) < nb - 1
        slope = _slopes(dil)
        by_row = lambda ref: jnp.stack([ref[...].T[h:h + 1, :] for h in range(ATT_H)])
        k, v = _heads(k_ref), _heads(v_ref)
        dk = jnp.zeros((ATT_H, ATT_BLK, HD), F32)
        dv = jnp.zeros((ATT_H, ATT_BLK, HD), F32)
        for q_ref, do_ref, l_ref, d_ref, far in ((q0_ref, do0_ref, l0_ref, d0_ref, False),
                                                 (q1_ref, do1_ref, l1_ref, d1_ref, True)):
            q, do = _heads(q_ref), _heads(do_ref)
            s, mask = _att_scores_b(q, k, slope, n_ok, far, keys_first=True)
            p = jnp.where(mask, jnp.exp(s - by_row(l_ref)), 0.0)
            dv = dv + _bdot(p, do, BNN)
            ds = p * (_bdot(v, do, BNT) - by_row(d_ref))
            dk = dk + _bdot(ds, q, BNN)
        for h in range(ATT_H):
            dk_ref[:, h * HD:(h + 1) * HD] = dk[h].astype(BF16)
            dv_ref[:, h * HD:(h + 1) * HD] = dv[h].astype(BF16)

    cur, nxt = _att_blocks(nb, ATT_W, 0), _att_blocks(nb, ATT_W, 1)
    s0, s1 = _att_blocks(nb, LANE, 0)(0), _att_blocks(nb, LANE, 1)(0)
    return _pcall(
        body, name=name, grid=(dil, nb),
        in_specs=[cur(1), cur(2), cur(0), nxt(0), cur(0), nxt(0), s0, s1, s0, s1],
        out_specs=[cur(0), cur(0)], out_shape=[jax.ShapeDtypeStruct((T, ATT_W), BF16)] * 2,
        long_call=True)(qkvn, qkvn, qkvn, qkvn, do, do, lse, lse, delta, delta)


def adamw(w, g, m, v, name):
    shape = w.shape
    C = shape[-1]
    R = math.prod(shape[:-1])
    to2d = lambda t: t.reshape(R, C)
    tb = _rows(R, max(16, (256 * 1536 // C) // 16 * 16))
    c1 = 1.0 - ADAM_B1 ** ADAM_STEP
    c2 = 1.0 - ADAM_B2 ** ADAM_STEP

    def body(w_ref, g_ref, m_ref, v_ref, d_ref, nm_ref, nv_ref):
        gv = g_ref[...]
        nm = ADAM_B1 * m_ref[...] + (1.0 - ADAM_B1) * gv
        nv = ADAM_B2 * v_ref[...] + (1.0 - ADAM_B2) * (gv * gv)
        d_ref[...] = -ADAM_LR * ((nm / c1) / (jnp.sqrt(nv / c2) + ADAM_EPS) + ADAM_WD * w_ref[...])
        nm_ref[...] = nm
        nv_ref[...] = nv

    blk = pl.BlockSpec((tb, C), lambda i: (i, 0))
    out = jax.ShapeDtypeStruct((R, C), F32)
    d, nm, nv = _pcall(
        body, name=name, grid=(R // tb,), in_specs=[blk] * 4, out_specs=[blk] * 3,
        out_shape=[out, out, out], compiler_params=_params(1))(to2d(w), to2d(g), to2d(m), to2d(v))
    return d.reshape(shape), nm.reshape(shape), nv.reshape(shape)


def _pad_rows8(w):
    return jnp.pad(w, ((0, SUBLANE - w.shape[0]), (0, 0)))


def _lane_row(v):
    return jnp.pad(v, (0, LANE - v.shape[0]))[None, :]


def _even_reorder(w_in):
    z4 = 4 * GDN_W
    pad = jnp.zeros((w_in.shape[0], EVEN_PAD - EVEN_COLS), w_in.dtype)
    return jnp.concatenate([w_in[:, :z4], w_in[:, z4 + 2 * GDN_H:], w_in[:, z4:z4 + 2 * GDN_H], pad], axis=1)


def _even_restore(dw):
    z4 = 4 * GDN_W
    p4 = POOL_G * HD
    return jnp.concatenate([dw[:, :z4], dw[:, z4 + p4:z4 + p4 + 2 * GDN_H], dw[:, z4:z4 + p4]], axis=1)


def _ffn_fwd(tag, x1, mod_f, wl):
    sh, sc, g = mod_f
    hf = modnorm_fwd(x1, wl["norm_ffn"], sc, sh, f"{tag}_ffn_norm")
    up = matmul(hf, wl["ffn_w_up"], "nn", BF16, f"{tag}_ffn_up")
    act = ffn_mid_fwd(up, wl["ffn_conv_w8"], wl["ffn_conv_b"], f"{tag}_ffn_mid")
    f = matmul(act, wl["ffn_w_down"], "nn", F32, f"{tag}_ffn_down")
    x2 = gres_fwd(x1, g, f, f"{tag}_ffn_res")
    return x2, (x1, hf, up, act, f)


def _ffn_bwd(tag, dx2, saved, mod_f, wl):
    x1, hf, up, act, f = saved
    sh, sc, g = mod_f
    df, dg = gres_bwd(dx2, g, f, f"{tag}_ffn_res_bwd")
    dact = matmul(df, wl["ffn_w_down"], "nt", BF16, f"{tag}_ffn_down_da")
    wl["on_grad"]("ffn_w_down", matmul(act, df, "tn", F32, f"{tag}_ffn_down_dw", pieces="row"))
    dgate, dval, dcw, dcb = ffn_mid_bwd(up, wl["ffn_conv_w8"], wl["ffn_conv_b"], dact, f"{tag}_ffn_mid_bwd")
    dup = jnp.concatenate([dgate, dval], axis=1)
    dhf = matmul(dup, wl["ffn_w_up"], "nt", F32, f"{tag}_ffn_up_da")
    wl["on_grad"]("ffn_w_up", matmul(hf, dup, "tn", F32, f"{tag}_ffn_up_dw", pieces="col"))
    dx1, dgain, dsc, dsh = modnorm_bwd(x1, wl["norm_ffn"], sc, dhf, dx2, f"{tag}_ffn_norm_bwd")
    grads = {"norm_ffn": dgain[0], "ffn_conv_w": dcw[:FFN_CONV], "ffn_conv_b": dcb[0]}
    return dx1, (dsh, dsc, dg), grads


def _even_fwd(tag, x, mod_m, wl):
    sh, sc, g = mod_m
    hm = modnorm_fwd(x, wl["norm_mix"], sc, sh, f"{tag}_mix_norm")
    proj = matmul(hm, wl["w_in"], "nn", F32, f"{tag}_ev_in")
    qkv = gdn_conv_fwd(proj, wl["gdn_conv_w8"], f"{tag}_gdn_conv")
    o_a, states = gdn_chunk_fwd(qkv, proj, wl["alog_row"], wl["dt_row"], wl["gdn_norm"], f"{tag}_gdn_chunk")
    o_b = pool_fwd(proj, wl["pool_w"], wl["pool_scale"], f"{tag}_pool")
    ocat = jnp.concatenate([o_a, o_b], axis=1)
    y = matmul(ocat, wl["w_out"], "nn", F32, f"{tag}_ev_out")
    x1 = gres_fwd(x, g, y, f"{tag}_mix_res")
    return x1, (x, hm, proj, qkv, states, ocat, y)


def _even_bwd(tag, dx1, saved, mod_m, wl):
    x, hm, proj, qkv, states, ocat, y = saved
    sh, sc, g = mod_m
    dy, dg = gres_bwd(dx1, g, y, f"{tag}_mix_res_bwd")
    docat = matmul(dy, wl["w_out"], "nt", F32, f"{tag}_ev_out_da")
    wl["on_grad"]("ev_w_out", matmul(ocat, dy, "tn", F32, f"{tag}_ev_out_dw", pieces="row"))
    dqkv, dz, dba, dalog, ddt, dgn = gdn_chunk_bwd(
        qkv, proj, wl["alog_row"], wl["dt_row"], wl["gdn_norm"], states, docat, f"{tag}_gdn_chunk_bwd")
    dxc, dconv = gdn_conv_bwd(proj, wl["gdn_conv_w8"], dqkv, f"{tag}_gdn_conv_bwd")
    dp, dpw, dps = pool_bwd(proj, wl["pool_w"], wl["pool_scale"], docat, f"{tag}_pool_bwd")
    dproj = jnp.concatenate([dxc, dz, dp, dba], axis=1)
    dhm = matmul(dproj, wl["w_in"], "nt", F32, f"{tag}_ev_in_da")
    wl["on_grad"]("ev_w_in", col_pieces(_even_restore(matmul(hm, dproj, "tn", F32, f"{tag}_ev_in_dw"))))
    dx, dgain, dsc, dsh = modnorm_bwd(x, wl["norm_mix"], sc, dhm, dx1, f"{tag}_mix_norm_bwd")
    grads = {"norm_mix": dgain[0], "gdn_conv_w": dconv[:GDN_CONV], "gdn_a_log": dalog[0, :GDN_H], "gdn_dt_bias": ddt[0, :GDN_H],
             "gdn_norm": dgn[0], "pool_w": dpw, "pool_scale": dps[0]}
    return dx, (dsh, dsc, dg), grads


def _odd_fwd(tag, x, mod_m, wl):
    sh, sc, g = mod_m
    hm = modnorm_fwd(x, wl["norm_mix"], sc, sh, f"{tag}_mix_norm")
    projs, qkvns, outs, lses = [], [], [], []
    for gi, d in enumerate(DIL):
        w_g = wl["w_in"][:, gi * GROUP_COLS:(gi + 1) * GROUP_COLS]
        proj = matmul(to_residue_major(hm, d), w_g, "nn", BF16, f"{tag}_od_in{gi}")
        qkvn = headnorm_fwd(proj, wl["qk_gain8"], f"{tag}_headnorm{gi}")
        o_g, l_g = att_fwd(qkvn, gi, f"{tag}_att{gi}")
        projs.append(proj)
        qkvns.append(qkvn)
        outs.append(to_token_order(o_g, d))
        lses.append(to_token_order(l_g, d))
    o, lse = att_merge(outs, lses, f"{tag}_att_merge")
    y = matmul(o, wl["w_out"], "nn", F32, f"{tag}_od_out")
    x1 = gres_fwd(x, g, y, f"{tag}_mix_res")
    return x1, (x, hm, projs, qkvns, o, lse, y)


def _odd_bwd(tag, dx1, saved, mod_m, wl):
    x, hm, projs, qkvns, o, lse, y = saved
    sh, sc, g = mod_m
    dy, dg = gres_bwd(dx1, g, y, f"{tag}_mix_res_bwd")
    do = matmul(dy, wl["w_out"], "nt", F32, f"{tag}_od_out_da")
    wl["on_grad"]("od_w_out", matmul(o, dy, "tn", F32, f"{tag}_od_out_dw", pieces="row"))
    delta = att_delta(do, o, f"{tag}_att_delta")
    dhm, dw_in, dgain_qk = None, [], None
    for gi, d in enumerate(DIL):
        w_g = wl["w_in"][:, gi * GROUP_COLS:(gi + 1) * GROUP_COLS]
        do_g, lse_g, dl_g = (to_residue_major(t, d) for t in (do, lse, delta))
        dq = att_bwd_q(qkvns[gi], do_g, lse_g, dl_g, gi, f"{tag}_att{gi}_dq")
        dk, dv = att_bwd_kv(qkvns[gi], do_g, lse_g, dl_g, gi, f"{tag}_att{gi}_dkv")
        dproj, dgain = headnorm_bwd(projs[gi], wl["qk_gain8"], dq, dk, dv, f"{tag}_headnorm{gi}_bwd")
        dhm_g = to_token_order(matmul(dproj, w_g, "nt", F32, f"{tag}_od_in{gi}_da"), d)
        dw_in.append(matmul(to_residue_major(hm, d), dproj, "tn", F32, f"{tag}_od_in{gi}_dw"))
        dhm = dhm_g if dhm is None else dhm + dhm_g
        dgain_qk = dgain if dgain_qk is None else dgain_qk + dgain
    wl["on_grad"]("od_w_in", col_pieces(jnp.concatenate(dw_in, axis=1)))
    dx, dgain, dsc, dsh = modnorm_bwd(x, wl["norm_mix"], sc, dhm, dx1, f"{tag}_mix_norm_bwd")
    grads = {"norm_mix": dgain[0], "att_q_norm": dgain_qk[0], "att_k_norm": dgain_qk[1]}
    return dx, (dsh, dsc, dg), grads


def col_pieces(dw):
    M, N = dw.shape
    return dw.reshape(2, M // 2, N_CHIPS, N // N_CHIPS).transpose(0, 2, 1, 3)


def _layer_weights(i, W, big, on_grad):
    e = i // 2
    wl = {"norm_mix": W["norm_mix"][i][None, :], "norm_ffn": W["norm_ffn"][i][None, :],
          "ffn_w_up": big("ffn_w_up", i), "ffn_w_down": big("ffn_w_down", i),
          "ffn_conv_w8": _pad_rows8(W["ffn_conv_w"][i]), "ffn_conv_b": W["ffn_conv_b"][i][None, :],
          "on_grad": lambda name, pieces: on_grad(name, i if name.startswith("ffn") else e, pieces)}
    if i % 2 == 0:
        wl.update({"w_in": _even_reorder(big("ev_w_in", e)), "w_out": big("ev_w_out", e),
                   "gdn_conv_w8": _pad_rows8(W["gdn_conv_w"][e]),
                   "alog_row": _lane_row(W["gdn_a_log"][e]), "dt_row": _lane_row(W["gdn_dt_bias"][e]),
                   "gdn_norm": W["gdn_norm"][e][None, :], "pool_w": W["pool_w"][e],
                   "pool_scale": W["pool_scale"][e][None, :]})
    else:
        wl.update({"w_in": big("od_w_in", e), "w_out": big("od_w_out", e),
                   "qk_gain8": _pad_rows8(jnp.stack([W["att_q_norm"][e], W["att_k_norm"][e]]))})
    return wl


def local_step(x, target, mod, W, big, on_grad):
    depth = mod.shape[0]
    row = lambda i, k: mod[i, k][None, :]
    saved, wls = [], []
    for i in range(depth):
        wl = _layer_weights(i, W, big, on_grad)
        mod_m = (row(i, 0), row(i, 1), row(i, 2))
        mod_f = (row(i, 3), row(i, 4), row(i, 5))
        fwd = _even_fwd if i % 2 == 0 else _odd_fwd
        x1, s_mix = fwd(f"l{i}", x, mod_m, wl)
        x, s_ffn = _ffn_fwd(f"l{i}", x1, mod_f, wl)
        saved.append((s_mix, s_ffn, mod_m, mod_f))
        wls.append(wl)
    sq, dx = loss_head(x, target, "loss_head")
    dmod, grads = [None] * depth, [None] * depth
    for i in reversed(range(depth)):
        s_mix, s_ffn, mod_m, mod_f = saved[i]
        dx, dmf, g_ffn = _ffn_bwd(f"l{i}", dx, s_ffn, mod_f, wls[i])
        bwd = _even_bwd if i % 2 == 0 else _odd_bwd
        dx, dmm, g_mix = bwd(f"l{i}", dx, s_mix, mod_m, wls[i])
        dmod[i] = jnp.concatenate([t for t in dmm + dmf], axis=0)
        grads[i] = {**g_mix, **g_ffn}
    return sq, dx, jnp.stack(dmod), grads


def _place():
    return lax.axis_index("x"), lax.axis_index("y"), lax.axis_index("c")


def _other_chips(mx, my):
    return [(1 - mx, my), (mx, 1 - my), (1 - mx, 1 - my)]


def _sems(n):
    return [DMA_SEM((n,)), DMA_SEM((n,))]


def _put(buf, block, index):
    return lax.dynamic_update_index_in_dim(buf, block, index, 0)


def gather8_ride(x, then):
    def parts(ins, outs, sems):
        (x_ref,), (out_ref,), (send_sems, recv_sems) = ins, outs, sems
        mx, my, mc = _place()
        me, sibling = (mx, my, mc), (mx, my, 1 - mc)
        chips = _other_chips(mx, my)

        def slot(px, py, pc):
            return out_ref.at[4 * px + 2 * py + pc]

        def copy(k, block, to, src=None):
            return pltpu.make_async_remote_copy(
                src_ref=slot(*block) if src is None else src, dst_ref=slot(*block),
                send_sem=send_sems.at[k], recv_sem=recv_sems.at[k], device_id=to, device_id_type=MESH)

        first = lambda: ([copy(0, me, sibling, src=x_ref)]
                         + [copy(1 + j, me, (*chip, mc), src=x_ref) for j, chip in enumerate(chips)])
        passed = lambda: [copy(4 + j, (*chip, mc), sibling) for j, chip in enumerate(chips)]
        landed = lambda: [copy(1 + j, (*chip, mc), me) for j, chip in enumerate(chips)]
        from_sibling = lambda: ([copy(0, sibling, me)]
                                + [copy(4 + j, (*chip, 1 - mc), me) for j, chip in enumerate(chips)])
        return first, passed, landed, from_sibling

    def start(ins, outs, sems):
        first, _, _, _ = parts(ins, outs, sems)
        for cp in first():
            cp.start()

    def mid(ins, outs, sems):
        _, passed, landed, _ = parts(ins, outs, sems)
        for cp, fwd in zip(landed(), passed()):
            cp.wait_recv()
            fwd.start()

    def finish(ins, outs, sems):
        first, passed, _, from_sibling = parts(ins, outs, sems)
        for cp in from_sibling():
            cp.wait_recv()
        for cp in first() + passed():
            cp.wait_send()

    def landed_all(outs):
        mx, my, mc = _place()
        then(_put(outs[0], x, 4 * mx + 2 * my + mc))

    return Ride([x], [jax.ShapeDtypeStruct((8,) + x.shape, x.dtype)], _sems(7), start, finish,
                landed_all, mid=mid, heavy=True)


def all_gather8(x):
    box = []
    waiting = list(_RIDES)
    _RIDES[:] = [gather8_ride(x, box.append)]
    flush_rides()
    _RIDES[:] = waiting + _RIDES
    return box[0]


def sibling_halves_ride(p, then):
    def copy(ins, outs, sems):
        (p_ref,), (got_ref,), (send_sems, recv_sems) = ins, outs, sems
        mx, my, mc = _place()
        return pltpu.make_async_remote_copy(src_ref=p_ref.at[1 - mc], dst_ref=got_ref, send_sem=send_sems.at[0],
                                            recv_sem=recv_sems.at[0], device_id=(mx, my, 1 - mc), device_id_type=MESH)

    def start(ins, outs, sems):
        copy(ins, outs, sems).start()

    def finish(ins, outs, sems):
        cp = copy(ins, outs, sems)
        cp.wait_send()
        cp.wait_recv()

    def landed(outs):
        then(lax.dynamic_index_in_dim(p, _place()[2], 0, keepdims=False), outs[0])

    return Ride([p], [jax.ShapeDtypeStruct(p.shape[1:], p.dtype)], _sems(1), start, finish, landed)


def sibling_pair_ride(r, then):
    def copy(ins, outs, sems):
        (r_ref,), (got_ref,), (send_sems, recv_sems) = ins, outs, sems
        mx, my, mc = _place()
        return pltpu.make_async_remote_copy(src_ref=r_ref, dst_ref=got_ref, send_sem=send_sems.at[0],
                                            recv_sem=recv_sems.at[0], device_id=(mx, my, 1 - mc), device_id_type=MESH)

    def start(ins, outs, sems):
        copy(ins, outs, sems).start()

    def finish(ins, outs, sems):
        cp = copy(ins, outs, sems)
        cp.wait_send()
        cp.wait_recv()

    def landed(outs):
        then(jnp.where(_place()[2] == 0, jnp.stack([r, outs[0]]), jnp.stack([outs[0], r])))

    return Ride([r], [jax.ShapeDtypeStruct(r.shape, r.dtype)], _sems(1), start, finish, landed)


def chip_scatter_ride(p, then):
    def parts(ins, outs, sems):
        (p_ref,), (out_ref,), (send_sems, recv_sems) = ins, outs, sems
        mx, my, mc = _place()
        mine = 2 * mx + my
        chips = _other_chips(mx, my)
        sends = [pltpu.make_async_remote_copy(
            src_ref=p_ref.at[2 * chip[0] + chip[1]], dst_ref=out_ref.at[mine], send_sem=send_sems.at[k],
            recv_sem=recv_sems.at[k], device_id=(*chip, mc), device_id_type=MESH) for k, chip in enumerate(chips)]
        recvs = lambda: [pltpu.make_async_remote_copy(
            src_ref=p_ref.at[mine], dst_ref=out_ref.at[2 * chip[0] + chip[1]], send_sem=send_sems.at[k],
            recv_sem=recv_sems.at[k], device_id=(*chip, mc), device_id_type=MESH) for k, chip in enumerate(chips)]
        return sends, recvs

    def start(ins, outs, sems):
        sends, _ = parts(ins, outs, sems)
        for cp in sends:
            cp.start()

    def finish(ins, outs, sems):
        sends, recvs = parts(ins, outs, sems)
        for cp in recvs():
            cp.wait_recv()
        for cp in sends:
            cp.wait_send()

    def landed(outs):
        mx, my, _ = _place()
        mine = 2 * mx + my
        then(_put(outs[0], lax.dynamic_index_in_dim(p, mine, 0, keepdims=False), mine))

    return Ride([p], [jax.ShapeDtypeStruct(p.shape, p.dtype)], _sems(3), start, finish, landed, heavy=True)


def _stream_rows(R, C):
    return _rows(R, max(16, (256 * 1536 // C) // 16 * 16))


def cast_bf16(w, name):
    R, C = w.shape
    tb = _stream_rows(R, C)

    def body(w_ref, o_ref):
        o_ref[...] = w_ref[...].astype(BF16)

    blk = pl.BlockSpec((tb, C), lambda i: (i, 0))
    return _pcall(body, name=name, grid=(R // tb,), in_specs=[blk], out_specs=blk,
                          out_shape=jax.ShapeDtypeStruct((R, C), BF16), compiler_params=_params(1))(w)


def sum_slots(g, name):
    n, R, C = g.shape
    tb = _stream_rows(R, C)

    def body(*refs):
        acc = refs[0][...].astype(F32)
        for r in refs[1:n]:
            acc = acc + r[...].astype(F32)
        refs[n][...] = acc

    specs = [pl.BlockSpec((None, tb, C), functools.partial(lambda k, i: (k, i, 0), k)) for k in range(n)]
    return _pcall(body, name=name, grid=(R // tb,), in_specs=specs,
                          out_specs=pl.BlockSpec((tb, C), lambda i: (i, 0)),
                          out_shape=jax.ShapeDtypeStruct((R, C), F32), compiler_params=_params(1))(*([g] * n))


def add_to_bf16(a, b, name):
    R, C = a.shape
    tb = _stream_rows(R, C)

    def body(a_ref, b_ref, o_ref):
        o_ref[...] = (a_ref[...] + b_ref[...]).astype(BF16)

    blk = pl.BlockSpec((tb, C), lambda i: (i, 0))
    return _pcall(body, name=name, grid=(R // tb,), in_specs=[blk, blk], out_specs=blk,
                          out_shape=jax.ShapeDtypeStruct((R, C), BF16), compiler_params=_params(1))(a, b)


def ada_fwd(c_all, ada_w, bias, name):
    n, D, Cs = ada_w.shape
    tn = _tile(Cs, 512)

    def body(c_ref, w_ref, b_ref, o_ref):
        o_ref[...] = _bdot(_silu(c_ref[...]), w_ref[...], NN) + b_ref[...]

    return _pcall(
        body, name=name, grid=(n, Cs // tn),
        in_specs=[pl.BlockSpec((8, D), lambda l, j: (0, 0)),
                  pl.BlockSpec((None, D, tn), lambda l, j: (l, 0, j)),
                  pl.BlockSpec((None, 1, tn), lambda l, j: (l, 0, j))],
        out_specs=pl.BlockSpec((None, 8, tn), lambda l, j: (l, 0, j)),
        out_shape=jax.ShapeDtypeStruct((n, 8, Cs), F32), compiler_params=_params(2))(c_all, ada_w, bias)


def ada_bwd(c16, dmod16, name):
    n, _, Cs = dmod16.shape
    D = c16.shape[1]
    tn = _tile(Cs, 512)

    def body(c_ref, d_ref, o_ref):
        o_ref[...] = _bdot(_silu(c_ref[...]), d_ref[...], TN)

    return _pcall(
        body, name=name, grid=(n, Cs // tn),
        in_specs=[pl.BlockSpec((16, D), lambda l, j: (0, 0)),
                  pl.BlockSpec((None, 16, tn), lambda l, j: (l, 0, j))],
        out_specs=pl.BlockSpec((None, D, tn), lambda l, j: (l, 0, j)),
        out_shape=jax.ShapeDtypeStruct((n, D, Cs), F32), compiler_params=_params(2))(c16, dmod16)


WEIGHTS = ["ada_w", "ada_b", "norm_mix", "norm_ffn", "ev_w_in", "ev_w_out", "gdn_conv_w", "gdn_a_log",
           "gdn_dt_bias", "gdn_norm", "pool_w", "pool_scale", "od_w_in", "od_w_out", "att_q_norm",
           "att_k_norm", "ffn_w_up", "ffn_conv_w", "ffn_conv_b", "ffn_w_down"]
COL_SHARDED = ("ev_w_in", "od_w_in", "ffn_w_up")
ROW_SHARDED = ("ev_w_out", "od_w_out", "ffn_w_down")


def _pack(parts):
    rows, offs = [], []
    at = 0
    for p in parts:
        flat = p.reshape(-1).astype(F32)
        n = -(-flat.shape[0] // LANE)
        rows.append(jnp.pad(flat, (0, n * LANE - flat.shape[0])).reshape(n, LANE))
        offs.append((at, n))
        at += n
    pad = -at % 16
    if pad:
        rows.append(jnp.zeros((pad, LANE), F32))
    return jnp.concatenate(rows, axis=0), offs


def _unpack(buf, off, shape):
    at, n = off
    lead = buf.shape[:-2]
    flat = buf[..., at:at + n, :].reshape(lead + (n * LANE,))
    return flat[..., :math.prod(shape)].reshape(lead + tuple(shape))


def submit_weight_gather(store, key, shard, col_sharded, mc):
    R, C = shard.shape
    half = lax.dynamic_index_in_dim(shard.reshape(2, R // 2, C), mc, 0, keepdims=False)

    def landed(g):
        g = g.reshape(N_CHIPS, R, C)
        store[key] = g.transpose(1, 0, 2).reshape(R, N_CHIPS * C) if col_sharded else g.reshape(N_CHIPS * R, C)

    submit_ride(gather8_ride(half, landed))


def submit_grad_reduce(store, key, pieces, col_sharded):
    _, _, R, C = pieces.shape
    tag = f"{key[0]}{key[1]}"

    def paired(out):
        store[key] = out.reshape(2 * R, C) if col_sharded else out.transpose(1, 0, 2).reshape(R, 2 * C)

    def scattered(got):
        submit_ride(sibling_pair_ride(sum_slots(got, f"gsum_{tag}"), paired))

    def swapped(keep, got):
        chip_sum = add_to_bf16(keep.reshape(N_CHIPS * R, C), got.reshape(N_CHIPS * R, C), f"gadd_{tag}")
        submit_ride(chip_scatter_ride(chip_sum.reshape(N_CHIPS, R, C), scattered))

    submit_ride(sibling_halves_ride(pieces, swapped))


def kernel(x, c, ada_w, ada_b, norm_mix, norm_ffn, ev_w_in, ev_w_out, gdn_conv_w, gdn_a_log, gdn_dt_bias, gdn_norm, pool_w, pool_scale, od_w_in, od_w_out, att_q_norm, att_k_norm, ffn_w_up, ffn_conv_w, ffn_conv_b, ffn_w_down, loss_target, m_ada_w, m_ada_b, m_norm_mix, m_norm_ffn, m_ev_w_in, m_ev_w_out, m_gdn_conv_w, m_gdn_a_log, m_gdn_dt_bias, m_gdn_norm, m_pool_w, m_pool_scale, m_od_w_in, m_od_w_out, m_att_q_norm, m_att_k_norm, m_ffn_w_up, m_ffn_conv_w, m_ffn_conv_b, m_ffn_w_down, v_ada_w, v_ada_b, v_norm_mix, v_norm_ffn, v_ev_w_in, v_ev_w_out, v_gdn_conv_w, v_gdn_a_log, v_gdn_dt_bias, v_gdn_norm, v_pool_w, v_pool_scale, v_od_w_in, v_od_w_out, v_att_q_norm, v_att_k_norm, v_ffn_w_up, v_ffn_conv_w, v_ffn_conv_b, v_ffn_w_down):
    local = dict(ada_w=ada_w, ada_b=ada_b, norm_mix=norm_mix, norm_ffn=norm_ffn, ev_w_in=ev_w_in,
                 ev_w_out=ev_w_out, gdn_conv_w=gdn_conv_w, gdn_a_log=gdn_a_log, gdn_dt_bias=gdn_dt_bias,
                 gdn_norm=gdn_norm, pool_w=pool_w, pool_scale=pool_scale, od_w_in=od_w_in, od_w_out=od_w_out,
                 att_q_norm=att_q_norm, att_k_norm=att_k_norm, ffn_w_up=ffn_w_up, ffn_conv_w=ffn_conv_w,
                 ffn_conv_b=ffn_conv_b, ffn_w_down=ffn_w_down)
    moments_m = dict(zip(WEIGHTS, (m_ada_w, m_ada_b, m_norm_mix, m_norm_ffn, m_ev_w_in, m_ev_w_out,
                                   m_gdn_conv_w, m_gdn_a_log, m_gdn_dt_bias, m_gdn_norm, m_pool_w, m_pool_scale,
                                   m_od_w_in, m_od_w_out, m_att_q_norm, m_att_k_norm, m_ffn_w_up, m_ffn_conv_w,
                                   m_ffn_conv_b, m_ffn_w_down)))
    moments_v = dict(zip(WEIGHTS, (v_ada_w, v_ada_b, v_norm_mix, v_norm_ffn, v_ev_w_in, v_ev_w_out,
                                   v_gdn_conv_w, v_gdn_a_log, v_gdn_dt_bias, v_gdn_norm, v_pool_w, v_pool_scale,
                                   v_od_w_in, v_od_w_out, v_att_q_norm, v_att_k_norm, v_ffn_w_up, v_ffn_conv_w,
                                   v_ffn_conv_b, v_ffn_w_down)))
    _RIDES.clear()
    _IDS[0] = 0
    mx, my, mc = _place()
    chip = 2 * mx + my
    T, D = x.shape[1], x.shape[2]
    depth = ada_w.shape[0]
    ada_cols = ada_w.shape[2]

    buf, offs = _pack([c, gdn_conv_w, ffn_conv_w])
    gathered = all_gather8(buf)
    c_all = _unpack(gathered, offs[0], (D,))
    by_chip = gathered[0::2]
    gdn_conv_full = jnp.concatenate(list(_unpack(by_chip, offs[1], gdn_conv_w.shape)), axis=-1)
    ffn_conv_full = jnp.concatenate(list(_unpack(by_chip, offs[2], ffn_conv_w.shape)), axis=-1)

    bias = lax.dynamic_slice_in_dim(ada_b, chip * ada_cols, ada_cols, axis=1)[:, None, :]
    mod_part = ada_fwd(c_all, ada_w, bias, "ada_fwd")
    mod_all = all_gather8(mod_part)[0::2]
    mod_all = mod_all.transpose(1, 2, 0, 3).reshape(depth, 8, N_CHIPS * ada_cols)
    mod = lax.dynamic_index_in_dim(mod_all, 4 * mx + 2 * my + mc, 1, keepdims=False).reshape(depth, 6, D)

    full_w, big_grad = {}, {}
    order = []
    for i in range(depth):
        mixer = ("ev_w_in", "ev_w_out") if i % 2 == 0 else ("od_w_in", "od_w_out")
        order += [(name, i // 2) for name in mixer] + [("ffn_w_up", i), ("ffn_w_down", i)]
    shards = {name: cast_bf16(local[name].reshape(-1, local[name].shape[-1]), f"cast_{name}")
              .reshape(local[name].shape) for name in COL_SHARDED + ROW_SHARDED}
    for name, e in order:
        submit_weight_gather(full_w, (name, e), shards[name][e], name in COL_SHARDED, mc)

    def big(name, e):
        flush_rides(until=lambda: (name, e) in full_w)
        return full_w[(name, e)]

    def on_grad(name, e, pieces):
        submit_grad_reduce(big_grad, (name, e), pieces, name in COL_SHARDED)

    W = dict(local)
    W["gdn_conv_w"], W["ffn_conv_w"] = gdn_conv_full, ffn_conv_full
    sq, dx, dmod, grads = local_step(x[0], loss_target[0], mod, W, big, on_grad)
    loss = lax.psum(0.5 * jnp.sum(sq) / D, ("x", "y", "c"))

    small = ["norm_mix", "norm_ffn", "gdn_conv_w", "gdn_a_log", "gdn_dt_bias", "gdn_norm", "pool_w",
             "pool_scale", "att_q_norm", "att_k_norm", "ffn_conv_w", "ffn_conv_b"]
    full = {name: jnp.stack([g[name] for g in grads if name in g]) for name in small}
    grad = {}
    buf, offs = _pack([dmod] + [full[name] for name in small])
    gathered = all_gather8(buf)
    summed = sum_slots(gathered, "sum_small_grads")
    grad["ada_b"] = _unpack(summed, offs[0], ada_b.shape)
    for k, name in enumerate(small):
        grad[name] = _unpack(summed, offs[1 + k], full[name].shape)
    for name, cols in (("gdn_conv_w", gdn_conv_w.shape[-1]), ("ffn_conv_w", ffn_conv_w.shape[-1])):
        grad[name] = lax.dynamic_slice_in_dim(grad[name], chip * cols, cols, axis=2)

    dmod_all = _unpack(gathered, offs[0], (depth, N_CHIPS * ada_cols))
    dmod_mine = lax.dynamic_slice_in_dim(dmod_all, chip * ada_cols, ada_cols, axis=2).transpose(1, 0, 2)
    grad["ada_w"] = ada_bwd(jnp.pad(c_all, ((0, 8), (0, 0))), jnp.pad(dmod_mine, ((0, 0), (0, 8), (0, 0))),
                            "ada_bwd")

    deltas, new_m, new_v = {}, {}, {}
    large = COL_SHARDED + ROW_SHARDED
    for name in [n for n in WEIGHTS if n not in large] + list(large):
        if name in large:
            keys = [k for k in order if k[0] == name]
            flush_rides(until=lambda: all(k in big_grad for k in keys))
            grad[name] = jnp.stack([big_grad[k] for k in keys])
        deltas[name], new_m[name], new_v[name] = adamw(local[name], grad[name], moments_m[name],
                                                       moments_v[name], f"adamw_{name}")
    flush_rides()
    return (loss, dx[None], *[grad[n] for n in WEIGHTS], *[deltas[n] for n in WEIGHTS],
            *[new_m[n] for n in WEIGHTS], *[new_v[n] for n in WEIGHTS])
```

```python
import functools
import math

import jax
import jax.numpy as jnp
from jax import lax
from jax.experimental import pallas as pl
from jax.experimental.pallas import tpu as pltpu

F32 = jnp.float32
BF16 = jnp.bfloat16
LANE = 128
SUBLANE = 8
VMEM_LIMIT = 56 * 1024 * 1024
MESH = pl.DeviceIdType.MESH
N_CHIPS = 4

RMS_EPS = 1e-6
GDN_H = 4
HD = 128
GDN_CHUNK = 64
GDN_STEP = 2
GDN_CONV = 4
FFN_CONV = 3
POOL_G = 4
ATT_H = 8
ATT_BLK = 128
DIL = (1, 4, 16)
EVEN_COLS = 2568
EVEN_PAD = 2688
ADAM_LR, ADAM_B1, ADAM_B2, ADAM_EPS, ADAM_WD, ADAM_STEP = 0.001, 0.9, 0.999, 1e-08, 0.01, 10
NEG = -1e30

NN = (((1,), (0,)), ((), ()))
NT = (((1,), (1,)), ((), ()))
TN = (((0,), (0,)), ((), ()))
BNN = (((2,), (1,)), ((0,), (0,)))
BNT = (((2,), (2,)), ((0,), (0,)))
BTN = (((1,), (1,)), ((0,), (0,)))


def _params(n_grid):
    return pltpu.CompilerParams(dimension_semantics=("arbitrary",) * n_grid,
                                vmem_limit_bytes=VMEM_LIMIT)


HBM = pl.BlockSpec(memory_space=pltpu.HBM)
DMA_SEM = pltpu.SemaphoreType.DMA


class Ride:
    def __init__(self, inputs, out_shapes, sems, start, finish, then, mid=None, heavy=False):
        self.inputs, self.out_shapes, self.sems = list(inputs), list(out_shapes), list(sems)
        self.start, self.mid, self.finish, self.then, self.heavy = start, mid, finish, then, heavy


_RIDES = []


def submit_ride(ride):
    _RIDES.append(ride)


def flush_rides(until=None):
    while _RIDES and not (until is not None and until()):
        ride = _RIDES.pop(0)

        def body(*refs, ride=ride):
            a, b = len(ride.inputs), len(ride.inputs) + len(ride.out_shapes)
            ride.start(refs[:a], refs[a:b], refs[b:])
            if ride.mid is not None:
                ride.mid(refs[:a], refs[a:b], refs[b:])
            ride.finish(refs[:a], refs[a:b], refs[b:])

        outs = pl.pallas_call(body, name=f"exchange{_next_id()}", in_specs=[HBM] * len(ride.inputs),
                              out_specs=[HBM] * len(ride.out_shapes), out_shape=ride.out_shapes,
                              scratch_shapes=ride.sems)(*ride.inputs)
        ride.then(list(outs))


_IDS = [0]


def _next_id():
    _IDS[0] += 1
    return _IDS[0]


def _pcall(body, *, name, grid, in_specs, out_specs, out_shape, scratch_shapes=(), compiler_params=None,
           long_call=False):
    del compiler_params
    single = not isinstance(out_shape, (list, tuple))
    outs = [out_shape] if single else list(out_shape)
    ospecs = [out_specs] if single else list(out_specs)
    total = math.prod(grid)
    fits = [k for k, r in enumerate(_RIDES) if long_call or not r.heavy] if total > 1 else []
    ride = _RIDES.pop(fits[0]) if fits else None
    if ride is None:
        call = pl.pallas_call(body, name=name, grid=grid, in_specs=list(in_specs), out_specs=ospecs,
                              out_shape=outs, scratch_shapes=list(scratch_shapes),
                              compiler_params=_params(len(grid)))

        def run_plain(*args):
            res = call(*args)
            return res[0] if single else res
        return run_plain

    n_in, n_out, n_scr = len(in_specs), len(outs), len(scratch_shapes)
    r_in, r_out = len(ride.inputs), len(ride.out_shapes)

    def carrying_body(*refs):
        at = 0
        ins = refs[at:at + n_in]; at += n_in
        r_ins = refs[at:at + r_in]; at += r_in
        os_ = refs[at:at + n_out]; at += n_out
        r_outs = refs[at:at + r_out]; at += r_out
        scr = refs[at:at + n_scr]; at += n_scr
        r_sems = refs[at:]
        step = pl.program_id(0)
        for ax in range(1, len(grid)):
            step = step * grid[ax] + pl.program_id(ax)

        @pl.when(step == 0)
        def _():
            ride.start(r_ins, r_outs, r_sems)

        body(*ins, *os_, *scr)

        if ride.mid is not None:
            @pl.when(step == total // 2)
            def _():
                ride.mid(r_ins, r_outs, r_sems)

        @pl.when(step == total - 1)
        def _():
            ride.finish(r_ins, r_outs, r_sems)

    call = pl.pallas_call(
        carrying_body, name=name, grid=grid, in_specs=list(in_specs) + [HBM] * r_in,
        out_specs=ospecs + [HBM] * r_out, out_shape=outs + ride.out_shapes,
        scratch_shapes=list(scratch_shapes) + ride.sems, compiler_params=_params(len(grid)))

    def run_carrying(*args):
        res = call(*args, *ride.inputs)
        ride.then(list(res[n_out:]))
        return res[0] if single else list(res[:n_out])
    return run_carrying


def _tile(n, target):
    if n <= target:
        return n
    best = None
    for t in range(LANE, target + 1, LANE):
        if n % t == 0:
            best = t
    assert best is not None, (n, target)
    return best


def _rows(n, target):
    if n <= target:
        return n
    best = None
    for t in range(16, target + 1, 16):
        if n % t == 0:
            best = t
    assert best is not None, (n, target)
    return best


def _bdot(a, b, dims):
    return lax.dot_general(a.astype(BF16), b.astype(BF16), dims, preferred_element_type=F32)


def _split(a):
    hi = a.astype(BF16)
    return hi, (a - hi.astype(F32)).astype(BF16)


def _dot3(a, b, dims):
    ah, al = _split(a)
    bh, bl = _split(b)
    d = lambda p, q: lax.dot_general(p, q, dims, preferred_element_type=F32)
    return d(ah, bh) + d(ah, bl) + d(al, bh)


def _sigmoid(x):
    return 1.0 / (1.0 + jnp.exp(-x))


def _silu(x):
    return x * _sigmoid(x)


def matmul(a, b, mode, out_dtype, name, tm=1024, tn=1536, tk=1536, pieces=None):
    if mode == "nn":
        (M, K), (K2, N) = a.shape, b.shape
    elif mode == "nt":
        (M, K), (N, K2) = a.shape, b.shape
    else:
        (K, M), (K2, N) = a.shape, b.shape
    assert K == K2, (a.shape, b.shape, mode)
    if pieces == "col":
        tm, tn = _tile(M // 2, tm), _tile(N // N_CHIPS, tn)
    elif pieces == "row":
        quarter = M // N_CHIPS
        tm = 2 * quarter if (2 * quarter) % LANE == 0 else M
        tn = _tile(N // 2, tn)
    else:
        tm, tn = _tile(M, tm), _tile(N, tn)
    tk = _tile(K, tk)
    nk = K // tk
    dims = {"nn": NN, "nt": NT, "tn": TN}[mode]
    if mode == "tn":
        a_spec = pl.BlockSpec((tk, tm), lambda i, j, k: (k, i))
    else:
        a_spec = pl.BlockSpec((tm, tk), lambda i, j, k: (i, k))
    if mode == "nt":
        b_spec = pl.BlockSpec((tn, tk), lambda i, j, k: (j, k))
    else:
        b_spec = pl.BlockSpec((tk, tn), lambda i, j, k: (k, j))

    out_spec = pl.BlockSpec((tm, tn), lambda i, j, k: (i, j))
    out_shape = jax.ShapeDtypeStruct((M, N), out_dtype)
    rows_per_slot = tm
    if pieces == "col":
        nih, njc = (M // 2) // tm, (N // N_CHIPS) // tn
        out_spec = pl.BlockSpec((None, None, tm, tn), lambda i, j, k: (i // nih, j // njc, i % nih, j % njc))
        out_shape = jax.ShapeDtypeStruct((2, N_CHIPS, M // 2, N // N_CHIPS), out_dtype)
    elif pieces == "row":
        rows_per_slot = M // N_CHIPS
        njh = (N // 2) // tn
        out_spec = pl.BlockSpec((None, tm // rows_per_slot, rows_per_slot, tn),
                                lambda i, j, k: (j // njh, i, 0, j % njh))
        out_shape = jax.ShapeDtypeStruct((2, N_CHIPS, rows_per_slot, N // 2), out_dtype)

    def store(o_ref, acc):
        if pieces == "row":
            for s in range(tm // rows_per_slot):
                o_ref[s] = acc[s * rows_per_slot:(s + 1) * rows_per_slot, :].astype(out_dtype)
        else:
            o_ref[...] = acc.astype(out_dtype)

    def body(a_ref, b_ref, o_ref, acc_ref):
        k = pl.program_id(2)
        p = _bdot(a_ref[...], b_ref[...], dims)
        if nk == 1:
            store(o_ref, p)
            return

        @pl.when(k == 0)
        def _():
            acc_ref[...] = p

        @pl.when(k > 0)
        def _():
            acc_ref[...] += p

        @pl.when(k == nk - 1)
        def _():
            store(o_ref, acc_ref[...])

    return _pcall(
        body, name=name, grid=(M // tm, N // tn, nk),
        in_specs=[a_spec, b_spec], out_specs=out_spec, out_shape=out_shape,
        scratch_shapes=[pltpu.VMEM((tm, tn), F32)],
        long_call=True)(a, b)


def _row_spec(d):
    return pl.BlockSpec((1, d), lambda i: (0, 0))


def modnorm_fwd(x, gain, sc, sh, name):
    T, D = x.shape
    tb = _rows(T, 512)

    def body(x_ref, g_ref, sc_ref, sh_ref, o_ref):
        xv = x_ref[...]
        r = lax.rsqrt(jnp.mean(xv * xv, axis=-1, keepdims=True) + RMS_EPS)
        o_ref[...] = ((xv * r) * g_ref[...] * (1.0 + sc_ref[...]) + sh_ref[...]).astype(BF16)

    blk = pl.BlockSpec((tb, D), lambda i: (i, 0))
    return _pcall(
        body, name=name, grid=(T // tb,),
        in_specs=[blk, _row_spec(D), _row_spec(D), _row_spec(D)],
        out_specs=blk, out_shape=jax.ShapeDtypeStruct((T, D), BF16),
        compiler_params=_params(1))(x, gain, sc, sh)


def modnorm_bwd(x, gain, sc, dh, dres, name):
    T, D = x.shape
    tb = _rows(T, 512)

    def body(x_ref, g_ref, sc_ref, dh_ref, dres_ref, dx_ref, dg_ref, dsc_ref, dsh_ref):
        i = pl.program_id(0)
        xv = x_ref[...]
        r = lax.rsqrt(jnp.mean(xv * xv, axis=-1, keepdims=True) + RMS_EPS)
        n = xv * r
        dhv = dh_ref[...].astype(F32)
        gain_v, sc1 = g_ref[...], 1.0 + sc_ref[...]
        dn = dhv * (gain_v * sc1)
        dx_ref[...] = r * (dn - n * jnp.mean(dn * n, axis=-1, keepdims=True)) + dres_ref[...]
        dhn = dhv * n

        @pl.when(i == 0)
        def _():
            dg_ref[...] = jnp.zeros_like(dg_ref)
            dsc_ref[...] = jnp.zeros_like(dsc_ref)
            dsh_ref[...] = jnp.zeros_like(dsh_ref)

        dg_ref[...] += jnp.sum(dhn * sc1, axis=0, keepdims=True)
        dsc_ref[...] += jnp.sum(dhn * gain_v, axis=0, keepdims=True)
        dsh_ref[...] += jnp.sum(dhv, axis=0, keepdims=True)

    blk = pl.BlockSpec((tb, D), lambda i: (i, 0))
    row = jax.ShapeDtypeStruct((1, D), F32)
    return _pcall(
        body, name=name, grid=(T // tb,),
        in_specs=[blk, _row_spec(D), _row_spec(D), blk, blk],
        out_specs=[blk, _row_spec(D), _row_spec(D), _row_spec(D)],
        out_shape=[jax.ShapeDtypeStruct((T, D), F32), row, row, row],
        compiler_params=_params(1))(x, gain, sc, dh, dres)


def gres_fwd(x, g, y, name):
    T, D = x.shape
    tb = _rows(T, 512)

    def body(x_ref, g_ref, y_ref, o_ref):
        o_ref[...] = x_ref[...] + g_ref[...] * y_ref[...]

    blk = pl.BlockSpec((tb, D), lambda i: (i, 0))
    return _pcall(
        body, name=name, grid=(T // tb,), in_specs=[blk, _row_spec(D), blk], out_specs=blk,
        out_shape=jax.ShapeDtypeStruct((T, D), F32), compiler_params=_params(1))(x, g, y)


def gres_bwd(dx, g, y, name):
    T, D = dx.shape
    tb = _rows(T, 512)

    def body(dx_ref, g_ref, y_ref, dy_ref, dg_ref):
        i = pl.program_id(0)
        dxv = dx_ref[...]
        dy_ref[...] = (dxv * g_ref[...]).astype(BF16)

        @pl.when(i == 0)
        def _():
            dg_ref[...] = jnp.zeros_like(dg_ref)

        dg_ref[...] += jnp.sum(dxv * y_ref[...], axis=0, keepdims=True)

    blk = pl.BlockSpec((tb, D), lambda i: (i, 0))
    return _pcall(
        body, name=name, grid=(T // tb,), in_specs=[blk, _row_spec(D), blk],
        out_specs=[blk, _row_spec(D)],
        out_shape=[jax.ShapeDtypeStruct((T, D), BF16), jax.ShapeDtypeStruct((1, D), F32)],
        compiler_params=_params(1))(dx, g, y)


def loss_head(y, target, name):
    T, D = y.shape
    tb = _rows(T, 512)

    def body(y_ref, t_ref, l_ref, dy_ref):
        i = pl.program_id(0)
        err = y_ref[...] - t_ref[...]
        dy_ref[...] = err * (1.0 / D)

        @pl.when(i == 0)
        def _():
            l_ref[...] = jnp.zeros_like(l_ref)

        sq = jnp.sum(err * err, axis=0, keepdims=True)
        tot = sq[:, 0:LANE]
        for k in range(1, D // LANE):
            tot = tot + sq[:, k * LANE:(k + 1) * LANE]
        l_ref[...] += tot

    blk = pl.BlockSpec((tb, D), lambda i: (i, 0))
    return _pcall(
        body, name=name, grid=(T // tb,), in_specs=[blk, blk],
        out_specs=[_row_spec(LANE), blk],
        out_shape=[jax.ShapeDtypeStruct((1, LANE), F32), jax.ShapeDtypeStruct((T, D), F32)],
        compiler_params=_params(1))(y, target)


def _back(ext, s):
    return ext if s == 0 else pltpu.roll(ext, s, 0)


def _ahead(ext, s):
    return ext if s == 0 else pltpu.roll(ext, ext.shape[0] - s, 0)


def _halo_prev(tb, h):
    return lambda i, j: (jnp.maximum(i * (tb // h) - 1, 0), j)


def _halo_next(tb, h, nrb):
    return lambda i, j: (jnp.minimum(i + 1, nrb - 1) * (tb // h), j)


FFN_TB, FFN_CB = 256, 1408
HALO16 = 16


def ffn_mid_fwd(up, conv_w8, conv_b, name):
    T, F2 = up.shape
    Fd = F2 // 2
    tb, cb = _rows(T, FFN_TB), _tile(Fd, FFN_CB)
    ncb = Fd // cb
    H = HALO16

    def body(g_ref, gp_ref, v_ref, w_ref, b_ref, o_ref):
        i = pl.program_id(0)
        g = g_ref[...].astype(F32)
        prev = jnp.where(i > 0, gp_ref[...].astype(F32), 0.0)
        ext = jnp.concatenate([prev, g], axis=0)
        w = w_ref[...]
        gc = w[2:3] * g + w[1:2] * _back(ext, 1)[H:] + w[0:1] * _back(ext, 2)[H:] + b_ref[...]
        o_ref[...] = (_silu(gc) * v_ref[...].astype(F32)).astype(BF16)

    return _pcall(
        body, name=name, grid=(T // tb, ncb),
        in_specs=[pl.BlockSpec((tb, cb), lambda i, j: (i, j)),
                  pl.BlockSpec((H, cb), _halo_prev(tb, H)),
                  pl.BlockSpec((tb, cb), lambda i, j: (i, j + ncb)),
                  pl.BlockSpec((SUBLANE, cb), lambda i, j: (0, j)),
                  pl.BlockSpec((1, cb), lambda i, j: (0, j))],
        out_specs=pl.BlockSpec((tb, cb), lambda i, j: (i, j)),
        out_shape=jax.ShapeDtypeStruct((T, Fd), BF16),
        long_call=True)(up, up, up, conv_w8, conv_b)


def ffn_mid_bwd(up, conv_w8, conv_b, dact, name):
    T, F2 = up.shape
    Fd = F2 // 2
    tb, cb = _rows(T, FFN_TB), _tile(Fd, FFN_CB)
    ncb, nrb = Fd // cb, T // tb
    H = HALO16

    def body(g_ref, gp_ref, gn_ref, v_ref, vn_ref, d_ref, dn_ref, w_ref, b_ref,
             dg_ref, dv_ref, dw_ref, db_ref):
        i = pl.program_id(1)
        g = g_ref[...].astype(F32)
        prev = jnp.where(i > 0, gp_ref[...].astype(F32), 0.0)
        ext = jnp.concatenate([prev, g, gn_ref[...].astype(F32)], axis=0)
        w = w_ref[...]
        e1, e2 = _back(ext, 1), _back(ext, 2)
        gc = (w[2:3] * ext + w[1:2] * e1 + w[0:1] * e2 + b_ref[...])[H:]
        val = jnp.concatenate([v_ref[...], vn_ref[...]], axis=0).astype(F32)
        dnext = jnp.where(i < nrb - 1, dn_ref[...].astype(F32), 0.0)
        da = jnp.concatenate([d_ref[...].astype(F32), dnext], axis=0)
        sg = _sigmoid(gc)
        dv_ref[...] = (da * gc * sg)[:tb].astype(BF16)
        dgc = da * val * (sg * (1.0 + gc * (1.0 - sg)))
        dg_ref[...] = (w[2:3] * dgc + w[1:2] * _ahead(dgc, 1) + w[0:1] * _ahead(dgc, 2))[:tb].astype(BF16)
        dc = dgc[:tb]

        @pl.when(i == 0)
        def _():
            dw_ref[...] = jnp.zeros_like(dw_ref)
            db_ref[...] = jnp.zeros_like(db_ref)

        dw_ref[2:3, :] += jnp.sum(dc * g, axis=0, keepdims=True)
        dw_ref[1:2, :] += jnp.sum(dc * e1[H:H + tb], axis=0, keepdims=True)
        dw_ref[0:1, :] += jnp.sum(dc * e2[H:H + tb], axis=0, keepdims=True)
        db_ref[...] += jnp.sum(dc, axis=0, keepdims=True)

    cur = lambda j, i: (i, j)
    prv = lambda j, i: _halo_prev(tb, H)(i, j)
    nxt = lambda j, i: _halo_next(tb, H, nrb)(i, j)
    return _pcall(
        body, name=name, grid=(ncb, nrb),
        in_specs=[pl.BlockSpec((tb, cb), cur), pl.BlockSpec((H, cb), prv), pl.BlockSpec((H, cb), nxt),
                  pl.BlockSpec((tb, cb), lambda j, i: (i, j + ncb)),
                  pl.BlockSpec((H, cb), lambda j, i: (jnp.minimum(i + 1, nrb - 1) * (tb // H), j + ncb)),
                  pl.BlockSpec((tb, cb), cur), pl.BlockSpec((H, cb), nxt),
                  pl.BlockSpec((SUBLANE, cb), lambda j, i: (0, j)),
                  pl.BlockSpec((1, cb), lambda j, i: (0, j))],
        out_specs=[pl.BlockSpec((tb, cb), cur), pl.BlockSpec((tb, cb), cur),
                   pl.BlockSpec((SUBLANE, cb), lambda j, i: (0, j)),
                   pl.BlockSpec((1, cb), lambda j, i: (0, j))],
        out_shape=[jax.ShapeDtypeStruct((T, Fd), BF16), jax.ShapeDtypeStruct((T, Fd), BF16),
                   jax.ShapeDtypeStruct((SUBLANE, Fd), F32), jax.ShapeDtypeStruct((1, Fd), F32)],
        long_call=True)(up, up, up, up, up, dact, dact, conv_w8, conv_b)


GDN_W = GDN_H * HD


def _head_l2norm(a, apply):
    parts = []
    for h in range(GDN_H):
        ah = a[:, h * HD:(h + 1) * HD]
        parts.append(ah * lax.rsqrt(jnp.sum(ah * ah, axis=-1, keepdims=True) + RMS_EPS))
    return jnp.where(apply, jnp.concatenate(parts, axis=1), a)


def _head_l2norm_bwd(a, dy, apply):
    parts = []
    for h in range(GDN_H):
        sl = slice(h * HD, (h + 1) * HD)
        ah, dh = a[:, sl], dy[:, sl]
        r = lax.rsqrt(jnp.sum(ah * ah, axis=-1, keepdims=True) + RMS_EPS)
        y = ah * r
        parts.append(r * (dh - y * jnp.sum(dh * y, axis=-1, keepdims=True)))
    return jnp.where(apply, jnp.concatenate(parts, axis=1), dy)


def gdn_conv_fwd(proj, w8, name):
    T = proj.shape[0]
    tb = _rows(T, 512)
    H = SUBLANE

    def body(x_ref, xp_ref, w_ref, o_ref):
        i, j = pl.program_id(0), pl.program_id(1)
        x = x_ref[...]
        prev = jnp.where(i > 0, xp_ref[...], 0.0)
        ext = jnp.concatenate([prev, x], axis=0)
        w = w_ref[...]
        c = (w[3:4] * x + w[2:3] * _back(ext, 1)[H:] + w[1:2] * _back(ext, 2)[H:]
             + w[0:1] * _back(ext, 3)[H:])
        o_ref[...] = _head_l2norm(_silu(c), j < 2)

    return _pcall(
        body, name=name, grid=(T // tb, 3),
        in_specs=[pl.BlockSpec((tb, GDN_W), lambda i, j: (i, j)),
                  pl.BlockSpec((H, GDN_W), _halo_prev(tb, H)),
                  pl.BlockSpec((SUBLANE, GDN_W), lambda i, j: (0, j))],
        out_specs=pl.BlockSpec((tb, GDN_W), lambda i, j: (i, j)),
        out_shape=jax.ShapeDtypeStruct((T, 3 * GDN_W), F32),
        compiler_params=_params(2))(proj, proj, w8)


def gdn_conv_bwd(proj, w8, dout, name):
    T = proj.shape[0]
    tb = _rows(T, 512)
    nrb = T // tb
    H = SUBLANE

    def body(x_ref, xp_ref, xn_ref, d_ref, dn_ref, w_ref, dx_ref, dw_ref):
        j, i = pl.program_id(0), pl.program_id(1)
        x = x_ref[...]
        prev = jnp.where(i > 0, xp_ref[...], 0.0)
        ext = jnp.concatenate([prev, x, xn_ref[...]], axis=0)
        w = w_ref[...]
        e1, e2, e3 = _back(ext, 1), _back(ext, 2), _back(ext, 3)
        c = (w[3:4] * ext + w[2:3] * e1 + w[1:2] * e2 + w[0:1] * e3)[H:]
        sg = _sigmoid(c)
        dnext = jnp.where(i < nrb - 1, dn_ref[...], 0.0)
        do = jnp.concatenate([d_ref[...], dnext], axis=0)
        da = _head_l2norm_bwd(c * sg, do, j < 2)
        dc = da * (sg * (1.0 + c * (1.0 - sg)))
        dx_ref[...] = (w[3:4] * dc + w[2:3] * _ahead(dc, 1) + w[1:2] * _ahead(dc, 2)
                       + w[0:1] * _ahead(dc, 3))[:tb].astype(BF16)
        dcc = dc[:tb]

        @pl.when(i == 0)
        def _():
            dw_ref[...] = jnp.zeros_like(dw_ref)

        dw_ref[3:4, :] += jnp.sum(dcc * x, axis=0, keepdims=True)
        dw_ref[2:3, :] += jnp.sum(dcc * e1[H:H + tb], axis=0, keepdims=True)
        dw_ref[1:2, :] += jnp.sum(dcc * e2[H:H + tb], axis=0, keepdims=True)
        dw_ref[0:1, :] += jnp.sum(dcc * e3[H:H + tb], axis=0, keepdims=True)

    cur = lambda j, i: (i, j)
    prv = lambda j, i: _halo_prev(tb, H)(i, j)
    nxt = lambda j, i: _halo_next(tb, H, nrb)(i, j)
    return _pcall(
        body, name=name, grid=(3, nrb),
        in_specs=[pl.BlockSpec((tb, GDN_W), cur), pl.BlockSpec((H, GDN_W), prv), pl.BlockSpec((H, GDN_W), nxt),
                  pl.BlockSpec((tb, GDN_W), cur), pl.BlockSpec((H, GDN_W), nxt),
                  pl.BlockSpec((SUBLANE, GDN_W), lambda j, i: (0, j))],
        out_specs=[pl.BlockSpec((tb, GDN_W), cur), pl.BlockSpec((SUBLANE, GDN_W), lambda j, i: (0, j))],
        out_shape=[jax.ShapeDtypeStruct((T, 3 * GDN_W), BF16),
                   jax.ShapeDtypeStruct((SUBLANE, 3 * GDN_W), F32)],
        compiler_params=_params(2))(proj, proj, proj, dout, dout, w8)


def _dot_family(dot, diff):
    if not diff:
        return tuple(functools.partial(lambda d, a, b: dot(a, b, d), d) for d in (BNN, BNT, BTN))

    @jax.custom_vjp
    def nn(a, b):
        return dot(a, b, BNN)
    nn.defvjp(lambda a, b: (dot(a, b, BNN), (a, b)),
              lambda res, g: (dot(g, res[1], BNT), dot(res[0], g, BTN)))

    @jax.custom_vjp
    def nt(a, b):
        return dot(a, b, BNT)
    nt.defvjp(lambda a, b: (dot(a, b, BNT), (a, b)),
              lambda res, g: (dot(g, res[1], BNN), dot(g, res[0], BTN)))

    @jax.custom_vjp
    def tn(a, b):
        return dot(a, b, BTN)
    tn.defvjp(lambda a, b: (dot(a, b, BTN), (a, b)),
              lambda res, g: (dot(res[1], g, BNT), dot(res[0], g, BNN)))
    return nn, nt, tn


def _saved_inverse(hdots):
    _, hnt, htn = hdots

    @jax.custom_vjp
    def inv(L, P):
        return P

    inv.defvjp(lambda L, P: (P, P), lambda P, g: (-hnt(htn(P, g), P), jnp.zeros_like(P)))
    return inv


def _gdn_step(dots, hdots, S, q, k, v, z, b_raw, a_raw, alog, dtb, gnorm, P_saved=None, return_P=False):
    nn, nt, tn = dots
    hnn = hdots[0]
    B, C = q.shape[0], GDN_CHUNK
    ii = lax.broadcasted_iota(jnp.int32, (B, C, C), 1)
    jj = lax.broadcasted_iota(jnp.int32, (B, C, C), 2)
    causal, strict = ii >= jj, ii > jj
    tri, tri_t = causal.astype(F32), (ii <= jj).astype(F32)
    eye, ones = (ii == jj).astype(F32), jnp.ones((B, C, C), F32)

    beta = _sigmoid(b_raw)
    xs = a_raw + dtb
    pos = xs > 0.0
    softplus = jnp.where(pos, xs, 0.0) + jnp.log(1.0 + jnp.exp(jnp.where(pos, -xs, xs)))
    g = -jnp.exp(alog) * softplus
    gb = jnp.broadcast_to(g, (B, C, C))
    gc_c = hnn(tri, gb)
    gc_r = hnn(hnn(ones, eye * gb), tri_t)
    gc = hnn(tri, jnp.broadcast_to(g, (B, C, HD)))
    gl = jnp.sum(g, axis=1, keepdims=True)
    decay = jnp.where(causal, jnp.exp(jnp.where(causal, gc_c - gc_r, 0.0)), 0.0)
    q = q * (HD ** -0.5)
    kb = k * beta
    L = jnp.where(strict, nt(kb, k) * decay, 0.0)
    egc = jnp.exp(gc)
    if P_saved is None:
        P = eye - L
        M = hnn(L, L)
        for step in range(5):
            P = P + hnn(P, M)
            if step < 4:
                M = hnn(M, M)
    else:
        P = _saved_inverse(hdots)(L, P_saved)
    u = hnn(P, v * beta)
    w = hnn(P, kb * egc)
    intra = jnp.where(causal, nt(q, k) * decay, 0.0)
    qg = q * egc
    kdec = k * jnp.exp(gl - gc)
    egl = jnp.exp(gl)
    outs = []
    for ci in range(B // GDN_H):
        sl = slice(ci * GDN_H, (ci + 1) * GDN_H)
        v_new = u[sl] - nn(w[sl], S)
        outs.append(nn(qg[sl], S) + nn(intra[sl], v_new))
        S = S * egl[sl] + tn(kdec[sl], v_new)
    o = jnp.concatenate(outs, axis=0)
    r = lax.rsqrt(jnp.mean(o * o, axis=-1, keepdims=True) + RMS_EPS)
    out = o * r * gnorm * _silu(z)
    return (out, S, P) if return_P else (out, S)


def _gdn_batches(qkv, ba, z, alog_row, dt_row):
    C = GDN_CHUNK
    q, k, v, zz, b_raw, a_raw, alog, dtb = ([] for _ in range(8))
    for ci in range(GDN_STEP):
        rows = slice(ci * C, (ci + 1) * C)
        for h in range(GDN_H):
            q.append(qkv[rows, h * HD:(h + 1) * HD])
            k.append(qkv[rows, GDN_W + h * HD:GDN_W + (h + 1) * HD])
            v.append(qkv[rows, 2 * GDN_W + h * HD:2 * GDN_W + (h + 1) * HD])
            zz.append(z[rows, h * HD:(h + 1) * HD])
            b_raw.append(ba[rows, h:h + 1])
            a_raw.append(ba[rows, GDN_H + h:GDN_H + h + 1])
            alog.append(alog_row[:, h:h + 1])
            dtb.append(dt_row[:, h:h + 1])
    return tuple(jnp.stack(t) for t in (q, k, v, zz, b_raw, a_raw, alog, dtb))


def gdn_chunk_fwd(qkv, proj, alog_row, dt_row, gnorm, name):
    T = qkv.shape[0]
    R = GDN_CHUNK * GDN_STEP
    N = T // R
    B = GDN_STEP * GDN_H
    dots, hdots = _dot_family(_bdot, False), _dot_family(_dot3, False)

    def body(qkv_ref, ba_ref, z_ref, al_ref, dt_ref, gn_ref, o_ref, save_ref, inv_ref, S_ref):
        n = pl.program_id(0)

        @pl.when(n == 0)
        def _():
            S_ref[...] = jnp.zeros_like(S_ref)

        S = S_ref[...]
        save_ref[0] = S
        batches = _gdn_batches(qkv_ref[...], ba_ref[...], z_ref[...], al_ref[...], dt_ref[...])
        o, S_new, P = _gdn_step(dots, hdots, S, *batches, gn_ref[...], return_P=True)
        S_ref[...] = S_new
        inv_ref[0] = P
        for ci in range(GDN_STEP):
            for h in range(GDN_H):
                o_ref[ci * GDN_CHUNK:(ci + 1) * GDN_CHUNK, h * HD:(h + 1) * HD] = o[ci * GDN_H + h].astype(BF16)

    return _pcall(
        body, name=name, grid=(N,),
        in_specs=[pl.BlockSpec((R, 3 * GDN_W), lambda n: (n, 0)),
                  pl.BlockSpec((R, LANE), lambda n: (n, (4 * GDN_W + POOL_G * HD) // LANE)),
                  pl.BlockSpec((R, GDN_W), lambda n: (n, 3)),
                  _row_spec(LANE), _row_spec(LANE), _row_spec(HD)],
        out_specs=[pl.BlockSpec((R, GDN_W), lambda n: (n, 0)),
                   pl.BlockSpec((1, GDN_H, HD, HD), lambda n: (n, 0, 0, 0)),
                   pl.BlockSpec((1, B, GDN_CHUNK, GDN_CHUNK), lambda n: (n, 0, 0, 0))],
        out_shape=[jax.ShapeDtypeStruct((T, GDN_W), BF16), jax.ShapeDtypeStruct((N, GDN_H, HD, HD), F32),
                   jax.ShapeDtypeStruct((N, B, GDN_CHUNK, GDN_CHUNK), F32)],
        scratch_shapes=[pltpu.VMEM((GDN_H, HD, HD), F32)],
        long_call=True)(qkv, proj, proj, alog_row, dt_row, gnorm)


def gdn_chunk_bwd(qkv, proj, alog_row, dt_row, gnorm, saved, inverses, docat, name):
    T = qkv.shape[0]
    C = GDN_CHUNK
    R = C * GDN_STEP
    N = T // R
    B = GDN_STEP * GDN_H
    dots, hdots = _dot_family(_bdot, True), _dot_family(_dot3, True)

    def body(qkv_ref, ba_ref, z_ref, al_ref, dt_ref, gn_ref, save_ref, inv_ref, do_ref,
             dqkv_ref, dz_ref, dba_ref, dal_ref, ddt_ref, dgn_ref, dS_ref):
        n = pl.program_id(0)

        @pl.when(n == 0)
        def _():
            dS_ref[...] = jnp.zeros_like(dS_ref)
            dal_ref[...] = jnp.zeros_like(dal_ref)
            ddt_ref[...] = jnp.zeros_like(ddt_ref)
            dgn_ref[...] = jnp.zeros_like(dgn_ref)

        batches = _gdn_batches(qkv_ref[...], ba_ref[...], z_ref[...], al_ref[...], dt_ref[...])
        do = do_ref[...]
        do_b = jnp.stack([do[ci * C:(ci + 1) * C, h * HD:(h + 1) * HD]
                          for ci in range(GDN_STEP) for h in range(GDN_H)])
        P = inv_ref[0]
        fn = lambda *args: _gdn_step(dots, hdots, *args, P_saved=P)
        _, vjp = jax.vjp(fn, save_ref[0], *batches, gn_ref[...])
        dS, dq, dk, dv, dz, db_raw, da_raw, dalog, ddtb, dgn = vjp((do_b, dS_ref[...]))
        dS_ref[...] = dS
        lane = lax.broadcasted_iota(jnp.int32, (1, LANE), 1)
        dal = jnp.zeros((1, LANE), F32)
        ddt = jnp.zeros((1, LANE), F32)
        for ci in range(GDN_STEP):
            rows = slice(ci * C, (ci + 1) * C)
            dba = jnp.zeros((C, LANE), F32)
            for h in range(GDN_H):
                b = ci * GDN_H + h
                dqkv_ref[rows, h * HD:(h + 1) * HD] = dq[b]
                dqkv_ref[rows, GDN_W + h * HD:GDN_W + (h + 1) * HD] = dk[b]
                dqkv_ref[rows, 2 * GDN_W + h * HD:2 * GDN_W + (h + 1) * HD] = dv[b]
                dz_ref[rows, h * HD:(h + 1) * HD] = dz[b].astype(BF16)
                hot_b = (lane == h).astype(F32)
                dba = dba + db_raw[b] * hot_b + da_raw[b] * (lane == GDN_H + h).astype(F32)
                dal = dal + dalog[b] * hot_b
                ddt = ddt + ddtb[b] * hot_b
            dba_ref[rows, :] = dba.astype(BF16)
        dal_ref[...] += dal
        ddt_ref[...] += ddt
        dgn_ref[...] += dgn

    rev = lambda n: N - 1 - n
    row = jax.ShapeDtypeStruct((1, LANE), F32)
    return _pcall(
        body, name=name, grid=(N,),
        in_specs=[pl.BlockSpec((R, 3 * GDN_W), lambda n: (rev(n), 0)),
                  pl.BlockSpec((R, LANE), lambda n: (rev(n), (4 * GDN_W + POOL_G * HD) // LANE)),
                  pl.BlockSpec((R, GDN_W), lambda n: (rev(n), 3)),
                  _row_spec(LANE), _row_spec(LANE), _row_spec(HD),
                  pl.BlockSpec((1, GDN_H, HD, HD), lambda n: (rev(n), 0, 0, 0)),
                  pl.BlockSpec((1, B, C, C), lambda n: (rev(n), 0, 0, 0)),
                  pl.BlockSpec((R, GDN_W), lambda n: (rev(n), 0))],
        out_specs=[pl.BlockSpec((R, 3 * GDN_W), lambda n: (rev(n), 0)),
                   pl.BlockSpec((R, GDN_W), lambda n: (rev(n), 0)),
                   pl.BlockSpec((R, LANE), lambda n: (rev(n), 0)),
                   _row_spec(LANE), _row_spec(LANE), _row_spec(HD)],
        out_shape=[jax.ShapeDtypeStruct((T, 3 * GDN_W), F32), jax.ShapeDtypeStruct((T, GDN_W), BF16),
                   jax.ShapeDtypeStruct((T, LANE), BF16), row, row, jax.ShapeDtypeStruct((1, HD), F32)],
        scratch_shapes=[pltpu.VMEM((GDN_H, HD, HD), F32)],
        long_call=True)(qkv, proj, proj, alog_row, dt_row, gnorm, saved, inverses, docat)


POOL_HALO = 16


def _pool_pick(j, s2, s4, s8, s16):
    return jnp.where(j == 0, s2, jnp.where(j == 1, s4, jnp.where(j == 2, s8, s16)))


def _pool_count(j, t0, rows):
    t1 = (t0 + 1 + lax.broadcasted_iota(jnp.int32, (rows, 1), 0)).astype(F32)
    win = jnp.where(j == 0, 2.0, jnp.where(j == 1, 4.0, jnp.where(j == 2, 8.0, 16.0)))
    return jnp.minimum(t1, win)


def _pooled(p, prev, i, j, tb):
    ext = jnp.concatenate([prev, p], axis=0)
    s2 = ext + _back(ext, 1)
    s4 = s2 + _back(s2, 2)
    s8 = s4 + _back(s4, 4)
    s16 = s8 + _back(s8, 8)
    s = _pool_pick(j, s2, s4, s8, s16)[POOL_HALO:]
    return s / _pool_count(j, i * tb, tb) - p


def pool_fwd(proj, pool_w, pool_scale, name):
    T = proj.shape[0]
    tb = _rows(T, 512)
    c0 = 4 * GDN_H

    def body(p_ref, pp_ref, w_ref, s_ref, o_ref):
        i, j = pl.program_id(0), pl.program_id(1)
        p = p_ref[...]
        prev = jnp.where(i > 0, pp_ref[...], 0.0)
        pooled = _pooled(p, prev, i, j, tb)
        o_ref[...] = (_bdot(pooled, w_ref[0], NN) * s_ref[...]).astype(BF16)

    return _pcall(
        body, name=name, grid=(T // tb, POOL_G),
        in_specs=[pl.BlockSpec((tb, HD), lambda i, j: (i, c0 + j)),
                  pl.BlockSpec((POOL_HALO, HD), lambda i, j: (jnp.maximum(i * (tb // POOL_HALO) - 1, 0), c0 + j)),
                  pl.BlockSpec((1, HD, HD), lambda i, j: (j, 0, 0)),
                  pl.BlockSpec((1, HD), lambda i, j: (0, j))],
        out_specs=pl.BlockSpec((tb, HD), lambda i, j: (i, j)),
        out_shape=jax.ShapeDtypeStruct((T, POOL_G * HD), BF16),
        compiler_params=_params(2))(proj, proj, pool_w, pool_scale)


def pool_bwd(proj, pool_w, pool_scale, docat, name):
    T = proj.shape[0]
    tb = _rows(T, 512)
    nrb = T // tb
    c0 = 4 * GDN_H
    HB = POOL_HALO

    def body(p_ref, pp_ref, w_ref, s_ref, d_ref, dn_ref, dp_ref, dw_ref, ds_ref):
        j, i = pl.program_id(0), pl.program_id(1)
        p = p_ref[...]
        prev = jnp.where(i > 0, pp_ref[...], 0.0)
        pooled = _pooled(p, prev, i, j, tb)
        w, scale = w_ref[0], s_ref[...]
        dy = d_ref[...]
        dnext = jnp.where(i < nrb - 1, dn_ref[...], 0.0)
        dyp = jnp.concatenate([dy, dnext], axis=0) * scale
        dpooled = _bdot(dyp, w, NT)
        qn = dpooled / _pool_count(j, i * tb, tb + HB)
        a2 = qn + _ahead(qn, 1)
        a4 = a2 + _ahead(a2, 2)
        a8 = a4 + _ahead(a4, 4)
        a16 = a8 + _ahead(a8, 8)
        dp_ref[...] = (_pool_pick(j, a2, a4, a8, a16) - dpooled)[:tb].astype(BF16)

        @pl.when(i == 0)
        def _():
            dw_ref[...] = jnp.zeros_like(dw_ref)
            ds_ref[...] = jnp.zeros_like(ds_ref)

        dw_ref[0] += _bdot(pooled, dyp[:tb], TN)
        ds_ref[...] += jnp.sum(dy * _bdot(pooled, w, NN), axis=0, keepdims=True)

    return _pcall(
        body, name=name, grid=(POOL_G, nrb),
        in_specs=[pl.BlockSpec((tb, HD), lambda j, i: (i, c0 + j)),
                  pl.BlockSpec((HB, HD), lambda j, i: (jnp.maximum(i * (tb // HB) - 1, 0), c0 + j)),
                  pl.BlockSpec((1, HD, HD), lambda j, i: (j, 0, 0)),
                  pl.BlockSpec((1, HD), lambda j, i: (0, j)),
                  pl.BlockSpec((tb, HD), lambda j, i: (i, POOL_G + j)),
                  pl.BlockSpec((HB, HD), lambda j, i: (jnp.minimum(i + 1, nrb - 1) * (tb // HB), POOL_G + j))],
        out_specs=[pl.BlockSpec((tb, HD), lambda j, i: (i, j)),
                   pl.BlockSpec((1, HD, HD), lambda j, i: (j, 0, 0)),
                   pl.BlockSpec((1, HD), lambda j, i: (0, j))],
        out_shape=[jax.ShapeDtypeStruct((T, POOL_G * HD), BF16),
                   jax.ShapeDtypeStruct((POOL_G, HD, HD), F32),
                   jax.ShapeDtypeStruct((1, POOL_G * HD), F32)],
        compiler_params=_params(2))(proj, proj, pool_w, pool_scale, docat, docat)


ATT_W = ATT_H * HD
GROUP_COLS = 3 * ATT_W


def to_residue_major(t, d):
    if d == 1:
        return t
    T, C = t.shape
    return t.reshape(T // d, d, C).transpose(1, 0, 2).reshape(T, C)


def to_token_order(t, d):
    if d == 1:
        return t
    T, C = t.shape
    return t.reshape(d, T // d, C).transpose(1, 0, 2).reshape(T, C)


def headnorm_fwd(proj, qk_gain, name):
    T = proj.shape[0]
    tb = _rows(T, 256)

    def body(x_ref, g_ref, o_ref):
        g = g_ref[...]
        for h in range(2 * ATT_H):
            sl = slice(h * HD, (h + 1) * HD)
            x = x_ref[:, sl].astype(F32)
            n = x * lax.rsqrt(jnp.mean(x * x, axis=-1, keepdims=True) + RMS_EPS)
            gain = g[0:1] * (HD ** -0.5) if h < ATT_H else g[1:2]
            o_ref[:, sl] = (n * gain).astype(BF16)
        o_ref[:, 2 * ATT_W:] = x_ref[:, 2 * ATT_W:]

    blk = pl.BlockSpec((tb, GROUP_COLS), lambda i: (i, 0))
    return _pcall(
        body, name=name, grid=(T // tb,),
        in_specs=[blk, pl.BlockSpec((SUBLANE, HD), lambda i: (0, 0))],
        out_specs=blk, out_shape=jax.ShapeDtypeStruct((T, GROUP_COLS), BF16),
        long_call=True)(proj, qk_gain)


def headnorm_bwd(proj, qk_gain, dq, dk, dv, name):
    T = proj.shape[0]
    tb = _rows(T, 256)

    def body(x_ref, g_ref, dq_ref, dk_ref, dv_ref, dx_ref, dg_ref):
        i = pl.program_id(0)
        g = g_ref[...]

        @pl.when(i == 0)
        def _():
            dg_ref[...] = jnp.zeros_like(dg_ref)

        for part, d_ref in enumerate((dq_ref, dk_ref)):
            gain = g[0:1] * (HD ** -0.5) if part == 0 else g[1:2]
            scale = (HD ** -0.5) if part == 0 else 1.0
            acc = jnp.zeros((1, HD), F32)
            for h in range(ATT_H):
                x = x_ref[:, part * ATT_W + h * HD:part * ATT_W + (h + 1) * HD].astype(F32)
                d = d_ref[:, h * HD:(h + 1) * HD].astype(F32)
                r = lax.rsqrt(jnp.mean(x * x, axis=-1, keepdims=True) + RMS_EPS)
                n = x * r
                dn = d * gain
                dx = r * (dn - n * jnp.mean(dn * n, axis=-1, keepdims=True))
                dx_ref[:, part * ATT_W + h * HD:part * ATT_W + (h + 1) * HD] = dx.astype(BF16)
                acc = acc + jnp.sum(d * n, axis=0, keepdims=True)
            dg_ref[part:part + 1, :] += acc * scale
        dx_ref[:, 2 * ATT_W:] = dv_ref[...]

    blk = pl.BlockSpec((tb, GROUP_COLS), lambda i: (i, 0))
    dblk = pl.BlockSpec((tb, ATT_W), lambda i: (i, 0))
    gspec = pl.BlockSpec((SUBLANE, HD), lambda i: (0, 0))
    return _pcall(
        body, name=name, grid=(T // tb,),
        in_specs=[blk, gspec, dblk, dblk, dblk],
        out_specs=[blk, gspec],
        out_shape=[jax.ShapeDtypeStruct((T, GROUP_COLS), BF16), jax.ShapeDtypeStruct((SUBLANE, HD), F32)],
        long_call=True)(proj, qk_gain, dq, dk, dv)


def _heads(ref):
    return jnp.stack([ref[:, h * HD:(h + 1) * HD] for h in range(ATT_H)])


def _slopes(dil):
    h = lax.broadcasted_iota(jnp.int32, (ATT_H, 1, 1), 0)
    return lax.bitcast_convert_type((126 - h) << 23, F32) * float(dil)


def _att_scores_b(q, k, slope, n_ok, far, keys_first=False):
    r = lax.broadcasted_iota(jnp.int32, (1, ATT_BLK, ATT_BLK), 1)
    c = lax.broadcasted_iota(jnp.int32, (1, ATT_BLK, ATT_BLK), 2)
    a, j = (c, r) if keys_first else (r, c)
    rel = (ATT_BLK + a - j) if far else (a - j)
    mask = ((j >= a) & n_ok) if far else (j <= a)
    s = (_bdot(k, q, BNT) if keys_first else _bdot(q, k, BNT)) - slope * rel.astype(F32)
    return jnp.where(mask, s, NEG), mask


def _att_blocks(nb, width, shift):
    def make(col):
        return pl.BlockSpec((ATT_BLK, width),
                            lambda r, n: (r * nb + jnp.clip(n + shift, 0, nb - 1), col))
    return make


def _lane_col(cols):
    lane = lax.broadcasted_iota(jnp.int32, (1, LANE), 1)
    out = jnp.zeros((ATT_BLK, LANE), F32)
    for h, c in enumerate(cols):
        out = out + c * (lane == h).astype(F32)
    return out


def att_fwd(qkvn, gi, name):
    T = qkvn.shape[0]
    dil = DIL[gi]
    nb = T // dil // ATT_BLK

    def body(q_ref, kp_ref, kc_ref, vp_ref, vc_ref, o_ref, l_ref):
        n_ok = pl.program_id(1) > 0
        slope = _slopes(dil)
        q = _heads(q_ref)
        s_c, _ = _att_scores_b(q, _heads(kc_ref), slope, n_ok, False)
        s_p, _ = _att_scores_b(q, _heads(kp_ref), slope, n_ok, True)
        m = jnp.maximum(jnp.max(s_c, axis=-1, keepdims=True), jnp.max(s_p, axis=-1, keepdims=True))
        p_c, p_p = jnp.exp(s_c - m), jnp.exp(s_p - m)
        l = jnp.sum(p_c, axis=-1, keepdims=True) + jnp.sum(p_p, axis=-1, keepdims=True)
        o = (_bdot(p_c, _heads(vc_ref), BNN) + _bdot(p_p, _heads(vp_ref), BNN)) / l
        lse = m + jnp.log(l)
        for h in range(ATT_H):
            o_ref[:, h * HD:(h + 1) * HD] = o[h]
        l_ref[...] = _lane_col([lse[h] for h in range(ATT_H)])

    cur, prv = _att_blocks(nb, ATT_W, 0), _att_blocks(nb, ATT_W, -1)
    return _pcall(
        body, name=name, grid=(dil, nb), in_specs=[cur(0), prv(1), cur(1), prv(2), cur(2)],
        out_specs=[cur(0), _att_blocks(nb, LANE, 0)(0)],
        out_shape=[jax.ShapeDtypeStruct((T, ATT_W), F32), jax.ShapeDtypeStruct((T, LANE), F32)],
        long_call=True)(qkvn, qkvn, qkvn, qkvn, qkvn)


def att_merge(os, lses, name):
    T = os[0].shape[0]
    tb = _rows(T, 512)

    def body(o0, o1, o2, l0, l1, l2, o_ref, l_ref):
        a, b, c = l0[...], l1[...], l2[...]
        m = jnp.maximum(a, jnp.maximum(b, c))
        wa, wb, wc = jnp.exp(a - m), jnp.exp(b - m), jnp.exp(c - m)
        den = wa + wb + wc
        l_ref[...] = m + jnp.log(den)
        wa, wb, wc = wa / den, wb / den, wc / den
        for h in range(ATT_H):
            sl = slice(h * HD, (h + 1) * HD)
            o_ref[:, sl] = (wa[:, h:h + 1] * o0[:, sl] + wb[:, h:h + 1] * o1[:, sl]
                            + wc[:, h:h + 1] * o2[:, sl])

    blk = pl.BlockSpec((tb, ATT_W), lambda i: (i, 0))
    lblk = pl.BlockSpec((tb, LANE), lambda i: (i, 0))
    return _pcall(
        body, name=name, grid=(T // tb,), in_specs=[blk] * 3 + [lblk] * 3, out_specs=[blk, lblk],
        out_shape=[jax.ShapeDtypeStruct((T, ATT_W), F32), jax.ShapeDtypeStruct((T, LANE), F32)],
        compiler_params=_params(1))(*os, *lses)


def att_delta(do, o, name):
    T = do.shape[0]
    tb = _rows(T, 512)

    def body(d_ref, o_ref, out_ref):
        lane = lax.broadcasted_iota(jnp.int32, (1, LANE), 1)
        out = jnp.zeros((tb, LANE), F32)
        for h in range(ATT_H):
            sl = slice(h * HD, (h + 1) * HD)
            s = jnp.sum(d_ref[:, sl] * o_ref[:, sl], axis=-1, keepdims=True)
            out = out + s * (lane == h).astype(F32)
        out_ref[...] = out

    blk = pl.BlockSpec((tb, ATT_W), lambda i: (i, 0))
    return _pcall(
        body, name=name, grid=(T // tb,), in_specs=[blk, blk],
        out_specs=pl.BlockSpec((tb, LANE), lambda i: (i, 0)),
        out_shape=jax.ShapeDtypeStruct((T, LANE), F32), compiler_params=_params(1))(do, o)


def att_bwd_q(qkvn, do, lse, delta, gi, name):
    T = qkvn.shape[0]
    dil = DIL[gi]
    nb = T // dil // ATT_BLK

    def body(q_ref, kp_ref, kc_ref, vp_ref, vc_ref, do_ref, l_ref, d_ref, dq_ref):
        n_ok = pl.program_id(1) > 0
        slope = _slopes(dil)
        lse, dl = l_ref[...], d_ref[...]
        lse = jnp.stack([lse[:, h:h + 1] for h in range(ATT_H)])
        dl = jnp.stack([dl[:, h:h + 1] for h in range(ATT_H)])
        q, do = _heads(q_ref), _heads(do_ref)
        dq = jnp.zeros((ATT_H, ATT_BLK, HD), F32)
        for k_ref, v_ref, far in ((kc_ref, vc_ref, False), (kp_ref, vp_ref, True)):
            k = _heads(k_ref)
            s, mask = _att_scores_b(q, k, slope, n_ok, far)
            p = jnp.where(mask, jnp.exp(s - lse), 0.0)
            ds = p * (_bdot(do, _heads(v_ref), BNT) - dl)
            dq = dq + _bdot(ds, k, BNN)
        for h in range(ATT_H):
            dq_ref[:, h * HD:(h + 1) * HD] = dq[h].astype(BF16)

    cur, prv = _att_blocks(nb, ATT_W, 0), _att_blocks(nb, ATT_W, -1)
    small = _att_blocks(nb, LANE, 0)(0)
    return _pcall(
        body, name=name, grid=(dil, nb),
        in_specs=[cur(0), prv(1), cur(1), prv(2), cur(2), cur(0), small, small],
        out_specs=cur(0), out_shape=jax.ShapeDtypeStruct((T, ATT_W), BF16),
        long_call=True)(qkvn, qkvn, qkvn, qkvn, qkvn, do, lse, delta)


def att_bwd_kv(qkvn, do, lse, delta, gi, name):
    T = qkvn.shape[0]
    dil = DIL[gi]
    nb = T // dil // ATT_BLK

    def body(k_ref, v_ref, q0_ref, q1_ref, do0_ref, do1_ref, l0_ref, l1_ref, d0_ref, d1_ref,
             dk_ref, dv_ref):
        n_ok = pl.program_id(1) < nb - 1
        slope = _slopes(dil)
        by_row = lambda ref: jnp.stack([ref[...].T[h:h + 1, :] for h in range(ATT_H)])
        k, v = _heads(k_ref), _heads(v_ref)
        dk = jnp.zeros((ATT_H, ATT_BLK, HD), F32)
        dv = jnp.zeros((ATT_H, ATT_BLK, HD), F32)
        for q_ref, do_ref, l_ref, d_ref, far in ((q0_ref, do0_ref, l0_ref, d0_ref, False),
                                                 (q1_ref, do1_ref, l1_ref, d1_ref, True)):
            q, do = _heads(q_ref), _heads(do_ref)
            s, mask = _att_scores_b(q, k, slope, n_ok, far, keys_first=True)
            p = jnp.where(mask, jnp.exp(s - by_row(l_ref)), 0.0)
            dv = dv + _bdot(p, do, BNN)
            ds = p * (_bdot(v, do, BNT) - by_row(d_ref))
            dk = dk + _bdot(ds, q, BNN)
        for h in range(ATT_H):
            dk_ref[:, h * HD:(h + 1) * HD] = dk[h].astype(BF16)
            dv_ref[:, h * HD:(h + 1) * HD] = dv[h].astype(BF16)

    cur, nxt = _att_blocks(nb, ATT_W, 0), _att_blocks(nb, ATT_W, 1)
    s0, s1 = _att_blocks(nb, LANE, 0)(0), _att_blocks(nb, LANE, 1)(0)
    return _pcall(
        body, name=name, grid=(dil, nb),
        in_specs=[cur(1), cur(2), cur(0), nxt(0), cur(0), nxt(0), s0, s1, s0, s1],
        out_specs=[cur(0), cur(0)], out_shape=[jax.ShapeDtypeStruct((T, ATT_W), BF16)] * 2,
        long_call=True)(qkvn, qkvn, qkvn, qkvn, do, do, lse, lse, delta, delta)


def adamw(w, g, m, v, name):
    shape = w.shape
    C = shape[-1]
    R = math.prod(shape[:-1])
    to2d = lambda t: t.reshape(R, C)
    tb = _rows(R, max(16, (256 * 1536 // C) // 16 * 16))
    c1 = 1.0 - ADAM_B1 ** ADAM_STEP
    c2 = 1.0 - ADAM_B2 ** ADAM_STEP

    def body(w_ref, g_ref, m_ref, v_ref, d_ref, nm_ref, nv_ref):
        gv = g_ref[...]
        nm = ADAM_B1 * m_ref[...] + (1.0 - ADAM_B1) * gv
        nv = ADAM_B2 * v_ref[...] + (1.0 - ADAM_B2) * (gv * gv)
        d_ref[...] = -ADAM_LR * ((nm / c1) / (jnp.sqrt(nv / c2) + ADAM_EPS) + ADAM_WD * w_ref[...])
        nm_ref[...] = nm
        nv_ref[...] = nv

    blk = pl.BlockSpec((tb, C), lambda i: (i, 0))
    out = jax.ShapeDtypeStruct((R, C), F32)
    d, nm, nv = _pcall(
        body, name=name, grid=(R // tb,), in_specs=[blk] * 4, out_specs=[blk] * 3,
        out_shape=[out, out, out], compiler_params=_params(1))(to2d(w), to2d(g), to2d(m), to2d(v))
    return d.reshape(shape), nm.reshape(shape), nv.reshape(shape)


def _pad_rows8(w):
    return jnp.pad(w, ((0, SUBLANE - w.shape[0]), (0, 0)))


def _lane_row(v):
    return jnp.pad(v, (0, LANE - v.shape[0]))[None, :]


def _even_reorder(w_in):
    z4 = 4 * GDN_W
    pad = jnp.zeros((w_in.shape[0], EVEN_PAD - EVEN_COLS), w_in.dtype)
    return jnp.concatenate([w_in[:, :z4], w_in[:, z4 + 2 * GDN_H:], w_in[:, z4:z4 + 2 * GDN_H], pad], axis=1)


def _even_restore(dw):
    z4 = 4 * GDN_W
    p4 = POOL_G * HD
    return jnp.concatenate([dw[:, :z4], dw[:, z4 + p4:z4 + p4 + 2 * GDN_H], dw[:, z4:z4 + p4]], axis=1)


def _ffn_fwd(tag, x1, mod_f, wl):
    sh, sc, g = mod_f
    hf = modnorm_fwd(x1, wl["norm_ffn"], sc, sh, f"{tag}_ffn_norm")
    up = matmul(hf, wl["ffn_w_up"], "nn", BF16, f"{tag}_ffn_up")
    act = ffn_mid_fwd(up, wl["ffn_conv_w8"], wl["ffn_conv_b"], f"{tag}_ffn_mid")
    f = matmul(act, wl["ffn_w_down"], "nn", F32, f"{tag}_ffn_down")
    x2 = gres_fwd(x1, g, f, f"{tag}_ffn_res")
    return x2, (x1, hf, up, act, f)


def _ffn_bwd(tag, dx2, saved, mod_f, wl):
    x1, hf, up, act, f = saved
    sh, sc, g = mod_f
    df, dg = gres_bwd(dx2, g, f, f"{tag}_ffn_res_bwd")
    dact = matmul(df, wl["ffn_w_down"], "nt", BF16, f"{tag}_ffn_down_da")
    wl["on_grad"]("ffn_w_down", matmul(act, df, "tn", F32, f"{tag}_ffn_down_dw", pieces="row"))
    dgate, dval, dcw, dcb = ffn_mid_bwd(up, wl["ffn_conv_w8"], wl["ffn_conv_b"], dact, f"{tag}_ffn_mid_bwd")
    dup = jnp.concatenate([dgate, dval], axis=1)
    dhf = matmul(dup, wl["ffn_w_up"], "nt", F32, f"{tag}_ffn_up_da")
    wl["on_grad"]("ffn_w_up", matmul(hf, dup, "tn", F32, f"{tag}_ffn_up_dw", pieces="col"))
    dx1, dgain, dsc, dsh = modnorm_bwd(x1, wl["norm_ffn"], sc, dhf, dx2, f"{tag}_ffn_norm_bwd")
    grads = {"norm_ffn": dgain[0], "ffn_conv_w": dcw[:FFN_CONV], "ffn_conv_b": dcb[0]}
    return dx1, (dsh, dsc, dg), grads


def _even_fwd(tag, x, mod_m, wl):
    sh, sc, g = mod_m
    hm = modnorm_fwd(x, wl["norm_mix"], sc, sh, f"{tag}_mix_norm")
    proj = matmul(hm, wl["w_in"], "nn", F32, f"{tag}_ev_in")
    qkv = gdn_conv_fwd(proj, wl["gdn_conv_w8"], f"{tag}_gdn_conv")
    o_a, *states = gdn_chunk_fwd(qkv, proj, wl["alog_row"], wl["dt_row"], wl["gdn_norm"], f"{tag}_gdn_chunk")
    o_b = pool_fwd(proj, wl["pool_w"], wl["pool_scale"], f"{tag}_pool")
    ocat = jnp.concatenate([o_a, o_b], axis=1)
    y = matmul(ocat, wl["w_out"], "nn", F32, f"{tag}_ev_out")
    x1 = gres_fwd(x, g, y, f"{tag}_mix_res")
    return x1, (x, hm, proj, qkv, states, ocat, y)


def _even_bwd(tag, dx1, saved, mod_m, wl):
    x, hm, proj, qkv, states, ocat, y = saved
    sh, sc, g = mod_m
    dy, dg = gres_bwd(dx1, g, y, f"{tag}_mix_res_bwd")
    docat = matmul(dy, wl["w_out"], "nt", F32, f"{tag}_ev_out_da")
    wl["on_grad"]("ev_w_out", matmul(ocat, dy, "tn", F32, f"{tag}_ev_out_dw", pieces="row"))
    dqkv, dz, dba, dalog, ddt, dgn = gdn_chunk_bwd(
        qkv, proj, wl["alog_row"], wl["dt_row"], wl["gdn_norm"], *states, docat, f"{tag}_gdn_chunk_bwd")
    dxc, dconv = gdn_conv_bwd(proj, wl["gdn_conv_w8"], dqkv, f"{tag}_gdn_conv_bwd")
    dp, dpw, dps = pool_bwd(proj, wl["pool_w"], wl["pool_scale"], docat, f"{tag}_pool_bwd")
    dproj = jnp.concatenate([dxc, dz, dp, dba], axis=1)
    dhm = matmul(dproj, wl["w_in"], "nt", F32, f"{tag}_ev_in_da")
    wl["on_grad"]("ev_w_in", col_pieces(_even_restore(matmul(hm, dproj, "tn", F32, f"{tag}_ev_in_dw"))))
    dx, dgain, dsc, dsh = modnorm_bwd(x, wl["norm_mix"], sc, dhm, dx1, f"{tag}_mix_norm_bwd")
    grads = {"norm_mix": dgain[0], "gdn_conv_w": dconv[:GDN_CONV], "gdn_a_log": dalog[0, :GDN_H], "gdn_dt_bias": ddt[0, :GDN_H],
             "gdn_norm": dgn[0], "pool_w": dpw, "pool_scale": dps[0]}
    return dx, (dsh, dsc, dg), grads


def _odd_fwd(tag, x, mod_m, wl):
    sh, sc, g = mod_m
    hm = modnorm_fwd(x, wl["norm_mix"], sc, sh, f"{tag}_mix_norm")
    projs, qkvns, outs, lses = [], [], [], []
    for gi, d in enumerate(DIL):
        w_g = wl["w_in"][:, gi * GROUP_COLS:(gi + 1) * GROUP_COLS]
        proj = matmul(to_residue_major(hm, d), w_g, "nn", BF16, f"{tag}_od_in{gi}")
        qkvn = headnorm_fwd(proj, wl["qk_gain8"], f"{tag}_headnorm{gi}")
        o_g, l_g = att_fwd(qkvn, gi, f"{tag}_att{gi}")
        projs.append(proj)
        qkvns.append(qkvn)
        outs.append(to_token_order(o_g, d))
        lses.append(to_token_order(l_g, d))
    o, lse = att_merge(outs, lses, f"{tag}_att_merge")
    y = matmul(o, wl["w_out"], "nn", F32, f"{tag}_od_out")
    x1 = gres_fwd(x, g, y, f"{tag}_mix_res")
    return x1, (x, hm, projs, qkvns, o, lse, y)


def _odd_bwd(tag, dx1, saved, mod_m, wl):
    x, hm, projs, qkvns, o, lse, y = saved
    sh, sc, g = mod_m
    dy, dg = gres_bwd(dx1, g, y, f"{tag}_mix_res_bwd")
    do = matmul(dy, wl["w_out"], "nt", F32, f"{tag}_od_out_da")
    wl["on_grad"]("od_w_out", matmul(o, dy, "tn", F32, f"{tag}_od_out_dw", pieces="row"))
    delta = att_delta(do, o, f"{tag}_att_delta")
    dhm, dw_in, dgain_qk = None, [], None
    for gi, d in enumerate(DIL):
        w_g = wl["w_in"][:, gi * GROUP_COLS:(gi + 1) * GROUP_COLS]
        do_g, lse_g, dl_g = (to_residue_major(t, d) for t in (do, lse, delta))
        dq = att_bwd_q(qkvns[gi], do_g, lse_g, dl_g, gi, f"{tag}_att{gi}_dq")
        dk, dv = att_bwd_kv(qkvns[gi], do_g, lse_g, dl_g, gi, f"{tag}_att{gi}_dkv")
        dproj, dgain = headnorm_bwd(projs[gi], wl["qk_gain8"], dq, dk, dv, f"{tag}_headnorm{gi}_bwd")
        dhm_g = to_token_order(matmul(dproj, w_g, "nt", F32, f"{tag}_od_in{gi}_da"), d)
        dw_in.append(matmul(to_residue_major(hm, d), dproj, "tn", F32, f"{tag}_od_in{gi}_dw"))
        dhm = dhm_g if dhm is None else dhm + dhm_g
        dgain_qk = dgain if dgain_qk is None else dgain_qk + dgain
    wl["on_grad"]("od_w_in", col_pieces(jnp.concatenate(dw_in, axis=1)))
    dx, dgain, dsc, dsh = modnorm_bwd(x, wl["norm_mix"], sc, dhm, dx1, f"{tag}_mix_norm_bwd")
    grads = {"norm_mix": dgain[0], "att_q_norm": dgain_qk[0], "att_k_norm": dgain_qk[1]}
    return dx, (dsh, dsc, dg), grads


def col_pieces(dw):
    M, N = dw.shape
    return dw.reshape(2, M // 2, N_CHIPS, N // N_CHIPS).transpose(0, 2, 1, 3)


class _Lazy:
    def __init__(self, fn):
        self.fn, self.value = fn, None


class _LayerWeights(dict):
    def __getitem__(self, key):
        v = dict.__getitem__(self, key)
        if isinstance(v, _Lazy):
            if v.value is None:
                v.value = v.fn()
            return v.value
        return v


def _layer_weights(i, W, big, on_grad):
    e = i // 2
    wl = _LayerWeights({
        "norm_mix": W["norm_mix"][i][None, :], "norm_ffn": W["norm_ffn"][i][None, :],
        "ffn_w_up": _Lazy(lambda: big("ffn_w_up", i)), "ffn_w_down": _Lazy(lambda: big("ffn_w_down", i)),
        "ffn_conv_w8": _pad_rows8(W["ffn_conv_w"][i]), "ffn_conv_b": W["ffn_conv_b"][i][None, :],
        "on_grad": lambda name, pieces: on_grad(name, i if name.startswith("ffn") else e, pieces)})
    if i % 2 == 0:
        wl.update({"w_in": _Lazy(lambda: _even_reorder(big("ev_w_in", e))),
                   "w_out": _Lazy(lambda: big("ev_w_out", e)),
                   "gdn_conv_w8": _pad_rows8(W["gdn_conv_w"][e]),
                   "alog_row": _lane_row(W["gdn_a_log"][e]), "dt_row": _lane_row(W["gdn_dt_bias"][e]),
                   "gdn_norm": W["gdn_norm"][e][None, :], "pool_w": W["pool_w"][e],
                   "pool_scale": W["pool_scale"][e][None, :]})
    else:
        wl.update({"w_in": _Lazy(lambda: big("od_w_in", e)), "w_out": _Lazy(lambda: big("od_w_out", e)),
                   "qk_gain8": _pad_rows8(jnp.stack([W["att_q_norm"][e], W["att_k_norm"][e]]))})
    return wl


def local_step(x, target, mod, W, big, on_grad):
    depth = mod.shape[0]
    row = lambda i, k: mod[i, k][None, :]
    saved, wls = [], []
    for i in range(depth):
        wl = _layer_weights(i, W, big, on_grad)
        mod_m = (row(i, 0), row(i, 1), row(i, 2))
        mod_f = (row(i, 3), row(i, 4), row(i, 5))
        fwd = _even_fwd if i % 2 == 0 else _odd_fwd
        x1, s_mix = fwd(f"l{i}", x, mod_m, wl)
        x, s_ffn = _ffn_fwd(f"l{i}", x1, mod_f, wl)
        saved.append((s_mix, s_ffn, mod_m, mod_f))
        wls.append(wl)
    sq, dx = loss_head(x, target, "loss_head")
    dmod, grads = [None] * depth, [None] * depth
    for i in reversed(range(depth)):
        s_mix, s_ffn, mod_m, mod_f = saved[i]
        dx, dmf, g_ffn = _ffn_bwd(f"l{i}", dx, s_ffn, mod_f, wls[i])
        bwd = _even_bwd if i % 2 == 0 else _odd_bwd
        dx, dmm, g_mix = bwd(f"l{i}", dx, s_mix, mod_m, wls[i])
        dmod[i] = jnp.concatenate([t for t in dmm + dmf], axis=0)
        grads[i] = {**g_mix, **g_ffn}
    return sq, dx, jnp.stack(dmod), grads


def _place():
    return lax.axis_index("x"), lax.axis_index("y"), lax.axis_index("c")


def _other_chips(mx, my):
    return [(1 - mx, my), (mx, 1 - my), (1 - mx, 1 - my)]


def _sems(n):
    return [DMA_SEM((n,)), DMA_SEM((n,))]


def _put(buf, block, index):
    return lax.dynamic_update_index_in_dim(buf, block, index, 0)


def gather8_ride(x, then):
    def parts(ins, outs, sems):
        (x_ref,), (out_ref,), (send_sems, recv_sems) = ins, outs, sems
        mx, my, mc = _place()
        me, sibling = (mx, my, mc), (mx, my, 1 - mc)
        chips = _other_chips(mx, my)

        def slot(px, py, pc):
            return out_ref.at[4 * px + 2 * py + pc]

        def copy(k, block, to, src=None):
            return pltpu.make_async_remote_copy(
                src_ref=slot(*block) if src is None else src, dst_ref=slot(*block),
                send_sem=send_sems.at[k], recv_sem=recv_sems.at[k], device_id=to, device_id_type=MESH)

        first = lambda: ([copy(0, me, sibling, src=x_ref)]
                         + [copy(1 + j, me, (*chip, mc), src=x_ref) for j, chip in enumerate(chips)])
        passed = lambda: [copy(4 + j, (*chip, mc), sibling) for j, chip in enumerate(chips)]
        landed = lambda: [copy(1 + j, (*chip, mc), me) for j, chip in enumerate(chips)]
        from_sibling = lambda: ([copy(0, sibling, me)]
                                + [copy(4 + j, (*chip, 1 - mc), me) for j, chip in enumerate(chips)])
        return first, passed, landed, from_sibling

    def start(ins, outs, sems):
        first, _, _, _ = parts(ins, outs, sems)
        for cp in first():
            cp.start()

    def mid(ins, outs, sems):
        _, passed, landed, _ = parts(ins, outs, sems)
        for cp, fwd in zip(landed(), passed()):
            cp.wait_recv()
            fwd.start()

    def finish(ins, outs, sems):
        first, passed, _, from_sibling = parts(ins, outs, sems)
        for cp in from_sibling():
            cp.wait_recv()
        for cp in first() + passed():
            cp.wait_send()

    def landed_all(outs):
        mx, my, mc = _place()
        then(_put(outs[0], x, 4 * mx + 2 * my + mc))

    return Ride([x], [jax.ShapeDtypeStruct((8,) + x.shape, x.dtype)], _sems(7), start, finish,
                landed_all, mid=mid, heavy=True)


def all_gather8(x):
    box = []
    waiting = list(_RIDES)
    _RIDES[:] = [gather8_ride(x, box.append)]
    flush_rides()
    _RIDES[:] = waiting + _RIDES
    return box[0]


def sibling_halves_ride(p, then):
    def copy(ins, outs, sems):
        (p_ref,), (got_ref,), (send_sems, recv_sems) = ins, outs, sems
        mx, my, mc = _place()
        return pltpu.make_async_remote_copy(src_ref=p_ref.at[1 - mc], dst_ref=got_ref, send_sem=send_sems.at[0],
                                            recv_sem=recv_sems.at[0], device_id=(mx, my, 1 - mc), device_id_type=MESH)

    def start(ins, outs, sems):
        copy(ins, outs, sems).start()

    def finish(ins, outs, sems):
        cp = copy(ins, outs, sems)
        cp.wait_send()
        cp.wait_recv()

    def landed(outs):
        then(lax.dynamic_index_in_dim(p, _place()[2], 0, keepdims=False), outs[0])

    return Ride([p], [jax.ShapeDtypeStruct(p.shape[1:], p.dtype)], _sems(1), start, finish, landed)


def sibling_pair_ride(r, then):
    def copy(ins, outs, sems):
        (r_ref,), (got_ref,), (send_sems, recv_sems) = ins, outs, sems
        mx, my, mc = _place()
        return pltpu.make_async_remote_copy(src_ref=r_ref, dst_ref=got_ref, send_sem=send_sems.at[0],
                                            recv_sem=recv_sems.at[0], device_id=(mx, my, 1 - mc), device_id_type=MESH)

    def start(ins, outs, sems):
        copy(ins, outs, sems).start()

    def finish(ins, outs, sems):
        cp = copy(ins, outs, sems)
        cp.wait_send()
        cp.wait_recv()

    def landed(outs):
        then(jnp.where(_place()[2] == 0, jnp.stack([r, outs[0]]), jnp.stack([outs[0], r])))

    return Ride([r], [jax.ShapeDtypeStruct(r.shape, r.dtype)], _sems(1), start, finish, landed)


def chip_scatter_ride(p, then):
    def parts(ins, outs, sems):
        (p_ref,), (out_ref,), (send_sems, recv_sems) = ins, outs, sems
        mx, my, mc = _place()
        mine = 2 * mx + my
        chips = _other_chips(mx, my)
        sends = [pltpu.make_async_remote_copy(
            src_ref=p_ref.at[2 * chip[0] + chip[1]], dst_ref=out_ref.at[mine], send_sem=send_sems.at[k],
            recv_sem=recv_sems.at[k], device_id=(*chip, mc), device_id_type=MESH) for k, chip in enumerate(chips)]
        recvs = lambda: [pltpu.make_async_remote_copy(
            src_ref=p_ref.at[mine], dst_ref=out_ref.at[2 * chip[0] + chip[1]], send_sem=send_sems.at[k],
            recv_sem=recv_sems.at[k], device_id=(*chip, mc), device_id_type=MESH) for k, chip in enumerate(chips)]
        return sends, recvs

    def start(ins, outs, sems):
        sends, _ = parts(ins, outs, sems)
        for cp in sends:
            cp.start()

    def finish(ins, outs, sems):
        sends, recvs = parts(ins, outs, sems)
        for cp in recvs():
            cp.wait_recv()
        for cp in sends:
            cp.wait_send()

    def landed(outs):
        mx, my, _ = _place()
        mine = 2 * mx + my
        then(_put(outs[0], lax.dynamic_index_in_dim(p, mine, 0, keepdims=False), mine))

    return Ride([p], [jax.ShapeDtypeStruct(p.shape, p.dtype)], _sems(3), start, finish, landed, heavy=True)


def _stream_rows(R, C):
    return _rows(R, max(16, (256 * 1536 // C) // 16 * 16))


def cast_bf16(w, name):
    R, C = w.shape
    tb = _stream_rows(R, C)

    def body(w_ref, o_ref):
        o_ref[...] = w_ref[...].astype(BF16)

    blk = pl.BlockSpec((tb, C), lambda i: (i, 0))
    return _pcall(body, name=name, grid=(R // tb,), in_specs=[blk], out_specs=blk,
                          out_shape=jax.ShapeDtypeStruct((R, C), BF16), compiler_params=_params(1))(w)


def sum_slots(g, name):
    n, R, C = g.shape
    tb = _stream_rows(R, C)

    def body(*refs):
        acc = refs[0][...].astype(F32)
        for r in refs[1:n]:
            acc = acc + r[...].astype(F32)
        refs[n][...] = acc

    specs = [pl.BlockSpec((None, tb, C), functools.partial(lambda k, i: (k, i, 0), k)) for k in range(n)]
    return _pcall(body, name=name, grid=(R // tb,), in_specs=specs,
                          out_specs=pl.BlockSpec((tb, C), lambda i: (i, 0)),
                          out_shape=jax.ShapeDtypeStruct((R, C), F32), compiler_params=_params(1))(*([g] * n))


def add_to_bf16(a, b, name):
    R, C = a.shape
    tb = _stream_rows(R, C)

    def body(a_ref, b_ref, o_ref):
        o_ref[...] = (a_ref[...] + b_ref[...]).astype(BF16)

    blk = pl.BlockSpec((tb, C), lambda i: (i, 0))
    return _pcall(body, name=name, grid=(R // tb,), in_specs=[blk, blk], out_specs=blk,
                          out_shape=jax.ShapeDtypeStruct((R, C), BF16), compiler_params=_params(1))(a, b)


def ada_fwd(c_all, ada_w, bias, name):
    n, D, Cs = ada_w.shape
    tn = _tile(Cs, 512)

    def body(c_ref, w_ref, b_ref, o_ref):
        o_ref[...] = _bdot(_silu(c_ref[...]), w_ref[...], NN) + b_ref[...]

    return _pcall(
        body, name=name, grid=(n, Cs // tn),
        in_specs=[pl.BlockSpec((8, D), lambda l, j: (0, 0)),
                  pl.BlockSpec((None, D, tn), lambda l, j: (l, 0, j)),
                  pl.BlockSpec((None, 1, tn), lambda l, j: (l, 0, j))],
        out_specs=pl.BlockSpec((None, 8, tn), lambda l, j: (l, 0, j)),
        out_shape=jax.ShapeDtypeStruct((n, 8, Cs), F32), compiler_params=_params(2))(c_all, ada_w, bias)


def ada_bwd(c16, dmod16, name):
    n, _, Cs = dmod16.shape
    D = c16.shape[1]
    tn = _tile(Cs, 512)

    def body(c_ref, d_ref, o_ref):
        o_ref[...] = _bdot(_silu(c_ref[...]), d_ref[...], TN)

    return _pcall(
        body, name=name, grid=(n, Cs // tn),
        in_specs=[pl.BlockSpec((16, D), lambda l, j: (0, 0)),
                  pl.BlockSpec((None, 16, tn), lambda l, j: (l, 0, j))],
        out_specs=pl.BlockSpec((None, D, tn), lambda l, j: (l, 0, j)),
        out_shape=jax.ShapeDtypeStruct((n, D, Cs), F32), compiler_params=_params(2))(c16, dmod16)


WEIGHTS = ["ada_w", "ada_b", "norm_mix", "norm_ffn", "ev_w_in", "ev_w_out", "gdn_conv_w", "gdn_a_log",
           "gdn_dt_bias", "gdn_norm", "pool_w", "pool_scale", "od_w_in", "od_w_out", "att_q_norm",
           "att_k_norm", "ffn_w_up", "ffn_conv_w", "ffn_conv_b", "ffn_w_down"]
COL_SHARDED = ("ev_w_in", "od_w_in", "ffn_w_up")
ROW_SHARDED = ("ev_w_out", "od_w_out", "ffn_w_down")


def _pack(parts):
    rows, offs = [], []
    at = 0
    for p in parts:
        flat = p.reshape(-1).astype(F32)
        n = -(-flat.shape[0] // LANE)
        rows.append(jnp.pad(flat, (0, n * LANE - flat.shape[0])).reshape(n, LANE))
        offs.append((at, n))
        at += n
    pad = -at % 16
    if pad:
        rows.append(jnp.zeros((pad, LANE), F32))
    return jnp.concatenate(rows, axis=0), offs


def _unpack(buf, off, shape):
    at, n = off
    lead = buf.shape[:-2]
    flat = buf[..., at:at + n, :].reshape(lead + (n * LANE,))
    return flat[..., :math.prod(shape)].reshape(lead + tuple(shape))


def submit_weight_gather(store, key, shard, col_sharded, mc):
    R, C = shard.shape
    half = lax.dynamic_index_in_dim(shard.reshape(2, R // 2, C), mc, 0, keepdims=False)

    def landed(g):
        g = g.reshape(N_CHIPS, R, C)
        store[key] = g.transpose(1, 0, 2).reshape(R, N_CHIPS * C) if col_sharded else g.reshape(N_CHIPS * R, C)

    submit_ride(gather8_ride(half, landed))


def submit_grad_reduce(store, key, pieces, col_sharded):
    _, _, R, C = pieces.shape
    tag = f"{key[0]}{key[1]}"

    def paired(out):
        store[key] = out.reshape(2 * R, C) if col_sharded else out.transpose(1, 0, 2).reshape(R, 2 * C)

    def scattered(got):
        submit_ride(sibling_pair_ride(sum_slots(got, f"gsum_{tag}"), paired))

    def swapped(keep, got):
        chip_sum = add_to_bf16(keep.reshape(N_CHIPS * R, C), got.reshape(N_CHIPS * R, C), f"gadd_{tag}")
        submit_ride(chip_scatter_ride(chip_sum.reshape(N_CHIPS, R, C), scattered))

    submit_ride(sibling_halves_ride(pieces, swapped))


def kernel(x, c, ada_w, ada_b, norm_mix, norm_ffn, ev_w_in, ev_w_out, gdn_conv_w, gdn_a_log, gdn_dt_bias, gdn_norm, pool_w, pool_scale, od_w_in, od_w_out, att_q_norm, att_k_norm, ffn_w_up, ffn_conv_w, ffn_conv_b, ffn_w_down, loss_target, m_ada_w, m_ada_b, m_norm_mix, m_norm_ffn, m_ev_w_in, m_ev_w_out, m_gdn_conv_w, m_gdn_a_log, m_gdn_dt_bias, m_gdn_norm, m_pool_w, m_pool_scale, m_od_w_in, m_od_w_out, m_att_q_norm, m_att_k_norm, m_ffn_w_up, m_ffn_conv_w, m_ffn_conv_b, m_ffn_w_down, v_ada_w, v_ada_b, v_norm_mix, v_norm_ffn, v_ev_w_in, v_ev_w_out, v_gdn_conv_w, v_gdn_a_log, v_gdn_dt_bias, v_gdn_norm, v_pool_w, v_pool_scale, v_od_w_in, v_od_w_out, v_att_q_norm, v_att_k_norm, v_ffn_w_up, v_ffn_conv_w, v_ffn_conv_b, v_ffn_w_down):
    local = dict(ada_w=ada_w, ada_b=ada_b, norm_mix=norm_mix, norm_ffn=norm_ffn, ev_w_in=ev_w_in,
                 ev_w_out=ev_w_out, gdn_conv_w=gdn_conv_w, gdn_a_log=gdn_a_log, gdn_dt_bias=gdn_dt_bias,
                 gdn_norm=gdn_norm, pool_w=pool_w, pool_scale=pool_scale, od_w_in=od_w_in, od_w_out=od_w_out,
                 att_q_norm=att_q_norm, att_k_norm=att_k_norm, ffn_w_up=ffn_w_up, ffn_conv_w=ffn_conv_w,
                 ffn_conv_b=ffn_conv_b, ffn_w_down=ffn_w_down)
    moments_m = dict(zip(WEIGHTS, (m_ada_w, m_ada_b, m_norm_mix, m_norm_ffn, m_ev_w_in, m_ev_w_out,
                                   m_gdn_conv_w, m_gdn_a_log, m_gdn_dt_bias, m_gdn_norm, m_pool_w, m_pool_scale,
                                   m_od_w_in, m_od_w_out, m_att_q_norm, m_att_k_norm, m_ffn_w_up, m_ffn_conv_w,
                                   m_ffn_conv_b, m_ffn_w_down)))
    moments_v = dict(zip(WEIGHTS, (v_ada_w, v_ada_b, v_norm_mix, v_norm_ffn, v_ev_w_in, v_ev_w_out,
                                   v_gdn_conv_w, v_gdn_a_log, v_gdn_dt_bias, v_gdn_norm, v_pool_w, v_pool_scale,
                                   v_od_w_in, v_od_w_out, v_att_q_norm, v_att_k_norm, v_ffn_w_up, v_ffn_conv_w,
                                   v_ffn_conv_b, v_ffn_w_down)))
    _RIDES.clear()
    _IDS[0] = 0
    mx, my, mc = _place()
    chip = 2 * mx + my
    T, D = x.shape[1], x.shape[2]
    depth = ada_w.shape[0]
    ada_cols = ada_w.shape[2]

    buf, offs = _pack([c, gdn_conv_w, ffn_conv_w])
    gathered = all_gather8(buf)
    c_all = _unpack(gathered, offs[0], (D,))
    by_chip = gathered[0::2]
    gdn_conv_full = jnp.concatenate(list(_unpack(by_chip, offs[1], gdn_conv_w.shape)), axis=-1)
    ffn_conv_full = jnp.concatenate(list(_unpack(by_chip, offs[2], ffn_conv_w.shape)), axis=-1)

    bias = lax.dynamic_slice_in_dim(ada_b, chip * ada_cols, ada_cols, axis=1)[:, None, :]
    mod_part = ada_fwd(c_all, ada_w, bias, "ada_fwd")
    mod_all = all_gather8(mod_part)[0::2]
    mod_all = mod_all.transpose(1, 2, 0, 3).reshape(depth, 8, N_CHIPS * ada_cols)
    mod = lax.dynamic_index_in_dim(mod_all, 4 * mx + 2 * my + mc, 1, keepdims=False).reshape(depth, 6, D)

    full_w, big_grad = {}, {}
    order = []
    for i in range(depth):
        mixer = ("ev_w_in", "ev_w_out") if i % 2 == 0 else ("od_w_in", "od_w_out")
        order += [(name, i // 2) for name in mixer] + [("ffn_w_up", i), ("ffn_w_down", i)]
    shards = {name: cast_bf16(local[name].reshape(-1, local[name].shape[-1]), f"cast_{name}")
              .reshape(local[name].shape) for name in COL_SHARDED + ROW_SHARDED}
    for name, e in order:
        submit_weight_gather(full_w, (name, e), shards[name][e], name in COL_SHARDED, mc)

    def big(name, e):
        flush_rides(until=lambda: (name, e) in full_w)
        return full_w[(name, e)]

    def on_grad(name, e, pieces):
        submit_grad_reduce(big_grad, (name, e), pieces, name in COL_SHARDED)

    W = dict(local)
    W["gdn_conv_w"], W["ffn_conv_w"] = gdn_conv_full, ffn_conv_full
    sq, dx, dmod, grads = local_step(x[0], loss_target[0], mod, W, big, on_grad)
    loss = lax.psum(0.5 * jnp.sum(sq) / D, ("x", "y", "c"))

    small = ["norm_mix", "norm_ffn", "gdn_conv_w", "gdn_a_log", "gdn_dt_bias", "gdn_norm", "pool_w",
             "pool_scale", "att_q_norm", "att_k_norm", "ffn_conv_w", "ffn_conv_b"]
    full = {name: jnp.stack([g[name] for g in grads if name in g]) for name in small}
    grad = {}
    buf, offs = _pack([dmod] + [full[name] for name in small])
    gathered = all_gather8(buf)
    summed = sum_slots(gathered, "sum_small_grads")
    grad["ada_b"] = _unpack(summed, offs[0], ada_b.shape)
    for k, name in enumerate(small):
        grad[name] = _unpack(summed, offs[1 + k], full[name].shape)
    for name, cols in (("gdn_conv_w", gdn_conv_w.shape[-1]), ("ffn_conv_w", ffn_conv_w.shape[-1])):
        grad[name] = lax.dynamic_slice_in_dim(grad[name], chip * cols, cols, axis=2)

    dmod_all = _unpack(gathered, offs[0], (depth, N_CHIPS * ada_cols))
    dmod_mine = lax.dynamic_slice_in_dim(dmod_all, chip * ada_cols, ada_cols, axis=2).transpose(1, 0, 2)
    grad["ada_w"] = ada_bwd(jnp.pad(c_all, ((0, 8), (0, 0))), jnp.pad(dmod_mine, ((0, 0), (0, 8), (0, 0))),
                            "ada_bwd")

    deltas, new_m, new_v = {}, {}, {}
    large = COL_SHARDED + ROW_SHARDED
    for name in [n for n in WEIGHTS if n not in large] + list(large):
        if name in large:
            keys = [k for k in order if k[0] == name]
            flush_rides(until=lambda: all(k in big_grad for k in keys))
            grad[name] = jnp.stack([big_grad[k] for k in keys])
        deltas[name], new_m[name], new_v[name] = adamw(local[name], grad[name], moments_m[name],
                                                       moments_v[name], f"adamw_{name}")
    flush_rides()
    return (loss, dx[None], *[grad[n] for n in WEIGHTS], *[deltas[n] for n in WEIGHTS],
            *[new_m[n] for n in WEIGHTS], *[new_v[n] for n in WEIGHTS])
```

```python
import functools
import math

import jax
import jax.numpy as jnp
from jax import lax
from jax.experimental import pallas as pl
from jax.experimental.pallas import tpu as pltpu

F32 = jnp.float32
BF16 = jnp.bfloat16
LANE = 128
SUBLANE = 8
VMEM_LIMIT = 56 * 1024 * 1024
MESH = pl.DeviceIdType.MESH
N_CHIPS = 4

RMS_EPS = 1e-6
GDN_H = 4
HD = 128
GDN_CHUNK = 64
GDN_STEP = 2
GDN_CONV = 4
FFN_CONV = 3
POOL_G = 4
ATT_H = 8
ATT_BLK = 128
DIL = (1, 4, 16)
EVEN_COLS = 2568
EVEN_PAD = 2688
ADAM_LR, ADAM_B1, ADAM_B2, ADAM_EPS, ADAM_WD, ADAM_STEP = 0.001, 0.9, 0.999, 1e-08, 0.01, 10
NEG = -1e30

NN = (((1,), (0,)), ((), ()))
NT = (((1,), (1,)), ((), ()))
TN = (((0,), (0,)), ((), ()))
BNN = (((2,), (1,)), ((0,), (0,)))
BNT = (((2,), (2,)), ((0,), (0,)))
BTN = (((1,), (1,)), ((0,), (0,)))


def _params(n_grid):
    return pltpu.CompilerParams(dimension_semantics=("arbitrary",) * n_grid,
                                vmem_limit_bytes=VMEM_LIMIT)


HBM = pl.BlockSpec(memory_space=pltpu.HBM)
DMA_SEM = pltpu.SemaphoreType.DMA


class Ride:
    def __init__(self, inputs, out_shapes, sems, start, finish, then, mid=None, heavy=False):
        self.inputs, self.out_shapes, self.sems = list(inputs), list(out_shapes), list(sems)
        self.start, self.mid, self.finish, self.then, self.heavy = start, mid, finish, then, heavy


_RIDES = []


def submit_ride(ride):
    _RIDES.append(ride)


def flush_rides(until=None):
    while _RIDES and not (until is not None and until()):
        ride = _RIDES.pop(0)

        def body(*refs, ride=ride):
            a, b = len(ride.inputs), len(ride.inputs) + len(ride.out_shapes)
            ride.start(refs[:a], refs[a:b], refs[b:])
            if ride.mid is not None:
                ride.mid(refs[:a], refs[a:b], refs[b:])
            ride.finish(refs[:a], refs[a:b], refs[b:])

        outs = pl.pallas_call(body, name=f"exchange{_next_id()}", in_specs=[HBM] * len(ride.inputs),
                              out_specs=[HBM] * len(ride.out_shapes), out_shape=ride.out_shapes,
                              scratch_shapes=ride.sems)(*ride.inputs)
        ride.then(list(outs))


_IDS = [0]


def _next_id():
    _IDS[0] += 1
    return _IDS[0]


def _pcall(body, *, name, grid, in_specs, out_specs, out_shape, scratch_shapes=(), compiler_params=None,
           long_call=False):
    del compiler_params
    single = not isinstance(out_shape, (list, tuple))
    outs = [out_shape] if single else list(out_shape)
    ospecs = [out_specs] if single else list(out_specs)
    total = math.prod(grid)
    fits = [k for k, r in enumerate(_RIDES) if long_call or not r.heavy] if total > 1 else []
    ride = _RIDES.pop(fits[0]) if fits else None
    if ride is None:
        call = pl.pallas_call(body, name=name, grid=grid, in_specs=list(in_specs), out_specs=ospecs,
                              out_shape=outs, scratch_shapes=list(scratch_shapes),
                              compiler_params=_params(len(grid)))

        def run_plain(*args):
            res = call(*args)
            return res[0] if single else res
        return run_plain

    n_in, n_out, n_scr = len(in_specs), len(outs), len(scratch_shapes)
    r_in, r_out = len(ride.inputs), len(ride.out_shapes)

    def carrying_body(*refs):
        at = 0
        ins = refs[at:at + n_in]; at += n_in
        r_ins = refs[at:at + r_in]; at += r_in
        os_ = refs[at:at + n_out]; at += n_out
        r_outs = refs[at:at + r_out]; at += r_out
        scr = refs[at:at + n_scr]; at += n_scr
        r_sems = refs[at:]
        step = pl.program_id(0)
        for ax in range(1, len(grid)):
            step = step * grid[ax] + pl.program_id(ax)

        @pl.when(step == 0)
        def _():
            ride.start(r_ins, r_outs, r_sems)

        body(*ins, *os_, *scr)

        if ride.mid is not None:
            @pl.when(step == total // 2)
            def _():
                ride.mid(r_ins, r_outs, r_sems)

        @pl.when(step == total - 1)
        def _():
            ride.finish(r_ins, r_outs, r_sems)

    call = pl.pallas_call(
        carrying_body, name=name, grid=grid, in_specs=list(in_specs) + [HBM] * r_in,
        out_specs=ospecs + [HBM] * r_out, out_shape=outs + ride.out_shapes,
        scratch_shapes=list(scratch_shapes) + ride.sems, compiler_params=_params(len(grid)))

    def run_carrying(*args):
        res = call(*args, *ride.inputs)
        ride.then(list(res[n_out:]))
        return res[0] if single else list(res[:n_out])
    return run_carrying


def _tile(n, target):
    if n <= target:
        return n
    best = None
    for t in range(LANE, target + 1, LANE):
        if n % t == 0:
            best = t
    assert best is not None, (n, target)
    return best


def _rows(n, target):
    if n <= target:
        return n
    best = None
    for t in range(16, target + 1, 16):
        if n % t == 0:
            best = t
    assert best is not None, (n, target)
    return best


def _bdot(a, b, dims):
    return lax.dot_general(a.astype(BF16), b.astype(BF16), dims, preferred_element_type=F32)


def _split(a):
    hi = a.astype(BF16)
    return hi, (a - hi.astype(F32)).astype(BF16)


def _dot3(a, b, dims):
    ah, al = _split(a)
    bh, bl = _split(b)
    d = lambda p, q: lax.dot_general(p, q, dims, preferred_element_type=F32)
    return d(ah, bh) + d(ah, bl) + d(al, bh)


def _sigmoid(x):
    return 1.0 / (1.0 + jnp.exp(-x))


def _silu(x):
    return x * _sigmoid(x)


def matmul(a, b, mode, out_dtype, name, tm=1024, tn=1536, tk=1536, pieces=None):
    if mode == "nn":
        (M, K), (K2, N) = a.shape, b.shape
    elif mode == "nt":
        (M, K), (N, K2) = a.shape, b.shape
    else:
        (K, M), (K2, N) = a.shape, b.shape
    assert K == K2, (a.shape, b.shape, mode)
    if pieces == "col":
        tm, tn = _tile(M // 2, tm), _tile(N // N_CHIPS, tn)
    elif pieces == "row":
        quarter = M // N_CHIPS
        tm = 2 * quarter if (2 * quarter) % LANE == 0 else M
        tn = _tile(N // 2, tn)
    else:
        tm, tn = _tile(M, tm), _tile(N, tn)
    tk = _tile(K, tk)
    nk = K // tk
    dims = {"nn": NN, "nt": NT, "tn": TN}[mode]
    if mode == "tn":
        a_spec = pl.BlockSpec((tk, tm), lambda i, j, k: (k, i))
    else:
        a_spec = pl.BlockSpec((tm, tk), lambda i, j, k: (i, k))
    if mode == "nt":
        b_spec = pl.BlockSpec((tn, tk), lambda i, j, k: (j, k))
    else:
        b_spec = pl.BlockSpec((tk, tn), lambda i, j, k: (k, j))

    out_spec = pl.BlockSpec((tm, tn), lambda i, j, k: (i, j))
    out_shape = jax.ShapeDtypeStruct((M, N), out_dtype)
    rows_per_slot = tm
    if pieces == "col":
        nih, njc = (M // 2) // tm, (N // N_CHIPS) // tn
        out_spec = pl.BlockSpec((None, None, tm, tn), lambda i, j, k: (i // nih, j // njc, i % nih, j % njc))
        out_shape = jax.ShapeDtypeStruct((2, N_CHIPS, M // 2, N // N_CHIPS), out_dtype)
    elif pieces == "row":
        rows_per_slot = M // N_CHIPS
        njh = (N // 2) // tn
        out_spec = pl.BlockSpec((None, tm // rows_per_slot, rows_per_slot, tn),
                                lambda i, j, k: (j // njh, i, 0, j % njh))
        out_shape = jax.ShapeDtypeStruct((2, N_CHIPS, rows_per_slot, N // 2), out_dtype)

    def store(o_ref, acc):
        if pieces == "row":
            for s in range(tm // rows_per_slot):
                o_ref[s] = acc[s * rows_per_slot:(s + 1) * rows_per_slot, :].astype(out_dtype)
        else:
            o_ref[...] = acc.astype(out_dtype)

    def body(a_ref, b_ref, o_ref, acc_ref):
        k = pl.program_id(2)
        p = _bdot(a_ref[...], b_ref[...], dims)
        if nk == 1:
            store(o_ref, p)
            return

        @pl.when(k == 0)
        def _():
            acc_ref[...] = p

        @pl.when(k > 0)
        def _():
            acc_ref[...] += p

        @pl.when(k == nk - 1)
        def _():
            store(o_ref, acc_ref[...])

    return _pcall(
        body, name=name, grid=(M // tm, N // tn, nk),
        in_specs=[a_spec, b_spec], out_specs=out_spec, out_shape=out_shape,
        scratch_shapes=[pltpu.VMEM((tm, tn), F32)],
        long_call=True)(a, b)


def _row_spec(d):
    return pl.BlockSpec((1, d), lambda i: (0, 0))


def modnorm_fwd(x, gain, sc, sh, name):
    T, D = x.shape
    tb = _rows(T, 512)

    def body(x_ref, g_ref, sc_ref, sh_ref, o_ref):
        xv = x_ref[...]
        r = lax.rsqrt(jnp.mean(xv * xv, axis=-1, keepdims=True) + RMS_EPS)
        o_ref[...] = ((xv * r) * g_ref[...] * (1.0 + sc_ref[...]) + sh_ref[...]).astype(BF16)

    blk = pl.BlockSpec((tb, D), lambda i: (i, 0))
    return _pcall(
        body, name=name, grid=(T // tb,),
        in_specs=[blk, _row_spec(D), _row_spec(D), _row_spec(D)],
        out_specs=blk, out_shape=jax.ShapeDtypeStruct((T, D), BF16),
        compiler_params=_params(1))(x, gain, sc, sh)


def modnorm_bwd(x, gain, sc, dh, dres, name):
    T, D = x.shape
    tb = _rows(T, 512)

    def body(x_ref, g_ref, sc_ref, dh_ref, dres_ref, dx_ref, dg_ref, dsc_ref, dsh_ref):
        i = pl.program_id(0)
        xv = x_ref[...]
        r = lax.rsqrt(jnp.mean(xv * xv, axis=-1, keepdims=True) + RMS_EPS)
        n = xv * r
        dhv = dh_ref[...].astype(F32)
        gain_v, sc1 = g_ref[...], 1.0 + sc_ref[...]
        dn = dhv * (gain_v * sc1)
        dx_ref[...] = r * (dn - n * jnp.mean(dn * n, axis=-1, keepdims=True)) + dres_ref[...]
        dhn = dhv * n

        @pl.when(i == 0)
        def _():
            dg_ref[...] = jnp.zeros_like(dg_ref)
            dsc_ref[...] = jnp.zeros_like(dsc_ref)
            dsh_ref[...] = jnp.zeros_like(dsh_ref)

        dg_ref[...] += jnp.sum(dhn * sc1, axis=0, keepdims=True)
        dsc_ref[...] += jnp.sum(dhn * gain_v, axis=0, keepdims=True)
        dsh_ref[...] += jnp.sum(dhv, axis=0, keepdims=True)

    blk = pl.BlockSpec((tb, D), lambda i: (i, 0))
    row = jax.ShapeDtypeStruct((1, D), F32)
    return _pcall(
        body, name=name, grid=(T // tb,),
        in_specs=[blk, _row_spec(D), _row_spec(D), blk, blk],
        out_specs=[blk, _row_spec(D), _row_spec(D), _row_spec(D)],
        out_shape=[jax.ShapeDtypeStruct((T, D), F32), row, row, row],
        compiler_params=_params(1))(x, gain, sc, dh, dres)


def gres_fwd(x, g, y, name):
    T, D = x.shape
    tb = _rows(T, 512)

    def body(x_ref, g_ref, y_ref, o_ref):
        o_ref[...] = x_ref[...] + g_ref[...] * y_ref[...]

    blk = pl.BlockSpec((tb, D), lambda i: (i, 0))
    return _pcall(
        body, name=name, grid=(T // tb,), in_specs=[blk, _row_spec(D), blk], out_specs=blk,
        out_shape=jax.ShapeDtypeStruct((T, D), F32), compiler_params=_params(1))(x, g, y)


def gres_bwd(dx, g, y, name):
    T, D = dx.shape
    tb = _rows(T, 512)

    def body(dx_ref, g_ref, y_ref, dy_ref, dg_ref):
        i = pl.program_id(0)
        dxv = dx_ref[...]
        dy_ref[...] = (dxv * g_ref[...]).astype(BF16)

        @pl.when(i == 0)
        def _():
            dg_ref[...] = jnp.zeros_like(dg_ref)

        dg_ref[...] += jnp.sum(dxv * y_ref[...], axis=0, keepdims=True)

    blk = pl.BlockSpec((tb, D), lambda i: (i, 0))
    return _pcall(
        body, name=name, grid=(T // tb,), in_specs=[blk, _row_spec(D), blk],
        out_specs=[blk, _row_spec(D)],
        out_shape=[jax.ShapeDtypeStruct((T, D), BF16), jax.ShapeDtypeStruct((1, D), F32)],
        compiler_params=_params(1))(dx, g, y)


def res_norm_fwd(x, g, y, gain, sc, sh, name):
    T, D = x.shape
    tb = _rows(T, 512)

    def body(x_ref, g_ref, y_ref, gn_ref, sc_ref, sh_ref, x1_ref, h_ref):
        xv = x_ref[...] + g_ref[...] * y_ref[...]
        x1_ref[...] = xv
        r = lax.rsqrt(jnp.mean(xv * xv, axis=-1, keepdims=True) + RMS_EPS)
        h_ref[...] = ((xv * r) * gn_ref[...] * (1.0 + sc_ref[...]) + sh_ref[...]).astype(BF16)

    blk = pl.BlockSpec((tb, D), lambda i: (i, 0))
    row = _row_spec(D)
    return _pcall(
        body, name=name, grid=(T // tb,), in_specs=[blk, row, blk, row, row, row], out_specs=[blk, blk],
        out_shape=[jax.ShapeDtypeStruct((T, D), F32), jax.ShapeDtypeStruct((T, D), BF16)],
        compiler_params=_params(1))(x, g, y, gain, sc, sh)


def norm_res_bwd(x, gain, sc, dh, dres, g, y, name):
    T, D = x.shape
    tb = _rows(T, 512)

    def body(x_ref, gn_ref, sc_ref, dh_ref, dres_ref, g_ref, y_ref,
             dx_ref, dgn_ref, dsc_ref, dsh_ref, dy_ref, dg_ref):
        i = pl.program_id(0)
        xv = x_ref[...]
        r = lax.rsqrt(jnp.mean(xv * xv, axis=-1, keepdims=True) + RMS_EPS)
        n = xv * r
        dhv = dh_ref[...].astype(F32)
        gain_v, sc1 = gn_ref[...], 1.0 + sc_ref[...]
        dn = dhv * (gain_v * sc1)
        dx = r * (dn - n * jnp.mean(dn * n, axis=-1, keepdims=True)) + dres_ref[...]
        dx_ref[...] = dx
        dy_ref[...] = (dx * g_ref[...]).astype(BF16)
        dhn = dhv * n

        @pl.when(i == 0)
        def _():
            dgn_ref[...] = jnp.zeros_like(dgn_ref)
            dsc_ref[...] = jnp.zeros_like(dsc_ref)
            dsh_ref[...] = jnp.zeros_like(dsh_ref)
            dg_ref[...] = jnp.zeros_like(dg_ref)

        dgn_ref[...] += jnp.sum(dhn * sc1, axis=0, keepdims=True)
        dsc_ref[...] += jnp.sum(dhn * gain_v, axis=0, keepdims=True)
        dsh_ref[...] += jnp.sum(dhv, axis=0, keepdims=True)
        dg_ref[...] += jnp.sum(dx * y_ref[...], axis=0, keepdims=True)

    blk = pl.BlockSpec((tb, D), lambda i: (i, 0))
    row = _row_spec(D)
    row_shape = jax.ShapeDtypeStruct((1, D), F32)
    return _pcall(
        body, name=name, grid=(T // tb,),
        in_specs=[blk, row, row, blk, blk, row, blk],
        out_specs=[blk, row, row, row, blk, row],
        out_shape=[jax.ShapeDtypeStruct((T, D), F32), row_shape, row_shape, row_shape,
                   jax.ShapeDtypeStruct((T, D), BF16), row_shape],
        compiler_params=_params(1))(x, gain, sc, dh, dres, g, y)


def loss_head(y, target, name):
    T, D = y.shape
    tb = _rows(T, 512)

    def body(y_ref, t_ref, l_ref, dy_ref):
        i = pl.program_id(0)
        err = y_ref[...] - t_ref[...]
        dy_ref[...] = err * (1.0 / D)

        @pl.when(i == 0)
        def _():
            l_ref[...] = jnp.zeros_like(l_ref)

        sq = jnp.sum(err * err, axis=0, keepdims=True)
        tot = sq[:, 0:LANE]
        for k in range(1, D // LANE):
            tot = tot + sq[:, k * LANE:(k + 1) * LANE]
        l_ref[...] += tot

    blk = pl.BlockSpec((tb, D), lambda i: (i, 0))
    return _pcall(
        body, name=name, grid=(T // tb,), in_specs=[blk, blk],
        out_specs=[_row_spec(LANE), blk],
        out_shape=[jax.ShapeDtypeStruct((1, LANE), F32), jax.ShapeDtypeStruct((T, D), F32)],
        compiler_params=_params(1))(y, target)


def _back(ext, s):
    return ext if s == 0 else pltpu.roll(ext, s, 0)


def _ahead(ext, s):
    return ext if s == 0 else pltpu.roll(ext, ext.shape[0] - s, 0)


def _halo_prev(tb, h):
    return lambda i, j: (jnp.maximum(i * (tb // h) - 1, 0), j)


def _halo_next(tb, h, nrb):
    return lambda i, j: (jnp.minimum(i + 1, nrb - 1) * (tb // h), j)


FFN_TB, FFN_CB = 256, 1408
HALO16 = 16


def ffn_mid_fwd(up, conv_w8, conv_b, name):
    T, F2 = up.shape
    Fd = F2 // 2
    tb, cb = _rows(T, FFN_TB), _tile(Fd, FFN_CB)
    ncb = Fd // cb
    H = HALO16

    def body(g_ref, gp_ref, v_ref, w_ref, b_ref, o_ref):
        i = pl.program_id(0)
        g = g_ref[...].astype(F32)
        prev = jnp.where(i > 0, gp_ref[...].astype(F32), 0.0)
        ext = jnp.concatenate([prev, g], axis=0)
        w = w_ref[...]
        gc = w[2:3] * g + w[1:2] * _back(ext, 1)[H:] + w[0:1] * _back(ext, 2)[H:] + b_ref[...]
        o_ref[...] = (_silu(gc) * v_ref[...].astype(F32)).astype(BF16)

    return _pcall(
        body, name=name, grid=(T // tb, ncb),
        in_specs=[pl.BlockSpec((tb, cb), lambda i, j: (i, j)),
                  pl.BlockSpec((H, cb), _halo_prev(tb, H)),
                  pl.BlockSpec((tb, cb), lambda i, j: (i, j + ncb)),
                  pl.BlockSpec((SUBLANE, cb), lambda i, j: (0, j)),
                  pl.BlockSpec((1, cb), lambda i, j: (0, j))],
        out_specs=pl.BlockSpec((tb, cb), lambda i, j: (i, j)),
        out_shape=jax.ShapeDtypeStruct((T, Fd), BF16),
        long_call=True)(up, up, up, conv_w8, conv_b)


def ffn_mid_bwd(up, conv_w8, conv_b, dact, name):
    T, F2 = up.shape
    Fd = F2 // 2
    tb, cb = _rows(T, FFN_TB), _tile(Fd, FFN_CB)
    ncb, nrb = Fd // cb, T // tb
    H = HALO16

    def body(g_ref, gp_ref, gn_ref, v_ref, vn_ref, d_ref, dn_ref, w_ref, b_ref,
             dg_ref, dv_ref, dw_ref, db_ref):
        i = pl.program_id(1)
        g = g_ref[...].astype(F32)
        prev = jnp.where(i > 0, gp_ref[...].astype(F32), 0.0)
        ext = jnp.concatenate([prev, g, gn_ref[...].astype(F32)], axis=0)
        w = w_ref[...]
        e1, e2 = _back(ext, 1), _back(ext, 2)
        gc = (w[2:3] * ext + w[1:2] * e1 + w[0:1] * e2 + b_ref[...])[H:]
        val = jnp.concatenate([v_ref[...], vn_ref[...]], axis=0).astype(F32)
        dnext = jnp.where(i < nrb - 1, dn_ref[...].astype(F32), 0.0)
        da = jnp.concatenate([d_ref[...].astype(F32), dnext], axis=0)
        sg = _sigmoid(gc)
        dv_ref[...] = (da * gc * sg)[:tb].astype(BF16)
        dgc = da * val * (sg * (1.0 + gc * (1.0 - sg)))
        dg_ref[...] = (w[2:3] * dgc + w[1:2] * _ahead(dgc, 1) + w[0:1] * _ahead(dgc, 2))[:tb].astype(BF16)
        dc = dgc[:tb]

        @pl.when(i == 0)
        def _():
            dw_ref[...] = jnp.zeros_like(dw_ref)
            db_ref[...] = jnp.zeros_like(db_ref)

        dw_ref[2:3, :] += jnp.sum(dc * g, axis=0, keepdims=True)
        dw_ref[1:2, :] += jnp.sum(dc * e1[H:H + tb], axis=0, keepdims=True)
        dw_ref[0:1, :] += jnp.sum(dc * e2[H:H + tb], axis=0, keepdims=True)
        db_ref[...] += jnp.sum(dc, axis=0, keepdims=True)

    cur = lambda j, i: (i, j)
    prv = lambda j, i: _halo_prev(tb, H)(i, j)
    nxt = lambda j, i: _halo_next(tb, H, nrb)(i, j)
    return _pcall(
        body, name=name, grid=(ncb, nrb),
        in_specs=[pl.BlockSpec((tb, cb), cur), pl.BlockSpec((H, cb), prv), pl.BlockSpec((H, cb), nxt),
                  pl.BlockSpec((tb, cb), lambda j, i: (i, j + ncb)),
                  pl.BlockSpec((H, cb), lambda j, i: (jnp.minimum(i + 1, nrb - 1) * (tb // H), j + ncb)),
                  pl.BlockSpec((tb, cb), cur), pl.BlockSpec((H, cb), nxt),
                  pl.BlockSpec((SUBLANE, cb), lambda j, i: (0, j)),
                  pl.BlockSpec((1, cb), lambda j, i: (0, j))],
        out_specs=[pl.BlockSpec((tb, cb), cur), pl.BlockSpec((tb, cb), cur),
                   pl.BlockSpec((SUBLANE, cb), lambda j, i: (0, j)),
                   pl.BlockSpec((1, cb), lambda j, i: (0, j))],
        out_shape=[jax.ShapeDtypeStruct((T, Fd), BF16), jax.ShapeDtypeStruct((T, Fd), BF16),
                   jax.ShapeDtypeStruct((SUBLANE, Fd), F32), jax.ShapeDtypeStruct((1, Fd), F32)],
        long_call=True)(up, up, up, up, up, dact, dact, conv_w8, conv_b)


GDN_W = GDN_H * HD


def _head_l2norm(a, apply):
    parts = []
    for h in range(GDN_H):
        ah = a[:, h * HD:(h + 1) * HD]
        parts.append(ah * lax.rsqrt(jnp.sum(ah * ah, axis=-1, keepdims=True) + RMS_EPS))
    return jnp.where(apply, jnp.concatenate(parts, axis=1), a)


def _head_l2norm_bwd(a, dy, apply):
    parts = []
    for h in range(GDN_H):
        sl = slice(h * HD, (h + 1) * HD)
        ah, dh = a[:, sl], dy[:, sl]
        r = lax.rsqrt(jnp.sum(ah * ah, axis=-1, keepdims=True) + RMS_EPS)
        y = ah * r
        parts.append(r * (dh - y * jnp.sum(dh * y, axis=-1, keepdims=True)))
    return jnp.where(apply, jnp.concatenate(parts, axis=1), dy)


def gdn_conv_fwd(proj, w8, name):
    T = proj.shape[0]
    tb = _rows(T, 512)
    H = SUBLANE

    def body(x_ref, xp_ref, w_ref, o_ref):
        i, j = pl.program_id(0), pl.program_id(1)
        x = x_ref[...]
        prev = jnp.where(i > 0, xp_ref[...], 0.0)
        ext = jnp.concatenate([prev, x], axis=0)
        w = w_ref[...]
        c = (w[3:4] * x + w[2:3] * _back(ext, 1)[H:] + w[1:2] * _back(ext, 2)[H:]
             + w[0:1] * _back(ext, 3)[H:])
        o_ref[...] = _head_l2norm(_silu(c), j < 2)

    return _pcall(
        body, name=name, grid=(T // tb, 3),
        in_specs=[pl.BlockSpec((tb, GDN_W), lambda i, j: (i, j)),
                  pl.BlockSpec((H, GDN_W), _halo_prev(tb, H)),
                  pl.BlockSpec((SUBLANE, GDN_W), lambda i, j: (0, j))],
        out_specs=pl.BlockSpec((tb, GDN_W), lambda i, j: (i, j)),
        out_shape=jax.ShapeDtypeStruct((T, 3 * GDN_W), F32),
        compiler_params=_params(2))(proj, proj, w8)


def gdn_conv_bwd(proj, w8, dout, name):
    T = proj.shape[0]
    tb = _rows(T, 512)
    nrb = T // tb
    H = SUBLANE

    def body(x_ref, xp_ref, xn_ref, d_ref, dn_ref, w_ref, dx_ref, dw_ref):
        j, i = pl.program_id(0), pl.program_id(1)
        x = x_ref[...]
        prev = jnp.where(i > 0, xp_ref[...], 0.0)
        ext = jnp.concatenate([prev, x, xn_ref[...]], axis=0)
        w = w_ref[...]
        e1, e2, e3 = _back(ext, 1), _back(ext, 2), _back(ext, 3)
        c = (w[3:4] * ext + w[2:3] * e1 + w[1:2] * e2 + w[0:1] * e3)[H:]
        sg = _sigmoid(c)
        dnext = jnp.where(i < nrb - 1, dn_ref[...], 0.0)
        do = jnp.concatenate([d_ref[...], dnext], axis=0)
        da = _head_l2norm_bwd(c * sg, do, j < 2)
        dc = da * (sg * (1.0 + c * (1.0 - sg)))
        dx_ref[...] = (w[3:4] * dc + w[2:3] * _ahead(dc, 1) + w[1:2] * _ahead(dc, 2)
                       + w[0:1] * _ahead(dc, 3))[:tb].astype(BF16)
        dcc = dc[:tb]

        @pl.when(i == 0)
        def _():
            dw_ref[...] = jnp.zeros_like(dw_ref)

        dw_ref[3:4, :] += jnp.sum(dcc * x, axis=0, keepdims=True)
        dw_ref[2:3, :] += jnp.sum(dcc * e1[H:H + tb], axis=0, keepdims=True)
        dw_ref[1:2, :] += jnp.sum(dcc * e2[H:H + tb], axis=0, keepdims=True)
        dw_ref[0:1, :] += jnp.sum(dcc * e3[H:H + tb], axis=0, keepdims=True)

    cur = lambda j, i: (i, j)
    prv = lambda j, i: _halo_prev(tb, H)(i, j)
    nxt = lambda j, i: _halo_next(tb, H, nrb)(i, j)
    return _pcall(
        body, name=name, grid=(3, nrb),
        in_specs=[pl.BlockSpec((tb, GDN_W), cur), pl.BlockSpec((H, GDN_W), prv), pl.BlockSpec((H, GDN_W), nxt),
                  pl.BlockSpec((tb, GDN_W), cur), pl.BlockSpec((H, GDN_W), nxt),
                  pl.BlockSpec((SUBLANE, GDN_W), lambda j, i: (0, j))],
        out_specs=[pl.BlockSpec((tb, GDN_W), cur), pl.BlockSpec((SUBLANE, GDN_W), lambda j, i: (0, j))],
        out_shape=[jax.ShapeDtypeStruct((T, 3 * GDN_W), BF16),
                   jax.ShapeDtypeStruct((SUBLANE, 3 * GDN_W), F32)],
        compiler_params=_params(2))(proj, proj, proj, dout, dout, w8)


def _dot_family(dot, diff):
    if not diff:
        return tuple(functools.partial(lambda d, a, b: dot(a, b, d), d) for d in (BNN, BNT, BTN))

    @jax.custom_vjp
    def nn(a, b):
        return dot(a, b, BNN)
    nn.defvjp(lambda a, b: (dot(a, b, BNN), (a, b)),
              lambda res, g: (dot(g, res[1], BNT), dot(res[0], g, BTN)))

    @jax.custom_vjp
    def nt(a, b):
        return dot(a, b, BNT)
    nt.defvjp(lambda a, b: (dot(a, b, BNT), (a, b)),
              lambda res, g: (dot(g, res[1], BNN), dot(g, res[0], BTN)))

    @jax.custom_vjp
    def tn(a, b):
        return dot(a, b, BTN)
    tn.defvjp(lambda a, b: (dot(a, b, BTN), (a, b)),
              lambda res, g: (dot(res[1], g, BNT), dot(res[0], g, BNN)))
    return nn, nt, tn


def _saved_inverse(hdots):
    _, hnt, htn = hdots

    @jax.custom_vjp
    def inv(L, P):
        return P

    inv.defvjp(lambda L, P: (P, P), lambda P, g: (-hnt(htn(P, g), P), jnp.zeros_like(P)))
    return inv


def _gdn_step(dots, hdots, S, q, k, v, z, b_raw, a_raw, alog, dtb, gnorm, P_saved=None, return_P=False):
    nn, nt, tn = dots
    hnn = hdots[0]
    B, C = q.shape[0], GDN_CHUNK
    ii = lax.broadcasted_iota(jnp.int32, (B, C, C), 1)
    jj = lax.broadcasted_iota(jnp.int32, (B, C, C), 2)
    causal, strict = ii >= jj, ii > jj
    tri, tri_t = causal.astype(F32), (ii <= jj).astype(F32)
    eye, ones = (ii == jj).astype(F32), jnp.ones((B, C, C), F32)

    beta = _sigmoid(b_raw)
    xs = a_raw + dtb
    pos = xs > 0.0
    softplus = jnp.where(pos, xs, 0.0) + jnp.log(1.0 + jnp.exp(jnp.where(pos, -xs, xs)))
    g = -jnp.exp(alog) * softplus
    gb = jnp.broadcast_to(g, (B, C, C))
    gc_c = hnn(tri, gb)
    gc_r = hnn(hnn(ones, eye * gb), tri_t)
    gc = hnn(tri, jnp.broadcast_to(g, (B, C, HD)))
    gl = jnp.sum(g, axis=1, keepdims=True)
    decay = jnp.where(causal, jnp.exp(jnp.where(causal, gc_c - gc_r, 0.0)), 0.0)
    q = q * (HD ** -0.5)
    kb = k * beta
    L = jnp.where(strict, nt(kb, k) * decay, 0.0)
    egc = jnp.exp(gc)
    if P_saved is None:
        P = eye - L
        M = hnn(L, L)
        for step in range(5):
            P = P + hnn(P, M)
            if step < 4:
                M = hnn(M, M)
    else:
        P = _saved_inverse(hdots)(L, P_saved)
    u = hnn(P, v * beta)
    w = hnn(P, kb * egc)
    intra = jnp.where(causal, nt(q, k) * decay, 0.0)
    qg = q * egc
    kdec = k * jnp.exp(gl - gc)
    egl = jnp.exp(gl)
    outs = []
    for ci in range(B // GDN_H):
        sl = slice(ci * GDN_H, (ci + 1) * GDN_H)
        v_new = u[sl] - nn(w[sl], S)
        outs.append(nn(qg[sl], S) + nn(intra[sl], v_new))
        S = S * egl[sl] + tn(kdec[sl], v_new)
    o = jnp.concatenate(outs, axis=0)
    r = lax.rsqrt(jnp.mean(o * o, axis=-1, keepdims=True) + RMS_EPS)
    out = o * r * gnorm * _silu(z)
    return (out, S, P) if return_P else (out, S)


def _gdn_batches(qkv, ba, z, alog_row, dt_row):
    C = GDN_CHUNK
    q, k, v, zz, b_raw, a_raw, alog, dtb = ([] for _ in range(8))
    for ci in range(GDN_STEP):
        rows = slice(ci * C, (ci + 1) * C)
        for h in range(GDN_H):
            q.append(qkv[rows, h * HD:(h + 1) * HD])
            k.append(qkv[rows, GDN_W + h * HD:GDN_W + (h + 1) * HD])
            v.append(qkv[rows, 2 * GDN_W + h * HD:2 * GDN_W + (h + 1) * HD])
            zz.append(z[rows, h * HD:(h + 1) * HD])
            b_raw.append(ba[rows, h:h + 1])
            a_raw.append(ba[rows, GDN_H + h:GDN_H + h + 1])
            alog.append(alog_row[:, h:h + 1])
            dtb.append(dt_row[:, h:h + 1])
    return tuple(jnp.stack(t) for t in (q, k, v, zz, b_raw, a_raw, alog, dtb))


def gdn_chunk_fwd(qkv, proj, alog_row, dt_row, gnorm, name):
    T = qkv.shape[0]
    R = GDN_CHUNK * GDN_STEP
    N = T // R
    B = GDN_STEP * GDN_H
    dots, hdots = _dot_family(_bdot, False), _dot_family(_dot3, False)

    def body(qkv_ref, ba_ref, z_ref, al_ref, dt_ref, gn_ref, o_ref, save_ref, inv_ref, S_ref):
        n = pl.program_id(0)

        @pl.when(n == 0)
        def _():
            S_ref[...] = jnp.zeros_like(S_ref)

        S = S_ref[...]
        save_ref[0] = S
        batches = _gdn_batches(qkv_ref[...], ba_ref[...], z_ref[...], al_ref[...], dt_ref[...])
        o, S_new, P = _gdn_step(dots, hdots, S, *batches, gn_ref[...], return_P=True)
        S_ref[...] = S_new
        inv_ref[0] = P
        for ci in range(GDN_STEP):
            for h in range(GDN_H):
                o_ref[ci * GDN_CHUNK:(ci + 1) * GDN_CHUNK, h * HD:(h + 1) * HD] = o[ci * GDN_H + h].astype(BF16)

    return _pcall(
        body, name=name, grid=(N,),
        in_specs=[pl.BlockSpec((R, 3 * GDN_W), lambda n: (n, 0)),
                  pl.BlockSpec((R, LANE), lambda n: (n, (4 * GDN_W + POOL_G * HD) // LANE)),
                  pl.BlockSpec((R, GDN_W), lambda n: (n, 3)),
                  _row_spec(LANE), _row_spec(LANE), _row_spec(HD)],
        out_specs=[pl.BlockSpec((R, GDN_W), lambda n: (n, 0)),
                   pl.BlockSpec((1, GDN_H, HD, HD), lambda n: (n, 0, 0, 0)),
                   pl.BlockSpec((1, B, GDN_CHUNK, GDN_CHUNK), lambda n: (n, 0, 0, 0))],
        out_shape=[jax.ShapeDtypeStruct((T, GDN_W), BF16), jax.ShapeDtypeStruct((N, GDN_H, HD, HD), F32),
                   jax.ShapeDtypeStruct((N, B, GDN_CHUNK, GDN_CHUNK), F32)],
        scratch_shapes=[pltpu.VMEM((GDN_H, HD, HD), F32)],
        long_call=True)(qkv, proj, proj, alog_row, dt_row, gnorm)


def gdn_chunk_bwd(qkv, proj, alog_row, dt_row, gnorm, saved, inverses, docat, name):
    T = qkv.shape[0]
    C = GDN_CHUNK
    R = C * GDN_STEP
    N = T // R
    B = GDN_STEP * GDN_H
    dots, hdots = _dot_family(_bdot, True), _dot_family(_dot3, True)

    def body(qkv_ref, ba_ref, z_ref, al_ref, dt_ref, gn_ref, save_ref, inv_ref, do_ref,
             dqkv_ref, dz_ref, dba_ref, dal_ref, ddt_ref, dgn_ref, dS_ref):
        n = pl.program_id(0)

        @pl.when(n == 0)
        def _():
            dS_ref[...] = jnp.zeros_like(dS_ref)
            dal_ref[...] = jnp.zeros_like(dal_ref)
            ddt_ref[...] = jnp.zeros_like(ddt_ref)
            dgn_ref[...] = jnp.zeros_like(dgn_ref)

        batches = _gdn_batches(qkv_ref[...], ba_ref[...], z_ref[...], al_ref[...], dt_ref[...])
        do = do_ref[...]
        do_b = jnp.stack([do[ci * C:(ci + 1) * C, h * HD:(h + 1) * HD]
                          for ci in range(GDN_STEP) for h in range(GDN_H)])
        P = inv_ref[0]
        fn = lambda *args: _gdn_step(dots, hdots, *args, P_saved=P)
        _, vjp = jax.vjp(fn, save_ref[0], *batches, gn_ref[...])
        dS, dq, dk, dv, dz, db_raw, da_raw, dalog, ddtb, dgn = vjp((do_b, dS_ref[...]))
        dS_ref[...] = dS
        lane = lax.broadcasted_iota(jnp.int32, (1, LANE), 1)
        dal = jnp.zeros((1, LANE), F32)
        ddt = jnp.zeros((1, LANE), F32)
        for ci in range(GDN_STEP):
            rows = slice(ci * C, (ci + 1) * C)
            dba = jnp.zeros((C, LANE), F32)
            for h in range(GDN_H):
                b = ci * GDN_H + h
                dqkv_ref[rows, h * HD:(h + 1) * HD] = dq[b]
                dqkv_ref[rows, GDN_W + h * HD:GDN_W + (h + 1) * HD] = dk[b]
                dqkv_ref[rows, 2 * GDN_W + h * HD:2 * GDN_W + (h + 1) * HD] = dv[b]
                dz_ref[rows, h * HD:(h + 1) * HD] = dz[b].astype(BF16)
                hot_b = (lane == h).astype(F32)
                dba = dba + db_raw[b] * hot_b + da_raw[b] * (lane == GDN_H + h).astype(F32)
                dal = dal + dalog[b] * hot_b
                ddt = ddt + ddtb[b] * hot_b
            dba_ref[rows, :] = dba.astype(BF16)
        dal_ref[...] += dal
        ddt_ref[...] += ddt
        dgn_ref[...] += dgn

    rev = lambda n: N - 1 - n
    row = jax.ShapeDtypeStruct((1, LANE), F32)
    return _pcall(
        body, name=name, grid=(N,),
        in_specs=[pl.BlockSpec((R, 3 * GDN_W), lambda n: (rev(n), 0)),
                  pl.BlockSpec((R, LANE), lambda n: (rev(n), (4 * GDN_W + POOL_G * HD) // LANE)),
                  pl.BlockSpec((R, GDN_W), lambda n: (rev(n), 3)),
                  _row_spec(LANE), _row_spec(LANE), _row_spec(HD),
                  pl.BlockSpec((1, GDN_H, HD, HD), lambda n: (rev(n), 0, 0, 0)),
                  pl.BlockSpec((1, B, C, C), lambda n: (rev(n), 0, 0, 0)),
                  pl.BlockSpec((R, GDN_W), lambda n: (rev(n), 0))],
        out_specs=[pl.BlockSpec((R, 3 * GDN_W), lambda n: (rev(n), 0)),
                   pl.BlockSpec((R, GDN_W), lambda n: (rev(n), 0)),
                   pl.BlockSpec((R, LANE), lambda n: (rev(n), 0)),
                   _row_spec(LANE), _row_spec(LANE), _row_spec(HD)],
        out_shape=[jax.ShapeDtypeStruct((T, 3 * GDN_W), F32), jax.ShapeDtypeStruct((T, GDN_W), BF16),
                   jax.ShapeDtypeStruct((T, LANE), BF16), row, row, jax.ShapeDtypeStruct((1, HD), F32)],
        scratch_shapes=[pltpu.VMEM((GDN_H, HD, HD), F32)],
        long_call=True)(qkv, proj, proj, alog_row, dt_row, gnorm, saved, inverses, docat)


POOL_HALO = 16


def _pool_pick(j, s2, s4, s8, s16):
    return jnp.where(j == 0, s2, jnp.where(j == 1, s4, jnp.where(j == 2, s8, s16)))


def _pool_count(j, t0, rows):
    t1 = (t0 + 1 + lax.broadcasted_iota(jnp.int32, (rows, 1), 0)).astype(F32)
    win = jnp.where(j == 0, 2.0, jnp.where(j == 1, 4.0, jnp.where(j == 2, 8.0, 16.0)))
    return jnp.minimum(t1, win)


def _pooled(p, prev, i, j, tb):
    ext = jnp.concatenate([prev, p], axis=0)
    s2 = ext + _back(ext, 1)
    s4 = s2 + _back(s2, 2)
    s8 = s4 + _back(s4, 4)
    s16 = s8 + _back(s8, 8)
    s = _pool_pick(j, s2, s4, s8, s16)[POOL_HALO:]
    return s / _pool_count(j, i * tb, tb) - p


def pool_fwd(proj, pool_w, pool_scale, name):
    T = proj.shape[0]
    tb = _rows(T, 512)
    c0 = 4 * GDN_H

    def body(p_ref, pp_ref, w_ref, s_ref, o_ref):
        i, j = pl.program_id(0), pl.program_id(1)
        p = p_ref[...]
        prev = jnp.where(i > 0, pp_ref[...], 0.0)
        pooled = _pooled(p, prev, i, j, tb)
        o_ref[...] = (_bdot(pooled, w_ref[0], NN) * s_ref[...]).astype(BF16)

    return _pcall(
        body, name=name, grid=(T // tb, POOL_G),
        in_specs=[pl.BlockSpec((tb, HD), lambda i, j: (i, c0 + j)),
                  pl.BlockSpec((POOL_HALO, HD), lambda i, j: (jnp.maximum(i * (tb // POOL_HALO) - 1, 0), c0 + j)),
                  pl.BlockSpec((1, HD, HD), lambda i, j: (j, 0, 0)),
                  pl.BlockSpec((1, HD), lambda i, j: (0, j))],
        out_specs=pl.BlockSpec((tb, HD), lambda i, j: (i, j)),
        out_shape=jax.ShapeDtypeStruct((T, POOL_G * HD), BF16),
        compiler_params=_params(2))(proj, proj, pool_w, pool_scale)


def pool_bwd(proj, pool_w, pool_scale, docat, name):
    T = proj.shape[0]
    tb = _rows(T, 512)
    nrb = T // tb
    c0 = 4 * GDN_H
    HB = POOL_HALO

    def body(p_ref, pp_ref, w_ref, s_ref, d_ref, dn_ref, dp_ref, dw_ref, ds_ref):
        j, i = pl.program_id(0), pl.program_id(1)
        p = p_ref[...]
        prev = jnp.where(i > 0, pp_ref[...], 0.0)
        pooled = _pooled(p, prev, i, j, tb)
        w, scale = w_ref[0], s_ref[...]
        dy = d_ref[...]
        dnext = jnp.where(i < nrb - 1, dn_ref[...], 0.0)
        dyp = jnp.concatenate([dy, dnext], axis=0) * scale
        dpooled = _bdot(dyp, w, NT)
        qn = dpooled / _pool_count(j, i * tb, tb + HB)
        a2 = qn + _ahead(qn, 1)
        a4 = a2 + _ahead(a2, 2)
        a8 = a4 + _ahead(a4, 4)
        a16 = a8 + _ahead(a8, 8)
        dp_ref[...] = (_pool_pick(j, a2, a4, a8, a16) - dpooled)[:tb].astype(BF16)

        @pl.when(i == 0)
        def _():
            dw_ref[...] = jnp.zeros_like(dw_ref)
            ds_ref[...] = jnp.zeros_like(ds_ref)

        dw_ref[0] += _bdot(pooled, dyp[:tb], TN)
        ds_ref[...] += jnp.sum(dy * _bdot(pooled, w, NN), axis=0, keepdims=True)

    return _pcall(
        body, name=name, grid=(POOL_G, nrb),
        in_specs=[pl.BlockSpec((tb, HD), lambda j, i: (i, c0 + j)),
                  pl.BlockSpec((HB, HD), lambda j, i: (jnp.maximum(i * (tb // HB) - 1, 0), c0 + j)),
                  pl.BlockSpec((1, HD, HD), lambda j, i: (j, 0, 0)),
                  pl.BlockSpec((1, HD), lambda j, i: (0, j)),
                  pl.BlockSpec((tb, HD), lambda j, i: (i, POOL_G + j)),
                  pl.BlockSpec((HB, HD), lambda j, i: (jnp.minimum(i + 1, nrb - 1) * (tb // HB), POOL_G + j))],
        out_specs=[pl.BlockSpec((tb, HD), lambda j, i: (i, j)),
                   pl.BlockSpec((1, HD, HD), lambda j, i: (j, 0, 0)),
                   pl.BlockSpec((1, HD), lambda j, i: (0, j))],
        out_shape=[jax.ShapeDtypeStruct((T, POOL_G * HD), BF16),
                   jax.ShapeDtypeStruct((POOL_G, HD, HD), F32),
                   jax.ShapeDtypeStruct((1, POOL_G * HD), F32)],
        compiler_params=_params(2))(proj, proj, pool_w, pool_scale, docat, docat)


ATT_W = ATT_H * HD
GROUP_COLS = 3 * ATT_W


def to_residue_major(t, d):
    if d == 1:
        return t
    T, C = t.shape
    return t.reshape(T // d, d, C).transpose(1, 0, 2).reshape(T, C)


def to_token_order(t, d):
    if d == 1:
        return t
    T, C = t.shape
    return t.reshape(d, T // d, C).transpose(1, 0, 2).reshape(T, C)


def headnorm_fwd(proj, qk_gain, name):
    T = proj.shape[0]
    tb = _rows(T, 256)

    def body(x_ref, g_ref, o_ref):
        g = g_ref[...]
        for h in range(2 * ATT_H):
            sl = slice(h * HD, (h + 1) * HD)
            x = x_ref[:, sl].astype(F32)
            n = x * lax.rsqrt(jnp.mean(x * x, axis=-1, keepdims=True) + RMS_EPS)
            gain = g[0:1] * (HD ** -0.5) if h < ATT_H else g[1:2]
            o_ref[:, sl] = (n * gain).astype(BF16)
        o_ref[:, 2 * ATT_W:] = x_ref[:, 2 * ATT_W:]

    blk = pl.BlockSpec((tb, GROUP_COLS), lambda i: (i, 0))
    return _pcall(
        body, name=name, grid=(T // tb,),
        in_specs=[blk, pl.BlockSpec((SUBLANE, HD), lambda i: (0, 0))],
        out_specs=blk, out_shape=jax.ShapeDtypeStruct((T, GROUP_COLS), BF16),
        long_call=True)(proj, qk_gain)


def headnorm_bwd(proj, qk_gain, dq, dk, dv, name):
    T = proj.shape[0]
    tb = _rows(T, 256)

    def body(x_ref, g_ref, dq_ref, dk_ref, dv_ref, dx_ref, dg_ref):
        i = pl.program_id(0)
        g = g_ref[...]

        @pl.when(i == 0)
        def _():
            dg_ref[...] = jnp.zeros_like(dg_ref)

        for part, d_ref in enumerate((dq_ref, dk_ref)):
            gain = g[0:1] * (HD ** -0.5) if part == 0 else g[1:2]
            scale = (HD ** -0.5) if part == 0 else 1.0
            acc = jnp.zeros((1, HD), F32)
            for h in range(ATT_H):
                x = x_ref[:, part * ATT_W + h * HD:part * ATT_W + (h + 1) * HD].astype(F32)
                d = d_ref[:, h * HD:(h + 1) * HD].astype(F32)
                r = lax.rsqrt(jnp.mean(x * x, axis=-1, keepdims=True) + RMS_EPS)
                n = x * r
                dn = d * gain
                dx = r * (dn - n * jnp.mean(dn * n, axis=-1, keepdims=True))
                dx_ref[:, part * ATT_W + h * HD:part * ATT_W + (h + 1) * HD] = dx.astype(BF16)
                acc = acc + jnp.sum(d * n, axis=0, keepdims=True)
            dg_ref[part:part + 1, :] += acc * scale
        dx_ref[:, 2 * ATT_W:] = dv_ref[...]

    blk = pl.BlockSpec((tb, GROUP_COLS), lambda i: (i, 0))
    dblk = pl.BlockSpec((tb, ATT_W), lambda i: (i, 0))
    gspec = pl.BlockSpec((SUBLANE, HD), lambda i: (0, 0))
    return _pcall(
        body, name=name, grid=(T // tb,),
        in_specs=[blk, gspec, dblk, dblk, dblk],
        out_specs=[blk, gspec],
        out_shape=[jax.ShapeDtypeStruct((T, GROUP_COLS), BF16), jax.ShapeDtypeStruct((SUBLANE, HD), F32)],
        long_call=True)(proj, qk_gain, dq, dk, dv)


def _heads(ref):
    return jnp.stack([ref[:, h * HD:(h + 1) * HD] for h in range(ATT_H)])


def _slopes(dil):
    h = lax.broadcasted_iota(jnp.int32, (ATT_H, 1, 1), 0)
    return lax.bitcast_convert_type((126 - h) << 23, F32) * float(dil)


def _att_scores_b(q, k, slope, n_ok, far, keys_first=False):
    r = lax.broadcasted_iota(jnp.int32, (1, ATT_BLK, ATT_BLK), 1)
    c = lax.broadcasted_iota(jnp.int32, (1, ATT_BLK, ATT_BLK), 2)
    a, j = (c, r) if keys_first else (r, c)
    rel = (ATT_BLK + a - j) if far else (a - j)
    mask = ((j >= a) & n_ok) if far else (j <= a)
    s = (_bdot(k, q, BNT) if keys_first else _bdot(q, k, BNT)) - slope * rel.astype(F32)
    return jnp.where(mask, s, NEG), mask


def _att_blocks(nb, width, shift):
    def make(col):
        return pl.BlockSpec((ATT_BLK, width),
                            lambda r, n: (r * nb + jnp.clip(n + shift, 0, nb - 1), col))
    return make


def _lane_col(cols):
    lane = lax.broadcasted_iota(jnp.int32, (1, LANE), 1)
    out = jnp.zeros((ATT_BLK, LANE), F32)
    for h, c in enumerate(cols):
        out = out + c * (lane == h).astype(F32)
    return out


def att_fwd(qkvn, gi, name):
    T = qkvn.shape[0]
    dil = DIL[gi]
    nb = T // dil // ATT_BLK

    def body(q_ref, kp_ref, kc_ref, vp_ref, vc_ref, o_ref, l_ref):
        n_ok = pl.program_id(1) > 0
        slope = _slopes(dil)
        q = _heads(q_ref)
        s_c, _ = _att_scores_b(q, _heads(kc_ref), slope, n_ok, False)
        s_p, _ = _att_scores_b(q, _heads(kp_ref), slope, n_ok, True)
        m = jnp.maximum(jnp.max(s_c, axis=-1, keepdims=True), jnp.max(s_p, axis=-1, keepdims=True))
        p_c, p_p = jnp.exp(s_c - m), jnp.exp(s_p - m)
        l = jnp.sum(p_c, axis=-1, keepdims=True) + jnp.sum(p_p, axis=-1, keepdims=True)
        o = (_bdot(p_c, _heads(vc_ref), BNN) + _bdot(p_p, _heads(vp_ref), BNN)) / l
        lse = m + jnp.log(l)
        for h in range(ATT_H):
            o_ref[:, h * HD:(h + 1) * HD] = o[h].astype(BF16)
        l_ref[...] = _lane_col([lse[h] for h in range(ATT_H)])

    cur, prv = _att_blocks(nb, ATT_W, 0), _att_blocks(nb, ATT_W, -1)
    return _pcall(
        body, name=name, grid=(dil, nb), in_specs=[cur(0), prv(1), cur(1), prv(2), cur(2)],
        out_specs=[cur(0), _att_blocks(nb, LANE, 0)(0)],
        out_shape=[jax.ShapeDtypeStruct((T, ATT_W), BF16), jax.ShapeDtypeStruct((T, LANE), F32)],
        long_call=True)(qkvn, qkvn, qkvn, qkvn, qkvn)


def att_merge(os, lses, name):
    T = os[0].shape[0]
    tb = _rows(T, 512)

    def body(o0, o1, o2, l0, l1, l2, o_ref, l_ref):
        a, b, c = l0[...], l1[...], l2[...]
        m = jnp.maximum(a, jnp.maximum(b, c))
        wa, wb, wc = jnp.exp(a - m), jnp.exp(b - m), jnp.exp(c - m)
        den = wa + wb + wc
        l_ref[...] = m + jnp.log(den)
        wa, wb, wc = wa / den, wb / den, wc / den
        for h in range(ATT_H):
            sl = slice(h * HD, (h + 1) * HD)
            o_ref[:, sl] = (wa[:, h:h + 1] * o0[:, sl].astype(F32) + wb[:, h:h + 1] * o1[:, sl].astype(F32)
                            + wc[:, h:h + 1] * o2[:, sl].astype(F32))

    blk = pl.BlockSpec((tb, ATT_W), lambda i: (i, 0))
    lblk = pl.BlockSpec((tb, LANE), lambda i: (i, 0))
    return _pcall(
        body, name=name, grid=(T // tb,), in_specs=[blk] * 3 + [lblk] * 3, out_specs=[blk, lblk],
        out_shape=[jax.ShapeDtypeStruct((T, ATT_W), F32), jax.ShapeDtypeStruct((T, LANE), F32)],
        compiler_params=_params(1))(*os, *lses)


def att_delta(do, o, name):
    T = do.shape[0]
    tb = _rows(T, 512)

    def body(d_ref, o_ref, out_ref):
        lane = lax.broadcasted_iota(jnp.int32, (1, LANE), 1)
        out = jnp.zeros((tb, LANE), F32)
        for h in range(ATT_H):
            sl = slice(h * HD, (h + 1) * HD)
            s = jnp.sum(d_ref[:, sl].astype(F32) * o_ref[:, sl], axis=-1, keepdims=True)
            out = out + s * (lane == h).astype(F32)
        out_ref[...] = out

    blk = pl.BlockSpec((tb, ATT_W), lambda i: (i, 0))
    return _pcall(
        body, name=name, grid=(T // tb,), in_specs=[blk, blk],
        out_specs=pl.BlockSpec((tb, LANE), lambda i: (i, 0)),
        out_shape=jax.ShapeDtypeStruct((T, LANE), F32), compiler_params=_params(1))(do, o)


def att_bwd_q(qkvn, do, lse, delta, gi, name):
    T = qkvn.shape[0]
    dil = DIL[gi]
    nb = T // dil // ATT_BLK

    def body(q_ref, kp_ref, kc_ref, vp_ref, vc_ref, do_ref, l_ref, d_ref, dq_ref):
        n_ok = pl.program_id(1) > 0
        slope = _slopes(dil)
        lse, dl = l_ref[...], d_ref[...]
        lse = jnp.stack([lse[:, h:h + 1] for h in range(ATT_H)])
        dl = jnp.stack([dl[:, h:h + 1] for h in range(ATT_H)])
        q, do = _heads(q_ref), _heads(do_ref)
        dq = jnp.zeros((ATT_H, ATT_BLK, HD), F32)
        for k_ref, v_ref, far in ((kc_ref, vc_ref, False), (kp_ref, vp_ref, True)):
            k = _heads(k_ref)
            s, mask = _att_scores_b(q, k, slope, n_ok, far)
            p = jnp.where(mask, jnp.exp(s - lse), 0.0)
            ds = p * (_bdot(do, _heads(v_ref), BNT) - dl)
            dq = dq + _bdot(ds, k, BNN)
        for h in range(ATT_H):
            dq_ref[:, h * HD:(h + 1) * HD] = dq[h].astype(BF16)

    cur, prv = _att_blocks(nb, ATT_W, 0), _att_blocks(nb, ATT_W, -1)
    small = _att_blocks(nb, LANE, 0)(0)
    return _pcall(
        body, name=name, grid=(dil, nb),
        in_specs=[cur(0), prv(1), cur(1), prv(2), cur(2), cur(0), small, small],
        out_specs=cur(0), out_shape=jax.ShapeDtypeStruct((T, ATT_W), BF16),
        long_call=True)(qkvn, qkvn, qkvn, qkvn, qkvn, do, lse, delta)


def att_bwd_kv(qkvn, do, lse, delta, gi, name):
    T = qkvn.shape[0]
    dil = DIL[gi]
    nb = T // dil // ATT_BLK

    def body(k_ref, v_ref, q0_ref, q1_ref, do0_ref, do1_ref, l0_ref, l1_ref, d0_ref, d1_ref,
             dk_ref, dv_ref):
        n_ok = pl.program_id(1) < nb - 1
        slope = _slopes(dil)
        by_row = lambda ref: jnp.stack([ref[...].T[h:h + 1, :] for h in range(ATT_H)])
        k, v = _heads(k_ref), _heads(v_ref)
        dk = jnp.zeros((ATT_H, ATT_BLK, HD), F32)
        dv = jnp.zeros((ATT_H, ATT_BLK, HD), F32)
        for q_ref, do_ref, l_ref, d_ref, far in ((q0_ref, do0_ref, l0_ref, d0_ref, False),
                                                 (q1_ref, do1_ref, l1_ref, d1_ref, True)):
            q, do = _heads(q_ref), _heads(do_ref)
            s, mask = _att_scores_b(q, k, slope, n_ok, far, keys_first=True)
            p = jnp.where(mask, jnp.exp(s - by_row(l_ref)), 0.0)
            dv = dv + _bdot(p, do, BNN)
            ds = p * (_bdot(v, do, BNT) - by_row(d_ref))
            dk = dk + _bdot(ds, q, BNN)
        for h in range(ATT_H):
            dk_ref[:, h * HD:(h + 1) * HD] = dk[h].astype(BF16)
            dv_ref[:, h * HD:(h + 1) * HD] = dv[h].astype(BF16)

    cur, nxt = _att_blocks(nb, ATT_W, 0), _att_blocks(nb, ATT_W, 1)
    s0, s1 = _att_blocks(nb, LANE, 0)(0), _att_blocks(nb, LANE, 1)(0)
    return _pcall(
        body, name=name, grid=(dil, nb),
        in_specs=[cur(1), cur(2), cur(0), nxt(0), cur(0), nxt(0), s0, s1, s0, s1],
        out_specs=[cur(0), cur(0)], out_shape=[jax.ShapeDtypeStruct((T, ATT_W), BF16)] * 2,
        long_call=True)(qkvn, qkvn, qkvn, qkvn, do, do, lse, lse, delta, delta)


def adamw(w, g, m, v, name):
    shape = w.shape
    C = shape[-1]
    R = math.prod(shape[:-1])
    to2d = lambda t: t.reshape(R, C)
    tb = _rows(R, max(16, (256 * 1536 // C) // 16 * 16))
    c1 = 1.0 - ADAM_B1 ** ADAM_STEP
    c2 = 1.0 - ADAM_B2 ** ADAM_STEP

    def body(w_ref, g_ref, m_ref, v_ref, d_ref, nm_ref, nv_ref):
        gv = g_ref[...]
        nm = ADAM_B1 * m_ref[...] + (1.0 - ADAM_B1) * gv
        nv = ADAM_B2 * v_ref[...] + (1.0 - ADAM_B2) * (gv * gv)
        d_ref[...] = -ADAM_LR * ((nm / c1) / (jnp.sqrt(nv / c2) + ADAM_EPS) + ADAM_WD * w_ref[...])
        nm_ref[...] = nm
        nv_ref[...] = nv

    blk = pl.BlockSpec((tb, C), lambda i: (i, 0))
    out = jax.ShapeDtypeStruct((R, C), F32)
    d, nm, nv = _pcall(
        body, name=name, grid=(R // tb,), in_specs=[blk] * 4, out_specs=[blk] * 3,
        out_shape=[out, out, out], compiler_params=_params(1))(to2d(w), to2d(g), to2d(m), to2d(v))
    return d.reshape(shape), nm.reshape(shape), nv.reshape(shape)


def _pad_rows8(w):
    return jnp.pad(w, ((0, SUBLANE - w.shape[0]), (0, 0)))


def _lane_row(v):
    return jnp.pad(v, (0, LANE - v.shape[0]))[None, :]


def _even_reorder(w_in):
    z4 = 4 * GDN_W
    pad = jnp.zeros((w_in.shape[0], EVEN_PAD - EVEN_COLS), w_in.dtype)
    return jnp.concatenate([w_in[:, :z4], w_in[:, z4 + 2 * GDN_H:], w_in[:, z4:z4 + 2 * GDN_H], pad], axis=1)


def _even_restore(dw):
    z4 = 4 * GDN_W
    p4 = POOL_G * HD
    return jnp.concatenate([dw[:, :z4], dw[:, z4 + p4:z4 + p4 + 2 * GDN_H], dw[:, z4:z4 + p4]], axis=1)


def _ffn_fwd(tag, hf, wl):
    up = matmul(hf, wl["ffn_w_up"], "nn", BF16, f"{tag}_ffn_up")
    act = ffn_mid_fwd(up, wl["ffn_conv_w8"], wl["ffn_conv_b"], f"{tag}_ffn_mid")
    f = matmul(act, wl["ffn_w_down"], "nn", F32, f"{tag}_ffn_down")
    return f, (hf, up, act)


def _ffn_bwd(tag, df, saved, wl):
    hf, up, act = saved
    dact = matmul(df, wl["ffn_w_down"], "nt", BF16, f"{tag}_ffn_down_da")
    wl["on_grad"]("ffn_w_down", matmul(act, df, "tn", F32, f"{tag}_ffn_down_dw", pieces="row"))
    dgate, dval, dcw, dcb = ffn_mid_bwd(up, wl["ffn_conv_w8"], wl["ffn_conv_b"], dact, f"{tag}_ffn_mid_bwd")
    dup = jnp.concatenate([dgate, dval], axis=1)
    dhf = matmul(dup, wl["ffn_w_up"], "nt", F32, f"{tag}_ffn_up_da")
    wl["on_grad"]("ffn_w_up", matmul(hf, dup, "tn", F32, f"{tag}_ffn_up_dw", pieces="col"))
    return dhf, {"ffn_conv_w": dcw[:FFN_CONV], "ffn_conv_b": dcb[0]}


def _even_fwd(tag, hm, wl):
    proj = matmul(hm, wl["w_in"], "nn", F32, f"{tag}_ev_in")
    qkv = gdn_conv_fwd(proj, wl["gdn_conv_w8"], f"{tag}_gdn_conv")
    o_a, *states = gdn_chunk_fwd(qkv, proj, wl["alog_row"], wl["dt_row"], wl["gdn_norm"], f"{tag}_gdn_chunk")
    o_b = pool_fwd(proj, wl["pool_w"], wl["pool_scale"], f"{tag}_pool")
    ocat = jnp.concatenate([o_a, o_b], axis=1)
    y = matmul(ocat, wl["w_out"], "nn", F32, f"{tag}_ev_out")
    return y, (hm, proj, qkv, states, ocat)


def _even_bwd(tag, dy, saved, wl):
    hm, proj, qkv, states, ocat = saved
    docat = matmul(dy, wl["w_out"], "nt", F32, f"{tag}_ev_out_da")
    wl["on_grad"]("ev_w_out", matmul(ocat, dy, "tn", F32, f"{tag}_ev_out_dw", pieces="row"))
    dqkv, dz, dba, dalog, ddt, dgn = gdn_chunk_bwd(
        qkv, proj, wl["alog_row"], wl["dt_row"], wl["gdn_norm"], *states, docat, f"{tag}_gdn_chunk_bwd")
    dxc, dconv = gdn_conv_bwd(proj, wl["gdn_conv_w8"], dqkv, f"{tag}_gdn_conv_bwd")
    dp, dpw, dps = pool_bwd(proj, wl["pool_w"], wl["pool_scale"], docat, f"{tag}_pool_bwd")
    dproj = jnp.concatenate([dxc, dz, dp, dba], axis=1)
    dhm = matmul(dproj, wl["w_in"], "nt", F32, f"{tag}_ev_in_da")
    wl["on_grad"]("ev_w_in", col_pieces(_even_restore(matmul(hm, dproj, "tn", F32, f"{tag}_ev_in_dw"))))
    return dhm, {"gdn_conv_w": dconv[:GDN_CONV], "gdn_a_log": dalog[0, :GDN_H], "gdn_dt_bias": ddt[0, :GDN_H],
                 "gdn_norm": dgn[0], "pool_w": dpw, "pool_scale": dps[0]}


def _odd_fwd(tag, hm, wl):
    projs, qkvns, outs, lses = [], [], [], []
    for gi, d in enumerate(DIL):
        w_g = wl["w_in"][:, gi * GROUP_COLS:(gi + 1) * GROUP_COLS]
        proj = matmul(to_residue_major(hm, d), w_g, "nn", BF16, f"{tag}_od_in{gi}")
        qkvn = headnorm_fwd(proj, wl["qk_gain8"], f"{tag}_headnorm{gi}")
        o_g, l_g = att_fwd(qkvn, gi, f"{tag}_att{gi}")
        projs.append(proj)
        qkvns.append(qkvn)
        outs.append(to_token_order(o_g, d))
        lses.append(to_token_order(l_g, d))
    o, lse = att_merge(outs, lses, f"{tag}_att_merge")
    y = matmul(o, wl["w_out"], "nn", F32, f"{tag}_od_out")
    return y, (hm, projs, qkvns, o, lse)


def _odd_bwd(tag, dy, saved, wl):
    hm, projs, qkvns, o, lse = saved
    do = matmul(dy, wl["w_out"], "nt", BF16, f"{tag}_od_out_da")
    wl["on_grad"]("od_w_out", matmul(o, dy, "tn", F32, f"{tag}_od_out_dw", pieces="row"))
    delta = att_delta(do, o, f"{tag}_att_delta")
    dhm, dw_in, dgain_qk = None, [], None
    for gi, d in enumerate(DIL):
        w_g = wl["w_in"][:, gi * GROUP_COLS:(gi + 1) * GROUP_COLS]
        do_g, lse_g, dl_g = (to_residue_major(t, d) for t in (do, lse, delta))
        dq = att_bwd_q(qkvns[gi], do_g, lse_g, dl_g, gi, f"{tag}_att{gi}_dq")
        dk, dv = att_bwd_kv(qkvns[gi], do_g, lse_g, dl_g, gi, f"{tag}_att{gi}_dkv")
        dproj, dgain = headnorm_bwd(projs[gi], wl["qk_gain8"], dq, dk, dv, f"{tag}_headnorm{gi}_bwd")
        dhm_g = to_token_order(matmul(dproj, w_g, "nt", F32, f"{tag}_od_in{gi}_da"), d)
        dw_in.append(matmul(to_residue_major(hm, d), dproj, "tn", F32, f"{tag}_od_in{gi}_dw"))
        dhm = dhm_g if dhm is None else dhm + dhm_g
        dgain_qk = dgain if dgain_qk is None else dgain_qk + dgain
    wl["on_grad"]("od_w_in", col_pieces(jnp.concatenate(dw_in, axis=1)))
    return dhm, {"att_q_norm": dgain_qk[0], "att_k_norm": dgain_qk[1]}


def col_pieces(dw):
    M, N = dw.shape
    return dw.reshape(2, M // 2, N_CHIPS, N // N_CHIPS).transpose(0, 2, 1, 3)


class _Lazy:
    def __init__(self, fn):
        self.fn, self.value = fn, None


class _LayerWeights(dict):
    def __getitem__(self, key):
        v = dict.__getitem__(self, key)
        if isinstance(v, _Lazy):
            if v.value is None:
                v.value = v.fn()
            return v.value
        return v


def _layer_weights(i, W, big, on_grad):
    e = i // 2
    wl = _LayerWeights({
        "norm_mix": W["norm_mix"][i][None, :], "norm_ffn": W["norm_ffn"][i][None, :],
        "ffn_w_up": _Lazy(lambda: big("ffn_w_up", i)), "ffn_w_down": _Lazy(lambda: big("ffn_w_down", i)),
        "ffn_conv_w8": _pad_rows8(W["ffn_conv_w"][i]), "ffn_conv_b": W["ffn_conv_b"][i][None, :],
        "on_grad": lambda name, pieces: on_grad(name, i if name.startswith("ffn") else e, pieces)})
    if i % 2 == 0:
        wl.update({"w_in": _Lazy(lambda: _even_reorder(big("ev_w_in", e))),
                   "w_out": _Lazy(lambda: big("ev_w_out", e)),
                   "gdn_conv_w8": _pad_rows8(W["gdn_conv_w"][e]),
                   "alog_row": _lane_row(W["gdn_a_log"][e]), "dt_row": _lane_row(W["gdn_dt_bias"][e]),
                   "gdn_norm": W["gdn_norm"][e][None, :], "pool_w": W["pool_w"][e],
                   "pool_scale": W["pool_scale"][e][None, :]})
    else:
        wl.update({"w_in": _Lazy(lambda: big("od_w_in", e)), "w_out": _Lazy(lambda: big("od_w_out", e)),
                   "qk_gain8": _pad_rows8(jnp.stack([W["att_q_norm"][e], W["att_k_norm"][e]]))})
    return wl


def local_step(x, target, mod, W, big, on_grad):
    depth = mod.shape[0]
    row = lambda i, k: mod[i, k][None, :]
    saved, wls = [], []
    pending = None
    for i in range(depth):
        tag = f"l{i}"
        wl = _layer_weights(i, W, big, on_grad)
        if pending is None:
            hm = modnorm_fwd(x, wl["norm_mix"], row(i, 1), row(i, 0), f"{tag}_mix_norm")
        else:
            x, hm = res_norm_fwd(x, *pending, wl["norm_mix"], row(i, 1), row(i, 0), f"{tag}_mix_norm")
        y, s_mix = (_even_fwd if i % 2 == 0 else _odd_fwd)(tag, hm, wl)
        x1, hf = res_norm_fwd(x, row(i, 2), y, wl["norm_ffn"], row(i, 4), row(i, 3), f"{tag}_ffn_norm")
        f, s_ffn = _ffn_fwd(tag, hf, wl)
        saved.append((x, y, s_mix, x1, f, s_ffn))
        wls.append(wl)
        x, pending = x1, (row(i, 5), f)
    sq, dx = loss_head(gres_fwd(x, *pending, "last_res"), target, "loss_head")

    dmod, grads = [None] * depth, [None] * depth
    df, dg_f = gres_bwd(dx, *pending, "last_res_bwd")
    for i in reversed(range(depth)):
        tag = f"l{i}"
        x, y, s_mix, x1, f, s_ffn = saved[i]
        dhf, g_ffn = _ffn_bwd(tag, df, s_ffn, wls[i])
        dx, dgain_f, dsc_f, dsh_f, dy, dg_m = norm_res_bwd(
            x1, wls[i]["norm_ffn"], row(i, 4), dhf, dx, row(i, 2), y, f"{tag}_ffn_norm_bwd")
        dhm, g_mix = (_even_bwd if i % 2 == 0 else _odd_bwd)(tag, dy, s_mix, wls[i])
        dmod_f = [dsh_f, dsc_f, dg_f]
        if i > 0:
            dx, dgain_m, dsc_m, dsh_m, df, dg_f = norm_res_bwd(
                x, wls[i]["norm_mix"], row(i, 1), dhm, dx, row(i - 1, 5), saved[i - 1][4], f"{tag}_mix_norm_bwd")
        else:
            dx, dgain_m, dsc_m, dsh_m = modnorm_bwd(x, wls[i]["norm_mix"], row(i, 1), dhm, dx, f"{tag}_mix_norm_bwd")
        dmod[i] = jnp.concatenate([dsh_m, dsc_m, dg_m] + dmod_f, axis=0)
        grads[i] = {"norm_mix": dgain_m[0], "norm_ffn": dgain_f[0], **g_mix, **g_ffn}
    return sq, dx, jnp.stack(dmod), grads


def _place():
    return lax.axis_index("x"), lax.axis_index("y"), lax.axis_index("c")


def _other_chips(mx, my):
    return [(1 - mx, my), (mx, 1 - my), (1 - mx, 1 - my)]


def _sems(n):
    return [DMA_SEM((n,)), DMA_SEM((n,))]


def _put(buf, block, index):
    return lax.dynamic_update_index_in_dim(buf, block, index, 0)


def gather8_ride(x, then):
    def parts(ins, outs, sems):
        (x_ref,), (out_ref,), (send_sems, recv_sems) = ins, outs, sems
        mx, my, mc = _place()
        me, sibling = (mx, my, mc), (mx, my, 1 - mc)
        chips = _other_chips(mx, my)

        def slot(px, py, pc):
            return out_ref.at[4 * px + 2 * py + pc]

        def copy(k, block, to, src=None):
            return pltpu.make_async_remote_copy(
                src_ref=slot(*block) if src is None else src, dst_ref=slot(*block),
                send_sem=send_sems.at[k], recv_sem=recv_sems.at[k], device_id=to, device_id_type=MESH)

        first = lambda: ([copy(0, me, sibling, src=x_ref)]
                         + [copy(1 + j, me, (*chip, mc), src=x_ref) for j, chip in enumerate(chips)])
        passed = lambda: [copy(4 + j, (*chip, mc), sibling) for j, chip in enumerate(chips)]
        landed = lambda: [copy(1 + j, (*chip, mc), me) for j, chip in enumerate(chips)]
        from_sibling = lambda: ([copy(0, sibling, me)]
                                + [copy(4 + j, (*chip, 1 - mc), me) for j, chip in enumerate(chips)])
        return first, passed, landed, from_sibling

    def start(ins, outs, sems):
        first, _, _, _ = parts(ins, outs, sems)
        for cp in first():
            cp.start()

    def mid(ins, outs, sems):
        _, passed, landed, _ = parts(ins, outs, sems)
        for cp, fwd in zip(landed(), passed()):
            cp.wait_recv()
            fwd.start()

    def finish(ins, outs, sems):
        first, passed, _, from_sibling = parts(ins, outs, sems)
        for cp in from_sibling():
            cp.wait_recv()
        for cp in first() + passed():
            cp.wait_send()

    def landed_all(outs):
        mx, my, mc = _place()
        then(_put(outs[0], x, 4 * mx + 2 * my + mc))

    return Ride([x], [jax.ShapeDtypeStruct((8,) + x.shape, x.dtype)], _sems(7), start, finish,
                landed_all, mid=mid, heavy=True)


def all_gather8(x):
    box = []
    waiting = list(_RIDES)
    _RIDES[:] = [gather8_ride(x, box.append)]
    flush_rides()
    _RIDES[:] = waiting + _RIDES
    return box[0]


def sibling_halves_ride(p, then):
    def copy(ins, outs, sems):
        (p_ref,), (got_ref,), (send_sems, recv_sems) = ins, outs, sems
        mx, my, mc = _place()
        return pltpu.make_async_remote_copy(src_ref=p_ref.at[1 - mc], dst_ref=got_ref, send_sem=send_sems.at[0],
                                            recv_sem=recv_sems.at[0], device_id=(mx, my, 1 - mc), device_id_type=MESH)

    def start(ins, outs, sems):
        copy(ins, outs, sems).start()

    def finish(ins, outs, sems):
        cp = copy(ins, outs, sems)
        cp.wait_send()
        cp.wait_recv()

    def landed(outs):
        then(lax.dynamic_index_in_dim(p, _place()[2], 0, keepdims=False), outs[0])

    return Ride([p], [jax.ShapeDtypeStruct(p.shape[1:], p.dtype)], _sems(1), start, finish, landed)


def sibling_pair_ride(r, then):
    def copy(ins, outs, sems):
        (r_ref,), (got_ref,), (send_sems, recv_sems) = ins, outs, sems
        mx, my, mc = _place()
        return pltpu.make_async_remote_copy(src_ref=r_ref, dst_ref=got_ref, send_sem=send_sems.at[0],
                                            recv_sem=recv_sems.at[0], device_id=(mx, my, 1 - mc), device_id_type=MESH)

    def start(ins, outs, sems):
        copy(ins, outs, sems).start()

    def finish(ins, outs, sems):
        cp = copy(ins, outs, sems)
        cp.wait_send()
        cp.wait_recv()

    def landed(outs):
        then(jnp.where(_place()[2] == 0, jnp.stack([r, outs[0]]), jnp.stack([outs[0], r])))

    return Ride([r], [jax.ShapeDtypeStruct(r.shape, r.dtype)], _sems(1), start, finish, landed)


def chip_scatter_ride(p, then):
    def parts(ins, outs, sems):
        (p_ref,), (out_ref,), (send_sems, recv_sems) = ins, outs, sems
        mx, my, mc = _place()
        mine = 2 * mx + my
        chips = _other_chips(mx, my)
        sends = [pltpu.make_async_remote_copy(
            src_ref=p_ref.at[2 * chip[0] + chip[1]], dst_ref=out_ref.at[mine], send_sem=send_sems.at[k],
            recv_sem=recv_sems.at[k], device_id=(*chip, mc), device_id_type=MESH) for k, chip in enumerate(chips)]
        recvs = lambda: [pltpu.make_async_remote_copy(
            src_ref=p_ref.at[mine], dst_ref=out_ref.at[2 * chip[0] + chip[1]], send_sem=send_sems.at[k],
            recv_sem=recv_sems.at[k], device_id=(*chip, mc), device_id_type=MESH) for k, chip in enumerate(chips)]
        return sends, recvs

    def start(ins, outs, sems):
        sends, _ = parts(ins, outs, sems)
        for cp in sends:
            cp.start()

    def finish(ins, outs, sems):
        sends, recvs = parts(ins, outs, sems)
        for cp in recvs():
            cp.wait_recv()
        for cp in sends:
            cp.wait_send()

    def landed(outs):
        mx, my, _ = _place()
        mine = 2 * mx + my
        then(_put(outs[0], lax.dynamic_index_in_dim(p, mine, 0, keepdims=False), mine))

    return Ride([p], [jax.ShapeDtypeStruct(p.shape, p.dtype)], _sems(3), start, finish, landed, heavy=True)


def _stream_rows(R, C):
    return _rows(R, max(16, (256 * 1536 // C) // 16 * 16))


def cast_bf16(w, name):
    R, C = w.shape
    tb = _stream_rows(R, C)

    def body(w_ref, o_ref):
        o_ref[...] = w_ref[...].astype(BF16)

    blk = pl.BlockSpec((tb, C), lambda i: (i, 0))
    return _pcall(body, name=name, grid=(R // tb,), in_specs=[blk], out_specs=blk,
                          out_shape=jax.ShapeDtypeStruct((R, C), BF16), compiler_params=_params(1))(w)


def sum_slots(g, name):
    n, R, C = g.shape
    tb = _stream_rows(R, C)

    def body(*refs):
        acc = refs[0][...].astype(F32)
        for r in refs[1:n]:
            acc = acc + r[...].astype(F32)
        refs[n][...] = acc

    specs = [pl.BlockSpec((None, tb, C), functools.partial(lambda k, i: (k, i, 0), k)) for k in range(n)]
    return _pcall(body, name=name, grid=(R // tb,), in_specs=specs,
                          out_specs=pl.BlockSpec((tb, C), lambda i: (i, 0)),
                          out_shape=jax.ShapeDtypeStruct((R, C), F32), compiler_params=_params(1))(*([g] * n))


def add_to_bf16(a, b, name):
    R, C = a.shape
    tb = _stream_rows(R, C)

    def body(a_ref, b_ref, o_ref):
        o_ref[...] = (a_ref[...] + b_ref[...]).astype(BF16)

    blk = pl.BlockSpec((tb, C), lambda i: (i, 0))
    return _pcall(body, name=name, grid=(R // tb,), in_specs=[blk, blk], out_specs=blk,
                          out_shape=jax.ShapeDtypeStruct((R, C), BF16), compiler_params=_params(1))(a, b)


def ada_fwd(c_all, ada_w, bias, name):
    n, D, Cs = ada_w.shape
    tn = _tile(Cs, 512)

    def body(c_ref, w_ref, b_ref, o_ref):
        o_ref[...] = _bdot(_silu(c_ref[...]), w_ref[...], NN) + b_ref[...]

    return _pcall(
        body, name=name, grid=(n, Cs // tn),
        in_specs=[pl.BlockSpec((8, D), lambda l, j: (0, 0)),
                  pl.BlockSpec((None, D, tn), lambda l, j: (l, 0, j)),
                  pl.BlockSpec((None, 1, tn), lambda l, j: (l, 0, j))],
        out_specs=pl.BlockSpec((None, 8, tn), lambda l, j: (l, 0, j)),
        out_shape=jax.ShapeDtypeStruct((n, 8, Cs), F32), compiler_params=_params(2))(c_all, ada_w, bias)


def ada_bwd(c16, dmod16, name):
    n, _, Cs = dmod16.shape
    D = c16.shape[1]
    tn = _tile(Cs, 512)

    def body(c_ref, d_ref, o_ref):
        o_ref[...] = _bdot(_silu(c_ref[...]), d_ref[...], TN)

    return _pcall(
        body, name=name, grid=(n, Cs // tn),
        in_specs=[pl.BlockSpec((16, D), lambda l, j: (0, 0)),
                  pl.BlockSpec((None, 16, tn), lambda l, j: (l, 0, j))],
        out_specs=pl.BlockSpec((None, D, tn), lambda l, j: (l, 0, j)),
        out_shape=jax.ShapeDtypeStruct((n, D, Cs), F32), compiler_params=_params(2))(c16, dmod16)


WEIGHTS = ["ada_w", "ada_b", "norm_mix", "norm_ffn", "ev_w_in", "ev_w_out", "gdn_conv_w", "gdn_a_log",
           "gdn_dt_bias", "gdn_norm", "pool_w", "pool_scale", "od_w_in", "od_w_out", "att_q_norm",
           "att_k_norm", "ffn_w_up", "ffn_conv_w", "ffn_conv_b", "ffn_w_down"]
COL_SHARDED = ("ev_w_in", "od_w_in", "ffn_w_up")
ROW_SHARDED = ("ev_w_out", "od_w_out", "ffn_w_down")


def _pack(parts):
    rows, offs = [], []
    at = 0
    for p in parts:
        flat = p.reshape(-1).astype(F32)
        n = -(-flat.shape[0] // LANE)
        rows.append(jnp.pad(flat, (0, n * LANE - flat.shape[0])).reshape(n, LANE))
        offs.append((at, n))
        at += n
    pad = -at % 16
    if pad:
        rows.append(jnp.zeros((pad, LANE), F32))
    return jnp.concatenate(rows, axis=0), offs


def _unpack(buf, off, shape):
    at, n = off
    lead = buf.shape[:-2]
    flat = buf[..., at:at + n, :].reshape(lead + (n * LANE,))
    return flat[..., :math.prod(shape)].reshape(lead + tuple(shape))


def submit_weight_gather(store, key, shard, col_sharded, mc):
    R, C = shard.shape
    half = lax.dynamic_index_in_dim(shard.reshape(2, R // 2, C), mc, 0, keepdims=False)

    def landed(g):
        g = g.reshape(N_CHIPS, R, C)
        store[key] = g.transpose(1, 0, 2).reshape(R, N_CHIPS * C) if col_sharded else g.reshape(N_CHIPS * R, C)

    submit_ride(gather8_ride(half, landed))


def submit_grad_reduce(store, key, pieces, col_sharded):
    _, _, R, C = pieces.shape
    tag = f"{key[0]}{key[1]}"

    def paired(out):
        store[key] = out.reshape(2 * R, C) if col_sharded else out.transpose(1, 0, 2).reshape(R, 2 * C)

    def scattered(got):
        submit_ride(sibling_pair_ride(sum_slots(got, f"gsum_{tag}"), paired))

    def swapped(keep, got):
        chip_sum = add_to_bf16(keep.reshape(N_CHIPS * R, C), got.reshape(N_CHIPS * R, C), f"gadd_{tag}")
        submit_ride(chip_scatter_ride(chip_sum.reshape(N_CHIPS, R, C), scattered))

    submit_ride(sibling_halves_ride(pieces, swapped))


def kernel(x, c, ada_w, ada_b, norm_mix, norm_ffn, ev_w_in, ev_w_out, gdn_conv_w, gdn_a_log, gdn_dt_bias, gdn_norm, pool_w, pool_scale, od_w_in, od_w_out, att_q_norm, att_k_norm, ffn_w_up, ffn_conv_w, ffn_conv_b, ffn_w_down, loss_target, m_ada_w, m_ada_b, m_norm_mix, m_norm_ffn, m_ev_w_in, m_ev_w_out, m_gdn_conv_w, m_gdn_a_log, m_gdn_dt_bias, m_gdn_norm, m_pool_w, m_pool_scale, m_od_w_in, m_od_w_out, m_att_q_norm, m_att_k_norm, m_ffn_w_up, m_ffn_conv_w, m_ffn_conv_b, m_ffn_w_down, v_ada_w, v_ada_b, v_norm_mix, v_norm_ffn, v_ev_w_in, v_ev_w_out, v_gdn_conv_w, v_gdn_a_log, v_gdn_dt_bias, v_gdn_norm, v_pool_w, v_pool_scale, v_od_w_in, v_od_w_out, v_att_q_norm, v_att_k_norm, v_ffn_w_up, v_ffn_conv_w, v_ffn_conv_b, v_ffn_w_down):
    local = dict(ada_w=ada_w, ada_b=ada_b, norm_mix=norm_mix, norm_ffn=norm_ffn, ev_w_in=ev_w_in,
                 ev_w_out=ev_w_out, gdn_conv_w=gdn_conv_w, gdn_a_log=gdn_a_log, gdn_dt_bias=gdn_dt_bias,
                 gdn_norm=gdn_norm, pool_w=pool_w, pool_scale=pool_scale, od_w_in=od_w_in, od_w_out=od_w_out,
                 att_q_norm=att_q_norm, att_k_norm=att_k_norm, ffn_w_up=ffn_w_up, ffn_conv_w=ffn_conv_w,
                 ffn_conv_b=ffn_conv_b, ffn_w_down=ffn_w_down)
    moments_m = dict(zip(WEIGHTS, (m_ada_w, m_ada_b, m_norm_mix, m_norm_ffn, m_ev_w_in, m_ev_w_out,
                                   m_gdn_conv_w, m_gdn_a_log, m_gdn_dt_bias, m_gdn_norm, m_pool_w, m_pool_scale,
                                   m_od_w_in, m_od_w_out, m_att_q_norm, m_att_k_norm, m_ffn_w_up, m_ffn_conv_w,
                                   m_ffn_conv_b, m_ffn_w_down)))
    moments_v = dict(zip(WEIGHTS, (v_ada_w, v_ada_b, v_norm_mix, v_norm_ffn, v_ev_w_in, v_ev_w_out,
                                   v_gdn_conv_w, v_gdn_a_log, v_gdn_dt_bias, v_gdn_norm, v_pool_w, v_pool_scale,
                                   v_od_w_in, v_od_w_out, v_att_q_norm, v_att_k_norm, v_ffn_w_up, v_ffn_conv_w,
                                   v_ffn_conv_b, v_ffn_w_down)))
    _RIDES.clear()
    _IDS[0] = 0
    mx, my, mc = _place()
    chip = 2 * mx + my
    T, D = x.shape[1], x.shape[2]
    depth = ada_w.shape[0]
    ada_cols = ada_w.shape[2]

    buf, offs = _pack([c, gdn_conv_w, ffn_conv_w])
    gathered = all_gather8(buf)
    c_all = _unpack(gathered, offs[0], (D,))
    by_chip = gathered[0::2]
    gdn_conv_full = jnp.concatenate(list(_unpack(by_chip, offs[1], gdn_conv_w.shape)), axis=-1)
    ffn_conv_full = jnp.concatenate(list(_unpack(by_chip, offs[2], ffn_conv_w.shape)), axis=-1)

    bias = lax.dynamic_slice_in_dim(ada_b, chip * ada_cols, ada_cols, axis=1)[:, None, :]
    mod_part = ada_fwd(c_all, ada_w, bias, "ada_fwd")
    mod_all = all_gather8(mod_part)[0::2]
    mod_all = mod_all.transpose(1, 2, 0, 3).reshape(depth, 8, N_CHIPS * ada_cols)
    mod = lax.dynamic_index_in_dim(mod_all, 4 * mx + 2 * my + mc, 1, keepdims=False).reshape(depth, 6, D)

    full_w, big_grad = {}, {}
    order = []
    for i in range(depth):
        mixer = ("ev_w_in", "ev_w_out") if i % 2 == 0 else ("od_w_in", "od_w_out")
        order += [(name, i // 2) for name in mixer] + [("ffn_w_up", i), ("ffn_w_down", i)]
    shards = {name: cast_bf16(local[name].reshape(-1, local[name].shape[-1]), f"cast_{name}")
              .reshape(local[name].shape) for name in COL_SHARDED + ROW_SHARDED}
    for name, e in order:
        submit_weight_gather(full_w, (name, e), shards[name][e], name in COL_SHARDED, mc)

    def big(name, e):
        flush_rides(until=lambda: (name, e) in full_w)
        return full_w[(name, e)]

    def on_grad(name, e, pieces):
        submit_grad_reduce(big_grad, (name, e), pieces, name in COL_SHARDED)

    W = dict(local)
    W["gdn_conv_w"], W["ffn_conv_w"] = gdn_conv_full, ffn_conv_full
    sq, dx, dmod, grads = local_step(x[0], loss_target[0], mod, W, big, on_grad)
    loss = lax.psum(0.5 * jnp.sum(sq) / D, ("x", "y", "c"))

    small = ["norm_mix", "norm_ffn", "gdn_conv_w", "gdn_a_log", "gdn_dt_bias", "gdn_norm", "pool_w",
             "pool_scale", "att_q_norm", "att_k_norm", "ffn_conv_w", "ffn_conv_b"]
    full = {name: jnp.stack([g[name] for g in grads if name in g]) for name in small}
    grad = {}
    buf, offs = _pack([dmod] + [full[name] for name in small])
    gathered = all_gather8(buf)
    summed = sum_slots(gathered, "sum_small_grads")
    grad["ada_b"] = _unpack(summed, offs[0], ada_b.shape)
    for k, name in enumerate(small):
        grad[name] = _unpack(summed, offs[1 + k], full[name].shape)
    for name, cols in (("gdn_conv_w", gdn_conv_w.shape[-1]), ("ffn_conv_w", ffn_conv_w.shape[-1])):
        grad[name] = lax.dynamic_slice_in_dim(grad[name], chip * cols, cols, axis=2)

    dmod_all = _unpack(gathered, offs[0], (depth, N_CHIPS * ada_cols))
    dmod_mine = lax.dynamic_slice_in_dim(dmod_all, chip * ada_cols, ada_cols, axis=2).transpose(1, 0, 2)
    grad["ada_w"] = ada_bwd(jnp.pad(c_all, ((0, 8), (0, 0))), jnp.pad(dmod_mine, ((0, 0), (0, 8), (0, 0))),
                            "ada_bwd")

    deltas, new_m, new_v = {}, {}, {}
    large = COL_SHARDED + ROW_SHARDED
    for name in [n for n in WEIGHTS if n not in large] + list(large):
        if name in large:
            keys = [k for k in order if k[0] == name]
            flush_rides(until=lambda: all(k in big_grad for k in keys))
            grad[name] = jnp.stack([big_grad[k] for k in keys])
        deltas[name], new_m[name], new_v[name] = adamw(local[name], grad[name], moments_m[name],
                                                       moments_v[name], f"adamw_{name}")
    flush_rides()
    return (loss, dx[None], *[grad[n] for n in WEIGHTS], *[deltas[n] for n in WEIGHTS],
            *[new_m[n] for n in WEIGHTS], *[new_v[n] for n in WEIGHTS])
```

```python
import functools
import math

import jax
import jax.numpy as jnp
from jax import lax
from jax.experimental import pallas as pl
from jax.experimental.pallas import tpu as pltpu

F32 = jnp.float32
BF16 = jnp.bfloat16
LANE = 128
SUBLANE = 8
VMEM_LIMIT = 56 * 1024 * 1024
MESH = pl.DeviceIdType.MESH
N_CHIPS = 4

RMS_EPS = 1e-6
GDN_H = 4
HD = 128
GDN_CHUNK = 64
GDN_STEP = 4
GDN_CONV = 4
FFN_CONV = 3
POOL_G = 4
ATT_H = 8
ATT_BLK = 128
DIL = (1, 4, 16)
EVEN_COLS = 2568
EVEN_PAD = 2688
ADAM_LR, ADAM_B1, ADAM_B2, ADAM_EPS, ADAM_WD, ADAM_STEP = 0.001, 0.9, 0.999, 1e-08, 0.01, 10
NEG = -1e30

NN = (((1,), (0,)), ((), ()))
NT = (((1,), (1,)), ((), ()))
TN = (((0,), (0,)), ((), ()))
BNN = (((2,), (1,)), ((0,), (0,)))
BNT = (((2,), (2,)), ((0,), (0,)))
BTN = (((1,), (1,)), ((0,), (0,)))


def _params(n_grid):
    return pltpu.CompilerParams(dimension_semantics=("arbitrary",) * n_grid,
                                vmem_limit_bytes=VMEM_LIMIT)


HBM = pl.BlockSpec(memory_space=pltpu.HBM)
DMA_SEM = pltpu.SemaphoreType.DMA


class Ride:
    def __init__(self, inputs, out_shapes, sems, start, finish, then, mid=None, heavy=False):
        self.inputs, self.out_shapes, self.sems = list(inputs), list(out_shapes), list(sems)
        self.start, self.mid, self.finish, self.then, self.heavy = start, mid, finish, then, heavy


_RIDES = []


def submit_ride(ride):
    _RIDES.append(ride)


def flush_rides(until=None):
    while _RIDES and not (until is not None and until()):
        ride = _RIDES.pop(0)

        def body(*refs, ride=ride):
            a, b = len(ride.inputs), len(ride.inputs) + len(ride.out_shapes)
            ride.start(refs[:a], refs[a:b], refs[b:])
            if ride.mid is not None:
                ride.mid(refs[:a], refs[a:b], refs[b:])
            ride.finish(refs[:a], refs[a:b], refs[b:])

        outs = pl.pallas_call(body, name=f"exchange{_next_id()}", in_specs=[HBM] * len(ride.inputs),
                              out_specs=[HBM] * len(ride.out_shapes), out_shape=ride.out_shapes,
                              scratch_shapes=ride.sems)(*ride.inputs)
        ride.then(list(outs))


_IDS = [0]


def _next_id():
    _IDS[0] += 1
    return _IDS[0]


def _pcall(body, *, name, grid, in_specs, out_specs, out_shape, scratch_shapes=(), compiler_params=None,
           long_call=False):
    del compiler_params
    single = not isinstance(out_shape, (list, tuple))
    outs = [out_shape] if single else list(out_shape)
    ospecs = [out_specs] if single else list(out_specs)
    total = math.prod(grid)
    fits = [k for k, r in enumerate(_RIDES) if long_call or not r.heavy] if total > 1 else []
    ride = _RIDES.pop(fits[0]) if fits else None
    if ride is None:
        call = pl.pallas_call(body, name=name, grid=grid, in_specs=list(in_specs), out_specs=ospecs,
                              out_shape=outs, scratch_shapes=list(scratch_shapes),
                              compiler_params=_params(len(grid)))

        def run_plain(*args):
            res = call(*args)
            return res[0] if single else res
        return run_plain

    n_in, n_out, n_scr = len(in_specs), len(outs), len(scratch_shapes)
    r_in, r_out = len(ride.inputs), len(ride.out_shapes)

    def carrying_body(*refs):
        at = 0
        ins = refs[at:at + n_in]; at += n_in
        r_ins = refs[at:at + r_in]; at += r_in
        os_ = refs[at:at + n_out]; at += n_out
        r_outs = refs[at:at + r_out]; at += r_out
        scr = refs[at:at + n_scr]; at += n_scr
        r_sems = refs[at:]
        step = pl.program_id(0)
        for ax in range(1, len(grid)):
            step = step * grid[ax] + pl.program_id(ax)

        @pl.when(step == 0)
        def _():
            ride.start(r_ins, r_outs, r_sems)

        body(*ins, *os_, *scr)

        if ride.mid is not None:
            @pl.when(step == total // 2)
            def _():
                ride.mid(r_ins, r_outs, r_sems)

        @pl.when(step == total - 1)
        def _():
            ride.finish(r_ins, r_outs, r_sems)

    call = pl.pallas_call(
        carrying_body, name=name, grid=grid, in_specs=list(in_specs) + [HBM] * r_in,
        out_specs=ospecs + [HBM] * r_out, out_shape=outs + ride.out_shapes,
        scratch_shapes=list(scratch_shapes) + ride.sems, compiler_params=_params(len(grid)))

    def run_carrying(*args):
        res = call(*args, *ride.inputs)
        ride.then(list(res[n_out:]))
        return res[0] if single else list(res[:n_out])
    return run_carrying


def _tile(n, target):
    if n <= target:
        return n
    best = None
    for t in range(LANE, target + 1, LANE):
        if n % t == 0:
            best = t
    assert best is not None, (n, target)
    return best


def _rows(n, target):
    if n <= target:
        return n
    best = None
    for t in range(16, target + 1, 16):
        if n % t == 0:
            best = t
    assert best is not None, (n, target)
    return best


def _bdot(a, b, dims):
    return lax.dot_general(a.astype(BF16), b.astype(BF16), dims, preferred_element_type=F32)


def _split(a):
    hi = a.astype(BF16)
    return hi, (a - hi.astype(F32)).astype(BF16)


def _dot3(a, b, dims):
    ah, al = _split(a)
    bh, bl = _split(b)
    d = lambda p, q: lax.dot_general(p, q, dims, preferred_element_type=F32)
    return d(ah, bh) + d(ah, bl) + d(al, bh)


def _sigmoid(x):
    return 1.0 / (1.0 + jnp.exp(-x))


def _silu(x):
    return x * _sigmoid(x)


def matmul(a, b, mode, out_dtype, name, tm=1024, tn=1536, tk=1536, pieces=None):
    if mode == "nn":
        (M, K), (K2, N) = a.shape, b.shape
    elif mode == "nt":
        (M, K), (N, K2) = a.shape, b.shape
    else:
        (K, M), (K2, N) = a.shape, b.shape
    assert K == K2, (a.shape, b.shape, mode)
    if pieces == "col":
        tm, tn = _tile(M // 2, tm), _tile(N // N_CHIPS, tn)
    elif pieces == "row":
        quarter = M // N_CHIPS
        tm = 2 * quarter if (2 * quarter) % LANE == 0 else M
        tn = _tile(N // 2, tn)
    else:
        tm, tn = _tile(M, tm), _tile(N, tn)
    tk = _tile(K, tk)
    nk = K // tk
    dims = {"nn": NN, "nt": NT, "tn": TN}[mode]
    if mode == "tn":
        a_spec = pl.BlockSpec((tk, tm), lambda i, j, k: (k, i))
    else:
        a_spec = pl.BlockSpec((tm, tk), lambda i, j, k: (i, k))
    if mode == "nt":
        b_spec = pl.BlockSpec((tn, tk), lambda i, j, k: (j, k))
    else:
        b_spec = pl.BlockSpec((tk, tn), lambda i, j, k: (k, j))

    out_spec = pl.BlockSpec((tm, tn), lambda i, j, k: (i, j))
    out_shape = jax.ShapeDtypeStruct((M, N), out_dtype)
    rows_per_slot = tm
    if pieces == "col":
        nih, njc = (M // 2) // tm, (N // N_CHIPS) // tn
        out_spec = pl.BlockSpec((None, None, tm, tn), lambda i, j, k: (i // nih, j // njc, i % nih, j % njc))
        out_shape = jax.ShapeDtypeStruct((2, N_CHIPS, M // 2, N // N_CHIPS), out_dtype)
    elif pieces == "row":
        rows_per_slot = M // N_CHIPS
        njh = (N // 2) // tn
        out_spec = pl.BlockSpec((None, tm // rows_per_slot, rows_per_slot, tn),
                                lambda i, j, k: (j // njh, i, 0, j % njh))
        out_shape = jax.ShapeDtypeStruct((2, N_CHIPS, rows_per_slot, N // 2), out_dtype)

    def store(o_ref, acc):
        if pieces == "row":
            for s in range(tm // rows_per_slot):
                o_ref[s] = acc[s * rows_per_slot:(s + 1) * rows_per_slot, :].astype(out_dtype)
        else:
            o_ref[...] = acc.astype(out_dtype)

    def body(a_ref, b_ref, o_ref, acc_ref):
        k = pl.program_id(2)
        p = _bdot(a_ref[...], b_ref[...], dims)
        if nk == 1:
            store(o_ref, p)
            return

        @pl.when(k == 0)
        def _():
            acc_ref[...] = p

        @pl.when(k > 0)
        def _():
            acc_ref[...] += p

        @pl.when(k == nk - 1)
        def _():
            store(o_ref, acc_ref[...])

    return _pcall(
        body, name=name, grid=(M // tm, N // tn, nk),
        in_specs=[a_spec, b_spec], out_specs=out_spec, out_shape=out_shape,
        scratch_shapes=[pltpu.VMEM((tm, tn), F32)],
        long_call=True)(a, b)


def _row_spec(d):
    return pl.BlockSpec((1, d), lambda i: (0, 0))


def modnorm_fwd(x, gain, sc, sh, name):
    T, D = x.shape
    tb = _rows(T, 512)

    def body(x_ref, g_ref, sc_ref, sh_ref, o_ref):
        xv = x_ref[...]
        r = lax.rsqrt(jnp.mean(xv * xv, axis=-1, keepdims=True) + RMS_EPS)
        o_ref[...] = ((xv * r) * g_ref[...] * (1.0 + sc_ref[...]) + sh_ref[...]).astype(BF16)

    blk = pl.BlockSpec((tb, D), lambda i: (i, 0))
    return _pcall(
        body, name=name, grid=(T // tb,),
        in_specs=[blk, _row_spec(D), _row_spec(D), _row_spec(D)],
        out_specs=blk, out_shape=jax.ShapeDtypeStruct((T, D), BF16),
        compiler_params=_params(1))(x, gain, sc, sh)


def modnorm_bwd(x, gain, sc, dh, dres, name):
    T, D = x.shape
    tb = _rows(T, 512)

    def body(x_ref, g_ref, sc_ref, dh_ref, dres_ref, dx_ref, dg_ref, dsc_ref, dsh_ref):
        i = pl.program_id(0)
        xv = x_ref[...]
        r = lax.rsqrt(jnp.mean(xv * xv, axis=-1, keepdims=True) + RMS_EPS)
        n = xv * r
        dhv = dh_ref[...].astype(F32)
        gain_v, sc1 = g_ref[...], 1.0 + sc_ref[...]
        dn = dhv * (gain_v * sc1)
        dx_ref[...] = r * (dn - n * jnp.mean(dn * n, axis=-1, keepdims=True)) + dres_ref[...]
        dhn = dhv * n

        @pl.when(i == 0)
        def _():
            dg_ref[...] = jnp.zeros_like(dg_ref)
            dsc_ref[...] = jnp.zeros_like(dsc_ref)
            dsh_ref[...] = jnp.zeros_like(dsh_ref)

        dg_ref[...] += jnp.sum(dhn * sc1, axis=0, keepdims=True)
        dsc_ref[...] += jnp.sum(dhn * gain_v, axis=0, keepdims=True)
        dsh_ref[...] += jnp.sum(dhv, axis=0, keepdims=True)

    blk = pl.BlockSpec((tb, D), lambda i: (i, 0))
    row = jax.ShapeDtypeStruct((1, D), F32)
    return _pcall(
        body, name=name, grid=(T // tb,),
        in_specs=[blk, _row_spec(D), _row_spec(D), blk, blk],
        out_specs=[blk, _row_spec(D), _row_spec(D), _row_spec(D)],
        out_shape=[jax.ShapeDtypeStruct((T, D), F32), row, row, row],
        compiler_params=_params(1))(x, gain, sc, dh, dres)


def gres_fwd(x, g, y, name):
    T, D = x.shape
    tb = _rows(T, 512)

    def body(x_ref, g_ref, y_ref, o_ref):
        o_ref[...] = x_ref[...] + g_ref[...] * y_ref[...]

    blk = pl.BlockSpec((tb, D), lambda i: (i, 0))
    return _pcall(
        body, name=name, grid=(T // tb,), in_specs=[blk, _row_spec(D), blk], out_specs=blk,
        out_shape=jax.ShapeDtypeStruct((T, D), F32), compiler_params=_params(1))(x, g, y)


def gres_bwd(dx, g, y, name):
    T, D = dx.shape
    tb = _rows(T, 512)

    def body(dx_ref, g_ref, y_ref, dy_ref, dg_ref):
        i = pl.program_id(0)
        dxv = dx_ref[...]
        dy_ref[...] = (dxv * g_ref[...]).astype(BF16)

        @pl.when(i == 0)
        def _():
            dg_ref[...] = jnp.zeros_like(dg_ref)

        dg_ref[...] += jnp.sum(dxv * y_ref[...], axis=0, keepdims=True)

    blk = pl.BlockSpec((tb, D), lambda i: (i, 0))
    return _pcall(
        body, name=name, grid=(T // tb,), in_specs=[blk, _row_spec(D), blk],
        out_specs=[blk, _row_spec(D)],
        out_shape=[jax.ShapeDtypeStruct((T, D), BF16), jax.ShapeDtypeStruct((1, D), F32)],
        compiler_params=_params(1))(dx, g, y)


def res_norm_fwd(x, g, y, gain, sc, sh, name):
    T, D = x.shape
    tb = _rows(T, 512)

    def body(x_ref, g_ref, y_ref, gn_ref, sc_ref, sh_ref, x1_ref, h_ref):
        xv = x_ref[...] + g_ref[...] * y_ref[...]
        x1_ref[...] = xv
        r = lax.rsqrt(jnp.mean(xv * xv, axis=-1, keepdims=True) + RMS_EPS)
        h_ref[...] = ((xv * r) * gn_ref[...] * (1.0 + sc_ref[...]) + sh_ref[...]).astype(BF16)

    blk = pl.BlockSpec((tb, D), lambda i: (i, 0))
    row = _row_spec(D)
    return _pcall(
        body, name=name, grid=(T // tb,), in_specs=[blk, row, blk, row, row, row], out_specs=[blk, blk],
        out_shape=[jax.ShapeDtypeStruct((T, D), F32), jax.ShapeDtypeStruct((T, D), BF16)],
        compiler_params=_params(1))(x, g, y, gain, sc, sh)


def norm_res_bwd(x, gain, sc, dh, dres, g, y, name):
    T, D = x.shape
    tb = _rows(T, 512)

    def body(x_ref, gn_ref, sc_ref, dh_ref, dres_ref, g_ref, y_ref,
             dx_ref, dgn_ref, dsc_ref, dsh_ref, dy_ref, dg_ref):
        i = pl.program_id(0)
        xv = x_ref[...]
        r = lax.rsqrt(jnp.mean(xv * xv, axis=-1, keepdims=True) + RMS_EPS)
        n = xv * r
        dhv = dh_ref[...].astype(F32)
        gain_v, sc1 = gn_ref[...], 1.0 + sc_ref[...]
        dn = dhv * (gain_v * sc1)
        dx = r * (dn - n * jnp.mean(dn * n, axis=-1, keepdims=True)) + dres_ref[...]
        dx_ref[...] = dx
        dy_ref[...] = (dx * g_ref[...]).astype(BF16)
        dhn = dhv * n

        @pl.when(i == 0)
        def _():
            dgn_ref[...] = jnp.zeros_like(dgn_ref)
            dsc_ref[...] = jnp.zeros_like(dsc_ref)
            dsh_ref[...] = jnp.zeros_like(dsh_ref)
            dg_ref[...] = jnp.zeros_like(dg_ref)

        dgn_ref[...] += jnp.sum(dhn * sc1, axis=0, keepdims=True)
        dsc_ref[...] += jnp.sum(dhn * gain_v, axis=0, keepdims=True)
        dsh_ref[...] += jnp.sum(dhv, axis=0, keepdims=True)
        dg_ref[...] += jnp.sum(dx * y_ref[...], axis=0, keepdims=True)

    blk = pl.BlockSpec((tb, D), lambda i: (i, 0))
    row = _row_spec(D)
    row_shape = jax.ShapeDtypeStruct((1, D), F32)
    return _pcall(
        body, name=name, grid=(T // tb,),
        in_specs=[blk, row, row, blk, blk, row, blk],
        out_specs=[blk, row, row, row, blk, row],
        out_shape=[jax.ShapeDtypeStruct((T, D), F32), row_shape, row_shape, row_shape,
                   jax.ShapeDtypeStruct((T, D), BF16), row_shape],
        compiler_params=_params(1))(x, gain, sc, dh, dres, g, y)


def loss_head(y, target, name):
    T, D = y.shape
    tb = _rows(T, 512)

    def body(y_ref, t_ref, l_ref, dy_ref):
        i = pl.program_id(0)
        err = y_ref[...] - t_ref[...]
        dy_ref[...] = err * (1.0 / D)

        @pl.when(i == 0)
        def _():
            l_ref[...] = jnp.zeros_like(l_ref)

        sq = jnp.sum(err * err, axis=0, keepdims=True)
        tot = sq[:, 0:LANE]
        for k in range(1, D // LANE):
            tot = tot + sq[:, k * LANE:(k + 1) * LANE]
        l_ref[...] += tot

    blk = pl.BlockSpec((tb, D), lambda i: (i, 0))
    return _pcall(
        body, name=name, grid=(T // tb,), in_specs=[blk, blk],
        out_specs=[_row_spec(LANE), blk],
        out_shape=[jax.ShapeDtypeStruct((1, LANE), F32), jax.ShapeDtypeStruct((T, D), F32)],
        compiler_params=_params(1))(y, target)


def _back(ext, s):
    return ext if s == 0 else pltpu.roll(ext, s, 0)


def _ahead(ext, s):
    return ext if s == 0 else pltpu.roll(ext, ext.shape[0] - s, 0)


def _halo_prev(tb, h):
    return lambda i, j: (jnp.maximum(i * (tb // h) - 1, 0), j)


def _halo_next(tb, h, nrb):
    return lambda i, j: (jnp.minimum(i + 1, nrb - 1) * (tb // h), j)


FFN_TB, FFN_CB = 256, 1408
HALO16 = 16


def ffn_mid_fwd(up, conv_w8, conv_b, name):
    T, F2 = up.shape
    Fd = F2 // 2
    tb, cb = _rows(T, FFN_TB), _tile(Fd, FFN_CB)
    ncb = Fd // cb
    H = HALO16

    def body(g_ref, gp_ref, v_ref, w_ref, b_ref, o_ref):
        i = pl.program_id(0)
        g = g_ref[...].astype(F32)
        prev = jnp.where(i > 0, gp_ref[...].astype(F32), 0.0)
        ext = jnp.concatenate([prev, g], axis=0)
        w = w_ref[...]
        gc = w[2:3] * g + w[1:2] * _back(ext, 1)[H:] + w[0:1] * _back(ext, 2)[H:] + b_ref[...]
        o_ref[...] = (_silu(gc) * v_ref[...].astype(F32)).astype(BF16)

    return _pcall(
        body, name=name, grid=(T // tb, ncb),
        in_specs=[pl.BlockSpec((tb, cb), lambda i, j: (i, j)),
                  pl.BlockSpec((H, cb), _halo_prev(tb, H)),
                  pl.BlockSpec((tb, cb), lambda i, j: (i, j + ncb)),
                  pl.BlockSpec((SUBLANE, cb), lambda i, j: (0, j)),
                  pl.BlockSpec((1, cb), lambda i, j: (0, j))],
        out_specs=pl.BlockSpec((tb, cb), lambda i, j: (i, j)),
        out_shape=jax.ShapeDtypeStruct((T, Fd), BF16),
        long_call=True)(up, up, up, conv_w8, conv_b)


def ffn_mid_bwd(up, conv_w8, conv_b, dact, name):
    T, F2 = up.shape
    Fd = F2 // 2
    tb, cb = _rows(T, FFN_TB), _tile(Fd, FFN_CB)
    ncb, nrb = Fd // cb, T // tb
    H = HALO16

    def body(g_ref, gp_ref, gn_ref, v_ref, vn_ref, d_ref, dn_ref, w_ref, b_ref,
             dg_ref, dv_ref, dw_ref, db_ref):
        i = pl.program_id(1)
        g = g_ref[...].astype(F32)
        prev = jnp.where(i > 0, gp_ref[...].astype(F32), 0.0)
        ext = jnp.concatenate([prev, g, gn_ref[...].astype(F32)], axis=0)
        w = w_ref[...]
        e1, e2 = _back(ext, 1), _back(ext, 2)
        gc = (w[2:3] * ext + w[1:2] * e1 + w[0:1] * e2 + b_ref[...])[H:]
        val = jnp.concatenate([v_ref[...], vn_ref[...]], axis=0).astype(F32)
        dnext = jnp.where(i < nrb - 1, dn_ref[...].astype(F32), 0.0)
        da = jnp.concatenate([d_ref[...].astype(F32), dnext], axis=0)
        sg = _sigmoid(gc)
        dv_ref[...] = (da * gc * sg)[:tb].astype(BF16)
        dgc = da * val * (sg * (1.0 + gc * (1.0 - sg)))
        dg_ref[...] = (w[2:3] * dgc + w[1:2] * _ahead(dgc, 1) + w[0:1] * _ahead(dgc, 2))[:tb].astype(BF16)
        dc = dgc[:tb]

        @pl.when(i == 0)
        def _():
            dw_ref[...] = jnp.zeros_like(dw_ref)
            db_ref[...] = jnp.zeros_like(db_ref)

        dw_ref[2:3, :] += jnp.sum(dc * g, axis=0, keepdims=True)
        dw_ref[1:2, :] += jnp.sum(dc * e1[H:H + tb], axis=0, keepdims=True)
        dw_ref[0:1, :] += jnp.sum(dc * e2[H:H + tb], axis=0, keepdims=True)
        db_ref[...] += jnp.sum(dc, axis=0, keepdims=True)

    cur = lambda j, i: (i, j)
    prv = lambda j, i: _halo_prev(tb, H)(i, j)
    nxt = lambda j, i: _halo_next(tb, H, nrb)(i, j)
    return _pcall(
        body, name=name, grid=(ncb, nrb),
        in_specs=[pl.BlockSpec((tb, cb), cur), pl.BlockSpec((H, cb), prv), pl.BlockSpec((H, cb), nxt),
                  pl.BlockSpec((tb, cb), lambda j, i: (i, j + ncb)),
                  pl.BlockSpec((H, cb), lambda j, i: (jnp.minimum(i + 1, nrb - 1) * (tb // H), j + ncb)),
                  pl.BlockSpec((tb, cb), cur), pl.BlockSpec((H, cb), nxt),
                  pl.BlockSpec((SUBLANE, cb), lambda j, i: (0, j)),
                  pl.BlockSpec((1, cb), lambda j, i: (0, j))],
        out_specs=[pl.BlockSpec((tb, cb), cur), pl.BlockSpec((tb, cb), cur),
                   pl.BlockSpec((SUBLANE, cb), lambda j, i: (0, j)),
                   pl.BlockSpec((1, cb), lambda j, i: (0, j))],
        out_shape=[jax.ShapeDtypeStruct((T, Fd), BF16), jax.ShapeDtypeStruct((T, Fd), BF16),
                   jax.ShapeDtypeStruct((SUBLANE, Fd), F32), jax.ShapeDtypeStruct((1, Fd), F32)],
        long_call=True)(up, up, up, up, up, dact, dact, conv_w8, conv_b)


GDN_W = GDN_H * HD


def _head_l2norm(a, apply):
    parts = []
    for h in range(GDN_H):
        ah = a[:, h * HD:(h + 1) * HD]
        parts.append(ah * lax.rsqrt(jnp.sum(ah * ah, axis=-1, keepdims=True) + RMS_EPS))
    return jnp.where(apply, jnp.concatenate(parts, axis=1), a)


def _head_l2norm_bwd(a, dy, apply):
    parts = []
    for h in range(GDN_H):
        sl = slice(h * HD, (h + 1) * HD)
        ah, dh = a[:, sl], dy[:, sl]
        r = lax.rsqrt(jnp.sum(ah * ah, axis=-1, keepdims=True) + RMS_EPS)
        y = ah * r
        parts.append(r * (dh - y * jnp.sum(dh * y, axis=-1, keepdims=True)))
    return jnp.where(apply, jnp.concatenate(parts, axis=1), dy)


def gdn_conv_fwd(proj, w8, name):
    T = proj.shape[0]
    tb = _rows(T, 512)
    H = SUBLANE

    def body(x_ref, xp_ref, w_ref, o_ref):
        i, j = pl.program_id(0), pl.program_id(1)
        x = x_ref[...]
        prev = jnp.where(i > 0, xp_ref[...], 0.0)
        ext = jnp.concatenate([prev, x], axis=0)
        w = w_ref[...]
        c = (w[3:4] * x + w[2:3] * _back(ext, 1)[H:] + w[1:2] * _back(ext, 2)[H:]
             + w[0:1] * _back(ext, 3)[H:])
        o_ref[...] = _head_l2norm(_silu(c), j < 2)

    return _pcall(
        body, name=name, grid=(T // tb, 3),
        in_specs=[pl.BlockSpec((tb, GDN_W), lambda i, j: (i, j)),
                  pl.BlockSpec((H, GDN_W), _halo_prev(tb, H)),
                  pl.BlockSpec((SUBLANE, GDN_W), lambda i, j: (0, j))],
        out_specs=pl.BlockSpec((tb, GDN_W), lambda i, j: (i, j)),
        out_shape=jax.ShapeDtypeStruct((T, 3 * GDN_W), F32),
        compiler_params=_params(2))(proj, proj, w8)


def gdn_conv_bwd(proj, w8, dout, name):
    T = proj.shape[0]
    tb = _rows(T, 512)
    nrb = T // tb
    H = SUBLANE

    def body(x_ref, xp_ref, xn_ref, d_ref, dn_ref, w_ref, dx_ref, dw_ref):
        j, i = pl.program_id(0), pl.program_id(1)
        x = x_ref[...]
        prev = jnp.where(i > 0, xp_ref[...], 0.0)
        ext = jnp.concatenate([prev, x, xn_ref[...]], axis=0)
        w = w_ref[...]
        e1, e2, e3 = _back(ext, 1), _back(ext, 2), _back(ext, 3)
        c = (w[3:4] * ext + w[2:3] * e1 + w[1:2] * e2 + w[0:1] * e3)[H:]
        sg = _sigmoid(c)
        dnext = jnp.where(i < nrb - 1, dn_ref[...], 0.0)
        do = jnp.concatenate([d_ref[...], dnext], axis=0)
        da = _head_l2norm_bwd(c * sg, do, j < 2)
        dc = da * (sg * (1.0 + c * (1.0 - sg)))
        dx_ref[...] = (w[3:4] * dc + w[2:3] * _ahead(dc, 1) + w[1:2] * _ahead(dc, 2)
                       + w[0:1] * _ahead(dc, 3))[:tb].astype(BF16)
        dcc = dc[:tb]

        @pl.when(i == 0)
        def _():
            dw_ref[...] = jnp.zeros_like(dw_ref)

        dw_ref[3:4, :] += jnp.sum(dcc * x, axis=0, keepdims=True)
        dw_ref[2:3, :] += jnp.sum(dcc * e1[H:H + tb], axis=0, keepdims=True)
        dw_ref[1:2, :] += jnp.sum(dcc * e2[H:H + tb], axis=0, keepdims=True)
        dw_ref[0:1, :] += jnp.sum(dcc * e3[H:H + tb], axis=0, keepdims=True)

    cur = lambda j, i: (i, j)
    prv = lambda j, i: _halo_prev(tb, H)(i, j)
    nxt = lambda j, i: _halo_next(tb, H, nrb)(i, j)
    return _pcall(
        body, name=name, grid=(3, nrb),
        in_specs=[pl.BlockSpec((tb, GDN_W), cur), pl.BlockSpec((H, GDN_W), prv), pl.BlockSpec((H, GDN_W), nxt),
                  pl.BlockSpec((tb, GDN_W), cur), pl.BlockSpec((H, GDN_W), nxt),
                  pl.BlockSpec((SUBLANE, GDN_W), lambda j, i: (0, j))],
        out_specs=[pl.BlockSpec((tb, GDN_W), cur), pl.BlockSpec((SUBLANE, GDN_W), lambda j, i: (0, j))],
        out_shape=[jax.ShapeDtypeStruct((T, 3 * GDN_W), BF16),
                   jax.ShapeDtypeStruct((SUBLANE, 3 * GDN_W), F32)],
        compiler_params=_params(2))(proj, proj, proj, dout, dout, w8)


def _dot_family(dot, diff):
    if not diff:
        return tuple(functools.partial(lambda d, a, b: dot(a, b, d), d) for d in (BNN, BNT, BTN))

    @jax.custom_vjp
    def nn(a, b):
        return dot(a, b, BNN)
    nn.defvjp(lambda a, b: (dot(a, b, BNN), (a, b)),
              lambda res, g: (dot(g, res[1], BNT), dot(res[0], g, BTN)))

    @jax.custom_vjp
    def nt(a, b):
        return dot(a, b, BNT)
    nt.defvjp(lambda a, b: (dot(a, b, BNT), (a, b)),
              lambda res, g: (dot(g, res[1], BNN), dot(g, res[0], BTN)))

    @jax.custom_vjp
    def tn(a, b):
        return dot(a, b, BTN)
    tn.defvjp(lambda a, b: (dot(a, b, BTN), (a, b)),
              lambda res, g: (dot(res[1], g, BNT), dot(res[0], g, BNN)))
    return nn, nt, tn


def _saved_inverse(hdots):
    _, hnt, htn = hdots

    @jax.custom_vjp
    def inv(L, P):
        return P

    inv.defvjp(lambda L, P: (P, P), lambda P, g: (-hnt(htn(P, g), P), jnp.zeros_like(P)))
    return inv


def _gdn_step(dots, hdots, S, q, k, v, z, b_raw, a_raw, alog, dtb, gnorm, P_saved=None, return_P=False):
    nn, nt, tn = dots
    hnn = hdots[0]
    B, C = q.shape[0], GDN_CHUNK
    ii = lax.broadcasted_iota(jnp.int32, (B, C, C), 1)
    jj = lax.broadcasted_iota(jnp.int32, (B, C, C), 2)
    causal, strict = ii >= jj, ii > jj
    tri, tri_t = causal.astype(F32), (ii <= jj).astype(F32)
    eye, ones = (ii == jj).astype(F32), jnp.ones((B, C, C), F32)

    beta = _sigmoid(b_raw)
    xs = a_raw + dtb
    pos = xs > 0.0
    softplus = jnp.where(pos, xs, 0.0) + jnp.log(1.0 + jnp.exp(jnp.where(pos, -xs, xs)))
    g = -jnp.exp(alog) * softplus
    gb = jnp.broadcast_to(g, (B, C, C))
    gc_c = hnn(tri, gb)
    gc_r = hnn(hnn(ones, eye * gb), tri_t)
    gc = hnn(tri, jnp.broadcast_to(g, (B, C, HD)))
    gl = jnp.sum(g, axis=1, keepdims=True)
    decay = jnp.where(causal, jnp.exp(jnp.where(causal, gc_c - gc_r, 0.0)), 0.0)
    q = q * (HD ** -0.5)
    kb = k * beta
    L = jnp.where(strict, nt(kb, k) * decay, 0.0)
    egc = jnp.exp(gc)
    if P_saved is None:
        P = eye - L
        M = hnn(L, L)
        for step in range(5):
            P = P + hnn(P, M)
            if step < 4:
                M = hnn(M, M)
    else:
        P = _saved_inverse(hdots)(L, P_saved)
    u = hnn(P, v * beta)
    w = hnn(P, kb * egc)
    intra = jnp.where(causal, nt(q, k) * decay, 0.0)
    qg = q * egc
    kdec = k * jnp.exp(gl - gc)
    egl = jnp.exp(gl)
    outs = []
    for ci in range(B // GDN_H):
        sl = slice(ci * GDN_H, (ci + 1) * GDN_H)
        v_new = u[sl] - nn(w[sl], S)
        outs.append(nn(qg[sl], S) + nn(intra[sl], v_new))
        S = S * egl[sl] + tn(kdec[sl], v_new)
    o = jnp.concatenate(outs, axis=0)
    r = lax.rsqrt(jnp.mean(o * o, axis=-1, keepdims=True) + RMS_EPS)
    out = o * r * gnorm * _silu(z)
    return (out, S, P) if return_P else (out, S)


def _gdn_batches(qkv, ba, z, alog_row, dt_row):
    C = GDN_CHUNK
    q, k, v, zz, b_raw, a_raw, alog, dtb = ([] for _ in range(8))
    for ci in range(GDN_STEP):
        rows = slice(ci * C, (ci + 1) * C)
        for h in range(GDN_H):
            q.append(qkv[rows, h * HD:(h + 1) * HD])
            k.append(qkv[rows, GDN_W + h * HD:GDN_W + (h + 1) * HD])
            v.append(qkv[rows, 2 * GDN_W + h * HD:2 * GDN_W + (h + 1) * HD])
            zz.append(z[rows, h * HD:(h + 1) * HD])
            b_raw.append(ba[rows, h:h + 1])
            a_raw.append(ba[rows, GDN_H + h:GDN_H + h + 1])
            alog.append(alog_row[:, h:h + 1])
            dtb.append(dt_row[:, h:h + 1])
    return tuple(jnp.stack(t) for t in (q, k, v, zz, b_raw, a_raw, alog, dtb))


def gdn_chunk_fwd(qkv, proj, alog_row, dt_row, gnorm, name):
    T = qkv.shape[0]
    R = GDN_CHUNK * GDN_STEP
    N = T // R
    B = GDN_STEP * GDN_H
    dots, hdots = _dot_family(_bdot, False), _dot_family(_dot3, False)

    def body(qkv_ref, ba_ref, z_ref, al_ref, dt_ref, gn_ref, o_ref, save_ref, inv_ref, S_ref):
        n = pl.program_id(0)

        @pl.when(n == 0)
        def _():
            S_ref[...] = jnp.zeros_like(S_ref)

        S = S_ref[...]
        save_ref[0] = S
        batches = _gdn_batches(qkv_ref[...], ba_ref[...], z_ref[...], al_ref[...], dt_ref[...])
        o, S_new, P = _gdn_step(dots, hdots, S, *batches, gn_ref[...], return_P=True)
        S_ref[...] = S_new
        inv_ref[0] = P
        for ci in range(GDN_STEP):
            for h in range(GDN_H):
                o_ref[ci * GDN_CHUNK:(ci + 1) * GDN_CHUNK, h * HD:(h + 1) * HD] = o[ci * GDN_H + h].astype(BF16)

    return _pcall(
        body, name=name, grid=(N,),
        in_specs=[pl.BlockSpec((R, 3 * GDN_W), lambda n: (n, 0)),
                  pl.BlockSpec((R, LANE), lambda n: (n, (4 * GDN_W + POOL_G * HD) // LANE)),
                  pl.BlockSpec((R, GDN_W), lambda n: (n, 3)),
                  _row_spec(LANE), _row_spec(LANE), _row_spec(HD)],
        out_specs=[pl.BlockSpec((R, GDN_W), lambda n: (n, 0)),
                   pl.BlockSpec((1, GDN_H, HD, HD), lambda n: (n, 0, 0, 0)),
                   pl.BlockSpec((1, B, GDN_CHUNK, GDN_CHUNK), lambda n: (n, 0, 0, 0))],
        out_shape=[jax.ShapeDtypeStruct((T, GDN_W), BF16), jax.ShapeDtypeStruct((N, GDN_H, HD, HD), F32),
                   jax.ShapeDtypeStruct((N, B, GDN_CHUNK, GDN_CHUNK), F32)],
        scratch_shapes=[pltpu.VMEM((GDN_H, HD, HD), F32)],
        long_call=True)(qkv, proj, proj, alog_row, dt_row, gnorm)


def gdn_chunk_bwd(qkv, proj, alog_row, dt_row, gnorm, saved, inverses, docat, name):
    T = qkv.shape[0]
    C = GDN_CHUNK
    R = C * GDN_STEP
    N = T // R
    B = GDN_STEP * GDN_H
    dots, hdots = _dot_family(_bdot, True), _dot_family(_dot3, True)

    def body(qkv_ref, ba_ref, z_ref, al_ref, dt_ref, gn_ref, save_ref, inv_ref, do_ref,
             dqkv_ref, dz_ref, dba_ref, dal_ref, ddt_ref, dgn_ref, dS_ref):
        n = pl.program_id(0)

        @pl.when(n == 0)
        def _():
            dS_ref[...] = jnp.zeros_like(dS_ref)
            dal_ref[...] = jnp.zeros_like(dal_ref)
            ddt_ref[...] = jnp.zeros_like(ddt_ref)
            dgn_ref[...] = jnp.zeros_like(dgn_ref)

        batches = _gdn_batches(qkv_ref[...], ba_ref[...], z_ref[...], al_ref[...], dt_ref[...])
        do = do_ref[...]
        do_b = jnp.stack([do[ci * C:(ci + 1) * C, h * HD:(h + 1) * HD]
                          for ci in range(GDN_STEP) for h in range(GDN_H)])
        P = inv_ref[0]
        fn = lambda *args: _gdn_step(dots, hdots, *args, P_saved=P)
        _, vjp = jax.vjp(fn, save_ref[0], *batches, gn_ref[...])
        dS, dq, dk, dv, dz, db_raw, da_raw, dalog, ddtb, dgn = vjp((do_b, dS_ref[...]))
        dS_ref[...] = dS
        lane = lax.broadcasted_iota(jnp.int32, (1, LANE), 1)
        dal = jnp.zeros((1, LANE), F32)
        ddt = jnp.zeros((1, LANE), F32)
        for ci in range(GDN_STEP):
            rows = slice(ci * C, (ci + 1) * C)
            dba = jnp.zeros((C, LANE), F32)
            for h in range(GDN_H):
                b = ci * GDN_H + h
                dqkv_ref[rows, h * HD:(h + 1) * HD] = dq[b]
                dqkv_ref[rows, GDN_W + h * HD:GDN_W + (h + 1) * HD] = dk[b]
                dqkv_ref[rows, 2 * GDN_W + h * HD:2 * GDN_W + (h + 1) * HD] = dv[b]
                dz_ref[rows, h * HD:(h + 1) * HD] = dz[b].astype(BF16)
                hot_b = (lane == h).astype(F32)
                dba = dba + db_raw[b] * hot_b + da_raw[b] * (lane == GDN_H + h).astype(F32)
                dal = dal + dalog[b] * hot_b
                ddt = ddt + ddtb[b] * hot_b
            dba_ref[rows, :] = dba.astype(BF16)
        dal_ref[...] += dal
        ddt_ref[...] += ddt
        dgn_ref[...] += dgn

    rev = lambda n: N - 1 - n
    row = jax.ShapeDtypeStruct((1, LANE), F32)
    return _pcall(
        body, name=name, grid=(N,),
        in_specs=[pl.BlockSpec((R, 3 * GDN_W), lambda n: (rev(n), 0)),
                  pl.BlockSpec((R, LANE), lambda n: (rev(n), (4 * GDN_W + POOL_G * HD) // LANE)),
                  pl.BlockSpec((R, GDN_W), lambda n: (rev(n), 3)),
                  _row_spec(LANE), _row_spec(LANE), _row_spec(HD),
                  pl.BlockSpec((1, GDN_H, HD, HD), lambda n: (rev(n), 0, 0, 0)),
                  pl.BlockSpec((1, B, C, C), lambda n: (rev(n), 0, 0, 0)),
                  pl.BlockSpec((R, GDN_W), lambda n: (rev(n), 0))],
        out_specs=[pl.BlockSpec((R, 3 * GDN_W), lambda n: (rev(n), 0)),
                   pl.BlockSpec((R, GDN_W), lambda n: (rev(n), 0)),
                   pl.BlockSpec((R, LANE), lambda n: (rev(n), 0)),
                   _row_spec(LANE), _row_spec(LANE), _row_spec(HD)],
        out_shape=[jax.ShapeDtypeStruct((T, 3 * GDN_W), F32), jax.ShapeDtypeStruct((T, GDN_W), BF16),
                   jax.ShapeDtypeStruct((T, LANE), BF16), row, row, jax.ShapeDtypeStruct((1, HD), F32)],
        scratch_shapes=[pltpu.VMEM((GDN_H, HD, HD), F32)],
        long_call=True)(qkv, proj, proj, alog_row, dt_row, gnorm, saved, inverses, docat)


POOL_HALO = 16


def _pool_pick(j, s2, s4, s8, s16):
    return jnp.where(j == 0, s2, jnp.where(j == 1, s4, jnp.where(j == 2, s8, s16)))


def _pool_count(j, t0, rows):
    t1 = (t0 + 1 + lax.broadcasted_iota(jnp.int32, (rows, 1), 0)).astype(F32)
    win = jnp.where(j == 0, 2.0, jnp.where(j == 1, 4.0, jnp.where(j == 2, 8.0, 16.0)))
    return jnp.minimum(t1, win)


def _pooled(p, prev, i, j, tb):
    ext = jnp.concatenate([prev, p], axis=0)
    s2 = ext + _back(ext, 1)
    s4 = s2 + _back(s2, 2)
    s8 = s4 + _back(s4, 4)
    s16 = s8 + _back(s8, 8)
    s = _pool_pick(j, s2, s4, s8, s16)[POOL_HALO:]
    return s / _pool_count(j, i * tb, tb) - p


def pool_fwd(proj, pool_w, pool_scale, name):
    T = proj.shape[0]
    tb = _rows(T, 512)
    c0 = 4 * GDN_H

    def body(p_ref, pp_ref, w_ref, s_ref, o_ref):
        i, j = pl.program_id(0), pl.program_id(1)
        p = p_ref[...]
        prev = jnp.where(i > 0, pp_ref[...], 0.0)
        pooled = _pooled(p, prev, i, j, tb)
        o_ref[...] = (_bdot(pooled, w_ref[0], NN) * s_ref[...]).astype(BF16)

    return _pcall(
        body, name=name, grid=(T // tb, POOL_G),
        in_specs=[pl.BlockSpec((tb, HD), lambda i, j: (i, c0 + j)),
                  pl.BlockSpec((POOL_HALO, HD), lambda i, j: (jnp.maximum(i * (tb // POOL_HALO) - 1, 0), c0 + j)),
                  pl.BlockSpec((1, HD, HD), lambda i, j: (j, 0, 0)),
                  pl.BlockSpec((1, HD), lambda i, j: (0, j))],
        out_specs=pl.BlockSpec((tb, HD), lambda i, j: (i, j)),
        out_shape=jax.ShapeDtypeStruct((T, POOL_G * HD), BF16),
        compiler_params=_params(2))(proj, proj, pool_w, pool_scale)


def pool_bwd(proj, pool_w, pool_scale, docat, name):
    T = proj.shape[0]
    tb = _rows(T, 512)
    nrb = T // tb
    c0 = 4 * GDN_H
    HB = POOL_HALO

    def body(p_ref, pp_ref, w_ref, s_ref, d_ref, dn_ref, dp_ref, dw_ref, ds_ref):
        j, i = pl.program_id(0), pl.program_id(1)
        p = p_ref[...]
        prev = jnp.where(i > 0, pp_ref[...], 0.0)
        pooled = _pooled(p, prev, i, j, tb)
        w, scale = w_ref[0], s_ref[...]
        dy = d_ref[...]
        dnext = jnp.where(i < nrb - 1, dn_ref[...], 0.0)
        dyp = jnp.concatenate([dy, dnext], axis=0) * scale
        dpooled = _bdot(dyp, w, NT)
        qn = dpooled / _pool_count(j, i * tb, tb + HB)
        a2 = qn + _ahead(qn, 1)
        a4 = a2 + _ahead(a2, 2)
        a8 = a4 + _ahead(a4, 4)
        a16 = a8 + _ahead(a8, 8)
        dp_ref[...] = (_pool_pick(j, a2, a4, a8, a16) - dpooled)[:tb].astype(BF16)

        @pl.when(i == 0)
        def _():
            dw_ref[...] = jnp.zeros_like(dw_ref)
            ds_ref[...] = jnp.zeros_like(ds_ref)

        dw_ref[0] += _bdot(pooled, dyp[:tb], TN)
        ds_ref[...] += jnp.sum(dy * _bdot(pooled, w, NN), axis=0, keepdims=True)

    return _pcall(
        body, name=name, grid=(POOL_G, nrb),
        in_specs=[pl.BlockSpec((tb, HD), lambda j, i: (i, c0 + j)),
                  pl.BlockSpec((HB, HD), lambda j, i: (jnp.maximum(i * (tb // HB) - 1, 0), c0 + j)),
                  pl.BlockSpec((1, HD, HD), lambda j, i: (j, 0, 0)),
                  pl.BlockSpec((1, HD), lambda j, i: (0, j)),
                  pl.BlockSpec((tb, HD), lambda j, i: (i, POOL_G + j)),
                  pl.BlockSpec((HB, HD), lambda j, i: (jnp.minimum(i + 1, nrb - 1) * (tb // HB), POOL_G + j))],
        out_specs=[pl.BlockSpec((tb, HD), lambda j, i: (i, j)),
                   pl.BlockSpec((1, HD, HD), lambda j, i: (j, 0, 0)),
                   pl.BlockSpec((1, HD), lambda j, i: (0, j))],
        out_shape=[jax.ShapeDtypeStruct((T, POOL_G * HD), BF16),
                   jax.ShapeDtypeStruct((POOL_G, HD, HD), F32),
                   jax.ShapeDtypeStruct((1, POOL_G * HD), F32)],
        compiler_params=_params(2))(proj, proj, pool_w, pool_scale, docat, docat)


ATT_W = ATT_H * HD
GROUP_COLS = 3 * ATT_W


def to_residue_major(t, d):
    if d == 1:
        return t
    T, C = t.shape
    return t.reshape(T // d, d, C).transpose(1, 0, 2).reshape(T, C)


def to_token_order(t, d):
    if d == 1:
        return t
    T, C = t.shape
    return t.reshape(d, T // d, C).transpose(1, 0, 2).reshape(T, C)


def headnorm_fwd(proj, qk_gain, name):
    T = proj.shape[0]
    tb = _rows(T, 256)

    def body(x_ref, g_ref, o_ref):
        g = g_ref[...]
        for h in range(2 * ATT_H):
            sl = slice(h * HD, (h + 1) * HD)
            x = x_ref[:, sl].astype(F32)
            n = x * lax.rsqrt(jnp.mean(x * x, axis=-1, keepdims=True) + RMS_EPS)
            gain = g[0:1] * (HD ** -0.5) if h < ATT_H else g[1:2]
            o_ref[:, sl] = (n * gain).astype(BF16)
        o_ref[:, 2 * ATT_W:] = x_ref[:, 2 * ATT_W:]

    blk = pl.BlockSpec((tb, GROUP_COLS), lambda i: (i, 0))
    return _pcall(
        body, name=name, grid=(T // tb,),
        in_specs=[blk, pl.BlockSpec((SUBLANE, HD), lambda i: (0, 0))],
        out_specs=blk, out_shape=jax.ShapeDtypeStruct((T, GROUP_COLS), BF16),
        long_call=True)(proj, qk_gain)


def headnorm_bwd(proj, qk_gain, dq, dk, dv, name):
    T = proj.shape[0]
    tb = _rows(T, 256)

    def body(x_ref, g_ref, dq_ref, dk_ref, dv_ref, dx_ref, dg_ref):
        i = pl.program_id(0)
        g = g_ref[...]

        @pl.when(i == 0)
        def _():
            dg_ref[...] = jnp.zeros_like(dg_ref)

        for part, d_ref in enumerate((dq_ref, dk_ref)):
            gain = g[0:1] * (HD ** -0.5) if part == 0 else g[1:2]
            scale = (HD ** -0.5) if part == 0 else 1.0
            acc = jnp.zeros((1, HD), F32)
            for h in range(ATT_H):
                x = x_ref[:, part * ATT_W + h * HD:part * ATT_W + (h + 1) * HD].astype(F32)
                d = d_ref[:, h * HD:(h + 1) * HD].astype(F32)
                r = lax.rsqrt(jnp.mean(x * x, axis=-1, keepdims=True) + RMS_EPS)
                n = x * r
                dn = d * gain
                dx = r * (dn - n * jnp.mean(dn * n, axis=-1, keepdims=True))
                dx_ref[:, part * ATT_W + h * HD:part * ATT_W + (h + 1) * HD] = dx.astype(BF16)
                acc = acc + jnp.sum(d * n, axis=0, keepdims=True)
            dg_ref[part:part + 1, :] += acc * scale
        dx_ref[:, 2 * ATT_W:] = dv_ref[...]

    blk = pl.BlockSpec((tb, GROUP_COLS), lambda i: (i, 0))
    dblk = pl.BlockSpec((tb, ATT_W), lambda i: (i, 0))
    gspec = pl.BlockSpec((SUBLANE, HD), lambda i: (0, 0))
    return _pcall(
        body, name=name, grid=(T // tb,),
        in_specs=[blk, gspec, dblk, dblk, dblk],
        out_specs=[blk, gspec],
        out_shape=[jax.ShapeDtypeStruct((T, GROUP_COLS), BF16), jax.ShapeDtypeStruct((SUBLANE, HD), F32)],
        long_call=True)(proj, qk_gain, dq, dk, dv)


def _heads(ref):
    return jnp.stack([ref[:, h * HD:(h + 1) * HD] for h in range(ATT_H)])


def _slopes(dil):
    h = lax.broadcasted_iota(jnp.int32, (ATT_H, 1, 1), 0)
    return lax.bitcast_convert_type((126 - h) << 23, F32) * float(dil)


def _att_scores_b(q, k, slope, n_ok, far, keys_first=False):
    r = lax.broadcasted_iota(jnp.int32, (1, ATT_BLK, ATT_BLK), 1)
    c = lax.broadcasted_iota(jnp.int32, (1, ATT_BLK, ATT_BLK), 2)
    a, j = (c, r) if keys_first else (r, c)
    rel = (ATT_BLK + a - j) if far else (a - j)
    mask = ((j >= a) & n_ok) if far else (j <= a)
    s = (_bdot(k, q, BNT) if keys_first else _bdot(q, k, BNT)) - slope * rel.astype(F32)
    return jnp.where(mask, s, NEG), mask


def _att_blocks(nb, width, shift):
    def make(col):
        return pl.BlockSpec((ATT_BLK, width),
                            lambda r, n: (r * nb + jnp.clip(n + shift, 0, nb - 1), col))
    return make


def _lane_col(cols):
    lane = lax.broadcasted_iota(jnp.int32, (1, LANE), 1)
    out = jnp.zeros((ATT_BLK, LANE), F32)
    for h, c in enumerate(cols):
        out = out + c * (lane == h).astype(F32)
    return out


def att_fwd(qkvn, gi, name):
    T = qkvn.shape[0]
    dil = DIL[gi]
    nb = T // dil // ATT_BLK

    def body(q_ref, kp_ref, kc_ref, vp_ref, vc_ref, o_ref, l_ref):
        n_ok = pl.program_id(1) > 0
        slope = _slopes(dil)
        q = _heads(q_ref)
        s_c, _ = _att_scores_b(q, _heads(kc_ref), slope, n_ok, False)
        s_p, _ = _att_scores_b(q, _heads(kp_ref), slope, n_ok, True)
        m = jnp.maximum(jnp.max(s_c, axis=-1, keepdims=True), jnp.max(s_p, axis=-1, keepdims=True))
        p_c, p_p = jnp.exp(s_c - m), jnp.exp(s_p - m)
        l = jnp.sum(p_c, axis=-1, keepdims=True) + jnp.sum(p_p, axis=-1, keepdims=True)
        o = (_bdot(p_c, _heads(vc_ref), BNN) + _bdot(p_p, _heads(vp_ref), BNN)) / l
        lse = m + jnp.log(l)
        for h in range(ATT_H):
            o_ref[:, h * HD:(h + 1) * HD] = o[h].astype(BF16)
        l_ref[...] = _lane_col([lse[h] for h in range(ATT_H)])

    cur, prv = _att_blocks(nb, ATT_W, 0), _att_blocks(nb, ATT_W, -1)
    return _pcall(
        body, name=name, grid=(dil, nb), in_specs=[cur(0), prv(1), cur(1), prv(2), cur(2)],
        out_specs=[cur(0), _att_blocks(nb, LANE, 0)(0)],
        out_shape=[jax.ShapeDtypeStruct((T, ATT_W), BF16), jax.ShapeDtypeStruct((T, LANE), F32)],
        long_call=True)(qkvn, qkvn, qkvn, qkvn, qkvn)


def att_merge(os, lses, name):
    T = os[0].shape[0]
    tb = _rows(T, 512)

    def body(o0, o1, o2, l0, l1, l2, o_ref, l_ref):
        a, b, c = l0[...], l1[...], l2[...]
        m = jnp.maximum(a, jnp.maximum(b, c))
        wa, wb, wc = jnp.exp(a - m), jnp.exp(b - m), jnp.exp(c - m)
        den = wa + wb + wc
        l_ref[...] = m + jnp.log(den)
        wa, wb, wc = wa / den, wb / den, wc / den
        for h in range(ATT_H):
            sl = slice(h * HD, (h + 1) * HD)
            o_ref[:, sl] = (wa[:, h:h + 1] * o0[:, sl].astype(F32) + wb[:, h:h + 1] * o1[:, sl].astype(F32)
                            + wc[:, h:h + 1] * o2[:, sl].astype(F32))

    blk = pl.BlockSpec((tb, ATT_W), lambda i: (i, 0))
    lblk = pl.BlockSpec((tb, LANE), lambda i: (i, 0))
    return _pcall(
        body, name=name, grid=(T // tb,), in_specs=[blk] * 3 + [lblk] * 3, out_specs=[blk, lblk],
        out_shape=[jax.ShapeDtypeStruct((T, ATT_W), F32), jax.ShapeDtypeStruct((T, LANE), F32)],
        compiler_params=_params(1))(*os, *lses)


def att_delta(do, o, name):
    T = do.shape[0]
    tb = _rows(T, 512)

    def body(d_ref, o_ref, out_ref):
        lane = lax.broadcasted_iota(jnp.int32, (1, LANE), 1)
        out = jnp.zeros((tb, LANE), F32)
        for h in range(ATT_H):
            sl = slice(h * HD, (h + 1) * HD)
            s = jnp.sum(d_ref[:, sl].astype(F32) * o_ref[:, sl], axis=-1, keepdims=True)
            out = out + s * (lane == h).astype(F32)
        out_ref[...] = out

    blk = pl.BlockSpec((tb, ATT_W), lambda i: (i, 0))
    return _pcall(
        body, name=name, grid=(T // tb,), in_specs=[blk, blk],
        out_specs=pl.BlockSpec((tb, LANE), lambda i: (i, 0)),
        out_shape=jax.ShapeDtypeStruct((T, LANE), F32), compiler_params=_params(1))(do, o)


def att_bwd_q(qkvn, do, lse, delta, gi, name):
    T = qkvn.shape[0]
    dil = DIL[gi]
    nb = T // dil // ATT_BLK

    def body(q_ref, kp_ref, kc_ref, vp_ref, vc_ref, do_ref, l_ref, d_ref, dq_ref):
        n_ok = pl.program_id(1) > 0
        slope = _slopes(dil)
        lse, dl = l_ref[...], d_ref[...]
        lse = jnp.stack([lse[:, h:h + 1] for h in range(ATT_H)])
        dl = jnp.stack([dl[:, h:h + 1] for h in range(ATT_H)])
        q, do = _heads(q_ref), _heads(do_ref)
        dq = jnp.zeros((ATT_H, ATT_BLK, HD), F32)
        for k_ref, v_ref, far in ((kc_ref, vc_ref, False), (kp_ref, vp_ref, True)):
            k = _heads(k_ref)
            s, mask = _att_scores_b(q, k, slope, n_ok, far)
            p = jnp.where(mask, jnp.exp(s - lse), 0.0)
            ds = p * (_bdot(do, _heads(v_ref), BNT) - dl)
            dq = dq + _bdot(ds, k, BNN)
        for h in range(ATT_H):
            dq_ref[:, h * HD:(h + 1) * HD] = dq[h].astype(BF16)

    cur, prv = _att_blocks(nb, ATT_W, 0), _att_blocks(nb, ATT_W, -1)
    small = _att_blocks(nb, LANE, 0)(0)
    return _pcall(
        body, name=name, grid=(dil, nb),
        in_specs=[cur(0), prv(1), cur(1), prv(2), cur(2), cur(0), small, small],
        out_specs=cur(0), out_shape=jax.ShapeDtypeStruct((T, ATT_W), BF16),
        long_call=True)(qkvn, qkvn, qkvn, qkvn, qkvn, do, lse, delta)


def att_bwd_kv(qkvn, do, lse, delta, gi, name):
    T = qkvn.shape[0]
    dil = DIL[gi]
    nb = T // dil // ATT_BLK

    def body(k_ref, v_ref, q0_ref, q1_ref, do0_ref, do1_ref, l0_ref, l1_ref, d0_ref, d1_ref,
             dk_ref, dv_ref):
        n_ok = pl.program_id(1) < nb - 1
        slope = _slopes(dil)
        by_row = lambda ref: jnp.stack([ref[...].T[h:h + 1, :] for h in range(ATT_H)])
        k, v = _heads(k_ref), _heads(v_ref)
        dk = jnp.zeros((ATT_H, ATT_BLK, HD), F32)
        dv = jnp.zeros((ATT_H, ATT_BLK, HD), F32)
        for q_ref, do_ref, l_ref, d_ref, far in ((q0_ref, do0_ref, l0_ref, d0_ref, False),
                                                 (q1_ref, do1_ref, l1_ref, d1_ref, True)):
            q, do = _heads(q_ref), _heads(do_ref)
            s, mask = _att_scores_b(q, k, slope, n_ok, far, keys_first=True)
            p = jnp.where(mask, jnp.exp(s - by_row(l_ref)), 0.0)
            dv = dv + _bdot(p, do, BNN)
            ds = p * (_bdot(v, do, BNT) - by_row(d_ref))
            dk = dk + _bdot(ds, q, BNN)
        for h in range(ATT_H):
            dk_ref[:, h * HD:(h + 1) * HD] = dk[h].astype(BF16)
            dv_ref[:, h * HD:(h + 1) * HD] = dv[h].astype(BF16)

    cur, nxt = _att_blocks(nb, ATT_W, 0), _att_blocks(nb, ATT_W, 1)
    s0, s1 = _att_blocks(nb, LANE, 0)(0), _att_blocks(nb, LANE, 1)(0)
    return _pcall(
        body, name=name, grid=(dil, nb),
        in_specs=[cur(1), cur(2), cur(0), nxt(0), cur(0), nxt(0), s0, s1, s0, s1],
        out_specs=[cur(0), cur(0)], out_shape=[jax.ShapeDtypeStruct((T, ATT_W), BF16)] * 2,
        long_call=True)(qkvn, qkvn, qkvn, qkvn, do, do, lse, lse, delta, delta)


def adamw(w, g, m, v, name):
    shape = w.shape
    C = shape[-1]
    R = math.prod(shape[:-1])
    to2d = lambda t: t.reshape(R, C)
    tb = _rows(R, max(16, (256 * 1536 // C) // 16 * 16))
    c1 = 1.0 - ADAM_B1 ** ADAM_STEP
    c2 = 1.0 - ADAM_B2 ** ADAM_STEP

    def body(w_ref, g_ref, m_ref, v_ref, d_ref, nm_ref, nv_ref):
        gv = g_ref[...]
        nm = ADAM_B1 * m_ref[...] + (1.0 - ADAM_B1) * gv
        nv = ADAM_B2 * v_ref[...] + (1.0 - ADAM_B2) * (gv * gv)
        d_ref[...] = -ADAM_LR * ((nm / c1) / (jnp.sqrt(nv / c2) + ADAM_EPS) + ADAM_WD * w_ref[...])
        nm_ref[...] = nm
        nv_ref[...] = nv

    blk = pl.BlockSpec((tb, C), lambda i: (i, 0))
    out = jax.ShapeDtypeStruct((R, C), F32)
    d, nm, nv = _pcall(
        body, name=name, grid=(R // tb,), in_specs=[blk] * 4, out_specs=[blk] * 3,
        out_shape=[out, out, out], long_call=R * C >= (1 << 21))(to2d(w), to2d(g), to2d(m), to2d(v))
    return d.reshape(shape), nm.reshape(shape), nv.reshape(shape)


def _pad_rows8(w):
    return jnp.pad(w, ((0, SUBLANE - w.shape[0]), (0, 0)))


def _lane_row(v):
    return jnp.pad(v, (0, LANE - v.shape[0]))[None, :]


def _even_reorder(w_in):
    z4 = 4 * GDN_W
    pad = jnp.zeros((w_in.shape[0], EVEN_PAD - EVEN_COLS), w_in.dtype)
    return jnp.concatenate([w_in[:, :z4], w_in[:, z4 + 2 * GDN_H:], w_in[:, z4:z4 + 2 * GDN_H], pad], axis=1)


def _even_restore(dw):
    z4 = 4 * GDN_W
    p4 = POOL_G * HD
    return jnp.concatenate([dw[:, :z4], dw[:, z4 + p4:z4 + p4 + 2 * GDN_H], dw[:, z4:z4 + p4]], axis=1)


def _ffn_fwd(tag, hf, wl):
    up = matmul(hf, wl["ffn_w_up"], "nn", BF16, f"{tag}_ffn_up")
    act = ffn_mid_fwd(up, wl["ffn_conv_w8"], wl["ffn_conv_b"], f"{tag}_ffn_mid")
    f = matmul(act, wl["ffn_w_down"], "nn", F32, f"{tag}_ffn_down")
    return f, (hf, up, act)


def _ffn_bwd(tag, df, saved, wl):
    hf, up, act = saved
    dact = matmul(df, wl["ffn_w_down"], "nt", BF16, f"{tag}_ffn_down_da")
    wl["on_grad"]("ffn_w_down", matmul(act, df, "tn", F32, f"{tag}_ffn_down_dw", pieces="row"))
    dgate, dval, dcw, dcb = ffn_mid_bwd(up, wl["ffn_conv_w8"], wl["ffn_conv_b"], dact, f"{tag}_ffn_mid_bwd")
    dup = jnp.concatenate([dgate, dval], axis=1)
    dhf = matmul(dup, wl["ffn_w_up"], "nt", F32, f"{tag}_ffn_up_da")
    wl["on_grad"]("ffn_w_up", matmul(hf, dup, "tn", F32, f"{tag}_ffn_up_dw", pieces="col"))
    return dhf, {"ffn_conv_w": dcw[:FFN_CONV], "ffn_conv_b": dcb[0]}


def _even_fwd(tag, hm, wl):
    proj = matmul(hm, wl["w_in"], "nn", F32, f"{tag}_ev_in")
    qkv = gdn_conv_fwd(proj, wl["gdn_conv_w8"], f"{tag}_gdn_conv")
    o_a, *states = gdn_chunk_fwd(qkv, proj, wl["alog_row"], wl["dt_row"], wl["gdn_norm"], f"{tag}_gdn_chunk")
    o_b = pool_fwd(proj, wl["pool_w"], wl["pool_scale"], f"{tag}_pool")
    ocat = jnp.concatenate([o_a, o_b], axis=1)
    y = matmul(ocat, wl["w_out"], "nn", F32, f"{tag}_ev_out")
    return y, (hm, proj, qkv, states, ocat)


def _even_bwd(tag, dy, saved, wl):
    hm, proj, qkv, states, ocat = saved
    docat = matmul(dy, wl["w_out"], "nt", F32, f"{tag}_ev_out_da")
    wl["on_grad"]("ev_w_out", matmul(ocat, dy, "tn", F32, f"{tag}_ev_out_dw", pieces="row"))
    dqkv, dz, dba, dalog, ddt, dgn = gdn_chunk_bwd(
        qkv, proj, wl["alog_row"], wl["dt_row"], wl["gdn_norm"], *states, docat, f"{tag}_gdn_chunk_bwd")
    dxc, dconv = gdn_conv_bwd(proj, wl["gdn_conv_w8"], dqkv, f"{tag}_gdn_conv_bwd")
    dp, dpw, dps = pool_bwd(proj, wl["pool_w"], wl["pool_scale"], docat, f"{tag}_pool_bwd")
    dproj = jnp.concatenate([dxc, dz, dp, dba], axis=1)
    dhm = matmul(dproj, wl["w_in"], "nt", F32, f"{tag}_ev_in_da")
    wl["on_grad"]("ev_w_in", col_pieces(_even_restore(matmul(hm, dproj, "tn", F32, f"{tag}_ev_in_dw"))))
    return dhm, {"gdn_conv_w": dconv[:GDN_CONV], "gdn_a_log": dalog[0, :GDN_H], "gdn_dt_bias": ddt[0, :GDN_H],
                 "gdn_norm": dgn[0], "pool_w": dpw, "pool_scale": dps[0]}


def _odd_fwd(tag, hm, wl):
    projs, qkvns, outs, lses = [], [], [], []
    for gi, d in enumerate(DIL):
        w_g = wl["w_in"][:, gi * GROUP_COLS:(gi + 1) * GROUP_COLS]
        proj = matmul(to_residue_major(hm, d), w_g, "nn", BF16, f"{tag}_od_in{gi}")
        qkvn = headnorm_fwd(proj, wl["qk_gain8"], f"{tag}_headnorm{gi}")
        o_g, l_g = att_fwd(qkvn, gi, f"{tag}_att{gi}")
        projs.append(proj)
        qkvns.append(qkvn)
        outs.append(to_token_order(o_g, d))
        lses.append(to_token_order(l_g, d))
    o, lse = att_merge(outs, lses, f"{tag}_att_merge")
    y = matmul(o, wl["w_out"], "nn", F32, f"{tag}_od_out")
    return y, (hm, projs, qkvns, o, lse)


def _odd_bwd(tag, dy, saved, wl):
    hm, projs, qkvns, o, lse = saved
    do = matmul(dy, wl["w_out"], "nt", BF16, f"{tag}_od_out_da")
    wl["on_grad"]("od_w_out", matmul(o, dy, "tn", F32, f"{tag}_od_out_dw", pieces="row"))
    delta = att_delta(do, o, f"{tag}_att_delta")
    dhm, dw_in, dgain_qk = None, [], None
    for gi, d in enumerate(DIL):
        w_g = wl["w_in"][:, gi * GROUP_COLS:(gi + 1) * GROUP_COLS]
        do_g, lse_g, dl_g = (to_residue_major(t, d) for t in (do, lse, delta))
        dq = att_bwd_q(qkvns[gi], do_g, lse_g, dl_g, gi, f"{tag}_att{gi}_dq")
        dk, dv = att_bwd_kv(qkvns[gi], do_g, lse_g, dl_g, gi, f"{tag}_att{gi}_dkv")
        dproj, dgain = headnorm_bwd(projs[gi], wl["qk_gain8"], dq, dk, dv, f"{tag}_headnorm{gi}_bwd")
        dhm_g = to_token_order(matmul(dproj, w_g, "nt", F32, f"{tag}_od_in{gi}_da"), d)
        dw_in.append(matmul(to_residue_major(hm, d), dproj, "tn", F32, f"{tag}_od_in{gi}_dw"))
        dhm = dhm_g if dhm is None else dhm + dhm_g
        dgain_qk = dgain if dgain_qk is None else dgain_qk + dgain
    wl["on_grad"]("od_w_in", col_pieces(jnp.concatenate(dw_in, axis=1)))
    return dhm, {"att_q_norm": dgain_qk[0], "att_k_norm": dgain_qk[1]}


def col_pieces(dw):
    M, N = dw.shape
    return dw.reshape(2, M // 2, N_CHIPS, N // N_CHIPS).transpose(0, 2, 1, 3)


class _Lazy:
    def __init__(self, fn):
        self.fn, self.value = fn, None


class _LayerWeights(dict):
    def __getitem__(self, key):
        v = dict.__getitem__(self, key)
        if isinstance(v, _Lazy):
            if v.value is None:
                v.value = v.fn()
            return v.value
        return v


def _layer_weights(i, W, big, on_grad):
    e = i // 2
    wl = _LayerWeights({
        "norm_mix": W["norm_mix"][i][None, :], "norm_ffn": W["norm_ffn"][i][None, :],
        "ffn_w_up": _Lazy(lambda: big("ffn_w_up", i)), "ffn_w_down": _Lazy(lambda: big("ffn_w_down", i)),
        "ffn_conv_w8": _pad_rows8(W["ffn_conv_w"][i]), "ffn_conv_b": W["ffn_conv_b"][i][None, :],
        "on_grad": lambda name, pieces: on_grad(name, i if name.startswith("ffn") else e, pieces)})
    if i % 2 == 0:
        wl.update({"w_in": _Lazy(lambda: _even_reorder(big("ev_w_in", e))),
                   "w_out": _Lazy(lambda: big("ev_w_out", e)),
                   "gdn_conv_w8": _pad_rows8(W["gdn_conv_w"][e]),
                   "alog_row": _lane_row(W["gdn_a_log"][e]), "dt_row": _lane_row(W["gdn_dt_bias"][e]),
                   "gdn_norm": W["gdn_norm"][e][None, :], "pool_w": W["pool_w"][e],
                   "pool_scale": W["pool_scale"][e][None, :]})
    else:
        wl.update({"w_in": _Lazy(lambda: big("od_w_in", e)), "w_out": _Lazy(lambda: big("od_w_out", e)),
                   "qk_gain8": _pad_rows8(jnp.stack([W["att_q_norm"][e], W["att_k_norm"][e]]))})
    return wl


def local_step(x, target, mod, W, big, on_grad):
    depth = mod.shape[0]
    row = lambda i, k: mod[i, k][None, :]
    saved, wls = [], []
    pending = None
    for i in range(depth):
        tag = f"l{i}"
        wl = _layer_weights(i, W, big, on_grad)
        if pending is None:
            hm = modnorm_fwd(x, wl["norm_mix"], row(i, 1), row(i, 0), f"{tag}_mix_norm")
        else:
            x, hm = res_norm_fwd(x, *pending, wl["norm_mix"], row(i, 1), row(i, 0), f"{tag}_mix_norm")
        y, s_mix = (_even_fwd if i % 2 == 0 else _odd_fwd)(tag, hm, wl)
        x1, hf = res_norm_fwd(x, row(i, 2), y, wl["norm_ffn"], row(i, 4), row(i, 3), f"{tag}_ffn_norm")
        f, s_ffn = _ffn_fwd(tag, hf, wl)
        saved.append((x, y, s_mix, x1, f, s_ffn))
        wls.append(wl)
        x, pending = x1, (row(i, 5), f)
    sq, dx = loss_head(gres_fwd(x, *pending, "last_res"), target, "loss_head")

    dmod, grads = [None] * depth, [None] * depth
    df, dg_f = gres_bwd(dx, *pending, "last_res_bwd")
    for i in reversed(range(depth)):
        tag = f"l{i}"
        x, y, s_mix, x1, f, s_ffn = saved[i]
        dhf, g_ffn = _ffn_bwd(tag, df, s_ffn, wls[i])
        dx, dgain_f, dsc_f, dsh_f, dy, dg_m = norm_res_bwd(
            x1, wls[i]["norm_ffn"], row(i, 4), dhf, dx, row(i, 2), y, f"{tag}_ffn_norm_bwd")
        dhm, g_mix = (_even_bwd if i % 2 == 0 else _odd_bwd)(tag, dy, s_mix, wls[i])
        dmod_f = [dsh_f, dsc_f, dg_f]
        if i > 0:
            dx, dgain_m, dsc_m, dsh_m, df, dg_f = norm_res_bwd(
                x, wls[i]["norm_mix"], row(i, 1), dhm, dx, row(i - 1, 5), saved[i - 1][4], f"{tag}_mix_norm_bwd")
        else:
            dx, dgain_m, dsc_m, dsh_m = modnorm_bwd(x, wls[i]["norm_mix"], row(i, 1), dhm, dx, f"{tag}_mix_norm_bwd")
        dmod[i] = jnp.concatenate([dsh_m, dsc_m, dg_m] + dmod_f, axis=0)
        grads[i] = {"norm_mix": dgain_m[0], "norm_ffn": dgain_f[0], **g_mix, **g_ffn}
    return sq, dx, jnp.stack(dmod), grads


def _place():
    return lax.axis_index("x"), lax.axis_index("y"), lax.axis_index("c")


def _other_chips(mx, my):
    return [(1 - mx, my), (mx, 1 - my), (1 - mx, 1 - my)]


def _sems(n):
    return [DMA_SEM((n,)), DMA_SEM((n,))]


def _put(buf, block, index):
    return lax.dynamic_update_index_in_dim(buf, block, index, 0)


def gather8_ride(x, then):
    def parts(ins, outs, sems):
        (x_ref,), (out_ref,), (send_sems, recv_sems) = ins, outs, sems
        mx, my, mc = _place()
        me, sibling = (mx, my, mc), (mx, my, 1 - mc)
        chips = _other_chips(mx, my)

        def slot(px, py, pc):
            return out_ref.at[4 * px + 2 * py + pc]

        def copy(k, block, to, src=None):
            return pltpu.make_async_remote_copy(
                src_ref=slot(*block) if src is None else src, dst_ref=slot(*block),
                send_sem=send_sems.at[k], recv_sem=recv_sems.at[k], device_id=to, device_id_type=MESH)

        first = lambda: ([copy(0, me, sibling, src=x_ref)]
                         + [copy(1 + j, me, (*chip, mc), src=x_ref) for j, chip in enumerate(chips)])
        passed = lambda: [copy(4 + j, (*chip, mc), sibling) for j, chip in enumerate(chips)]
        landed = lambda: [copy(1 + j, (*chip, mc), me) for j, chip in enumerate(chips)]
        from_sibling = lambda: ([copy(0, sibling, me)]
                                + [copy(4 + j, (*chip, 1 - mc), me) for j, chip in enumerate(chips)])
        return first, passed, landed, from_sibling

    def start(ins, outs, sems):
        first, _, _, _ = parts(ins, outs, sems)
        for cp in first():
            cp.start()

    def mid(ins, outs, sems):
        _, passed, landed, _ = parts(ins, outs, sems)
        for cp, fwd in zip(landed(), passed()):
            cp.wait_recv()
            fwd.start()

    def finish(ins, outs, sems):
        first, passed, _, from_sibling = parts(ins, outs, sems)
        for cp in from_sibling():
            cp.wait_recv()
        for cp in first() + passed():
            cp.wait_send()

    def landed_all(outs):
        mx, my, mc = _place()
        then(_put(outs[0], x, 4 * mx + 2 * my + mc))

    return Ride([x], [jax.ShapeDtypeStruct((8,) + x.shape, x.dtype)], _sems(7), start, finish,
                landed_all, mid=mid, heavy=True)


def all_gather8(x):
    box = []
    waiting = list(_RIDES)
    _RIDES[:] = [gather8_ride(x, box.append)]
    flush_rides()
    _RIDES[:] = waiting + _RIDES
    return box[0]


def sibling_halves_ride(p, then):
    def copy(ins, outs, sems):
        (p_ref,), (got_ref,), (send_sems, recv_sems) = ins, outs, sems
        mx, my, mc = _place()
        return pltpu.make_async_remote_copy(src_ref=p_ref.at[1 - mc], dst_ref=got_ref, send_sem=send_sems.at[0],
                                            recv_sem=recv_sems.at[0], device_id=(mx, my, 1 - mc), device_id_type=MESH)

    def start(ins, outs, sems):
        copy(ins, outs, sems).start()

    def finish(ins, outs, sems):
        cp = copy(ins, outs, sems)
        cp.wait_send()
        cp.wait_recv()

    def landed(outs):
        then(lax.dynamic_index_in_dim(p, _place()[2], 0, keepdims=False), outs[0])

    return Ride([p], [jax.ShapeDtypeStruct(p.shape[1:], p.dtype)], _sems(1), start, finish, landed)


def sibling_pair_ride(r, then):
    def copy(ins, outs, sems):
        (r_ref,), (got_ref,), (send_sems, recv_sems) = ins, outs, sems
        mx, my, mc = _place()
        return pltpu.make_async_remote_copy(src_ref=r_ref, dst_ref=got_ref, send_sem=send_sems.at[0],
                                            recv_sem=recv_sems.at[0], device_id=(mx, my, 1 - mc), device_id_type=MESH)

    def start(ins, outs, sems):
        copy(ins, outs, sems).start()

    def finish(ins, outs, sems):
        cp = copy(ins, outs, sems)
        cp.wait_send()
        cp.wait_recv()

    def landed(outs):
        then(jnp.where(_place()[2] == 0, jnp.stack([r, outs[0]]), jnp.stack([outs[0], r])))

    return Ride([r], [jax.ShapeDtypeStruct(r.shape, r.dtype)], _sems(1), start, finish, landed)


def chip_scatter_ride(p, then):
    def parts(ins, outs, sems):
        (p_ref,), (out_ref,), (send_sems, recv_sems) = ins, outs, sems
        mx, my, mc = _place()
        mine = 2 * mx + my
        chips = _other_chips(mx, my)
        sends = [pltpu.make_async_remote_copy(
            src_ref=p_ref.at[2 * chip[0] + chip[1]], dst_ref=out_ref.at[mine], send_sem=send_sems.at[k],
            recv_sem=recv_sems.at[k], device_id=(*chip, mc), device_id_type=MESH) for k, chip in enumerate(chips)]
        recvs = lambda: [pltpu.make_async_remote_copy(
            src_ref=p_ref.at[mine], dst_ref=out_ref.at[2 * chip[0] + chip[1]], send_sem=send_sems.at[k],
            recv_sem=recv_sems.at[k], device_id=(*chip, mc), device_id_type=MESH) for k, chip in enumerate(chips)]
        return sends, recvs

    def start(ins, outs, sems):
        sends, _ = parts(ins, outs, sems)
        for cp in sends:
            cp.start()

    def finish(ins, outs, sems):
        sends, recvs = parts(ins, outs, sems)
        for cp in recvs():
            cp.wait_recv()
        for cp in sends:
            cp.wait_send()

    def landed(outs):
        mx, my, _ = _place()
        mine = 2 * mx + my
        then(_put(outs[0], lax.dynamic_index_in_dim(p, mine, 0, keepdims=False), mine))

    return Ride([p], [jax.ShapeDtypeStruct(p.shape, p.dtype)], _sems(3), start, finish, landed, heavy=True)


def _stream_rows(R, C):
    return _rows(R, max(16, (256 * 1536 // C) // 16 * 16))


def cast_bf16(w, name):
    R, C = w.shape
    tb = _stream_rows(R, C)

    def body(w_ref, o_ref):
        o_ref[...] = w_ref[...].astype(BF16)

    blk = pl.BlockSpec((tb, C), lambda i: (i, 0))
    return _pcall(body, name=name, grid=(R // tb,), in_specs=[blk], out_specs=blk,
                          out_shape=jax.ShapeDtypeStruct((R, C), BF16), compiler_params=_params(1))(w)


def sum_slots(g, name):
    n, R, C = g.shape
    tb = _stream_rows(R, C)

    def body(*refs):
        acc = refs[0][...].astype(F32)
        for r in refs[1:n]:
            acc = acc + r[...].astype(F32)
        refs[n][...] = acc

    specs = [pl.BlockSpec((None, tb, C), functools.partial(lambda k, i: (k, i, 0), k)) for k in range(n)]
    return _pcall(body, name=name, grid=(R // tb,), in_specs=specs,
                          out_specs=pl.BlockSpec((tb, C), lambda i: (i, 0)),
                          out_shape=jax.ShapeDtypeStruct((R, C), F32), compiler_params=_params(1))(*([g] * n))


def add_to_bf16(a, b, name):
    R, C = a.shape
    tb = _stream_rows(R, C)

    def body(a_ref, b_ref, o_ref):
        o_ref[...] = (a_ref[...] + b_ref[...]).astype(BF16)

    blk = pl.BlockSpec((tb, C), lambda i: (i, 0))
    return _pcall(body, name=name, grid=(R // tb,), in_specs=[blk, blk], out_specs=blk,
                          out_shape=jax.ShapeDtypeStruct((R, C), BF16), compiler_params=_params(1))(a, b)


def ada_fwd(c_all, ada_w, bias, name):
    n, D, Cs = ada_w.shape
    tn = _tile(Cs, 512)

    def body(c_ref, w_ref, b_ref, o_ref):
        o_ref[...] = _bdot(_silu(c_ref[...]), w_ref[...], NN) + b_ref[...]

    return _pcall(
        body, name=name, grid=(n, Cs // tn),
        in_specs=[pl.BlockSpec((8, D), lambda l, j: (0, 0)),
                  pl.BlockSpec((None, D, tn), lambda l, j: (l, 0, j)),
                  pl.BlockSpec((None, 1, tn), lambda l, j: (l, 0, j))],
        out_specs=pl.BlockSpec((None, 8, tn), lambda l, j: (l, 0, j)),
        out_shape=jax.ShapeDtypeStruct((n, 8, Cs), F32), compiler_params=_params(2))(c_all, ada_w, bias)


def ada_bwd(c16, dmod16, name):
    n, _, Cs = dmod16.shape
    D = c16.shape[1]
    tn = _tile(Cs, 512)

    def body(c_ref, d_ref, o_ref):
        o_ref[...] = _bdot(_silu(c_ref[...]), d_ref[...], TN)

    return _pcall(
        body, name=name, grid=(n, Cs // tn),
        in_specs=[pl.BlockSpec((16, D), lambda l, j: (0, 0)),
                  pl.BlockSpec((None, 16, tn), lambda l, j: (l, 0, j))],
        out_specs=pl.BlockSpec((None, D, tn), lambda l, j: (l, 0, j)),
        out_shape=jax.ShapeDtypeStruct((n, D, Cs), F32), compiler_params=_params(2))(c16, dmod16)


WEIGHTS = ["ada_w", "ada_b", "norm_mix", "norm_ffn", "ev_w_in", "ev_w_out", "gdn_conv_w", "gdn_a_log",
           "gdn_dt_bias", "gdn_norm", "pool_w", "pool_scale", "od_w_in", "od_w_out", "att_q_norm",
           "att_k_norm", "ffn_w_up", "ffn_conv_w", "ffn_conv_b", "ffn_w_down"]
COL_SHARDED = ("ev_w_in", "od_w_in", "ffn_w_up")
ROW_SHARDED = ("ev_w_out", "od_w_out", "ffn_w_down")


def _pack(parts):
    rows, offs = [], []
    at = 0
    for p in parts:
        flat = p.reshape(-1).astype(F32)
        n = -(-flat.shape[0] // LANE)
        rows.append(jnp.pad(flat, (0, n * LANE - flat.shape[0])).reshape(n, LANE))
        offs.append((at, n))
        at += n
    pad = -at % 16
    if pad:
        rows.append(jnp.zeros((pad, LANE), F32))
    return jnp.concatenate(rows, axis=0), offs


def _unpack(buf, off, shape):
    at, n = off
    lead = buf.shape[:-2]
    flat = buf[..., at:at + n, :].reshape(lead + (n * LANE,))
    return flat[..., :math.prod(shape)].reshape(lead + tuple(shape))


def submit_weight_gather(store, key, shard, col_sharded, mc):
    R, C = shard.shape
    half = lax.dynamic_index_in_dim(shard.reshape(2, R // 2, C), mc, 0, keepdims=False)

    def landed(g):
        g = g.reshape(N_CHIPS, R, C)
        store[key] = g.transpose(1, 0, 2).reshape(R, N_CHIPS * C) if col_sharded else g.reshape(N_CHIPS * R, C)

    submit_ride(gather8_ride(half, landed))


def submit_grad_reduce(store, key, pieces, col_sharded):
    _, _, R, C = pieces.shape
    tag = f"{key[0]}{key[1]}"

    def paired(out):
        store[key] = out.reshape(2 * R, C) if col_sharded else out.transpose(1, 0, 2).reshape(R, 2 * C)

    def scattered(got):
        submit_ride(sibling_pair_ride(sum_slots(got, f"gsum_{tag}"), paired))

    def swapped(keep, got):
        chip_sum = add_to_bf16(keep.reshape(N_CHIPS * R, C), got.reshape(N_CHIPS * R, C), f"gadd_{tag}")
        submit_ride(chip_scatter_ride(chip_sum.reshape(N_CHIPS, R, C), scattered))

    submit_ride(sibling_halves_ride(pieces, swapped))


def kernel(x, c, ada_w, ada_b, norm_mix, norm_ffn, ev_w_in, ev_w_out, gdn_conv_w, gdn_a_log, gdn_dt_bias, gdn_norm, pool_w, pool_scale, od_w_in, od_w_out, att_q_norm, att_k_norm, ffn_w_up, ffn_conv_w, ffn_conv_b, ffn_w_down, loss_target, m_ada_w, m_ada_b, m_norm_mix, m_norm_ffn, m_ev_w_in, m_ev_w_out, m_gdn_conv_w, m_gdn_a_log, m_gdn_dt_bias, m_gdn_norm, m_pool_w, m_pool_scale, m_od_w_in, m_od_w_out, m_att_q_norm, m_att_k_norm, m_ffn_w_up, m_ffn_conv_w, m_ffn_conv_b, m_ffn_w_down, v_ada_w, v_ada_b, v_norm_mix, v_norm_ffn, v_ev_w_in, v_ev_w_out, v_gdn_conv_w, v_gdn_a_log, v_gdn_dt_bias, v_gdn_norm, v_pool_w, v_pool_scale, v_od_w_in, v_od_w_out, v_att_q_norm, v_att_k_norm, v_ffn_w_up, v_ffn_conv_w, v_ffn_conv_b, v_ffn_w_down):
    local = dict(ada_w=ada_w, ada_b=ada_b, norm_mix=norm_mix, norm_ffn=norm_ffn, ev_w_in=ev_w_in,
                 ev_w_out=ev_w_out, gdn_conv_w=gdn_conv_w, gdn_a_log=gdn_a_log, gdn_dt_bias=gdn_dt_bias,
                 gdn_norm=gdn_norm, pool_w=pool_w, pool_scale=pool_scale, od_w_in=od_w_in, od_w_out=od_w_out,
                 att_q_norm=att_q_norm, att_k_norm=att_k_norm, ffn_w_up=ffn_w_up, ffn_conv_w=ffn_conv_w,
                 ffn_conv_b=ffn_conv_b, ffn_w_down=ffn_w_down)
    moments_m = dict(zip(WEIGHTS, (m_ada_w, m_ada_b, m_norm_mix, m_norm_ffn, m_ev_w_in, m_ev_w_out,
                                   m_gdn_conv_w, m_gdn_a_log, m_gdn_dt_bias, m_gdn_norm, m_pool_w, m_pool_scale,
                                   m_od_w_in, m_od_w_out, m_att_q_norm, m_att_k_norm, m_ffn_w_up, m_ffn_conv_w,
                                   m_ffn_conv_b, m_ffn_w_down)))
    moments_v = dict(zip(WEIGHTS, (v_ada_w, v_ada_b, v_norm_mix, v_norm_ffn, v_ev_w_in, v_ev_w_out,
                                   v_gdn_conv_w, v_gdn_a_log, v_gdn_dt_bias, v_gdn_norm, v_pool_w, v_pool_scale,
                                   v_od_w_in, v_od_w_out, v_att_q_norm, v_att_k_norm, v_ffn_w_up, v_ffn_conv_w,
                                   v_ffn_conv_b, v_ffn_w_down)))
    _RIDES.clear()
    _IDS[0] = 0
    mx, my, mc = _place()
    chip = 2 * mx + my
    T, D = x.shape[1], x.shape[2]
    depth = ada_w.shape[0]
    ada_cols = ada_w.shape[2]

    buf, offs = _pack([c, gdn_conv_w, ffn_conv_w])
    gathered = all_gather8(buf)
    c_all = _unpack(gathered, offs[0], (D,))
    by_chip = gathered[0::2]
    gdn_conv_full = jnp.concatenate(list(_unpack(by_chip, offs[1], gdn_conv_w.shape)), axis=-1)
    ffn_conv_full = jnp.concatenate(list(_unpack(by_chip, offs[2], ffn_conv_w.shape)), axis=-1)

    bias = lax.dynamic_slice_in_dim(ada_b, chip * ada_cols, ada_cols, axis=1)[:, None, :]
    mod_part = ada_fwd(c_all, ada_w, bias, "ada_fwd")
    mod_all = all_gather8(mod_part)[0::2]
    mod_all = mod_all.transpose(1, 2, 0, 3).reshape(depth, 8, N_CHIPS * ada_cols)
    mod = lax.dynamic_index_in_dim(mod_all, 4 * mx + 2 * my + mc, 1, keepdims=False).reshape(depth, 6, D)

    full_w, big_grad = {}, {}
    order = []
    for i in range(depth):
        mixer = ("ev_w_in", "ev_w_out") if i % 2 == 0 else ("od_w_in", "od_w_out")
        order += [(name, i // 2) for name in mixer] + [("ffn_w_up", i), ("ffn_w_down", i)]
    shards = {name: cast_bf16(local[name].reshape(-1, local[name].shape[-1]), f"cast_{name}")
              .reshape(local[name].shape) for name in COL_SHARDED + ROW_SHARDED}
    for name, e in order:
        submit_weight_gather(full_w, (name, e), shards[name][e], name in COL_SHARDED, mc)

    def big(name, e):
        flush_rides(until=lambda: (name, e) in full_w)
        return full_w[(name, e)]

    def on_grad(name, e, pieces):
        submit_grad_reduce(big_grad, (name, e), pieces, name in COL_SHARDED)

    W = dict(local)
    W["gdn_conv_w"], W["ffn_conv_w"] = gdn_conv_full, ffn_conv_full
    sq, dx, dmod, grads = local_step(x[0], loss_target[0], mod, W, big, on_grad)
    loss = lax.psum(0.5 * jnp.sum(sq) / D, ("x", "y", "c"))

    small = ["norm_mix", "norm_ffn", "gdn_conv_w", "gdn_a_log", "gdn_dt_bias", "gdn_norm", "pool_w",
             "pool_scale", "att_q_norm", "att_k_norm", "ffn_conv_w", "ffn_conv_b"]
    full = {name: jnp.stack([g[name] for g in grads if name in g]) for name in small}
    grad = {}
    buf, offs = _pack([dmod] + [full[name] for name in small])
    gathered = all_gather8(buf)
    summed = sum_slots(gathered, "sum_small_grads")
    grad["ada_b"] = _unpack(summed, offs[0], ada_b.shape)
    for k, name in enumerate(small):
        grad[name] = _unpack(summed, offs[1 + k], full[name].shape)
    for name, cols in (("gdn_conv_w", gdn_conv_w.shape[-1]), ("ffn_conv_w", ffn_conv_w.shape[-1])):
        grad[name] = lax.dynamic_slice_in_dim(grad[name], chip * cols, cols, axis=2)

    dmod_all = _unpack(gathered, offs[0], (depth, N_CHIPS * ada_cols))
    dmod_mine = lax.dynamic_slice_in_dim(dmod_all, chip * ada_cols, ada_cols, axis=2).transpose(1, 0, 2)
    grad["ada_w"] = ada_bwd(jnp.pad(c_all, ((0, 8), (0, 0))), jnp.pad(dmod_mine, ((0, 0), (0, 8), (0, 0))),
                            "ada_bwd")

    deltas, new_m, new_v = {}, {}, {}
    large = ("ffn_w_down", "ffn_w_up", "od_w_out", "od_w_in", "ev_w_out", "ev_w_in")
    for name in [n for n in WEIGHTS if n not in large] + list(large):
        if name in large:
            keys = [k for k in order if k[0] == name]
            flush_rides(until=lambda: all(k in big_grad for k in keys))
            grad[name] = jnp.stack([big_grad[k] for k in keys])
        deltas[name], new_m[name], new_v[name] = adamw(local[name], grad[name], moments_m[name],
                                                       moments_v[name], f"adamw_{name}")
    flush_rides()
    return (loss, dx[None], *[grad[n] for n in WEIGHTS], *[deltas[n] for n in WEIGHTS],
            *[new_m[n] for n in WEIGHTS], *[new_v[n] for n in WEIGHTS])
```

```python
import functools
import math

import jax
import jax.numpy as jnp
from jax import lax
from jax.experimental import pallas as pl
from jax.experimental.pallas import tpu as pltpu

F32 = jnp.float32
BF16 = jnp.bfloat16
LANE = 128
SUBLANE = 8
VMEM_LIMIT = 56 * 1024 * 1024
MESH = pl.DeviceIdType.MESH
N_CHIPS = 4

RMS_EPS = 1e-6
GDN_H = 4
HD = 128
GDN_CHUNK = 64
GDN_STEP = 4
GDN_CONV = 4
FFN_CONV = 3
POOL_G = 4
ATT_H = 8
ATT_BLK = 128
DIL = (1, 4, 16)
EVEN_COLS = 2568
EVEN_PAD = 2688
ADAM_LR, ADAM_B1, ADAM_B2, ADAM_EPS, ADAM_WD, ADAM_STEP = 0.001, 0.9, 0.999, 1e-08, 0.01, 10
NEG = -1e30

NN = (((1,), (0,)), ((), ()))
NT = (((1,), (1,)), ((), ()))
TN = (((0,), (0,)), ((), ()))
BNN = (((2,), (1,)), ((0,), (0,)))
BNT = (((2,), (2,)), ((0,), (0,)))
BTN = (((1,), (1,)), ((0,), (0,)))


def _params(n_grid):
    return pltpu.CompilerParams(dimension_semantics=("arbitrary",) * n_grid,
                                vmem_limit_bytes=VMEM_LIMIT)


HBM = pl.BlockSpec(memory_space=pltpu.HBM)
DMA_SEM = pltpu.SemaphoreType.DMA


class Ride:
    def __init__(self, inputs, out_shapes, sems, start, finish, then, mid=None, heavy=False):
        self.inputs, self.out_shapes, self.sems = list(inputs), list(out_shapes), list(sems)
        self.start, self.mid, self.finish, self.then, self.heavy = start, mid, finish, then, heavy


_RIDES = []


def submit_ride(ride):
    _RIDES.append(ride)


def flush_rides(until=None):
    while _RIDES and not (until is not None and until()):
        ride = _RIDES.pop(0)

        def body(*refs, ride=ride):
            a, b = len(ride.inputs), len(ride.inputs) + len(ride.out_shapes)
            ride.start(refs[:a], refs[a:b], refs[b:])
            if ride.mid is not None:
                ride.mid(refs[:a], refs[a:b], refs[b:])
            ride.finish(refs[:a], refs[a:b], refs[b:])

        outs = pl.pallas_call(body, name=f"exchange{_next_id()}", in_specs=[HBM] * len(ride.inputs),
                              out_specs=[HBM] * len(ride.out_shapes), out_shape=ride.out_shapes,
                              scratch_shapes=ride.sems)(*ride.inputs)
        ride.then(list(outs))


_IDS = [0]


def _next_id():
    _IDS[0] += 1
    return _IDS[0]


def _pcall(body, *, name, grid, in_specs, out_specs, out_shape, scratch_shapes=(), compiler_params=None,
           long_call=False):
    del compiler_params
    single = not isinstance(out_shape, (list, tuple))
    outs = [out_shape] if single else list(out_shape)
    ospecs = [out_specs] if single else list(out_specs)
    total = math.prod(grid)
    fits = [k for k, r in enumerate(_RIDES) if long_call or not r.heavy] if total > 1 else []
    ride = _RIDES.pop(fits[0]) if fits else None
    if ride is None:
        call = pl.pallas_call(body, name=name, grid=grid, in_specs=list(in_specs), out_specs=ospecs,
                              out_shape=outs, scratch_shapes=list(scratch_shapes),
                              compiler_params=_params(len(grid)))

        def run_plain(*args):
            res = call(*args)
            return res[0] if single else res
        return run_plain

    n_in, n_out, n_scr = len(in_specs), len(outs), len(scratch_shapes)
    r_in, r_out = len(ride.inputs), len(ride.out_shapes)

    def carrying_body(*refs):
        at = 0
        ins = refs[at:at + n_in]; at += n_in
        r_ins = refs[at:at + r_in]; at += r_in
        os_ = refs[at:at + n_out]; at += n_out
        r_outs = refs[at:at + r_out]; at += r_out
        scr = refs[at:at + n_scr]; at += n_scr
        r_sems = refs[at:]
        step = pl.program_id(0)
        for ax in range(1, len(grid)):
            step = step * grid[ax] + pl.program_id(ax)

        @pl.when(step == 0)
        def _():
            ride.start(r_ins, r_outs, r_sems)

        body(*ins, *os_, *scr)

        if ride.mid is not None:
            @pl.when(step == total // 2)
            def _():
                ride.mid(r_ins, r_outs, r_sems)

        @pl.when(step == total - 1)
        def _():
            ride.finish(r_ins, r_outs, r_sems)

    call = pl.pallas_call(
        carrying_body, name=name, grid=grid, in_specs=list(in_specs) + [HBM] * r_in,
        out_specs=ospecs + [HBM] * r_out, out_shape=outs + ride.out_shapes,
        scratch_shapes=list(scratch_shapes) + ride.sems, compiler_params=_params(len(grid)))

    def run_carrying(*args):
        res = call(*args, *ride.inputs)
        ride.then(list(res[n_out:]))
        return res[0] if single else list(res[:n_out])
    return run_carrying


def _tile(n, target):
    if n <= target:
        return n
    best = None
    for t in range(LANE, target + 1, LANE):
        if n % t == 0:
            best = t
    assert best is not None, (n, target)
    return best


def _rows(n, target):
    if n <= target:
        return n
    best = None
    for t in range(16, target + 1, 16):
        if n % t == 0:
            best = t
    assert best is not None, (n, target)
    return best


def _bdot(a, b, dims):
    return lax.dot_general(a.astype(BF16), b.astype(BF16), dims, preferred_element_type=F32)


def _split(a):
    hi = a.astype(BF16)
    return hi, (a - hi.astype(F32)).astype(BF16)


def _dot3(a, b, dims):
    ah, al = _split(a)
    bh, bl = _split(b)
    d = lambda p, q: lax.dot_general(p, q, dims, preferred_element_type=F32)
    return d(ah, bh) + d(ah, bl) + d(al, bh)


def _sigmoid(x):
    return 1.0 / (1.0 + jnp.exp(-x))


def _silu(x):
    return x * _sigmoid(x)


def matmul(a, b, mode, out_dtype, name, tm=1024, tn=1536, tk=1536, pieces=None):
    a_parts = a.shape[0] if a.ndim == 3 else 1
    b_parts = b.shape[0] if b.ndim == 3 else 1
    if mode == "nn":
        (M, K), (K2, N) = a.shape, b.shape
    elif mode == "nt":
        M, K = a.shape[-2], a.shape[-1] * a_parts
        N, K2 = b.shape
    else:
        K, M = a.shape
        K2, N = b.shape[-2], b.shape[-1] * b_parts
    assert K == K2, (a.shape, b.shape, mode)
    if pieces == "col":
        tm, tn = _tile(M // 2, tm), _tile(N // N_CHIPS, tn)
    elif pieces == "row":
        quarter = M // N_CHIPS
        tm = 2 * quarter if (2 * quarter) % LANE == 0 else M
        tn = _tile(N // 2, tn)
    else:
        tm, tn = _tile(M, tm), _tile(N, tn)
    tk = _tile(K // a_parts, tk)
    nk = K // tk
    assert (N // b_parts) % tn == 0, (N, b_parts, tn)
    dims = {"nn": NN, "nt": NT, "tn": TN}[mode]
    if mode == "tn":
        a_spec = pl.BlockSpec((tk, tm), lambda i, j, k: (k, i))
    elif a.ndim == 3:
        nkp = (K // a_parts) // tk
        a_spec = pl.BlockSpec((None, tm, tk), lambda i, j, k: (k // nkp, i, k % nkp))
    else:
        a_spec = pl.BlockSpec((tm, tk), lambda i, j, k: (i, k))
    if mode == "nt":
        b_spec = pl.BlockSpec((tn, tk), lambda i, j, k: (j, k))
    elif b.ndim == 3:
        njp = (N // b_parts) // tn
        b_spec = pl.BlockSpec((None, tk, tn), lambda i, j, k: (j // njp, k, j % njp))
    else:
        b_spec = pl.BlockSpec((tk, tn), lambda i, j, k: (k, j))

    out_spec = pl.BlockSpec((tm, tn), lambda i, j, k: (i, j))
    out_shape = jax.ShapeDtypeStruct((M, N), out_dtype)
    rows_per_slot = tm
    if pieces == "col":
        nih, njc = (M // 2) // tm, (N // N_CHIPS) // tn
        out_spec = pl.BlockSpec((None, None, tm, tn), lambda i, j, k: (i // nih, j // njc, i % nih, j % njc))
        out_shape = jax.ShapeDtypeStruct((2, N_CHIPS, M // 2, N // N_CHIPS), out_dtype)
    elif pieces == "row":
        rows_per_slot = M // N_CHIPS
        njh = (N // 2) // tn
        out_spec = pl.BlockSpec((None, tm // rows_per_slot, rows_per_slot, tn),
                                lambda i, j, k: (j // njh, i, 0, j % njh))
        out_shape = jax.ShapeDtypeStruct((2, N_CHIPS, rows_per_slot, N // 2), out_dtype)

    def store(o_ref, acc):
        if pieces == "row":
            for s in range(tm // rows_per_slot):
                o_ref[s] = acc[s * rows_per_slot:(s + 1) * rows_per_slot, :].astype(out_dtype)
        else:
            o_ref[...] = acc.astype(out_dtype)

    def body(a_ref, b_ref, o_ref, acc_ref):
        k = pl.program_id(2)
        p = _bdot(a_ref[...], b_ref[...], dims)
        if nk == 1:
            store(o_ref, p)
            return

        @pl.when(k == 0)
        def _():
            acc_ref[...] = p

        @pl.when(k > 0)
        def _():
            acc_ref[...] += p

        @pl.when(k == nk - 1)
        def _():
            store(o_ref, acc_ref[...])

    return _pcall(
        body, name=name, grid=(M // tm, N // tn, nk),
        in_specs=[a_spec, b_spec], out_specs=out_spec, out_shape=out_shape,
        scratch_shapes=[pltpu.VMEM((tm, tn), F32)],
        long_call=True)(a, b)


def _row_spec(d):
    return pl.BlockSpec((1, d), lambda i: (0, 0))


def modnorm_fwd(x, gain, sc, sh, name):
    T, D = x.shape
    tb = _rows(T, 512)

    def body(x_ref, g_ref, sc_ref, sh_ref, o_ref):
        xv = x_ref[...]
        r = lax.rsqrt(jnp.mean(xv * xv, axis=-1, keepdims=True) + RMS_EPS)
        o_ref[...] = ((xv * r) * g_ref[...] * (1.0 + sc_ref[...]) + sh_ref[...]).astype(BF16)

    blk = pl.BlockSpec((tb, D), lambda i: (i, 0))
    return _pcall(
        body, name=name, grid=(T // tb,),
        in_specs=[blk, _row_spec(D), _row_spec(D), _row_spec(D)],
        out_specs=blk, out_shape=jax.ShapeDtypeStruct((T, D), BF16),
        compiler_params=_params(1))(x, gain, sc, sh)


def modnorm_bwd(x, gain, sc, dh, dres, name):
    T, D = x.shape
    tb = _rows(T, 512)

    def body(x_ref, g_ref, sc_ref, dh_ref, dres_ref, dx_ref, dg_ref, dsc_ref, dsh_ref):
        i = pl.program_id(0)
        xv = x_ref[...]
        r = lax.rsqrt(jnp.mean(xv * xv, axis=-1, keepdims=True) + RMS_EPS)
        n = xv * r
        dhv = dh_ref[...].astype(F32)
        gain_v, sc1 = g_ref[...], 1.0 + sc_ref[...]
        dn = dhv * (gain_v * sc1)
        dx_ref[...] = r * (dn - n * jnp.mean(dn * n, axis=-1, keepdims=True)) + dres_ref[...]
        dhn = dhv * n

        @pl.when(i == 0)
        def _():
            dg_ref[...] = jnp.zeros_like(dg_ref)
            dsc_ref[...] = jnp.zeros_like(dsc_ref)
            dsh_ref[...] = jnp.zeros_like(dsh_ref)

        dg_ref[...] += jnp.sum(dhn * sc1, axis=0, keepdims=True)
        dsc_ref[...] += jnp.sum(dhn * gain_v, axis=0, keepdims=True)
        dsh_ref[...] += jnp.sum(dhv, axis=0, keepdims=True)

    blk = pl.BlockSpec((tb, D), lambda i: (i, 0))
    row = jax.ShapeDtypeStruct((1, D), F32)
    return _pcall(
        body, name=name, grid=(T // tb,),
        in_specs=[blk, _row_spec(D), _row_spec(D), blk, blk],
        out_specs=[blk, _row_spec(D), _row_spec(D), _row_spec(D)],
        out_shape=[jax.ShapeDtypeStruct((T, D), F32), row, row, row],
        compiler_params=_params(1))(x, gain, sc, dh, dres)


def gres_fwd(x, g, y, name):
    T, D = x.shape
    tb = _rows(T, 512)

    def body(x_ref, g_ref, y_ref, o_ref):
        o_ref[...] = x_ref[...] + g_ref[...] * y_ref[...]

    blk = pl.BlockSpec((tb, D), lambda i: (i, 0))
    return _pcall(
        body, name=name, grid=(T // tb,), in_specs=[blk, _row_spec(D), blk], out_specs=blk,
        out_shape=jax.ShapeDtypeStruct((T, D), F32), compiler_params=_params(1))(x, g, y)


def gres_bwd(dx, g, y, name):
    T, D = dx.shape
    tb = _rows(T, 512)

    def body(dx_ref, g_ref, y_ref, dy_ref, dg_ref):
        i = pl.program_id(0)
        dxv = dx_ref[...]
        dy_ref[...] = (dxv * g_ref[...]).astype(BF16)

        @pl.when(i == 0)
        def _():
            dg_ref[...] = jnp.zeros_like(dg_ref)

        dg_ref[...] += jnp.sum(dxv * y_ref[...], axis=0, keepdims=True)

    blk = pl.BlockSpec((tb, D), lambda i: (i, 0))
    return _pcall(
        body, name=name, grid=(T // tb,), in_specs=[blk, _row_spec(D), blk],
        out_specs=[blk, _row_spec(D)],
        out_shape=[jax.ShapeDtypeStruct((T, D), BF16), jax.ShapeDtypeStruct((1, D), F32)],
        compiler_params=_params(1))(dx, g, y)


def res_norm_fwd(x, g, y, gain, sc, sh, name):
    T, D = x.shape
    tb = _rows(T, 512)

    def body(x_ref, g_ref, y_ref, gn_ref, sc_ref, sh_ref, x1_ref, h_ref):
        xv = x_ref[...] + g_ref[...] * y_ref[...]
        x1_ref[...] = xv
        r = lax.rsqrt(jnp.mean(xv * xv, axis=-1, keepdims=True) + RMS_EPS)
        h_ref[...] = ((xv * r) * gn_ref[...] * (1.0 + sc_ref[...]) + sh_ref[...]).astype(BF16)

    blk = pl.BlockSpec((tb, D), lambda i: (i, 0))
    row = _row_spec(D)
    return _pcall(
        body, name=name, grid=(T // tb,), in_specs=[blk, row, blk, row, row, row], out_specs=[blk, blk],
        out_shape=[jax.ShapeDtypeStruct((T, D), F32), jax.ShapeDtypeStruct((T, D), BF16)],
        compiler_params=_params(1))(x, g, y, gain, sc, sh)


def norm_res_bwd(x, gain, sc, dh, dres, g, y, name):
    T, D = x.shape
    tb = _rows(T, 512)

    def body(x_ref, gn_ref, sc_ref, dh_ref, dres_ref, g_ref, y_ref,
             dx_ref, dgn_ref, dsc_ref, dsh_ref, dy_ref, dg_ref):
        i = pl.program_id(0)
        xv = x_ref[...]
        r = lax.rsqrt(jnp.mean(xv * xv, axis=-1, keepdims=True) + RMS_EPS)
        n = xv * r
        dhv = dh_ref[...].astype(F32)
        gain_v, sc1 = gn_ref[...], 1.0 + sc_ref[...]
        dn = dhv * (gain_v * sc1)
        dx = r * (dn - n * jnp.mean(dn * n, axis=-1, keepdims=True)) + dres_ref[...]
        dx_ref[...] = dx
        dy_ref[...] = (dx * g_ref[...]).astype(BF16)
        dhn = dhv * n

        @pl.when(i == 0)
        def _():
            dgn_ref[...] = jnp.zeros_like(dgn_ref)
            dsc_ref[...] = jnp.zeros_like(dsc_ref)
            dsh_ref[...] = jnp.zeros_like(dsh_ref)
            dg_ref[...] = jnp.zeros_like(dg_ref)

        dgn_ref[...] += jnp.sum(dhn * sc1, axis=0, keepdims=True)
        dsc_ref[...] += jnp.sum(dhn * gain_v, axis=0, keepdims=True)
        dsh_ref[...] += jnp.sum(dhv, axis=0, keepdims=True)
        dg_ref[...] += jnp.sum(dx * y_ref[...], axis=0, keepdims=True)

    blk = pl.BlockSpec((tb, D), lambda i: (i, 0))
    row = _row_spec(D)
    row_shape = jax.ShapeDtypeStruct((1, D), F32)
    return _pcall(
        body, name=name, grid=(T // tb,),
        in_specs=[blk, row, row, blk, blk, row, blk],
        out_specs=[blk, row, row, row, blk, row],
        out_shape=[jax.ShapeDtypeStruct((T, D), F32), row_shape, row_shape, row_shape,
                   jax.ShapeDtypeStruct((T, D), BF16), row_shape],
        compiler_params=_params(1))(x, gain, sc, dh, dres, g, y)


def loss_head(y, target, name):
    T, D = y.shape
    tb = _rows(T, 512)

    def body(y_ref, t_ref, l_ref, dy_ref):
        i = pl.program_id(0)
        err = y_ref[...] - t_ref[...]
        dy_ref[...] = err * (1.0 / D)

        @pl.when(i == 0)
        def _():
            l_ref[...] = jnp.zeros_like(l_ref)

        sq = jnp.sum(err * err, axis=0, keepdims=True)
        tot = sq[:, 0:LANE]
        for k in range(1, D // LANE):
            tot = tot + sq[:, k * LANE:(k + 1) * LANE]
        l_ref[...] += tot

    blk = pl.BlockSpec((tb, D), lambda i: (i, 0))
    return _pcall(
        body, name=name, grid=(T // tb,), in_specs=[blk, blk],
        out_specs=[_row_spec(LANE), blk],
        out_shape=[jax.ShapeDtypeStruct((1, LANE), F32), jax.ShapeDtypeStruct((T, D), F32)],
        compiler_params=_params(1))(y, target)


def _back(ext, s):
    return ext if s == 0 else pltpu.roll(ext, s, 0)


def _ahead(ext, s):
    return ext if s == 0 else pltpu.roll(ext, ext.shape[0] - s, 0)


def _halo_prev(tb, h):
    return lambda i, j: (jnp.maximum(i * (tb // h) - 1, 0), j)


def _halo_next(tb, h, nrb):
    return lambda i, j: (jnp.minimum(i + 1, nrb - 1) * (tb // h), j)


FFN_TB, FFN_CB = 256, 1408
HALO16 = 16


def ffn_mid_fwd(up, conv_w8, conv_b, name):
    T, F2 = up.shape
    Fd = F2 // 2
    tb, cb = _rows(T, FFN_TB), _tile(Fd, FFN_CB)
    ncb = Fd // cb
    H = HALO16

    def body(g_ref, gp_ref, v_ref, w_ref, b_ref, o_ref):
        i = pl.program_id(0)
        g = g_ref[...].astype(F32)
        prev = jnp.where(i > 0, gp_ref[...].astype(F32), 0.0)
        ext = jnp.concatenate([prev, g], axis=0)
        w = w_ref[...]
        gc = w[2:3] * g + w[1:2] * _back(ext, 1)[H:] + w[0:1] * _back(ext, 2)[H:] + b_ref[...]
        o_ref[...] = (_silu(gc) * v_ref[...].astype(F32)).astype(BF16)

    return _pcall(
        body, name=name, grid=(T // tb, ncb),
        in_specs=[pl.BlockSpec((tb, cb), lambda i, j: (i, j)),
                  pl.BlockSpec((H, cb), _halo_prev(tb, H)),
                  pl.BlockSpec((tb, cb), lambda i, j: (i, j + ncb)),
                  pl.BlockSpec((SUBLANE, cb), lambda i, j: (0, j)),
                  pl.BlockSpec((1, cb), lambda i, j: (0, j))],
        out_specs=pl.BlockSpec((tb, cb), lambda i, j: (i, j)),
        out_shape=jax.ShapeDtypeStruct((T, Fd), BF16),
        long_call=True)(up, up, up, conv_w8, conv_b)


def ffn_mid_bwd(up, conv_w8, conv_b, dact, name):
    T, F2 = up.shape
    Fd = F2 // 2
    tb, cb = _rows(T, FFN_TB), _tile(Fd, FFN_CB)
    ncb, nrb = Fd // cb, T // tb
    H = HALO16

    def body(g_ref, gp_ref, gn_ref, v_ref, vn_ref, d_ref, dn_ref, w_ref, b_ref,
             dup_ref, dw_ref, db_ref):
        i = pl.program_id(1)
        g = g_ref[...].astype(F32)
        prev = jnp.where(i > 0, gp_ref[...].astype(F32), 0.0)
        ext = jnp.concatenate([prev, g, gn_ref[...].astype(F32)], axis=0)
        w = w_ref[...]
        e1, e2 = _back(ext, 1), _back(ext, 2)
        gc = (w[2:3] * ext + w[1:2] * e1 + w[0:1] * e2 + b_ref[...])[H:]
        val = jnp.concatenate([v_ref[...], vn_ref[...]], axis=0).astype(F32)
        dnext = jnp.where(i < nrb - 1, dn_ref[...].astype(F32), 0.0)
        da = jnp.concatenate([d_ref[...].astype(F32), dnext], axis=0)
        sg = _sigmoid(gc)
        dup_ref[1] = (da * gc * sg)[:tb].astype(BF16)
        dgc = da * val * (sg * (1.0 + gc * (1.0 - sg)))
        dup_ref[0] = (w[2:3] * dgc + w[1:2] * _ahead(dgc, 1) + w[0:1] * _ahead(dgc, 2))[:tb].astype(BF16)
        dc = dgc[:tb]

        @pl.when(i == 0)
        def _():
            dw_ref[...] = jnp.zeros_like(dw_ref)
            db_ref[...] = jnp.zeros_like(db_ref)

        dw_ref[2:3, :] += jnp.sum(dc * g, axis=0, keepdims=True)
        dw_ref[1:2, :] += jnp.sum(dc * e1[H:H + tb], axis=0, keepdims=True)
        dw_ref[0:1, :] += jnp.sum(dc * e2[H:H + tb], axis=0, keepdims=True)
        db_ref[...] += jnp.sum(dc, axis=0, keepdims=True)

    cur = lambda j, i: (i, j)
    prv = lambda j, i: _halo_prev(tb, H)(i, j)
    nxt = lambda j, i: _halo_next(tb, H, nrb)(i, j)
    return _pcall(
        body, name=name, grid=(ncb, nrb),
        in_specs=[pl.BlockSpec((tb, cb), cur), pl.BlockSpec((H, cb), prv), pl.BlockSpec((H, cb), nxt),
                  pl.BlockSpec((tb, cb), lambda j, i: (i, j + ncb)),
                  pl.BlockSpec((H, cb), lambda j, i: (jnp.minimum(i + 1, nrb - 1) * (tb // H), j + ncb)),
                  pl.BlockSpec((tb, cb), cur), pl.BlockSpec((H, cb), nxt),
                  pl.BlockSpec((SUBLANE, cb), lambda j, i: (0, j)),
                  pl.BlockSpec((1, cb), lambda j, i: (0, j))],
        out_specs=[pl.BlockSpec((2, tb, cb), lambda j, i: (0, i, j)),
                   pl.BlockSpec((SUBLANE, cb), lambda j, i: (0, j)),
                   pl.BlockSpec((1, cb), lambda j, i: (0, j))],
        out_shape=[jax.ShapeDtypeStruct((2, T, Fd), BF16),
                   jax.ShapeDtypeStruct((SUBLANE, Fd), F32), jax.ShapeDtypeStruct((1, Fd), F32)],
        long_call=True)(up, up, up, up, up, dact, dact, conv_w8, conv_b)


GDN_W = GDN_H * HD


def _head_l2norm(a, apply):
    parts = []
    for h in range(GDN_H):
        ah = a[:, h * HD:(h + 1) * HD]
        parts.append(ah * lax.rsqrt(jnp.sum(ah * ah, axis=-1, keepdims=True) + RMS_EPS))
    return jnp.where(apply, jnp.concatenate(parts, axis=1), a)


def _head_l2norm_bwd(a, dy, apply):
    parts = []
    for h in range(GDN_H):
        sl = slice(h * HD, (h + 1) * HD)
        ah, dh = a[:, sl], dy[:, sl]
        r = lax.rsqrt(jnp.sum(ah * ah, axis=-1, keepdims=True) + RMS_EPS)
        y = ah * r
        parts.append(r * (dh - y * jnp.sum(dh * y, axis=-1, keepdims=True)))
    return jnp.where(apply, jnp.concatenate(parts, axis=1), dy)


def gdn_conv_fwd(proj, w8, name):
    T = proj.shape[0]
    tb = _rows(T, 512)
    H = SUBLANE

    def body(x_ref, xp_ref, w_ref, o_ref):
        i, j = pl.program_id(0), pl.program_id(1)
        x = x_ref[...]
        prev = jnp.where(i > 0, xp_ref[...], 0.0)
        ext = jnp.concatenate([prev, x], axis=0)
        w = w_ref[...]
        c = (w[3:4] * x + w[2:3] * _back(ext, 1)[H:] + w[1:2] * _back(ext, 2)[H:]
             + w[0:1] * _back(ext, 3)[H:])
        o_ref[...] = _head_l2norm(_silu(c), j < 2)

    return _pcall(
        body, name=name, grid=(T // tb, 3),
        in_specs=[pl.BlockSpec((tb, GDN_W), lambda i, j: (i, j)),
                  pl.BlockSpec((H, GDN_W), _halo_prev(tb, H)),
                  pl.BlockSpec((SUBLANE, GDN_W), lambda i, j: (0, j))],
        out_specs=pl.BlockSpec((tb, GDN_W), lambda i, j: (i, j)),
        out_shape=jax.ShapeDtypeStruct((T, 3 * GDN_W), F32),
        compiler_params=_params(2))(proj, proj, w8)


def gdn_conv_bwd(proj, w8, dout, name):
    T = proj.shape[0]
    tb = _rows(T, 512)
    nrb = T // tb
    H = SUBLANE

    def body(x_ref, xp_ref, xn_ref, d_ref, dn_ref, w_ref, dx_ref, dw_ref):
        j, i = pl.program_id(0), pl.program_id(1)
        x = x_ref[...]
        prev = jnp.where(i > 0, xp_ref[...], 0.0)
        ext = jnp.concatenate([prev, x, xn_ref[...]], axis=0)
        w = w_ref[...]
        e1, e2, e3 = _back(ext, 1), _back(ext, 2), _back(ext, 3)
        c = (w[3:4] * ext + w[2:3] * e1 + w[1:2] * e2 + w[0:1] * e3)[H:]
        sg = _sigmoid(c)
        dnext = jnp.where(i < nrb - 1, dn_ref[...], 0.0)
        do = jnp.concatenate([d_ref[...], dnext], axis=0)
        da = _head_l2norm_bwd(c * sg, do, j < 2)
        dc = da * (sg * (1.0 + c * (1.0 - sg)))
        dx_ref[...] = (w[3:4] * dc + w[2:3] * _ahead(dc, 1) + w[1:2] * _ahead(dc, 2)
                       + w[0:1] * _ahead(dc, 3))[:tb].astype(BF16)
        dcc = dc[:tb]

        @pl.when(i == 0)
        def _():
            dw_ref[...] = jnp.zeros_like(dw_ref)

        dw_ref[3:4, :] += jnp.sum(dcc * x, axis=0, keepdims=True)
        dw_ref[2:3, :] += jnp.sum(dcc * e1[H:H + tb], axis=0, keepdims=True)
        dw_ref[1:2, :] += jnp.sum(dcc * e2[H:H + tb], axis=0, keepdims=True)
        dw_ref[0:1, :] += jnp.sum(dcc * e3[H:H + tb], axis=0, keepdims=True)

    cur = lambda j, i: (i, j)
    prv = lambda j, i: _halo_prev(tb, H)(i, j)
    nxt = lambda j, i: _halo_next(tb, H, nrb)(i, j)
    return _pcall(
        body, name=name, grid=(3, nrb),
        in_specs=[pl.BlockSpec((tb, GDN_W), cur), pl.BlockSpec((H, GDN_W), prv), pl.BlockSpec((H, GDN_W), nxt),
                  pl.BlockSpec((tb, GDN_W), cur), pl.BlockSpec((H, GDN_W), nxt),
                  pl.BlockSpec((SUBLANE, GDN_W), lambda j, i: (0, j))],
        out_specs=[pl.BlockSpec((tb, GDN_W), cur), pl.BlockSpec((SUBLANE, GDN_W), lambda j, i: (0, j))],
        out_shape=[jax.ShapeDtypeStruct((T, 3 * GDN_W), BF16),
                   jax.ShapeDtypeStruct((SUBLANE, 3 * GDN_W), F32)],
        compiler_params=_params(2))(proj, proj, proj, dout, dout, w8)


def _dot_family(dot, diff):
    if not diff:
        return tuple(functools.partial(lambda d, a, b: dot(a, b, d), d) for d in (BNN, BNT, BTN))

    @jax.custom_vjp
    def nn(a, b):
        return dot(a, b, BNN)
    nn.defvjp(lambda a, b: (dot(a, b, BNN), (a, b)),
              lambda res, g: (dot(g, res[1], BNT), dot(res[0], g, BTN)))

    @jax.custom_vjp
    def nt(a, b):
        return dot(a, b, BNT)
    nt.defvjp(lambda a, b: (dot(a, b, BNT), (a, b)),
              lambda res, g: (dot(g, res[1], BNN), dot(g, res[0], BTN)))

    @jax.custom_vjp
    def tn(a, b):
        return dot(a, b, BTN)
    tn.defvjp(lambda a, b: (dot(a, b, BTN), (a, b)),
              lambda res, g: (dot(res[1], g, BNT), dot(res[0], g, BNN)))
    return nn, nt, tn


def _saved_inverse(hdots):
    _, hnt, htn = hdots

    @jax.custom_vjp
    def inv(L, P):
        return P

    inv.defvjp(lambda L, P: (P, P), lambda P, g: (-hnt(htn(P, g), P), jnp.zeros_like(P)))
    return inv


def _gdn_step(dots, hdots, S, q, k, v, z, b_raw, a_raw, alog, dtb, gnorm, P_saved=None, return_P=False):
    nn, nt, tn = dots
    hnn = hdots[0]
    B, C = q.shape[0], GDN_CHUNK
    ii = lax.broadcasted_iota(jnp.int32, (B, C, C), 1)
    jj = lax.broadcasted_iota(jnp.int32, (B, C, C), 2)
    causal, strict = ii >= jj, ii > jj
    tri, tri_t = causal.astype(F32), (ii <= jj).astype(F32)
    eye, ones = (ii == jj).astype(F32), jnp.ones((B, C, C), F32)

    beta = _sigmoid(b_raw)
    xs = a_raw + dtb
    pos = xs > 0.0
    softplus = jnp.where(pos, xs, 0.0) + jnp.log(1.0 + jnp.exp(jnp.where(pos, -xs, xs)))
    g = -jnp.exp(alog) * softplus
    gb = jnp.broadcast_to(g, (B, C, C))
    gc_c = hnn(tri, gb)
    gc_r = hnn(hnn(ones, eye * gb), tri_t)
    gc = hnn(tri, jnp.broadcast_to(g, (B, C, HD)))
    gl = jnp.sum(g, axis=1, keepdims=True)
    decay = jnp.where(causal, jnp.exp(jnp.where(causal, gc_c - gc_r, 0.0)), 0.0)
    q = q * (HD ** -0.5)
    kb = k * beta
    L = jnp.where(strict, nt(kb, k) * decay, 0.0)
    egc = jnp.exp(gc)
    if P_saved is None:
        P = eye - L
        M = hnn(L, L)
        for step in range(5):
            P = P + hnn(P, M)
            if step < 4:
                M = hnn(M, M)
    else:
        P = _saved_inverse(hdots)(L, P_saved)
    u = hnn(P, v * beta)
    w = hnn(P, kb * egc)
    intra = jnp.where(causal, nt(q, k) * decay, 0.0)
    qg = q * egc
    kdec = k * jnp.exp(gl - gc)
    egl = jnp.exp(gl)
    outs = []
    for ci in range(B // GDN_H):
        sl = slice(ci * GDN_H, (ci + 1) * GDN_H)
        v_new = u[sl] - nn(w[sl], S)
        outs.append(nn(qg[sl], S) + nn(intra[sl], v_new))
        S = S * egl[sl] + tn(kdec[sl], v_new)
    o = jnp.concatenate(outs, axis=0)
    r = lax.rsqrt(jnp.mean(o * o, axis=-1, keepdims=True) + RMS_EPS)
    out = o * r * gnorm * _silu(z)
    return (out, S, P) if return_P else (out, S)


def _gdn_batches(qkv, ba, z, alog_row, dt_row):
    C = GDN_CHUNK
    q, k, v, zz, b_raw, a_raw, alog, dtb = ([] for _ in range(8))
    for ci in range(GDN_STEP):
        rows = slice(ci * C, (ci + 1) * C)
        for h in range(GDN_H):
            q.append(qkv[rows, h * HD:(h + 1) * HD])
            k.append(qkv[rows, GDN_W + h * HD:GDN_W + (h + 1) * HD])
            v.append(qkv[rows, 2 * GDN_W + h * HD:2 * GDN_W + (h + 1) * HD])
            zz.append(z[rows, h * HD:(h + 1) * HD])
            b_raw.append(ba[rows, h:h + 1])
            a_raw.append(ba[rows, GDN_H + h:GDN_H + h + 1])
            alog.append(alog_row[:, h:h + 1])
            dtb.append(dt_row[:, h:h + 1])
    return tuple(jnp.stack(t) for t in (q, k, v, zz, b_raw, a_raw, alog, dtb))


def gdn_chunk_fwd(qkv, proj, alog_row, dt_row, gnorm, name):
    T = qkv.shape[0]
    R = GDN_CHUNK * GDN_STEP
    N = T // R
    B = GDN_STEP * GDN_H
    dots, hdots = _dot_family(_bdot, False), _dot_family(_dot3, False)

    def body(qkv_ref, ba_ref, z_ref, al_ref, dt_ref, gn_ref, o_ref, save_ref, inv_ref, S_ref):
        n = pl.program_id(0)

        @pl.when(n == 0)
        def _():
            S_ref[...] = jnp.zeros_like(S_ref)

        S = S_ref[...]
        save_ref[0] = S
        batches = _gdn_batches(qkv_ref[...], ba_ref[...], z_ref[...], al_ref[...], dt_ref[...])
        o, S_new, P = _gdn_step(dots, hdots, S, *batches, gn_ref[...], return_P=True)
        S_ref[...] = S_new
        inv_ref[0] = P
        for ci in range(GDN_STEP):
            for h in range(GDN_H):
                o_ref[ci * GDN_CHUNK:(ci + 1) * GDN_CHUNK, h * HD:(h + 1) * HD] = o[ci * GDN_H + h].astype(BF16)

    return _pcall(
        body, name=name, grid=(N,),
        in_specs=[pl.BlockSpec((R, 3 * GDN_W), lambda n: (n, 0)),
                  pl.BlockSpec((R, LANE), lambda n: (n, (4 * GDN_W + POOL_G * HD) // LANE)),
                  pl.BlockSpec((R, GDN_W), lambda n: (n, 3)),
                  _row_spec(LANE), _row_spec(LANE), _row_spec(HD)],
        out_specs=[pl.BlockSpec((R, GDN_W), lambda n: (n, 0)),
                   pl.BlockSpec((1, GDN_H, HD, HD), lambda n: (n, 0, 0, 0)),
                   pl.BlockSpec((1, B, GDN_CHUNK, GDN_CHUNK), lambda n: (n, 0, 0, 0))],
        out_shape=[jax.ShapeDtypeStruct((T, GDN_W), BF16), jax.ShapeDtypeStruct((N, GDN_H, HD, HD), F32),
                   jax.ShapeDtypeStruct((N, B, GDN_CHUNK, GDN_CHUNK), F32)],
        scratch_shapes=[pltpu.VMEM((GDN_H, HD, HD), F32)],
        long_call=True)(qkv, proj, proj, alog_row, dt_row, gnorm)


def gdn_chunk_bwd(qkv, proj, alog_row, dt_row, gnorm, saved, inverses, docat, name):
    T = qkv.shape[0]
    C = GDN_CHUNK
    R = C * GDN_STEP
    N = T // R
    B = GDN_STEP * GDN_H
    dots, hdots = _dot_family(_bdot, True), _dot_family(_dot3, True)

    def body(qkv_ref, ba_ref, z_ref, al_ref, dt_ref, gn_ref, save_ref, inv_ref, do_ref,
             dqkv_ref, dz_ref, dba_ref, dal_ref, ddt_ref, dgn_ref, dS_ref):
        n = pl.program_id(0)

        @pl.when(n == 0)
        def _():
            dS_ref[...] = jnp.zeros_like(dS_ref)
            dal_ref[...] = jnp.zeros_like(dal_ref)
            ddt_ref[...] = jnp.zeros_like(ddt_ref)
            dgn_ref[...] = jnp.zeros_like(dgn_ref)

        batches = _gdn_batches(qkv_ref[...], ba_ref[...], z_ref[...], al_ref[...], dt_ref[...])
        do = do_ref[...]
        do_b = jnp.stack([do[ci * C:(ci + 1) * C, h * HD:(h + 1) * HD]
                          for ci in range(GDN_STEP) for h in range(GDN_H)])
        P = inv_ref[0]
        fn = lambda *args: _gdn_step(dots, hdots, *args, P_saved=P)
        _, vjp = jax.vjp(fn, save_ref[0], *batches, gn_ref[...])
        dS, dq, dk, dv, dz, db_raw, da_raw, dalog, ddtb, dgn = vjp((do_b, dS_ref[...]))
        dS_ref[...] = dS
        lane = lax.broadcasted_iota(jnp.int32, (1, LANE), 1)
        dal = jnp.zeros((1, LANE), F32)
        ddt = jnp.zeros((1, LANE), F32)
        for ci in range(GDN_STEP):
            rows = slice(ci * C, (ci + 1) * C)
            dba = jnp.zeros((C, LANE), F32)
            for h in range(GDN_H):
                b = ci * GDN_H + h
                dqkv_ref[rows, h * HD:(h + 1) * HD] = dq[b]
                dqkv_ref[rows, GDN_W + h * HD:GDN_W + (h + 1) * HD] = dk[b]
                dqkv_ref[rows, 2 * GDN_W + h * HD:2 * GDN_W + (h + 1) * HD] = dv[b]
                dz_ref[rows, h * HD:(h + 1) * HD] = dz[b].astype(BF16)
                hot_b = (lane == h).astype(F32)
                dba = dba + db_raw[b] * hot_b + da_raw[b] * (lane == GDN_H + h).astype(F32)
                dal = dal + dalog[b] * hot_b
                ddt = ddt + ddtb[b] * hot_b
            dba_ref[rows, :] = dba.astype(BF16)
        dal_ref[...] += dal
        ddt_ref[...] += ddt
        dgn_ref[...] += dgn

    rev = lambda n: N - 1 - n
    row = jax.ShapeDtypeStruct((1, LANE), F32)
    return _pcall(
        body, name=name, grid=(N,),
        in_specs=[pl.BlockSpec((R, 3 * GDN_W), lambda n: (rev(n), 0)),
                  pl.BlockSpec((R, LANE), lambda n: (rev(n), (4 * GDN_W + POOL_G * HD) // LANE)),
                  pl.BlockSpec((R, GDN_W), lambda n: (rev(n), 3)),
                  _row_spec(LANE), _row_spec(LANE), _row_spec(HD),
                  pl.BlockSpec((1, GDN_H, HD, HD), lambda n: (rev(n), 0, 0, 0)),
                  pl.BlockSpec((1, B, C, C), lambda n: (rev(n), 0, 0, 0)),
                  pl.BlockSpec((R, GDN_W), lambda n: (rev(n), 0))],
        out_specs=[pl.BlockSpec((R, 3 * GDN_W), lambda n: (rev(n), 0)),
                   pl.BlockSpec((R, GDN_W), lambda n: (rev(n), 0)),
                   pl.BlockSpec((R, LANE), lambda n: (rev(n), 0)),
                   _row_spec(LANE), _row_spec(LANE), _row_spec(HD)],
        out_shape=[jax.ShapeDtypeStruct((T, 3 * GDN_W), F32), jax.ShapeDtypeStruct((T, GDN_W), BF16),
                   jax.ShapeDtypeStruct((T, LANE), BF16), row, row, jax.ShapeDtypeStruct((1, HD), F32)],
        scratch_shapes=[pltpu.VMEM((GDN_H, HD, HD), F32)],
        long_call=True)(qkv, proj, proj, alog_row, dt_row, gnorm, saved, inverses, docat)


POOL_HALO = 16


def _pool_pick(j, s2, s4, s8, s16):
    return jnp.where(j == 0, s2, jnp.where(j == 1, s4, jnp.where(j == 2, s8, s16)))


def _pool_count(j, t0, rows):
    t1 = (t0 + 1 + lax.broadcasted_iota(jnp.int32, (rows, 1), 0)).astype(F32)
    win = jnp.where(j == 0, 2.0, jnp.where(j == 1, 4.0, jnp.where(j == 2, 8.0, 16.0)))
    return jnp.minimum(t1, win)


def _pooled(p, prev, i, j, tb):
    ext = jnp.concatenate([prev, p], axis=0)
    s2 = ext + _back(ext, 1)
    s4 = s2 + _back(s2, 2)
    s8 = s4 + _back(s4, 4)
    s16 = s8 + _back(s8, 8)
    s = _pool_pick(j, s2, s4, s8, s16)[POOL_HALO:]
    return s / _pool_count(j, i * tb, tb) - p


def pool_fwd(proj, pool_w, pool_scale, name):
    T = proj.shape[0]
    tb = _rows(T, 512)
    c0 = 4 * GDN_H

    def body(p_ref, pp_ref, w_ref, s_ref, o_ref):
        i, j = pl.program_id(0), pl.program_id(1)
        p = p_ref[...]
        prev = jnp.where(i > 0, pp_ref[...], 0.0)
        pooled = _pooled(p, prev, i, j, tb)
        o_ref[...] = (_bdot(pooled, w_ref[0], NN) * s_ref[...]).astype(BF16)

    return _pcall(
        body, name=name, grid=(T // tb, POOL_G),
        in_specs=[pl.BlockSpec((tb, HD), lambda i, j: (i, c0 + j)),
                  pl.BlockSpec((POOL_HALO, HD), lambda i, j: (jnp.maximum(i * (tb // POOL_HALO) - 1, 0), c0 + j)),
                  pl.BlockSpec((1, HD, HD), lambda i, j: (j, 0, 0)),
                  pl.BlockSpec((1, HD), lambda i, j: (0, j))],
        out_specs=pl.BlockSpec((tb, HD), lambda i, j: (i, j)),
        out_shape=jax.ShapeDtypeStruct((T, POOL_G * HD), BF16),
        compiler_params=_params(2))(proj, proj, pool_w, pool_scale)


def pool_bwd(proj, pool_w, pool_scale, docat, name):
    T = proj.shape[0]
    tb = _rows(T, 512)
    nrb = T // tb
    c0 = 4 * GDN_H
    HB = POOL_HALO

    def body(p_ref, pp_ref, w_ref, s_ref, d_ref, dn_ref, dp_ref, dw_ref, ds_ref):
        j, i = pl.program_id(0), pl.program_id(1)
        p = p_ref[...]
        prev = jnp.where(i > 0, pp_ref[...], 0.0)
        pooled = _pooled(p, prev, i, j, tb)
        w, scale = w_ref[0], s_ref[...]
        dy = d_ref[...]
        dnext = jnp.where(i < nrb - 1, dn_ref[...], 0.0)
        dyp = jnp.concatenate([dy, dnext], axis=0) * scale
        dpooled = _bdot(dyp, w, NT)
        qn = dpooled / _pool_count(j, i * tb, tb + HB)
        a2 = qn + _ahead(qn, 1)
        a4 = a2 + _ahead(a2, 2)
        a8 = a4 + _ahead(a4, 4)
        a16 = a8 + _ahead(a8, 8)
        dp_ref[...] = (_pool_pick(j, a2, a4, a8, a16) - dpooled)[:tb].astype(BF16)

        @pl.when(i == 0)
        def _():
            dw_ref[...] = jnp.zeros_like(dw_ref)
            ds_ref[...] = jnp.zeros_like(ds_ref)

        dw_ref[0] += _bdot(pooled, dyp[:tb], TN)
        ds_ref[...] += jnp.sum(dy * _bdot(pooled, w, NN), axis=0, keepdims=True)

    return _pcall(
        body, name=name, grid=(POOL_G, nrb),
        in_specs=[pl.BlockSpec((tb, HD), lambda j, i: (i, c0 + j)),
                  pl.BlockSpec((HB, HD), lambda j, i: (jnp.maximum(i * (tb // HB) - 1, 0), c0 + j)),
                  pl.BlockSpec((1, HD, HD), lambda j, i: (j, 0, 0)),
                  pl.BlockSpec((1, HD), lambda j, i: (0, j)),
                  pl.BlockSpec((tb, HD), lambda j, i: (i, POOL_G + j)),
                  pl.BlockSpec((HB, HD), lambda j, i: (jnp.minimum(i + 1, nrb - 1) * (tb // HB), POOL_G + j))],
        out_specs=[pl.BlockSpec((tb, HD), lambda j, i: (i, j)),
                   pl.BlockSpec((1, HD, HD), lambda j, i: (j, 0, 0)),
                   pl.BlockSpec((1, HD), lambda j, i: (0, j))],
        out_shape=[jax.ShapeDtypeStruct((T, POOL_G * HD), BF16),
                   jax.ShapeDtypeStruct((POOL_G, HD, HD), F32),
                   jax.ShapeDtypeStruct((1, POOL_G * HD), F32)],
        compiler_params=_params(2))(proj, proj, pool_w, pool_scale, docat, docat)


ATT_W = ATT_H * HD
GROUP_COLS = 3 * ATT_W


def to_residue_major(t, d):
    if d == 1:
        return t
    T, C = t.shape
    return t.reshape(T // d, d, C).transpose(1, 0, 2).reshape(T, C)


def to_token_order(t, d):
    if d == 1:
        return t
    T, C = t.shape
    return t.reshape(d, T // d, C).transpose(1, 0, 2).reshape(T, C)


def headnorm_fwd(proj, qk_gain, name):
    T = proj.shape[0]
    tb = _rows(T, 256)

    def body(x_ref, g_ref, o_ref):
        g = g_ref[...]
        for h in range(2 * ATT_H):
            sl = slice(h * HD, (h + 1) * HD)
            x = x_ref[:, sl].astype(F32)
            n = x * lax.rsqrt(jnp.mean(x * x, axis=-1, keepdims=True) + RMS_EPS)
            gain = g[0:1] * (HD ** -0.5) if h < ATT_H else g[1:2]
            o_ref[:, sl] = (n * gain).astype(BF16)
        o_ref[:, 2 * ATT_W:] = x_ref[:, 2 * ATT_W:]

    blk = pl.BlockSpec((tb, GROUP_COLS), lambda i: (i, 0))
    return _pcall(
        body, name=name, grid=(T // tb,),
        in_specs=[blk, pl.BlockSpec((SUBLANE, HD), lambda i: (0, 0))],
        out_specs=blk, out_shape=jax.ShapeDtypeStruct((T, GROUP_COLS), BF16),
        long_call=True)(proj, qk_gain)


def headnorm_bwd(proj, qk_gain, dq, dk, dv, name):
    T = proj.shape[0]
    tb = _rows(T, 256)

    def body(x_ref, g_ref, dq_ref, dk_ref, dv_ref, dx_ref, dg_ref):
        i = pl.program_id(0)
        g = g_ref[...]

        @pl.when(i == 0)
        def _():
            dg_ref[...] = jnp.zeros_like(dg_ref)

        for part, d_ref in enumerate((dq_ref, dk_ref)):
            gain = g[0:1] * (HD ** -0.5) if part == 0 else g[1:2]
            scale = (HD ** -0.5) if part == 0 else 1.0
            acc = jnp.zeros((1, HD), F32)
            for h in range(ATT_H):
                x = x_ref[:, part * ATT_W + h * HD:part * ATT_W + (h + 1) * HD].astype(F32)
                d = d_ref[:, h * HD:(h + 1) * HD].astype(F32)
                r = lax.rsqrt(jnp.mean(x * x, axis=-1, keepdims=True) + RMS_EPS)
                n = x * r
                dn = d * gain
                dx = r * (dn - n * jnp.mean(dn * n, axis=-1, keepdims=True))
                dx_ref[:, part * ATT_W + h * HD:part * ATT_W + (h + 1) * HD] = dx.astype(BF16)
                acc = acc + jnp.sum(d * n, axis=0, keepdims=True)
            dg_ref[part:part + 1, :] += acc * scale
        dx_ref[:, 2 * ATT_W:] = dv_ref[...]

    blk = pl.BlockSpec((tb, GROUP_COLS), lambda i: (i, 0))
    dblk = pl.BlockSpec((tb, ATT_W), lambda i: (i, 0))
    gspec = pl.BlockSpec((SUBLANE, HD), lambda i: (0, 0))
    return _pcall(
        body, name=name, grid=(T // tb,),
        in_specs=[blk, gspec, dblk, dblk, dblk],
        out_specs=[blk, gspec],
        out_shape=[jax.ShapeDtypeStruct((T, GROUP_COLS), BF16), jax.ShapeDtypeStruct((SUBLANE, HD), F32)],
        long_call=True)(proj, qk_gain, dq, dk, dv)


def _heads(ref):
    return jnp.stack([ref[:, h * HD:(h + 1) * HD] for h in range(ATT_H)])


def _slopes(dil):
    h = lax.broadcasted_iota(jnp.int32, (ATT_H, 1, 1), 0)
    return lax.bitcast_convert_type((126 - h) << 23, F32) * float(dil)


def _att_scores_b(q, k, slope, n_ok, far, keys_first=False):
    r = lax.broadcasted_iota(jnp.int32, (1, ATT_BLK, ATT_BLK), 1)
    c = lax.broadcasted_iota(jnp.int32, (1, ATT_BLK, ATT_BLK), 2)
    a, j = (c, r) if keys_first else (r, c)
    rel = (ATT_BLK + a - j) if far else (a - j)
    mask = ((j >= a) & n_ok) if far else (j <= a)
    s = (_bdot(k, q, BNT) if keys_first else _bdot(q, k, BNT)) - slope * rel.astype(F32)
    return jnp.where(mask, s, NEG), mask


def _att_blocks(nb, width, shift):
    def make(col):
        return pl.BlockSpec((ATT_BLK, width),
                            lambda r, n: (r * nb + jnp.clip(n + shift, 0, nb - 1), col))
    return make


def _lane_col(cols):
    lane = lax.broadcasted_iota(jnp.int32, (1, LANE), 1)
    out = jnp.zeros((ATT_BLK, LANE), F32)
    for h, c in enumerate(cols):
        out = out + c * (lane == h).astype(F32)
    return out


def att_fwd(qkvn, gi, name):
    T = qkvn.shape[0]
    dil = DIL[gi]
    nb = T // dil // ATT_BLK

    def body(q_ref, kp_ref, kc_ref, vp_ref, vc_ref, o_ref, l_ref):
        n_ok = pl.program_id(1) > 0
        slope = _slopes(dil)
        q = _heads(q_ref)
        s_c, _ = _att_scores_b(q, _heads(kc_ref), slope, n_ok, False)
        s_p, _ = _att_scores_b(q, _heads(kp_ref), slope, n_ok, True)
        m = jnp.maximum(jnp.max(s_c, axis=-1, keepdims=True), jnp.max(s_p, axis=-1, keepdims=True))
        p_c, p_p = jnp.exp(s_c - m), jnp.exp(s_p - m)
        l = jnp.sum(p_c, axis=-1, keepdims=True) + jnp.sum(p_p, axis=-1, keepdims=True)
        o = (_bdot(p_c, _heads(vc_ref), BNN) + _bdot(p_p, _heads(vp_ref), BNN)) / l
        lse = m + jnp.log(l)
        for h in range(ATT_H):
            o_ref[:, h * HD:(h + 1) * HD] = o[h].astype(BF16)
        l_ref[...] = _lane_col([lse[h] for h in range(ATT_H)])

    cur, prv = _att_blocks(nb, ATT_W, 0), _att_blocks(nb, ATT_W, -1)
    return _pcall(
        body, name=name, grid=(dil, nb), in_specs=[cur(0), prv(1), cur(1), prv(2), cur(2)],
        out_specs=[cur(0), _att_blocks(nb, LANE, 0)(0)],
        out_shape=[jax.ShapeDtypeStruct((T, ATT_W), BF16), jax.ShapeDtypeStruct((T, LANE), F32)],
        long_call=True)(qkvn, qkvn, qkvn, qkvn, qkvn)


def att_merge(os, lses, name):
    T = os[0].shape[0]
    tb = _rows(T, 512)

    def body(o0, o1, o2, l0, l1, l2, o_ref, l_ref):
        a, b, c = l0[...], l1[...], l2[...]
        m = jnp.maximum(a, jnp.maximum(b, c))
        wa, wb, wc = jnp.exp(a - m), jnp.exp(b - m), jnp.exp(c - m)
        den = wa + wb + wc
        l_ref[...] = m + jnp.log(den)
        wa, wb, wc = wa / den, wb / den, wc / den
        for h in range(ATT_H):
            sl = slice(h * HD, (h + 1) * HD)
            o_ref[:, sl] = (wa[:, h:h + 1] * o0[:, sl].astype(F32) + wb[:, h:h + 1] * o1[:, sl].astype(F32)
                            + wc[:, h:h + 1] * o2[:, sl].astype(F32))

    blk = pl.BlockSpec((tb, ATT_W), lambda i: (i, 0))
    lblk = pl.BlockSpec((tb, LANE), lambda i: (i, 0))
    return _pcall(
        body, name=name, grid=(T // tb,), in_specs=[blk] * 3 + [lblk] * 3, out_specs=[blk, lblk],
        out_shape=[jax.ShapeDtypeStruct((T, ATT_W), F32), jax.ShapeDtypeStruct((T, LANE), F32)],
        compiler_params=_params(1))(*os, *lses)


def att_delta(do, o, name):
    T = do.shape[0]
    tb = _rows(T, 512)

    def body(d_ref, o_ref, out_ref):
        lane = lax.broadcasted_iota(jnp.int32, (1, LANE), 1)
        out = jnp.zeros((tb, LANE), F32)
        for h in range(ATT_H):
            sl = slice(h * HD, (h + 1) * HD)
            s = jnp.sum(d_ref[:, sl].astype(F32) * o_ref[:, sl], axis=-1, keepdims=True)
            out = out + s * (lane == h).astype(F32)
        out_ref[...] = out

    blk = pl.BlockSpec((tb, ATT_W), lambda i: (i, 0))
    return _pcall(
        body, name=name, grid=(T // tb,), in_specs=[blk, blk],
        out_specs=pl.BlockSpec((tb, LANE), lambda i: (i, 0)),
        out_shape=jax.ShapeDtypeStruct((T, LANE), F32), compiler_params=_params(1))(do, o)


def att_bwd_q(qkvn, do, lse, delta, gi, name):
    T = qkvn.shape[0]
    dil = DIL[gi]
    nb = T // dil // ATT_BLK

    def body(q_ref, kp_ref, kc_ref, vp_ref, vc_ref, do_ref, l_ref, d_ref, dq_ref):
        n_ok = pl.program_id(1) > 0
        slope = _slopes(dil)
        lse, dl = l_ref[...], d_ref[...]
        lse = jnp.stack([lse[:, h:h + 1] for h in range(ATT_H)])
        dl = jnp.stack([dl[:, h:h + 1] for h in range(ATT_H)])
        q, do = _heads(q_ref), _heads(do_ref)
        dq = jnp.zeros((ATT_H, ATT_BLK, HD), F32)
        for k_ref, v_ref, far in ((kc_ref, vc_ref, False), (kp_ref, vp_ref, True)):
            k = _heads(k_ref)
            s, mask = _att_scores_b(q, k, slope, n_ok, far)
            p = jnp.where(mask, jnp.exp(s - lse), 0.0)
            ds = p * (_bdot(do, _heads(v_ref), BNT) - dl)
            dq = dq + _bdot(ds, k, BNN)
        for h in range(ATT_H):
            dq_ref[:, h * HD:(h + 1) * HD] = dq[h].astype(BF16)

    cur, prv = _att_blocks(nb, ATT_W, 0), _att_blocks(nb, ATT_W, -1)
    small = _att_blocks(nb, LANE, 0)(0)
    return _pcall(
        body, name=name, grid=(dil, nb),
        in_specs=[cur(0), prv(1), cur(1), prv(2), cur(2), cur(0), small, small],
        out_specs=cur(0), out_shape=jax.ShapeDtypeStruct((T, ATT_W), BF16),
        long_call=True)(qkvn, qkvn, qkvn, qkvn, qkvn, do, lse, delta)


def att_bwd_kv(qkvn, do, lse, delta, gi, name):
    T = qkvn.shape[0]
    dil = DIL[gi]
    nb = T // dil // ATT_BLK

    def body(k_ref, v_ref, q0_ref, q1_ref, do0_ref, do1_ref, l0_ref, l1_ref, d0_ref, d1_ref,
             dk_ref, dv_ref):
        n_ok = pl.program_id(1) < nb - 1
        slope = _slopes(dil)
        by_row = lambda ref: jnp.stack([ref[...].T[h:h + 1, :] for h in range(ATT_H)])
        k, v = _heads(k_ref), _heads(v_ref)
        dk = jnp.zeros((ATT_H, ATT_BLK, HD), F32)
        dv = jnp.zeros((ATT_H, ATT_BLK, HD), F32)
        for q_ref, do_ref, l_ref, d_ref, far in ((q0_ref, do0_ref, l0_ref, d0_ref, False),
                                                 (q1_ref, do1_ref, l1_ref, d1_ref, True)):
            q, do = _heads(q_ref), _heads(do_ref)
            s, mask = _att_scores_b(q, k, slope, n_ok, far, keys_first=True)
            p = jnp.where(mask, jnp.exp(s - by_row(l_ref)), 0.0)
            dv = dv + _bdot(p, do, BNN)
            ds = p * (_bdot(v, do, BNT) - by_row(d_ref))
            dk = dk + _bdot(ds, q, BNN)
        for h in range(ATT_H):
            dk_ref[:, h * HD:(h + 1) * HD] = dk[h].astype(BF16)
            dv_ref[:, h * HD:(h + 1) * HD] = dv[h].astype(BF16)

    cur, nxt = _att_blocks(nb, ATT_W, 0), _att_blocks(nb, ATT_W, 1)
    s0, s1 = _att_blocks(nb, LANE, 0)(0), _att_blocks(nb, LANE, 1)(0)
    return _pcall(
        body, name=name, grid=(dil, nb),
        in_specs=[cur(1), cur(2), cur(0), nxt(0), cur(0), nxt(0), s0, s1, s0, s1],
        out_specs=[cur(0), cur(0)], out_shape=[jax.ShapeDtypeStruct((T, ATT_W), BF16)] * 2,
        long_call=True)(qkvn, qkvn, qkvn, qkvn, do, do, lse, lse, delta, delta)


def adamw(w, g, m, v, name):
    shape = w.shape
    C = shape[-1]
    R = math.prod(shape[:-1])
    to2d = lambda t: t.reshape(R, C)
    tb = _rows(R, max(16, (256 * 1536 // C) // 16 * 16))
    c1 = 1.0 - ADAM_B1 ** ADAM_STEP
    c2 = 1.0 - ADAM_B2 ** ADAM_STEP

    def body(w_ref, g_ref, m_ref, v_ref, d_ref, nm_ref, nv_ref):
        gv = g_ref[...]
        nm = ADAM_B1 * m_ref[...] + (1.0 - ADAM_B1) * gv
        nv = ADAM_B2 * v_ref[...] + (1.0 - ADAM_B2) * (gv * gv)
        d_ref[...] = -ADAM_LR * ((nm / c1) / (jnp.sqrt(nv / c2) + ADAM_EPS) + ADAM_WD * w_ref[...])
        nm_ref[...] = nm
        nv_ref[...] = nv

    blk = pl.BlockSpec((tb, C), lambda i: (i, 0))
    out = jax.ShapeDtypeStruct((R, C), F32)
    d, nm, nv = _pcall(
        body, name=name, grid=(R // tb,), in_specs=[blk] * 4, out_specs=[blk] * 3,
        out_shape=[out, out, out], long_call=R * C >= (1 << 21))(to2d(w), to2d(g), to2d(m), to2d(v))
    return d.reshape(shape), nm.reshape(shape), nv.reshape(shape)


def _pad_rows8(w):
    return jnp.pad(w, ((0, SUBLANE - w.shape[0]), (0, 0)))


def _lane_row(v):
    return jnp.pad(v, (0, LANE - v.shape[0]))[None, :]


def _even_reorder(w_in):
    z4 = 4 * GDN_W
    pad = jnp.zeros((w_in.shape[0], EVEN_PAD - EVEN_COLS), w_in.dtype)
    return jnp.concatenate([w_in[:, :z4], w_in[:, z4 + 2 * GDN_H:], w_in[:, z4:z4 + 2 * GDN_H], pad], axis=1)


def _even_restore(dw):
    z4 = 4 * GDN_W
    p4 = POOL_G * HD
    return jnp.concatenate([dw[:, :z4], dw[:, z4 + p4:z4 + p4 + 2 * GDN_H], dw[:, z4:z4 + p4]], axis=1)


def _ffn_fwd(tag, hf, wl):
    up = matmul(hf, wl["ffn_w_up"], "nn", BF16, f"{tag}_ffn_up")
    act = ffn_mid_fwd(up, wl["ffn_conv_w8"], wl["ffn_conv_b"], f"{tag}_ffn_mid")
    f = matmul(act, wl["ffn_w_down"], "nn", F32, f"{tag}_ffn_down")
    return f, (hf, up, act)


def _ffn_bwd(tag, df, saved, wl):
    hf, up, act = saved
    dact = matmul(df, wl["ffn_w_down"], "nt", BF16, f"{tag}_ffn_down_da")
    wl["on_grad"]("ffn_w_down", matmul(act, df, "tn", F32, f"{tag}_ffn_down_dw", pieces="row"))
    dup, dcw, dcb = ffn_mid_bwd(up, wl["ffn_conv_w8"], wl["ffn_conv_b"], dact, f"{tag}_ffn_mid_bwd")
    dhf = matmul(dup, wl["ffn_w_up"], "nt", F32, f"{tag}_ffn_up_da")
    wl["on_grad"]("ffn_w_up", matmul(hf, dup, "tn", F32, f"{tag}_ffn_up_dw", pieces="col"))
    return dhf, {"ffn_conv_w": dcw[:FFN_CONV], "ffn_conv_b": dcb[0]}


def _even_fwd(tag, hm, wl):
    proj = matmul(hm, wl["w_in"], "nn", F32, f"{tag}_ev_in")
    qkv = gdn_conv_fwd(proj, wl["gdn_conv_w8"], f"{tag}_gdn_conv")
    o_a, *states = gdn_chunk_fwd(qkv, proj, wl["alog_row"], wl["dt_row"], wl["gdn_norm"], f"{tag}_gdn_chunk")
    o_b = pool_fwd(proj, wl["pool_w"], wl["pool_scale"], f"{tag}_pool")
    ocat = jnp.concatenate([o_a, o_b], axis=1)
    y = matmul(ocat, wl["w_out"], "nn", F32, f"{tag}_ev_out")
    return y, (hm, proj, qkv, states, ocat)


def _even_bwd(tag, dy, saved, wl):
    hm, proj, qkv, states, ocat = saved
    docat = matmul(dy, wl["w_out"], "nt", F32, f"{tag}_ev_out_da")
    wl["on_grad"]("ev_w_out", matmul(ocat, dy, "tn", F32, f"{tag}_ev_out_dw", pieces="row"))
    dqkv, dz, dba, dalog, ddt, dgn = gdn_chunk_bwd(
        qkv, proj, wl["alog_row"], wl["dt_row"], wl["gdn_norm"], *states, docat, f"{tag}_gdn_chunk_bwd")
    dxc, dconv = gdn_conv_bwd(proj, wl["gdn_conv_w8"], dqkv, f"{tag}_gdn_conv_bwd")
    dp, dpw, dps = pool_bwd(proj, wl["pool_w"], wl["pool_scale"], docat, f"{tag}_pool_bwd")
    dproj = jnp.concatenate([dxc, dz, dp, dba], axis=1)
    dhm = matmul(dproj, wl["w_in"], "nt", F32, f"{tag}_ev_in_da")
    wl["on_grad"]("ev_w_in", col_pieces(_even_restore(matmul(hm, dproj, "tn", F32, f"{tag}_ev_in_dw"))))
    return dhm, {"gdn_conv_w": dconv[:GDN_CONV], "gdn_a_log": dalog[0, :GDN_H], "gdn_dt_bias": ddt[0, :GDN_H],
                 "gdn_norm": dgn[0], "pool_w": dpw, "pool_scale": dps[0]}


def _odd_fwd(tag, hm, wl):
    projs, qkvns, outs, lses = [], [], [], []
    for gi, d in enumerate(DIL):
        w_g = wl["w_in"][:, gi * GROUP_COLS:(gi + 1) * GROUP_COLS]
        proj = matmul(to_residue_major(hm, d), w_g, "nn", BF16, f"{tag}_od_in{gi}")
        qkvn = headnorm_fwd(proj, wl["qk_gain8"], f"{tag}_headnorm{gi}")
        o_g, l_g = att_fwd(qkvn, gi, f"{tag}_att{gi}")
        projs.append(proj)
        qkvns.append(qkvn)
        outs.append(to_token_order(o_g, d))
        lses.append(to_token_order(l_g, d))
    o, lse = att_merge(outs, lses, f"{tag}_att_merge")
    y = matmul(o, wl["w_out"], "nn", F32, f"{tag}_od_out")
    return y, (hm, projs, qkvns, o, lse)


def _odd_bwd(tag, dy, saved, wl):
    hm, projs, qkvns, o, lse = saved
    do = matmul(dy, wl["w_out"], "nt", BF16, f"{tag}_od_out_da")
    wl["on_grad"]("od_w_out", matmul(o, dy, "tn", F32, f"{tag}_od_out_dw", pieces="row"))
    delta = att_delta(do, o, f"{tag}_att_delta")
    dhm, dw_in, dgain_qk = None, [], None
    for gi, d in enumerate(DIL):
        w_g = wl["w_in"][:, gi * GROUP_COLS:(gi + 1) * GROUP_COLS]
        do_g, lse_g, dl_g = (to_residue_major(t, d) for t in (do, lse, delta))
        dq = att_bwd_q(qkvns[gi], do_g, lse_g, dl_g, gi, f"{tag}_att{gi}_dq")
        dk, dv = att_bwd_kv(qkvns[gi], do_g, lse_g, dl_g, gi, f"{tag}_att{gi}_dkv")
        dproj, dgain = headnorm_bwd(projs[gi], wl["qk_gain8"], dq, dk, dv, f"{tag}_headnorm{gi}_bwd")
        dhm_g = to_token_order(matmul(dproj, w_g, "nt", F32, f"{tag}_od_in{gi}_da"), d)
        dw_in.append(matmul(to_residue_major(hm, d), dproj, "tn", F32, f"{tag}_od_in{gi}_dw"))
        dhm = dhm_g if dhm is None else dhm + dhm_g
        dgain_qk = dgain if dgain_qk is None else dgain_qk + dgain
    wl["on_grad"]("od_w_in", col_pieces(jnp.concatenate(dw_in, axis=1)))
    return dhm, {"att_q_norm": dgain_qk[0], "att_k_norm": dgain_qk[1]}


def col_pieces(dw):
    M, N = dw.shape
    return dw.reshape(2, M // 2, N_CHIPS, N // N_CHIPS).transpose(0, 2, 1, 3)


class _Lazy:
    def __init__(self, fn):
        self.fn, self.value = fn, None


class _LayerWeights(dict):
    def __getitem__(self, key):
        v = dict.__getitem__(self, key)
        if isinstance(v, _Lazy):
            if v.value is None:
                v.value = v.fn()
            return v.value
        return v


def _layer_weights(i, W, big, on_grad):
    e = i // 2
    wl = _LayerWeights({
        "norm_mix": W["norm_mix"][i][None, :], "norm_ffn": W["norm_ffn"][i][None, :],
        "ffn_w_up": _Lazy(lambda: big("ffn_w_up", i)), "ffn_w_down": _Lazy(lambda: big("ffn_w_down", i)),
        "ffn_conv_w8": _pad_rows8(W["ffn_conv_w"][i]), "ffn_conv_b": W["ffn_conv_b"][i][None, :],
        "on_grad": lambda name, pieces: on_grad(name, i if name.startswith("ffn") else e, pieces)})
    if i % 2 == 0:
        wl.update({"w_in": _Lazy(lambda: _even_reorder(big("ev_w_in", e))),
                   "w_out": _Lazy(lambda: big("ev_w_out", e)),
                   "gdn_conv_w8": _pad_rows8(W["gdn_conv_w"][e]),
                   "alog_row": _lane_row(W["gdn_a_log"][e]), "dt_row": _lane_row(W["gdn_dt_bias"][e]),
                   "gdn_norm": W["gdn_norm"][e][None, :], "pool_w": W["pool_w"][e],
                   "pool_scale": W["pool_scale"][e][None, :]})
    else:
        wl.update({"w_in": _Lazy(lambda: big("od_w_in", e)), "w_out": _Lazy(lambda: big("od_w_out", e)),
                   "qk_gain8": _pad_rows8(jnp.stack([W["att_q_norm"][e], W["att_k_norm"][e]]))})
    return wl


def local_step(x, target, mod, W, big, on_grad):
    depth = mod.shape[0]
    row = lambda i, k: mod[i, k][None, :]
    saved, wls = [], []
    pending = None
    for i in range(depth):
        tag = f"l{i}"
        wl = _layer_weights(i, W, big, on_grad)
        if pending is None:
            hm = modnorm_fwd(x, wl["norm_mix"], row(i, 1), row(i, 0), f"{tag}_mix_norm")
        else:
            x, hm = res_norm_fwd(x, *pending, wl["norm_mix"], row(i, 1), row(i, 0), f"{tag}_mix_norm")
        y, s_mix = (_even_fwd if i % 2 == 0 else _odd_fwd)(tag, hm, wl)
        x1, hf = res_norm_fwd(x, row(i, 2), y, wl["norm_ffn"], row(i, 4), row(i, 3), f"{tag}_ffn_norm")
        f, s_ffn = _ffn_fwd(tag, hf, wl)
        saved.append((x, y, s_mix, x1, f, s_ffn))
        wls.append(wl)
        x, pending = x1, (row(i, 5), f)
    sq, dx = loss_head(gres_fwd(x, *pending, "last_res"), target, "loss_head")

    dmod, grads = [None] * depth, [None] * depth
    df, dg_f = gres_bwd(dx, *pending, "last_res_bwd")
    for i in reversed(range(depth)):
        tag = f"l{i}"
        x, y, s_mix, x1, f, s_ffn = saved[i]
        dhf, g_ffn = _ffn_bwd(tag, df, s_ffn, wls[i])
        dx, dgain_f, dsc_f, dsh_f, dy, dg_m = norm_res_bwd(
            x1, wls[i]["norm_ffn"], row(i, 4), dhf, dx, row(i, 2), y, f"{tag}_ffn_norm_bwd")
        dhm, g_mix = (_even_bwd if i % 2 == 0 else _odd_bwd)(tag, dy, s_mix, wls[i])
        dmod_f = [dsh_f, dsc_f, dg_f]
        if i > 0:
            dx, dgain_m, dsc_m, dsh_m, df, dg_f = norm_res_bwd(
                x, wls[i]["norm_mix"], row(i, 1), dhm, dx, row(i - 1, 5), saved[i - 1][4], f"{tag}_mix_norm_bwd")
        else:
            dx, dgain_m, dsc_m, dsh_m = modnorm_bwd(x, wls[i]["norm_mix"], row(i, 1), dhm, dx, f"{tag}_mix_norm_bwd")
        dmod[i] = jnp.concatenate([dsh_m, dsc_m, dg_m] + dmod_f, axis=0)
        grads[i] = {"norm_mix": dgain_m[0], "norm_ffn": dgain_f[0], **g_mix, **g_ffn}
    return sq, dx, jnp.stack(dmod), grads


def _place():
    return lax.axis_index("x"), lax.axis_index("y"), lax.axis_index("c")


def _other_chips(mx, my):
    return [(1 - mx, my), (mx, 1 - my), (1 - mx, 1 - my)]


def _sems(n):
    return [DMA_SEM((n,)), DMA_SEM((n,))]


def _put(buf, block, index):
    return lax.dynamic_update_index_in_dim(buf, block, index, 0)


def gather8_ride(x, then):
    def parts(ins, outs, sems):
        (x_ref,), (out_ref,), (send_sems, recv_sems) = ins, outs, sems
        mx, my, mc = _place()
        me, sibling = (mx, my, mc), (mx, my, 1 - mc)
        chips = _other_chips(mx, my)

        def slot(px, py, pc):
            return out_ref.at[4 * px + 2 * py + pc]

        def copy(k, block, to, src=None):
            return pltpu.make_async_remote_copy(
                src_ref=slot(*block) if src is None else src, dst_ref=slot(*block),
                send_sem=send_sems.at[k], recv_sem=recv_sems.at[k], device_id=to, device_id_type=MESH)

        first = lambda: ([copy(0, me, sibling, src=x_ref)]
                         + [copy(1 + j, me, (*chip, mc), src=x_ref) for j, chip in enumerate(chips)])
        passed = lambda: [copy(4 + j, (*chip, mc), sibling) for j, chip in enumerate(chips)]
        landed = lambda: [copy(1 + j, (*chip, mc), me) for j, chip in enumerate(chips)]
        from_sibling = lambda: ([copy(0, sibling, me)]
                                + [copy(4 + j, (*chip, 1 - mc), me) for j, chip in enumerate(chips)])
        return first, passed, landed, from_sibling

    def start(ins, outs, sems):
        first, _, _, _ = parts(ins, outs, sems)
        for cp in first():
            cp.start()

    def mid(ins, outs, sems):
        _, passed, landed, _ = parts(ins, outs, sems)
        for cp, fwd in zip(landed(), passed()):
            cp.wait_recv()
            fwd.start()

    def finish(ins, outs, sems):
        first, passed, _, from_sibling = parts(ins, outs, sems)
        for cp in from_sibling():
            cp.wait_recv()
        for cp in first() + passed():
            cp.wait_send()

    def landed_all(outs):
        mx, my, mc = _place()
        then(_put(outs[0], x, 4 * mx + 2 * my + mc))

    return Ride([x], [jax.ShapeDtypeStruct((8,) + x.shape, x.dtype)], _sems(7), start, finish,
                landed_all, mid=mid, heavy=True)


def all_gather8(x):
    box = []
    waiting = list(_RIDES)
    _RIDES[:] = [gather8_ride(x, box.append)]
    flush_rides()
    _RIDES[:] = waiting + _RIDES
    return box[0]


def sibling_halves_ride(p, then):
    def copy(ins, outs, sems):
        (p_ref,), (got_ref,), (send_sems, recv_sems) = ins, outs, sems
        mx, my, mc = _place()
        return pltpu.make_async_remote_copy(src_ref=p_ref.at[1 - mc], dst_ref=got_ref, send_sem=send_sems.at[0],
                                            recv_sem=recv_sems.at[0], device_id=(mx, my, 1 - mc), device_id_type=MESH)

    def start(ins, outs, sems):
        copy(ins, outs, sems).start()

    def finish(ins, outs, sems):
        cp = copy(ins, outs, sems)
        cp.wait_send()
        cp.wait_recv()

    def landed(outs):
        then(lax.dynamic_index_in_dim(p, _place()[2], 0, keepdims=False), outs[0])

    return Ride([p], [jax.ShapeDtypeStruct(p.shape[1:], p.dtype)], _sems(1), start, finish, landed)


def sibling_pair_ride(r, then):
    def copy(ins, outs, sems):
        (r_ref,), (got_ref,), (send_sems, recv_sems) = ins, outs, sems
        mx, my, mc = _place()
        return pltpu.make_async_remote_copy(src_ref=r_ref, dst_ref=got_ref, send_sem=send_sems.at[0],
                                            recv_sem=recv_sems.at[0], device_id=(mx, my, 1 - mc), device_id_type=MESH)

    def start(ins, outs, sems):
        copy(ins, outs, sems).start()

    def finish(ins, outs, sems):
        cp = copy(ins, outs, sems)
        cp.wait_send()
        cp.wait_recv()

    def landed(outs):
        then(jnp.where(_place()[2] == 0, jnp.stack([r, outs[0]]), jnp.stack([outs[0], r])))

    return Ride([r], [jax.ShapeDtypeStruct(r.shape, r.dtype)], _sems(1), start, finish, landed)


def chip_scatter_ride(p, then):
    def parts(ins, outs, sems):
        (p_ref,), (out_ref,), (send_sems, recv_sems) = ins, outs, sems
        mx, my, mc = _place()
        mine = 2 * mx + my
        chips = _other_chips(mx, my)
        sends = [pltpu.make_async_remote_copy(
            src_ref=p_ref.at[2 * chip[0] + chip[1]], dst_ref=out_ref.at[mine], send_sem=send_sems.at[k],
            recv_sem=recv_sems.at[k], device_id=(*chip, mc), device_id_type=MESH) for k, chip in enumerate(chips)]
        recvs = lambda: [pltpu.make_async_remote_copy(
            src_ref=p_ref.at[mine], dst_ref=out_ref.at[2 * chip[0] + chip[1]], send_sem=send_sems.at[k],
            recv_sem=recv_sems.at[k], device_id=(*chip, mc), device_id_type=MESH) for k, chip in enumerate(chips)]
        return sends, recvs

    def start(ins, outs, sems):
        sends, _ = parts(ins, outs, sems)
        for cp in sends:
            cp.start()

    def finish(ins, outs, sems):
        sends, recvs = parts(ins, outs, sems)
        for cp in recvs():
            cp.wait_recv()
        for cp in sends:
            cp.wait_send()

    def landed(outs):
        mx, my, _ = _place()
        mine = 2 * mx + my
        then(_put(outs[0], lax.dynamic_index_in_dim(p, mine, 0, keepdims=False), mine))

    return Ride([p], [jax.ShapeDtypeStruct(p.shape, p.dtype)], _sems(3), start, finish, landed, heavy=True)


def _stream_rows(R, C):
    return _rows(R, max(16, (256 * 1536 // C) // 16 * 16))


def cast_bf16(w, name):
    R, C = w.shape
    tb = _stream_rows(R, C)

    def body(w_ref, o_ref):
        o_ref[...] = w_ref[...].astype(BF16)

    blk = pl.BlockSpec((tb, C), lambda i: (i, 0))
    return _pcall(body, name=name, grid=(R // tb,), in_specs=[blk], out_specs=blk,
                          out_shape=jax.ShapeDtypeStruct((R, C), BF16), compiler_params=_params(1))(w)


def sum_slots(g, name):
    n, R, C = g.shape
    tb = _stream_rows(R, C)

    def body(*refs):
        acc = refs[0][...].astype(F32)
        for r in refs[1:n]:
            acc = acc + r[...].astype(F32)
        refs[n][...] = acc

    specs = [pl.BlockSpec((None, tb, C), functools.partial(lambda k, i: (k, i, 0), k)) for k in range(n)]
    return _pcall(body, name=name, grid=(R // tb,), in_specs=specs,
                          out_specs=pl.BlockSpec((tb, C), lambda i: (i, 0)),
                          out_shape=jax.ShapeDtypeStruct((R, C), F32), compiler_params=_params(1))(*([g] * n))


def add_to_bf16(a, b, name):
    R, C = a.shape
    tb = _stream_rows(R, C)

    def body(a_ref, b_ref, o_ref):
        o_ref[...] = (a_ref[...] + b_ref[...]).astype(BF16)

    blk = pl.BlockSpec((tb, C), lambda i: (i, 0))
    return _pcall(body, name=name, grid=(R // tb,), in_specs=[blk, blk], out_specs=blk,
                          out_shape=jax.ShapeDtypeStruct((R, C), BF16), compiler_params=_params(1))(a, b)


def ada_fwd(c_all, ada_w, bias, name):
    n, D, Cs = ada_w.shape
    tn = _tile(Cs, 512)

    def body(c_ref, w_ref, b_ref, o_ref):
        o_ref[...] = _bdot(_silu(c_ref[...]), w_ref[...], NN) + b_ref[...]

    return _pcall(
        body, name=name, grid=(n, Cs // tn),
        in_specs=[pl.BlockSpec((8, D), lambda l, j: (0, 0)),
                  pl.BlockSpec((None, D, tn), lambda l, j: (l, 0, j)),
                  pl.BlockSpec((None, 1, tn), lambda l, j: (l, 0, j))],
        out_specs=pl.BlockSpec((None, 8, tn), lambda l, j: (l, 0, j)),
        out_shape=jax.ShapeDtypeStruct((n, 8, Cs), F32), compiler_params=_params(2))(c_all, ada_w, bias)


def ada_bwd(c16, dmod16, name):
    n, _, Cs = dmod16.shape
    D = c16.shape[1]
    tn = _tile(Cs, 512)

    def body(c_ref, d_ref, o_ref):
        o_ref[...] = _bdot(_silu(c_ref[...]), d_ref[...], TN)

    return _pcall(
        body, name=name, grid=(n, Cs // tn),
        in_specs=[pl.BlockSpec((16, D), lambda l, j: (0, 0)),
                  pl.BlockSpec((None, 16, tn), lambda l, j: (l, 0, j))],
        out_specs=pl.BlockSpec((None, D, tn), lambda l, j: (l, 0, j)),
        out_shape=jax.ShapeDtypeStruct((n, D, Cs), F32), compiler_params=_params(2))(c16, dmod16)


WEIGHTS = ["ada_w", "ada_b", "norm_mix", "norm_ffn", "ev_w_in", "ev_w_out", "gdn_conv_w", "gdn_a_log",
           "gdn_dt_bias", "gdn_norm", "pool_w", "pool_scale", "od_w_in", "od_w_out", "att_q_norm",
           "att_k_norm", "ffn_w_up", "ffn_conv_w", "ffn_conv_b", "ffn_w_down"]
COL_SHARDED = ("ev_w_in", "od_w_in", "ffn_w_up")
ROW_SHARDED = ("ev_w_out", "od_w_out", "ffn_w_down")


def _pack(parts):
    rows, offs = [], []
    at = 0
    for p in parts:
        flat = p.reshape(-1).astype(F32)
        n = -(-flat.shape[0] // LANE)
        rows.append(jnp.pad(flat, (0, n * LANE - flat.shape[0])).reshape(n, LANE))
        offs.append((at, n))
        at += n
    pad = -at % 16
    if pad:
        rows.append(jnp.zeros((pad, LANE), F32))
    return jnp.concatenate(rows, axis=0), offs


def _unpack(buf, off, shape):
    at, n = off
    lead = buf.shape[:-2]
    flat = buf[..., at:at + n, :].reshape(lead + (n * LANE,))
    return flat[..., :math.prod(shape)].reshape(lead + tuple(shape))


def submit_weight_gather(store, key, shard, col_sharded, mc):
    R, C = shard.shape
    half = lax.dynamic_index_in_dim(shard.reshape(2, R // 2, C), mc, 0, keepdims=False)

    def landed(g):
        g = g.reshape(N_CHIPS, R, C)
        store[key] = g.transpose(1, 0, 2).reshape(R, N_CHIPS * C) if col_sharded else g.reshape(N_CHIPS * R, C)

    submit_ride(gather8_ride(half, landed))


def submit_grad_reduce(store, key, pieces, col_sharded):
    _, _, R, C = pieces.shape
    tag = f"{key[0]}{key[1]}"

    def paired(out):
        store[key] = out.reshape(2 * R, C) if col_sharded else out.transpose(1, 0, 2).reshape(R, 2 * C)

    def scattered(got):
        submit_ride(sibling_pair_ride(sum_slots(got, f"gsum_{tag}"), paired))

    def swapped(keep, got):
        chip_sum = add_to_bf16(keep.reshape(N_CHIPS * R, C), got.reshape(N_CHIPS * R, C), f"gadd_{tag}")
        submit_ride(chip_scatter_ride(chip_sum.reshape(N_CHIPS, R, C), scattered))

    submit_ride(sibling_halves_ride(pieces, swapped))


def kernel(x, c, ada_w, ada_b, norm_mix, norm_ffn, ev_w_in, ev_w_out, gdn_conv_w, gdn_a_log, gdn_dt_bias, gdn_norm, pool_w, pool_scale, od_w_in, od_w_out, att_q_norm, att_k_norm, ffn_w_up, ffn_conv_w, ffn_conv_b, ffn_w_down, loss_target, m_ada_w, m_ada_b, m_norm_mix, m_norm_ffn, m_ev_w_in, m_ev_w_out, m_gdn_conv_w, m_gdn_a_log, m_gdn_dt_bias, m_gdn_norm, m_pool_w, m_pool_scale, m_od_w_in, m_od_w_out, m_att_q_norm, m_att_k_norm, m_ffn_w_up, m_ffn_conv_w, m_ffn_conv_b, m_ffn_w_down, v_ada_w, v_ada_b, v_norm_mix, v_norm_ffn, v_ev_w_in, v_ev_w_out, v_gdn_conv_w, v_gdn_a_log, v_gdn_dt_bias, v_gdn_norm, v_pool_w, v_pool_scale, v_od_w_in, v_od_w_out, v_att_q_norm, v_att_k_norm, v_ffn_w_up, v_ffn_conv_w, v_ffn_conv_b, v_ffn_w_down):
    local = dict(ada_w=ada_w, ada_b=ada_b, norm_mix=norm_mix, norm_ffn=norm_ffn, ev_w_in=ev_w_in,
                 ev_w_out=ev_w_out, gdn_conv_w=gdn_conv_w, gdn_a_log=gdn_a_log, gdn_dt_bias=gdn_dt_bias,
                 gdn_norm=gdn_norm, pool_w=pool_w, pool_scale=pool_scale, od_w_in=od_w_in, od_w_out=od_w_out,
                 att_q_norm=att_q_norm, att_k_norm=att_k_norm, ffn_w_up=ffn_w_up, ffn_conv_w=ffn_conv_w,
                 ffn_conv_b=ffn_conv_b, ffn_w_down=ffn_w_down)
    moments_m = dict(zip(WEIGHTS, (m_ada_w, m_ada_b, m_norm_mix, m_norm_ffn, m_ev_w_in, m_ev_w_out,
                                   m_gdn_conv_w, m_gdn_a_log, m_gdn_dt_bias, m_gdn_norm, m_pool_w, m_pool_scale,
                                   m_od_w_in, m_od_w_out, m_att_q_norm, m_att_k_norm, m_ffn_w_up, m_ffn_conv_w,
                                   m_ffn_conv_b, m_ffn_w_down)))
    moments_v = dict(zip(WEIGHTS, (v_ada_w, v_ada_b, v_norm_mix, v_norm_ffn, v_ev_w_in, v_ev_w_out,
                                   v_gdn_conv_w, v_gdn_a_log, v_gdn_dt_bias, v_gdn_norm, v_pool_w, v_pool_scale,
                                   v_od_w_in, v_od_w_out, v_att_q_norm, v_att_k_norm, v_ffn_w_up, v_ffn_conv_w,
                                   v_ffn_conv_b, v_ffn_w_down)))
    _RIDES.clear()
    _IDS[0] = 0
    mx, my, mc = _place()
    chip = 2 * mx + my
    T, D = x.shape[1], x.shape[2]
    depth = ada_w.shape[0]
    ada_cols = ada_w.shape[2]

    buf, offs = _pack([c, gdn_conv_w, ffn_conv_w])
    gathered = all_gather8(buf)
    c_all = _unpack(gathered, offs[0], (D,))
    by_chip = gathered[0::2]
    gdn_conv_full = jnp.concatenate(list(_unpack(by_chip, offs[1], gdn_conv_w.shape)), axis=-1)
    ffn_conv_full = jnp.concatenate(list(_unpack(by_chip, offs[2], ffn_conv_w.shape)), axis=-1)

    bias = lax.dynamic_slice_in_dim(ada_b, chip * ada_cols, ada_cols, axis=1)[:, None, :]
    mod_part = ada_fwd(c_all, ada_w, bias, "ada_fwd")
    mod_all = all_gather8(mod_part)[0::2]
    mod_all = mod_all.transpose(1, 2, 0, 3).reshape(depth, 8, N_CHIPS * ada_cols)
    mod = lax.dynamic_index_in_dim(mod_all, 4 * mx + 2 * my + mc, 1, keepdims=False).reshape(depth, 6, D)

    full_w, big_grad = {}, {}
    order = []
    for i in range(depth):
        mixer = ("ev_w_in", "ev_w_out") if i % 2 == 0 else ("od_w_in", "od_w_out")
        order += [(name, i // 2) for name in mixer] + [("ffn_w_up", i), ("ffn_w_down", i)]
    shards = {name: cast_bf16(local[name].reshape(-1, local[name].shape[-1]), f"cast_{name}")
              .reshape(local[name].shape) for name in COL_SHARDED + ROW_SHARDED}
    for name, e in order:
        submit_weight_gather(full_w, (name, e), shards[name][e], name in COL_SHARDED, mc)

    def big(name, e):
        flush_rides(until=lambda: (name, e) in full_w)
        return full_w[(name, e)]

    def on_grad(name, e, pieces):
        submit_grad_reduce(big_grad, (name, e), pieces, name in COL_SHARDED)

    W = dict(local)
    W["gdn_conv_w"], W["ffn_conv_w"] = gdn_conv_full, ffn_conv_full
    sq, dx, dmod, grads = local_step(x[0], loss_target[0], mod, W, big, on_grad)
    loss = lax.psum(0.5 * jnp.sum(sq) / D, ("x", "y", "c"))

    small = ["norm_mix", "norm_ffn", "gdn_conv_w", "gdn_a_log", "gdn_dt_bias", "gdn_norm", "pool_w",
             "pool_scale", "att_q_norm", "att_k_norm", "ffn_conv_w", "ffn_conv_b"]
    full = {name: jnp.stack([g[name] for g in grads if name in g]) for name in small}
    grad = {}
    buf, offs = _pack([dmod] + [full[name] for name in small])
    gathered = all_gather8(buf)
    summed = sum_slots(gathered, "sum_small_grads")
    grad["ada_b"] = _unpack(summed, offs[0], ada_b.shape)
    for k, name in enumerate(small):
        grad[name] = _unpack(summed, offs[1 + k], full[name].shape)
    for name, cols in (("gdn_conv_w", gdn_conv_w.shape[-1]), ("ffn_conv_w", ffn_conv_w.shape[-1])):
        grad[name] = lax.dynamic_slice_in_dim(grad[name], chip * cols, cols, axis=2)

    dmod_all = _unpack(gathered, offs[0], (depth, N_CHIPS * ada_cols))
    dmod_mine = lax.dynamic_slice_in_dim(dmod_all, chip * ada_cols, ada_cols, axis=2).transpose(1, 0, 2)
    grad["ada_w"] = ada_bwd(jnp.pad(c_all, ((0, 8), (0, 0))), jnp.pad(dmod_mine, ((0, 0), (0, 8), (0, 0))),
                            "ada_bwd")

    deltas, new_m, new_v = {}, {}, {}
    large = ("ffn_w_down", "ffn_w_up", "od_w_out", "od_w_in", "ev_w_out", "ev_w_in")
    for name in [n for n in WEIGHTS if n not in large] + list(large):
        if name in large:
            keys = [k for k in order if k[0] == name]
            flush_rides(until=lambda: all(k in big_grad for k in keys))
            grad[name] = jnp.stack([big_grad[k] for k in keys])
        deltas[name], new_m[name], new_v[name] = adamw(local[name], grad[name], moments_m[name],
                                                       moments_v[name], f"adamw_{name}")
    flush_rides()
    return (loss, dx[None], *[grad[n] for n in WEIGHTS], *[deltas[n] for n in WEIGHTS],
            *[new_m[n] for n in WEIGHTS], *[new_v[n] for n in WEIGHTS])
```

```python
import functools
import math

import jax
import jax.numpy as jnp
from jax import lax
from jax.experimental import pallas as pl
from jax.experimental.pallas import tpu as pltpu

F32 = jnp.float32
BF16 = jnp.bfloat16
LANE = 128
SUBLANE = 8
VMEM_LIMIT = 56 * 1024 * 1024
MESH = pl.DeviceIdType.MESH
N_CHIPS = 4

RMS_EPS = 1e-6
GDN_H = 4
HD = 128
GDN_CHUNK = 64
GDN_STEP = 4
GDN_CONV = 4
FFN_CONV = 3
POOL_G = 4
ATT_H = 8
ATT_BLK = 128
DIL = (1, 4, 16)
EVEN_COLS = 2568
EVEN_PAD = 2688
ADAM_LR, ADAM_B1, ADAM_B2, ADAM_EPS, ADAM_WD, ADAM_STEP = 0.001, 0.9, 0.999, 1e-08, 0.01, 10
NEG = -1e30

NN = (((1,), (0,)), ((), ()))
NT = (((1,), (1,)), ((), ()))
TN = (((0,), (0,)), ((), ()))
BNN = (((2,), (1,)), ((0,), (0,)))
BNT = (((2,), (2,)), ((0,), (0,)))
BTN = (((1,), (1,)), ((0,), (0,)))


def _params(n_grid):
    return pltpu.CompilerParams(dimension_semantics=("arbitrary",) * n_grid,
                                vmem_limit_bytes=VMEM_LIMIT)


HBM = pl.BlockSpec(memory_space=pltpu.HBM)
DMA_SEM = pltpu.SemaphoreType.DMA


class Ride:
    def __init__(self, inputs, out_shapes, sems, start, finish, then, mid=None, heavy=False):
        self.inputs, self.out_shapes, self.sems = list(inputs), list(out_shapes), list(sems)
        self.start, self.mid, self.finish, self.then, self.heavy = start, mid, finish, then, heavy


_RIDES = []


def submit_ride(ride):
    _RIDES.append(ride)


def flush_rides(until=None):
    while _RIDES and not (until is not None and until()):
        ride = _RIDES.pop(0)

        def body(*refs, ride=ride):
            a, b = len(ride.inputs), len(ride.inputs) + len(ride.out_shapes)
            ride.start(refs[:a], refs[a:b], refs[b:])
            if ride.mid is not None:
                ride.mid(refs[:a], refs[a:b], refs[b:])
            ride.finish(refs[:a], refs[a:b], refs[b:])

        outs = pl.pallas_call(body, name=f"exchange{_next_id()}", in_specs=[HBM] * len(ride.inputs),
                              out_specs=[HBM] * len(ride.out_shapes), out_shape=ride.out_shapes,
                              scratch_shapes=ride.sems)(*ride.inputs)
        ride.then(list(outs))


_IDS = [0]


def _next_id():
    _IDS[0] += 1
    return _IDS[0]


def _pcall(body, *, name, grid, in_specs, out_specs, out_shape, scratch_shapes=(), compiler_params=None,
           long_call=False):
    del compiler_params
    single = not isinstance(out_shape, (list, tuple))
    outs = [out_shape] if single else list(out_shape)
    ospecs = [out_specs] if single else list(out_specs)
    total = math.prod(grid)
    fits = [k for k, r in enumerate(_RIDES) if long_call or not r.heavy] if total > 1 else []
    ride = _RIDES.pop(fits[0]) if fits else None
    if ride is None:
        call = pl.pallas_call(body, name=name, grid=grid, in_specs=list(in_specs), out_specs=ospecs,
                              out_shape=outs, scratch_shapes=list(scratch_shapes),
                              compiler_params=_params(len(grid)))

        def run_plain(*args):
            res = call(*args)
            return res[0] if single else res
        return run_plain

    n_in, n_out, n_scr = len(in_specs), len(outs), len(scratch_shapes)
    r_in, r_out = len(ride.inputs), len(ride.out_shapes)

    def carrying_body(*refs):
        at = 0
        ins = refs[at:at + n_in]; at += n_in
        r_ins = refs[at:at + r_in]; at += r_in
        os_ = refs[at:at + n_out]; at += n_out
        r_outs = refs[at:at + r_out]; at += r_out
        scr = refs[at:at + n_scr]; at += n_scr
        r_sems = refs[at:]
        step = pl.program_id(0)
        for ax in range(1, len(grid)):
            step = step * grid[ax] + pl.program_id(ax)

        @pl.when(step == 0)
        def _():
            ride.start(r_ins, r_outs, r_sems)

        body(*ins, *os_, *scr)

        if ride.mid is not None:
            @pl.when(step == total // 2)
            def _():
                ride.mid(r_ins, r_outs, r_sems)

        @pl.when(step == total - 1)
        def _():
            ride.finish(r_ins, r_outs, r_sems)

    call = pl.pallas_call(
        carrying_body, name=name, grid=grid, in_specs=list(in_specs) + [HBM] * r_in,
        out_specs=ospecs + [HBM] * r_out, out_shape=outs + ride.out_shapes,
        scratch_shapes=list(scratch_shapes) + ride.sems, compiler_params=_params(len(grid)))

    def run_carrying(*args):
        res = call(*args, *ride.inputs)
        ride.then(list(res[n_out:]))
        return res[0] if single else list(res[:n_out])
    return run_carrying


def _tile(n, target):
    if n <= target:
        return n
    best = None
    for t in range(LANE, target + 1, LANE):
        if n % t == 0:
            best = t
    assert best is not None, (n, target)
    return best


def _rows(n, target):
    if n <= target:
        return n
    best = None
    for t in range(16, target + 1, 16):
        if n % t == 0:
            best = t
    assert best is not None, (n, target)
    return best


def _bdot(a, b, dims):
    return lax.dot_general(a.astype(BF16), b.astype(BF16), dims, preferred_element_type=F32)


def _split(a):
    hi = a.astype(BF16)
    return hi, (a - hi.astype(F32)).astype(BF16)


def _dot3(a, b, dims):
    ah, al = _split(a)
    bh, bl = _split(b)
    d = lambda p, q: lax.dot_general(p, q, dims, preferred_element_type=F32)
    return d(ah, bh) + d(ah, bl) + d(al, bh)


def _sigmoid(x):
    return 1.0 / (1.0 + jnp.exp(-x))


def _silu(x):
    return x * _sigmoid(x)


def matmul(a, b, mode, out_dtype, name, tm=1024, tn=1536, tk=1536, pieces=None):
    a_parts = a.shape[0] if a.ndim == 3 else 1
    b_parts = b.shape[0] if b.ndim == 3 else 1
    if mode == "nn":
        (M, K), (K2, N) = a.shape, b.shape
    elif mode == "nt":
        M, K = a.shape[-2], a.shape[-1] * a_parts
        N, K2 = b.shape
    else:
        K, M = a.shape
        K2, N = b.shape[-2], b.shape[-1] * b_parts
    assert K == K2, (a.shape, b.shape, mode)
    if pieces == "col":
        tm, tn = _tile(M // 2, tm), _tile(N // N_CHIPS, tn)
    elif pieces == "row":
        quarter = M // N_CHIPS
        tm = 2 * quarter if (2 * quarter) % LANE == 0 else M
        tn = _tile(N // 2, tn)
    else:
        tm, tn = _tile(M, tm), _tile(N, tn)
    if mode == "tn":
        tk = 2048 if K % 2048 == 0 else tk
    tk = _tile(K // a_parts, tk)
    nk = K // tk
    assert (N // b_parts) % tn == 0, (N, b_parts, tn)
    dims = {"nn": NN, "nt": NT, "tn": TN}[mode]
    if mode == "tn":
        a_spec = pl.BlockSpec((tk, tm), lambda i, j, k: (k, i))
    elif a.ndim == 3:
        nkp = (K // a_parts) // tk
        a_spec = pl.BlockSpec((None, tm, tk), lambda i, j, k: (k // nkp, i, k % nkp))
    else:
        a_spec = pl.BlockSpec((tm, tk), lambda i, j, k: (i, k))
    if mode == "nt":
        b_spec = pl.BlockSpec((tn, tk), lambda i, j, k: (j, k))
    elif b.ndim == 3:
        njp = (N // b_parts) // tn
        b_spec = pl.BlockSpec((None, tk, tn), lambda i, j, k: (j // njp, k, j % njp))
    else:
        b_spec = pl.BlockSpec((tk, tn), lambda i, j, k: (k, j))

    out_spec = pl.BlockSpec((tm, tn), lambda i, j, k: (i, j))
    out_shape = jax.ShapeDtypeStruct((M, N), out_dtype)
    rows_per_slot = tm
    if pieces == "col":
        nih, njc = (M // 2) // tm, (N // N_CHIPS) // tn
        out_spec = pl.BlockSpec((None, None, tm, tn), lambda i, j, k: (i // nih, j // njc, i % nih, j % njc))
        out_shape = jax.ShapeDtypeStruct((2, N_CHIPS, M // 2, N // N_CHIPS), out_dtype)
    elif pieces == "row":
        rows_per_slot = M // N_CHIPS
        njh = (N // 2) // tn
        out_spec = pl.BlockSpec((None, tm // rows_per_slot, rows_per_slot, tn),
                                lambda i, j, k: (j // njh, i, 0, j % njh))
        out_shape = jax.ShapeDtypeStruct((2, N_CHIPS, rows_per_slot, N // 2), out_dtype)

    def store(o_ref, acc):
        if pieces == "row":
            for s in range(tm // rows_per_slot):
                o_ref[s] = acc[s * rows_per_slot:(s + 1) * rows_per_slot, :].astype(out_dtype)
        else:
            o_ref[...] = acc.astype(out_dtype)

    def body(a_ref, b_ref, o_ref, acc_ref):
        k = pl.program_id(2)
        p = _bdot(a_ref[...], b_ref[...], dims)
        if nk == 1:
            store(o_ref, p)
            return

        @pl.when(k == 0)
        def _():
            acc_ref[...] = p

        @pl.when(k > 0)
        def _():
            acc_ref[...] += p

        @pl.when(k == nk - 1)
        def _():
            store(o_ref, acc_ref[...])

    return _pcall(
        body, name=name, grid=(M // tm, N // tn, nk),
        in_specs=[a_spec, b_spec], out_specs=out_spec, out_shape=out_shape,
        scratch_shapes=[pltpu.VMEM((tm, tn), F32)],
        long_call=True)(a, b)


def _row_spec(d):
    return pl.BlockSpec((1, d), lambda i: (0, 0))


def modnorm_fwd(x, gain, sc, sh, name):
    T, D = x.shape
    tb = _rows(T, 512)

    def body(x_ref, g_ref, sc_ref, sh_ref, o_ref):
        xv = x_ref[...]
        r = lax.rsqrt(jnp.mean(xv * xv, axis=-1, keepdims=True) + RMS_EPS)
        o_ref[...] = ((xv * r) * g_ref[...] * (1.0 + sc_ref[...]) + sh_ref[...]).astype(BF16)

    blk = pl.BlockSpec((tb, D), lambda i: (i, 0))
    return _pcall(
        body, name=name, grid=(T // tb,),
        in_specs=[blk, _row_spec(D), _row_spec(D), _row_spec(D)],
        out_specs=blk, out_shape=jax.ShapeDtypeStruct((T, D), BF16),
        compiler_params=_params(1))(x, gain, sc, sh)


def modnorm_bwd(x, gain, sc, dh, dres, name):
    T, D = x.shape
    tb = _rows(T, 512)

    def body(x_ref, g_ref, sc_ref, dh_ref, dres_ref, dx_ref, dg_ref, dsc_ref, dsh_ref):
        i = pl.program_id(0)
        xv = x_ref[...]
        r = lax.rsqrt(jnp.mean(xv * xv, axis=-1, keepdims=True) + RMS_EPS)
        n = xv * r
        dhv = dh_ref[...].astype(F32)
        gain_v, sc1 = g_ref[...], 1.0 + sc_ref[...]
        dn = dhv * (gain_v * sc1)
        dx_ref[...] = r * (dn - n * jnp.mean(dn * n, axis=-1, keepdims=True)) + dres_ref[...]
        dhn = dhv * n

        @pl.when(i == 0)
        def _():
            dg_ref[...] = jnp.zeros_like(dg_ref)
            dsc_ref[...] = jnp.zeros_like(dsc_ref)
            dsh_ref[...] = jnp.zeros_like(dsh_ref)

        dg_ref[...] += jnp.sum(dhn * sc1, axis=0, keepdims=True)
        dsc_ref[...] += jnp.sum(dhn * gain_v, axis=0, keepdims=True)
        dsh_ref[...] += jnp.sum(dhv, axis=0, keepdims=True)

    blk = pl.BlockSpec((tb, D), lambda i: (i, 0))
    row = jax.ShapeDtypeStruct((1, D), F32)
    return _pcall(
        body, name=name, grid=(T // tb,),
        in_specs=[blk, _row_spec(D), _row_spec(D), blk, blk],
        out_specs=[blk, _row_spec(D), _row_spec(D), _row_spec(D)],
        out_shape=[jax.ShapeDtypeStruct((T, D), F32), row, row, row],
        compiler_params=_params(1))(x, gain, sc, dh, dres)


def gres_fwd(x, g, y, name):
    T, D = x.shape
    tb = _rows(T, 512)

    def body(x_ref, g_ref, y_ref, o_ref):
        o_ref[...] = x_ref[...] + g_ref[...] * y_ref[...]

    blk = pl.BlockSpec((tb, D), lambda i: (i, 0))
    return _pcall(
        body, name=name, grid=(T // tb,), in_specs=[blk, _row_spec(D), blk], out_specs=blk,
        out_shape=jax.ShapeDtypeStruct((T, D), F32), compiler_params=_params(1))(x, g, y)


def gres_bwd(dx, g, y, name):
    T, D = dx.shape
    tb = _rows(T, 512)

    def body(dx_ref, g_ref, y_ref, dy_ref, dg_ref):
        i = pl.program_id(0)
        dxv = dx_ref[...]
        dy_ref[...] = (dxv * g_ref[...]).astype(BF16)

        @pl.when(i == 0)
        def _():
            dg_ref[...] = jnp.zeros_like(dg_ref)

        dg_ref[...] += jnp.sum(dxv * y_ref[...], axis=0, keepdims=True)

    blk = pl.BlockSpec((tb, D), lambda i: (i, 0))
    return _pcall(
        body, name=name, grid=(T // tb,), in_specs=[blk, _row_spec(D), blk],
        out_specs=[blk, _row_spec(D)],
        out_shape=[jax.ShapeDtypeStruct((T, D), BF16), jax.ShapeDtypeStruct((1, D), F32)],
        compiler_params=_params(1))(dx, g, y)


def res_norm_fwd(x, g, y, gain, sc, sh, name):
    T, D = x.shape
    tb = _rows(T, 512)

    def body(x_ref, g_ref, y_ref, gn_ref, sc_ref, sh_ref, x1_ref, h_ref):
        xv = x_ref[...] + g_ref[...] * y_ref[...]
        x1_ref[...] = xv
        r = lax.rsqrt(jnp.mean(xv * xv, axis=-1, keepdims=True) + RMS_EPS)
        h_ref[...] = ((xv * r) * gn_ref[...] * (1.0 + sc_ref[...]) + sh_ref[...]).astype(BF16)

    blk = pl.BlockSpec((tb, D), lambda i: (i, 0))
    row = _row_spec(D)
    return _pcall(
        body, name=name, grid=(T // tb,), in_specs=[blk, row, blk, row, row, row], out_specs=[blk, blk],
        out_shape=[jax.ShapeDtypeStruct((T, D), F32), jax.ShapeDtypeStruct((T, D), BF16)],
        compiler_params=_params(1))(x, g, y, gain, sc, sh)


def norm_res_bwd(x, gain, sc, dh, dres, g, y, name):
    T, D = x.shape
    tb = _rows(T, 512)

    def body(x_ref, gn_ref, sc_ref, dh_ref, dres_ref, g_ref, y_ref,
             dx_ref, dgn_ref, dsc_ref, dsh_ref, dy_ref, dg_ref):
        i = pl.program_id(0)
        xv = x_ref[...]
        r = lax.rsqrt(jnp.mean(xv * xv, axis=-1, keepdims=True) + RMS_EPS)
        n = xv * r
        dhv = dh_ref[...].astype(F32)
        gain_v, sc1 = gn_ref[...], 1.0 + sc_ref[...]
        dn = dhv * (gain_v * sc1)
        dx = r * (dn - n * jnp.mean(dn * n, axis=-1, keepdims=True)) + dres_ref[...]
        dx_ref[...] = dx
        dy_ref[...] = (dx * g_ref[...]).astype(BF16)
        dhn = dhv * n

        @pl.when(i == 0)
        def _():
            dgn_ref[...] = jnp.zeros_like(dgn_ref)
            dsc_ref[...] = jnp.zeros_like(dsc_ref)
            dsh_ref[...] = jnp.zeros_like(dsh_ref)
            dg_ref[...] = jnp.zeros_like(dg_ref)

        dgn_ref[...] += jnp.sum(dhn * sc1, axis=0, keepdims=True)
        dsc_ref[...] += jnp.sum(dhn * gain_v, axis=0, keepdims=True)
        dsh_ref[...] += jnp.sum(dhv, axis=0, keepdims=True)
        dg_ref[...] += jnp.sum(dx * y_ref[...], axis=0, keepdims=True)

    blk = pl.BlockSpec((tb, D), lambda i: (i, 0))
    row = _row_spec(D)
    row_shape = jax.ShapeDtypeStruct((1, D), F32)
    return _pcall(
        body, name=name, grid=(T // tb,),
        in_specs=[blk, row, row, blk, blk, row, blk],
        out_specs=[blk, row, row, row, blk, row],
        out_shape=[jax.ShapeDtypeStruct((T, D), F32), row_shape, row_shape, row_shape,
                   jax.ShapeDtypeStruct((T, D), BF16), row_shape],
        compiler_params=_params(1))(x, gain, sc, dh, dres, g, y)


def loss_head(y, target, name):
    T, D = y.shape
    tb = _rows(T, 512)

    def body(y_ref, t_ref, l_ref, dy_ref):
        i = pl.program_id(0)
        err = y_ref[...] - t_ref[...]
        dy_ref[...] = err * (1.0 / D)

        @pl.when(i == 0)
        def _():
            l_ref[...] = jnp.zeros_like(l_ref)

        sq = jnp.sum(err * err, axis=0, keepdims=True)
        tot = sq[:, 0:LANE]
        for k in range(1, D // LANE):
            tot = tot + sq[:, k * LANE:(k + 1) * LANE]
        l_ref[...] += tot

    blk = pl.BlockSpec((tb, D), lambda i: (i, 0))
    return _pcall(
        body, name=name, grid=(T // tb,), in_specs=[blk, blk],
        out_specs=[_row_spec(LANE), blk],
        out_shape=[jax.ShapeDtypeStruct((1, LANE), F32), jax.ShapeDtypeStruct((T, D), F32)],
        compiler_params=_params(1))(y, target)


def _back(ext, s):
    return ext if s == 0 else pltpu.roll(ext, s, 0)


def _ahead(ext, s):
    return ext if s == 0 else pltpu.roll(ext, ext.shape[0] - s, 0)


def _halo_prev(tb, h):
    return lambda i, j: (jnp.maximum(i * (tb // h) - 1, 0), j)


def _halo_next(tb, h, nrb):
    return lambda i, j: (jnp.minimum(i + 1, nrb - 1) * (tb // h), j)


FFN_TB, FFN_CB = 256, 1408
HALO16 = 16


def ffn_mid_fwd(up, conv_w8, conv_b, name):
    T, F2 = up.shape
    Fd = F2 // 2
    tb, cb = _rows(T, FFN_TB), _tile(Fd, FFN_CB)
    ncb = Fd // cb
    H = HALO16

    def body(g_ref, gp_ref, v_ref, w_ref, b_ref, o_ref):
        i = pl.program_id(0)
        g = g_ref[...].astype(F32)
        prev = jnp.where(i > 0, gp_ref[...].astype(F32), 0.0)
        ext = jnp.concatenate([prev, g], axis=0)
        w = w_ref[...]
        gc = w[2:3] * g + w[1:2] * _back(ext, 1)[H:] + w[0:1] * _back(ext, 2)[H:] + b_ref[...]
        o_ref[...] = (_silu(gc) * v_ref[...].astype(F32)).astype(BF16)

    return _pcall(
        body, name=name, grid=(T // tb, ncb),
        in_specs=[pl.BlockSpec((tb, cb), lambda i, j: (i, j)),
                  pl.BlockSpec((H, cb), _halo_prev(tb, H)),
                  pl.BlockSpec((tb, cb), lambda i, j: (i, j + ncb)),
                  pl.BlockSpec((SUBLANE, cb), lambda i, j: (0, j)),
                  pl.BlockSpec((1, cb), lambda i, j: (0, j))],
        out_specs=pl.BlockSpec((tb, cb), lambda i, j: (i, j)),
        out_shape=jax.ShapeDtypeStruct((T, Fd), BF16),
        long_call=True)(up, up, up, conv_w8, conv_b)


def ffn_mid_bwd(up, conv_w8, conv_b, dact, name):
    T, F2 = up.shape
    Fd = F2 // 2
    tb, cb = _rows(T, FFN_TB), _tile(Fd, FFN_CB)
    ncb, nrb = Fd // cb, T // tb
    H = HALO16

    def body(g_ref, gp_ref, gn_ref, v_ref, vn_ref, d_ref, dn_ref, w_ref, b_ref,
             dup_ref, dw_ref, db_ref):
        i = pl.program_id(1)
        g = g_ref[...].astype(F32)
        prev = jnp.where(i > 0, gp_ref[...].astype(F32), 0.0)
        ext = jnp.concatenate([prev, g, gn_ref[...].astype(F32)], axis=0)
        w = w_ref[...]
        e1, e2 = _back(ext, 1), _back(ext, 2)
        gc = (w[2:3] * ext + w[1:2] * e1 + w[0:1] * e2 + b_ref[...])[H:]
        val = jnp.concatenate([v_ref[...], vn_ref[...]], axis=0).astype(F32)
        dnext = jnp.where(i < nrb - 1, dn_ref[...].astype(F32), 0.0)
        da = jnp.concatenate([d_ref[...].astype(F32), dnext], axis=0)
        sg = _sigmoid(gc)
        dup_ref[1] = (da * gc * sg)[:tb].astype(BF16)
        dgc = da * val * (sg * (1.0 + gc * (1.0 - sg)))
        dup_ref[0] = (w[2:3] * dgc + w[1:2] * _ahead(dgc, 1) + w[0:1] * _ahead(dgc, 2))[:tb].astype(BF16)
        dc = dgc[:tb]

        @pl.when(i == 0)
        def _():
            dw_ref[...] = jnp.zeros_like(dw_ref)
            db_ref[...] = jnp.zeros_like(db_ref)

        dw_ref[2:3, :] += jnp.sum(dc * g, axis=0, keepdims=True)
        dw_ref[1:2, :] += jnp.sum(dc * e1[H:H + tb], axis=0, keepdims=True)
        dw_ref[0:1, :] += jnp.sum(dc * e2[H:H + tb], axis=0, keepdims=True)
        db_ref[...] += jnp.sum(dc, axis=0, keepdims=True)

    cur = lambda j, i: (i, j)
    prv = lambda j, i: _halo_prev(tb, H)(i, j)
    nxt = lambda j, i: _halo_next(tb, H, nrb)(i, j)
    return _pcall(
        body, name=name, grid=(ncb, nrb),
        in_specs=[pl.BlockSpec((tb, cb), cur), pl.BlockSpec((H, cb), prv), pl.BlockSpec((H, cb), nxt),
                  pl.BlockSpec((tb, cb), lambda j, i: (i, j + ncb)),
                  pl.BlockSpec((H, cb), lambda j, i: (jnp.minimum(i + 1, nrb - 1) * (tb // H), j + ncb)),
                  pl.BlockSpec((tb, cb), cur), pl.BlockSpec((H, cb), nxt),
                  pl.BlockSpec((SUBLANE, cb), lambda j, i: (0, j)),
                  pl.BlockSpec((1, cb), lambda j, i: (0, j))],
        out_specs=[pl.BlockSpec((2, tb, cb), lambda j, i: (0, i, j)),
                   pl.BlockSpec((SUBLANE, cb), lambda j, i: (0, j)),
                   pl.BlockSpec((1, cb), lambda j, i: (0, j))],
        out_shape=[jax.ShapeDtypeStruct((2, T, Fd), BF16),
                   jax.ShapeDtypeStruct((SUBLANE, Fd), F32), jax.ShapeDtypeStruct((1, Fd), F32)],
        long_call=True)(up, up, up, up, up, dact, dact, conv_w8, conv_b)


GDN_W = GDN_H * HD


def _head_l2norm(a, apply):
    parts = []
    for h in range(GDN_H):
        ah = a[:, h * HD:(h + 1) * HD]
        parts.append(ah * lax.rsqrt(jnp.sum(ah * ah, axis=-1, keepdims=True) + RMS_EPS))
    return jnp.where(apply, jnp.concatenate(parts, axis=1), a)


def _head_l2norm_bwd(a, dy, apply):
    parts = []
    for h in range(GDN_H):
        sl = slice(h * HD, (h + 1) * HD)
        ah, dh = a[:, sl], dy[:, sl]
        r = lax.rsqrt(jnp.sum(ah * ah, axis=-1, keepdims=True) + RMS_EPS)
        y = ah * r
        parts.append(r * (dh - y * jnp.sum(dh * y, axis=-1, keepdims=True)))
    return jnp.where(apply, jnp.concatenate(parts, axis=1), dy)


def gdn_conv_fwd(proj, w8, name):
    T = proj.shape[0]
    tb = _rows(T, 512)
    H = SUBLANE

    def body(x_ref, xp_ref, w_ref, o_ref):
        i, j = pl.program_id(0), pl.program_id(1)
        x = x_ref[...]
        prev = jnp.where(i > 0, xp_ref[...], 0.0)
        ext = jnp.concatenate([prev, x], axis=0)
        w = w_ref[...]
        c = (w[3:4] * x + w[2:3] * _back(ext, 1)[H:] + w[1:2] * _back(ext, 2)[H:]
             + w[0:1] * _back(ext, 3)[H:])
        o_ref[...] = _head_l2norm(_silu(c), j < 2)

    return _pcall(
        body, name=name, grid=(T // tb, 3),
        in_specs=[pl.BlockSpec((tb, GDN_W), lambda i, j: (i, j)),
                  pl.BlockSpec((H, GDN_W), _halo_prev(tb, H)),
                  pl.BlockSpec((SUBLANE, GDN_W), lambda i, j: (0, j))],
        out_specs=pl.BlockSpec((tb, GDN_W), lambda i, j: (i, j)),
        out_shape=jax.ShapeDtypeStruct((T, 3 * GDN_W), F32),
        compiler_params=_params(2))(proj, proj, w8)


def gdn_conv_bwd(proj, w8, dout, name):
    T = proj.shape[0]
    tb = _rows(T, 512)
    nrb = T // tb
    H = SUBLANE

    def body(x_ref, xp_ref, xn_ref, d_ref, dn_ref, w_ref, dx_ref, dw_ref):
        j, i = pl.program_id(0), pl.program_id(1)
        x = x_ref[...]
        prev = jnp.where(i > 0, xp_ref[...], 0.0)
        ext = jnp.concatenate([prev, x, xn_ref[...]], axis=0)
        w = w_ref[...]
        e1, e2, e3 = _back(ext, 1), _back(ext, 2), _back(ext, 3)
        c = (w[3:4] * ext + w[2:3] * e1 + w[1:2] * e2 + w[0:1] * e3)[H:]
        sg = _sigmoid(c)
        dnext = jnp.where(i < nrb - 1, dn_ref[...], 0.0)
        do = jnp.concatenate([d_ref[...], dnext], axis=0)
        da = _head_l2norm_bwd(c * sg, do, j < 2)
        dc = da * (sg * (1.0 + c * (1.0 - sg)))
        dx_ref[...] = (w[3:4] * dc + w[2:3] * _ahead(dc, 1) + w[1:2] * _ahead(dc, 2)
                       + w[0:1] * _ahead(dc, 3))[:tb].astype(BF16)
        dcc = dc[:tb]

        @pl.when(i == 0)
        def _():
            dw_ref[...] = jnp.zeros_like(dw_ref)

        dw_ref[3:4, :] += jnp.sum(dcc * x, axis=0, keepdims=True)
        dw_ref[2:3, :] += jnp.sum(dcc * e1[H:H + tb], axis=0, keepdims=True)
        dw_ref[1:2, :] += jnp.sum(dcc * e2[H:H + tb], axis=0, keepdims=True)
        dw_ref[0:1, :] += jnp.sum(dcc * e3[H:H + tb], axis=0, keepdims=True)

    cur = lambda j, i: (i, j)
    prv = lambda j, i: _halo_prev(tb, H)(i, j)
    nxt = lambda j, i: _halo_next(tb, H, nrb)(i, j)
    return _pcall(
        body, name=name, grid=(3, nrb),
        in_specs=[pl.BlockSpec((tb, GDN_W), cur), pl.BlockSpec((H, GDN_W), prv), pl.BlockSpec((H, GDN_W), nxt),
                  pl.BlockSpec((tb, GDN_W), cur), pl.BlockSpec((H, GDN_W), nxt),
                  pl.BlockSpec((SUBLANE, GDN_W), lambda j, i: (0, j))],
        out_specs=[pl.BlockSpec((tb, GDN_W), cur), pl.BlockSpec((SUBLANE, GDN_W), lambda j, i: (0, j))],
        out_shape=[jax.ShapeDtypeStruct((T, 3 * GDN_W), BF16),
                   jax.ShapeDtypeStruct((SUBLANE, 3 * GDN_W), F32)],
        compiler_params=_params(2))(proj, proj, proj, dout, dout, w8)


def _dot_family(dot, diff):
    if not diff:
        return tuple(functools.partial(lambda d, a, b: dot(a, b, d), d) for d in (BNN, BNT, BTN))

    @jax.custom_vjp
    def nn(a, b):
        return dot(a, b, BNN)
    nn.defvjp(lambda a, b: (dot(a, b, BNN), (a, b)),
              lambda res, g: (dot(g, res[1], BNT), dot(res[0], g, BTN)))

    @jax.custom_vjp
    def nt(a, b):
        return dot(a, b, BNT)
    nt.defvjp(lambda a, b: (dot(a, b, BNT), (a, b)),
              lambda res, g: (dot(g, res[1], BNN), dot(g, res[0], BTN)))

    @jax.custom_vjp
    def tn(a, b):
        return dot(a, b, BTN)
    tn.defvjp(lambda a, b: (dot(a, b, BTN), (a, b)),
              lambda res, g: (dot(res[1], g, BNT), dot(res[0], g, BNN)))
    return nn, nt, tn


def _saved_inverse(hdots):
    _, hnt, htn = hdots

    @jax.custom_vjp
    def inv(L, P):
        return P

    inv.defvjp(lambda L, P: (P, P), lambda P, g: (-hnt(htn(P, g), P), jnp.zeros_like(P)))
    return inv


def _gdn_step(dots, hdots, S, q, k, v, z, b_raw, a_raw, alog, dtb, gnorm, P_saved=None, return_P=False):
    nn, nt, tn = dots
    hnn = hdots[0]
    B, C = q.shape[0], GDN_CHUNK
    ii = lax.broadcasted_iota(jnp.int32, (B, C, C), 1)
    jj = lax.broadcasted_iota(jnp.int32, (B, C, C), 2)
    causal, strict = ii >= jj, ii > jj
    tri, tri_t = causal.astype(F32), (ii <= jj).astype(F32)
    eye, ones = (ii == jj).astype(F32), jnp.ones((B, C, C), F32)

    beta = _sigmoid(b_raw)
    xs = a_raw + dtb
    pos = xs > 0.0
    softplus = jnp.where(pos, xs, 0.0) + jnp.log(1.0 + jnp.exp(jnp.where(pos, -xs, xs)))
    g = -jnp.exp(alog) * softplus
    gb = jnp.broadcast_to(g, (B, C, C))
    gc_c = hnn(tri, gb)
    gc_r = hnn(hnn(ones, eye * gb), tri_t)
    gc = hnn(tri, jnp.broadcast_to(g, (B, C, HD)))
    gl = jnp.sum(g, axis=1, keepdims=True)
    decay = jnp.where(causal, jnp.exp(jnp.where(causal, gc_c - gc_r, 0.0)), 0.0)
    q = q * (HD ** -0.5)
    kb = k * beta
    L = jnp.where(strict, nt(kb, k) * decay, 0.0)
    egc = jnp.exp(gc)
    if P_saved is None:
        P = eye - L
        M = hnn(L, L)
        for step in range(5):
            P = P + hnn(P, M)
            if step < 4:
                M = hnn(M, M)
    else:
        P = _saved_inverse(hdots)(L, P_saved)
    u = hnn(P, v * beta)
    w = hnn(P, kb * egc)
    intra = jnp.where(causal, nt(q, k) * decay, 0.0)
    qg = q * egc
    kdec = k * jnp.exp(gl - gc)
    egl = jnp.exp(gl)
    outs = []
    for ci in range(B // GDN_H):
        sl = slice(ci * GDN_H, (ci + 1) * GDN_H)
        v_new = u[sl] - nn(w[sl], S)
        outs.append(nn(qg[sl], S) + nn(intra[sl], v_new))
        S = S * egl[sl] + tn(kdec[sl], v_new)
    o = jnp.concatenate(outs, axis=0)
    r = lax.rsqrt(jnp.mean(o * o, axis=-1, keepdims=True) + RMS_EPS)
    out = o * r * gnorm * _silu(z)
    return (out, S, P) if return_P else (out, S)


def _gdn_batches(qkv, ba, z, alog_row, dt_row):
    C = GDN_CHUNK
    q, k, v, zz, b_raw, a_raw, alog, dtb = ([] for _ in range(8))
    for ci in range(GDN_STEP):
        rows = slice(ci * C, (ci + 1) * C)
        for h in range(GDN_H):
            q.append(qkv[rows, h * HD:(h + 1) * HD])
            k.append(qkv[rows, GDN_W + h * HD:GDN_W + (h + 1) * HD])
            v.append(qkv[rows, 2 * GDN_W + h * HD:2 * GDN_W + (h + 1) * HD])
            zz.append(z[rows, h * HD:(h + 1) * HD])
            b_raw.append(ba[rows, h:h + 1])
            a_raw.append(ba[rows, GDN_H + h:GDN_H + h + 1])
            alog.append(alog_row[:, h:h + 1])
            dtb.append(dt_row[:, h:h + 1])
    return tuple(jnp.stack(t) for t in (q, k, v, zz, b_raw, a_raw, alog, dtb))


def gdn_chunk_fwd(qkv, proj, alog_row, dt_row, gnorm, name):
    T = qkv.shape[0]
    R = GDN_CHUNK * GDN_STEP
    N = T // R
    B = GDN_STEP * GDN_H
    dots, hdots = _dot_family(_bdot, False), _dot_family(_dot3, False)

    def body(qkv_ref, ba_ref, z_ref, al_ref, dt_ref, gn_ref, o_ref, save_ref, inv_ref, S_ref):
        n = pl.program_id(0)

        @pl.when(n == 0)
        def _():
            S_ref[...] = jnp.zeros_like(S_ref)

        S = S_ref[...]
        save_ref[0] = S
        batches = _gdn_batches(qkv_ref[...], ba_ref[...], z_ref[...], al_ref[...], dt_ref[...])
        o, S_new, P = _gdn_step(dots, hdots, S, *batches, gn_ref[...], return_P=True)
        S_ref[...] = S_new
        inv_ref[0] = P
        for ci in range(GDN_STEP):
            for h in range(GDN_H):
                o_ref[ci * GDN_CHUNK:(ci + 1) * GDN_CHUNK, h * HD:(h + 1) * HD] = o[ci * GDN_H + h].astype(BF16)

    return _pcall(
        body, name=name, grid=(N,),
        in_specs=[pl.BlockSpec((R, 3 * GDN_W), lambda n: (n, 0)),
                  pl.BlockSpec((R, LANE), lambda n: (n, (4 * GDN_W + POOL_G * HD) // LANE)),
                  pl.BlockSpec((R, GDN_W), lambda n: (n, 3)),
                  _row_spec(LANE), _row_spec(LANE), _row_spec(HD)],
        out_specs=[pl.BlockSpec((R, GDN_W), lambda n: (n, 0)),
                   pl.BlockSpec((1, GDN_H, HD, HD), lambda n: (n, 0, 0, 0)),
                   pl.BlockSpec((1, B, GDN_CHUNK, GDN_CHUNK), lambda n: (n, 0, 0, 0))],
        out_shape=[jax.ShapeDtypeStruct((T, GDN_W), BF16), jax.ShapeDtypeStruct((N, GDN_H, HD, HD), F32),
                   jax.ShapeDtypeStruct((N, B, GDN_CHUNK, GDN_CHUNK), F32)],
        scratch_shapes=[pltpu.VMEM((GDN_H, HD, HD), F32)],
        long_call=True)(qkv, proj, proj, alog_row, dt_row, gnorm)


def gdn_chunk_bwd(qkv, proj, alog_row, dt_row, gnorm, saved, inverses, docat, name):
    T = qkv.shape[0]
    C = GDN_CHUNK
    R = C * GDN_STEP
    N = T // R
    B = GDN_STEP * GDN_H
    dots, hdots = _dot_family(_bdot, True), _dot_family(_dot3, True)

    def body(qkv_ref, ba_ref, z_ref, al_ref, dt_ref, gn_ref, save_ref, inv_ref, do_ref,
             dqkv_ref, dz_ref, dba_ref, dal_ref, ddt_ref, dgn_ref, dS_ref):
        n = pl.program_id(0)

        @pl.when(n == 0)
        def _():
            dS_ref[...] = jnp.zeros_like(dS_ref)
            dal_ref[...] = jnp.zeros_like(dal_ref)
            ddt_ref[...] = jnp.zeros_like(ddt_ref)
            dgn_ref[...] = jnp.zeros_like(dgn_ref)

        batches = _gdn_batches(qkv_ref[...], ba_ref[...], z_ref[...], al_ref[...], dt_ref[...])
        do = do_ref[...]
        do_b = jnp.stack([do[ci * C:(ci + 1) * C, h * HD:(h + 1) * HD]
                          for ci in range(GDN_STEP) for h in range(GDN_H)])
        P = inv_ref[0]
        fn = lambda *args: _gdn_step(dots, hdots, *args, P_saved=P)
        _, vjp = jax.vjp(fn, save_ref[0], *batches, gn_ref[...])
        dS, dq, dk, dv, dz, db_raw, da_raw, dalog, ddtb, dgn = vjp((do_b, dS_ref[...]))
        dS_ref[...] = dS
        lane = lax.broadcasted_iota(jnp.int32, (1, LANE), 1)
        dal = jnp.zeros((1, LANE), F32)
        ddt = jnp.zeros((1, LANE), F32)
        for ci in range(GDN_STEP):
            rows = slice(ci * C, (ci + 1) * C)
            dba = jnp.zeros((C, LANE), F32)
            for h in range(GDN_H):
                b = ci * GDN_H + h
                dqkv_ref[rows, h * HD:(h + 1) * HD] = dq[b]
                dqkv_ref[rows, GDN_W + h * HD:GDN_W + (h + 1) * HD] = dk[b]
                dqkv_ref[rows, 2 * GDN_W + h * HD:2 * GDN_W + (h + 1) * HD] = dv[b]
                dz_ref[rows, h * HD:(h + 1) * HD] = dz[b].astype(BF16)
                hot_b = (lane == h).astype(F32)
                dba = dba + db_raw[b] * hot_b + da_raw[b] * (lane == GDN_H + h).astype(F32)
                dal = dal + dalog[b] * hot_b
                ddt = ddt + ddtb[b] * hot_b
            dba_ref[rows, :] = dba.astype(BF16)
        dal_ref[...] += dal
        ddt_ref[...] += ddt
        dgn_ref[...] += dgn

    rev = lambda n: N - 1 - n
    row = jax.ShapeDtypeStruct((1, LANE), F32)
    return _pcall(
        body, name=name, grid=(N,),
        in_specs=[pl.BlockSpec((R, 3 * GDN_W), lambda n: (rev(n), 0)),
                  pl.BlockSpec((R, LANE), lambda n: (rev(n), (4 * GDN_W + POOL_G * HD) // LANE)),
                  pl.BlockSpec((R, GDN_W), lambda n: (rev(n), 3)),
                  _row_spec(LANE), _row_spec(LANE), _row_spec(HD),
                  pl.BlockSpec((1, GDN_H, HD, HD), lambda n: (rev(n), 0, 0, 0)),
                  pl.BlockSpec((1, B, C, C), lambda n: (rev(n), 0, 0, 0)),
                  pl.BlockSpec((R, GDN_W), lambda n: (rev(n), 0))],
        out_specs=[pl.BlockSpec((R, 3 * GDN_W), lambda n: (rev(n), 0)),
                   pl.BlockSpec((R, GDN_W), lambda n: (rev(n), 0)),
                   pl.BlockSpec((R, LANE), lambda n: (rev(n), 0)),
                   _row_spec(LANE), _row_spec(LANE), _row_spec(HD)],
        out_shape=[jax.ShapeDtypeStruct((T, 3 * GDN_W), F32), jax.ShapeDtypeStruct((T, GDN_W), BF16),
                   jax.ShapeDtypeStruct((T, LANE), BF16), row, row, jax.ShapeDtypeStruct((1, HD), F32)],
        scratch_shapes=[pltpu.VMEM((GDN_H, HD, HD), F32)],
        long_call=True)(qkv, proj, proj, alog_row, dt_row, gnorm, saved, inverses, docat)


POOL_HALO = 16


def _pool_pick(j, s2, s4, s8, s16):
    return jnp.where(j == 0, s2, jnp.where(j == 1, s4, jnp.where(j == 2, s8, s16)))


def _pool_count(j, t0, rows):
    t1 = (t0 + 1 + lax.broadcasted_iota(jnp.int32, (rows, 1), 0)).astype(F32)
    win = jnp.where(j == 0, 2.0, jnp.where(j == 1, 4.0, jnp.where(j == 2, 8.0, 16.0)))
    return jnp.minimum(t1, win)


def _pooled(p, prev, i, j, tb):
    ext = jnp.concatenate([prev, p], axis=0)
    s2 = ext + _back(ext, 1)
    s4 = s2 + _back(s2, 2)
    s8 = s4 + _back(s4, 4)
    s16 = s8 + _back(s8, 8)
    s = _pool_pick(j, s2, s4, s8, s16)[POOL_HALO:]
    return s / _pool_count(j, i * tb, tb) - p


def pool_fwd(proj, pool_w, pool_scale, name):
    T = proj.shape[0]
    tb = _rows(T, 512)
    c0 = 4 * GDN_H

    def body(p_ref, pp_ref, w_ref, s_ref, o_ref):
        i, j = pl.program_id(0), pl.program_id(1)
        p = p_ref[...]
        prev = jnp.where(i > 0, pp_ref[...], 0.0)
        pooled = _pooled(p, prev, i, j, tb)
        o_ref[...] = (_bdot(pooled, w_ref[0], NN) * s_ref[...]).astype(BF16)

    return _pcall(
        body, name=name, grid=(T // tb, POOL_G),
        in_specs=[pl.BlockSpec((tb, HD), lambda i, j: (i, c0 + j)),
                  pl.BlockSpec((POOL_HALO, HD), lambda i, j: (jnp.maximum(i * (tb // POOL_HALO) - 1, 0), c0 + j)),
                  pl.BlockSpec((1, HD, HD), lambda i, j: (j, 0, 0)),
                  pl.BlockSpec((1, HD), lambda i, j: (0, j))],
        out_specs=pl.BlockSpec((tb, HD), lambda i, j: (i, j)),
        out_shape=jax.ShapeDtypeStruct((T, POOL_G * HD), BF16),
        compiler_params=_params(2))(proj, proj, pool_w, pool_scale)


def pool_bwd(proj, pool_w, pool_scale, docat, name):
    T = proj.shape[0]
    tb = _rows(T, 512)
    nrb = T // tb
    c0 = 4 * GDN_H
    HB = POOL_HALO

    def body(p_ref, pp_ref, w_ref, s_ref, d_ref, dn_ref, dp_ref, dw_ref, ds_ref):
        j, i = pl.program_id(0), pl.program_id(1)
        p = p_ref[...]
        prev = jnp.where(i > 0, pp_ref[...], 0.0)
        pooled = _pooled(p, prev, i, j, tb)
        w, scale = w_ref[0], s_ref[...]
        dy = d_ref[...]
        dnext = jnp.where(i < nrb - 1, dn_ref[...], 0.0)
        dyp = jnp.concatenate([dy, dnext], axis=0) * scale
        dpooled = _bdot(dyp, w, NT)
        qn = dpooled / _pool_count(j, i * tb, tb + HB)
        a2 = qn + _ahead(qn, 1)
        a4 = a2 + _ahead(a2, 2)
        a8 = a4 + _ahead(a4, 4)
        a16 = a8 + _ahead(a8, 8)
        dp_ref[...] = (_pool_pick(j, a2, a4, a8, a16) - dpooled)[:tb].astype(BF16)

        @pl.when(i == 0)
        def _():
            dw_ref[...] = jnp.zeros_like(dw_ref)
            ds_ref[...] = jnp.zeros_like(ds_ref)

        dw_ref[0] += _bdot(pooled, dyp[:tb], TN)
        ds_ref[...] += jnp.sum(dy * _bdot(pooled, w, NN), axis=0, keepdims=True)

    return _pcall(
        body, name=name, grid=(POOL_G, nrb),
        in_specs=[pl.BlockSpec((tb, HD), lambda j, i: (i, c0 + j)),
                  pl.BlockSpec((HB, HD), lambda j, i: (jnp.maximum(i * (tb // HB) - 1, 0), c0 + j)),
                  pl.BlockSpec((1, HD, HD), lambda j, i: (j, 0, 0)),
                  pl.BlockSpec((1, HD), lambda j, i: (0, j)),
                  pl.BlockSpec((tb, HD), lambda j, i: (i, POOL_G + j)),
                  pl.BlockSpec((HB, HD), lambda j, i: (jnp.minimum(i + 1, nrb - 1) * (tb // HB), POOL_G + j))],
        out_specs=[pl.BlockSpec((tb, HD), lambda j, i: (i, j)),
                   pl.BlockSpec((1, HD, HD), lambda j, i: (j, 0, 0)),
                   pl.BlockSpec((1, HD), lambda j, i: (0, j))],
        out_shape=[jax.ShapeDtypeStruct((T, POOL_G * HD), BF16),
                   jax.ShapeDtypeStruct((POOL_G, HD, HD), F32),
                   jax.ShapeDtypeStruct((1, POOL_G * HD), F32)],
        compiler_params=_params(2))(proj, proj, pool_w, pool_scale, docat, docat)


ATT_W = ATT_H * HD
GROUP_COLS = 3 * ATT_W


def to_residue_major(t, d):
    if d == 1:
        return t
    T, C = t.shape
    return t.reshape(T // d, d, C).transpose(1, 0, 2).reshape(T, C)


def to_token_order(t, d):
    if d == 1:
        return t
    T, C = t.shape
    return t.reshape(d, T // d, C).transpose(1, 0, 2).reshape(T, C)


def headnorm_fwd(proj, qk_gain, name):
    T = proj.shape[0]
    tb = _rows(T, 256)

    def body(x_ref, g_ref, o_ref):
        g = g_ref[...]
        for h in range(2 * ATT_H):
            sl = slice(h * HD, (h + 1) * HD)
            x = x_ref[:, sl].astype(F32)
            n = x * lax.rsqrt(jnp.mean(x * x, axis=-1, keepdims=True) + RMS_EPS)
            gain = g[0:1] * (HD ** -0.5) if h < ATT_H else g[1:2]
            o_ref[:, sl] = (n * gain).astype(BF16)
        o_ref[:, 2 * ATT_W:] = x_ref[:, 2 * ATT_W:]

    blk = pl.BlockSpec((tb, GROUP_COLS), lambda i: (i, 0))
    return _pcall(
        body, name=name, grid=(T // tb,),
        in_specs=[blk, pl.BlockSpec((SUBLANE, HD), lambda i: (0, 0))],
        out_specs=blk, out_shape=jax.ShapeDtypeStruct((T, GROUP_COLS), BF16),
        long_call=True)(proj, qk_gain)


def headnorm_bwd(proj, qk_gain, dq, dk, dv, name):
    T = proj.shape[0]
    tb = _rows(T, 256)

    def body(x_ref, g_ref, dq_ref, dk_ref, dv_ref, dx_ref, dg_ref):
        i = pl.program_id(0)
        g = g_ref[...]

        @pl.when(i == 0)
        def _():
            dg_ref[...] = jnp.zeros_like(dg_ref)

        for part, d_ref in enumerate((dq_ref, dk_ref)):
            gain = g[0:1] * (HD ** -0.5) if part == 0 else g[1:2]
            scale = (HD ** -0.5) if part == 0 else 1.0
            acc = jnp.zeros((1, HD), F32)
            for h in range(ATT_H):
                x = x_ref[:, part * ATT_W + h * HD:part * ATT_W + (h + 1) * HD].astype(F32)
                d = d_ref[:, h * HD:(h + 1) * HD].astype(F32)
                r = lax.rsqrt(jnp.mean(x * x, axis=-1, keepdims=True) + RMS_EPS)
                n = x * r
                dn = d * gain
                dx = r * (dn - n * jnp.mean(dn * n, axis=-1, keepdims=True))
                dx_ref[:, part * ATT_W + h * HD:part * ATT_W + (h + 1) * HD] = dx.astype(BF16)
                acc = acc + jnp.sum(d * n, axis=0, keepdims=True)
            dg_ref[part:part + 1, :] += acc * scale
        dx_ref[:, 2 * ATT_W:] = dv_ref[...]

    blk = pl.BlockSpec((tb, GROUP_COLS), lambda i: (i, 0))
    dblk = pl.BlockSpec((tb, ATT_W), lambda i: (i, 0))
    gspec = pl.BlockSpec((SUBLANE, HD), lambda i: (0, 0))
    return _pcall(
        body, name=name, grid=(T // tb,),
        in_specs=[blk, gspec, dblk, dblk, dblk],
        out_specs=[blk, gspec],
        out_shape=[jax.ShapeDtypeStruct((T, GROUP_COLS), BF16), jax.ShapeDtypeStruct((SUBLANE, HD), F32)],
        long_call=True)(proj, qk_gain, dq, dk, dv)


def _heads(ref):
    return jnp.stack([ref[:, h * HD:(h + 1) * HD] for h in range(ATT_H)])


def _slopes(dil):
    h = lax.broadcasted_iota(jnp.int32, (ATT_H, 1, 1), 0)
    return lax.bitcast_convert_type((126 - h) << 23, F32) * float(dil)


def _att_scores_b(q, k, slope, n_ok, far, keys_first=False):
    r = lax.broadcasted_iota(jnp.int32, (1, ATT_BLK, ATT_BLK), 1)
    c = lax.broadcasted_iota(jnp.int32, (1, ATT_BLK, ATT_BLK), 2)
    a, j = (c, r) if keys_first else (r, c)
    rel = (ATT_BLK + a - j) if far else (a - j)
    mask = ((j >= a) & n_ok) if far else (j <= a)
    s = (_bdot(k, q, BNT) if keys_first else _bdot(q, k, BNT)) - slope * rel.astype(F32)
    return jnp.where(mask, s, NEG), mask


def _att_blocks(nb, width, shift):
    def make(col):
        return pl.BlockSpec((ATT_BLK, width),
                            lambda r, n: (r * nb + jnp.clip(n + shift, 0, nb - 1), col))
    return make


def _lane_col(cols):
    lane = lax.broadcasted_iota(jnp.int32, (1, LANE), 1)
    out = jnp.zeros((ATT_BLK, LANE), F32)
    for h, c in enumerate(cols):
        out = out + c * (lane == h).astype(F32)
    return out


def att_fwd(qkvn, gi, name):
    T = qkvn.shape[0]
    dil = DIL[gi]
    nb = T // dil // ATT_BLK

    def body(q_ref, kp_ref, kc_ref, vp_ref, vc_ref, o_ref, l_ref):
        n_ok = pl.program_id(1) > 0
        slope = _slopes(dil)
        q = _heads(q_ref)
        s_c, _ = _att_scores_b(q, _heads(kc_ref), slope, n_ok, False)
        s_p, _ = _att_scores_b(q, _heads(kp_ref), slope, n_ok, True)
        m = jnp.maximum(jnp.max(s_c, axis=-1, keepdims=True), jnp.max(s_p, axis=-1, keepdims=True))
        p_c, p_p = jnp.exp(s_c - m), jnp.exp(s_p - m)
        l = jnp.sum(p_c, axis=-1, keepdims=True) + jnp.sum(p_p, axis=-1, keepdims=True)
        o = (_bdot(p_c, _heads(vc_ref), BNN) + _bdot(p_p, _heads(vp_ref), BNN)) / l
        lse = m + jnp.log(l)
        for h in range(ATT_H):
            o_ref[:, h * HD:(h + 1) * HD] = o[h].astype(BF16)
        l_ref[...] = _lane_col([lse[h] for h in range(ATT_H)])

    cur, prv = _att_blocks(nb, ATT_W, 0), _att_blocks(nb, ATT_W, -1)
    return _pcall(
        body, name=name, grid=(dil, nb), in_specs=[cur(0), prv(1), cur(1), prv(2), cur(2)],
        out_specs=[cur(0), _att_blocks(nb, LANE, 0)(0)],
        out_shape=[jax.ShapeDtypeStruct((T, ATT_W), BF16), jax.ShapeDtypeStruct((T, LANE), F32)],
        long_call=True)(qkvn, qkvn, qkvn, qkvn, qkvn)


def att_merge(os, lses, name):
    T = os[0].shape[0]
    tb = _rows(T, 512)

    def body(o0, o1, o2, l0, l1, l2, o_ref, l_ref):
        a, b, c = l0[...], l1[...], l2[...]
        m = jnp.maximum(a, jnp.maximum(b, c))
        wa, wb, wc = jnp.exp(a - m), jnp.exp(b - m), jnp.exp(c - m)
        den = wa + wb + wc
        l_ref[...] = m + jnp.log(den)
        wa, wb, wc = wa / den, wb / den, wc / den
        for h in range(ATT_H):
            sl = slice(h * HD, (h + 1) * HD)
            o_ref[:, sl] = (wa[:, h:h + 1] * o0[:, sl].astype(F32) + wb[:, h:h + 1] * o1[:, sl].astype(F32)
                            + wc[:, h:h + 1] * o2[:, sl].astype(F32))

    blk = pl.BlockSpec((tb, ATT_W), lambda i: (i, 0))
    lblk = pl.BlockSpec((tb, LANE), lambda i: (i, 0))
    return _pcall(
        body, name=name, grid=(T // tb,), in_specs=[blk] * 3 + [lblk] * 3, out_specs=[blk, lblk],
        out_shape=[jax.ShapeDtypeStruct((T, ATT_W), F32), jax.ShapeDtypeStruct((T, LANE), F32)],
        compiler_params=_params(1))(*os, *lses)


def att_delta(do, o, name):
    T = do.shape[0]
    tb = _rows(T, 512)

    def body(d_ref, o_ref, out_ref):
        lane = lax.broadcasted_iota(jnp.int32, (1, LANE), 1)
        out = jnp.zeros((tb, LANE), F32)
        for h in range(ATT_H):
            sl = slice(h * HD, (h + 1) * HD)
            s = jnp.sum(d_ref[:, sl].astype(F32) * o_ref[:, sl], axis=-1, keepdims=True)
            out = out + s * (lane == h).astype(F32)
        out_ref[...] = out

    blk = pl.BlockSpec((tb, ATT_W), lambda i: (i, 0))
    return _pcall(
        body, name=name, grid=(T // tb,), in_specs=[blk, blk],
        out_specs=pl.BlockSpec((tb, LANE), lambda i: (i, 0)),
        out_shape=jax.ShapeDtypeStruct((T, LANE), F32), compiler_params=_params(1))(do, o)


def att_bwd_q(qkvn, do, lse, delta, gi, name):
    T = qkvn.shape[0]
    dil = DIL[gi]
    nb = T // dil // ATT_BLK

    def body(q_ref, kp_ref, kc_ref, vp_ref, vc_ref, do_ref, l_ref, d_ref, dq_ref):
        n_ok = pl.program_id(1) > 0
        slope = _slopes(dil)
        lse, dl = l_ref[...], d_ref[...]
        lse = jnp.stack([lse[:, h:h + 1] for h in range(ATT_H)])
        dl = jnp.stack([dl[:, h:h + 1] for h in range(ATT_H)])
        q, do = _heads(q_ref), _heads(do_ref)
        dq = jnp.zeros((ATT_H, ATT_BLK, HD), F32)
        for k_ref, v_ref, far in ((kc_ref, vc_ref, False), (kp_ref, vp_ref, True)):
            k = _heads(k_ref)
            s, mask = _att_scores_b(q, k, slope, n_ok, far)
            p = jnp.where(mask, jnp.exp(s - lse), 0.0)
            ds = p * (_bdot(do, _heads(v_ref), BNT) - dl)
            dq = dq + _bdot(ds, k, BNN)
        for h in range(ATT_H):
            dq_ref[:, h * HD:(h + 1) * HD] = dq[h].astype(BF16)

    cur, prv = _att_blocks(nb, ATT_W, 0), _att_blocks(nb, ATT_W, -1)
    small = _att_blocks(nb, LANE, 0)(0)
    return _pcall(
        body, name=name, grid=(dil, nb),
        in_specs=[cur(0), prv(1), cur(1), prv(2), cur(2), cur(0), small, small],
        out_specs=cur(0), out_shape=jax.ShapeDtypeStruct((T, ATT_W), BF16),
        long_call=True)(qkvn, qkvn, qkvn, qkvn, qkvn, do, lse, delta)


def att_bwd_kv(qkvn, do, lse, delta, gi, name):
    T = qkvn.shape[0]
    dil = DIL[gi]
    nb = T // dil // ATT_BLK

    def body(k_ref, v_ref, q0_ref, q1_ref, do0_ref, do1_ref, l0_ref, l1_ref, d0_ref, d1_ref,
             dk_ref, dv_ref):
        n_ok = pl.program_id(1) < nb - 1
        slope = _slopes(dil)
        by_row = lambda ref: jnp.stack([ref[...].T[h:h + 1, :] for h in range(ATT_H)])
        k, v = _heads(k_ref), _heads(v_ref)
        dk = jnp.zeros((ATT_H, ATT_BLK, HD), F32)
        dv = jnp.zeros((ATT_H, ATT_BLK, HD), F32)
        for q_ref, do_ref, l_ref, d_ref, far in ((q0_ref, do0_ref, l0_ref, d0_ref, False),
                                                 (q1_ref, do1_ref, l1_ref, d1_ref, True)):
            q, do = _heads(q_ref), _heads(do_ref)
            s, mask = _att_scores_b(q, k, slope, n_ok, far, keys_first=True)
            p = jnp.where(mask, jnp.exp(s - by_row(l_ref)), 0.0)
            dv = dv + _bdot(p, do, BNN)
            ds = p * (_bdot(v, do, BNT) - by_row(d_ref))
            dk = dk + _bdot(ds, q, BNN)
        for h in range(ATT_H):
            dk_ref[:, h * HD:(h + 1) * HD] = dk[h].astype(BF16)
            dv_ref[:, h * HD:(h + 1) * HD] = dv[h].astype(BF16)

    cur, nxt = _att_blocks(nb, ATT_W, 0), _att_blocks(nb, ATT_W, 1)
    s0, s1 = _att_blocks(nb, LANE, 0)(0), _att_blocks(nb, LANE, 1)(0)
    return _pcall(
        body, name=name, grid=(dil, nb),
        in_specs=[cur(1), cur(2), cur(0), nxt(0), cur(0), nxt(0), s0, s1, s0, s1],
        out_specs=[cur(0), cur(0)], out_shape=[jax.ShapeDtypeStruct((T, ATT_W), BF16)] * 2,
        long_call=True)(qkvn, qkvn, qkvn, qkvn, do, do, lse, lse, delta, delta)


def adamw(w, g, m, v, name):
    shape = w.shape
    C = shape[-1]
    R = math.prod(shape[:-1])
    to2d = lambda t: t.reshape(R, C)
    tb = _rows(R, max(16, (256 * 1536 // C) // 16 * 16))
    c1 = 1.0 - ADAM_B1 ** ADAM_STEP
    c2 = 1.0 - ADAM_B2 ** ADAM_STEP

    def body(w_ref, g_ref, m_ref, v_ref, d_ref, nm_ref, nv_ref):
        gv = g_ref[...]
        nm = ADAM_B1 * m_ref[...] + (1.0 - ADAM_B1) * gv
        nv = ADAM_B2 * v_ref[...] + (1.0 - ADAM_B2) * (gv * gv)
        d_ref[...] = -ADAM_LR * ((nm / c1) / (jnp.sqrt(nv / c2) + ADAM_EPS) + ADAM_WD * w_ref[...])
        nm_ref[...] = nm
        nv_ref[...] = nv

    blk = pl.BlockSpec((tb, C), lambda i: (i, 0))
    out = jax.ShapeDtypeStruct((R, C), F32)
    d, nm, nv = _pcall(
        body, name=name, grid=(R // tb,), in_specs=[blk] * 4, out_specs=[blk] * 3,
        out_shape=[out, out, out], long_call=R * C >= (1 << 21))(to2d(w), to2d(g), to2d(m), to2d(v))
    return d.reshape(shape), nm.reshape(shape), nv.reshape(shape)


def _pad_rows8(w):
    return jnp.pad(w, ((0, SUBLANE - w.shape[0]), (0, 0)))


def _lane_row(v):
    return jnp.pad(v, (0, LANE - v.shape[0]))[None, :]


def _even_reorder(w_in):
    z4 = 4 * GDN_W
    pad = jnp.zeros((w_in.shape[0], EVEN_PAD - EVEN_COLS), w_in.dtype)
    return jnp.concatenate([w_in[:, :z4], w_in[:, z4 + 2 * GDN_H:], w_in[:, z4:z4 + 2 * GDN_H], pad], axis=1)


def _even_restore(dw):
    z4 = 4 * GDN_W
    p4 = POOL_G * HD
    return jnp.concatenate([dw[:, :z4], dw[:, z4 + p4:z4 + p4 + 2 * GDN_H], dw[:, z4:z4 + p4]], axis=1)


def _ffn_fwd(tag, hf, wl):
    up = matmul(hf, wl["ffn_w_up"], "nn", BF16, f"{tag}_ffn_up")
    act = ffn_mid_fwd(up, wl["ffn_conv_w8"], wl["ffn_conv_b"], f"{tag}_ffn_mid")
    f = matmul(act, wl["ffn_w_down"], "nn", F32, f"{tag}_ffn_down")
    return f, (hf, up, act)


def _ffn_bwd(tag, df, saved, wl):
    hf, up, act = saved
    dact = matmul(df, wl["ffn_w_down"], "nt", BF16, f"{tag}_ffn_down_da")
    wl["on_grad"]("ffn_w_down", matmul(act, df, "tn", F32, f"{tag}_ffn_down_dw", pieces="row"))
    dup, dcw, dcb = ffn_mid_bwd(up, wl["ffn_conv_w8"], wl["ffn_conv_b"], dact, f"{tag}_ffn_mid_bwd")
    dhf = matmul(dup, wl["ffn_w_up"], "nt", F32, f"{tag}_ffn_up_da")
    wl["on_grad"]("ffn_w_up", matmul(hf, dup, "tn", F32, f"{tag}_ffn_up_dw", pieces="col"))
    return dhf, {"ffn_conv_w": dcw[:FFN_CONV], "ffn_conv_b": dcb[0]}


def _even_fwd(tag, hm, wl):
    proj = matmul(hm, wl["w_in"], "nn", F32, f"{tag}_ev_in")
    qkv = gdn_conv_fwd(proj, wl["gdn_conv_w8"], f"{tag}_gdn_conv")
    o_a, *states = gdn_chunk_fwd(qkv, proj, wl["alog_row"], wl["dt_row"], wl["gdn_norm"], f"{tag}_gdn_chunk")
    o_b = pool_fwd(proj, wl["pool_w"], wl["pool_scale"], f"{tag}_pool")
    ocat = jnp.concatenate([o_a, o_b], axis=1)
    y = matmul(ocat, wl["w_out"], "nn", F32, f"{tag}_ev_out")
    return y, (hm, proj, qkv, states, ocat)


def _even_bwd(tag, dy, saved, wl):
    hm, proj, qkv, states, ocat = saved
    docat = matmul(dy, wl["w_out"], "nt", F32, f"{tag}_ev_out_da")
    wl["on_grad"]("ev_w_out", matmul(ocat, dy, "tn", F32, f"{tag}_ev_out_dw", pieces="row"))
    dqkv, dz, dba, dalog, ddt, dgn = gdn_chunk_bwd(
        qkv, proj, wl["alog_row"], wl["dt_row"], wl["gdn_norm"], *states, docat, f"{tag}_gdn_chunk_bwd")
    dxc, dconv = gdn_conv_bwd(proj, wl["gdn_conv_w8"], dqkv, f"{tag}_gdn_conv_bwd")
    dp, dpw, dps = pool_bwd(proj, wl["pool_w"], wl["pool_scale"], docat, f"{tag}_pool_bwd")
    dproj = jnp.concatenate([dxc, dz, dp, dba], axis=1)
    dhm = matmul(dproj, wl["w_in"], "nt", F32, f"{tag}_ev_in_da")
    wl["on_grad"]("ev_w_in", col_pieces(_even_restore(matmul(hm, dproj, "tn", F32, f"{tag}_ev_in_dw"))))
    return dhm, {"gdn_conv_w": dconv[:GDN_CONV], "gdn_a_log": dalog[0, :GDN_H], "gdn_dt_bias": ddt[0, :GDN_H],
                 "gdn_norm": dgn[0], "pool_w": dpw, "pool_scale": dps[0]}


def _odd_fwd(tag, hm, wl):
    projs, qkvns, outs, lses = [], [], [], []
    for gi, d in enumerate(DIL):
        w_g = wl["w_in"][:, gi * GROUP_COLS:(gi + 1) * GROUP_COLS]
        proj = matmul(to_residue_major(hm, d), w_g, "nn", BF16, f"{tag}_od_in{gi}")
        qkvn = headnorm_fwd(proj, wl["qk_gain8"], f"{tag}_headnorm{gi}")
        o_g, l_g = att_fwd(qkvn, gi, f"{tag}_att{gi}")
        projs.append(proj)
        qkvns.append(qkvn)
        outs.append(to_token_order(o_g, d))
        lses.append(to_token_order(l_g, d))
    o, lse = att_merge(outs, lses, f"{tag}_att_merge")
    y = matmul(o, wl["w_out"], "nn", F32, f"{tag}_od_out")
    return y, (hm, projs, qkvns, o, lse)


def _odd_bwd(tag, dy, saved, wl):
    hm, projs, qkvns, o, lse = saved
    do = matmul(dy, wl["w_out"], "nt", BF16, f"{tag}_od_out_da")
    wl["on_grad"]("od_w_out", matmul(o, dy, "tn", F32, f"{tag}_od_out_dw", pieces="row"))
    delta = att_delta(do, o, f"{tag}_att_delta")
    dhm, dw_in, dgain_qk = None, [], None
    for gi, d in enumerate(DIL):
        w_g = wl["w_in"][:, gi * GROUP_COLS:(gi + 1) * GROUP_COLS]
        do_g, lse_g, dl_g = (to_residue_major(t, d) for t in (do, lse, delta))
        dq = att_bwd_q(qkvns[gi], do_g, lse_g, dl_g, gi, f"{tag}_att{gi}_dq")
        dk, dv = att_bwd_kv(qkvns[gi], do_g, lse_g, dl_g, gi, f"{tag}_att{gi}_dkv")
        dproj, dgain = headnorm_bwd(projs[gi], wl["qk_gain8"], dq, dk, dv, f"{tag}_headnorm{gi}_bwd")
        dhm_g = to_token_order(matmul(dproj, w_g, "nt", F32, f"{tag}_od_in{gi}_da"), d)
        dw_in.append(matmul(to_residue_major(hm, d), dproj, "tn", F32, f"{tag}_od_in{gi}_dw"))
        dhm = dhm_g if dhm is None else dhm + dhm_g
        dgain_qk = dgain if dgain_qk is None else dgain_qk + dgain
    wl["on_grad"]("od_w_in", col_pieces(jnp.concatenate(dw_in, axis=1)))
    return dhm, {"att_q_norm": dgain_qk[0], "att_k_norm": dgain_qk[1]}


def col_pieces(dw):
    M, N = dw.shape
    return dw.reshape(2, M // 2, N_CHIPS, N // N_CHIPS).transpose(0, 2, 1, 3)


class _Lazy:
    def __init__(self, fn):
        self.fn, self.value = fn, None


class _LayerWeights(dict):
    def __getitem__(self, key):
        v = dict.__getitem__(self, key)
        if isinstance(v, _Lazy):
            if v.value is None:
                v.value = v.fn()
            return v.value
        return v


def _layer_weights(i, W, big, on_grad):
    e = i // 2
    wl = _LayerWeights({
        "norm_mix": W["norm_mix"][i][None, :], "norm_ffn": W["norm_ffn"][i][None, :],
        "ffn_w_up": _Lazy(lambda: big("ffn_w_up", i)), "ffn_w_down": _Lazy(lambda: big("ffn_w_down", i)),
        "ffn_conv_w8": _pad_rows8(W["ffn_conv_w"][i]), "ffn_conv_b": W["ffn_conv_b"][i][None, :],
        "on_grad": lambda name, pieces: on_grad(name, i if name.startswith("ffn") else e, pieces)})
    if i % 2 == 0:
        wl.update({"w_in": _Lazy(lambda: _even_reorder(big("ev_w_in", e))),
                   "w_out": _Lazy(lambda: big("ev_w_out", e)),
                   "gdn_conv_w8": _pad_rows8(W["gdn_conv_w"][e]),
                   "alog_row": _lane_row(W["gdn_a_log"][e]), "dt_row": _lane_row(W["gdn_dt_bias"][e]),
                   "gdn_norm": W["gdn_norm"][e][None, :], "pool_w": W["pool_w"][e],
                   "pool_scale": W["pool_scale"][e][None, :]})
    else:
        wl.update({"w_in": _Lazy(lambda: big("od_w_in", e)), "w_out": _Lazy(lambda: big("od_w_out", e)),
                   "qk_gain8": _pad_rows8(jnp.stack([W["att_q_norm"][e], W["att_k_norm"][e]]))})
    return wl


def local_step(x, target, mod, W, big, on_grad):
    depth = mod.shape[0]
    row = lambda i, k: mod[i, k][None, :]
    saved, wls = [], []
    pending = None
    for i in range(depth):
        tag = f"l{i}"
        wl = _layer_weights(i, W, big, on_grad)
        if pending is None:
            hm = modnorm_fwd(x, wl["norm_mix"], row(i, 1), row(i, 0), f"{tag}_mix_norm")
        else:
            x, hm = res_norm_fwd(x, *pending, wl["norm_mix"], row(i, 1), row(i, 0), f"{tag}_mix_norm")
        y, s_mix = (_even_fwd if i % 2 == 0 else _odd_fwd)(tag, hm, wl)
        x1, hf = res_norm_fwd(x, row(i, 2), y, wl["norm_ffn"], row(i, 4), row(i, 3), f"{tag}_ffn_norm")
        f, s_ffn = _ffn_fwd(tag, hf, wl)
        saved.append((x, y, s_mix, x1, f, s_ffn))
        wls.append(wl)
        x, pending = x1, (row(i, 5), f)
    sq, dx = loss_head(gres_fwd(x, *pending, "last_res"), target, "loss_head")

    dmod, grads = [None] * depth, [None] * depth
    df, dg_f = gres_bwd(dx, *pending, "last_res_bwd")
    for i in reversed(range(depth)):
        tag = f"l{i}"
        x, y, s_mix, x1, f, s_ffn = saved[i]
        dhf, g_ffn = _ffn_bwd(tag, df, s_ffn, wls[i])
        dx, dgain_f, dsc_f, dsh_f, dy, dg_m = norm_res_bwd(
            x1, wls[i]["norm_ffn"], row(i, 4), dhf, dx, row(i, 2), y, f"{tag}_ffn_norm_bwd")
        dhm, g_mix = (_even_bwd if i % 2 == 0 else _odd_bwd)(tag, dy, s_mix, wls[i])
        dmod_f = [dsh_f, dsc_f, dg_f]
        if i > 0:
            dx, dgain_m, dsc_m, dsh_m, df, dg_f = norm_res_bwd(
                x, wls[i]["norm_mix"], row(i, 1), dhm, dx, row(i - 1, 5), saved[i - 1][4], f"{tag}_mix_norm_bwd")
        else:
            dx, dgain_m, dsc_m, dsh_m = modnorm_bwd(x, wls[i]["norm_mix"], row(i, 1), dhm, dx, f"{tag}_mix_norm_bwd")
        dmod[i] = jnp.concatenate([dsh_m, dsc_m, dg_m] + dmod_f, axis=0)
        grads[i] = {"norm_mix": dgain_m[0], "norm_ffn": dgain_f[0], **g_mix, **g_ffn}
    return sq, dx, jnp.stack(dmod), grads


def _place():
    return lax.axis_index("x"), lax.axis_index("y"), lax.axis_index("c")


def _other_chips(mx, my):
    return [(1 - mx, my), (mx, 1 - my), (1 - mx, 1 - my)]


def _sems(n):
    return [DMA_SEM((n,)), DMA_SEM((n,))]


def _put(buf, block, index):
    return lax.dynamic_update_index_in_dim(buf, block, index, 0)


def gather8_ride(x, then):
    def parts(ins, outs, sems):
        (x_ref,), (out_ref,), (send_sems, recv_sems) = ins, outs, sems
        mx, my, mc = _place()
        me, sibling = (mx, my, mc), (mx, my, 1 - mc)
        chips = _other_chips(mx, my)

        def slot(px, py, pc):
            return out_ref.at[4 * px + 2 * py + pc]

        def copy(k, block, to, src=None):
            return pltpu.make_async_remote_copy(
                src_ref=slot(*block) if src is None else src, dst_ref=slot(*block),
                send_sem=send_sems.at[k], recv_sem=recv_sems.at[k], device_id=to, device_id_type=MESH)

        first = lambda: ([copy(0, me, sibling, src=x_ref)]
                         + [copy(1 + j, me, (*chip, mc), src=x_ref) for j, chip in enumerate(chips)])
        passed = lambda: [copy(4 + j, (*chip, mc), sibling) for j, chip in enumerate(chips)]
        landed = lambda: [copy(1 + j, (*chip, mc), me) for j, chip in enumerate(chips)]
        from_sibling = lambda: ([copy(0, sibling, me)]
                                + [copy(4 + j, (*chip, 1 - mc), me) for j, chip in enumerate(chips)])
        return first, passed, landed, from_sibling

    def start(ins, outs, sems):
        first, _, _, _ = parts(ins, outs, sems)
        for cp in first():
            cp.start()

    def mid(ins, outs, sems):
        _, passed, landed, _ = parts(ins, outs, sems)
        for cp, fwd in zip(landed(), passed()):
            cp.wait_recv()
            fwd.start()

    def finish(ins, outs, sems):
        first, passed, _, from_sibling = parts(ins, outs, sems)
        for cp in from_sibling():
            cp.wait_recv()
        for cp in first() + passed():
            cp.wait_send()

    def landed_all(outs):
        mx, my, mc = _place()
        then(_put(outs[0], x, 4 * mx + 2 * my + mc))

    return Ride([x], [jax.ShapeDtypeStruct((8,) + x.shape, x.dtype)], _sems(7), start, finish,
                landed_all, mid=mid, heavy=True)


def all_gather8(x):
    box = []
    waiting = list(_RIDES)
    _RIDES[:] = [gather8_ride(x, box.append)]
    flush_rides()
    _RIDES[:] = waiting + _RIDES
    return box[0]


def sibling_halves_ride(p, then):
    def copy(ins, outs, sems):
        (p_ref,), (got_ref,), (send_sems, recv_sems) = ins, outs, sems
        mx, my, mc = _place()
        return pltpu.make_async_remote_copy(src_ref=p_ref.at[1 - mc], dst_ref=got_ref, send_sem=send_sems.at[0],
                                            recv_sem=recv_sems.at[0], device_id=(mx, my, 1 - mc), device_id_type=MESH)

    def start(ins, outs, sems):
        copy(ins, outs, sems).start()

    def finish(ins, outs, sems):
        cp = copy(ins, outs, sems)
        cp.wait_send()
        cp.wait_recv()

    def landed(outs):
        then(lax.dynamic_index_in_dim(p, _place()[2], 0, keepdims=False), outs[0])

    return Ride([p], [jax.ShapeDtypeStruct(p.shape[1:], p.dtype)], _sems(1), start, finish, landed)


def sibling_pair_ride(r, then):
    def copy(ins, outs, sems):
        (r_ref,), (got_ref,), (send_sems, recv_sems) = ins, outs, sems
        mx, my, mc = _place()
        return pltpu.make_async_remote_copy(src_ref=r_ref, dst_ref=got_ref, send_sem=send_sems.at[0],
                                            recv_sem=recv_sems.at[0], device_id=(mx, my, 1 - mc), device_id_type=MESH)

    def start(ins, outs, sems):
        copy(ins, outs, sems).start()

    def finish(ins, outs, sems):
        cp = copy(ins, outs, sems)
        cp.wait_send()
        cp.wait_recv()

    def landed(outs):
        then(jnp.where(_place()[2] == 0, jnp.stack([r, outs[0]]), jnp.stack([outs[0], r])))

    return Ride([r], [jax.ShapeDtypeStruct(r.shape, r.dtype)], _sems(1), start, finish, landed)


def chip_scatter_ride(p, then):
    def parts(ins, outs, sems):
        (p_ref,), (out_ref,), (send_sems, recv_sems) = ins, outs, sems
        mx, my, mc = _place()
        mine = 2 * mx + my
        chips = _other_chips(mx, my)
        sends = [pltpu.make_async_remote_copy(
            src_ref=p_ref.at[2 * chip[0] + chip[1]], dst_ref=out_ref.at[mine], send_sem=send_sems.at[k],
            recv_sem=recv_sems.at[k], device_id=(*chip, mc), device_id_type=MESH) for k, chip in enumerate(chips)]
        recvs = lambda: [pltpu.make_async_remote_copy(
            src_ref=p_ref.at[mine], dst_ref=out_ref.at[2 * chip[0] + chip[1]], send_sem=send_sems.at[k],
            recv_sem=recv_sems.at[k], device_id=(*chip, mc), device_id_type=MESH) for k, chip in enumerate(chips)]
        return sends, recvs

    def start(ins, outs, sems):
        sends, _ = parts(ins, outs, sems)
        for cp in sends:
            cp.start()

    def finish(ins, outs, sems):
        sends, recvs = parts(ins, outs, sems)
        for cp in recvs():
            cp.wait_recv()
        for cp in sends:
            cp.wait_send()

    def landed(outs):
        mx, my, _ = _place()
        mine = 2 * mx + my
        then(_put(outs[0], lax.dynamic_index_in_dim(p, mine, 0, keepdims=False), mine))

    return Ride([p], [jax.ShapeDtypeStruct(p.shape, p.dtype)], _sems(3), start, finish, landed, heavy=True)


def _stream_rows(R, C):
    return _rows(R, max(16, (256 * 1536 // C) // 16 * 16))


def cast_bf16(w, name):
    R, C = w.shape
    tb = _stream_rows(R, C)

    def body(w_ref, o_ref):
        o_ref[...] = w_ref[...].astype(BF16)

    blk = pl.BlockSpec((tb, C), lambda i: (i, 0))
    return _pcall(body, name=name, grid=(R // tb,), in_specs=[blk], out_specs=blk,
                          out_shape=jax.ShapeDtypeStruct((R, C), BF16), compiler_params=_params(1))(w)


def sum_slots(g, name):
    n, R, C = g.shape
    tb = _stream_rows(R, C)

    def body(*refs):
        acc = refs[0][...].astype(F32)
        for r in refs[1:n]:
            acc = acc + r[...].astype(F32)
        refs[n][...] = acc

    specs = [pl.BlockSpec((None, tb, C), functools.partial(lambda k, i: (k, i, 0), k)) for k in range(n)]
    return _pcall(body, name=name, grid=(R // tb,), in_specs=specs,
                          out_specs=pl.BlockSpec((tb, C), lambda i: (i, 0)),
                          out_shape=jax.ShapeDtypeStruct((R, C), F32), compiler_params=_params(1))(*([g] * n))


def add_to_bf16(a, b, name):
    R, C = a.shape
    tb = _stream_rows(R, C)

    def body(a_ref, b_ref, o_ref):
        o_ref[...] = (a_ref[...] + b_ref[...]).astype(BF16)

    blk = pl.BlockSpec((tb, C), lambda i: (i, 0))
    return _pcall(body, name=name, grid=(R // tb,), in_specs=[blk, blk], out_specs=blk,
                          out_shape=jax.ShapeDtypeStruct((R, C), BF16), compiler_params=_params(1))(a, b)


def ada_fwd(c_all, ada_w, bias, name):
    n, D, Cs = ada_w.shape
    tn = _tile(Cs, 512)

    def body(c_ref, w_ref, b_ref, o_ref):
        o_ref[...] = _bdot(_silu(c_ref[...]), w_ref[...], NN) + b_ref[...]

    return _pcall(
        body, name=name, grid=(n, Cs // tn),
        in_specs=[pl.BlockSpec((8, D), lambda l, j: (0, 0)),
                  pl.BlockSpec((None, D, tn), lambda l, j: (l, 0, j)),
                  pl.BlockSpec((None, 1, tn), lambda l, j: (l, 0, j))],
        out_specs=pl.BlockSpec((None, 8, tn), lambda l, j: (l, 0, j)),
        out_shape=jax.ShapeDtypeStruct((n, 8, Cs), F32), compiler_params=_params(2))(c_all, ada_w, bias)


def ada_bwd(c16, dmod16, name):
    n, _, Cs = dmod16.shape
    D = c16.shape[1]
    tn = _tile(Cs, 512)

    def body(c_ref, d_ref, o_ref):
        o_ref[...] = _bdot(_silu(c_ref[...]), d_ref[...], TN)

    return _pcall(
        body, name=name, grid=(n, Cs // tn),
        in_specs=[pl.BlockSpec((16, D), lambda l, j: (0, 0)),
                  pl.BlockSpec((None, 16, tn), lambda l, j: (l, 0, j))],
        out_specs=pl.BlockSpec((None, D, tn), lambda l, j: (l, 0, j)),
        out_shape=jax.ShapeDtypeStruct((n, D, Cs), F32), compiler_params=_params(2))(c16, dmod16)


WEIGHTS = ["ada_w", "ada_b", "norm_mix", "norm_ffn", "ev_w_in", "ev_w_out", "gdn_conv_w", "gdn_a_log",
           "gdn_dt_bias", "gdn_norm", "pool_w", "pool_scale", "od_w_in", "od_w_out", "att_q_norm",
           "att_k_norm", "ffn_w_up", "ffn_conv_w", "ffn_conv_b", "ffn_w_down"]
COL_SHARDED = ("ev_w_in", "od_w_in", "ffn_w_up")
ROW_SHARDED = ("ev_w_out", "od_w_out", "ffn_w_down")


def _pack(parts):
    rows, offs = [], []
    at = 0
    for p in parts:
        flat = p.reshape(-1).astype(F32)
        n = -(-flat.shape[0] // LANE)
        rows.append(jnp.pad(flat, (0, n * LANE - flat.shape[0])).reshape(n, LANE))
        offs.append((at, n))
        at += n
    pad = -at % 16
    if pad:
        rows.append(jnp.zeros((pad, LANE), F32))
    return jnp.concatenate(rows, axis=0), offs


def _unpack(buf, off, shape):
    at, n = off
    lead = buf.shape[:-2]
    flat = buf[..., at:at + n, :].reshape(lead + (n * LANE,))
    return flat[..., :math.prod(shape)].reshape(lead + tuple(shape))


def submit_weight_gather(store, key, shard, col_sharded, mc):
    R, C = shard.shape
    half = lax.dynamic_index_in_dim(shard.reshape(2, R // 2, C), mc, 0, keepdims=False)

    def landed(g):
        g = g.reshape(N_CHIPS, R, C)
        store[key] = g.transpose(1, 0, 2).reshape(R, N_CHIPS * C) if col_sharded else g.reshape(N_CHIPS * R, C)

    submit_ride(gather8_ride(half, landed))


def submit_grad_reduce(store, key, pieces, col_sharded):
    _, _, R, C = pieces.shape
    tag = f"{key[0]}{key[1]}"

    def paired(out):
        store[key] = out.reshape(2 * R, C) if col_sharded else out.transpose(1, 0, 2).reshape(R, 2 * C)

    def scattered(got):
        submit_ride(sibling_pair_ride(sum_slots(got, f"gsum_{tag}"), paired))

    def swapped(keep, got):
        chip_sum = add_to_bf16(keep.reshape(N_CHIPS * R, C), got.reshape(N_CHIPS * R, C), f"gadd_{tag}")
        submit_ride(chip_scatter_ride(chip_sum.reshape(N_CHIPS, R, C), scattered))

    submit_ride(sibling_halves_ride(pieces, swapped))


def kernel(x, c, ada_w, ada_b, norm_mix, norm_ffn, ev_w_in, ev_w_out, gdn_conv_w, gdn_a_log, gdn_dt_bias, gdn_norm, pool_w, pool_scale, od_w_in, od_w_out, att_q_norm, att_k_norm, ffn_w_up, ffn_conv_w, ffn_conv_b, ffn_w_down, loss_target, m_ada_w, m_ada_b, m_norm_mix, m_norm_ffn, m_ev_w_in, m_ev_w_out, m_gdn_conv_w, m_gdn_a_log, m_gdn_dt_bias, m_gdn_norm, m_pool_w, m_pool_scale, m_od_w_in, m_od_w_out, m_att_q_norm, m_att_k_norm, m_ffn_w_up, m_ffn_conv_w, m_ffn_conv_b, m_ffn_w_down, v_ada_w, v_ada_b, v_norm_mix, v_norm_ffn, v_ev_w_in, v_ev_w_out, v_gdn_conv_w, v_gdn_a_log, v_gdn_dt_bias, v_gdn_norm, v_pool_w, v_pool_scale, v_od_w_in, v_od_w_out, v_att_q_norm, v_att_k_norm, v_ffn_w_up, v_ffn_conv_w, v_ffn_conv_b, v_ffn_w_down):
    local = dict(ada_w=ada_w, ada_b=ada_b, norm_mix=norm_mix, norm_ffn=norm_ffn, ev_w_in=ev_w_in,
                 ev_w_out=ev_w_out, gdn_conv_w=gdn_conv_w, gdn_a_log=gdn_a_log, gdn_dt_bias=gdn_dt_bias,
                 gdn_norm=gdn_norm, pool_w=pool_w, pool_scale=pool_scale, od_w_in=od_w_in, od_w_out=od_w_out,
                 att_q_norm=att_q_norm, att_k_norm=att_k_norm, ffn_w_up=ffn_w_up, ffn_conv_w=ffn_conv_w,
                 ffn_conv_b=ffn_conv_b, ffn_w_down=ffn_w_down)
    moments_m = dict(zip(WEIGHTS, (m_ada_w, m_ada_b, m_norm_mix, m_norm_ffn, m_ev_w_in, m_ev_w_out,
                                   m_gdn_conv_w, m_gdn_a_log, m_gdn_dt_bias, m_gdn_norm, m_pool_w, m_pool_scale,
                                   m_od_w_in, m_od_w_out, m_att_q_norm, m_att_k_norm, m_ffn_w_up, m_ffn_conv_w,
                                   m_ffn_conv_b, m_ffn_w_down)))
    moments_v = dict(zip(WEIGHTS, (v_ada_w, v_ada_b, v_norm_mix, v_norm_ffn, v_ev_w_in, v_ev_w_out,
                                   v_gdn_conv_w, v_gdn_a_log, v_gdn_dt_bias, v_gdn_norm, v_pool_w, v_pool_scale,
                                   v_od_w_in, v_od_w_out, v_att_q_norm, v_att_k_norm, v_ffn_w_up, v_ffn_conv_w,
                                   v_ffn_conv_b, v_ffn_w_down)))
    _RIDES.clear()
    _IDS[0] = 0
    mx, my, mc = _place()
    chip = 2 * mx + my
    T, D = x.shape[1], x.shape[2]
    depth = ada_w.shape[0]
    ada_cols = ada_w.shape[2]

    buf, offs = _pack([c, gdn_conv_w, ffn_conv_w])
    gathered = all_gather8(buf)
    c_all = _unpack(gathered, offs[0], (D,))
    by_chip = gathered[0::2]
    gdn_conv_full = jnp.concatenate(list(_unpack(by_chip, offs[1], gdn_conv_w.shape)), axis=-1)
    ffn_conv_full = jnp.concatenate(list(_unpack(by_chip, offs[2], ffn_conv_w.shape)), axis=-1)

    bias = lax.dynamic_slice_in_dim(ada_b, chip * ada_cols, ada_cols, axis=1)[:, None, :]
    mod_part = ada_fwd(c_all, ada_w, bias, "ada_fwd")
    mod_all = all_gather8(mod_part)[0::2]
    mod_all = mod_all.transpose(1, 2, 0, 3).reshape(depth, 8, N_CHIPS * ada_cols)
    mod = lax.dynamic_index_in_dim(mod_all, 4 * mx + 2 * my + mc, 1, keepdims=False).reshape(depth, 6, D)

    full_w, big_grad = {}, {}
    order = []
    for i in range(depth):
        mixer = ("ev_w_in", "ev_w_out") if i % 2 == 0 else ("od_w_in", "od_w_out")
        order += [(name, i // 2) for name in mixer] + [("ffn_w_up", i), ("ffn_w_down", i)]
    shards = {name: cast_bf16(local[name].reshape(-1, local[name].shape[-1]), f"cast_{name}")
              .reshape(local[name].shape) for name in COL_SHARDED + ROW_SHARDED}
    for name, e in order:
        submit_weight_gather(full_w, (name, e), shards[name][e], name in COL_SHARDED, mc)

    def big(name, e):
        flush_rides(until=lambda: (name, e) in full_w)
        return full_w[(name, e)]

    def on_grad(name, e, pieces):
        submit_grad_reduce(big_grad, (name, e), pieces, name in COL_SHARDED)

    W = dict(local)
    W["gdn_conv_w"], W["ffn_conv_w"] = gdn_conv_full, ffn_conv_full
    sq, dx, dmod, grads = local_step(x[0], loss_target[0], mod, W, big, on_grad)
    loss = lax.psum(0.5 * jnp.sum(sq) / D, ("x", "y", "c"))

    small = ["norm_mix", "norm_ffn", "gdn_conv_w", "gdn_a_log", "gdn_dt_bias", "gdn_norm", "pool_w",
             "pool_scale", "att_q_norm", "att_k_norm", "ffn_conv_w", "ffn_conv_b"]
    full = {name: jnp.stack([g[name] for g in grads if name in g]) for name in small}
    grad = {}
    buf, offs = _pack([dmod] + [full[name] for name in small])
    gathered = all_gather8(buf)
    summed = sum_slots(gathered, "sum_small_grads")
    grad["ada_b"] = _unpack(summed, offs[0], ada_b.shape)
    for k, name in enumerate(small):
        grad[name] = _unpack(summed, offs[1 + k], full[name].shape)
    for name, cols in (("gdn_conv_w", gdn_conv_w.shape[-1]), ("ffn_conv_w", ffn_conv_w.shape[-1])):
        grad[name] = lax.dynamic_slice_in_dim(grad[name], chip * cols, cols, axis=2)

    dmod_all = _unpack(gathered, offs[0], (depth, N_CHIPS * ada_cols))
    dmod_mine = lax.dynamic_slice_in_dim(dmod_all, chip * ada_cols, ada_cols, axis=2).transpose(1, 0, 2)
    grad["ada_w"] = ada_bwd(jnp.pad(c_all, ((0, 8), (0, 0))), jnp.pad(dmod_mine, ((0, 0), (0, 8), (0, 0))),
                            "ada_bwd")

    deltas, new_m, new_v = {}, {}, {}
    large = ("ffn_w_down", "ffn_w_up", "od_w_out", "od_w_in", "ev_w_out", "ev_w_in")
    for name in [n for n in WEIGHTS if n not in large] + list(large):
        if name in large:
            keys = [k for k in order if k[0] == name]
            flush_rides(until=lambda: all(k in big_grad for k in keys))
            grad[name] = jnp.stack([big_grad[k] for k in keys])
        deltas[name], new_m[name], new_v[name] = adamw(local[name], grad[name], moments_m[name],
                                                       moments_v[name], f"adamw_{name}")
    flush_rides()
    return (loss, dx[None], *[grad[n] for n in WEIGHTS], *[deltas[n] for n in WEIGHTS],
            *[new_m[n] for n in WEIGHTS], *[new_v[n] for n in WEIGHTS])
```
